```python
import math
import jax
import jax.numpy as jnp
from jax import lax
import numpy as np

D_MODEL = 1024
BATCH = 2
SEQ = 8192
DEPTH = 4

MIX_W = 256
N_BRANCH = 4
GM_CHUNK = 128
GM_GROUPS = 4
GM_GD = MIX_W // GM_GROUPS
ATT_HEADS = 4
ATT_HD = MIX_W // ATT_HEADS
DIL_PATTERNS = ((128, 1), (512, 4), (2048, 16))
ATT_BLOCK = 64
REL_BUCKETS = 32
REL_MAX_DIST = 1024
ML_HEADS = 4
ML_HD = MIX_W // ML_HEADS
ML_CHUNK = 64
ML_CONV = 3
POOL_WINDOWS = (2, 4, 8, 16)
POOL_GD = MIX_W // len(POOL_WINDOWS)
N_EXPERTS = 16
EXPERT_FF = 1024
EC_FACTOR = 2
DEEPNORM_ALPHA = (2 * DEPTH) ** 0.25
DEEPNORM_BETA = (8 * DEPTH) ** -0.25
LN_EPS = 1e-5
IN_SIZES = (MIX_W, MIX_W,
            3 * MIX_W,
            4 * MIX_W,
            2 * ML_HEADS, 2 * ML_HEADS,
            MIX_W,
            N_BRANCH * D_MODEL)
N_IN = 10 * MIX_W + 4 * ML_HEADS + N_BRANCH * D_MODEL

kernel_name = 'hybrid_gated_mixers_ec_moe_encoder'


def _standardize(x):
    xf = x.astype(jnp.float32)
    mu = jnp.mean(xf, axis=-1, keepdims=True)
    var = jnp.mean(jnp.square(xf - mu), axis=-1, keepdims=True)
    return (xf - mu) * lax.rsqrt(var + LN_EPS)


def layer_norm(x, g, b):
    return (_standardize(x) * g + b).astype(x.dtype)


def gmlp_spatial_gate(u, v, ln_g, ws, bs):
    B, S, _ = u.shape
    vn = (_standardize(v) * ln_g).astype(v.dtype)
    vc = vn.reshape(B, S // GM_CHUNK, GM_CHUNK, GM_GROUPS, GM_GD)
    mixed = jnp.einsum('gpq,bcqgd->bcpgd', ws, vc) + jnp.transpose(bs)[None, None, :, :, None]
    return u * mixed.reshape(B, S, MIX_W)


def t5_bucket(rel):
    half = REL_BUCKETS // 2
    max_exact = half // 2
    ret = jnp.where(rel > 0, half, 0)
    n = jnp.abs(rel)
    nf = jnp.maximum(n, 1).astype(jnp.float32)
    large = max_exact + (jnp.log(nf / max_exact) / math.log(REL_MAX_DIST / max_exact)
                         * (half - max_exact)).astype(jnp.int32)
    large = jnp.minimum(large, half - 1)
    return ret + jnp.where(n < max_exact, n, large)


def dilated_window_attention(q, k, v, rel_bias, window, dil):
    B, S, H, Dh = q.shape
    side = (window // 2) // dil
    L = S // dil
    nb = -(-L // ATT_BLOCK)
    Lp = nb * ATT_BLOCK

    def to_blocks(t):
        t = t.reshape(B, L, dil, H, Dh).transpose(0, 2, 3, 1, 4)
        t = jnp.pad(t, ((0, 0), (0, 0), (0, 0), (0, Lp - L), (0, 0)))
        return t.reshape(B, dil, H, nb, ATT_BLOCK, Dh)

    def band(t):
        tp = jnp.pad(t, ((0, 0), (0, 0), (0, 0), (1, 1), (0, 0), (0, 0)))
        return jnp.concatenate([tp[:, :, :, :-2], tp[:, :, :, 1:-1], tp[:, :, :, 2:]], axis=4)

    qb = to_blocks(q)
    kw = band(to_blocks(k))
    vw = band(to_blocks(v))
    rel_local = jnp.arange(3 * ATT_BLOCK)[None, :] - ATT_BLOCK - jnp.arange(ATT_BLOCK)[:, None]
    bias = jnp.transpose(rel_bias[t5_bucket(dil * rel_local)], (2, 0, 1))
    kpos = (jnp.arange(nb)[:, None] - 1) * ATT_BLOCK + jnp.arange(3 * ATT_BLOCK)[None, :]
    valid = (jnp.abs(rel_local) <= side)[None] & ((kpos >= 0) & (kpos < L))[:, None, :]
    logits = (jnp.einsum('brhnqd,brhnkd->brhnqk', qb, kw).astype(jnp.float32) * ATT_HD ** -0.5
              + bias[:, None].astype(jnp.float32))
    logits = jnp.where(valid, logits, -1e30)
    lse = jax.nn.logsumexp(logits, axis=-1)
    p = jnp.exp(logits - lse[..., None])
    o = jnp.einsum('brhnqk,brhnkd->brhnqd', p.astype(v.dtype), vw)
    o = o.reshape(B, dil, H, Lp, Dh)[:, :, :, :L].transpose(0, 3, 1, 2, 4).reshape(B, S, H, Dh)
    lse = lse.reshape(B, dil, H, Lp)[..., :L].transpose(0, 3, 1, 2).reshape(B, S, H)
    return o, lse


def dilated_mixture_attention(q, k, v, rel_bias):
    outs, lses = [], []
    for window, dil in DIL_PATTERNS:
        o, l = dilated_window_attention(q, k, v, rel_bias, window, dil)
        outs.append(o)
        lses.append(l)
    wts = jax.nn.softmax(jnp.stack(lses, axis=0), axis=0)
    return jnp.einsum('pbsh,pbshd->bshd', wts.astype(q.dtype), jnp.stack(outs, axis=0))


def depthwise_conv(x, w):
    K, C = w.shape
    pad = K // 2
    return lax.conv_general_dilated(x, w[:, None, :], window_strides=(1,),
                                    padding=[(pad, K - 1 - pad)],
                                    dimension_numbers=('NWC', 'WIO', 'NWC'),
                                    feature_group_count=C)


def mlstm_scan(q, k, v, li, lf):
    B, H, S, Dh = q.shape
    nc = S // ML_CHUNK

    def chunks(t):
        return jnp.moveaxis(t.reshape(B, H, nc, ML_CHUNK, *t.shape[3:]), 2, 0)

    tri = jnp.tril(jnp.ones((ML_CHUNK, ML_CHUNK), dtype=bool))

    def step(carry, inp):
        C, n, m = carry
        qc, kc, vc, lic, lfc = inp
        b = jnp.cumsum(lfc, axis=-1)
        D = jnp.where(tri, b[..., :, None] - b[..., None, :] + lic[..., None, :], -jnp.inf)
        m_inter = b + m[..., None]
        m_t = jnp.maximum(jnp.max(D, axis=-1), m_inter)
        inter_w = jnp.exp(m_inter - m_t)
        s = jnp.einsum('bhtd,bhsd->bhts', qc, kc) * jnp.exp(D - m_t[..., None])
        num = (jnp.einsum('bhts,bhsd->bhtd', s, vc)
               + inter_w[..., None] * jnp.einsum('bhvk,bhtk->bhtv', C, qc))
        den = jnp.sum(s, axis=-1) + inter_w * jnp.einsum('bhk,bhtk->bht', n, qc)
        h = num / jnp.maximum(jnp.abs(den), jnp.exp(-m_t))[..., None]
        g = b[..., -1]
        w_s = g[..., None] - b + lic
        m_new = jnp.maximum(g + m, jnp.max(w_s, axis=-1))
        decay = jnp.exp(g + m - m_new)
        ws = jnp.exp(w_s - m_new[..., None])
        C = decay[..., None, None] * C + jnp.einsum('bhs,bhsv,bhsk->bhvk', ws, vc, kc)
        n = decay[..., None] * n + jnp.einsum('bhs,bhsk->bhk', ws, kc)
        return (C, n, m_new), h

    init = (jnp.zeros((B, H, Dh, Dh), jnp.float32), jnp.zeros((B, H, Dh), jnp.float32),
            jnp.zeros((B, H), jnp.float32))
    _, hs = lax.scan(step, init, (chunks(q), chunks(k), chunks(v), chunks(li), chunks(lf)))
    return jnp.moveaxis(hs, 0, 2).reshape(B, H, S, Dh)


def bidirectional_mlstm(q, k, v, li, lf):
    flip = lambda t: jnp.flip(t, axis=2)
    fwd = mlstm_scan(q, k, v, li[0], lf[0])
    bwd = flip(mlstm_scan(flip(q), flip(k), flip(v), flip(li[1]), flip(lf[1])))
    return fwd + bwd


def head_norm(h, g):
    B, H, S, Dh = h.shape
    mu = jnp.mean(h, axis=-1, keepdims=True)
    var = jnp.mean(jnp.square(h - mu), axis=-1, keepdims=True)
    hn = (h - mu) * lax.rsqrt(var + LN_EPS)
    return hn.transpose(0, 2, 1, 3).reshape(B, S, H * Dh) * g


def pool_mixer(xd, pool_w, pool_scale):
    B, S, _ = xd.shape
    xg = xd.reshape(B, S, len(POOL_WINDOWS), POOL_GD).astype(jnp.float32)
    cs = jnp.pad(jnp.cumsum(xg, axis=1), ((0, 0), (1, 0), (0, 0), (0, 0)))
    pos = jnp.arange(S)
    outs = []
    for gi, win in enumerate(POOL_WINDOWS):
        lo = jnp.clip(pos - win // 2, 0, S)
        hi = jnp.clip(pos + win // 2, 0, S)
        cg = cs[:, :, gi]
        mean = (cg[:, hi] - cg[:, lo]) / (hi - lo).astype(jnp.float32)[None, :, None]
        outs.append(mean - xg[:, :, gi])
    pooled = jnp.stack(outs, axis=2).astype(xd.dtype)
    mixed = jnp.einsum('bsgi,gio->bsgo', pooled, pool_w)
    return mixed.reshape(B, S, MIX_W) * pool_scale


def mixer_sublayer(x, w_in, b_in, gm_ln_g, gm_ws, gm_bs, rel_bias, ml_conv, ml_fbias,
                   ml_norm_g, pool_w, pool_scale, w_branch, w_out):
    B, S, _ = x.shape
    h = jnp.einsum('bsd,dn->bsn', x, w_in) + b_in
    split_points = np.cumsum(IN_SIZES)[:-1].tolist()
    a_u, a_v, b_qkv, c_qkvo, c_ig, c_fg, d_x, gate_raw = jnp.split(h, split_points, axis=-1)

    y_a = gmlp_spatial_gate(jax.nn.gelu(a_u), jax.nn.gelu(a_v), gm_ln_g, gm_ws, gm_bs)

    qkv = b_qkv.reshape(B, S, 3, ATT_HEADS, ATT_HD)
    y_b = dilated_mixture_attention(qkv[:, :, 0], qkv[:, :, 1], qkv[:, :, 2], rel_bias)
    y_b = y_b.reshape(B, S, MIX_W)

    c_qk = jax.nn.silu(depthwise_conv(c_qkvo[..., :2 * MIX_W], ml_conv))
    heads = lambda t: t.reshape(B, S, ML_HEADS, ML_HD).transpose(0, 2, 1, 3).astype(jnp.float32)
    c_q = heads(c_qk[..., :MIX_W])
    c_k = heads(c_qk[..., MIX_W:]) * ML_HD ** -0.5
    c_v = heads(c_qkvo[..., 2 * MIX_W:3 * MIX_W])
    c_o = c_qkvo[..., 3 * MIX_W:]
    li = c_ig.reshape(B, S, 2, ML_HEADS).astype(jnp.float32).transpose(2, 0, 3, 1)
    lf = jax.nn.log_sigmoid(c_fg.reshape(B, S, 2, ML_HEADS).astype(jnp.float32)
                            + ml_fbias).transpose(2, 0, 3, 1)
    y_c = head_norm(bidirectional_mlstm(c_q, c_k, c_v, li, lf), ml_norm_g)
    y_c = (jax.nn.sigmoid(c_o.astype(jnp.float32)) * y_c).astype(x.dtype)

    y_d = pool_mixer(d_x, pool_w, pool_scale)

    ys = jnp.stack([y_a, y_b.astype(x.dtype), y_c, y_d.astype(x.dtype)], axis=2)
    proj = jnp.einsum('bsnc,ncd->bsnd', ys, w_branch)
    gates = jax.nn.sigmoid(gate_raw.reshape(B, S, N_BRANCH, D_MODEL))
    merged = jnp.einsum('bsnd,bsnd->bsd', gates, proj)
    return jnp.einsum('bsd,de->bse', merged, w_out)


def expert_choice_ffn(x, w_router, w1, w3, w2):
    B, T, D = x.shape
    cap = EC_FACTOR * T // N_EXPERTS
    aff = jax.nn.softmax(jnp.einsum('btd,de->bte', x, w_router).astype(jnp.float32), axis=-1)
    gate, idx = lax.top_k(jnp.swapaxes(aff, 1, 2), cap)
    xs = jax.vmap(lambda xb, ib: xb[ib])(x, idx)
    hid = jax.nn.silu(jnp.einsum('becd,edf->becf', xs, w1)) * jnp.einsum('becd,edf->becf', xs, w3)
    ye = jnp.einsum('becf,efd->becd', hid, w2) * gate[..., None].astype(x.dtype)
    scatter = lambda yb, ib: jnp.zeros((T, D), yb.dtype).at[ib.reshape(-1)].add(yb.reshape(-1, D))
    return jax.vmap(scatter)(ye, idx)


def setup_inputs(seed: int = 0) -> dict:
    key = jax.random.key(seed)
    ks = jax.random.split(key, 24)
    nrm = lambda k, shape, scale: jax.random.normal(k, shape, jnp.float32) * scale
    L = DEPTH
    return {
        'x': nrm(ks[0], (BATCH, SEQ, D_MODEL), 1.0),
        'w_in': nrm(ks[1], (L, D_MODEL, N_IN), D_MODEL ** -0.5),
        'b_in': nrm(ks[2], (L, N_IN), 0.02),
        'gm_ln_g': 1.0 + nrm(ks[3], (L, MIX_W), 0.02),
        'gm_ws': nrm(ks[4], (L, GM_GROUPS, GM_CHUNK, GM_CHUNK), GM_CHUNK ** -0.5),
        'gm_bs': 1.0 + nrm(ks[5], (L, GM_GROUPS, GM_CHUNK), 0.02),
        'rel_bias': nrm(ks[6], (REL_BUCKETS, ATT_HEADS), 0.3),
        'ml_conv': nrm(ks[7], (L, ML_CONV, 2 * MIX_W), ML_CONV ** -0.5),
        'ml_fbias': jnp.linspace(3.0, 6.0, ML_HEADS, dtype=jnp.float32) + nrm(ks[8], (L, 2, ML_HEADS), 0.1),
        'ml_norm_g': 1.0 + nrm(ks[9], (L, MIX_W), 0.02),
        'pool_w': nrm(ks[10], (L, len(POOL_WINDOWS), POOL_GD, POOL_GD), POOL_GD ** -0.5),
        'pool_scale': 1.0 + nrm(ks[11], (L, MIX_W), 0.02),
        'w_branch': nrm(ks[12], (L, N_BRANCH, MIX_W, D_MODEL), MIX_W ** -0.5),
        'w_out': nrm(ks[13], (L, D_MODEL, D_MODEL), D_MODEL ** -0.5 * DEEPNORM_BETA),
        'ln1_g': 1.0 + nrm(ks[14], (L, D_MODEL), 0.02),
        'ln1_b': nrm(ks[15], (L, D_MODEL), 0.02),
        'w_router': nrm(ks[16], (L, D_MODEL, N_EXPERTS), D_MODEL ** -0.5),
        'w_e1': nrm(ks[17], (L, N_EXPERTS, D_MODEL, EXPERT_FF), D_MODEL ** -0.5),
        'w_e3': nrm(ks[18], (L, N_EXPERTS, D_MODEL, EXPERT_FF), D_MODEL ** -0.5),
        'w_e2': nrm(ks[19], (L, N_EXPERTS, EXPERT_FF, D_MODEL), EXPERT_FF ** -0.5 * DEEPNORM_BETA),
        'ln2_g': 1.0 + nrm(ks[20], (L, D_MODEL), 0.02),
        'ln2_b': nrm(ks[21], (L, D_MODEL), 0.02),
    }


def reference(x, w_in, b_in, gm_ln_g, gm_ws, gm_bs, rel_bias, ml_conv, ml_fbias, ml_norm_g,
              pool_w, pool_scale, w_branch, w_out, ln1_g, ln1_b, w_router, w_e1, w_e3, w_e2,
              ln2_g, ln2_b):
    for l in range(DEPTH):
        mix = mixer_sublayer(x, w_in[l], b_in[l], gm_ln_g[l], gm_ws[l], gm_bs[l], rel_bias,
                             ml_conv[l], ml_fbias[l], ml_norm_g[l], pool_w[l], pool_scale[l],
                             w_branch[l], w_out[l])
        x = layer_norm(DEEPNORM_ALPHA * x + mix, ln1_g[l], ln1_b[l])
        ffn = expert_choice_ffn(x, w_router[l], w_e1[l], w_e3[l], w_e2[l])
        x = layer_norm(DEEPNORM_ALPHA * x + ffn, ln2_g[l], ln2_b[l])
    return x
```

```python
import functools
import math

import jax
import jax.numpy as jnp
import numpy as np
from jax import lax
from jax.experimental import pallas as pl
from jax.experimental.pallas import tpu as pltpu

F32 = jnp.float32
BF16 = jnp.bfloat16

D_MODEL = 1024
MIX_W = 256
N_BRANCH = 4
GM_CHUNK = 128
GM_GROUPS = 4
ATT_HEADS = 4
ATT_HD = 64
DIL_PATTERNS = ((128, 1), (512, 4), (2048, 16))
ATT_BLOCK = 64
REL_BUCKETS = 32
REL_MAX_DIST = 1024
ML_HEADS = 4
ML_HD = 64
ML_CHUNK = 64
POOL_WINDOWS = (2, 4, 8, 16)
N_EXPERTS = 16
EXPERT_FF = 1024
EC_FACTOR = 2
LN_EPS = 1e-5
NEG_BIG = -1e30

V7X_VMEM_LIMIT = 56 * 1024 * 1024
LANES = 128
HALO = 8


def _cparams(n_grid, vmem=V7X_VMEM_LIMIT):
    return pltpu.CompilerParams(dimension_semantics=("arbitrary",) * n_grid,
                                vmem_limit_bytes=vmem)


def _standardize(xf):
    mu = jnp.mean(xf, axis=-1, keepdims=True)
    var = jnp.mean(jnp.square(xf - mu), axis=-1, keepdims=True)
    return (xf - mu) * lax.rsqrt(var + LN_EPS)


TA = 512
A_COLS = 2560 + 2 * LANES


def _inproj_kernel(x_ref, w_ref, b_ref, lng_ref, wscat_ref, bsfull_ref,
                   ya_ref, qkv_ref, cqk_ref, cv_ref, co_ref, dx_ref, ig_ref, fg_ref):
    xb = x_ref[...].astype(BF16)
    h = jnp.dot(xb, w_ref[...], preferred_element_type=F32) + b_ref[...]
    qkv_ref[...] = h[:, 512:1280].astype(BF16)
    cqk_ref[...] = h[:, 1280:1792]
    cv_ref[...] = h[:, 1792:2048].astype(BF16)
    co_ref[...] = h[:, 2048:2304]
    dx_ref[...] = h[:, 2304:2560]
    ig_ref[...] = h[:, 2560:2688]
    fg_ref[...] = h[:, 2688:2816]
    u = jax.nn.gelu(h[:, 0:256])
    v = jax.nn.gelu(h[:, 256:512])
    vn = _standardize(v) * lng_ref[...]
    lane_grp = lax.broadcasted_iota(jnp.int32, (GM_CHUNK, MIX_W), 1) // (MIX_W // GM_GROUPS)
    for c in range(TA // GM_CHUNK):
        vc = vn[c * GM_CHUNK:(c + 1) * GM_CHUNK]
        stacked = jnp.concatenate(
            [jnp.where(lane_grp == g, vc, 0.0).astype(BF16) for g in range(GM_GROUPS)], axis=0)
        mixed = jnp.dot(wscat_ref[...], stacked, preferred_element_type=F32) + bsfull_ref[...]
        ya_ref[c * GM_CHUNK:(c + 1) * GM_CHUNK, :] = (
            u[c * GM_CHUNK:(c + 1) * GM_CHUNK] * mixed).astype(BF16)


def _inproj_call(x2d, w_cat, b_cat, lng, wscat, bsfull):
    n = x2d.shape[0]
    tok = lambda w: pl.BlockSpec((TA, w), lambda i: (i, 0))
    const = lambda s: pl.BlockSpec(s, lambda i: (0,) * len(s))
    out_shape = (
        jax.ShapeDtypeStruct((n, 256), BF16),
        jax.ShapeDtypeStruct((n, 768), BF16),
        jax.ShapeDtypeStruct((n, 512), F32),
        jax.ShapeDtypeStruct((n, 256), BF16),
        jax.ShapeDtypeStruct((n, 256), F32),
        jax.ShapeDtypeStruct((n, 256), F32),
        jax.ShapeDtypeStruct((n, LANES), F32),
        jax.ShapeDtypeStruct((n, LANES), F32),
    )
    return pl.pallas_call(
        _inproj_kernel, name="inproj_gmlp",
        grid=(n // TA,),
        in_specs=[tok(D_MODEL), const((D_MODEL, A_COLS)), const((1, A_COLS)), const((1, MIX_W)),
                  const((GM_CHUNK, GM_GROUPS * GM_CHUNK)), const((GM_CHUNK, MIX_W))],
        out_specs=(tok(256), tok(768), tok(512), tok(256), tok(256), tok(256), tok(LANES), tok(LANES)),
        out_shape=out_shape,
        compiler_params=_cparams(1),
    )(x2d, w_cat, b_cat, lng, wscat, bsfull)


def _pack_inproj_weights(w_in, b_in):
    pad = lambda a: jnp.pad(a, ((0, 0), (0, LANES - 8)))
    w_cat = jnp.concatenate([w_in[:, 0:2304], w_in[:, 2320:2576],
                             pad(w_in[:, 2304:2312]), pad(w_in[:, 2312:2320])], axis=1)
    b2 = b_in[None, :]
    b_cat = jnp.concatenate([b2[:, 0:2304], b2[:, 2320:2576],
                             pad(b2[:, 2304:2312]), pad(b2[:, 2312:2320])], axis=1)
    return w_cat.astype(BF16), b_cat


def _pack_gmlp(gm_ws, gm_bs):
    wscat = jnp.transpose(gm_ws, (1, 0, 2)).reshape(GM_CHUNK, GM_GROUPS * GM_CHUNK).astype(BF16)
    bsfull = jnp.repeat(jnp.transpose(gm_bs), MIX_W // GM_GROUPS, axis=1)
    return wscat, bsfull


def _halo_specs(t, width, n_tiles):
    r = t // HALO
    main = pl.BlockSpec((1, t, width), lambda b, i: (b, i, 0))
    prev = pl.BlockSpec((1, HALO, width), lambda b, i: (b, jnp.maximum(i * r - 1, 0), 0))
    nxt = pl.BlockSpec((1, HALO, width), lambda b, i: (b, jnp.minimum((i + 1) * r, n_tiles * r - 1), 0))
    return main, prev, nxt


def _fill_halo_scratch(buf, x_ref, p_ref, n_ref, t):
    i = pl.program_id(1)
    last = pl.num_programs(1) - 1
    buf[0:HALO, :] = jnp.where(i > 0, p_ref[0], 0.0)
    buf[HALO:HALO + t, :] = x_ref[0]
    buf[HALO + t:2 * HALO + t, :] = jnp.where(i < last, n_ref[0], 0.0)


TP = 512


def _pool_kernel(x_ref, p_ref, n_ref, w_ref, sc_ref, o_ref, buf):
    _fill_halo_scratch(buf, x_ref, p_ref, n_ref, TP)
    seq = pl.num_programs(1) * TP
    pos = pl.program_id(1) * TP + lax.broadcasted_iota(jnp.int32, (TP, 1), 0)
    lane_grp = lax.broadcasted_iota(jnp.int32, (TP, MIX_W), 1) // (MIX_W // len(POOL_WINDOWS))
    x0 = buf[HALO:HALO + TP, :]
    pooled = jnp.zeros((TP, MIX_W), F32)
    acc = None
    half_done = 0
    for gi, win in enumerate(POOL_WINDOWS):
        half = win // 2
        for o in list(range(-half, -half_done)) + list(range(half_done, half)):
            term = buf[HALO + o:HALO + o + TP, :]
            acc = term if acc is None else acc + term
        half_done = half
        cnt = (jnp.minimum(pos + half, seq) - jnp.maximum(pos - half, 0)).astype(F32)
        pooled = jnp.where(lane_grp == gi, acc / cnt - x0, pooled)
    mixed = jnp.dot(pooled.astype(BF16), w_ref[...], preferred_element_type=F32)
    o_ref[0] = (mixed * sc_ref[...]).astype(BF16)


def _pool_call(dx, w_block, scale):
    b, s, _ = dx.shape
    nt = s // TP
    main, prev, nxt = _halo_specs(TP, MIX_W, nt)
    return pl.pallas_call(
        _pool_kernel, name="pool_mixer",
        grid=(b, nt),
        in_specs=[main, prev, nxt,
                  pl.BlockSpec((MIX_W, MIX_W), lambda b, i: (0, 0)),
                  pl.BlockSpec((1, MIX_W), lambda b, i: (0, 0))],
        out_specs=pl.BlockSpec((1, TP, MIX_W), lambda b, i: (b, i, 0)),
        out_shape=jax.ShapeDtypeStruct((b, s, MIX_W), BF16),
        scratch_shapes=[pltpu.VMEM((TP + 2 * HALO, MIX_W), F32)],
        compiler_params=_cparams(2),
    )(dx, dx, dx, w_block, scale)


TQ = 128
TKEYS = TQ + 2 * ATT_BLOCK


def _attn_kernel(q_ref, kp_ref, km_ref, kn_ref, vp_ref, vm_ref, vn_ref, bias_ref, o_ref, lse_ref):
    i = pl.program_id(1)
    seq = pl.num_programs(1) * TQ
    kpos = i * TQ - ATT_BLOCK + lax.broadcasted_iota(jnp.int32, (1, TKEYS), 1)
    kvalid = (kpos >= 0) & (kpos < seq)
    q = q_ref[0]
    k = jnp.concatenate([kp_ref[0], km_ref[0], kn_ref[0]], axis=0)
    v = jnp.concatenate([vp_ref[0], vm_ref[0], vn_ref[0]], axis=0)
    for h in range(ATT_HEADS):
        sl = slice(h * ATT_HD, (h + 1) * ATT_HD)
        logits = lax.dot_general(q[:, sl], k[:, sl], (((1,), (1,)), ((), ())),
                                 preferred_element_type=F32) * ATT_HD ** -0.5 + bias_ref[h]
        logits = jnp.where(kvalid, logits, NEG_BIG)
        m = jnp.max(logits, axis=-1, keepdims=True)
        p = jnp.exp(logits - m)
        ssum = jnp.sum(p, axis=-1, keepdims=True)
        o = jnp.dot(p.astype(BF16), v[:, sl], preferred_element_type=F32) / ssum
        o_ref[0, :, sl] = o
        lse_ref[0, :, sl] = jnp.broadcast_to(m + jnp.log(ssum), (TQ, ATT_HD))


def _attn_call(qkv, bias):
    n, l, _ = qkv.shape
    nt = l // TQ
    r = TQ // ATT_BLOCK
    main = lambda c: pl.BlockSpec((1, TQ, MIX_W), lambda b, i: (b, i, c))
    prev = lambda c: pl.BlockSpec((1, ATT_BLOCK, MIX_W), lambda b, i: (b, jnp.maximum(i * r - 1, 0), c))
    nxt = lambda c: pl.BlockSpec((1, ATT_BLOCK, MIX_W),
                                 lambda b, i: (b, jnp.minimum((i + 1) * r, nt * r - 1), c))
    out = pl.BlockSpec((1, TQ, MIX_W), lambda b, i: (b, i, 0))
    return pl.pallas_call(
        _attn_kernel, name="band_attention",
        grid=(n, nt),
        in_specs=[main(0), prev(1), main(1), nxt(1), prev(2), main(2), nxt(2),
                  pl.BlockSpec((ATT_HEADS, TQ, TKEYS), lambda b, i: (0, 0, 0))],
        out_specs=(out, out),
        out_shape=(jax.ShapeDtypeStruct((n, l, MIX_W), F32), jax.ShapeDtypeStruct((n, l, MIX_W), F32)),
        compiler_params=_cparams(2),
    )(qkv, qkv, qkv, qkv, qkv, qkv, qkv, bias)


def _t5_bucket(rel):
    half = REL_BUCKETS // 2
    max_exact = half // 2
    ret = jnp.where(rel > 0, half, 0)
    n = jnp.abs(rel)
    nf = jnp.maximum(n, 1).astype(F32)
    large = max_exact + (jnp.log(nf / max_exact) / math.log(REL_MAX_DIST / max_exact)
                         * (half - max_exact)).astype(jnp.int32)
    large = jnp.minimum(large, half - 1)
    return ret + jnp.where(n < max_exact, n, large)


def _attn_bias_tile(rel_bias, window, dil):
    side = (window // 2) // dil
    rel = jnp.arange(TKEYS)[None, :] - ATT_BLOCK - jnp.arange(TQ)[:, None]
    bias = jnp.transpose(rel_bias[_t5_bucket(dil * rel)], (2, 0, 1)).astype(F32)
    return jnp.where((jnp.abs(rel) <= side)[None], bias, NEG_BIG)


def _to_residues(t, dil):
    b, s, c = t.shape
    return t.reshape(b, s // dil, dil, c).transpose(0, 2, 1, 3).reshape(b * dil, s // dil, c)


def _from_residues(t, dil, b):
    n, l, c = t.shape
    return t.reshape(b, dil, l, c).transpose(0, 2, 1, 3).reshape(b, l * dil, c)


TM = 512


def _mlprep_kernel(x_ref, p_ref, n_ref, v_ref, w_ref, q_out, kt_out, v_out, buf):
    _fill_halo_scratch(buf, x_ref, p_ref, n_ref, TM)
    conv = (buf[HALO - 1:HALO - 1 + TM, :] * w_ref[0:1, :] + buf[HALO:HALO + TM, :] * w_ref[1:2, :]
            + buf[HALO + 1:HALO + 1 + TM, :] * w_ref[2:3, :])
    qk = jax.nn.silu(conv)
    kt = jnp.transpose(qk[:, MIX_W:] * ML_HD ** -0.5)
    v = v_ref[0]
    ones_col = jnp.where(lax.broadcasted_iota(jnp.int32, (TM, ML_HD), 1) == 0, 1.0, 0.0).astype(BF16)
    for h in range(ML_HEADS):
        sl = slice(h * ML_HD, (h + 1) * ML_HD)
        q_out[0, h] = qk[:, sl].astype(BF16)
        kt_out[0, h] = kt[sl, :].astype(BF16)
        v_out[0, h] = jnp.concatenate([v[:, sl], ones_col], axis=1)


def _mlprep_call(cqk, cv, conv_w):
    b, s, _ = cqk.shape
    nt = s // TM
    main, prev, nxt = _halo_specs(TM, 2 * MIX_W, nt)
    return pl.pallas_call(
        _mlprep_kernel, name="mlstm_prep",
        grid=(b, nt),
        in_specs=[main, prev, nxt,
                  pl.BlockSpec((1, TM, MIX_W), lambda b, i: (b, i, 0)),
                  pl.BlockSpec((3, 2 * MIX_W), lambda b, i: (0, 0))],
        out_specs=(pl.BlockSpec((1, ML_HEADS, TM, ML_HD), lambda b, i: (b, 0, i, 0)),
                   pl.BlockSpec((1, ML_HEADS, ML_HD, TM), lambda b, i: (b, 0, 0, i)),
                   pl.BlockSpec((1, ML_HEADS, TM, 2 * ML_HD), lambda b, i: (b, 0, i, 0))),
        out_shape=(jax.ShapeDtypeStruct((b, ML_HEADS, s, ML_HD), BF16),
                   jax.ShapeDtypeStruct((b, ML_HEADS, ML_HD, s), BF16),
                   jax.ShapeDtypeStruct((b, ML_HEADS, s, 2 * ML_HD), BF16)),
        scratch_shapes=[pltpu.VMEM((TM + 2 * HALO, 2 * MIX_W), F32)],
        compiler_params=_cparams(2),
    )(cqk, cqk, cqk, cv, conv_w)


GP_BLK = 512


def _chunk_scan(x, fwd_lane, t_in_chunk, op, ident):
    n = x.shape[0]
    shift = 1
    while shift < ML_CHUNK:
        down = pltpu.roll(x, shift, 0)
        up = pltpu.roll(x, n - shift, 0)
        nb = jnp.where(fwd_lane,
                       jnp.where(t_in_chunk >= shift, down, ident),
                       jnp.where(t_in_chunk < ML_CHUNK - shift, up, ident))
        x = op(x, nb)
        shift *= 2
    return x


def _gatescan_kernel(ig_ref, fg_ref, fb_ref, b_ref, a_ref, cm_ref, g_ref, amax_ref):
    lane = lax.broadcasted_iota(jnp.int32, (1, LANES), 1)
    fwd_lane = lane < ML_HEADS
    t_in_chunk = lax.broadcasted_iota(jnp.int32, (GP_BLK, 1), 0) % ML_CHUNK
    z = fg_ref[0] + fb_ref[...]
    lf = jnp.minimum(z, 0.0) - jnp.log1p(jnp.exp(-jnp.abs(z)))
    b = _chunk_scan(lf, fwd_lane, t_in_chunk, jnp.add, 0.0)
    a = ig_ref[0] - b
    b_ref[0] = b
    a_ref[0] = a
    cm_ref[0] = _chunk_scan(a, fwd_lane, t_in_chunk, jnp.maximum, -jnp.inf)
    cpb = GP_BLK // ML_CHUNK
    last = pl.ds(ML_CHUNK - 1, cpb, stride=ML_CHUNK)
    first = pl.ds(0, cpb, stride=ML_CHUNK)
    g_ref[0] = jnp.where(fwd_lane, b_ref[0, last, :], b_ref[0, first, :])
    amax_ref[0] = jnp.where(fwd_lane, cm_ref[0, last, :], cm_ref[0, first, :])


def _gateout_kernel(b_ref, a_ref, cm_ref, g_ref, amax_ref, ws_ref, m_ref, iw_ref, en_ref, dec_ref,
                    mch_s, mlast_s):
    nc = g_ref.shape[1]
    j = pl.program_id(1)
    fwd_lane = lax.broadcasted_iota(jnp.int32, (1, LANES), 1) < ML_HEADS

    @pl.when(j == 0)
    def _():
        def m_step(i, carry):
            mf, mb = carry
            cf = pl.ds(i, 1)
            cb = pl.ds(nc - 1 - i, 1)
            mch_s[cf, :] = jnp.where(fwd_lane, mf, mch_s[cf, :])
            mch_s[cb, :] = jnp.where(fwd_lane, mch_s[cb, :], mb)
            mf = g_ref[0, cf, :] + jnp.maximum(mf, amax_ref[0, cf, :])
            mb = g_ref[0, cb, :] + jnp.maximum(mb, amax_ref[0, cb, :])
            return mf, mb

        mch_s[...] = jnp.zeros((nc, LANES), F32)
        zero = jnp.zeros((1, LANES), F32)
        lax.fori_loop(0, nc, m_step, (zero, zero))
        mlast = jnp.maximum(amax_ref[0], mch_s[...])
        mlast_s[...] = mlast
        dec_ref[0] = jnp.exp(mch_s[...] - mlast)

    cpb = GP_BLK // ML_CHUNK
    crow = pl.ds(pl.multiple_of(j * cpb, cpb), cpb)
    expand = lambda t: jnp.broadcast_to(t[:, None, :], (cpb, ML_CHUNK, LANES)).reshape(GP_BLK, LANES)
    m_tok = expand(mch_s[crow, :])
    mlast_tok = expand(mlast_s[crow, :])
    mt = jnp.maximum(cm_ref[0], m_tok)
    m_ref[0] = mt
    iw_ref[0] = jnp.exp(m_tok - mt)
    en_ref[0] = jnp.exp(-(b_ref[0] + mt))
    ws_ref[0] = jnp.exp(a_ref[0] - mlast_tok)


def _gateprep_call(ig, fg, fbias_row):
    b, s, _ = ig.shape
    nc = s // ML_CHUNK
    cpb = GP_BLK // ML_CHUNK
    tok = pl.BlockSpec((1, GP_BLK, LANES), lambda i, j: (i, j, 0))
    tok_shape = jax.ShapeDtypeStruct((b, s, LANES), F32)
    chunk_shape = jax.ShapeDtypeStruct((b, nc, LANES), F32)
    chunk_tile = pl.BlockSpec((1, cpb, LANES), lambda i, j: (i, j, 0))
    chunk_all = pl.BlockSpec((1, nc, LANES), lambda i, j: (i, 0, 0))
    bcum, a, cm, g, amax = pl.pallas_call(
        _gatescan_kernel, name="mlstm_gate_scan",
        grid=(b, s // GP_BLK),
        in_specs=[tok, tok, pl.BlockSpec((1, LANES), lambda i, j: (0, 0))],
        out_specs=(tok, tok, tok, chunk_tile, chunk_tile),
        out_shape=(tok_shape,) * 3 + (chunk_shape,) * 2,
        compiler_params=_cparams(2),
    )(ig, fg, fbias_row)
    ws, m_col, iw_col, en_col, decay = pl.pallas_call(
        _gateout_kernel, name="mlstm_gate_out",
        grid=(b, s // GP_BLK),
        in_specs=[tok, tok, tok, chunk_all, chunk_all],
        out_specs=(tok, tok, tok, tok, chunk_all),
        out_shape=(tok_shape,) * 4 + (chunk_shape,),
        scratch_shapes=[pltpu.VMEM((nc, LANES), F32)] * 2,
        compiler_params=_cparams(2),
    )(bcum, a, cm, g, amax)
    return a, ws, m_col, iw_col, en_col, decay


TE = 512
PAIR = 2 * ML_CHUNK


def _mlstm_chunk(state, d, h, c, qc, ktc, vc, a_row, ws_row, m_col, iw_col, en_col, decay):
    t_idx = lax.broadcasted_iota(jnp.int32, (ML_CHUNK, ML_CHUNK), 0)
    s_idx = lax.broadcasted_iota(jnp.int32, (ML_CHUNK, ML_CHUNK), 1)
    tri = (s_idx <= t_idx) if d == 0 else (s_idx >= t_idx)
    s = jnp.dot(qc, ktc, preferred_element_type=F32)
    w = jnp.exp(jnp.where(tri, a_row - m_col, NEG_BIG))
    sw = s * w
    den_intra = jnp.sum(sw, axis=-1, keepdims=True)
    cst = state[d * ML_HEADS + h]
    intra = jnp.dot(sw.astype(BF16), vc, preferred_element_type=F32)
    inter = jnp.dot(qc, cst.astype(BF16), preferred_element_type=F32)
    tot = intra + iw_col * inter
    den = den_intra + iw_col * inter[:, ML_HD:ML_HD + 1]
    hh = tot[:, :ML_HD] / jnp.maximum(jnp.abs(den), en_col)
    upd = jnp.dot((ktc.astype(F32) * ws_row).astype(BF16), vc, preferred_element_type=F32)
    state[d * ML_HEADS + h] = decay * cst + upd
    return hh


def _mlstm_kernel(*refs):
    (qf, ktf, vf, af, wsf, mf, iwf, enf, qb, ktb, vb, ab, wsb, mb, iwb, enb, dec_ref,
     hf_ref, hb_ref, state) = refs
    i = pl.program_id(1)
    nt = pl.num_programs(1)

    @pl.when(i == 0)
    def _():
        state[...] = jnp.zeros(state.shape, F32)

    n_pairs = TE // PAIR
    cpt = TE // ML_CHUNK
    dirs = ((0, qf, ktf, vf, af, wsf, mf, iwf, enf, hf_ref), (1, qb, ktb, vb, ab, wsb, mb, iwb, enb, hb_ref))

    def pair_body(p, carry):
        for d, q_r, kt_r, v_r, a_r, ws_r, m_r, iw_r, en_r, out_r in dirs:
            pp = p if d == 0 else n_pairs - 1 - p
            tile = i if d == 0 else nt - 1 - i
            lanes = pl.ds(pl.multiple_of(pp * PAIR, PAIR), PAIR)
            for half in ((0, 1) if d == 0 else (1, 0)):
                rows = pl.ds(pl.multiple_of(pp * PAIR + half * ML_CHUNK, ML_CHUNK), ML_CHUNK)
                chunk = tile * cpt + pp * 2 + half
                hs = slice(half * ML_CHUNK, (half + 1) * ML_CHUNK)
                for h in range(ML_HEADS):
                    ch = d * ML_HEADS + h
                    hh = _mlstm_chunk(
                        state, d, h, chunk,
                        q_r[0, h, rows, :], kt_r[0, h, :, lanes][:, hs], v_r[0, h, rows, :],
                        a_r[0, ch:ch + 1, lanes][:, hs], ws_r[0, ch:ch + 1, lanes][:, hs],
                        m_r[0, rows, ch:ch + 1], iw_r[0, rows, ch:ch + 1], en_r[0, rows, ch:ch + 1],
                        dec_ref[0, pl.ds(chunk, 1), ch:ch + 1])
                    out_r[0, rows, h * ML_HD:(h + 1) * ML_HD] = hh
        return carry

    lax.fori_loop(0, n_pairs, pair_body, 0)


def _mlstm_call(q_hm, kt_hm, v_aug, a_row, ws_row, m_col, iw_col, en_col, decay):
    b, _, s, _ = q_hm.shape
    nt = s // TE
    nc = s // ML_CHUNK

    def specs(rev):
        ti = (lambda i: nt - 1 - i) if rev else (lambda i: i)
        return [
            pl.BlockSpec((1, ML_HEADS, TE, ML_HD), lambda b, i: (b, 0, ti(i), 0)),
            pl.BlockSpec((1, ML_HEADS, ML_HD, TE), lambda b, i: (b, 0, 0, ti(i))),
            pl.BlockSpec((1, ML_HEADS, TE, 2 * ML_HD), lambda b, i: (b, 0, ti(i), 0)),
            pl.BlockSpec((1, 2 * ML_HEADS, TE), lambda b, i: (b, 0, ti(i))),
            pl.BlockSpec((1, 2 * ML_HEADS, TE), lambda b, i: (b, 0, ti(i))),
            pl.BlockSpec((1, TE, LANES), lambda b, i: (b, ti(i), 0)),
            pl.BlockSpec((1, TE, LANES), lambda b, i: (b, ti(i), 0)),
            pl.BlockSpec((1, TE, LANES), lambda b, i: (b, ti(i), 0)),
        ]

    args = [q_hm, kt_hm, v_aug, a_row, ws_row, m_col, iw_col, en_col]
    out_f = pl.BlockSpec((1, TE, MIX_W), lambda b, i: (b, i, 0))
    out_b = pl.BlockSpec((1, TE, MIX_W), lambda b, i: (b, nt - 1 - i, 0))
    return pl.pallas_call(
        _mlstm_kernel, name="mlstm_scan",
        grid=(b, nt),
        in_specs=specs(False) + specs(True) + [pl.BlockSpec((1, nc, LANES), lambda b, i: (b, 0, 0))],
        out_specs=(out_f, out_b),
        out_shape=(jax.ShapeDtypeStruct((b, s, MIX_W), F32),) * 2,
        scratch_shapes=[pltpu.VMEM((2 * ML_HEADS, ML_HD, 2 * ML_HD), F32)],
        compiler_params=_cparams(2),
    )(*args, *args, decay)


def _mlstm_branch(cqk, cv, ig, fg, conv_w, fbias):
    q_hm, kt_hm, v_aug = _mlprep_call(cqk, cv, conv_w)
    fb_row = jnp.pad(fbias.reshape(1, 2 * ML_HEADS), ((0, 0), (0, LANES - 2 * ML_HEADS)))
    a, ws, m_col, iw_col, en_col, decay = _gateprep_call(ig, fg, fb_row)
    rows = lambda t: jnp.transpose(t[..., :2 * ML_HEADS], (0, 2, 1))
    return _mlstm_call(q_hm, kt_hm, v_aug, rows(a), rows(ws), m_col, iw_col, en_col, decay)


TF = 256


def _merge_kernel(x_ref, ya_ref, o1_ref, o2_ref, o3_ref, l1_ref, l2_ref, l3_ref, hf_ref, hb_ref,
                  co_ref, yd_ref, wg_ref, bg_ref, wbr_ref, wout_ref, mng_ref, lng_ref, lnb_ref,
                  wr_ref, x1_ref, aff_ref, *, alpha):
    x = x_ref[0]
    xb = x.astype(BF16)
    l1, l2, l3 = l1_ref[0], l2_ref[0], l3_ref[0]
    lm = jnp.maximum(jnp.maximum(l1, l2), l3)
    e1, e2, e3 = jnp.exp(l1 - lm), jnp.exp(l2 - lm), jnp.exp(l3 - lm)
    y_b = (e1 * o1_ref[0] + e2 * o2_ref[0] + e3 * o3_ref[0]) / (e1 + e2 + e3)
    hsum = hf_ref[0] + hb_ref[0]
    lane_head = lax.broadcasted_iota(jnp.int32, (TF, MIX_W), 1) // ML_HD
    mu = jnp.zeros((TF, MIX_W), F32)
    for h in range(ML_HEADS):
        sel = lane_head == h
        mu = jnp.where(sel, jnp.sum(jnp.where(sel, hsum, 0.0), axis=-1, keepdims=True) / ML_HD, mu)
    cen = hsum - mu
    var = jnp.zeros((TF, MIX_W), F32)
    for h in range(ML_HEADS):
        sel = lane_head == h
        var = jnp.where(sel, jnp.sum(jnp.where(sel, cen * cen, 0.0), axis=-1, keepdims=True) / ML_HD, var)
    y_c = jax.nn.sigmoid(co_ref[0]) * (cen * lax.rsqrt(var + LN_EPS) * mng_ref[...])
    ys = (ya_ref[0], y_b.astype(BF16), y_c.astype(BF16), yd_ref[0])
    merged = jnp.zeros((TF, D_MODEL), F32)
    for n in range(N_BRANCH):
        cols = slice(n * D_MODEL, (n + 1) * D_MODEL)
        gate = jax.nn.sigmoid(jnp.dot(xb, wg_ref[:, cols], preferred_element_type=F32) + bg_ref[:, cols])
        merged = merged + gate * jnp.dot(ys[n], wbr_ref[n], preferred_element_type=F32)
    mix = jnp.dot(merged.astype(BF16), wout_ref[...], preferred_element_type=F32)
    x1 = _standardize(alpha * x + mix) * lng_ref[...] + lnb_ref[...]
    x1_ref[0] = x1
    logits = lax.dot_general(wr_ref[...], x1.astype(BF16), (((1,), (1,)), ((), ())),
                             preferred_element_type=F32)
    ex = jnp.exp(logits - jnp.max(logits, axis=0, keepdims=True))
    aff_ref[0] = ex / jnp.sum(ex, axis=0, keepdims=True)


def _merge_call(x, ya, o_list, l_list, hf, hb, co, yd, wg, bg, wbr, wout, mng, lng, lnb, wr_t, alpha):
    b, s, _ = x.shape
    tok = lambda w: pl.BlockSpec((1, TF, w), lambda b, i: (b, i, 0))
    const = lambda shp: pl.BlockSpec(shp, lambda b, i: (0,) * len(shp))
    return pl.pallas_call(
        functools.partial(_merge_kernel, alpha=alpha), name="merge_ln_router",
        grid=(b, s // TF),
        in_specs=[tok(D_MODEL)] + [tok(MIX_W)] * 11
                 + [const((D_MODEL, N_BRANCH * D_MODEL)), const((1, N_BRANCH * D_MODEL)),
                    const((N_BRANCH, MIX_W, D_MODEL)), const((D_MODEL, D_MODEL)), const((1, MIX_W)),
                    const((1, D_MODEL)), const((1, D_MODEL)), const((N_EXPERTS, D_MODEL))],
        out_specs=(tok(D_MODEL), pl.BlockSpec((1, N_EXPERTS, TF), lambda b, i: (b, 0, i))),
        out_shape=(jax.ShapeDtypeStruct((b, s, D_MODEL), F32),
                   jax.ShapeDtypeStruct((b, N_EXPERTS, s), F32)),
        compiler_params=_cparams(2),
    )(x, ya, *o_list, *l_list, hf, hb, co, yd, wg, bg, wbr, wout, mng, lng, lnb, wr_t)


TT = 256


def _select_kernel(aff_ref, slot_ref, tstart_ref, *, cap):
    s = aff_ref.shape[2]
    bits = pltpu.bitcast(aff_ref[0], jnp.int32)

    def bit_step(i, thr):
        cand = thr | jnp.left_shift(jnp.int32(1), 30 - i)
        cnt = jnp.sum((bits >= cand).astype(jnp.int32), axis=1, keepdims=True)
        return jnp.where(cnt >= cap, cand, thr)

    thr = lax.fori_loop(0, 31, bit_step, jnp.zeros((N_EXPERTS, 1), jnp.int32))
    gt = bits > thr
    eq = bits == thr
    need = (cap - jnp.sum(gt.astype(jnp.int32), axis=1, keepdims=True)).astype(F32)
    upper = (lax.broadcasted_iota(jnp.int32, (TT, TT), 0)
             <= lax.broadcasted_iota(jnp.int32, (TT, TT), 1)).astype(BF16)
    eq_before = jnp.zeros((N_EXPERTS, 1), F32)
    sel_before = jnp.zeros((N_EXPERTS, 1), F32)
    for j in range(s // TT):
        cols = slice(j * TT, (j + 1) * TT)
        eq_j = eq[:, cols]
        eq_incl = eq_before + jnp.dot(eq_j.astype(BF16), upper, preferred_element_type=F32)
        sel_j = gt[:, cols] | (eq_j & (eq_incl <= need))
        sel_f = sel_j.astype(F32)
        sel_incl = sel_before + jnp.dot(sel_f.astype(BF16), upper, preferred_element_type=F32)
        slot_ref[0, :, cols] = jnp.where(sel_j, sel_incl - 1.0, -1.0).astype(jnp.int32)
        tstart_ref[0, :, j:j + 1] = sel_before.astype(jnp.int32)
        eq_before = eq_incl[:, TT - 1:TT]
        sel_before = sel_incl[:, TT - 1:TT]


def _select_call(aff_t, cap):
    b, e, s = aff_t.shape
    return pl.pallas_call(
        functools.partial(_select_kernel, cap=cap), name="expert_choice_select",
        grid=(b,),
        in_specs=[pl.BlockSpec((1, e, s), lambda i: (i, 0, 0))],
        out_specs=(pl.BlockSpec((1, e, s), lambda i: (i, 0, 0)),
                   pl.BlockSpec((1, e, s // TT), lambda i: (i, 0, 0))),
        out_shape=(jax.ShapeDtypeStruct((b, e, s), jnp.int32),
                   jax.ShapeDtypeStruct((b, e, s // TT), jnp.int32)),
        compiler_params=_cparams(1),
    )(aff_t)


CB = 256


def _tile_range(tstart_ref, base, n_tiles, c0, cap):
    def body(t, carry):
        lo, hi = carry
        start = tstart_ref[base + t]
        end = jnp.where(t + 1 < n_tiles, tstart_ref[base + jnp.minimum(t + 1, n_tiles - 1)], cap)
        lo = lo + (end <= c0).astype(jnp.int32)
        hi = hi + (start < c0 + CB).astype(jnp.int32)
        return lo, hi
    return lax.fori_loop(0, n_tiles, body, (jnp.int32(0), jnp.int32(0)))


def _expert_kernel(tstart_ref, x_ref, slot_ref, aff_ref, w1_ref, w3_ref, w2_ref, hi_ref, lo_ref,
                   xs_acc, g_acc, *, cap):
    b, e, j = pl.program_id(0), pl.program_id(1), pl.program_id(2)
    n_tiles = slot_ref.shape[2]
    c0 = j * CB
    lo_t, hi_t = _tile_range(tstart_ref, (b * N_EXPERTS + e) * n_tiles, n_tiles, c0, cap)
    xs_acc[...] = jnp.zeros(xs_acc.shape, F32)
    g_acc[...] = jnp.zeros(g_acc.shape, F32)
    slot_iota = c0 + lax.broadcasted_iota(jnp.int32, (CB, TT), 0)

    def gather_tile(t, carry):
        hit = slot_ref[0, 0, pl.ds(t, 1), :] == slot_iota
        onehot = jnp.where(hit, 1.0, 0.0).astype(BF16)
        xt = x_ref[0, pl.ds(pl.multiple_of(t * TT, TT), TT), :]
        xs_acc[...] += jnp.dot(onehot, xt, preferred_element_type=F32)
        g_acc[...] += jnp.sum(jnp.where(hit, aff_ref[0, 0, pl.ds(t, 1), :], 0.0), axis=1, keepdims=True)
        return carry

    lax.fori_loop(lo_t, hi_t, gather_tile, 0)
    xs = xs_acc[...].astype(BF16)
    hid = (jax.nn.silu(jnp.dot(xs, w1_ref[0], preferred_element_type=F32))
           * jnp.dot(xs, w3_ref[0], preferred_element_type=F32))
    ye = jnp.dot(hid.astype(BF16), w2_ref[0], preferred_element_type=F32) * g_acc[...]
    hi = ye.astype(BF16)
    hi_ref[0, 0] = hi
    lo_ref[0, 0] = (ye - hi.astype(F32)).astype(BF16)


def _expert_call(tstart_flat, x1_bf, slot4, aff4, w1, w3, w2, cap):
    b, s, d = x1_bf.shape
    e, _, ff = w1.shape
    nt = s // TT
    grid_spec = pltpu.PrefetchScalarGridSpec(
        num_scalar_prefetch=1,
        grid=(b, e, cap // CB),
        in_specs=[pl.BlockSpec((1, s, d), lambda b, e, j, ts: (b, 0, 0)),
                  pl.BlockSpec((1, 1, nt, TT), lambda b, e, j, ts: (b, e, 0, 0)),
                  pl.BlockSpec((1, 1, nt, TT), lambda b, e, j, ts: (b, e, 0, 0)),
                  pl.BlockSpec((1, d, ff), lambda b, e, j, ts: (e, 0, 0)),
                  pl.BlockSpec((1, d, ff), lambda b, e, j, ts: (e, 0, 0)),
                  pl.BlockSpec((1, ff, d), lambda b, e, j, ts: (e, 0, 0))],
        out_specs=(pl.BlockSpec((1, 1, CB, d), lambda b, e, j, ts: (b, e, j, 0)),
                   pl.BlockSpec((1, 1, CB, d), lambda b, e, j, ts: (b, e, j, 0))),
        scratch_shapes=[pltpu.VMEM((CB, d), F32), pltpu.VMEM((CB, 1), F32)])
    return pl.pallas_call(
        functools.partial(_expert_kernel, cap=cap), name="expert_gather_ffn",
        grid_spec=grid_spec,
        out_shape=(jax.ShapeDtypeStruct((b, e, cap, d), BF16),) * 2,
        compiler_params=_cparams(3),
    )(tstart_flat, x1_bf, slot4, aff4, w1, w3, w2)


DQ = 512


def _combine_kernel(tstart_ref, slot_ref, hi_ref, lo_ref, out_ref, *, cap):
    b, e, j = pl.program_id(0), pl.program_id(2), pl.program_id(3)
    n_tiles = slot_ref.shape[2]
    c0 = j * CB

    @pl.when((e == 0) & (j == 0))
    def _():
        out_ref[...] = jnp.zeros(out_ref.shape, F32)

    lo_t, hi_t = _tile_range(tstart_ref, (b * N_EXPERTS + e) * n_tiles, n_tiles, c0, cap)
    slot_iota = c0 + lax.broadcasted_iota(jnp.int32, (CB, TT), 0)
    tdims = (((0,), (0,)), ((), ()))

    def scatter_tile(t, carry):
        onehot = jnp.where(slot_ref[0, 0, pl.ds(t, 1), :] == slot_iota, 1.0, 0.0).astype(BF16)
        rows = pl.ds(pl.multiple_of(t * TT, TT), TT)
        out_ref[0, rows, :] += (
            lax.dot_general(onehot, hi_ref[0, 0], tdims, preferred_element_type=F32)
            + lax.dot_general(onehot, lo_ref[0, 0], tdims, preferred_element_type=F32))
        return carry

    lax.fori_loop(lo_t, hi_t, scatter_tile, 0)


def _combine_call(tstart_flat, slot4, ye_hi, ye_lo, s, cap):
    b, e, _, d = ye_hi.shape
    nt = s // TT
    grid_spec = pltpu.PrefetchScalarGridSpec(
        num_scalar_prefetch=1,
        grid=(b, d // DQ, e, cap // CB),
        in_specs=[pl.BlockSpec((1, 1, nt, TT), lambda b, q, e, j, ts: (b, e, 0, 0)),
                  pl.BlockSpec((1, 1, CB, DQ), lambda b, q, e, j, ts: (b, e, j, q)),
                  pl.BlockSpec((1, 1, CB, DQ), lambda b, q, e, j, ts: (b, e, j, q))],
        out_specs=pl.BlockSpec((1, s, DQ), lambda b, q, e, j, ts: (b, 0, q)))
    return pl.pallas_call(
        functools.partial(_combine_kernel, cap=cap), name="expert_combine",
        grid_spec=grid_spec,
        out_shape=jax.ShapeDtypeStruct((b, s, d), F32),
        compiler_params=_cparams(4),
    )(tstart_flat, slot4, ye_hi, ye_lo)


TN = 512


def _resln_kernel(x_ref, y_ref, g_ref, b_ref, o_ref, *, alpha):
    o_ref[...] = _standardize(alpha * x_ref[...] + y_ref[...]) * g_ref[...] + b_ref[...]


def _resln_call(x2d, y2d, g, bta, alpha):
    n, d = x2d.shape
    tok = pl.BlockSpec((TN, d), lambda i: (i, 0))
    vec = pl.BlockSpec((1, d), lambda i: (0, 0))
    return pl.pallas_call(
        functools.partial(_resln_kernel, alpha=alpha), name="residual_layernorm",
        grid=(n // TN,), in_specs=[tok, tok, vec, vec], out_specs=tok,
        out_shape=jax.ShapeDtypeStruct((n, d), F32),
        compiler_params=_cparams(1),
    )(x2d, y2d, g, bta)


def _expert_choice_ffn(x1, aff_t, w1, w3, w2):
    b, s, d = x1.shape
    cap = EC_FACTOR * s // N_EXPERTS
    nt = s // TT
    slot, tstart = _select_call(aff_t, cap)
    slot4 = slot.reshape(b, N_EXPERTS, nt, TT)
    aff4 = aff_t.reshape(b, N_EXPERTS, nt, TT)
    ts_flat = tstart.reshape(-1)
    ye_hi, ye_lo = _expert_call(ts_flat, x1.astype(BF16), slot4, aff4, w1, w3, w2, cap)
    return _combine_call(ts_flat, slot4, ye_hi, ye_lo, s, cap)


def _pack_pool(pool_w):
    g, gd, _ = pool_w.shape
    out = jnp.zeros((g * gd, g * gd), F32)
    for i in range(g):
        out = out.at[i * gd:(i + 1) * gd, i * gd:(i + 1) * gd].set(pool_w[i])
    return out.astype(BF16)


def _layer(x, alpha, bias_tiles, w_in, b_in, gm_ln_g, gm_ws, gm_bs, ml_conv, ml_fbias, ml_norm_g,
           pool_w, pool_scale, w_branch, w_out, ln1_g, ln1_b, w_router, w_e1, w_e3, w_e2, ln2_g, ln2_b):
    b, s, d = x.shape
    n_small = 2576
    w_cat, b_cat = _pack_inproj_weights(w_in, b_in)
    wscat, bsfull = _pack_gmlp(gm_ws, gm_bs)
    ya, qkv, cqk, cv, co, dx, ig, fg = _inproj_call(x.reshape(b * s, d), w_cat, b_cat, gm_ln_g[None],
                                                     wscat, bsfull)
    r3 = lambda t: t.reshape(b, s, t.shape[-1])
    qkv3 = r3(qkv)
    o_list, l_list = [], []
    for (window, dil), bias in zip(DIL_PATTERNS, bias_tiles):
        o, lse = _attn_call(_to_residues(qkv3, dil), bias)
        o_list.append(_from_residues(o, dil, b))
        l_list.append(_from_residues(lse, dil, b))
    hf, hb = _mlstm_branch(r3(cqk), r3(cv), r3(ig), r3(fg), ml_conv, ml_fbias)
    yd = _pool_call(r3(dx), _pack_pool(pool_w), pool_scale[None])
    x1, aff_t = _merge_call(
        x, r3(ya), o_list, l_list, hf, hb, r3(co), yd,
        w_in[:, n_small:].astype(BF16), b_in[None, n_small:], w_branch.astype(BF16), w_out.astype(BF16),
        ml_norm_g[None], ln1_g[None], ln1_b[None], jnp.transpose(w_router).astype(BF16), alpha)
    ffn = _expert_choice_ffn(x1, aff_t, w_e1.astype(BF16), w_e3.astype(BF16), w_e2.astype(BF16))
    x2 = _resln_call(x1.reshape(b * s, d), ffn.reshape(b * s, d), ln2_g[None], ln2_b[None], alpha)
    return x2.reshape(b, s, d)


def kernel(x, w_in, b_in, gm_ln_g, gm_ws, gm_bs, rel_bias, ml_conv, ml_fbias, ml_norm_g, pool_w,
           pool_scale, w_branch, w_out, ln1_g, ln1_b, w_router, w_e1, w_e3, w_e2, ln2_g, ln2_b):
    depth = w_in.shape[0]
    alpha = (2 * depth) ** 0.25
    bias_tiles = [_attn_bias_tile(rel_bias, window, dil) for window, dil in DIL_PATTERNS]
    for l in range(depth):
        x = _layer(x, alpha, bias_tiles, w_in[l], b_in[l], gm_ln_g[l], gm_ws[l], gm_bs[l], ml_conv[l],
                   ml_fbias[l], ml_norm_g[l], pool_w[l], pool_scale[l], w_branch[l], w_out[l],
                   ln1_g[l], ln1_b[l], w_router[l], w_e1[l], w_e3[l], w_e2[l], ln2_g[l], ln2_b[l])
    return x
```

```python
import functools
import math

import jax
import jax.numpy as jnp
import numpy as np
from jax import lax
from jax.experimental import pallas as pl
from jax.experimental.pallas import tpu as pltpu
from jax.experimental.pallas import tpu_sc as plsc

F32 = jnp.float32
BF16 = jnp.bfloat16

D_MODEL = 1024
MIX_W = 256
N_BRANCH = 4
GM_CHUNK = 128
GM_GROUPS = 4
ATT_HEADS = 4
ATT_HD = 64
DIL_PATTERNS = ((128, 1), (512, 4), (2048, 16))
ATT_BLOCK = 64
REL_BUCKETS = 32
REL_MAX_DIST = 1024
ML_HEADS = 4
ML_HD = 64
ML_CHUNK = 64
POOL_WINDOWS = (2, 4, 8, 16)
N_EXPERTS = 16
EXPERT_FF = 1024
EC_FACTOR = 2
LN_EPS = 1e-5
NEG_BIG = -1e30

V7X_VMEM_LIMIT = 56 * 1024 * 1024
LANES = 128
HALO = 8


def _cparams(n_grid, vmem=V7X_VMEM_LIMIT):
    return pltpu.CompilerParams(dimension_semantics=("arbitrary",) * n_grid,
                                vmem_limit_bytes=vmem)


def _standardize(xf):
    mu = jnp.mean(xf, axis=-1, keepdims=True)
    var = jnp.mean(jnp.square(xf - mu), axis=-1, keepdims=True)
    return (xf - mu) * lax.rsqrt(var + LN_EPS)


TA = 512
A_COLS = 2560 + 2 * LANES


def _inproj_kernel(x_ref, w_ref, b_ref, lng_ref, wscat_ref, bsfull_ref,
                   ya_ref, qkv_ref, cqk_ref, cv_ref, co_ref, dx_ref, ig_ref, fg_ref):
    xb = x_ref[...].astype(BF16)
    h = jnp.dot(xb, w_ref[...], preferred_element_type=F32) + b_ref[...]
    qkv_ref[...] = h[:, 512:1280].astype(BF16)
    cqk_ref[...] = h[:, 1280:1792]
    cv_ref[...] = h[:, 1792:2048].astype(BF16)
    co_ref[...] = h[:, 2048:2304]
    dx_ref[...] = h[:, 2304:2560]
    ig_ref[...] = h[:, 2560:2688]
    fg_ref[...] = h[:, 2688:2816]
    u = jax.nn.gelu(h[:, 0:256])
    v = jax.nn.gelu(h[:, 256:512])
    vn = _standardize(v) * lng_ref[...]
    lane_grp = lax.broadcasted_iota(jnp.int32, (GM_CHUNK, MIX_W), 1) // (MIX_W // GM_GROUPS)
    for c in range(TA // GM_CHUNK):
        vc = vn[c * GM_CHUNK:(c + 1) * GM_CHUNK]
        stacked = jnp.concatenate(
            [jnp.where(lane_grp == g, vc, 0.0).astype(BF16) for g in range(GM_GROUPS)], axis=0)
        mixed = jnp.dot(wscat_ref[...], stacked, preferred_element_type=F32) + bsfull_ref[...]
        ya_ref[c * GM_CHUNK:(c + 1) * GM_CHUNK, :] = (
            u[c * GM_CHUNK:(c + 1) * GM_CHUNK] * mixed).astype(BF16)


def _inproj_call(x2d, w_cat, b_cat, lng, wscat, bsfull):
    n = x2d.shape[0]
    tok = lambda w: pl.BlockSpec((TA, w), lambda i: (i, 0))
    const = lambda s: pl.BlockSpec(s, lambda i: (0,) * len(s))
    out_shape = (
        jax.ShapeDtypeStruct((n, 256), BF16),
        jax.ShapeDtypeStruct((n, 768), BF16),
        jax.ShapeDtypeStruct((n, 512), F32),
        jax.ShapeDtypeStruct((n, 256), BF16),
        jax.ShapeDtypeStruct((n, 256), F32),
        jax.ShapeDtypeStruct((n, 256), F32),
        jax.ShapeDtypeStruct((n, LANES), F32),
        jax.ShapeDtypeStruct((n, LANES), F32),
    )
    return pl.pallas_call(
        _inproj_kernel, name="inproj_gmlp",
        grid=(n // TA,),
        in_specs=[tok(D_MODEL), const((D_MODEL, A_COLS)), const((1, A_COLS)), const((1, MIX_W)),
                  const((GM_CHUNK, GM_GROUPS * GM_CHUNK)), const((GM_CHUNK, MIX_W))],
        out_specs=(tok(256), tok(768), tok(512), tok(256), tok(256), tok(256), tok(LANES), tok(LANES)),
        out_shape=out_shape,
        compiler_params=_cparams(1),
    )(x2d, w_cat, b_cat, lng, wscat, bsfull)


def _pack_inproj_weights(w_in, b_in):
    pad = lambda a: jnp.pad(a, ((0, 0), (0, LANES - 8)))
    w_cat = jnp.concatenate([w_in[:, 0:2304], w_in[:, 2320:2576],
                             pad(w_in[:, 2304:2312]), pad(w_in[:, 2312:2320])], axis=1)
    b2 = b_in[None, :]
    b_cat = jnp.concatenate([b2[:, 0:2304], b2[:, 2320:2576],
                             pad(b2[:, 2304:2312]), pad(b2[:, 2312:2320])], axis=1)
    return w_cat.astype(BF16), b_cat


def _pack_gmlp(gm_ws, gm_bs):
    wscat = jnp.transpose(gm_ws, (1, 0, 2)).reshape(GM_CHUNK, GM_GROUPS * GM_CHUNK).astype(BF16)
    bsfull = jnp.repeat(jnp.transpose(gm_bs), MIX_W // GM_GROUPS, axis=1)
    return wscat, bsfull


def _halo_specs(t, width, n_tiles):
    r = t // HALO
    main = pl.BlockSpec((1, t, width), lambda b, i: (b, i, 0))
    prev = pl.BlockSpec((1, HALO, width), lambda b, i: (b, jnp.maximum(i * r - 1, 0), 0))
    nxt = pl.BlockSpec((1, HALO, width), lambda b, i: (b, jnp.minimum((i + 1) * r, n_tiles * r - 1), 0))
    return main, prev, nxt


def _fill_halo_scratch(buf, x_ref, p_ref, n_ref, t):
    i = pl.program_id(1)
    last = pl.num_programs(1) - 1
    buf[0:HALO, :] = jnp.where(i > 0, p_ref[0], 0.0)
    buf[HALO:HALO + t, :] = x_ref[0]
    buf[HALO + t:2 * HALO + t, :] = jnp.where(i < last, n_ref[0], 0.0)


TP = 512


def _pool_kernel(x_ref, p_ref, n_ref, w_ref, sc_ref, o_ref, buf):
    _fill_halo_scratch(buf, x_ref, p_ref, n_ref, TP)
    seq = pl.num_programs(1) * TP
    pos = pl.program_id(1) * TP + lax.broadcasted_iota(jnp.int32, (TP, 1), 0)
    lane_grp = lax.broadcasted_iota(jnp.int32, (TP, MIX_W), 1) // (MIX_W // len(POOL_WINDOWS))
    x0 = buf[HALO:HALO + TP, :]
    pooled = jnp.zeros((TP, MIX_W), F32)
    acc = None
    half_done = 0
    for gi, win in enumerate(POOL_WINDOWS):
        half = win // 2
        for o in list(range(-half, -half_done)) + list(range(half_done, half)):
            term = buf[HALO + o:HALO + o + TP, :]
            acc = term if acc is None else acc + term
        half_done = half
        cnt = (jnp.minimum(pos + half, seq) - jnp.maximum(pos - half, 0)).astype(F32)
        pooled = jnp.where(lane_grp == gi, acc / cnt - x0, pooled)
    mixed = jnp.dot(pooled.astype(BF16), w_ref[...], preferred_element_type=F32)
    o_ref[0] = (mixed * sc_ref[...]).astype(BF16)


def _pool_call(dx, w_block, scale):
    b, s, _ = dx.shape
    nt = s // TP
    main, prev, nxt = _halo_specs(TP, MIX_W, nt)
    return pl.pallas_call(
        _pool_kernel, name="pool_mixer",
        grid=(b, nt),
        in_specs=[main, prev, nxt,
                  pl.BlockSpec((MIX_W, MIX_W), lambda b, i: (0, 0)),
                  pl.BlockSpec((1, MIX_W), lambda b, i: (0, 0))],
        out_specs=pl.BlockSpec((1, TP, MIX_W), lambda b, i: (b, i, 0)),
        out_shape=jax.ShapeDtypeStruct((b, s, MIX_W), BF16),
        scratch_shapes=[pltpu.VMEM((TP + 2 * HALO, MIX_W), F32)],
        compiler_params=_cparams(2),
    )(dx, dx, dx, w_block, scale)


TQ = 128
TKEYS = TQ + 2 * ATT_BLOCK


def _attn_kernel(q_ref, kp_ref, km_ref, kn_ref, vp_ref, vm_ref, vn_ref, bias_ref, o_ref, lse_ref):
    i = pl.program_id(1)
    seq = pl.num_programs(1) * TQ
    kpos = i * TQ - ATT_BLOCK + lax.broadcasted_iota(jnp.int32, (1, TKEYS), 1)
    kvalid = (kpos >= 0) & (kpos < seq)
    q = q_ref[0]
    k = jnp.concatenate([kp_ref[0], km_ref[0], kn_ref[0]], axis=0)
    v = jnp.concatenate([vp_ref[0], vm_ref[0], vn_ref[0]], axis=0)
    for h in range(ATT_HEADS):
        sl = slice(h * ATT_HD, (h + 1) * ATT_HD)
        logits = lax.dot_general(q[:, sl], k[:, sl], (((1,), (1,)), ((), ())),
                                 preferred_element_type=F32) * ATT_HD ** -0.5 + bias_ref[h]
        logits = jnp.where(kvalid, logits, NEG_BIG)
        m = jnp.max(logits, axis=-1, keepdims=True)
        p = jnp.exp(logits - m)
        ssum = jnp.sum(p, axis=-1, keepdims=True)
        o = jnp.dot(p.astype(BF16), v[:, sl], preferred_element_type=F32) / ssum
        o_ref[0, :, sl] = o
        lse_ref[0, :, sl] = jnp.broadcast_to(m + jnp.log(ssum), (TQ, ATT_HD))


def _attn_call(qkv, bias):
    n, l, _ = qkv.shape
    nt = l // TQ
    r = TQ // ATT_BLOCK
    main = lambda c: pl.BlockSpec((1, TQ, MIX_W), lambda b, i: (b, i, c))
    prev = lambda c: pl.BlockSpec((1, ATT_BLOCK, MIX_W), lambda b, i: (b, jnp.maximum(i * r - 1, 0), c))
    nxt = lambda c: pl.BlockSpec((1, ATT_BLOCK, MIX_W),
                                 lambda b, i: (b, jnp.minimum((i + 1) * r, nt * r - 1), c))
    out = pl.BlockSpec((1, TQ, MIX_W), lambda b, i: (b, i, 0))
    return pl.pallas_call(
        _attn_kernel, name="band_attention",
        grid=(n, nt),
        in_specs=[main(0), prev(1), main(1), nxt(1), prev(2), main(2), nxt(2),
                  pl.BlockSpec((ATT_HEADS, TQ, TKEYS), lambda b, i: (0, 0, 0))],
        out_specs=(out, out),
        out_shape=(jax.ShapeDtypeStruct((n, l, MIX_W), F32), jax.ShapeDtypeStruct((n, l, MIX_W), F32)),
        compiler_params=_cparams(2),
    )(qkv, qkv, qkv, qkv, qkv, qkv, qkv, bias)


def _t5_bucket(rel):
    half = REL_BUCKETS // 2
    max_exact = half // 2
    ret = jnp.where(rel > 0, half, 0)
    n = jnp.abs(rel)
    nf = jnp.maximum(n, 1).astype(F32)
    large = max_exact + (jnp.log(nf / max_exact) / math.log(REL_MAX_DIST / max_exact)
                         * (half - max_exact)).astype(jnp.int32)
    large = jnp.minimum(large, half - 1)
    return ret + jnp.where(n < max_exact, n, large)


def _attn_bias_tile(rel_bias, window, dil):
    side = (window // 2) // dil
    rel = jnp.arange(TKEYS)[None, :] - ATT_BLOCK - jnp.arange(TQ)[:, None]
    bias = jnp.transpose(rel_bias[_t5_bucket(dil * rel)], (2, 0, 1)).astype(F32)
    return jnp.where((jnp.abs(rel) <= side)[None], bias, NEG_BIG)


def _to_residues(t, dil):
    b, s, c = t.shape
    return t.reshape(b, s // dil, dil, c).transpose(0, 2, 1, 3).reshape(b * dil, s // dil, c)


def _from_residues(t, dil, b):
    n, l, c = t.shape
    return t.reshape(b, dil, l, c).transpose(0, 2, 1, 3).reshape(b, l * dil, c)


TM = 512


def _mlprep_kernel(x_ref, p_ref, n_ref, v_ref, w_ref, q_out, kt_out, v_out, buf):
    _fill_halo_scratch(buf, x_ref, p_ref, n_ref, TM)
    conv = (buf[HALO - 1:HALO - 1 + TM, :] * w_ref[0:1, :] + buf[HALO:HALO + TM, :] * w_ref[1:2, :]
            + buf[HALO + 1:HALO + 1 + TM, :] * w_ref[2:3, :])
    qk = jax.nn.silu(conv)
    kt = jnp.transpose(qk[:, MIX_W:] * ML_HD ** -0.5)
    v = v_ref[0]
    ones_col = jnp.where(lax.broadcasted_iota(jnp.int32, (TM, ML_HD), 1) == 0, 1.0, 0.0).astype(BF16)
    for h in range(ML_HEADS):
        sl = slice(h * ML_HD, (h + 1) * ML_HD)
        q_out[0, h] = qk[:, sl].astype(BF16)
        kt_out[0, h] = kt[sl, :].astype(BF16)
        v_out[0, h] = jnp.concatenate([v[:, sl], ones_col], axis=1)


def _mlprep_call(cqk, cv, conv_w):
    b, s, _ = cqk.shape
    nt = s // TM
    main, prev, nxt = _halo_specs(TM, 2 * MIX_W, nt)
    return pl.pallas_call(
        _mlprep_kernel, name="mlstm_prep",
        grid=(b, nt),
        in_specs=[main, prev, nxt,
                  pl.BlockSpec((1, TM, MIX_W), lambda b, i: (b, i, 0)),
                  pl.BlockSpec((3, 2 * MIX_W), lambda b, i: (0, 0))],
        out_specs=(pl.BlockSpec((1, ML_HEADS, TM, ML_HD), lambda b, i: (b, 0, i, 0)),
                   pl.BlockSpec((1, ML_HEADS, ML_HD, TM), lambda b, i: (b, 0, 0, i)),
                   pl.BlockSpec((1, ML_HEADS, TM, 2 * ML_HD), lambda b, i: (b, 0, i, 0))),
        out_shape=(jax.ShapeDtypeStruct((b, ML_HEADS, s, ML_HD), BF16),
                   jax.ShapeDtypeStruct((b, ML_HEADS, ML_HD, s), BF16),
                   jax.ShapeDtypeStruct((b, ML_HEADS, s, 2 * ML_HD), BF16)),
        scratch_shapes=[pltpu.VMEM((TM + 2 * HALO, 2 * MIX_W), F32)],
        compiler_params=_cparams(2),
    )(cqk, cqk, cqk, cv, conv_w)


GP_BLK = 512


def _chunk_scan(x, fwd_lane, t_in_chunk, op, ident):
    n = x.shape[0]
    shift = 1
    while shift < ML_CHUNK:
        down = pltpu.roll(x, shift, 0)
        up = pltpu.roll(x, n - shift, 0)
        nb = jnp.where(fwd_lane,
                       jnp.where(t_in_chunk >= shift, down, ident),
                       jnp.where(t_in_chunk < ML_CHUNK - shift, up, ident))
        x = op(x, nb)
        shift *= 2
    return x


def _gatescan_kernel(ig_ref, fg_ref, fb_ref, b_ref, a_ref, cm_ref, g_ref, amax_ref):
    lane = lax.broadcasted_iota(jnp.int32, (1, LANES), 1)
    fwd_lane = lane < ML_HEADS
    t_in_chunk = lax.broadcasted_iota(jnp.int32, (GP_BLK, 1), 0) % ML_CHUNK
    z = fg_ref[0] + fb_ref[...]
    lf = jnp.minimum(z, 0.0) - jnp.log1p(jnp.exp(-jnp.abs(z)))
    b = _chunk_scan(lf, fwd_lane, t_in_chunk, jnp.add, 0.0)
    a = ig_ref[0] - b
    b_ref[0] = b
    a_ref[0] = a
    cm_ref[0] = _chunk_scan(a, fwd_lane, t_in_chunk, jnp.maximum, -jnp.inf)
    cpb = GP_BLK // ML_CHUNK
    last = pl.ds(ML_CHUNK - 1, cpb, stride=ML_CHUNK)
    first = pl.ds(0, cpb, stride=ML_CHUNK)
    g_ref[0] = jnp.where(fwd_lane, b_ref[0, last, :], b_ref[0, first, :])
    amax_ref[0] = jnp.where(fwd_lane, cm_ref[0, last, :], cm_ref[0, first, :])


def _gateout_kernel(b_ref, a_ref, cm_ref, g_ref, amax_ref, ws_ref, m_ref, iw_ref, en_ref, dec_ref,
                    mch_s, mlast_s):
    nc = g_ref.shape[1]
    j = pl.program_id(1)
    fwd_lane = lax.broadcasted_iota(jnp.int32, (1, LANES), 1) < ML_HEADS

    @pl.when(j == 0)
    def _():
        def m_step(i, carry):
            mf, mb = carry
            cf = pl.ds(i, 1)
            cb = pl.ds(nc - 1 - i, 1)
            mch_s[cf, :] = jnp.where(fwd_lane, mf, mch_s[cf, :])
            mch_s[cb, :] = jnp.where(fwd_lane, mch_s[cb, :], mb)
            mf = g_ref[0, cf, :] + jnp.maximum(mf, amax_ref[0, cf, :])
            mb = g_ref[0, cb, :] + jnp.maximum(mb, amax_ref[0, cb, :])
            return mf, mb

        mch_s[...] = jnp.zeros((nc, LANES), F32)
        zero = jnp.zeros((1, LANES), F32)
        lax.fori_loop(0, nc, m_step, (zero, zero))
        mlast = jnp.maximum(amax_ref[0], mch_s[...])
        mlast_s[...] = mlast
        dec_ref[0] = jnp.exp(mch_s[...] - mlast)

    cpb = GP_BLK // ML_CHUNK
    crow = pl.ds(pl.multiple_of(j * cpb, cpb), cpb)
    expand = lambda t: jnp.broadcast_to(t[:, None, :], (cpb, ML_CHUNK, LANES)).reshape(GP_BLK, LANES)
    m_tok = expand(mch_s[crow, :])
    mlast_tok = expand(mlast_s[crow, :])
    mt = jnp.maximum(cm_ref[0], m_tok)
    m_ref[0] = mt
    iw_ref[0] = jnp.exp(m_tok - mt)
    en_ref[0] = jnp.exp(-(b_ref[0] + mt))
    ws_ref[0] = jnp.exp(a_ref[0] - mlast_tok)


def _gateprep_call(ig, fg, fbias_row):
    b, s, _ = ig.shape
    nc = s // ML_CHUNK
    cpb = GP_BLK // ML_CHUNK
    tok = pl.BlockSpec((1, GP_BLK, LANES), lambda i, j: (i, j, 0))
    tok_shape = jax.ShapeDtypeStruct((b, s, LANES), F32)
    chunk_shape = jax.ShapeDtypeStruct((b, nc, LANES), F32)
    chunk_tile = pl.BlockSpec((1, cpb, LANES), lambda i, j: (i, j, 0))
    chunk_all = pl.BlockSpec((1, nc, LANES), lambda i, j: (i, 0, 0))
    bcum, a, cm, g, amax = pl.pallas_call(
        _gatescan_kernel, name="mlstm_gate_scan",
        grid=(b, s // GP_BLK),
        in_specs=[tok, tok, pl.BlockSpec((1, LANES), lambda i, j: (0, 0))],
        out_specs=(tok, tok, tok, chunk_tile, chunk_tile),
        out_shape=(tok_shape,) * 3 + (chunk_shape,) * 2,
        compiler_params=_cparams(2),
    )(ig, fg, fbias_row)
    ws, m_col, iw_col, en_col, decay = pl.pallas_call(
        _gateout_kernel, name="mlstm_gate_out",
        grid=(b, s // GP_BLK),
        in_specs=[tok, tok, tok, chunk_all, chunk_all],
        out_specs=(tok, tok, tok, tok, chunk_all),
        out_shape=(tok_shape,) * 4 + (chunk_shape,),
        scratch_shapes=[pltpu.VMEM((nc, LANES), F32)] * 2,
        compiler_params=_cparams(2),
    )(bcum, a, cm, g, amax)
    return a, ws, m_col, iw_col, en_col, decay


TE = 512
PAIR = 2 * ML_CHUNK


def _mlstm_chunk(state, d, h, c, qc, ktc, vc, a_row, ws_row, m_col, iw_col, en_col, decay):
    t_idx = lax.broadcasted_iota(jnp.int32, (ML_CHUNK, ML_CHUNK), 0)
    s_idx = lax.broadcasted_iota(jnp.int32, (ML_CHUNK, ML_CHUNK), 1)
    tri = (s_idx <= t_idx) if d == 0 else (s_idx >= t_idx)
    s = jnp.dot(qc, ktc, preferred_element_type=F32)
    w = jnp.exp(jnp.where(tri, a_row - m_col, NEG_BIG))
    sw = s * w
    den_intra = jnp.sum(sw, axis=-1, keepdims=True)
    cst = state[d * ML_HEADS + h]
    intra = jnp.dot(sw.astype(BF16), vc, preferred_element_type=F32)
    inter = jnp.dot(qc, cst.astype(BF16), preferred_element_type=F32)
    tot = intra + iw_col * inter
    den = den_intra + iw_col * inter[:, ML_HD:ML_HD + 1]
    hh = tot[:, :ML_HD] / jnp.maximum(jnp.abs(den), en_col)
    upd = jnp.dot((ktc.astype(F32) * ws_row).astype(BF16), vc, preferred_element_type=F32)
    state[d * ML_HEADS + h] = decay * cst + upd
    return hh


def _mlstm_kernel(*refs):
    (qf, ktf, vf, af, wsf, mf, iwf, enf, qb, ktb, vb, ab, wsb, mb, iwb, enb, dec_ref,
     hf_ref, hb_ref, state) = refs
    i = pl.program_id(1)
    nt = pl.num_programs(1)

    @pl.when(i == 0)
    def _():
        state[...] = jnp.zeros(state.shape, F32)

    n_pairs = TE // PAIR
    cpt = TE // ML_CHUNK
    dirs = ((0, qf, ktf, vf, af, wsf, mf, iwf, enf, hf_ref), (1, qb, ktb, vb, ab, wsb, mb, iwb, enb, hb_ref))

    def pair_body(p, carry):
        for d, q_r, kt_r, v_r, a_r, ws_r, m_r, iw_r, en_r, out_r in dirs:
            pp = p if d == 0 else n_pairs - 1 - p
            tile = i if d == 0 else nt - 1 - i
            lanes = pl.ds(pl.multiple_of(pp * PAIR, PAIR), PAIR)
            for half in ((0, 1) if d == 0 else (1, 0)):
                rows = pl.ds(pl.multiple_of(pp * PAIR + half * ML_CHUNK, ML_CHUNK), ML_CHUNK)
                chunk = tile * cpt + pp * 2 + half
                hs = slice(half * ML_CHUNK, (half + 1) * ML_CHUNK)
                for h in range(ML_HEADS):
                    ch = d * ML_HEADS + h
                    hh = _mlstm_chunk(
                        state, d, h, chunk,
                        q_r[0, h, rows, :], kt_r[0, h, :, lanes][:, hs], v_r[0, h, rows, :],
                        a_r[0, ch:ch + 1, lanes][:, hs], ws_r[0, ch:ch + 1, lanes][:, hs],
                        m_r[0, rows, ch:ch + 1], iw_r[0, rows, ch:ch + 1], en_r[0, rows, ch:ch + 1],
                        dec_ref[0, pl.ds(chunk, 1), ch:ch + 1])
                    out_r[0, rows, h * ML_HD:(h + 1) * ML_HD] = hh
        return carry

    lax.fori_loop(0, n_pairs, pair_body, 0)


def _mlstm_call(q_hm, kt_hm, v_aug, a_row, ws_row, m_col, iw_col, en_col, decay):
    b, _, s, _ = q_hm.shape
    nt = s // TE
    nc = s // ML_CHUNK

    def specs(rev):
        ti = (lambda i: nt - 1 - i) if rev else (lambda i: i)
        return [
            pl.BlockSpec((1, ML_HEADS, TE, ML_HD), lambda b, i: (b, 0, ti(i), 0)),
            pl.BlockSpec((1, ML_HEADS, ML_HD, TE), lambda b, i: (b, 0, 0, ti(i))),
            pl.BlockSpec((1, ML_HEADS, TE, 2 * ML_HD), lambda b, i: (b, 0, ti(i), 0)),
            pl.BlockSpec((1, 2 * ML_HEADS, TE), lambda b, i: (b, 0, ti(i))),
            pl.BlockSpec((1, 2 * ML_HEADS, TE), lambda b, i: (b, 0, ti(i))),
            pl.BlockSpec((1, TE, LANES), lambda b, i: (b, ti(i), 0)),
            pl.BlockSpec((1, TE, LANES), lambda b, i: (b, ti(i), 0)),
            pl.BlockSpec((1, TE, LANES), lambda b, i: (b, ti(i), 0)),
        ]

    args = [q_hm, kt_hm, v_aug, a_row, ws_row, m_col, iw_col, en_col]
    out_f = pl.BlockSpec((1, TE, MIX_W), lambda b, i: (b, i, 0))
    out_b = pl.BlockSpec((1, TE, MIX_W), lambda b, i: (b, nt - 1 - i, 0))
    return pl.pallas_call(
        _mlstm_kernel, name="mlstm_scan",
        grid=(b, nt),
        in_specs=specs(False) + specs(True) + [pl.BlockSpec((1, nc, LANES), lambda b, i: (b, 0, 0))],
        out_specs=(out_f, out_b),
        out_shape=(jax.ShapeDtypeStruct((b, s, MIX_W), F32),) * 2,
        scratch_shapes=[pltpu.VMEM((2 * ML_HEADS, ML_HD, 2 * ML_HD), F32)],
        compiler_params=_cparams(2),
    )(*args, *args, decay)


def _mlstm_branch(cqk, cv, ig, fg, conv_w, fbias):
    q_hm, kt_hm, v_aug = _mlprep_call(cqk, cv, conv_w)
    fb_row = jnp.pad(fbias.reshape(1, 2 * ML_HEADS), ((0, 0), (0, LANES - 2 * ML_HEADS)))
    a, ws, m_col, iw_col, en_col, decay = _gateprep_call(ig, fg, fb_row)
    rows = lambda t: jnp.transpose(t[..., :2 * ML_HEADS], (0, 2, 1))
    return _mlstm_call(q_hm, kt_hm, v_aug, rows(a), rows(ws), m_col, iw_col, en_col, decay)


TF = 256


def _merge_kernel(x_ref, ya_ref, o1_ref, o2_ref, o3_ref, l1_ref, l2_ref, l3_ref, hf_ref, hb_ref,
                  co_ref, yd_ref, wg_ref, bg_ref, wbr_ref, wout_ref, mng_ref, lng_ref, lnb_ref,
                  wr_ref, x1_ref, aff_ref, *, alpha):
    x = x_ref[0]
    xb = x.astype(BF16)
    l1, l2, l3 = l1_ref[0], l2_ref[0], l3_ref[0]
    lm = jnp.maximum(jnp.maximum(l1, l2), l3)
    e1, e2, e3 = jnp.exp(l1 - lm), jnp.exp(l2 - lm), jnp.exp(l3 - lm)
    y_b = (e1 * o1_ref[0] + e2 * o2_ref[0] + e3 * o3_ref[0]) / (e1 + e2 + e3)
    hsum = hf_ref[0] + hb_ref[0]
    lane_head = lax.broadcasted_iota(jnp.int32, (TF, MIX_W), 1) // ML_HD
    mu = jnp.zeros((TF, MIX_W), F32)
    for h in range(ML_HEADS):
        sel = lane_head == h
        mu = jnp.where(sel, jnp.sum(jnp.where(sel, hsum, 0.0), axis=-1, keepdims=True) / ML_HD, mu)
    cen = hsum - mu
    var = jnp.zeros((TF, MIX_W), F32)
    for h in range(ML_HEADS):
        sel = lane_head == h
        var = jnp.where(sel, jnp.sum(jnp.where(sel, cen * cen, 0.0), axis=-1, keepdims=True) / ML_HD, var)
    y_c = jax.nn.sigmoid(co_ref[0]) * (cen * lax.rsqrt(var + LN_EPS) * mng_ref[...])
    ys = (ya_ref[0], y_b.astype(BF16), y_c.astype(BF16), yd_ref[0])
    merged = jnp.zeros((TF, D_MODEL), F32)
    for n in range(N_BRANCH):
        cols = slice(n * D_MODEL, (n + 1) * D_MODEL)
        gate = jax.nn.sigmoid(jnp.dot(xb, wg_ref[:, cols], preferred_element_type=F32) + bg_ref[:, cols])
        merged = merged + gate * jnp.dot(ys[n], wbr_ref[n], preferred_element_type=F32)
    mix = jnp.dot(merged.astype(BF16), wout_ref[...], preferred_element_type=F32)
    x1 = _standardize(alpha * x + mix) * lng_ref[...] + lnb_ref[...]
    x1_ref[0] = x1
    logits = lax.dot_general(wr_ref[...], x1.astype(BF16), (((1,), (1,)), ((), ())),
                             preferred_element_type=F32)
    ex = jnp.exp(logits - jnp.max(logits, axis=0, keepdims=True))
    aff_ref[0] = ex / jnp.sum(ex, axis=0, keepdims=True)


def _merge_call(x, ya, o_list, l_list, hf, hb, co, yd, wg, bg, wbr, wout, mng, lng, lnb, wr_t, alpha):
    b, s, _ = x.shape
    tok = lambda w: pl.BlockSpec((1, TF, w), lambda b, i: (b, i, 0))
    const = lambda shp: pl.BlockSpec(shp, lambda b, i: (0,) * len(shp))
    return pl.pallas_call(
        functools.partial(_merge_kernel, alpha=alpha), name="merge_ln_router",
        grid=(b, s // TF),
        in_specs=[tok(D_MODEL)] + [tok(MIX_W)] * 11
                 + [const((D_MODEL, N_BRANCH * D_MODEL)), const((1, N_BRANCH * D_MODEL)),
                    const((N_BRANCH, MIX_W, D_MODEL)), const((D_MODEL, D_MODEL)), const((1, MIX_W)),
                    const((1, D_MODEL)), const((1, D_MODEL)), const((N_EXPERTS, D_MODEL))],
        out_specs=(tok(D_MODEL), pl.BlockSpec((1, N_EXPERTS, TF), lambda b, i: (b, 0, i))),
        out_shape=(jax.ShapeDtypeStruct((b, s, D_MODEL), F32),
                   jax.ShapeDtypeStruct((b, N_EXPERTS, s), F32)),
        compiler_params=_cparams(2),
    )(x, ya, *o_list, *l_list, hf, hb, co, yd, wg, bg, wbr, wout, mng, lng, lnb, wr_t)


TT = 256


def _select_kernel(aff_ref, slot_ref, tstart_ref, *, cap):
    s = aff_ref.shape[2]
    bits = pltpu.bitcast(aff_ref[0], jnp.int32)

    def bit_step(i, thr):
        cand = thr | jnp.left_shift(jnp.int32(1), 30 - i)
        cnt = jnp.sum((bits >= cand).astype(jnp.int32), axis=1, keepdims=True)
        return jnp.where(cnt >= cap, cand, thr)

    thr = lax.fori_loop(0, 31, bit_step, jnp.zeros((N_EXPERTS, 1), jnp.int32))
    gt = bits > thr
    eq = bits == thr
    need = (cap - jnp.sum(gt.astype(jnp.int32), axis=1, keepdims=True)).astype(F32)
    upper = (lax.broadcasted_iota(jnp.int32, (TT, TT), 0)
             <= lax.broadcasted_iota(jnp.int32, (TT, TT), 1)).astype(BF16)
    eq_before = jnp.zeros((N_EXPERTS, 1), F32)
    sel_before = jnp.zeros((N_EXPERTS, 1), F32)
    for j in range(s // TT):
        cols = slice(j * TT, (j + 1) * TT)
        eq_j = eq[:, cols]
        eq_incl = eq_before + jnp.dot(eq_j.astype(BF16), upper, preferred_element_type=F32)
        sel_j = gt[:, cols] | (eq_j & (eq_incl <= need))
        sel_f = sel_j.astype(F32)
        sel_incl = sel_before + jnp.dot(sel_f.astype(BF16), upper, preferred_element_type=F32)
        slot_ref[0, :, cols] = jnp.where(sel_j, sel_incl - 1.0, -1.0).astype(jnp.int32)
        tstart_ref[0, :, j:j + 1] = sel_before.astype(jnp.int32)
        eq_before = eq_incl[:, TT - 1:TT]
        sel_before = sel_incl[:, TT - 1:TT]


def _select_call(aff_t, cap):
    b, e, s = aff_t.shape
    return pl.pallas_call(
        functools.partial(_select_kernel, cap=cap), name="expert_choice_select",
        grid=(b,),
        in_specs=[pl.BlockSpec((1, e, s), lambda i: (i, 0, 0))],
        out_specs=(pl.BlockSpec((1, e, s), lambda i: (i, 0, 0)),
                   pl.BlockSpec((1, e, s // TT), lambda i: (i, 0, 0))),
        out_shape=(jax.ShapeDtypeStruct((b, e, s), jnp.int32),
                   jax.ShapeDtypeStruct((b, e, s // TT), jnp.int32)),
        compiler_params=_cparams(1),
    )(aff_t)


CB = 256


def _tile_range(tstart_ref, base, n_tiles, c0, cap):
    def body(t, carry):
        lo, hi = carry
        start = tstart_ref[base + t]
        end = jnp.where(t + 1 < n_tiles, tstart_ref[base + jnp.minimum(t + 1, n_tiles - 1)], cap)
        lo = lo + (end <= c0).astype(jnp.int32)
        hi = hi + (start < c0 + CB).astype(jnp.int32)
        return lo, hi
    return lax.fori_loop(0, n_tiles, body, (jnp.int32(0), jnp.int32(0)))


SC_LANES = 16
SC_ROWS = 64
CF = 512


def _sc_dispatch_call(x_flat, slot2, aff2, seq, cap):
    n_pairs = slot2.shape[0]
    d = x_flat.shape[1]
    info = plsc.get_sparse_core_info()
    n_workers = info.num_cores * info.num_subcores
    assert n_pairs % n_workers == 0 and seq % SC_LANES == 0 and cap % SC_ROWS == 0
    pairs_per_worker = n_pairs // n_workers
    mesh = plsc.VectorSubcoreMesh(core_axis_name="c", subcore_axis_name="s")

    @functools.partial(
        pl.kernel, mesh=mesh, name="expert_dispatch_sc",
        compiler_params=pltpu.CompilerParams(needs_layout_passes=False),
        out_type=(jax.ShapeDtypeStruct((n_pairs * cap, d), F32),
                  jax.ShapeDtypeStruct((n_pairs, cap), jnp.int32),
                  jax.ShapeDtypeStruct((n_pairs, cap), F32)),
        scratch_types=[pltpu.VMEM((seq,), jnp.int32), pltpu.VMEM((seq,), F32),
                       pltpu.VMEM((cap,), jnp.int32), pltpu.VMEM((cap,), F32),
                       pltpu.VMEM((SC_ROWS, d), F32), pltpu.SemaphoreType.DMA])
    def dispatch(x_hbm, slot_hbm, aff_hbm, xs_hbm, idx_hbm, gate_hbm,
                 slot_v, aff_v, idx_v, gate_v, rows_v, sem):
        worker = lax.axis_index("s") * info.num_cores + lax.axis_index("c")
        lane = lax.iota(jnp.int32, SC_LANES)
        for k in range(pairs_per_worker):
            pair = worker * pairs_per_worker + k
            row0 = (pair // N_EXPERTS) * seq
            pltpu.sync_copy(slot_hbm.at[pair], slot_v)
            pltpu.sync_copy(aff_hbm.at[pair], aff_v)

            @pl.loop(0, seq, step=SC_LANES)
            def _(t0):
                sv = slot_v[pl.ds(t0, SC_LANES)]
                picked = sv >= 0
                plsc.store_scatter(idx_v, [sv], row0 + t0 + lane, mask=picked)
                plsc.store_scatter(gate_v, [sv], aff_v[pl.ds(t0, SC_LANES)], mask=picked)

            pltpu.sync_copy(idx_v, idx_hbm.at[pair])
            pltpu.sync_copy(gate_v, gate_hbm.at[pair])

            @pl.loop(0, cap, step=SC_ROWS)
            def _(c0):
                pltpu.async_copy(x_hbm.at[idx_v.at[pl.ds(c0, SC_ROWS)]], rows_v, sem).wait()
                pltpu.sync_copy(rows_v, xs_hbm.at[pl.ds(pair * cap + c0, SC_ROWS)])

    return dispatch(x_flat, slot2, aff2)


def _expert_kernel(xs_ref, g_ref, w1_ref, w3_ref, w2_ref, hi_ref, lo_ref):
    xs = xs_ref[0, 0].astype(BF16)
    hid = (jax.nn.silu(jnp.dot(xs, w1_ref[0], preferred_element_type=F32))
           * jnp.dot(xs, w3_ref[0], preferred_element_type=F32))
    ye = jnp.dot(hid.astype(BF16), w2_ref[0], preferred_element_type=F32) * g_ref[0, 0]
    hi = ye.astype(BF16)
    hi_ref[0, 0] = hi
    lo_ref[0, 0] = (ye - hi.astype(F32)).astype(BF16)


def _expert_call(xs4, gate4, w1, w3, w2):
    b, e, cap, d = xs4.shape
    ff = w1.shape[2]
    rows = lambda w: pl.BlockSpec((1, 1, CF, w), lambda e, b, j: (b, e, j, 0))
    wspec = lambda r, c: pl.BlockSpec((1, r, c), lambda e, b, j: (e, 0, 0))
    return pl.pallas_call(
        _expert_kernel, name="expert_ffn",
        grid=(e, b, cap // CF),
        in_specs=[rows(d), rows(1), wspec(d, ff), wspec(d, ff), wspec(ff, d)],
        out_specs=(rows(d), rows(d)),
        out_shape=(jax.ShapeDtypeStruct((b, e, cap, d), BF16),) * 2,
        compiler_params=_cparams(3),
    )(xs4, gate4, w1, w3, w2)


DQ = 512


def _combine_kernel(tstart_ref, slot_ref, hi_ref, lo_ref, out_ref, *, cap):
    b, e, j = pl.program_id(0), pl.program_id(2), pl.program_id(3)
    n_tiles = slot_ref.shape[2]
    c0 = j * CB

    @pl.when((e == 0) & (j == 0))
    def _():
        out_ref[...] = jnp.zeros(out_ref.shape, F32)

    lo_t, hi_t = _tile_range(tstart_ref, (b * N_EXPERTS + e) * n_tiles, n_tiles, c0, cap)
    slot_iota = c0 + lax.broadcasted_iota(jnp.int32, (CB, TT), 0)
    tdims = (((0,), (0,)), ((), ()))

    def scatter_tile(t, carry):
        onehot = jnp.where(slot_ref[0, 0, pl.ds(t, 1), :] == slot_iota, 1.0, 0.0).astype(BF16)
        rows = pl.ds(pl.multiple_of(t * TT, TT), TT)
        out_ref[0, rows, :] += (
            lax.dot_general(onehot, hi_ref[0, 0], tdims, preferred_element_type=F32)
            + lax.dot_general(onehot, lo_ref[0, 0], tdims, preferred_element_type=F32))
        return carry

    lax.fori_loop(lo_t, hi_t, scatter_tile, 0)


def _combine_call(tstart_flat, slot4, ye_hi, ye_lo, s, cap):
    b, e, _, d = ye_hi.shape
    nt = s // TT
    grid_spec = pltpu.PrefetchScalarGridSpec(
        num_scalar_prefetch=1,
        grid=(b, d // DQ, e, cap // CB),
        in_specs=[pl.BlockSpec((1, 1, nt, TT), lambda b, q, e, j, ts: (b, e, 0, 0)),
                  pl.BlockSpec((1, 1, CB, DQ), lambda b, q, e, j, ts: (b, e, j, q)),
                  pl.BlockSpec((1, 1, CB, DQ), lambda b, q, e, j, ts: (b, e, j, q))],
        out_specs=pl.BlockSpec((1, s, DQ), lambda b, q, e, j, ts: (b, 0, q)))
    return pl.pallas_call(
        functools.partial(_combine_kernel, cap=cap), name="expert_combine",
        grid_spec=grid_spec,
        out_shape=jax.ShapeDtypeStruct((b, s, d), F32),
        compiler_params=_cparams(4),
    )(tstart_flat, slot4, ye_hi, ye_lo)


TN = 512


def _resln_kernel(x_ref, y_ref, g_ref, b_ref, o_ref, *, alpha):
    o_ref[...] = _standardize(alpha * x_ref[...] + y_ref[...]) * g_ref[...] + b_ref[...]


def _resln_call(x2d, y2d, g, bta, alpha):
    n, d = x2d.shape
    tok = pl.BlockSpec((TN, d), lambda i: (i, 0))
    vec = pl.BlockSpec((1, d), lambda i: (0, 0))
    return pl.pallas_call(
        functools.partial(_resln_kernel, alpha=alpha), name="residual_layernorm",
        grid=(n // TN,), in_specs=[tok, tok, vec, vec], out_specs=tok,
        out_shape=jax.ShapeDtypeStruct((n, d), F32),
        compiler_params=_cparams(1),
    )(x2d, y2d, g, bta)


def _expert_choice_ffn(x1, aff_t, w1, w3, w2):
    b, s, d = x1.shape
    cap = EC_FACTOR * s // N_EXPERTS
    nt = s // TT
    slot, tstart = _select_call(aff_t, cap)
    slot4 = slot.reshape(b, N_EXPERTS, nt, TT)
    ts_flat = tstart.reshape(-1)
    xs, _, gate = _sc_dispatch_call(x1.reshape(b * s, d), slot.reshape(b * N_EXPERTS, s),
                                    aff_t.reshape(b * N_EXPERTS, s), s, cap)
    ye_hi, ye_lo = _expert_call(xs.reshape(b, N_EXPERTS, cap, d), gate.reshape(b, N_EXPERTS, cap, 1),
                                w1, w3, w2)
    return _combine_call(ts_flat, slot4, ye_hi, ye_lo, s, cap)


def _pack_pool(pool_w):
    g, gd, _ = pool_w.shape
    out = jnp.zeros((g * gd, g * gd), F32)
    for i in range(g):
        out = out.at[i * gd:(i + 1) * gd, i * gd:(i + 1) * gd].set(pool_w[i])
    return out.astype(BF16)


def _layer(x, alpha, bias_tiles, w_in, b_in, gm_ln_g, gm_ws, gm_bs, ml_conv, ml_fbias, ml_norm_g,
           pool_w, pool_scale, w_branch, w_out, ln1_g, ln1_b, w_router, w_e1, w_e3, w_e2, ln2_g, ln2_b):
    b, s, d = x.shape
    n_small = 2576
    w_cat, b_cat = _pack_inproj_weights(w_in, b_in)
    wscat, bsfull = _pack_gmlp(gm_ws, gm_bs)
    ya, qkv, cqk, cv, co, dx, ig, fg = _inproj_call(x.reshape(b * s, d), w_cat, b_cat, gm_ln_g[None],
                                                     wscat, bsfull)
    r3 = lambda t: t.reshape(b, s, t.shape[-1])
    qkv3 = r3(qkv)
    o_list, l_list = [], []
    for (window, dil), bias in zip(DIL_PATTERNS, bias_tiles):
        o, lse = _attn_call(_to_residues(qkv3, dil), bias)
        o_list.append(_from_residues(o, dil, b))
        l_list.append(_from_residues(lse, dil, b))
    hf, hb = _mlstm_branch(r3(cqk), r3(cv), r3(ig), r3(fg), ml_conv, ml_fbias)
    yd = _pool_call(r3(dx), _pack_pool(pool_w), pool_scale[None])
    x1, aff_t = _merge_call(
        x, r3(ya), o_list, l_list, hf, hb, r3(co), yd,
        w_in[:, n_small:].astype(BF16), b_in[None, n_small:], w_branch.astype(BF16), w_out.astype(BF16),
        ml_norm_g[None], ln1_g[None], ln1_b[None], jnp.transpose(w_router).astype(BF16), alpha)
    ffn = _expert_choice_ffn(x1, aff_t, w_e1.astype(BF16), w_e3.astype(BF16), w_e2.astype(BF16))
    x2 = _resln_call(x1.reshape(b * s, d), ffn.reshape(b * s, d), ln2_g[None], ln2_b[None], alpha)
    return x2.reshape(b, s, d)


def kernel(x, w_in, b_in, gm_ln_g, gm_ws, gm_bs, rel_bias, ml_conv, ml_fbias, ml_norm_g, pool_w,
           pool_scale, w_branch, w_out, ln1_g, ln1_b, w_router, w_e1, w_e3, w_e2, ln2_g, ln2_b):
    depth = w_in.shape[0]
    alpha = (2 * depth) ** 0.25
    bias_tiles = [_attn_bias_tile(rel_bias, window, dil) for window, dil in DIL_PATTERNS]
    for l in range(depth):
        x = _layer(x, alpha, bias_tiles, w_in[l], b_in[l], gm_ln_g[l], gm_ws[l], gm_bs[l], ml_conv[l],
                   ml_fbias[l], ml_norm_g[l], pool_w[l], pool_scale[l], w_branch[l], w_out[l],
                   ln1_g[l], ln1_b[l], w_router[l], w_e1[l], w_e3[l], w_e2[l], ln2_g[l], ln2_b[l])
    return x
```

```python
import functools
import math

import jax
import jax.numpy as jnp
import numpy as np
from jax import lax
from jax.experimental import pallas as pl
from jax.experimental.pallas import tpu as pltpu
from jax.experimental.pallas import tpu_sc as plsc

F32 = jnp.float32
BF16 = jnp.bfloat16

D_MODEL = 1024
MIX_W = 256
N_BRANCH = 4
GM_CHUNK = 128
GM_GROUPS = 4
ATT_HEADS = 4
ATT_HD = 64
DIL_PATTERNS = ((128, 1), (512, 4), (2048, 16))
ATT_BLOCK = 64
REL_BUCKETS = 32
REL_MAX_DIST = 1024
ML_HEADS = 4
ML_HD = 64
ML_CHUNK = 64
POOL_WINDOWS = (2, 4, 8, 16)
N_EXPERTS = 16
EXPERT_FF = 1024
EC_FACTOR = 2
LN_EPS = 1e-5
NEG_BIG = -1e30

V7X_VMEM_LIMIT = 56 * 1024 * 1024
LANES = 128
HALO = 8


def _cparams(n_grid, vmem=V7X_VMEM_LIMIT):
    return pltpu.CompilerParams(dimension_semantics=("arbitrary",) * n_grid,
                                vmem_limit_bytes=vmem)


def _standardize(xf):
    mu = jnp.mean(xf, axis=-1, keepdims=True)
    var = jnp.mean(jnp.square(xf - mu), axis=-1, keepdims=True)
    return (xf - mu) * lax.rsqrt(var + LN_EPS)


TA = 512
A_COLS = 2560 + 2 * LANES


def _inproj_kernel(x_ref, w_ref, b_ref, lng_ref, wscat_ref, bsfull_ref,
                   ya_ref, qkv_ref, cqk_ref, cv_ref, co_ref, dx_ref, ig_ref, fg_ref):
    xb = x_ref[...].astype(BF16)
    h = jnp.dot(xb, w_ref[...], preferred_element_type=F32) + b_ref[...]
    qkv_ref[...] = h[:, 512:1280].astype(BF16)
    cqk_ref[...] = h[:, 1280:1792]
    cv_ref[...] = h[:, 1792:2048].astype(BF16)
    co_ref[...] = h[:, 2048:2304]
    dx_ref[...] = h[:, 2304:2560]
    ig_ref[...] = h[:, 2560:2688]
    fg_ref[...] = h[:, 2688:2816]
    u = jax.nn.gelu(h[:, 0:256])
    v = jax.nn.gelu(h[:, 256:512])
    vn = _standardize(v) * lng_ref[...]
    lane_grp = lax.broadcasted_iota(jnp.int32, (GM_CHUNK, MIX_W), 1) // (MIX_W // GM_GROUPS)
    for c in range(TA // GM_CHUNK):
        vc = vn[c * GM_CHUNK:(c + 1) * GM_CHUNK]
        stacked = jnp.concatenate(
            [jnp.where(lane_grp == g, vc, 0.0).astype(BF16) for g in range(GM_GROUPS)], axis=0)
        mixed = jnp.dot(wscat_ref[...], stacked, preferred_element_type=F32) + bsfull_ref[...]
        ya_ref[c * GM_CHUNK:(c + 1) * GM_CHUNK, :] = (
            u[c * GM_CHUNK:(c + 1) * GM_CHUNK] * mixed).astype(BF16)


def _inproj_call(x2d, w_cat, b_cat, lng, wscat, bsfull):
    n = x2d.shape[0]
    tok = lambda w: pl.BlockSpec((TA, w), lambda i: (i, 0))
    const = lambda s: pl.BlockSpec(s, lambda i: (0,) * len(s))
    out_shape = (
        jax.ShapeDtypeStruct((n, 256), BF16),
        jax.ShapeDtypeStruct((n, 768), BF16),
        jax.ShapeDtypeStruct((n, 512), F32),
        jax.ShapeDtypeStruct((n, 256), BF16),
        jax.ShapeDtypeStruct((n, 256), F32),
        jax.ShapeDtypeStruct((n, 256), F32),
        jax.ShapeDtypeStruct((n, LANES), F32),
        jax.ShapeDtypeStruct((n, LANES), F32),
    )
    return pl.pallas_call(
        _inproj_kernel, name="inproj_gmlp",
        grid=(n // TA,),
        in_specs=[tok(D_MODEL), const((D_MODEL, A_COLS)), const((1, A_COLS)), const((1, MIX_W)),
                  const((GM_CHUNK, GM_GROUPS * GM_CHUNK)), const((GM_CHUNK, MIX_W))],
        out_specs=(tok(256), tok(768), tok(512), tok(256), tok(256), tok(256), tok(LANES), tok(LANES)),
        out_shape=out_shape,
        compiler_params=_cparams(1),
    )(x2d, w_cat, b_cat, lng, wscat, bsfull)


def _pack_inproj_weights(w_in, b_in):
    pad = lambda a: jnp.pad(a, ((0, 0), (0, LANES - 8)))
    w_cat = jnp.concatenate([w_in[:, 0:2304], w_in[:, 2320:2576],
                             pad(w_in[:, 2304:2312]), pad(w_in[:, 2312:2320])], axis=1)
    b2 = b_in[None, :]
    b_cat = jnp.concatenate([b2[:, 0:2304], b2[:, 2320:2576],
                             pad(b2[:, 2304:2312]), pad(b2[:, 2312:2320])], axis=1)
    return w_cat.astype(BF16), b_cat


def _pack_gmlp(gm_ws, gm_bs):
    wscat = jnp.transpose(gm_ws, (1, 0, 2)).reshape(GM_CHUNK, GM_GROUPS * GM_CHUNK).astype(BF16)
    bsfull = jnp.repeat(jnp.transpose(gm_bs), MIX_W // GM_GROUPS, axis=1)
    return wscat, bsfull


def _halo_specs(t, width, n_tiles):
    r = t // HALO
    main = pl.BlockSpec((1, t, width), lambda b, i: (b, i, 0))
    prev = pl.BlockSpec((1, HALO, width), lambda b, i: (b, jnp.maximum(i * r - 1, 0), 0))
    nxt = pl.BlockSpec((1, HALO, width), lambda b, i: (b, jnp.minimum((i + 1) * r, n_tiles * r - 1), 0))
    return main, prev, nxt


def _fill_halo_scratch(buf, x_ref, p_ref, n_ref, t):
    i = pl.program_id(1)
    last = pl.num_programs(1) - 1
    buf[0:HALO, :] = jnp.where(i > 0, p_ref[0], 0.0)
    buf[HALO:HALO + t, :] = x_ref[0]
    buf[HALO + t:2 * HALO + t, :] = jnp.where(i < last, n_ref[0], 0.0)


TP = 512


def _pool_kernel(x_ref, p_ref, n_ref, w_ref, sc_ref, o_ref, buf):
    _fill_halo_scratch(buf, x_ref, p_ref, n_ref, TP)
    seq = pl.num_programs(1) * TP
    pos = pl.program_id(1) * TP + lax.broadcasted_iota(jnp.int32, (TP, 1), 0)
    lane_grp = lax.broadcasted_iota(jnp.int32, (TP, MIX_W), 1) // (MIX_W // len(POOL_WINDOWS))
    x0 = buf[HALO:HALO + TP, :]
    pooled = jnp.zeros((TP, MIX_W), F32)
    acc = None
    half_done = 0
    for gi, win in enumerate(POOL_WINDOWS):
        half = win // 2
        for o in list(range(-half, -half_done)) + list(range(half_done, half)):
            term = buf[HALO + o:HALO + o + TP, :]
            acc = term if acc is None else acc + term
        half_done = half
        cnt = (jnp.minimum(pos + half, seq) - jnp.maximum(pos - half, 0)).astype(F32)
        pooled = jnp.where(lane_grp == gi, acc / cnt - x0, pooled)
    mixed = jnp.dot(pooled.astype(BF16), w_ref[...], preferred_element_type=F32)
    o_ref[0] = (mixed * sc_ref[...]).astype(BF16)


def _pool_call(dx, w_block, scale):
    b, s, _ = dx.shape
    nt = s // TP
    main, prev, nxt = _halo_specs(TP, MIX_W, nt)
    return pl.pallas_call(
        _pool_kernel, name="pool_mixer",
        grid=(b, nt),
        in_specs=[main, prev, nxt,
                  pl.BlockSpec((MIX_W, MIX_W), lambda b, i: (0, 0)),
                  pl.BlockSpec((1, MIX_W), lambda b, i: (0, 0))],
        out_specs=pl.BlockSpec((1, TP, MIX_W), lambda b, i: (b, i, 0)),
        out_shape=jax.ShapeDtypeStruct((b, s, MIX_W), BF16),
        scratch_shapes=[pltpu.VMEM((TP + 2 * HALO, MIX_W), F32)],
        compiler_params=_cparams(2),
    )(dx, dx, dx, w_block, scale)


TQ = 128
TKEYS = TQ + 2 * ATT_BLOCK


def _attn_kernel(q_ref, kp_ref, km_ref, kn_ref, vp_ref, vm_ref, vn_ref, bias_ref, o_ref, lse_ref):
    i = pl.program_id(1)
    seq = pl.num_programs(1) * TQ
    kpos = i * TQ - ATT_BLOCK + lax.broadcasted_iota(jnp.int32, (1, TKEYS), 1)
    kvalid = (kpos >= 0) & (kpos < seq)
    q = q_ref[0]
    k = jnp.concatenate([kp_ref[0], km_ref[0], kn_ref[0]], axis=0)
    v = jnp.concatenate([vp_ref[0], vm_ref[0], vn_ref[0]], axis=0)
    for h in range(ATT_HEADS):
        sl = slice(h * ATT_HD, (h + 1) * ATT_HD)
        logits = lax.dot_general(q[:, sl], k[:, sl], (((1,), (1,)), ((), ())),
                                 preferred_element_type=F32) * ATT_HD ** -0.5 + bias_ref[h]
        logits = jnp.where(kvalid, logits, NEG_BIG)
        m = jnp.max(logits, axis=-1, keepdims=True)
        p = jnp.exp(logits - m)
        ssum = jnp.sum(p, axis=-1, keepdims=True)
        o = jnp.dot(p.astype(BF16), v[:, sl], preferred_element_type=F32) / ssum
        o_ref[0, :, sl] = o
        lse_ref[0, :, sl] = jnp.broadcast_to(m + jnp.log(ssum), (TQ, ATT_HD))


def _attn_call(qkv, bias):
    n, l, _ = qkv.shape
    nt = l // TQ
    r = TQ // ATT_BLOCK
    main = lambda c: pl.BlockSpec((1, TQ, MIX_W), lambda b, i: (b, i, c))
    prev = lambda c: pl.BlockSpec((1, ATT_BLOCK, MIX_W), lambda b, i: (b, jnp.maximum(i * r - 1, 0), c))
    nxt = lambda c: pl.BlockSpec((1, ATT_BLOCK, MIX_W),
                                 lambda b, i: (b, jnp.minimum((i + 1) * r, nt * r - 1), c))
    out = pl.BlockSpec((1, TQ, MIX_W), lambda b, i: (b, i, 0))
    return pl.pallas_call(
        _attn_kernel, name="band_attention",
        grid=(n, nt),
        in_specs=[main(0), prev(1), main(1), nxt(1), prev(2), main(2), nxt(2),
                  pl.BlockSpec((ATT_HEADS, TQ, TKEYS), lambda b, i: (0, 0, 0))],
        out_specs=(out, out),
        out_shape=(jax.ShapeDtypeStruct((n, l, MIX_W), F32), jax.ShapeDtypeStruct((n, l, MIX_W), F32)),
        compiler_params=_cparams(2),
    )(qkv, qkv, qkv, qkv, qkv, qkv, qkv, bias)


def _t5_bucket(rel):
    half = REL_BUCKETS // 2
    max_exact = half // 2
    ret = jnp.where(rel > 0, half, 0)
    n = jnp.abs(rel)
    nf = jnp.maximum(n, 1).astype(F32)
    large = max_exact + (jnp.log(nf / max_exact) / math.log(REL_MAX_DIST / max_exact)
                         * (half - max_exact)).astype(jnp.int32)
    large = jnp.minimum(large, half - 1)
    return ret + jnp.where(n < max_exact, n, large)


def _attn_bias_tile(rel_bias, window, dil):
    side = (window // 2) // dil
    rel = jnp.arange(TKEYS)[None, :] - ATT_BLOCK - jnp.arange(TQ)[:, None]
    bias = jnp.transpose(rel_bias[_t5_bucket(dil * rel)], (2, 0, 1)).astype(F32)
    return jnp.where((jnp.abs(rel) <= side)[None], bias, NEG_BIG)


def _to_residues(t, dil):
    b, s, c = t.shape
    return t.reshape(b, s // dil, dil, c).transpose(0, 2, 1, 3).reshape(b * dil, s // dil, c)


def _from_residues(t, dil, b):
    n, l, c = t.shape
    return t.reshape(b, dil, l, c).transpose(0, 2, 1, 3).reshape(b, l * dil, c)


TM = 512


def _mlprep_kernel(x_ref, p_ref, n_ref, v_ref, w_ref, q_out, kt_out, v_out, buf):
    _fill_halo_scratch(buf, x_ref, p_ref, n_ref, TM)
    conv = (buf[HALO - 1:HALO - 1 + TM, :] * w_ref[0:1, :] + buf[HALO:HALO + TM, :] * w_ref[1:2, :]
            + buf[HALO + 1:HALO + 1 + TM, :] * w_ref[2:3, :])
    qk = jax.nn.silu(conv)
    kt = jnp.transpose(qk[:, MIX_W:] * ML_HD ** -0.5)
    v = v_ref[0]
    ones_col = jnp.where(lax.broadcasted_iota(jnp.int32, (TM, ML_HD), 1) == 0, 1.0, 0.0).astype(BF16)
    for h in range(ML_HEADS):
        sl = slice(h * ML_HD, (h + 1) * ML_HD)
        q_out[0, h] = qk[:, sl].astype(BF16)
        kt_out[0, h] = kt[sl, :].astype(BF16)
        v_out[0, h] = jnp.concatenate([v[:, sl], ones_col], axis=1)


def _mlprep_call(cqk, cv, conv_w):
    b, s, _ = cqk.shape
    nt = s // TM
    main, prev, nxt = _halo_specs(TM, 2 * MIX_W, nt)
    return pl.pallas_call(
        _mlprep_kernel, name="mlstm_prep",
        grid=(b, nt),
        in_specs=[main, prev, nxt,
                  pl.BlockSpec((1, TM, MIX_W), lambda b, i: (b, i, 0)),
                  pl.BlockSpec((3, 2 * MIX_W), lambda b, i: (0, 0))],
        out_specs=(pl.BlockSpec((1, ML_HEADS, TM, ML_HD), lambda b, i: (b, 0, i, 0)),
                   pl.BlockSpec((1, ML_HEADS, ML_HD, TM), lambda b, i: (b, 0, 0, i)),
                   pl.BlockSpec((1, ML_HEADS, TM, 2 * ML_HD), lambda b, i: (b, 0, i, 0))),
        out_shape=(jax.ShapeDtypeStruct((b, ML_HEADS, s, ML_HD), BF16),
                   jax.ShapeDtypeStruct((b, ML_HEADS, ML_HD, s), BF16),
                   jax.ShapeDtypeStruct((b, ML_HEADS, s, 2 * ML_HD), BF16)),
        scratch_shapes=[pltpu.VMEM((TM + 2 * HALO, 2 * MIX_W), F32)],
        compiler_params=_cparams(2),
    )(cqk, cqk, cqk, cv, conv_w)


GP_BLK = 512


def _chunk_scan(x, fwd_lane, t_in_chunk, op, ident):
    n = x.shape[0]
    shift = 1
    while shift < ML_CHUNK:
        down = pltpu.roll(x, shift, 0)
        up = pltpu.roll(x, n - shift, 0)
        nb = jnp.where(fwd_lane,
                       jnp.where(t_in_chunk >= shift, down, ident),
                       jnp.where(t_in_chunk < ML_CHUNK - shift, up, ident))
        x = op(x, nb)
        shift *= 2
    return x


def _gatescan_kernel(ig_ref, fg_ref, fb_ref, b_ref, a_ref, cm_ref, g_ref, amax_ref):
    lane = lax.broadcasted_iota(jnp.int32, (1, LANES), 1)
    fwd_lane = lane < ML_HEADS
    t_in_chunk = lax.broadcasted_iota(jnp.int32, (GP_BLK, 1), 0) % ML_CHUNK
    z = fg_ref[0] + fb_ref[...]
    lf = jnp.minimum(z, 0.0) - jnp.log1p(jnp.exp(-jnp.abs(z)))
    b = _chunk_scan(lf, fwd_lane, t_in_chunk, jnp.add, 0.0)
    a = ig_ref[0] - b
    b_ref[0] = b
    a_ref[0] = a
    cm_ref[0] = _chunk_scan(a, fwd_lane, t_in_chunk, jnp.maximum, -jnp.inf)
    cpb = GP_BLK // ML_CHUNK
    last = pl.ds(ML_CHUNK - 1, cpb, stride=ML_CHUNK)
    first = pl.ds(0, cpb, stride=ML_CHUNK)
    g_ref[0] = jnp.where(fwd_lane, b_ref[0, last, :], b_ref[0, first, :])
    amax_ref[0] = jnp.where(fwd_lane, cm_ref[0, last, :], cm_ref[0, first, :])


def _gateout_kernel(b_ref, a_ref, cm_ref, g_ref, amax_ref, ws_ref, m_ref, iw_ref, en_ref, dec_ref,
                    mch_s, mlast_s):
    nc = g_ref.shape[1]
    j = pl.program_id(1)
    fwd_lane = lax.broadcasted_iota(jnp.int32, (1, LANES), 1) < ML_HEADS

    @pl.when(j == 0)
    def _():
        def m_step(i, carry):
            mf, mb = carry
            cf = pl.ds(i, 1)
            cb = pl.ds(nc - 1 - i, 1)
            mch_s[cf, :] = jnp.where(fwd_lane, mf, mch_s[cf, :])
            mch_s[cb, :] = jnp.where(fwd_lane, mch_s[cb, :], mb)
            mf = g_ref[0, cf, :] + jnp.maximum(mf, amax_ref[0, cf, :])
            mb = g_ref[0, cb, :] + jnp.maximum(mb, amax_ref[0, cb, :])
            return mf, mb

        mch_s[...] = jnp.zeros((nc, LANES), F32)
        zero = jnp.zeros((1, LANES), F32)
        lax.fori_loop(0, nc, m_step, (zero, zero))
        mlast = jnp.maximum(amax_ref[0], mch_s[...])
        mlast_s[...] = mlast
        dec_ref[0] = jnp.exp(mch_s[...] - mlast)

    cpb = GP_BLK // ML_CHUNK
    crow = pl.ds(pl.multiple_of(j * cpb, cpb), cpb)
    expand = lambda t: jnp.broadcast_to(t[:, None, :], (cpb, ML_CHUNK, LANES)).reshape(GP_BLK, LANES)
    m_tok = expand(mch_s[crow, :])
    mlast_tok = expand(mlast_s[crow, :])
    mt = jnp.maximum(cm_ref[0], m_tok)
    m_ref[0] = mt
    iw_ref[0] = jnp.exp(m_tok - mt)
    en_ref[0] = jnp.exp(-(b_ref[0] + mt))
    ws_ref[0] = jnp.exp(a_ref[0] - mlast_tok)


def _gateprep_call(ig, fg, fbias_row):
    b, s, _ = ig.shape
    nc = s // ML_CHUNK
    cpb = GP_BLK // ML_CHUNK
    tok = pl.BlockSpec((1, GP_BLK, LANES), lambda i, j: (i, j, 0))
    tok_shape = jax.ShapeDtypeStruct((b, s, LANES), F32)
    chunk_shape = jax.ShapeDtypeStruct((b, nc, LANES), F32)
    chunk_tile = pl.BlockSpec((1, cpb, LANES), lambda i, j: (i, j, 0))
    chunk_all = pl.BlockSpec((1, nc, LANES), lambda i, j: (i, 0, 0))
    bcum, a, cm, g, amax = pl.pallas_call(
        _gatescan_kernel, name="mlstm_gate_scan",
        grid=(b, s // GP_BLK),
        in_specs=[tok, tok, pl.BlockSpec((1, LANES), lambda i, j: (0, 0))],
        out_specs=(tok, tok, tok, chunk_tile, chunk_tile),
        out_shape=(tok_shape,) * 3 + (chunk_shape,) * 2,
        compiler_params=_cparams(2),
    )(ig, fg, fbias_row)
    ws, m_col, iw_col, en_col, decay = pl.pallas_call(
        _gateout_kernel, name="mlstm_gate_out",
        grid=(b, s // GP_BLK),
        in_specs=[tok, tok, tok, chunk_all, chunk_all],
        out_specs=(tok, tok, tok, tok, chunk_all),
        out_shape=(tok_shape,) * 4 + (chunk_shape,),
        scratch_shapes=[pltpu.VMEM((nc, LANES), F32)] * 2,
        compiler_params=_cparams(2),
    )(bcum, a, cm, g, amax)
    return a, ws, m_col, iw_col, en_col, decay


TE = 512
PAIR = 2 * ML_CHUNK


def _mlstm_chunk(state, d, h, c, qc, ktc, vc, a_row, ws_row, m_col, iw_col, en_col, decay):
    t_idx = lax.broadcasted_iota(jnp.int32, (ML_CHUNK, ML_CHUNK), 0)
    s_idx = lax.broadcasted_iota(jnp.int32, (ML_CHUNK, ML_CHUNK), 1)
    tri = (s_idx <= t_idx) if d == 0 else (s_idx >= t_idx)
    s = jnp.dot(qc, ktc, preferred_element_type=F32)
    w = jnp.exp(jnp.where(tri, a_row - m_col, NEG_BIG))
    sw = s * w
    den_intra = jnp.sum(sw, axis=-1, keepdims=True)
    cst = state[d * ML_HEADS + h]
    intra = jnp.dot(sw.astype(BF16), vc, preferred_element_type=F32)
    inter = jnp.dot(qc, cst.astype(BF16), preferred_element_type=F32)
    tot = intra + iw_col * inter
    den = den_intra + iw_col * inter[:, ML_HD:ML_HD + 1]
    hh = tot[:, :ML_HD] / jnp.maximum(jnp.abs(den), en_col)
    upd = jnp.dot((ktc.astype(F32) * ws_row).astype(BF16), vc, preferred_element_type=F32)
    state[d * ML_HEADS + h] = decay * cst + upd
    return hh


def _mlstm_kernel(*refs):
    (qf, ktf, vf, af, wsf, mf, iwf, enf, qb, ktb, vb, ab, wsb, mb, iwb, enb, dec_ref,
     hf_ref, hb_ref, state) = refs
    i = pl.program_id(1)
    nt = pl.num_programs(1)

    @pl.when(i == 0)
    def _():
        state[...] = jnp.zeros(state.shape, F32)

    n_pairs = TE // PAIR
    cpt = TE // ML_CHUNK
    dirs = ((0, qf, ktf, vf, af, wsf, mf, iwf, enf, hf_ref), (1, qb, ktb, vb, ab, wsb, mb, iwb, enb, hb_ref))

    def pair_body(p, carry):
        for d, q_r, kt_r, v_r, a_r, ws_r, m_r, iw_r, en_r, out_r in dirs:
            pp = p if d == 0 else n_pairs - 1 - p
            tile = i if d == 0 else nt - 1 - i
            lanes = pl.ds(pl.multiple_of(pp * PAIR, PAIR), PAIR)
            for half in ((0, 1) if d == 0 else (1, 0)):
                rows = pl.ds(pl.multiple_of(pp * PAIR + half * ML_CHUNK, ML_CHUNK), ML_CHUNK)
                chunk = tile * cpt + pp * 2 + half
                hs = slice(half * ML_CHUNK, (half + 1) * ML_CHUNK)
                for h in range(ML_HEADS):
                    ch = d * ML_HEADS + h
                    hh = _mlstm_chunk(
                        state, d, h, chunk,
                        q_r[0, h, rows, :], kt_r[0, h, :, lanes][:, hs], v_r[0, h, rows, :],
                        a_r[0, ch:ch + 1, lanes][:, hs], ws_r[0, ch:ch + 1, lanes][:, hs],
                        m_r[0, rows, ch:ch + 1], iw_r[0, rows, ch:ch + 1], en_r[0, rows, ch:ch + 1],
                        dec_ref[0, pl.ds(chunk, 1), ch:ch + 1])
                    out_r[0, rows, h * ML_HD:(h + 1) * ML_HD] = hh
        return carry

    lax.fori_loop(0, n_pairs, pair_body, 0)


def _mlstm_call(q_hm, kt_hm, v_aug, a_row, ws_row, m_col, iw_col, en_col, decay):
    b, _, s, _ = q_hm.shape
    nt = s // TE
    nc = s // ML_CHUNK

    def specs(rev):
        ti = (lambda i: nt - 1 - i) if rev else (lambda i: i)
        return [
            pl.BlockSpec((1, ML_HEADS, TE, ML_HD), lambda b, i: (b, 0, ti(i), 0)),
            pl.BlockSpec((1, ML_HEADS, ML_HD, TE), lambda b, i: (b, 0, 0, ti(i))),
            pl.BlockSpec((1, ML_HEADS, TE, 2 * ML_HD), lambda b, i: (b, 0, ti(i), 0)),
            pl.BlockSpec((1, 2 * ML_HEADS, TE), lambda b, i: (b, 0, ti(i))),
            pl.BlockSpec((1, 2 * ML_HEADS, TE), lambda b, i: (b, 0, ti(i))),
            pl.BlockSpec((1, TE, LANES), lambda b, i: (b, ti(i), 0)),
            pl.BlockSpec((1, TE, LANES), lambda b, i: (b, ti(i), 0)),
            pl.BlockSpec((1, TE, LANES), lambda b, i: (b, ti(i), 0)),
        ]

    args = [q_hm, kt_hm, v_aug, a_row, ws_row, m_col, iw_col, en_col]
    out_f = pl.BlockSpec((1, TE, MIX_W), lambda b, i: (b, i, 0))
    out_b = pl.BlockSpec((1, TE, MIX_W), lambda b, i: (b, nt - 1 - i, 0))
    return pl.pallas_call(
        _mlstm_kernel, name="mlstm_scan",
        grid=(b, nt),
        in_specs=specs(False) + specs(True) + [pl.BlockSpec((1, nc, LANES), lambda b, i: (b, 0, 0))],
        out_specs=(out_f, out_b),
        out_shape=(jax.ShapeDtypeStruct((b, s, MIX_W), F32),) * 2,
        scratch_shapes=[pltpu.VMEM((2 * ML_HEADS, ML_HD, 2 * ML_HD), F32)],
        compiler_params=_cparams(2),
    )(*args, *args, decay)


def _mlstm_branch(cqk, cv, ig, fg, conv_w, fbias):
    q_hm, kt_hm, v_aug = _mlprep_call(cqk, cv, conv_w)
    fb_row = jnp.pad(fbias.reshape(1, 2 * ML_HEADS), ((0, 0), (0, LANES - 2 * ML_HEADS)))
    a, ws, m_col, iw_col, en_col, decay = _gateprep_call(ig, fg, fb_row)
    rows = lambda t: jnp.transpose(t[..., :2 * ML_HEADS], (0, 2, 1))
    return _mlstm_call(q_hm, kt_hm, v_aug, rows(a), rows(ws), m_col, iw_col, en_col, decay)


TF = 256


def _merge_kernel(x_ref, ya_ref, o1_ref, o2_ref, o3_ref, l1_ref, l2_ref, l3_ref, hf_ref, hb_ref,
                  co_ref, yd_ref, wg_ref, bg_ref, wbr_ref, wout_ref, mng_ref, lng_ref, lnb_ref,
                  wr_ref, x1_ref, aff_ref, *, alpha):
    x = x_ref[0]
    xb = x.astype(BF16)
    l1, l2, l3 = l1_ref[0], l2_ref[0], l3_ref[0]
    lm = jnp.maximum(jnp.maximum(l1, l2), l3)
    e1, e2, e3 = jnp.exp(l1 - lm), jnp.exp(l2 - lm), jnp.exp(l3 - lm)
    y_b = (e1 * o1_ref[0] + e2 * o2_ref[0] + e3 * o3_ref[0]) / (e1 + e2 + e3)
    hsum = hf_ref[0] + hb_ref[0]
    lane_head = lax.broadcasted_iota(jnp.int32, (TF, MIX_W), 1) // ML_HD
    mu = jnp.zeros((TF, MIX_W), F32)
    for h in range(ML_HEADS):
        sel = lane_head == h
        mu = jnp.where(sel, jnp.sum(jnp.where(sel, hsum, 0.0), axis=-1, keepdims=True) / ML_HD, mu)
    cen = hsum - mu
    var = jnp.zeros((TF, MIX_W), F32)
    for h in range(ML_HEADS):
        sel = lane_head == h
        var = jnp.where(sel, jnp.sum(jnp.where(sel, cen * cen, 0.0), axis=-1, keepdims=True) / ML_HD, var)
    y_c = jax.nn.sigmoid(co_ref[0]) * (cen * lax.rsqrt(var + LN_EPS) * mng_ref[...])
    ys = (ya_ref[0], y_b.astype(BF16), y_c.astype(BF16), yd_ref[0])
    merged = jnp.zeros((TF, D_MODEL), F32)
    for n in range(N_BRANCH):
        cols = slice(n * D_MODEL, (n + 1) * D_MODEL)
        gate = jax.nn.sigmoid(jnp.dot(xb, wg_ref[:, cols], preferred_element_type=F32) + bg_ref[:, cols])
        merged = merged + gate * jnp.dot(ys[n], wbr_ref[n], preferred_element_type=F32)
    mix = jnp.dot(merged.astype(BF16), wout_ref[...], preferred_element_type=F32)
    x1 = _standardize(alpha * x + mix) * lng_ref[...] + lnb_ref[...]
    x1_ref[0] = x1
    logits = lax.dot_general(wr_ref[...], x1.astype(BF16), (((1,), (1,)), ((), ())),
                             preferred_element_type=F32)
    ex = jnp.exp(logits - jnp.max(logits, axis=0, keepdims=True))
    aff_ref[0] = ex / jnp.sum(ex, axis=0, keepdims=True)


def _merge_call(x, ya, o_list, l_list, hf, hb, co, yd, wg, bg, wbr, wout, mng, lng, lnb, wr_t, alpha):
    b, s, _ = x.shape
    tok = lambda w: pl.BlockSpec((1, TF, w), lambda b, i: (b, i, 0))
    const = lambda shp: pl.BlockSpec(shp, lambda b, i: (0,) * len(shp))
    return pl.pallas_call(
        functools.partial(_merge_kernel, alpha=alpha), name="merge_ln_router",
        grid=(b, s // TF),
        in_specs=[tok(D_MODEL)] + [tok(MIX_W)] * 11
                 + [const((D_MODEL, N_BRANCH * D_MODEL)), const((1, N_BRANCH * D_MODEL)),
                    const((N_BRANCH, MIX_W, D_MODEL)), const((D_MODEL, D_MODEL)), const((1, MIX_W)),
                    const((1, D_MODEL)), const((1, D_MODEL)), const((N_EXPERTS, D_MODEL))],
        out_specs=(tok(D_MODEL), pl.BlockSpec((1, N_EXPERTS, TF), lambda b, i: (b, 0, i))),
        out_shape=(jax.ShapeDtypeStruct((b, s, D_MODEL), F32),
                   jax.ShapeDtypeStruct((b, N_EXPERTS, s), F32)),
        compiler_params=_cparams(2),
    )(x, ya, *o_list, *l_list, hf, hb, co, yd, wg, bg, wbr, wout, mng, lng, lnb, wr_t)


TT = 256


def _select_kernel(aff_ref, slot_ref, *, cap):
    s = aff_ref.shape[2]
    bits = pltpu.bitcast(aff_ref[0], jnp.int32)

    def bit_step(i, thr):
        cand = thr | jnp.left_shift(jnp.int32(1), 30 - i)
        cnt = jnp.sum((bits >= cand).astype(jnp.int32), axis=1, keepdims=True)
        return jnp.where(cnt >= cap, cand, thr)

    thr = lax.fori_loop(0, 31, bit_step, jnp.zeros((N_EXPERTS, 1), jnp.int32))
    gt = bits > thr
    eq = bits == thr
    need = (cap - jnp.sum(gt.astype(jnp.int32), axis=1, keepdims=True)).astype(F32)
    upper = (lax.broadcasted_iota(jnp.int32, (TT, TT), 0)
             <= lax.broadcasted_iota(jnp.int32, (TT, TT), 1)).astype(BF16)
    eq_before = jnp.zeros((N_EXPERTS, 1), F32)
    sel_before = jnp.zeros((N_EXPERTS, 1), F32)
    for j in range(s // TT):
        cols = slice(j * TT, (j + 1) * TT)
        eq_j = eq[:, cols]
        eq_incl = eq_before + jnp.dot(eq_j.astype(BF16), upper, preferred_element_type=F32)
        sel_j = gt[:, cols] | (eq_j & (eq_incl <= need))
        sel_f = sel_j.astype(F32)
        sel_incl = sel_before + jnp.dot(sel_f.astype(BF16), upper, preferred_element_type=F32)
        slot_ref[0, :, cols] = jnp.where(sel_j, sel_incl - 1.0, -1.0).astype(jnp.int32)
        eq_before = eq_incl[:, TT - 1:TT]
        sel_before = sel_incl[:, TT - 1:TT]


def _select_call(aff_t, cap):
    b, e, s = aff_t.shape
    return pl.pallas_call(
        functools.partial(_select_kernel, cap=cap), name="expert_choice_select",
        grid=(b,),
        in_specs=[pl.BlockSpec((1, e, s), lambda i: (i, 0, 0))],
        out_specs=pl.BlockSpec((1, e, s), lambda i: (i, 0, 0)),
        out_shape=jax.ShapeDtypeStruct((b, e, s), jnp.int32),
        compiler_params=_cparams(1),
    )(aff_t)


SC_LANES = 16
SC_ROWS = 64
SC_IDX = 128
SC_SLAB = 128
SC_ZROWS = 64
CF = 512


def _sc_dispatch_call(x_flat, slot2, aff2, seq, cap):
    n_pairs = slot2.shape[0]
    d = x_flat.shape[1]
    info = plsc.get_sparse_core_info()
    n_workers = info.num_cores * info.num_subcores
    assert n_pairs % n_workers == 0 and seq % SC_LANES == 0 and cap % SC_ROWS == 0
    pairs_per_worker = n_pairs // n_workers
    mesh = plsc.VectorSubcoreMesh(core_axis_name="c", subcore_axis_name="s")

    @functools.partial(
        pl.kernel, mesh=mesh, name="expert_dispatch_sc",
        compiler_params=pltpu.CompilerParams(needs_layout_passes=False),
        out_type=(jax.ShapeDtypeStruct((n_pairs * cap, d), F32),
                  jax.ShapeDtypeStruct((n_pairs, cap), jnp.int32),
                  jax.ShapeDtypeStruct((n_pairs, cap), F32)),
        scratch_types=[pltpu.VMEM((seq,), jnp.int32), pltpu.VMEM((seq,), F32),
                       pltpu.VMEM((cap,), jnp.int32), pltpu.VMEM((cap,), jnp.int32),
                       pltpu.VMEM((cap,), F32),
                       pltpu.VMEM((SC_ROWS, d), F32), pltpu.SemaphoreType.DMA])
    def dispatch(x_hbm, slot_hbm, aff_hbm, xs_hbm, tok_hbm, gate_hbm,
                 slot_v, aff_v, idx_v, tok_v, gate_v, rows_v, sem):
        worker = lax.axis_index("s") * info.num_cores + lax.axis_index("c")
        lane = lax.iota(jnp.int32, SC_LANES)
        for k in range(pairs_per_worker):
            pair = worker * pairs_per_worker + k
            row0 = (pair // N_EXPERTS) * seq
            pltpu.sync_copy(slot_hbm.at[pair], slot_v)
            pltpu.sync_copy(aff_hbm.at[pair], aff_v)

            @pl.loop(0, seq, step=SC_LANES)
            def _(t0):
                sv = slot_v[pl.ds(t0, SC_LANES)]
                picked = sv >= 0
                plsc.store_scatter(tok_v, [sv], t0 + lane, mask=picked)
                plsc.store_scatter(idx_v, [sv], row0 + t0 + lane, mask=picked)
                plsc.store_scatter(gate_v, [sv], aff_v[pl.ds(t0, SC_LANES)], mask=picked)

            pltpu.sync_copy(tok_v, tok_hbm.at[pair])
            pltpu.sync_copy(gate_v, gate_hbm.at[pair])

            @pl.loop(0, cap, step=SC_ROWS)
            def _(c0):
                pltpu.async_copy(x_hbm.at[idx_v.at[pl.ds(c0, SC_ROWS)]], rows_v, sem).wait()
                pltpu.sync_copy(rows_v, xs_hbm.at[pl.ds(pair * cap + c0, SC_ROWS)])

    return dispatch(x_flat, slot2, aff2)


def _expert_kernel(xs_ref, g_ref, w1_ref, w3_ref, w2_ref, ye_ref):
    xs = xs_ref[0, 0].astype(BF16)
    hid = (jax.nn.silu(jnp.dot(xs, w1_ref[0], preferred_element_type=F32))
           * jnp.dot(xs, w3_ref[0], preferred_element_type=F32))
    ye_ref[0, 0] = jnp.dot(hid.astype(BF16), w2_ref[0], preferred_element_type=F32) * g_ref[0, 0]


def _expert_call(xs4, gate4, w1, w3, w2):
    b, e, cap, d = xs4.shape
    ff = w1.shape[2]
    rows = lambda w: pl.BlockSpec((1, 1, CF, w), lambda e, b, j: (b, e, j, 0))
    wspec = lambda r, c: pl.BlockSpec((1, r, c), lambda e, b, j: (e, 0, 0))
    return pl.pallas_call(
        _expert_kernel, name="expert_ffn",
        grid=(e, b, cap // CF),
        in_specs=[rows(d), rows(1), wspec(d, ff), wspec(d, ff), wspec(ff, d)],
        out_specs=rows(d),
        out_shape=jax.ShapeDtypeStruct((b, e, cap, d), F32),
        compiler_params=_cparams(3),
    )(xs4, gate4, w1, w3, w2)


def _sc_combine_call(ye_flat, tok3, seq):
    n_pairs, n_chunks, _ = tok3.shape
    cap = n_chunks * SC_IDX
    d = ye_flat.shape[1]
    nb = n_pairs // N_EXPERTS
    info = plsc.get_sparse_core_info()
    assert info.num_subcores == N_EXPERTS and nb % info.num_cores == 0
    assert seq % (info.num_subcores * SC_ZROWS) == 0 and d % SC_SLAB == 0
    batches_per_core = nb // info.num_cores
    own_rows = seq // info.num_subcores
    mesh = plsc.VectorSubcoreMesh(core_axis_name="c", subcore_axis_name="s")

    @functools.partial(
        pl.kernel, mesh=mesh, name="expert_combine_sc",
        compiler_params=pltpu.CompilerParams(needs_layout_passes=False),
        out_type=jax.ShapeDtypeStruct((nb * seq, d), F32),
        scratch_types=[pltpu.VMEM_SHARED((seq, SC_SLAB), F32),
                       pltpu.VMEM((n_chunks, SC_IDX), jnp.int32),
                       pltpu.VMEM((SC_IDX, SC_SLAB), F32),
                       pltpu.VMEM((SC_ZROWS, SC_SLAB), F32)])
    def combine(ye_hbm, tok_hbm, out_hbm, acc_sh, tok_v, rows_v, zero_v):
        core = lax.axis_index("c")
        sub = lax.axis_index("s")

        @pl.loop(0, SC_ZROWS)
        def _(r):
            for l0 in range(0, SC_SLAB, SC_LANES):
                zero_v[r, pl.ds(l0, SC_LANES)] = jnp.zeros((SC_LANES,), F32)

        for bb in range(batches_per_core):
            batch = core * batches_per_core + bb
            pair = batch * N_EXPERTS + sub
            pltpu.sync_copy(tok_hbm.at[pair], tok_v)

            @pl.loop(0, d // SC_SLAB)
            def _(slab):
                cols = pl.ds(pl.multiple_of(slab * SC_SLAB, SC_SLAB), SC_SLAB)

                @pl.loop(0, own_rows, step=SC_ZROWS)
                def _(r0):
                    pltpu.sync_copy(zero_v, acc_sh.at[pl.ds(sub * own_rows + r0, SC_ZROWS)])

                plsc.subcore_barrier()

                @pl.loop(0, n_chunks)
                def _(j):
                    pltpu.sync_copy(ye_hbm.at[pl.ds(pair * cap + j * SC_IDX, SC_IDX), cols], rows_v)
                    pltpu.sync_copy(rows_v, acc_sh.at[tok_v.at[j]], add=True)

                plsc.subcore_barrier()
                pltpu.sync_copy(acc_sh.at[pl.ds(sub * own_rows, own_rows)],
                                out_hbm.at[pl.ds(batch * seq + sub * own_rows, own_rows), cols])

    return combine(ye_flat, tok3)


TN = 512


def _resln_kernel(x_ref, y_ref, g_ref, b_ref, o_ref, *, alpha):
    o_ref[...] = _standardize(alpha * x_ref[...] + y_ref[...]) * g_ref[...] + b_ref[...]


def _resln_call(x2d, y2d, g, bta, alpha):
    n, d = x2d.shape
    tok = pl.BlockSpec((TN, d), lambda i: (i, 0))
    vec = pl.BlockSpec((1, d), lambda i: (0, 0))
    return pl.pallas_call(
        functools.partial(_resln_kernel, alpha=alpha), name="residual_layernorm",
        grid=(n // TN,), in_specs=[tok, tok, vec, vec], out_specs=tok,
        out_shape=jax.ShapeDtypeStruct((n, d), F32),
        compiler_params=_cparams(1),
    )(x2d, y2d, g, bta)


def _expert_choice_ffn(x1, aff_t, w1, w3, w2):
    b, s, d = x1.shape
    cap = EC_FACTOR * s // N_EXPERTS
    slot = _select_call(aff_t, cap)
    xs, tok, gate = _sc_dispatch_call(x1.reshape(b * s, d), slot.reshape(b * N_EXPERTS, s),
                                      aff_t.reshape(b * N_EXPERTS, s), s, cap)
    ye = _expert_call(xs.reshape(b, N_EXPERTS, cap, d), gate.reshape(b, N_EXPERTS, cap, 1), w1, w3, w2)
    out = _sc_combine_call(ye.reshape(b * N_EXPERTS * cap, d),
                           tok.reshape(b * N_EXPERTS, cap // SC_IDX, SC_IDX), s)
    return out.reshape(b, s, d)


def _pack_pool(pool_w):
    g, gd, _ = pool_w.shape
    out = jnp.zeros((g * gd, g * gd), F32)
    for i in range(g):
        out = out.at[i * gd:(i + 1) * gd, i * gd:(i + 1) * gd].set(pool_w[i])
    return out.astype(BF16)


def _layer(x, alpha, bias_tiles, w_in, b_in, gm_ln_g, gm_ws, gm_bs, ml_conv, ml_fbias, ml_norm_g,
           pool_w, pool_scale, w_branch, w_out, ln1_g, ln1_b, w_router, w_e1, w_e3, w_e2, ln2_g, ln2_b):
    b, s, d = x.shape
    n_small = 2576
    w_cat, b_cat = _pack_inproj_weights(w_in, b_in)
    wscat, bsfull = _pack_gmlp(gm_ws, gm_bs)
    ya, qkv, cqk, cv, co, dx, ig, fg = _inproj_call(x.reshape(b * s, d), w_cat, b_cat, gm_ln_g[None],
                                                     wscat, bsfull)
    r3 = lambda t: t.reshape(b, s, t.shape[-1])
    qkv3 = r3(qkv)
    o_list, l_list = [], []
    for (window, dil), bias in zip(DIL_PATTERNS, bias_tiles):
        o, lse = _attn_call(_to_residues(qkv3, dil), bias)
        o_list.append(_from_residues(o, dil, b))
        l_list.append(_from_residues(lse, dil, b))
    hf, hb = _mlstm_branch(r3(cqk), r3(cv), r3(ig), r3(fg), ml_conv, ml_fbias)
    yd = _pool_call(r3(dx), _pack_pool(pool_w), pool_scale[None])
    x1, aff_t = _merge_call(
        x, r3(ya), o_list, l_list, hf, hb, r3(co), yd,
        w_in[:, n_small:].astype(BF16), b_in[None, n_small:], w_branch.astype(BF16), w_out.astype(BF16),
        ml_norm_g[None], ln1_g[None], ln1_b[None], jnp.transpose(w_router).astype(BF16), alpha)
    ffn = _expert_choice_ffn(x1, aff_t, w_e1.astype(BF16), w_e3.astype(BF16), w_e2.astype(BF16))
    x2 = _resln_call(x1.reshape(b * s, d), ffn.reshape(b * s, d), ln2_g[None], ln2_b[None], alpha)
    return x2.reshape(b, s, d)


def kernel(x, w_in, b_in, gm_ln_g, gm_ws, gm_bs, rel_bias, ml_conv, ml_fbias, ml_norm_g, pool_w,
           pool_scale, w_branch, w_out, ln1_g, ln1_b, w_router, w_e1, w_e3, w_e2, ln2_g, ln2_b):
    depth = w_in.shape[0]
    alpha = (2 * depth) ** 0.25
    bias_tiles = [_attn_bias_tile(rel_bias, window, dil) for window, dil in DIL_PATTERNS]
    for l in range(depth):
        x = _layer(x, alpha, bias_tiles, w_in[l], b_in[l], gm_ln_g[l], gm_ws[l], gm_bs[l], ml_conv[l],
                   ml_fbias[l], ml_norm_g[l], pool_w[l], pool_scale[l], w_branch[l], w_out[l],
                   ln1_g[l], ln1_b[l], w_router[l], w_e1[l], w_e3[l], w_e2[l], ln2_g[l], ln2_b[l])
    return x
```

```python
import functools
import math

import jax
import jax.numpy as jnp
import numpy as np
from jax import lax
from jax.experimental import pallas as pl
from jax.experimental.pallas import tpu as pltpu
from jax.experimental.pallas import tpu_sc as plsc

F32 = jnp.float32
BF16 = jnp.bfloat16

D_MODEL = 1024
MIX_W = 256
N_BRANCH = 4
GM_CHUNK = 128
GM_GROUPS = 4
ATT_HEADS = 4
ATT_HD = 64
DIL_PATTERNS = ((128, 1), (512, 4), (2048, 16))
ATT_BLOCK = 64
REL_BUCKETS = 32
REL_MAX_DIST = 1024
ML_HEADS = 4
ML_HD = 64
ML_CHUNK = 64
POOL_WINDOWS = (2, 4, 8, 16)
N_EXPERTS = 16
EXPERT_FF = 1024
EC_FACTOR = 2
LN_EPS = 1e-5
NEG_BIG = -1e30

V7X_VMEM_LIMIT = 56 * 1024 * 1024
LANES = 128
HALO = 8


def _cparams(n_grid, vmem=V7X_VMEM_LIMIT):
    return pltpu.CompilerParams(dimension_semantics=("arbitrary",) * n_grid,
                                vmem_limit_bytes=vmem)


def _standardize(xf):
    mu = jnp.mean(xf, axis=-1, keepdims=True)
    var = jnp.mean(jnp.square(xf - mu), axis=-1, keepdims=True)
    return (xf - mu) * lax.rsqrt(var + LN_EPS)


TA = 512
A_COLS = 2560 + 2 * LANES


def _inproj_kernel(x_ref, w_ref, b_ref, lng_ref, wscat_ref, bsfull_ref,
                   ya_ref, qkv_ref, cqk_ref, cv_ref, co_ref, dx_ref, ig_ref, fg_ref):
    xb = x_ref[...].astype(BF16)
    h = jnp.dot(xb, w_ref[...], preferred_element_type=F32) + b_ref[...]
    qkv_ref[...] = h[:, 512:1280].astype(BF16)
    cqk_ref[...] = h[:, 1280:1792]
    cv_ref[...] = h[:, 1792:2048].astype(BF16)
    co_ref[...] = h[:, 2048:2304]
    dx_ref[...] = h[:, 2304:2560]
    ig_ref[...] = h[:, 2560:2688]
    fg_ref[...] = h[:, 2688:2816]
    u = jax.nn.gelu(h[:, 0:256])
    v = jax.nn.gelu(h[:, 256:512])
    vn = _standardize(v) * lng_ref[...]
    lane_grp = lax.broadcasted_iota(jnp.int32, (GM_CHUNK, MIX_W), 1) // (MIX_W // GM_GROUPS)
    for c in range(TA // GM_CHUNK):
        vc = vn[c * GM_CHUNK:(c + 1) * GM_CHUNK]
        stacked = jnp.concatenate(
            [jnp.where(lane_grp == g, vc, 0.0).astype(BF16) for g in range(GM_GROUPS)], axis=0)
        mixed = jnp.dot(wscat_ref[...], stacked, preferred_element_type=F32) + bsfull_ref[...]
        ya_ref[c * GM_CHUNK:(c + 1) * GM_CHUNK, :] = (
            u[c * GM_CHUNK:(c + 1) * GM_CHUNK] * mixed).astype(BF16)


def _inproj_call(x2d, w_cat, b_cat, lng, wscat, bsfull):
    n = x2d.shape[0]
    tok = lambda w: pl.BlockSpec((TA, w), lambda i: (i, 0))
    const = lambda s: pl.BlockSpec(s, lambda i: (0,) * len(s))
    out_shape = (
        jax.ShapeDtypeStruct((n, 256), BF16),
        jax.ShapeDtypeStruct((n, 768), BF16),
        jax.ShapeDtypeStruct((n, 512), F32),
        jax.ShapeDtypeStruct((n, 256), BF16),
        jax.ShapeDtypeStruct((n, 256), F32),
        jax.ShapeDtypeStruct((n, 256), F32),
        jax.ShapeDtypeStruct((n, LANES), F32),
        jax.ShapeDtypeStruct((n, LANES), F32),
    )
    return pl.pallas_call(
        _inproj_kernel, name="inproj_gmlp",
        grid=(n // TA,),
        in_specs=[tok(D_MODEL), const((D_MODEL, A_COLS)), const((1, A_COLS)), const((1, MIX_W)),
                  const((GM_CHUNK, GM_GROUPS * GM_CHUNK)), const((GM_CHUNK, MIX_W))],
        out_specs=(tok(256), tok(768), tok(512), tok(256), tok(256), tok(256), tok(LANES), tok(LANES)),
        out_shape=out_shape,
        compiler_params=_cparams(1),
    )(x2d, w_cat, b_cat, lng, wscat, bsfull)


def _pack_inproj_weights(w_in, b_in):
    pad = lambda a: jnp.pad(a, ((0, 0), (0, LANES - 8)))
    w_cat = jnp.concatenate([w_in[:, 0:2304], w_in[:, 2320:2576],
                             pad(w_in[:, 2304:2312]), pad(w_in[:, 2312:2320])], axis=1)
    b2 = b_in[None, :]
    b_cat = jnp.concatenate([b2[:, 0:2304], b2[:, 2320:2576],
                             pad(b2[:, 2304:2312]), pad(b2[:, 2312:2320])], axis=1)
    return w_cat.astype(BF16), b_cat


def _pack_gmlp(gm_ws, gm_bs):
    wscat = jnp.transpose(gm_ws, (1, 0, 2)).reshape(GM_CHUNK, GM_GROUPS * GM_CHUNK).astype(BF16)
    bsfull = jnp.repeat(jnp.transpose(gm_bs), MIX_W // GM_GROUPS, axis=1)
    return wscat, bsfull


def _halo_specs(t, width, n_tiles):
    r = t // HALO
    main = pl.BlockSpec((1, t, width), lambda b, i: (b, i, 0))
    prev = pl.BlockSpec((1, HALO, width), lambda b, i: (b, jnp.maximum(i * r - 1, 0), 0))
    nxt = pl.BlockSpec((1, HALO, width), lambda b, i: (b, jnp.minimum((i + 1) * r, n_tiles * r - 1), 0))
    return main, prev, nxt


def _fill_halo_scratch(buf, x_ref, p_ref, n_ref, t):
    i = pl.program_id(1)
    last = pl.num_programs(1) - 1
    buf[0:HALO, :] = jnp.where(i > 0, p_ref[0], 0.0)
    buf[HALO:HALO + t, :] = x_ref[0]
    buf[HALO + t:2 * HALO + t, :] = jnp.where(i < last, n_ref[0], 0.0)


TP = 512


def _pool_kernel(x_ref, p_ref, n_ref, w_ref, sc_ref, o_ref, buf):
    _fill_halo_scratch(buf, x_ref, p_ref, n_ref, TP)
    seq = pl.num_programs(1) * TP
    pos = pl.program_id(1) * TP + lax.broadcasted_iota(jnp.int32, (TP, 1), 0)
    lane_grp = lax.broadcasted_iota(jnp.int32, (TP, MIX_W), 1) // (MIX_W // len(POOL_WINDOWS))
    x0 = buf[HALO:HALO + TP, :]
    pooled = jnp.zeros((TP, MIX_W), F32)
    acc = None
    half_done = 0
    for gi, win in enumerate(POOL_WINDOWS):
        half = win // 2
        for o in list(range(-half, -half_done)) + list(range(half_done, half)):
            term = buf[HALO + o:HALO + o + TP, :]
            acc = term if acc is None else acc + term
        half_done = half
        cnt = (jnp.minimum(pos + half, seq) - jnp.maximum(pos - half, 0)).astype(F32)
        pooled = jnp.where(lane_grp == gi, acc / cnt - x0, pooled)
    mixed = jnp.dot(pooled.astype(BF16), w_ref[...], preferred_element_type=F32)
    o_ref[0] = (mixed * sc_ref[...]).astype(BF16)


def _pool_call(dx, w_block, scale):
    b, s, _ = dx.shape
    nt = s // TP
    main, prev, nxt = _halo_specs(TP, MIX_W, nt)
    return pl.pallas_call(
        _pool_kernel, name="pool_mixer",
        grid=(b, nt),
        in_specs=[main, prev, nxt,
                  pl.BlockSpec((MIX_W, MIX_W), lambda b, i: (0, 0)),
                  pl.BlockSpec((1, MIX_W), lambda b, i: (0, 0))],
        out_specs=pl.BlockSpec((1, TP, MIX_W), lambda b, i: (b, i, 0)),
        out_shape=jax.ShapeDtypeStruct((b, s, MIX_W), BF16),
        scratch_shapes=[pltpu.VMEM((TP + 2 * HALO, MIX_W), F32)],
        compiler_params=_cparams(2),
    )(dx, dx, dx, w_block, scale)


TQ = 128
TQS = 512
TKEYS = TQ + 2 * ATT_BLOCK


def _attn_kernel(q_ref, kp_ref, km_ref, kn_ref, vp_ref, vm_ref, vn_ref, bias_ref, o_ref, lse_ref):
    i = pl.program_id(2)
    seq = pl.num_programs(2) * TQS
    q = q_ref[0] * ATT_HD ** -0.5
    k = jnp.concatenate([kp_ref[0], km_ref[0], kn_ref[0]], axis=0)
    v = jnp.concatenate([vp_ref[0], vm_ref[0], vn_ref[0]], axis=0)
    lane = lax.broadcasted_iota(jnp.int32, (TQ, LANES), 1)
    for j in range(TQS // TQ):
        kpos = i * TQS + j * TQ - ATT_BLOCK + lax.broadcasted_iota(jnp.int32, (1, TKEYS), 1)
        kvalid = (kpos >= 0) & (kpos < seq)
        qrows = slice(j * TQ, (j + 1) * TQ)
        krows = slice(j * TQ, j * TQ + TKEYS)
        lse_tile = jnp.zeros((TQ, LANES), F32)
        for h in range(ATT_HEADS):
            sl = slice(h * ATT_HD, (h + 1) * ATT_HD)
            logits = lax.dot_general(q[qrows, sl], k[krows, sl], (((1,), (1,)), ((), ())),
                                     preferred_element_type=F32) + bias_ref[h]
            logits = jnp.where(kvalid, logits, NEG_BIG)
            m = jnp.max(logits, axis=-1, keepdims=True)
            p = jnp.exp(logits - m)
            ssum = jnp.sum(p, axis=-1, keepdims=True)
            o = jnp.dot(p.astype(BF16), v[krows, sl], preferred_element_type=F32) / ssum
            o_ref[0, qrows, sl] = o
            lse_tile = jnp.where(lane == h, m + jnp.log(ssum), lse_tile)
        lse_ref[0, qrows, :] = lse_tile


def _attn_call(qkv3, bias, dil):
    b, s, _ = qkv3.shape
    l = s // dil
    nt = l // TQS
    r64 = TQS // ATT_BLOCK
    view = qkv3.reshape(b, l, dil * 3 * MIX_W)
    main = lambda c: pl.BlockSpec((1, TQS, MIX_W), lambda b, r, i: (b, i, r * 3 + c))
    prev = lambda c: pl.BlockSpec((1, ATT_BLOCK, MIX_W),
                                  lambda b, r, i: (b, jnp.maximum(i * r64 - 1, 0), r * 3 + c))
    nxt = lambda c: pl.BlockSpec((1, ATT_BLOCK, MIX_W),
                                 lambda b, r, i: (b, jnp.minimum((i + 1) * r64, nt * r64 - 1), r * 3 + c))
    o, lse = pl.pallas_call(
        _attn_kernel, name="band_attention",
        grid=(b, dil, nt),
        in_specs=[main(0), prev(1), main(1), nxt(1), prev(2), main(2), nxt(2),
                  pl.BlockSpec((ATT_HEADS, TQ, TKEYS), lambda b, r, i: (0, 0, 0))],
        out_specs=(pl.BlockSpec((1, TQS, MIX_W), lambda b, r, i: (b, i, r)),
                   pl.BlockSpec((1, TQS, LANES), lambda b, r, i: (b, i, r))),
        out_shape=(jax.ShapeDtypeStruct((b, l, dil * MIX_W), F32),
                   jax.ShapeDtypeStruct((b, l, dil * LANES), F32)),
        compiler_params=_cparams(3),
    )(view, view, view, view, view, view, view, bias)
    return o.reshape(b, s, MIX_W), lse.reshape(b, s, LANES)


def _t5_bucket_static(rel):
    half = REL_BUCKETS // 2
    max_exact = half // 2
    ret = np.where(rel > 0, half, 0)
    n = np.abs(rel)
    nf = np.maximum(n, 1).astype(np.float32)
    large = max_exact + (np.log(nf / np.float32(max_exact)) / np.float32(math.log(REL_MAX_DIST / max_exact))
                         * np.float32(half - max_exact)).astype(np.int32)
    large = np.minimum(large, half - 1)
    return ret + np.where(n < max_exact, n, large)


def _attn_bias_tile(rel_bias, window, dil):
    side = (window // 2) // dil
    rel = np.arange(TKEYS)[None, :] - ATT_BLOCK - np.arange(TQ)[:, None]
    onehot = jax.nn.one_hot(jnp.asarray(_t5_bucket_static(dil * rel), jnp.int32), REL_BUCKETS, dtype=F32)
    bias = jnp.einsum('qkr,rh->hqk', onehot, rel_bias, precision=lax.Precision.HIGHEST)
    return jnp.where(jnp.asarray(np.abs(rel) <= side)[None], bias, NEG_BIG)


TM = 512


def _mlprep_kernel(x_ref, p_ref, n_ref, v_ref, w_ref, q_out, kt_out, v_out, buf):
    _fill_halo_scratch(buf, x_ref, p_ref, n_ref, TM)
    conv = (buf[HALO - 1:HALO - 1 + TM, :] * w_ref[0:1, :] + buf[HALO:HALO + TM, :] * w_ref[1:2, :]
            + buf[HALO + 1:HALO + 1 + TM, :] * w_ref[2:3, :])
    qk = jax.nn.silu(conv)
    kt = jnp.transpose(qk[:, MIX_W:] * ML_HD ** -0.5)
    v = v_ref[0]
    ones_col = jnp.where(lax.broadcasted_iota(jnp.int32, (TM, ML_HD), 1) == 0, 1.0, 0.0).astype(BF16)
    for h in range(ML_HEADS):
        sl = slice(h * ML_HD, (h + 1) * ML_HD)
        q_out[0, h] = qk[:, sl].astype(BF16)
        kt_out[0, h] = kt[sl, :].astype(BF16)
        v_out[0, h] = jnp.concatenate([v[:, sl], ones_col], axis=1)


def _mlprep_call(cqk, cv, conv_w):
    b, s, _ = cqk.shape
    nt = s // TM
    main, prev, nxt = _halo_specs(TM, 2 * MIX_W, nt)
    return pl.pallas_call(
        _mlprep_kernel, name="mlstm_prep",
        grid=(b, nt),
        in_specs=[main, prev, nxt,
                  pl.BlockSpec((1, TM, MIX_W), lambda b, i: (b, i, 0)),
                  pl.BlockSpec((3, 2 * MIX_W), lambda b, i: (0, 0))],
        out_specs=(pl.BlockSpec((1, ML_HEADS, TM, ML_HD), lambda b, i: (b, 0, i, 0)),
                   pl.BlockSpec((1, ML_HEADS, ML_HD, TM), lambda b, i: (b, 0, 0, i)),
                   pl.BlockSpec((1, ML_HEADS, TM, 2 * ML_HD), lambda b, i: (b, 0, i, 0))),
        out_shape=(jax.ShapeDtypeStruct((b, ML_HEADS, s, ML_HD), BF16),
                   jax.ShapeDtypeStruct((b, ML_HEADS, ML_HD, s), BF16),
                   jax.ShapeDtypeStruct((b, ML_HEADS, s, 2 * ML_HD), BF16)),
        scratch_shapes=[pltpu.VMEM((TM + 2 * HALO, 2 * MIX_W), F32)],
        compiler_params=_cparams(2),
    )(cqk, cqk, cqk, cv, conv_w)


GP_BLK = 512


def _chunk_scan(x, fwd_lane, t_in_chunk, op, ident):
    n = x.shape[0]
    shift = 1
    while shift < ML_CHUNK:
        down = pltpu.roll(x, shift, 0)
        up = pltpu.roll(x, n - shift, 0)
        nb = jnp.where(fwd_lane,
                       jnp.where(t_in_chunk >= shift, down, ident),
                       jnp.where(t_in_chunk < ML_CHUNK - shift, up, ident))
        x = op(x, nb)
        shift *= 2
    return x


def _gatescan_kernel(ig_ref, fg_ref, fb_ref, b_ref, a_ref, cm_ref, g_ref, amax_ref):
    lane = lax.broadcasted_iota(jnp.int32, (1, LANES), 1)
    fwd_lane = lane < ML_HEADS
    t_in_chunk = lax.broadcasted_iota(jnp.int32, (GP_BLK, 1), 0) % ML_CHUNK
    z = fg_ref[0] + fb_ref[...]
    lf = jnp.minimum(z, 0.0) - jnp.log1p(jnp.exp(-jnp.abs(z)))
    b = _chunk_scan(lf, fwd_lane, t_in_chunk, jnp.add, 0.0)
    a = ig_ref[0] - b
    b_ref[0] = b
    a_ref[0] = a
    cm_ref[0] = _chunk_scan(a, fwd_lane, t_in_chunk, jnp.maximum, -jnp.inf)
    cpb = GP_BLK // ML_CHUNK
    last = pl.ds(ML_CHUNK - 1, cpb, stride=ML_CHUNK)
    first = pl.ds(0, cpb, stride=ML_CHUNK)
    g_ref[0] = jnp.where(fwd_lane, b_ref[0, last, :], b_ref[0, first, :])
    amax_ref[0] = jnp.where(fwd_lane, cm_ref[0, last, :], cm_ref[0, first, :])


def _gateout_kernel(b_ref, a_ref, cm_ref, g_ref, amax_ref, ws_ref, m_ref, iw_ref, en_ref, dec_ref,
                    mch_s, mlast_s):
    nc = g_ref.shape[1]
    j = pl.program_id(1)
    fwd_lane = lax.broadcasted_iota(jnp.int32, (1, LANES), 1) < ML_HEADS

    @pl.when(j == 0)
    def _():
        def m_step(i, carry):
            mf, mb = carry
            cf = pl.ds(i, 1)
            cb = pl.ds(nc - 1 - i, 1)
            mch_s[cf, :] = jnp.where(fwd_lane, mf, mch_s[cf, :])
            mch_s[cb, :] = jnp.where(fwd_lane, mch_s[cb, :], mb)
            mf = g_ref[0, cf, :] + jnp.maximum(mf, amax_ref[0, cf, :])
            mb = g_ref[0, cb, :] + jnp.maximum(mb, amax_ref[0, cb, :])
            return mf, mb

        mch_s[...] = jnp.zeros((nc, LANES), F32)
        zero = jnp.zeros((1, LANES), F32)
        lax.fori_loop(0, nc, m_step, (zero, zero))
        mlast = jnp.maximum(amax_ref[0], mch_s[...])
        mlast_s[...] = mlast
        dec_ref[0] = jnp.exp(mch_s[...] - mlast)

    cpb = GP_BLK // ML_CHUNK
    crow = pl.ds(pl.multiple_of(j * cpb, cpb), cpb)
    expand = lambda t: jnp.broadcast_to(t[:, None, :], (cpb, ML_CHUNK, LANES)).reshape(GP_BLK, LANES)
    m_tok = expand(mch_s[crow, :])
    mlast_tok = expand(mlast_s[crow, :])
    mt = jnp.maximum(cm_ref[0], m_tok)
    m_ref[0] = mt
    iw_ref[0] = jnp.exp(m_tok - mt)
    en_ref[0] = jnp.exp(-(b_ref[0] + mt))
    ws_ref[0] = jnp.exp(a_ref[0] - mlast_tok)


def _gateprep_call(ig, fg, fbias_row):
    b, s, _ = ig.shape
    nc = s // ML_CHUNK
    cpb = GP_BLK // ML_CHUNK
    tok = pl.BlockSpec((1, GP_BLK, LANES), lambda i, j: (i, j, 0))
    tok_shape = jax.ShapeDtypeStruct((b, s, LANES), F32)
    chunk_shape = jax.ShapeDtypeStruct((b, nc, LANES), F32)
    chunk_tile = pl.BlockSpec((1, cpb, LANES), lambda i, j: (i, j, 0))
    chunk_all = pl.BlockSpec((1, nc, LANES), lambda i, j: (i, 0, 0))
    bcum, a, cm, g, amax = pl.pallas_call(
        _gatescan_kernel, name="mlstm_gate_scan",
        grid=(b, s // GP_BLK),
        in_specs=[tok, tok, pl.BlockSpec((1, LANES), lambda i, j: (0, 0))],
        out_specs=(tok, tok, tok, chunk_tile, chunk_tile),
        out_shape=(tok_shape,) * 3 + (chunk_shape,) * 2,
        compiler_params=_cparams(2),
    )(ig, fg, fbias_row)
    ws, m_col, iw_col, en_col, decay = pl.pallas_call(
        _gateout_kernel, name="mlstm_gate_out",
        grid=(b, s // GP_BLK),
        in_specs=[tok, tok, tok, chunk_all, chunk_all],
        out_specs=(tok, tok, tok, tok, chunk_all),
        out_shape=(tok_shape,) * 4 + (chunk_shape,),
        scratch_shapes=[pltpu.VMEM((nc, LANES), F32)] * 2,
        compiler_params=_cparams(2),
    )(bcum, a, cm, g, amax)
    return a, ws, m_col, iw_col, en_col, decay


TE = 512
PAIR = 2 * ML_CHUNK


def _mlstm_chunk(state, d, h, c, qc, ktc, vc, a_row, ws_row, m_col, iw_col, en_col, decay):
    t_idx = lax.broadcasted_iota(jnp.int32, (ML_CHUNK, ML_CHUNK), 0)
    s_idx = lax.broadcasted_iota(jnp.int32, (ML_CHUNK, ML_CHUNK), 1)
    tri = (s_idx <= t_idx) if d == 0 else (s_idx >= t_idx)
    s = jnp.dot(qc, ktc, preferred_element_type=F32)
    w = jnp.exp(jnp.where(tri, a_row - m_col, NEG_BIG))
    sw = s * w
    den_intra = jnp.sum(sw, axis=-1, keepdims=True)
    cst = state[d * ML_HEADS + h]
    intra = jnp.dot(sw.astype(BF16), vc, preferred_element_type=F32)
    inter = jnp.dot(qc, cst.astype(BF16), preferred_element_type=F32)
    tot = intra + iw_col * inter
    den = den_intra + iw_col * inter[:, ML_HD:ML_HD + 1]
    hh = tot[:, :ML_HD] / jnp.maximum(jnp.abs(den), en_col)
    upd = jnp.dot((ktc.astype(F32) * ws_row).astype(BF16), vc, preferred_element_type=F32)
    state[d * ML_HEADS + h] = decay * cst + upd
    return hh


def _mlstm_kernel(*refs):
    (qf, ktf, vf, af, wsf, mf, iwf, enf, qb, ktb, vb, ab, wsb, mb, iwb, enb, dec_ref,
     hf_ref, hb_ref, state) = refs
    i = pl.program_id(1)
    nt = pl.num_programs(1)

    @pl.when(i == 0)
    def _():
        state[...] = jnp.zeros(state.shape, F32)

    n_pairs = TE // PAIR
    cpt = TE // ML_CHUNK
    dirs = ((0, qf, ktf, vf, af, wsf, mf, iwf, enf, hf_ref), (1, qb, ktb, vb, ab, wsb, mb, iwb, enb, hb_ref))

    def pair_body(p, carry):
        for d, q_r, kt_r, v_r, a_r, ws_r, m_r, iw_r, en_r, out_r in dirs:
            pp = p if d == 0 else n_pairs - 1 - p
            tile = i if d == 0 else nt - 1 - i
            lanes = pl.ds(pl.multiple_of(pp * PAIR, PAIR), PAIR)
            for half in ((0, 1) if d == 0 else (1, 0)):
                rows = pl.ds(pl.multiple_of(pp * PAIR + half * ML_CHUNK, ML_CHUNK), ML_CHUNK)
                chunk = tile * cpt + pp * 2 + half
                hs = slice(half * ML_CHUNK, (half + 1) * ML_CHUNK)
                for h in range(ML_HEADS):
                    ch = d * ML_HEADS + h
                    hh = _mlstm_chunk(
                        state, d, h, chunk,
                        q_r[0, h, rows, :], kt_r[0, h, :, lanes][:, hs], v_r[0, h, rows, :],
                        a_r[0, ch:ch + 1, lanes][:, hs], ws_r[0, ch:ch + 1, lanes][:, hs],
                        m_r[0, rows, ch:ch + 1], iw_r[0, rows, ch:ch + 1], en_r[0, rows, ch:ch + 1],
                        dec_ref[0, pl.ds(chunk, 1), ch:ch + 1])
                    out_r[0, rows, h * ML_HD:(h + 1) * ML_HD] = hh
        return carry

    lax.fori_loop(0, n_pairs, pair_body, 0)


def _mlstm_call(q_hm, kt_hm, v_aug, a_row, ws_row, m_col, iw_col, en_col, decay):
    b, _, s, _ = q_hm.shape
    nt = s // TE
    nc = s // ML_CHUNK

    def specs(rev):
        ti = (lambda i: nt - 1 - i) if rev else (lambda i: i)
        return [
            pl.BlockSpec((1, ML_HEADS, TE, ML_HD), lambda b, i: (b, 0, ti(i), 0)),
            pl.BlockSpec((1, ML_HEADS, ML_HD, TE), lambda b, i: (b, 0, 0, ti(i))),
            pl.BlockSpec((1, ML_HEADS, TE, 2 * ML_HD), lambda b, i: (b, 0, ti(i), 0)),
            pl.BlockSpec((1, 2 * ML_HEADS, TE), lambda b, i: (b, 0, ti(i))),
            pl.BlockSpec((1, 2 * ML_HEADS, TE), lambda b, i: (b, 0, ti(i))),
            pl.BlockSpec((1, TE, LANES), lambda b, i: (b, ti(i), 0)),
            pl.BlockSpec((1, TE, LANES), lambda b, i: (b, ti(i), 0)),
            pl.BlockSpec((1, TE, LANES), lambda b, i: (b, ti(i), 0)),
        ]

    args = [q_hm, kt_hm, v_aug, a_row, ws_row, m_col, iw_col, en_col]
    out_f = pl.BlockSpec((1, TE, MIX_W), lambda b, i: (b, i, 0))
    out_b = pl.BlockSpec((1, TE, MIX_W), lambda b, i: (b, nt - 1 - i, 0))
    return pl.pallas_call(
        _mlstm_kernel, name="mlstm_scan",
        grid=(b, nt),
        in_specs=specs(False) + specs(True) + [pl.BlockSpec((1, nc, LANES), lambda b, i: (b, 0, 0))],
        out_specs=(out_f, out_b),
        out_shape=(jax.ShapeDtypeStruct((b, s, MIX_W), F32),) * 2,
        scratch_shapes=[pltpu.VMEM((2 * ML_HEADS, ML_HD, 2 * ML_HD), F32)],
        compiler_params=_cparams(2),
    )(*args, *args, decay)


def _mlstm_branch(cqk, cv, ig, fg, conv_w, fbias):
    q_hm, kt_hm, v_aug = _mlprep_call(cqk, cv, conv_w)
    fb_row = jnp.pad(fbias.reshape(1, 2 * ML_HEADS), ((0, 0), (0, LANES - 2 * ML_HEADS)))
    a, ws, m_col, iw_col, en_col, decay = _gateprep_call(ig, fg, fb_row)
    rows = lambda t: jnp.transpose(t[..., :2 * ML_HEADS], (0, 2, 1))
    return _mlstm_call(q_hm, kt_hm, v_aug, rows(a), rows(ws), m_col, iw_col, en_col, decay)


TF = 256


def _merge_kernel(x_ref, ya_ref, o1_ref, o2_ref, o3_ref, l1_ref, l2_ref, l3_ref, hf_ref, hb_ref,
                  co_ref, yd_ref, wg_ref, bg_ref, wbr_ref, wout_ref, mng_ref, lng_ref, lnb_ref,
                  wr_ref, x1_ref, aff_ref, *, alpha):
    x = x_ref[0]
    xb = x.astype(BF16)
    lane_head = lax.broadcasted_iota(jnp.int32, (TF, MIX_W), 1) // ML_HD
    l1, l2, l3 = l1_ref[0], l2_ref[0], l3_ref[0]
    lm = jnp.maximum(jnp.maximum(l1, l2), l3)
    e1, e2, e3 = jnp.exp(l1 - lm), jnp.exp(l2 - lm), jnp.exp(l3 - lm)
    inv = 1.0 / (e1 + e2 + e3)

    def per_head(w):
        out = jnp.zeros((TF, MIX_W), F32)
        for h in range(ATT_HEADS):
            out = jnp.where(lane_head == h, w[:, h:h + 1], out)
        return out

    y_b = (per_head(e1 * inv) * o1_ref[0] + per_head(e2 * inv) * o2_ref[0]
           + per_head(e3 * inv) * o3_ref[0])
    hsum = hf_ref[0] + hb_ref[0]
    mu = jnp.zeros((TF, MIX_W), F32)
    for h in range(ML_HEADS):
        sel = lane_head == h
        mu = jnp.where(sel, jnp.sum(jnp.where(sel, hsum, 0.0), axis=-1, keepdims=True) / ML_HD, mu)
    cen = hsum - mu
    var = jnp.zeros((TF, MIX_W), F32)
    for h in range(ML_HEADS):
        sel = lane_head == h
        var = jnp.where(sel, jnp.sum(jnp.where(sel, cen * cen, 0.0), axis=-1, keepdims=True) / ML_HD, var)
    y_c = jax.nn.sigmoid(co_ref[0]) * (cen * lax.rsqrt(var + LN_EPS) * mng_ref[...])
    ys = (ya_ref[0], y_b.astype(BF16), y_c.astype(BF16), yd_ref[0])
    merged = jnp.zeros((TF, D_MODEL), F32)
    for n in range(N_BRANCH):
        cols = slice(n * D_MODEL, (n + 1) * D_MODEL)
        gate = jax.nn.sigmoid(jnp.dot(xb, wg_ref[:, cols], preferred_element_type=F32) + bg_ref[:, cols])
        merged = merged + gate * jnp.dot(ys[n], wbr_ref[n], preferred_element_type=F32)
    mix = jnp.dot(merged.astype(BF16), wout_ref[...], preferred_element_type=F32)
    x1 = _standardize(alpha * x + mix) * lng_ref[...] + lnb_ref[...]
    x1_ref[0] = x1
    logits = lax.dot_general(wr_ref[...], x1.astype(BF16), (((1,), (1,)), ((), ())),
                             preferred_element_type=F32)
    ex = jnp.exp(logits - jnp.max(logits, axis=0, keepdims=True))
    aff_ref[0] = ex / jnp.sum(ex, axis=0, keepdims=True)


def _merge_call(x, ya, o_list, l_list, hf, hb, co, yd, wg, bg, wbr, wout, mng, lng, lnb, wr_t, alpha):
    b, s, _ = x.shape
    tok = lambda w: pl.BlockSpec((1, TF, w), lambda b, i: (b, i, 0))
    const = lambda shp: pl.BlockSpec(shp, lambda b, i: (0,) * len(shp))
    return pl.pallas_call(
        functools.partial(_merge_kernel, alpha=alpha), name="merge_ln_router",
        grid=(b, s // TF),
        in_specs=[tok(D_MODEL)] + [tok(MIX_W)] * 4 + [tok(LANES)] * 3 + [tok(MIX_W)] * 4
                 + [const((D_MODEL, N_BRANCH * D_MODEL)), const((1, N_BRANCH * D_MODEL)),
                    const((N_BRANCH, MIX_W, D_MODEL)), const((D_MODEL, D_MODEL)), const((1, MIX_W)),
                    const((1, D_MODEL)), const((1, D_MODEL)), const((N_EXPERTS, D_MODEL))],
        out_specs=(tok(D_MODEL), pl.BlockSpec((1, N_EXPERTS, TF), lambda b, i: (b, 0, i))),
        out_shape=(jax.ShapeDtypeStruct((b, s, D_MODEL), F32),
                   jax.ShapeDtypeStruct((b, N_EXPERTS, s), F32)),
        compiler_params=_cparams(2),
    )(x, ya, *o_list, *l_list, hf, hb, co, yd, wg, bg, wbr, wout, mng, lng, lnb, wr_t)


TT = 256


def _select_kernel(aff_ref, slot_ref, *, cap):
    s = aff_ref.shape[2]
    bits = pltpu.bitcast(aff_ref[0], jnp.int32)

    def bit_step(i, thr):
        cand = thr | jnp.left_shift(jnp.int32(1), 30 - i)
        cnt = jnp.sum((bits >= cand).astype(jnp.int32), axis=1, keepdims=True)
        return jnp.where(cnt >= cap, cand, thr)

    thr = lax.fori_loop(0, 31, bit_step, jnp.zeros((N_EXPERTS, 1), jnp.int32))
    gt = bits > thr
    eq = bits == thr
    need = (cap - jnp.sum(gt.astype(jnp.int32), axis=1, keepdims=True)).astype(F32)
    upper = (lax.broadcasted_iota(jnp.int32, (TT, TT), 0)
             <= lax.broadcasted_iota(jnp.int32, (TT, TT), 1)).astype(BF16)
    eq_before = jnp.zeros((N_EXPERTS, 1), F32)
    sel_before = jnp.zeros((N_EXPERTS, 1), F32)
    for j in range(s // TT):
        cols = slice(j * TT, (j + 1) * TT)
        eq_j = eq[:, cols]
        eq_incl = eq_before + jnp.dot(eq_j.astype(BF16), upper, preferred_element_type=F32)
        sel_j = gt[:, cols] | (eq_j & (eq_incl <= need))
        sel_f = sel_j.astype(F32)
        sel_incl = sel_before + jnp.dot(sel_f.astype(BF16), upper, preferred_element_type=F32)
        slot_ref[0, :, cols] = jnp.where(sel_j, sel_incl - 1.0, -1.0).astype(jnp.int32)
        eq_before = eq_incl[:, TT - 1:TT]
        sel_before = sel_incl[:, TT - 1:TT]


def _select_call(aff_t, cap):
    b, e, s = aff_t.shape
    return pl.pallas_call(
        functools.partial(_select_kernel, cap=cap), name="expert_choice_select",
        grid=(b,),
        in_specs=[pl.BlockSpec((1, e, s), lambda i: (i, 0, 0))],
        out_specs=pl.BlockSpec((1, e, s), lambda i: (i, 0, 0)),
        out_shape=jax.ShapeDtypeStruct((b, e, s), jnp.int32),
        compiler_params=_cparams(1),
    )(aff_t)


SC_LANES = 16
SC_ROWS = 64
SC_IDX = 128
SC_SLAB = 128
SC_ZROWS = 64
CF = 512


def _sc_dispatch_call(x_flat, slot2, aff2, seq, cap):
    n_pairs = slot2.shape[0]
    d = x_flat.shape[1]
    info = plsc.get_sparse_core_info()
    n_workers = info.num_cores * info.num_subcores
    assert n_pairs % n_workers == 0 and seq % SC_LANES == 0 and cap % SC_ROWS == 0
    pairs_per_worker = n_pairs // n_workers
    mesh = plsc.VectorSubcoreMesh(core_axis_name="c", subcore_axis_name="s")

    @functools.partial(
        pl.kernel, mesh=mesh, name="expert_dispatch_sc",
        compiler_params=pltpu.CompilerParams(needs_layout_passes=False),
        out_type=(jax.ShapeDtypeStruct((n_pairs * cap, d), F32),
                  jax.ShapeDtypeStruct((n_pairs, cap), jnp.int32),
                  jax.ShapeDtypeStruct((n_pairs, cap), F32)),
        scratch_types=[pltpu.VMEM((seq,), jnp.int32), pltpu.VMEM((seq,), F32),
                       pltpu.VMEM((cap,), jnp.int32), pltpu.VMEM((cap,), jnp.int32),
                       pltpu.VMEM((cap,), F32),
                       pltpu.VMEM((SC_ROWS, d), F32), pltpu.SemaphoreType.DMA])
    def dispatch(x_hbm, slot_hbm, aff_hbm, xs_hbm, tok_hbm, gate_hbm,
                 slot_v, aff_v, idx_v, tok_v, gate_v, rows_v, sem):
        worker = lax.axis_index("s") * info.num_cores + lax.axis_index("c")
        lane = lax.iota(jnp.int32, SC_LANES)
        for k in range(pairs_per_worker):
            pair = worker * pairs_per_worker + k
            row0 = (pair // N_EXPERTS) * seq
            pltpu.sync_copy(slot_hbm.at[pair], slot_v)
            pltpu.sync_copy(aff_hbm.at[pair], aff_v)

            @pl.loop(0, seq, step=SC_LANES)
            def _(t0):
                sv = slot_v[pl.ds(t0, SC_LANES)]
                picked = sv >= 0
                plsc.store_scatter(tok_v, [sv], t0 + lane, mask=picked)
                plsc.store_scatter(idx_v, [sv], row0 + t0 + lane, mask=picked)
                plsc.store_scatter(gate_v, [sv], aff_v[pl.ds(t0, SC_LANES)], mask=picked)

            pltpu.sync_copy(tok_v, tok_hbm.at[pair])
            pltpu.sync_copy(gate_v, gate_hbm.at[pair])

            @pl.loop(0, cap, step=SC_ROWS)
            def _(c0):
                pltpu.async_copy(x_hbm.at[idx_v.at[pl.ds(c0, SC_ROWS)]], rows_v, sem).wait()
                pltpu.sync_copy(rows_v, xs_hbm.at[pl.ds(pair * cap + c0, SC_ROWS)])

    return dispatch(x_flat, slot2, aff2)


def _expert_kernel(xs_ref, g_ref, w1_ref, w3_ref, w2_ref, ye_ref):
    xs = xs_ref[0, 0].astype(BF16)
    hid = (jax.nn.silu(jnp.dot(xs, w1_ref[0], preferred_element_type=F32))
           * jnp.dot(xs, w3_ref[0], preferred_element_type=F32))
    ye_ref[0, 0] = jnp.dot(hid.astype(BF16), w2_ref[0], preferred_element_type=F32) * g_ref[0, 0]


def _expert_call(xs4, gate4, w1, w3, w2):
    b, e, cap, d = xs4.shape
    ff = w1.shape[2]
    rows = lambda w: pl.BlockSpec((1, 1, CF, w), lambda e, b, j: (b, e, j, 0))
    wspec = lambda r, c: pl.BlockSpec((1, r, c), lambda e, b, j: (e, 0, 0))
    return pl.pallas_call(
        _expert_kernel, name="expert_ffn",
        grid=(e, b, cap // CF),
        in_specs=[rows(d), rows(1), wspec(d, ff), wspec(d, ff), wspec(ff, d)],
        out_specs=rows(d),
        out_shape=jax.ShapeDtypeStruct((b, e, cap, d), F32),
        compiler_params=_cparams(3),
    )(xs4, gate4, w1, w3, w2)


def _sc_combine_call(ye_flat, tok3, seq):
    n_pairs, n_chunks, _ = tok3.shape
    cap = n_chunks * SC_IDX
    d = ye_flat.shape[1]
    nb = n_pairs // N_EXPERTS
    info = plsc.get_sparse_core_info()
    assert info.num_subcores == N_EXPERTS and nb % info.num_cores == 0
    assert seq % (info.num_subcores * SC_ZROWS) == 0 and d % SC_SLAB == 0
    batches_per_core = nb // info.num_cores
    own_rows = seq // info.num_subcores
    mesh = plsc.VectorSubcoreMesh(core_axis_name="c", subcore_axis_name="s")

    @functools.partial(
        pl.kernel, mesh=mesh, name="expert_combine_sc",
        compiler_params=pltpu.CompilerParams(needs_layout_passes=False),
        out_type=jax.ShapeDtypeStruct((nb * seq, d), F32),
        scratch_types=[pltpu.VMEM_SHARED((seq, SC_SLAB), F32),
                       pltpu.VMEM((n_chunks, SC_IDX), jnp.int32),
                       pltpu.VMEM((SC_IDX, SC_SLAB), F32),
                       pltpu.VMEM((SC_ZROWS, SC_SLAB), F32)])
    def combine(ye_hbm, tok_hbm, out_hbm, acc_sh, tok_v, rows_v, zero_v):
        core = lax.axis_index("c")
        sub = lax.axis_index("s")

        @pl.loop(0, SC_ZROWS)
        def _(r):
            for l0 in range(0, SC_SLAB, SC_LANES):
                zero_v[r, pl.ds(l0, SC_LANES)] = jnp.zeros((SC_LANES,), F32)

        for bb in range(batches_per_core):
            batch = core * batches_per_core + bb
            pair = batch * N_EXPERTS + sub
            pltpu.sync_copy(tok_hbm.at[pair], tok_v)

            @pl.loop(0, d // SC_SLAB)
            def _(slab):
                cols = pl.ds(pl.multiple_of(slab * SC_SLAB, SC_SLAB), SC_SLAB)

                @pl.loop(0, own_rows, step=SC_ZROWS)
                def _(r0):
                    pltpu.sync_copy(zero_v, acc_sh.at[pl.ds(sub * own_rows + r0, SC_ZROWS)])

                plsc.subcore_barrier()

                @pl.loop(0, n_chunks)
                def _(j):
                    pltpu.sync_copy(ye_hbm.at[pl.ds(pair * cap + j * SC_IDX, SC_IDX), cols], rows_v)
                    pltpu.sync_copy(rows_v, acc_sh.at[tok_v.at[j]], add=True)

                plsc.subcore_barrier()
                pltpu.sync_copy(acc_sh.at[pl.ds(sub * own_rows, own_rows)],
                                out_hbm.at[pl.ds(batch * seq + sub * own_rows, own_rows), cols])

    return combine(ye_flat, tok3)


TN = 512


def _resln_kernel(x_ref, y_ref, g_ref, b_ref, o_ref, *, alpha):
    o_ref[...] = _standardize(alpha * x_ref[...] + y_ref[...]) * g_ref[...] + b_ref[...]


def _resln_call(x2d, y2d, g, bta, alpha):
    n, d = x2d.shape
    tok = pl.BlockSpec((TN, d), lambda i: (i, 0))
    vec = pl.BlockSpec((1, d), lambda i: (0, 0))
    return pl.pallas_call(
        functools.partial(_resln_kernel, alpha=alpha), name="residual_layernorm",
        grid=(n // TN,), in_specs=[tok, tok, vec, vec], out_specs=tok,
        out_shape=jax.ShapeDtypeStruct((n, d), F32),
        compiler_params=_cparams(1),
    )(x2d, y2d, g, bta)


def _expert_choice_ffn(x1, aff_t, w1, w3, w2):
    b, s, d = x1.shape
    cap = EC_FACTOR * s // N_EXPERTS
    slot = _select_call(aff_t, cap)
    xs, tok, gate = _sc_dispatch_call(x1.reshape(b * s, d), slot.reshape(b * N_EXPERTS, s),
                                      aff_t.reshape(b * N_EXPERTS, s), s, cap)
    ye = _expert_call(xs.reshape(b, N_EXPERTS, cap, d), gate.reshape(b, N_EXPERTS, cap, 1), w1, w3, w2)
    out = _sc_combine_call(ye.reshape(b * N_EXPERTS * cap, d),
                           tok.reshape(b * N_EXPERTS, cap // SC_IDX, SC_IDX), s)
    return out.reshape(b, s, d)


def _pack_pool(pool_w):
    g, gd, _ = pool_w.shape
    out = jnp.zeros((g * gd, g * gd), F32)
    for i in range(g):
        out = out.at[i * gd:(i + 1) * gd, i * gd:(i + 1) * gd].set(pool_w[i])
    return out.astype(BF16)


def _layer(x, alpha, bias_tiles, w_in, b_in, gm_ln_g, gm_ws, gm_bs, ml_conv, ml_fbias, ml_norm_g,
           pool_w, pool_scale, w_branch, w_out, ln1_g, ln1_b, w_router, w_e1, w_e3, w_e2, ln2_g, ln2_b):
    b, s, d = x.shape
    n_small = 2576
    w_cat, b_cat = _pack_inproj_weights(w_in, b_in)
    wscat, bsfull = _pack_gmlp(gm_ws, gm_bs)
    ya, qkv, cqk, cv, co, dx, ig, fg = _inproj_call(x.reshape(b * s, d), w_cat, b_cat, gm_ln_g[None],
                                                     wscat, bsfull)
    r3 = lambda t: t.reshape(b, s, t.shape[-1])
    qkv3 = r3(qkv)
    o_list, l_list = [], []
    for (window, dil), bias in zip(DIL_PATTERNS, bias_tiles):
        o, lse = _attn_call(qkv3, bias, dil)
        o_list.append(o)
        l_list.append(lse)
    hf, hb = _mlstm_branch(r3(cqk), r3(cv), r3(ig), r3(fg), ml_conv, ml_fbias)
    yd = _pool_call(r3(dx), _pack_pool(pool_w), pool_scale[None])
    x1, aff_t = _merge_call(
        x, r3(ya), o_list, l_list, hf, hb, r3(co), yd,
        w_in[:, n_small:].astype(BF16), b_in[None, n_small:], w_branch.astype(BF16), w_out.astype(BF16),
        ml_norm_g[None], ln1_g[None], ln1_b[None], jnp.transpose(w_router).astype(BF16), alpha)
    ffn = _expert_choice_ffn(x1, aff_t, w_e1.astype(BF16), w_e3.astype(BF16), w_e2.astype(BF16))
    x2 = _resln_call(x1.reshape(b * s, d), ffn.reshape(b * s, d), ln2_g[None], ln2_b[None], alpha)
    return x2.reshape(b, s, d)


def kernel(x, w_in, b_in, gm_ln_g, gm_ws, gm_bs, rel_bias, ml_conv, ml_fbias, ml_norm_g, pool_w,
           pool_scale, w_branch, w_out, ln1_g, ln1_b, w_router, w_e1, w_e3, w_e2, ln2_g, ln2_b):
    depth = w_in.shape[0]
    alpha = (2 * depth) ** 0.25
    bias_tiles = [_attn_bias_tile(rel_bias, window, dil) for window, dil in DIL_PATTERNS]
    for l in range(depth):
        x = _layer(x, alpha, bias_tiles, w_in[l], b_in[l], gm_ln_g[l], gm_ws[l], gm_bs[l], ml_conv[l],
                   ml_fbias[l], ml_norm_g[l], pool_w[l], pool_scale[l], w_branch[l], w_out[l],
                   ln1_g[l], ln1_b[l], w_router[l], w_e1[l], w_e3[l], w_e2[l], ln2_g[l], ln2_b[l])
    return x
```

```python
import functools
import math

import jax
import jax.numpy as jnp
import numpy as np
from jax import lax
from jax.experimental import pallas as pl
from jax.experimental.pallas import tpu as pltpu
from jax.experimental.pallas import tpu_sc as plsc

F32 = jnp.float32
BF16 = jnp.bfloat16

D_MODEL = 1024
MIX_W = 256
N_BRANCH = 4
GM_CHUNK = 128
GM_GROUPS = 4
ATT_HEADS = 4
ATT_HD = 64
DIL_PATTERNS = ((128, 1), (512, 4), (2048, 16))
ATT_BLOCK = 64
REL_BUCKETS = 32
REL_MAX_DIST = 1024
ML_HEADS = 4
ML_HD = 64
ML_CHUNK = 64
POOL_WINDOWS = (2, 4, 8, 16)
N_EXPERTS = 16
EXPERT_FF = 1024
EC_FACTOR = 2
LN_EPS = 1e-5
NEG_BIG = -1e30

V7X_VMEM_LIMIT = 56 * 1024 * 1024
LANES = 128
HALO = 8


def _cparams(n_grid, vmem=V7X_VMEM_LIMIT):
    return pltpu.CompilerParams(dimension_semantics=("arbitrary",) * n_grid,
                                vmem_limit_bytes=vmem)


def _standardize(xf):
    mu = jnp.mean(xf, axis=-1, keepdims=True)
    var = jnp.mean(jnp.square(xf - mu), axis=-1, keepdims=True)
    return (xf - mu) * lax.rsqrt(var + LN_EPS)


TA = 512
A_COLS = 2560 + 2 * LANES


def _inproj_kernel(x_ref, w_ref, b_ref, lng_ref, wscat_ref, bsfull_ref,
                   ya_ref, qkv1_ref, qkv4_ref, qkv16_ref, cqk_ref, cv_ref, co_ref, dx_ref, ig_ref, fg_ref,
                   qkv_scr):
    xb = x_ref[...].astype(BF16)
    h = jnp.dot(xb, w_ref[...], preferred_element_type=F32) + b_ref[...]
    qkv1_ref[0, 0] = h[:, 512:1280].astype(BF16)
    for c in range(768 // LANES):
        qkv_scr[c] = h[:, 512 + c * LANES:512 + (c + 1) * LANES]
    for (_, dil), out_ref in zip(DIL_PATTERNS[1:], (qkv4_ref, qkv16_ref)):
        for r in range(dil):
            for c in range(768 // LANES):
                out_ref[0, r, :, c * LANES:(c + 1) * LANES] = (
                    qkv_scr[c, pl.ds(r, TA // dil, stride=dil), :].astype(BF16))
    cqk_ref[...] = h[:, 1280:1792]
    cv_ref[...] = h[:, 1792:2048].astype(BF16)
    co_ref[...] = h[:, 2048:2304]
    dx_ref[...] = h[:, 2304:2560]
    ig_ref[...] = h[:, 2560:2688]
    fg_ref[...] = h[:, 2688:2816]
    u = jax.nn.gelu(h[:, 0:256])
    v = jax.nn.gelu(h[:, 256:512])
    vn = _standardize(v) * lng_ref[...]
    lane_grp = lax.broadcasted_iota(jnp.int32, (GM_CHUNK, MIX_W), 1) // (MIX_W // GM_GROUPS)
    for c in range(TA // GM_CHUNK):
        vc = vn[c * GM_CHUNK:(c + 1) * GM_CHUNK]
        stacked = jnp.concatenate(
            [jnp.where(lane_grp == g, vc, 0.0).astype(BF16) for g in range(GM_GROUPS)], axis=0)
        mixed = jnp.dot(wscat_ref[...], stacked, preferred_element_type=F32) + bsfull_ref[...]
        ya_ref[c * GM_CHUNK:(c + 1) * GM_CHUNK, :] = (
            u[c * GM_CHUNK:(c + 1) * GM_CHUNK] * mixed).astype(BF16)


def _inproj_call(x2d, w_cat, b_cat, lng, wscat, bsfull, batch):
    n = x2d.shape[0]
    seq = n // batch
    tpb = seq // TA
    tok = lambda w: pl.BlockSpec((TA, w), lambda i: (i, 0))
    const = lambda s: pl.BlockSpec(s, lambda i: (0,) * len(s))
    regrouped = lambda dil: pl.BlockSpec((1, dil, TA // dil, 768), lambda i: (i // tpb, 0, i % tpb, 0))
    out_shape = (
        jax.ShapeDtypeStruct((n, 256), BF16),
    ) + tuple(jax.ShapeDtypeStruct((batch, dil, seq // dil, 768), BF16)
              for _, dil in DIL_PATTERNS) + (
        jax.ShapeDtypeStruct((n, 512), F32),
        jax.ShapeDtypeStruct((n, 256), BF16),
        jax.ShapeDtypeStruct((n, 256), F32),
        jax.ShapeDtypeStruct((n, 256), F32),
        jax.ShapeDtypeStruct((n, LANES), F32),
        jax.ShapeDtypeStruct((n, LANES), F32),
    )
    return pl.pallas_call(
        _inproj_kernel, name="inproj_gmlp",
        grid=(n // TA,),
        in_specs=[tok(D_MODEL), const((D_MODEL, A_COLS)), const((1, A_COLS)), const((1, MIX_W)),
                  const((GM_CHUNK, GM_GROUPS * GM_CHUNK)), const((GM_CHUNK, MIX_W))],
        out_specs=(tok(256),) + tuple(regrouped(dil) for _, dil in DIL_PATTERNS)
                  + (tok(512), tok(256), tok(256), tok(256), tok(LANES), tok(LANES)),
        out_shape=out_shape,
        scratch_shapes=[pltpu.VMEM((768 // LANES, TA, LANES), F32)],
        compiler_params=_cparams(1),
    )(x2d, w_cat, b_cat, lng, wscat, bsfull)


def _pack_inproj_weights(w_in, b_in):
    pad = lambda a: jnp.pad(a, ((0, 0), (0, LANES - 8)))
    w_cat = jnp.concatenate([w_in[:, 0:2304], w_in[:, 2320:2576],
                             pad(w_in[:, 2304:2312]), pad(w_in[:, 2312:2320])], axis=1)
    b2 = b_in[None, :]
    b_cat = jnp.concatenate([b2[:, 0:2304], b2[:, 2320:2576],
                             pad(b2[:, 2304:2312]), pad(b2[:, 2312:2320])], axis=1)
    return w_cat.astype(BF16), b_cat


def _pack_gmlp(gm_ws, gm_bs):
    wscat = jnp.transpose(gm_ws, (1, 0, 2)).reshape(GM_CHUNK, GM_GROUPS * GM_CHUNK).astype(BF16)
    bsfull = jnp.repeat(jnp.transpose(gm_bs), MIX_W // GM_GROUPS, axis=1)
    return wscat, bsfull


def _halo_specs(t, width, n_tiles):
    r = t // HALO
    main = pl.BlockSpec((1, t, width), lambda b, i: (b, i, 0))
    prev = pl.BlockSpec((1, HALO, width), lambda b, i: (b, jnp.maximum(i * r - 1, 0), 0))
    nxt = pl.BlockSpec((1, HALO, width), lambda b, i: (b, jnp.minimum((i + 1) * r, n_tiles * r - 1), 0))
    return main, prev, nxt


def _fill_halo_scratch(buf, x_ref, p_ref, n_ref, t):
    i = pl.program_id(1)
    last = pl.num_programs(1) - 1
    buf[0:HALO, :] = jnp.where(i > 0, p_ref[0], 0.0)
    buf[HALO:HALO + t, :] = x_ref[0]
    buf[HALO + t:2 * HALO + t, :] = jnp.where(i < last, n_ref[0], 0.0)


TP = 512


def _pool_kernel(x_ref, p_ref, n_ref, w_ref, sc_ref, o_ref, buf):
    _fill_halo_scratch(buf, x_ref, p_ref, n_ref, TP)
    seq = pl.num_programs(1) * TP
    pos = pl.program_id(1) * TP + lax.broadcasted_iota(jnp.int32, (TP, 1), 0)
    lane_grp = lax.broadcasted_iota(jnp.int32, (TP, MIX_W), 1) // (MIX_W // len(POOL_WINDOWS))
    x0 = buf[HALO:HALO + TP, :]
    pooled = jnp.zeros((TP, MIX_W), F32)
    acc = None
    half_done = 0
    for gi, win in enumerate(POOL_WINDOWS):
        half = win // 2
        for o in list(range(-half, -half_done)) + list(range(half_done, half)):
            term = buf[HALO + o:HALO + o + TP, :]
            acc = term if acc is None else acc + term
        half_done = half
        cnt = (jnp.minimum(pos + half, seq) - jnp.maximum(pos - half, 0)).astype(F32)
        pooled = jnp.where(lane_grp == gi, acc / cnt - x0, pooled)
    mixed = jnp.dot(pooled.astype(BF16), w_ref[...], preferred_element_type=F32)
    o_ref[0] = (mixed * sc_ref[...]).astype(BF16)


def _pool_call(dx, w_block, scale):
    b, s, _ = dx.shape
    nt = s // TP
    main, prev, nxt = _halo_specs(TP, MIX_W, nt)
    return pl.pallas_call(
        _pool_kernel, name="pool_mixer",
        grid=(b, nt),
        in_specs=[main, prev, nxt,
                  pl.BlockSpec((MIX_W, MIX_W), lambda b, i: (0, 0)),
                  pl.BlockSpec((1, MIX_W), lambda b, i: (0, 0))],
        out_specs=pl.BlockSpec((1, TP, MIX_W), lambda b, i: (b, i, 0)),
        out_shape=jax.ShapeDtypeStruct((b, s, MIX_W), BF16),
        scratch_shapes=[pltpu.VMEM((TP + 2 * HALO, MIX_W), F32)],
        compiler_params=_cparams(2),
    )(dx, dx, dx, w_block, scale)


TQ = 128
TQS = 512
TKEYS = TQ + 2 * ATT_BLOCK


def _attn_kernel(q_ref, kp_ref, km_ref, kn_ref, vp_ref, vm_ref, vn_ref, bias_ref, o_ref, lse_ref):
    i = pl.program_id(2)
    seq = pl.num_programs(2) * TQS
    q = q_ref[0, 0] * ATT_HD ** -0.5
    k = jnp.concatenate([kp_ref[0, 0], km_ref[0, 0], kn_ref[0, 0]], axis=0)
    v = jnp.concatenate([vp_ref[0, 0], vm_ref[0, 0], vn_ref[0, 0]], axis=0)
    lane = lax.broadcasted_iota(jnp.int32, (TQ, LANES), 1)
    for j in range(TQS // TQ):
        kpos = i * TQS + j * TQ - ATT_BLOCK + lax.broadcasted_iota(jnp.int32, (1, TKEYS), 1)
        kvalid = (kpos >= 0) & (kpos < seq)
        qrows = slice(j * TQ, (j + 1) * TQ)
        krows = slice(j * TQ, j * TQ + TKEYS)
        lse_tile = jnp.zeros((TQ, LANES), F32)
        for h in range(ATT_HEADS):
            sl = slice(h * ATT_HD, (h + 1) * ATT_HD)
            logits = lax.dot_general(q[qrows, sl], k[krows, sl], (((1,), (1,)), ((), ())),
                                     preferred_element_type=F32) + bias_ref[h]
            logits = jnp.where(kvalid, logits, NEG_BIG)
            m = jnp.max(logits, axis=-1, keepdims=True)
            p = jnp.exp(logits - m)
            ssum = jnp.sum(p, axis=-1, keepdims=True)
            o = jnp.dot(p.astype(BF16), v[krows, sl], preferred_element_type=F32) / ssum
            o_ref[0, 0, qrows, sl] = o
            lse_tile = jnp.where(lane == h, m + jnp.log(ssum), lse_tile)
        lse_ref[0, 0, qrows, :] = lse_tile


def _attn_call(qkv, bias):
    b, dil, l, _ = qkv.shape
    nt = l // TQS
    r64 = TQS // ATT_BLOCK
    main = lambda c: pl.BlockSpec((1, 1, TQS, MIX_W), lambda b, r, i: (b, r, i, c))
    prev = lambda c: pl.BlockSpec((1, 1, ATT_BLOCK, MIX_W),
                                  lambda b, r, i: (b, r, jnp.maximum(i * r64 - 1, 0), c))
    nxt = lambda c: pl.BlockSpec((1, 1, ATT_BLOCK, MIX_W),
                                 lambda b, r, i: (b, r, jnp.minimum((i + 1) * r64, nt * r64 - 1), c))
    return pl.pallas_call(
        _attn_kernel, name="band_attention",
        grid=(b, dil, nt),
        in_specs=[main(0), prev(1), main(1), nxt(1), prev(2), main(2), nxt(2),
                  pl.BlockSpec((ATT_HEADS, TQ, TKEYS), lambda b, r, i: (0, 0, 0))],
        out_specs=(pl.BlockSpec((1, 1, TQS, MIX_W), lambda b, r, i: (b, r, i, 0)),
                   pl.BlockSpec((1, 1, TQS, LANES), lambda b, r, i: (b, r, i, 0))),
        out_shape=(jax.ShapeDtypeStruct((b, dil, l, MIX_W), F32),
                   jax.ShapeDtypeStruct((b, dil, l, LANES), F32)),
        compiler_params=_cparams(3),
    )(qkv, qkv, qkv, qkv, qkv, qkv, qkv, bias)


def _t5_bucket_static(rel):
    half = REL_BUCKETS // 2
    max_exact = half // 2
    ret = np.where(rel > 0, half, 0)
    n = np.abs(rel)
    nf = np.maximum(n, 1).astype(np.float32)
    large = max_exact + (np.log(nf / np.float32(max_exact)) / np.float32(math.log(REL_MAX_DIST / max_exact))
                         * np.float32(half - max_exact)).astype(np.int32)
    large = np.minimum(large, half - 1)
    return ret + np.where(n < max_exact, n, large)


def _attn_bias_tile(rel_bias, window, dil):
    side = (window // 2) // dil
    rel = np.arange(TKEYS)[None, :] - ATT_BLOCK - np.arange(TQ)[:, None]
    onehot = jax.nn.one_hot(jnp.asarray(_t5_bucket_static(dil * rel), jnp.int32), REL_BUCKETS, dtype=F32)
    bias = jnp.einsum('qkr,rh->hqk', onehot, rel_bias, precision=lax.Precision.HIGHEST)
    return jnp.where(jnp.asarray(np.abs(rel) <= side)[None], bias, NEG_BIG)


TM = 512


def _mlprep_kernel(x_ref, p_ref, n_ref, v_ref, w_ref, q_out, kt_out, v_out, buf):
    _fill_halo_scratch(buf, x_ref, p_ref, n_ref, TM)
    conv = (buf[HALO - 1:HALO - 1 + TM, :] * w_ref[0:1, :] + buf[HALO:HALO + TM, :] * w_ref[1:2, :]
            + buf[HALO + 1:HALO + 1 + TM, :] * w_ref[2:3, :])
    qk = jax.nn.silu(conv)
    kt = jnp.transpose(qk[:, MIX_W:] * ML_HD ** -0.5)
    v = v_ref[0]
    ones_col = jnp.where(lax.broadcasted_iota(jnp.int32, (TM, ML_HD), 1) == 0, 1.0, 0.0).astype(BF16)
    for h in range(ML_HEADS):
        sl = slice(h * ML_HD, (h + 1) * ML_HD)
        q_out[0, h] = qk[:, sl].astype(BF16)
        kt_out[0, h] = kt[sl, :].astype(BF16)
        v_out[0, h] = jnp.concatenate([v[:, sl], ones_col], axis=1)


def _mlprep_call(cqk, cv, conv_w):
    b, s, _ = cqk.shape
    nt = s // TM
    main, prev, nxt = _halo_specs(TM, 2 * MIX_W, nt)
    return pl.pallas_call(
        _mlprep_kernel, name="mlstm_prep",
        grid=(b, nt),
        in_specs=[main, prev, nxt,
                  pl.BlockSpec((1, TM, MIX_W), lambda b, i: (b, i, 0)),
                  pl.BlockSpec((3, 2 * MIX_W), lambda b, i: (0, 0))],
        out_specs=(pl.BlockSpec((1, ML_HEADS, TM, ML_HD), lambda b, i: (b, 0, i, 0)),
                   pl.BlockSpec((1, ML_HEADS, ML_HD, TM), lambda b, i: (b, 0, 0, i)),
                   pl.BlockSpec((1, ML_HEADS, TM, 2 * ML_HD), lambda b, i: (b, 0, i, 0))),
        out_shape=(jax.ShapeDtypeStruct((b, ML_HEADS, s, ML_HD), BF16),
                   jax.ShapeDtypeStruct((b, ML_HEADS, ML_HD, s), BF16),
                   jax.ShapeDtypeStruct((b, ML_HEADS, s, 2 * ML_HD), BF16)),
        scratch_shapes=[pltpu.VMEM((TM + 2 * HALO, 2 * MIX_W), F32)],
        compiler_params=_cparams(2),
    )(cqk, cqk, cqk, cv, conv_w)


GP_BLK = 512


def _chunk_scan(x, fwd_lane, t_in_chunk, op, ident):
    n = x.shape[0]
    shift = 1
    while shift < ML_CHUNK:
        down = pltpu.roll(x, shift, 0)
        up = pltpu.roll(x, n - shift, 0)
        nb = jnp.where(fwd_lane,
                       jnp.where(t_in_chunk >= shift, down, ident),
                       jnp.where(t_in_chunk < ML_CHUNK - shift, up, ident))
        x = op(x, nb)
        shift *= 2
    return x


def _gatescan_kernel(ig_ref, fg_ref, fb_ref, b_ref, a_ref, cm_ref, g_ref, amax_ref):
    lane = lax.broadcasted_iota(jnp.int32, (1, LANES), 1)
    fwd_lane = lane < ML_HEADS
    t_in_chunk = lax.broadcasted_iota(jnp.int32, (GP_BLK, 1), 0) % ML_CHUNK
    z = fg_ref[0] + fb_ref[...]
    lf = jnp.minimum(z, 0.0) - jnp.log1p(jnp.exp(-jnp.abs(z)))
    b = _chunk_scan(lf, fwd_lane, t_in_chunk, jnp.add, 0.0)
    a = ig_ref[0] - b
    b_ref[0] = b
    a_ref[0] = a
    cm_ref[0] = _chunk_scan(a, fwd_lane, t_in_chunk, jnp.maximum, -jnp.inf)
    cpb = GP_BLK // ML_CHUNK
    last = pl.ds(ML_CHUNK - 1, cpb, stride=ML_CHUNK)
    first = pl.ds(0, cpb, stride=ML_CHUNK)
    g_ref[0] = jnp.where(fwd_lane, b_ref[0, last, :], b_ref[0, first, :])
    amax_ref[0] = jnp.where(fwd_lane, cm_ref[0, last, :], cm_ref[0, first, :])


def _gateout_kernel(b_ref, a_ref, cm_ref, g_ref, amax_ref, ws_ref, m_ref, iw_ref, en_ref, dec_ref,
                    mch_s, mlast_s):
    nc = g_ref.shape[1]
    j = pl.program_id(1)
    fwd_lane = lax.broadcasted_iota(jnp.int32, (1, LANES), 1) < ML_HEADS

    @pl.when(j == 0)
    def _():
        def m_step(i, carry):
            mf, mb = carry
            cf = pl.ds(i, 1)
            cb = pl.ds(nc - 1 - i, 1)
            mch_s[cf, :] = jnp.where(fwd_lane, mf, mch_s[cf, :])
            mch_s[cb, :] = jnp.where(fwd_lane, mch_s[cb, :], mb)
            mf = g_ref[0, cf, :] + jnp.maximum(mf, amax_ref[0, cf, :])
            mb = g_ref[0, cb, :] + jnp.maximum(mb, amax_ref[0, cb, :])
            return mf, mb

        mch_s[...] = jnp.zeros((nc, LANES), F32)
        zero = jnp.zeros((1, LANES), F32)
        lax.fori_loop(0, nc, m_step, (zero, zero))
        mlast = jnp.maximum(amax_ref[0], mch_s[...])
        mlast_s[...] = mlast
        dec_ref[0] = jnp.exp(mch_s[...] - mlast)

    cpb = GP_BLK // ML_CHUNK
    crow = pl.ds(pl.multiple_of(j * cpb, cpb), cpb)
    expand = lambda t: jnp.broadcast_to(t[:, None, :], (cpb, ML_CHUNK, LANES)).reshape(GP_BLK, LANES)
    m_tok = expand(mch_s[crow, :])
    mlast_tok = expand(mlast_s[crow, :])
    mt = jnp.maximum(cm_ref[0], m_tok)
    m_ref[0] = mt
    iw_ref[0] = jnp.exp(m_tok - mt)
    en_ref[0] = jnp.exp(-(b_ref[0] + mt))
    ws_ref[0] = jnp.exp(a_ref[0] - mlast_tok)


def _gateprep_call(ig, fg, fbias_row):
    b, s, _ = ig.shape
    nc = s // ML_CHUNK
    cpb = GP_BLK // ML_CHUNK
    tok = pl.BlockSpec((1, GP_BLK, LANES), lambda i, j: (i, j, 0))
    tok_shape = jax.ShapeDtypeStruct((b, s, LANES), F32)
    chunk_shape = jax.ShapeDtypeStruct((b, nc, LANES), F32)
    chunk_tile = pl.BlockSpec((1, cpb, LANES), lambda i, j: (i, j, 0))
    chunk_all = pl.BlockSpec((1, nc, LANES), lambda i, j: (i, 0, 0))
    bcum, a, cm, g, amax = pl.pallas_call(
        _gatescan_kernel, name="mlstm_gate_scan",
        grid=(b, s // GP_BLK),
        in_specs=[tok, tok, pl.BlockSpec((1, LANES), lambda i, j: (0, 0))],
        out_specs=(tok, tok, tok, chunk_tile, chunk_tile),
        out_shape=(tok_shape,) * 3 + (chunk_shape,) * 2,
        compiler_params=_cparams(2),
    )(ig, fg, fbias_row)
    ws, m_col, iw_col, en_col, decay = pl.pallas_call(
        _gateout_kernel, name="mlstm_gate_out",
        grid=(b, s // GP_BLK),
        in_specs=[tok, tok, tok, chunk_all, chunk_all],
        out_specs=(tok, tok, tok, tok, chunk_all),
        out_shape=(tok_shape,) * 4 + (chunk_shape,),
        scratch_shapes=[pltpu.VMEM((nc, LANES), F32)] * 2,
        compiler_params=_cparams(2),
    )(bcum, a, cm, g, amax)
    return a, ws, m_col, iw_col, en_col, decay


TE = 512
PAIR = 2 * ML_CHUNK


def _mlstm_chunk(state, d, h, c, qc, ktc, vc, a_row, ws_row, m_col, iw_col, en_col, decay):
    t_idx = lax.broadcasted_iota(jnp.int32, (ML_CHUNK, ML_CHUNK), 0)
    s_idx = lax.broadcasted_iota(jnp.int32, (ML_CHUNK, ML_CHUNK), 1)
    tri = (s_idx <= t_idx) if d == 0 else (s_idx >= t_idx)
    s = jnp.dot(qc, ktc, preferred_element_type=F32)
    w = jnp.exp(jnp.where(tri, a_row - m_col, NEG_BIG))
    sw = s * w
    den_intra = jnp.sum(sw, axis=-1, keepdims=True)
    cst = state[d * ML_HEADS + h]
    intra = jnp.dot(sw.astype(BF16), vc, preferred_element_type=F32)
    inter = jnp.dot(qc, cst.astype(BF16), preferred_element_type=F32)
    tot = intra + iw_col * inter
    den = den_intra + iw_col * inter[:, ML_HD:ML_HD + 1]
    hh = tot[:, :ML_HD] / jnp.maximum(jnp.abs(den), en_col)
    upd = jnp.dot((ktc.astype(F32) * ws_row).astype(BF16), vc, preferred_element_type=F32)
    state[d * ML_HEADS + h] = decay * cst + upd
    return hh


def _mlstm_kernel(*refs):
    (qf, ktf, vf, af, wsf, mf, iwf, enf, qb, ktb, vb, ab, wsb, mb, iwb, enb, dec_ref,
     hf_ref, hb_ref, state) = refs
    i = pl.program_id(1)
    nt = pl.num_programs(1)

    @pl.when(i == 0)
    def _():
        state[...] = jnp.zeros(state.shape, F32)

    n_pairs = TE // PAIR
    cpt = TE // ML_CHUNK
    dirs = ((0, qf, ktf, vf, af, wsf, mf, iwf, enf, hf_ref), (1, qb, ktb, vb, ab, wsb, mb, iwb, enb, hb_ref))

    def pair_body(p, carry):
        for d, q_r, kt_r, v_r, a_r, ws_r, m_r, iw_r, en_r, out_r in dirs:
            pp = p if d == 0 else n_pairs - 1 - p
            tile = i if d == 0 else nt - 1 - i
            lanes = pl.ds(pl.multiple_of(pp * PAIR, PAIR), PAIR)
            for half in ((0, 1) if d == 0 else (1, 0)):
                rows = pl.ds(pl.multiple_of(pp * PAIR + half * ML_CHUNK, ML_CHUNK), ML_CHUNK)
                chunk = tile * cpt + pp * 2 + half
                hs = slice(half * ML_CHUNK, (half + 1) * ML_CHUNK)
                for h in range(ML_HEADS):
                    ch = d * ML_HEADS + h
                    hh = _mlstm_chunk(
                        state, d, h, chunk,
                        q_r[0, h, rows, :], kt_r[0, h, :, lanes][:, hs], v_r[0, h, rows, :],
                        a_r[0, ch:ch + 1, lanes][:, hs], ws_r[0, ch:ch + 1, lanes][:, hs],
                        m_r[0, rows, ch:ch + 1], iw_r[0, rows, ch:ch + 1], en_r[0, rows, ch:ch + 1],
                        dec_ref[0, pl.ds(chunk, 1), ch:ch + 1])
                    out_r[0, rows, h * ML_HD:(h + 1) * ML_HD] = hh
        return carry

    lax.fori_loop(0, n_pairs, pair_body, 0)


def _mlstm_call(q_hm, kt_hm, v_aug, a_row, ws_row, m_col, iw_col, en_col, decay):
    b, _, s, _ = q_hm.shape
    nt = s // TE
    nc = s // ML_CHUNK

    def specs(rev):
        ti = (lambda i: nt - 1 - i) if rev else (lambda i: i)
        return [
            pl.BlockSpec((1, ML_HEADS, TE, ML_HD), lambda b, i: (b, 0, ti(i), 0)),
            pl.BlockSpec((1, ML_HEADS, ML_HD, TE), lambda b, i: (b, 0, 0, ti(i))),
            pl.BlockSpec((1, ML_HEADS, TE, 2 * ML_HD), lambda b, i: (b, 0, ti(i), 0)),
            pl.BlockSpec((1, 2 * ML_HEADS, TE), lambda b, i: (b, 0, ti(i))),
            pl.BlockSpec((1, 2 * ML_HEADS, TE), lambda b, i: (b, 0, ti(i))),
            pl.BlockSpec((1, TE, LANES), lambda b, i: (b, ti(i), 0)),
            pl.BlockSpec((1, TE, LANES), lambda b, i: (b, ti(i), 0)),
            pl.BlockSpec((1, TE, LANES), lambda b, i: (b, ti(i), 0)),
        ]

    args = [q_hm, kt_hm, v_aug, a_row, ws_row, m_col, iw_col, en_col]
    out_f = pl.BlockSpec((1, TE, MIX_W), lambda b, i: (b, i, 0))
    out_b = pl.BlockSpec((1, TE, MIX_W), lambda b, i: (b, nt - 1 - i, 0))
    return pl.pallas_call(
        _mlstm_kernel, name="mlstm_scan",
        grid=(b, nt),
        in_specs=specs(False) + specs(True) + [pl.BlockSpec((1, nc, LANES), lambda b, i: (b, 0, 0))],
        out_specs=(out_f, out_b),
        out_shape=(jax.ShapeDtypeStruct((b, s, MIX_W), F32),) * 2,
        scratch_shapes=[pltpu.VMEM((2 * ML_HEADS, ML_HD, 2 * ML_HD), F32)],
        compiler_params=_cparams(2),
    )(*args, *args, decay)


def _mlstm_branch(cqk, cv, ig, fg, conv_w, fbias):
    q_hm, kt_hm, v_aug = _mlprep_call(cqk, cv, conv_w)
    fb_row = jnp.pad(fbias.reshape(1, 2 * ML_HEADS), ((0, 0), (0, LANES - 2 * ML_HEADS)))
    a, ws, m_col, iw_col, en_col, decay = _gateprep_call(ig, fg, fb_row)
    rows = lambda t: jnp.transpose(t[..., :2 * ML_HEADS], (0, 2, 1))
    return _mlstm_call(q_hm, kt_hm, v_aug, rows(a), rows(ws), m_col, iw_col, en_col, decay)


TF = 256


def _merge_kernel(x_ref, ya_ref, o1_ref, o2_ref, o3_ref, l1_ref, l2_ref, l3_ref, hf_ref, hb_ref,
                  co_ref, yd_ref, wg_ref, bg_ref, wbr_ref, wout_ref, mng_ref, lng_ref, lnb_ref,
                  wr_ref, x1_ref, aff_ref, o_scr, l_scr, *, alpha):
    x = x_ref[0]
    xb = x.astype(BF16)

    def natural_order(src_ref, scr):
        dil, width = src_ref.shape[1], src_ref.shape[3]
        if dil == 1:
            return src_ref[0, 0]
        for r in range(dil):
            for c in range(width // LANES):
                scr[c, pl.ds(r, TF // dil, stride=dil), :] = src_ref[0, r, :, c * LANES:(c + 1) * LANES]
        return jnp.concatenate([scr[c] for c in range(width // LANES)], axis=1)

    lane_head = lax.broadcasted_iota(jnp.int32, (TF, MIX_W), 1) // ML_HD
    l1, l2, l3 = [natural_order(r, l_scr.at[p]) for p, r in enumerate((l1_ref, l2_ref, l3_ref))]
    o1, o2, o3 = [natural_order(r, o_scr.at[p]) for p, r in enumerate((o1_ref, o2_ref, o3_ref))]
    lm = jnp.maximum(jnp.maximum(l1, l2), l3)
    e1, e2, e3 = jnp.exp(l1 - lm), jnp.exp(l2 - lm), jnp.exp(l3 - lm)
    inv = 1.0 / (e1 + e2 + e3)

    def per_head(w):
        out = jnp.zeros((TF, MIX_W), F32)
        for h in range(ATT_HEADS):
            out = jnp.where(lane_head == h, w[:, h:h + 1], out)
        return out

    y_b = per_head(e1 * inv) * o1 + per_head(e2 * inv) * o2 + per_head(e3 * inv) * o3
    hsum = hf_ref[0] + hb_ref[0]
    mu = jnp.zeros((TF, MIX_W), F32)
    for h in range(ML_HEADS):
        sel = lane_head == h
        mu = jnp.where(sel, jnp.sum(jnp.where(sel, hsum, 0.0), axis=-1, keepdims=True) / ML_HD, mu)
    cen = hsum - mu
    var = jnp.zeros((TF, MIX_W), F32)
    for h in range(ML_HEADS):
        sel = lane_head == h
        var = jnp.where(sel, jnp.sum(jnp.where(sel, cen * cen, 0.0), axis=-1, keepdims=True) / ML_HD, var)
    y_c = jax.nn.sigmoid(co_ref[0]) * (cen * lax.rsqrt(var + LN_EPS) * mng_ref[...])
    ys = (ya_ref[0], y_b.astype(BF16), y_c.astype(BF16), yd_ref[0])
    merged = jnp.zeros((TF, D_MODEL), F32)
    for n in range(N_BRANCH):
        cols = slice(n * D_MODEL, (n + 1) * D_MODEL)
        gate = jax.nn.sigmoid(jnp.dot(xb, wg_ref[:, cols], preferred_element_type=F32) + bg_ref[:, cols])
        merged = merged + gate * jnp.dot(ys[n], wbr_ref[n], preferred_element_type=F32)
    mix = jnp.dot(merged.astype(BF16), wout_ref[...], preferred_element_type=F32)
    x1 = _standardize(alpha * x + mix) * lng_ref[...] + lnb_ref[...]
    x1_ref[0] = x1
    logits = lax.dot_general(wr_ref[...], x1.astype(BF16), (((1,), (1,)), ((), ())),
                             preferred_element_type=F32)
    ex = jnp.exp(logits - jnp.max(logits, axis=0, keepdims=True))
    aff_ref[0] = ex / jnp.sum(ex, axis=0, keepdims=True)


def _merge_call(x, ya, o_list, l_list, hf, hb, co, yd, wg, bg, wbr, wout, mng, lng, lnb, wr_t, alpha):
    b, s, _ = x.shape
    tok = lambda w: pl.BlockSpec((1, TF, w), lambda b, i: (b, i, 0))
    grouped = lambda dil, w: pl.BlockSpec((1, dil, TF // dil, w), lambda b, i: (b, 0, i, 0))
    const = lambda shp: pl.BlockSpec(shp, lambda b, i: (0,) * len(shp))
    return pl.pallas_call(
        functools.partial(_merge_kernel, alpha=alpha), name="merge_ln_router",
        grid=(b, s // TF),
        in_specs=[tok(D_MODEL), tok(MIX_W)] + [grouped(dil, MIX_W) for _, dil in DIL_PATTERNS]
                 + [grouped(dil, LANES) for _, dil in DIL_PATTERNS] + [tok(MIX_W)] * 4
                 + [const((D_MODEL, N_BRANCH * D_MODEL)), const((1, N_BRANCH * D_MODEL)),
                    const((N_BRANCH, MIX_W, D_MODEL)), const((D_MODEL, D_MODEL)), const((1, MIX_W)),
                    const((1, D_MODEL)), const((1, D_MODEL)), const((N_EXPERTS, D_MODEL))],
        out_specs=(tok(D_MODEL), pl.BlockSpec((1, N_EXPERTS, TF), lambda b, i: (b, 0, i))),
        out_shape=(jax.ShapeDtypeStruct((b, s, D_MODEL), F32),
                   jax.ShapeDtypeStruct((b, N_EXPERTS, s), F32)),
        scratch_shapes=[pltpu.VMEM((len(DIL_PATTERNS), MIX_W // LANES, TF, LANES), F32),
                        pltpu.VMEM((len(DIL_PATTERNS), 1, TF, LANES), F32)],
        compiler_params=_cparams(2),
    )(x, ya, *o_list, *l_list, hf, hb, co, yd, wg, bg, wbr, wout, mng, lng, lnb, wr_t)


TT = 256


def _select_kernel(aff_ref, slot_ref, *, cap):
    s = aff_ref.shape[2]
    bits = pltpu.bitcast(aff_ref[0], jnp.int32)

    def bit_step(i, thr):
        cand = thr | jnp.left_shift(jnp.int32(1), 30 - i)
        cnt = jnp.sum((bits >= cand).astype(jnp.int32), axis=1, keepdims=True)
        return jnp.where(cnt >= cap, cand, thr)

    thr = lax.fori_loop(0, 31, bit_step, jnp.zeros((N_EXPERTS, 1), jnp.int32))
    gt = bits > thr
    eq = bits == thr
    need = (cap - jnp.sum(gt.astype(jnp.int32), axis=1, keepdims=True)).astype(F32)
    upper = (lax.broadcasted_iota(jnp.int32, (TT, TT), 0)
             <= lax.broadcasted_iota(jnp.int32, (TT, TT), 1)).astype(BF16)
    eq_before = jnp.zeros((N_EXPERTS, 1), F32)
    sel_before = jnp.zeros((N_EXPERTS, 1), F32)
    for j in range(s // TT):
        cols = slice(j * TT, (j + 1) * TT)
        eq_j = eq[:, cols]
        eq_incl = eq_before + jnp.dot(eq_j.astype(BF16), upper, preferred_element_type=F32)
        sel_j = gt[:, cols] | (eq_j & (eq_incl <= need))
        sel_f = sel_j.astype(F32)
        sel_incl = sel_before + jnp.dot(sel_f.astype(BF16), upper, preferred_element_type=F32)
        slot_ref[0, :, cols] = jnp.where(sel_j, sel_incl - 1.0, -1.0).astype(jnp.int32)
        eq_before = eq_incl[:, TT - 1:TT]
        sel_before = sel_incl[:, TT - 1:TT]


def _select_call(aff_t, cap):
    b, e, s = aff_t.shape
    return pl.pallas_call(
        functools.partial(_select_kernel, cap=cap), name="expert_choice_select",
        grid=(b,),
        in_specs=[pl.BlockSpec((1, e, s), lambda i: (i, 0, 0))],
        out_specs=pl.BlockSpec((1, e, s), lambda i: (i, 0, 0)),
        out_shape=jax.ShapeDtypeStruct((b, e, s), jnp.int32),
        compiler_params=_cparams(1),
    )(aff_t)


SC_LANES = 16
SC_ROWS = 64
SC_IDX = 128
SC_SLAB = 128
SC_ZROWS = 64
CF = 512


def _sc_dispatch_call(x_flat, slot2, aff2, seq, cap):
    n_pairs = slot2.shape[0]
    d = x_flat.shape[1]
    info = plsc.get_sparse_core_info()
    n_workers = info.num_cores * info.num_subcores
    assert n_pairs % n_workers == 0 and seq % SC_LANES == 0 and cap % SC_ROWS == 0
    pairs_per_worker = n_pairs // n_workers
    mesh = plsc.VectorSubcoreMesh(core_axis_name="c", subcore_axis_name="s")

    @functools.partial(
        pl.kernel, mesh=mesh, name="expert_dispatch_sc",
        compiler_params=pltpu.CompilerParams(needs_layout_passes=False),
        out_type=(jax.ShapeDtypeStruct((n_pairs * cap, d), F32),
                  jax.ShapeDtypeStruct((n_pairs, cap), jnp.int32),
                  jax.ShapeDtypeStruct((n_pairs, cap), F32)),
        scratch_types=[pltpu.VMEM((seq,), jnp.int32), pltpu.VMEM((seq,), F32),
                       pltpu.VMEM((cap,), jnp.int32), pltpu.VMEM((cap,), jnp.int32),
                       pltpu.VMEM((cap,), F32),
                       pltpu.VMEM((SC_ROWS, d), F32), pltpu.SemaphoreType.DMA])
    def dispatch(x_hbm, slot_hbm, aff_hbm, xs_hbm, tok_hbm, gate_hbm,
                 slot_v, aff_v, idx_v, tok_v, gate_v, rows_v, sem):
        worker = lax.axis_index("s") * info.num_cores + lax.axis_index("c")
        lane = lax.iota(jnp.int32, SC_LANES)
        for k in range(pairs_per_worker):
            pair = worker * pairs_per_worker + k
            row0 = (pair // N_EXPERTS) * seq
            pltpu.sync_copy(slot_hbm.at[pair], slot_v)
            pltpu.sync_copy(aff_hbm.at[pair], aff_v)

            @pl.loop(0, seq, step=SC_LANES)
            def _(t0):
                sv = slot_v[pl.ds(t0, SC_LANES)]
                picked = sv >= 0
                plsc.store_scatter(tok_v, [sv], t0 + lane, mask=picked)
                plsc.store_scatter(idx_v, [sv], row0 + t0 + lane, mask=picked)
                plsc.store_scatter(gate_v, [sv], aff_v[pl.ds(t0, SC_LANES)], mask=picked)

            pltpu.sync_copy(tok_v, tok_hbm.at[pair])
            pltpu.sync_copy(gate_v, gate_hbm.at[pair])

            @pl.loop(0, cap, step=SC_ROWS)
            def _(c0):
                pltpu.async_copy(x_hbm.at[idx_v.at[pl.ds(c0, SC_ROWS)]], rows_v, sem).wait()
                pltpu.sync_copy(rows_v, xs_hbm.at[pl.ds(pair * cap + c0, SC_ROWS)])

    return dispatch(x_flat, slot2, aff2)


def _expert_kernel(xs_ref, g_ref, w1_ref, w3_ref, w2_ref, ye_ref, w1_bf, w3_bf, w2_bf):
    @pl.when((pl.program_id(1) == 0) & (pl.program_id(2) == 0))
    def _():
        w1_bf[...] = w1_ref[0].astype(BF16)
        w3_bf[...] = w3_ref[0].astype(BF16)
        w2_bf[...] = w2_ref[0].astype(BF16)

    xs = xs_ref[0, 0].astype(BF16)
    hid = (jax.nn.silu(jnp.dot(xs, w1_bf[...], preferred_element_type=F32))
           * jnp.dot(xs, w3_bf[...], preferred_element_type=F32))
    ye_ref[0, 0] = jnp.dot(hid.astype(BF16), w2_bf[...], preferred_element_type=F32) * g_ref[0, 0]


def _expert_call(xs4, gate4, w1, w3, w2):
    b, e, cap, d = xs4.shape
    ff = w1.shape[2]
    rows = lambda w: pl.BlockSpec((1, 1, CF, w), lambda e, b, j: (b, e, j, 0))
    wspec = lambda r, c: pl.BlockSpec((1, r, c), lambda e, b, j: (e, 0, 0))
    return pl.pallas_call(
        _expert_kernel, name="expert_ffn",
        grid=(e, b, cap // CF),
        in_specs=[rows(d), rows(1), wspec(d, ff), wspec(d, ff), wspec(ff, d)],
        out_specs=rows(d),
        out_shape=jax.ShapeDtypeStruct((b, e, cap, d), F32),
        scratch_shapes=[pltpu.VMEM((d, ff), BF16), pltpu.VMEM((d, ff), BF16), pltpu.VMEM((ff, d), BF16)],
        compiler_params=_cparams(3),
    )(xs4, gate4, w1, w3, w2)


def _sc_combine_call(ye_flat, tok3, seq):
    n_pairs, n_chunks, _ = tok3.shape
    cap = n_chunks * SC_IDX
    d = ye_flat.shape[1]
    nb = n_pairs // N_EXPERTS
    info = plsc.get_sparse_core_info()
    assert info.num_subcores == N_EXPERTS and nb % info.num_cores == 0
    assert seq % (info.num_subcores * SC_ZROWS) == 0 and d % SC_SLAB == 0
    batches_per_core = nb // info.num_cores
    own_rows = seq // info.num_subcores
    mesh = plsc.VectorSubcoreMesh(core_axis_name="c", subcore_axis_name="s")

    @functools.partial(
        pl.kernel, mesh=mesh, name="expert_combine_sc",
        compiler_params=pltpu.CompilerParams(needs_layout_passes=False),
        out_type=jax.ShapeDtypeStruct((nb * seq, d), F32),
        scratch_types=[pltpu.VMEM_SHARED((seq, SC_SLAB), F32),
                       pltpu.VMEM((n_chunks, SC_IDX), jnp.int32),
                       pltpu.VMEM((SC_IDX, SC_SLAB), F32),
                       pltpu.VMEM((SC_ZROWS, SC_SLAB), F32)])
    def combine(ye_hbm, tok_hbm, out_hbm, acc_sh, tok_v, rows_v, zero_v):
        core = lax.axis_index("c")
        sub = lax.axis_index("s")

        @pl.loop(0, SC_ZROWS)
        def _(r):
            for l0 in range(0, SC_SLAB, SC_LANES):
                zero_v[r, pl.ds(l0, SC_LANES)] = jnp.zeros((SC_LANES,), F32)

        for bb in range(batches_per_core):
            batch = core * batches_per_core + bb
            pair = batch * N_EXPERTS + sub
            pltpu.sync_copy(tok_hbm.at[pair], tok_v)

            @pl.loop(0, d // SC_SLAB)
            def _(slab):
                cols = pl.ds(pl.multiple_of(slab * SC_SLAB, SC_SLAB), SC_SLAB)

                @pl.loop(0, own_rows, step=SC_ZROWS)
                def _(r0):
                    pltpu.sync_copy(zero_v, acc_sh.at[pl.ds(sub * own_rows + r0, SC_ZROWS)])

                plsc.subcore_barrier()

                @pl.loop(0, n_chunks)
                def _(j):
                    pltpu.sync_copy(ye_hbm.at[pl.ds(pair * cap + j * SC_IDX, SC_IDX), cols], rows_v)
                    pltpu.sync_copy(rows_v, acc_sh.at[tok_v.at[j]], add=True)

                plsc.subcore_barrier()
                pltpu.sync_copy(acc_sh.at[pl.ds(sub * own_rows, own_rows)],
                                out_hbm.at[pl.ds(batch * seq + sub * own_rows, own_rows), cols])

    return combine(ye_flat, tok3)


TN = 512


def _resln_kernel(x_ref, y_ref, g_ref, b_ref, o_ref, *, alpha):
    o_ref[...] = _standardize(alpha * x_ref[...] + y_ref[...]) * g_ref[...] + b_ref[...]


def _resln_call(x2d, y2d, g, bta, alpha):
    n, d = x2d.shape
    tok = pl.BlockSpec((TN, d), lambda i: (i, 0))
    vec = pl.BlockSpec((1, d), lambda i: (0, 0))
    return pl.pallas_call(
        functools.partial(_resln_kernel, alpha=alpha), name="residual_layernorm",
        grid=(n // TN,), in_specs=[tok, tok, vec, vec], out_specs=tok,
        out_shape=jax.ShapeDtypeStruct((n, d), F32),
        compiler_params=_cparams(1),
    )(x2d, y2d, g, bta)


def _expert_choice_ffn(x1, aff_t, w1, w3, w2):
    b, s, d = x1.shape
    cap = EC_FACTOR * s // N_EXPERTS
    slot = _select_call(aff_t, cap)
    xs, tok, gate = _sc_dispatch_call(x1.reshape(b * s, d), slot.reshape(b * N_EXPERTS, s),
                                      aff_t.reshape(b * N_EXPERTS, s), s, cap)
    ye = _expert_call(xs.reshape(b, N_EXPERTS, cap, d), gate.reshape(b, N_EXPERTS, cap, 1), w1, w3, w2)
    out = _sc_combine_call(ye.reshape(b * N_EXPERTS * cap, d),
                           tok.reshape(b * N_EXPERTS, cap // SC_IDX, SC_IDX), s)
    return out.reshape(b, s, d)


def _pack_pool(pool_w):
    g, gd, _ = pool_w.shape
    out = jnp.zeros((g * gd, g * gd), F32)
    for i in range(g):
        out = out.at[i * gd:(i + 1) * gd, i * gd:(i + 1) * gd].set(pool_w[i])
    return out.astype(BF16)


def _layer(x, alpha, bias_tiles, w_in, b_in, gm_ln_g, gm_ws, gm_bs, ml_conv, ml_fbias, ml_norm_g,
           pool_w, pool_scale, w_branch, w_out, ln1_g, ln1_b, w_router, w_e1, w_e3, w_e2, ln2_g, ln2_b):
    b, s, d = x.shape
    n_small = 2576
    w_cat, b_cat = _pack_inproj_weights(w_in, b_in)
    wscat, bsfull = _pack_gmlp(gm_ws, gm_bs)
    ya, qkv1, qkv4, qkv16, cqk, cv, co, dx, ig, fg = _inproj_call(
        x.reshape(b * s, d), w_cat, b_cat, gm_ln_g[None], wscat, bsfull, b)
    r3 = lambda t: t.reshape(b, s, t.shape[-1])
    o_list, l_list = [], []
    for qkv, bias in zip((qkv1, qkv4, qkv16), bias_tiles):
        o, lse = _attn_call(qkv, bias)
        o_list.append(o)
        l_list.append(lse)
    hf, hb = _mlstm_branch(r3(cqk), r3(cv), r3(ig), r3(fg), ml_conv, ml_fbias)
    yd = _pool_call(r3(dx), _pack_pool(pool_w), pool_scale[None])
    x1, aff_t = _merge_call(
        x, r3(ya), o_list, l_list, hf, hb, r3(co), yd,
        w_in[:, n_small:].astype(BF16), b_in[None, n_small:], w_branch.astype(BF16), w_out.astype(BF16),
        ml_norm_g[None], ln1_g[None], ln1_b[None], jnp.transpose(w_router).astype(BF16), alpha)
    ffn = _expert_choice_ffn(x1, aff_t, w_e1, w_e3, w_e2)
    x2 = _resln_call(x1.reshape(b * s, d), ffn.reshape(b * s, d), ln2_g[None], ln2_b[None], alpha)
    return x2.reshape(b, s, d)


def kernel(x, w_in, b_in, gm_ln_g, gm_ws, gm_bs, rel_bias, ml_conv, ml_fbias, ml_norm_g, pool_w,
           pool_scale, w_branch, w_out, ln1_g, ln1_b, w_router, w_e1, w_e3, w_e2, ln2_g, ln2_b):
    depth = w_in.shape[0]
    alpha = (2 * depth) ** 0.25
    bias_tiles = [_attn_bias_tile(rel_bias, window, dil) for window, dil in DIL_PATTERNS]
    for l in range(depth):
        x = _layer(x, alpha, bias_tiles, w_in[l], b_in[l], gm_ln_g[l], gm_ws[l], gm_bs[l], ml_conv[l],
                   ml_fbias[l], ml_norm_g[l], pool_w[l], pool_scale[l], w_branch[l], w_out[l],
                   ln1_g[l], ln1_b[l], w_router[l], w_e1[l], w_e3[l], w_e2[l], ln2_g[l], ln2_b[l])
    return x
```

```python
import functools
import math

import jax
import jax.numpy as jnp
import numpy as np
from jax import lax
from jax.experimental import pallas as pl
from jax.experimental.pallas import tpu as pltpu
from jax.experimental.pallas import tpu_sc as plsc

F32 = jnp.float32
BF16 = jnp.bfloat16

D_MODEL = 1024
MIX_W = 256
N_BRANCH = 4
GM_CHUNK = 128
GM_GROUPS = 4
ATT_HEADS = 4
ATT_HD = 64
DIL_PATTERNS = ((128, 1), (512, 4), (2048, 16))
ATT_BLOCK = 64
REL_BUCKETS = 32
REL_MAX_DIST = 1024
ML_HEADS = 4
ML_HD = 64
ML_CHUNK = 64
POOL_WINDOWS = (2, 4, 8, 16)
N_EXPERTS = 16
EXPERT_FF = 1024
EC_FACTOR = 2
LN_EPS = 1e-5
NEG_BIG = -1e30

V7X_VMEM_LIMIT = 56 * 1024 * 1024
LANES = 128
HALO = 8


def _cparams(n_grid, vmem=V7X_VMEM_LIMIT):
    return pltpu.CompilerParams(dimension_semantics=("arbitrary",) * n_grid,
                                vmem_limit_bytes=vmem)


def _standardize(xf):
    mu = jnp.mean(xf, axis=-1, keepdims=True)
    var = jnp.mean(jnp.square(xf - mu), axis=-1, keepdims=True)
    return (xf - mu) * lax.rsqrt(var + LN_EPS)


TA = 512
A_COLS = 2560 + 2 * LANES


def _inproj_kernel(x_ref, w_ref, b_ref, lng_ref, wscat_ref, bsfull_ref,
                   ya_ref, qkv1_ref, qkv4_ref, qkv16_ref, cqk_ref, cv_ref, co_ref, dx_ref, ig_ref, fg_ref,
                   qkv_scr):
    xb = x_ref[...].astype(BF16)
    h = jnp.dot(xb, w_ref[...], preferred_element_type=F32) + b_ref[...]
    qkv1_ref[0, 0] = h[:, 512:1280].astype(BF16)
    for c in range(768 // LANES):
        qkv_scr[c] = h[:, 512 + c * LANES:512 + (c + 1) * LANES]
    for (_, dil), out_ref in zip(DIL_PATTERNS[1:], (qkv4_ref, qkv16_ref)):
        for r in range(dil):
            for c in range(768 // LANES):
                out_ref[0, r, :, c * LANES:(c + 1) * LANES] = (
                    qkv_scr[c, pl.ds(r, TA // dil, stride=dil), :].astype(BF16))
    cqk_ref[...] = h[:, 1280:1792]
    cv_ref[...] = h[:, 1792:2048].astype(BF16)
    co_ref[...] = h[:, 2048:2304]
    dx_ref[...] = h[:, 2304:2560]
    ig_ref[...] = h[:, 2560:2688]
    fg_ref[...] = h[:, 2688:2816]
    u = jax.nn.gelu(h[:, 0:256])
    v = jax.nn.gelu(h[:, 256:512])
    vn = _standardize(v) * lng_ref[...]
    lane_grp = lax.broadcasted_iota(jnp.int32, (GM_CHUNK, MIX_W), 1) // (MIX_W // GM_GROUPS)
    for c in range(TA // GM_CHUNK):
        vc = vn[c * GM_CHUNK:(c + 1) * GM_CHUNK]
        stacked = jnp.concatenate(
            [jnp.where(lane_grp == g, vc, 0.0).astype(BF16) for g in range(GM_GROUPS)], axis=0)
        mixed = jnp.dot(wscat_ref[...], stacked, preferred_element_type=F32) + bsfull_ref[...]
        ya_ref[c * GM_CHUNK:(c + 1) * GM_CHUNK, :] = (
            u[c * GM_CHUNK:(c + 1) * GM_CHUNK] * mixed).astype(BF16)


def _inproj_call(x2d, w_cat, b_cat, lng, wscat, bsfull, batch):
    n = x2d.shape[0]
    seq = n // batch
    tpb = seq // TA
    tok = lambda w: pl.BlockSpec((TA, w), lambda i: (i, 0))
    const = lambda s: pl.BlockSpec(s, lambda i: (0,) * len(s))
    regrouped = lambda dil: pl.BlockSpec((1, dil, TA // dil, 768), lambda i: (i // tpb, 0, i % tpb, 0))
    out_shape = (
        jax.ShapeDtypeStruct((n, 256), BF16),
    ) + tuple(jax.ShapeDtypeStruct((batch, dil, seq // dil, 768), BF16)
              for _, dil in DIL_PATTERNS) + (
        jax.ShapeDtypeStruct((n, 512), F32),
        jax.ShapeDtypeStruct((n, 256), BF16),
        jax.ShapeDtypeStruct((n, 256), F32),
        jax.ShapeDtypeStruct((n, 256), F32),
        jax.ShapeDtypeStruct((n, LANES), F32),
        jax.ShapeDtypeStruct((n, LANES), F32),
    )
    return pl.pallas_call(
        _inproj_kernel, name="inproj_gmlp",
        grid=(n // TA,),
        in_specs=[tok(D_MODEL), const((D_MODEL, A_COLS)), const((1, A_COLS)), const((1, MIX_W)),
                  const((GM_CHUNK, GM_GROUPS * GM_CHUNK)), const((GM_CHUNK, MIX_W))],
        out_specs=(tok(256),) + tuple(regrouped(dil) for _, dil in DIL_PATTERNS)
                  + (tok(512), tok(256), tok(256), tok(256), tok(LANES), tok(LANES)),
        out_shape=out_shape,
        scratch_shapes=[pltpu.VMEM((768 // LANES, TA, LANES), F32)],
        compiler_params=_cparams(1),
    )(x2d, w_cat, b_cat, lng, wscat, bsfull)


def _pack_inproj_weights(w_in, b_in):
    pad = lambda a: jnp.pad(a, ((0, 0), (0, LANES - 8)))
    w_cat = jnp.concatenate([w_in[:, 0:2304], w_in[:, 2320:2576],
                             pad(w_in[:, 2304:2312]), pad(w_in[:, 2312:2320])], axis=1)
    b2 = b_in[None, :]
    b_cat = jnp.concatenate([b2[:, 0:2304], b2[:, 2320:2576],
                             pad(b2[:, 2304:2312]), pad(b2[:, 2312:2320])], axis=1)
    return w_cat.astype(BF16), b_cat


def _pack_gmlp(gm_ws, gm_bs):
    wscat = jnp.transpose(gm_ws, (1, 0, 2)).reshape(GM_CHUNK, GM_GROUPS * GM_CHUNK).astype(BF16)
    bsfull = jnp.repeat(jnp.transpose(gm_bs), MIX_W // GM_GROUPS, axis=1)
    return wscat, bsfull


def _halo_specs(t, width, n_tiles):
    r = t // HALO
    main = pl.BlockSpec((1, t, width), lambda b, i: (b, i, 0))
    prev = pl.BlockSpec((1, HALO, width), lambda b, i: (b, jnp.maximum(i * r - 1, 0), 0))
    nxt = pl.BlockSpec((1, HALO, width), lambda b, i: (b, jnp.minimum((i + 1) * r, n_tiles * r - 1), 0))
    return main, prev, nxt


def _fill_halo_scratch(buf, x_ref, p_ref, n_ref, t):
    i = pl.program_id(1)
    last = pl.num_programs(1) - 1
    buf[0:HALO, :] = jnp.where(i > 0, p_ref[0], 0.0)
    buf[HALO:HALO + t, :] = x_ref[0]
    buf[HALO + t:2 * HALO + t, :] = jnp.where(i < last, n_ref[0], 0.0)


TP = 512


def _pool_kernel(x_ref, p_ref, n_ref, w_ref, sc_ref, o_ref, buf):
    _fill_halo_scratch(buf, x_ref, p_ref, n_ref, TP)
    seq = pl.num_programs(1) * TP
    pos = pl.program_id(1) * TP + lax.broadcasted_iota(jnp.int32, (TP, 1), 0)
    lane_grp = lax.broadcasted_iota(jnp.int32, (TP, MIX_W), 1) // (MIX_W // len(POOL_WINDOWS))
    x0 = buf[HALO:HALO + TP, :]
    pooled = jnp.zeros((TP, MIX_W), F32)
    acc = None
    half_done = 0
    for gi, win in enumerate(POOL_WINDOWS):
        half = win // 2
        for o in list(range(-half, -half_done)) + list(range(half_done, half)):
            term = buf[HALO + o:HALO + o + TP, :]
            acc = term if acc is None else acc + term
        half_done = half
        cnt = (jnp.minimum(pos + half, seq) - jnp.maximum(pos - half, 0)).astype(F32)
        pooled = jnp.where(lane_grp == gi, acc / cnt - x0, pooled)
    mixed = jnp.dot(pooled.astype(BF16), w_ref[...], preferred_element_type=F32)
    o_ref[0] = (mixed * sc_ref[...]).astype(BF16)


def _pool_call(dx, w_block, scale):
    b, s, _ = dx.shape
    nt = s // TP
    main, prev, nxt = _halo_specs(TP, MIX_W, nt)
    return pl.pallas_call(
        _pool_kernel, name="pool_mixer",
        grid=(b, nt),
        in_specs=[main, prev, nxt,
                  pl.BlockSpec((MIX_W, MIX_W), lambda b, i: (0, 0)),
                  pl.BlockSpec((1, MIX_W), lambda b, i: (0, 0))],
        out_specs=pl.BlockSpec((1, TP, MIX_W), lambda b, i: (b, i, 0)),
        out_shape=jax.ShapeDtypeStruct((b, s, MIX_W), BF16),
        scratch_shapes=[pltpu.VMEM((TP + 2 * HALO, MIX_W), F32)],
        compiler_params=_cparams(2),
    )(dx, dx, dx, w_block, scale)


TQ = 128
TQS = 512
TKEYS = TQ + 2 * ATT_BLOCK


def _attn_kernel(q_ref, kp_ref, km_ref, kn_ref, vp_ref, vm_ref, vn_ref, bias_ref, o_ref, lse_ref):
    i = pl.program_id(2)
    seq = pl.num_programs(2) * TQS
    q = q_ref[0, 0] * ATT_HD ** -0.5
    k = jnp.concatenate([kp_ref[0, 0], km_ref[0, 0], kn_ref[0, 0]], axis=0)
    v = jnp.concatenate([vp_ref[0, 0], vm_ref[0, 0], vn_ref[0, 0]], axis=0)
    lane = lax.broadcasted_iota(jnp.int32, (TQ, LANES), 1)
    for j in range(TQS // TQ):
        kpos = i * TQS + j * TQ - ATT_BLOCK + lax.broadcasted_iota(jnp.int32, (1, TKEYS), 1)
        kvalid = (kpos >= 0) & (kpos < seq)
        qrows = slice(j * TQ, (j + 1) * TQ)
        krows = slice(j * TQ, j * TQ + TKEYS)
        lse_tile = jnp.zeros((TQ, LANES), F32)
        for h in range(ATT_HEADS):
            sl = slice(h * ATT_HD, (h + 1) * ATT_HD)
            logits = lax.dot_general(q[qrows, sl], k[krows, sl], (((1,), (1,)), ((), ())),
                                     preferred_element_type=F32) + bias_ref[h]
            logits = jnp.where(kvalid, logits, NEG_BIG)
            m = jnp.max(logits, axis=-1, keepdims=True)
            p = jnp.exp(logits - m)
            ssum = jnp.sum(p, axis=-1, keepdims=True)
            o = jnp.dot(p.astype(BF16), v[krows, sl], preferred_element_type=F32) / ssum
            o_ref[0, 0, qrows, sl] = o
            lse_tile = jnp.where(lane == h, m + jnp.log(ssum), lse_tile)
        lse_ref[0, 0, qrows, :] = lse_tile


def _attn_call(qkv, bias):
    b, dil, l, _ = qkv.shape
    nt = l // TQS
    r64 = TQS // ATT_BLOCK
    main = lambda c: pl.BlockSpec((1, 1, TQS, MIX_W), lambda b, r, i: (b, r, i, c))
    prev = lambda c: pl.BlockSpec((1, 1, ATT_BLOCK, MIX_W),
                                  lambda b, r, i: (b, r, jnp.maximum(i * r64 - 1, 0), c))
    nxt = lambda c: pl.BlockSpec((1, 1, ATT_BLOCK, MIX_W),
                                 lambda b, r, i: (b, r, jnp.minimum((i + 1) * r64, nt * r64 - 1), c))
    return pl.pallas_call(
        _attn_kernel, name="band_attention",
        grid=(b, dil, nt),
        in_specs=[main(0), prev(1), main(1), nxt(1), prev(2), main(2), nxt(2),
                  pl.BlockSpec((ATT_HEADS, TQ, TKEYS), lambda b, r, i: (0, 0, 0))],
        out_specs=(pl.BlockSpec((1, 1, TQS, MIX_W), lambda b, r, i: (b, r, i, 0)),
                   pl.BlockSpec((1, 1, TQS, LANES), lambda b, r, i: (b, r, i, 0))),
        out_shape=(jax.ShapeDtypeStruct((b, dil, l, MIX_W), F32),
                   jax.ShapeDtypeStruct((b, dil, l, LANES), F32)),
        compiler_params=_cparams(3),
    )(qkv, qkv, qkv, qkv, qkv, qkv, qkv, bias)


def _t5_bucket_static(rel):
    half = REL_BUCKETS // 2
    max_exact = half // 2
    ret = np.where(rel > 0, half, 0)
    n = np.abs(rel)
    nf = np.maximum(n, 1).astype(np.float32)
    large = max_exact + (np.log(nf / np.float32(max_exact)) / np.float32(math.log(REL_MAX_DIST / max_exact))
                         * np.float32(half - max_exact)).astype(np.int32)
    large = np.minimum(large, half - 1)
    return ret + np.where(n < max_exact, n, large)


def _attn_bias_tile(rel_bias, window, dil):
    side = (window // 2) // dil
    rel = np.arange(TKEYS)[None, :] - ATT_BLOCK - np.arange(TQ)[:, None]
    onehot = jax.nn.one_hot(jnp.asarray(_t5_bucket_static(dil * rel), jnp.int32), REL_BUCKETS, dtype=F32)
    bias = jnp.einsum('qkr,rh->hqk', onehot, rel_bias, precision=lax.Precision.HIGHEST)
    return jnp.where(jnp.asarray(np.abs(rel) <= side)[None], bias, NEG_BIG)


TM = 512
VT_ROWS = ML_HD + 16


def _mlprep_kernel(x_ref, p_ref, n_ref, v_ref, w_ref, qt_out, k_out, vt_out, buf):
    _fill_halo_scratch(buf, x_ref, p_ref, n_ref, TM)
    conv = (buf[HALO - 1:HALO - 1 + TM, :] * w_ref[0:1, :] + buf[HALO:HALO + TM, :] * w_ref[1:2, :]
            + buf[HALO + 1:HALO + 1 + TM, :] * w_ref[2:3, :])
    qk = jax.nn.silu(conv)
    qt = jnp.transpose(qk[:, :MIX_W])
    vt = jnp.transpose(v_ref[0].astype(F32))
    ones_rows = jnp.where(lax.broadcasted_iota(jnp.int32, (VT_ROWS - ML_HD, ML_CHUNK), 0) == 0, 1.0, 0.0)
    for h in range(ML_HEADS):
        sl = slice(h * ML_HD, (h + 1) * ML_HD)
        k_out[0, h] = (qk[:, MIX_W + h * ML_HD:MIX_W + (h + 1) * ML_HD] * ML_HD ** -0.5).astype(BF16)
        for c in range(TM // ML_CHUNK):
            cl = slice(c * ML_CHUNK, (c + 1) * ML_CHUNK)
            qt_out[0, h, c] = qt[sl, cl].astype(BF16)
            vt_out[0, h, c] = jnp.concatenate([vt[sl, cl], ones_rows], axis=0).astype(BF16)


def _mlprep_call(cqk, cv, conv_w):
    b, s, _ = cqk.shape
    nt = s // TM
    nc = s // ML_CHUNK
    cpt = TM // ML_CHUNK
    main, prev, nxt = _halo_specs(TM, 2 * MIX_W, nt)
    return pl.pallas_call(
        _mlprep_kernel, name="mlstm_prep",
        grid=(b, nt),
        in_specs=[main, prev, nxt,
                  pl.BlockSpec((1, TM, MIX_W), lambda b, i: (b, i, 0)),
                  pl.BlockSpec((3, 2 * MIX_W), lambda b, i: (0, 0))],
        out_specs=(pl.BlockSpec((1, ML_HEADS, cpt, ML_HD, ML_CHUNK), lambda b, i: (b, 0, i, 0, 0)),
                   pl.BlockSpec((1, ML_HEADS, TM, ML_HD), lambda b, i: (b, 0, i, 0)),
                   pl.BlockSpec((1, ML_HEADS, cpt, VT_ROWS, ML_CHUNK), lambda b, i: (b, 0, i, 0, 0))),
        out_shape=(jax.ShapeDtypeStruct((b, ML_HEADS, nc, ML_HD, ML_CHUNK), BF16),
                   jax.ShapeDtypeStruct((b, ML_HEADS, s, ML_HD), BF16),
                   jax.ShapeDtypeStruct((b, ML_HEADS, nc, VT_ROWS, ML_CHUNK), BF16)),
        scratch_shapes=[pltpu.VMEM((TM + 2 * HALO, 2 * MIX_W), F32)],
        compiler_params=_cparams(2),
    )(cqk, cqk, cqk, cv, conv_w)


GP_BLK = 512


def _chunk_scan(x, fwd_lane, t_in_chunk, op, ident):
    n = x.shape[0]
    shift = 1
    while shift < ML_CHUNK:
        down = pltpu.roll(x, shift, 0)
        up = pltpu.roll(x, n - shift, 0)
        nb = jnp.where(fwd_lane,
                       jnp.where(t_in_chunk >= shift, down, ident),
                       jnp.where(t_in_chunk < ML_CHUNK - shift, up, ident))
        x = op(x, nb)
        shift *= 2
    return x


def _gatescan_kernel(ig_ref, fg_ref, fb_ref, b_ref, a_ref, cm_ref, g_ref, amax_ref):
    lane = lax.broadcasted_iota(jnp.int32, (1, LANES), 1)
    fwd_lane = lane < ML_HEADS
    t_in_chunk = lax.broadcasted_iota(jnp.int32, (GP_BLK, 1), 0) % ML_CHUNK
    z = fg_ref[0] + fb_ref[...]
    lf = jnp.minimum(z, 0.0) - jnp.log1p(jnp.exp(-jnp.abs(z)))
    b = _chunk_scan(lf, fwd_lane, t_in_chunk, jnp.add, 0.0)
    a = ig_ref[0] - b
    b_ref[0] = b
    a_ref[0] = a
    cm_ref[0] = _chunk_scan(a, fwd_lane, t_in_chunk, jnp.maximum, -jnp.inf)
    cpb = GP_BLK // ML_CHUNK
    last = pl.ds(ML_CHUNK - 1, cpb, stride=ML_CHUNK)
    first = pl.ds(0, cpb, stride=ML_CHUNK)
    g_ref[0] = jnp.where(fwd_lane, b_ref[0, last, :], b_ref[0, first, :])
    amax_ref[0] = jnp.where(fwd_lane, cm_ref[0, last, :], cm_ref[0, first, :])


def _gateout_kernel(b_ref, a_ref, cm_ref, g_ref, amax_ref, ws_ref, m_ref, iw_ref, en_ref, dec_ref,
                    mch_s, mlast_s):
    nc = g_ref.shape[1]
    j = pl.program_id(1)
    fwd_lane = lax.broadcasted_iota(jnp.int32, (1, LANES), 1) < ML_HEADS

    @pl.when(j == 0)
    def _():
        def m_step(i, carry):
            mf, mb = carry
            cf = pl.ds(i, 1)
            cb = pl.ds(nc - 1 - i, 1)
            mch_s[cf, :] = jnp.where(fwd_lane, mf, mch_s[cf, :])
            mch_s[cb, :] = jnp.where(fwd_lane, mch_s[cb, :], mb)
            mf = g_ref[0, cf, :] + jnp.maximum(mf, amax_ref[0, cf, :])
            mb = g_ref[0, cb, :] + jnp.maximum(mb, amax_ref[0, cb, :])
            return mf, mb

        mch_s[...] = jnp.zeros((nc, LANES), F32)
        zero = jnp.zeros((1, LANES), F32)
        lax.fori_loop(0, nc, m_step, (zero, zero))
        mlast = jnp.maximum(amax_ref[0], mch_s[...])
        mlast_s[...] = mlast
        dec_ref[0] = jnp.exp(mch_s[...] - mlast)

    cpb = GP_BLK // ML_CHUNK
    crow = pl.ds(pl.multiple_of(j * cpb, cpb), cpb)
    expand = lambda t: jnp.broadcast_to(t[:, None, :], (cpb, ML_CHUNK, LANES)).reshape(GP_BLK, LANES)
    m_tok = expand(mch_s[crow, :])
    mlast_tok = expand(mlast_s[crow, :])
    mt = jnp.maximum(cm_ref[0], m_tok)
    m_ref[0] = mt
    iw_ref[0] = jnp.exp(m_tok - mt)
    en_ref[0] = jnp.exp(-(b_ref[0] + mt))
    ws_ref[0] = jnp.exp(a_ref[0] - mlast_tok)


def _gateprep_call(ig, fg, fbias_row):
    b, s, _ = ig.shape
    nc = s // ML_CHUNK
    cpb = GP_BLK // ML_CHUNK
    tok = pl.BlockSpec((1, GP_BLK, LANES), lambda i, j: (i, j, 0))
    tok_shape = jax.ShapeDtypeStruct((b, s, LANES), F32)
    chunk_shape = jax.ShapeDtypeStruct((b, nc, LANES), F32)
    chunk_tile = pl.BlockSpec((1, cpb, LANES), lambda i, j: (i, j, 0))
    chunk_all = pl.BlockSpec((1, nc, LANES), lambda i, j: (i, 0, 0))
    bcum, a, cm, g, amax = pl.pallas_call(
        _gatescan_kernel, name="mlstm_gate_scan",
        grid=(b, s // GP_BLK),
        in_specs=[tok, tok, pl.BlockSpec((1, LANES), lambda i, j: (0, 0))],
        out_specs=(tok, tok, tok, chunk_tile, chunk_tile),
        out_shape=(tok_shape,) * 3 + (chunk_shape,) * 2,
        compiler_params=_cparams(2),
    )(ig, fg, fbias_row)
    ws, m_col, iw_col, en_col, decay = pl.pallas_call(
        _gateout_kernel, name="mlstm_gate_out",
        grid=(b, s // GP_BLK),
        in_specs=[tok, tok, tok, chunk_all, chunk_all],
        out_specs=(tok, tok, tok, tok, chunk_all),
        out_shape=(tok_shape,) * 4 + (chunk_shape,),
        scratch_shapes=[pltpu.VMEM((nc, LANES), F32)] * 2,
        compiler_params=_cparams(2),
    )(bcum, a, cm, g, amax)
    return a, ws, m_col, iw_col, en_col, decay


TE = 512


def _mlstm_kernel(*refs):
    fwd, bwd, (hf_ref, hb_ref, state) = refs[:9], refs[9:18], refs[18:]
    i = pl.program_id(1)

    @pl.when(i == 0)
    def _():
        state[...] = jnp.zeros(state.shape, F32)

    cpt = TE // ML_CHUNK
    s_idx = lax.broadcasted_iota(jnp.int32, (ML_CHUNK, ML_CHUNK), 0)
    t_idx = lax.broadcasted_iota(jnp.int32, (ML_CHUNK, ML_CHUNK), 1)

    def chunk_body(c, carry):
        jobs = []
        for d, (qt_r, k_r, vt_r, a_r, m_r, iw_r, en_r, ws_r, dec_r), out_r in ((0, fwd, hf_ref), (1, bwd, hb_ref)):
            cc = c if d == 0 else cpt - 1 - c
            rows = pl.ds(pl.multiple_of(cc * ML_CHUNK, ML_CHUNK), ML_CHUNK)
            crow = pl.ds(cc, 1)
            for h in range(ML_HEADS):
                ch = d * ML_HEADS + h
                row = lambda r: r[0, ch, crow, :]
                jobs.append(dict(
                    ch=ch, tri=(s_idx >= t_idx) if d else (s_idx <= t_idx),
                    k=k_r[0, h, rows, :], qt=qt_r[0, h, cc], vt=vt_r[0, h, cc], a=a_r[0, rows, ch:ch + 1],
                    m=row(m_r), iw=row(iw_r), en=row(en_r), ws=row(ws_r), dec=row(dec_r),
                    out=(out_r, cc, h)))
        for j in jobs:
            j["cst"] = state[j["ch"]]
            j["st"] = jnp.dot(j["k"], j["qt"], preferred_element_type=F32)
            j["inter"] = jnp.dot(j["cst"].astype(BF16), j["qt"], preferred_element_type=F32)
            j["upd"] = jnp.dot((j["vt"].astype(F32) * j["ws"]).astype(BF16), j["k"],
                               preferred_element_type=F32)
        for j in jobs:
            j["swt"] = j["st"] * jnp.exp(jnp.where(j["tri"], j["a"] - j["m"], NEG_BIG))
            j["intra"] = jnp.dot(j["vt"], j["swt"].astype(BF16), preferred_element_type=F32)
        for j in jobs:
            den = jnp.sum(j["swt"], axis=0, keepdims=True) + j["iw"] * j["inter"][ML_HD:ML_HD + 1]
            tot = j["intra"][:ML_HD] + j["iw"] * j["inter"][:ML_HD]
            out_r, cc, h = j["out"]
            out_r[0, cc, h * ML_HD:(h + 1) * ML_HD, :] = tot / jnp.maximum(jnp.abs(den), j["en"])
            state[j["ch"]] = j["dec"] * j["cst"] + j["upd"]
        return carry

    lax.fori_loop(0, cpt, chunk_body, 0)


def _mlstm_call(qt, k, vt, a_col, m_row, iw_row, en_row, ws_row, dec_row):
    b, _, s, _ = k.shape
    nt = s // TE
    nc = s // ML_CHUNK
    cpt = TE // ML_CHUNK

    def specs(rev):
        ti = (lambda i: nt - 1 - i) if rev else (lambda i: i)
        row = pl.BlockSpec((1, 2 * ML_HEADS, cpt, ML_CHUNK), lambda b, i: (b, 0, ti(i), 0))
        return [
            pl.BlockSpec((1, ML_HEADS, cpt, ML_HD, ML_CHUNK), lambda b, i: (b, 0, ti(i), 0, 0)),
            pl.BlockSpec((1, ML_HEADS, TE, ML_HD), lambda b, i: (b, 0, ti(i), 0)),
            pl.BlockSpec((1, ML_HEADS, cpt, VT_ROWS, ML_CHUNK), lambda b, i: (b, 0, ti(i), 0, 0)),
            pl.BlockSpec((1, TE, LANES), lambda b, i: (b, ti(i), 0)),
            row, row, row, row, row]

    args = [qt, k, vt, a_col, m_row, iw_row, en_row, ws_row, dec_row]
    out_f = pl.BlockSpec((1, cpt, MIX_W, ML_CHUNK), lambda b, i: (b, i, 0, 0))
    out_b = pl.BlockSpec((1, cpt, MIX_W, ML_CHUNK), lambda b, i: (b, nt - 1 - i, 0, 0))
    return pl.pallas_call(
        _mlstm_kernel, name="mlstm_scan",
        grid=(b, nt),
        in_specs=specs(False) + specs(True),
        out_specs=(out_f, out_b),
        out_shape=(jax.ShapeDtypeStruct((b, nc, MIX_W, ML_CHUNK), F32),) * 2,
        scratch_shapes=[pltpu.VMEM((2 * ML_HEADS, VT_ROWS, ML_HD), F32)],
        compiler_params=_cparams(2),
    )(*args, *args)


def _mlstm_branch(cqk, cv, ig, fg, conv_w, fbias):
    b, s, _ = cqk.shape
    nc = s // ML_CHUNK
    qt, k, vt = _mlprep_call(cqk, cv, conv_w)
    fb_row = jnp.pad(fbias.reshape(1, 2 * ML_HEADS), ((0, 0), (0, LANES - 2 * ML_HEADS)))
    a, ws, m_col, iw_col, en_col, decay = _gateprep_call(ig, fg, fb_row)
    rows = lambda t: jnp.transpose(t[..., :2 * ML_HEADS], (0, 2, 1)).reshape(b, 2 * ML_HEADS, nc, ML_CHUNK)
    dec_row = jnp.broadcast_to(jnp.transpose(decay[..., :2 * ML_HEADS], (0, 2, 1))[..., None],
                               (b, 2 * ML_HEADS, nc, ML_CHUNK))
    return _mlstm_call(qt, k, vt, a, rows(m_col), rows(iw_col), rows(en_col), rows(ws), dec_row)


TF = 256


def _merge_kernel(x_ref, ya_ref, o1_ref, o2_ref, o3_ref, l1_ref, l2_ref, l3_ref, hf_ref, hb_ref,
                  co_ref, yd_ref, wg_ref, bg_ref, wbr_ref, wout_ref, mng_ref, lng_ref, lnb_ref,
                  wr_ref, x1_ref, aff_ref, o_scr, l_scr, *, alpha):
    x = x_ref[0]
    xb = x.astype(BF16)

    def natural_order(src_ref, scr):
        dil, width = src_ref.shape[1], src_ref.shape[3]
        if dil == 1:
            return src_ref[0, 0]
        for r in range(dil):
            for c in range(width // LANES):
                scr[c, pl.ds(r, TF // dil, stride=dil), :] = src_ref[0, r, :, c * LANES:(c + 1) * LANES]
        return jnp.concatenate([scr[c] for c in range(width // LANES)], axis=1)

    lane_head = lax.broadcasted_iota(jnp.int32, (TF, MIX_W), 1) // ML_HD
    l1, l2, l3 = [natural_order(r, l_scr.at[p]) for p, r in enumerate((l1_ref, l2_ref, l3_ref))]
    o1, o2, o3 = [natural_order(r, o_scr.at[p]) for p, r in enumerate((o1_ref, o2_ref, o3_ref))]
    lm = jnp.maximum(jnp.maximum(l1, l2), l3)
    e1, e2, e3 = jnp.exp(l1 - lm), jnp.exp(l2 - lm), jnp.exp(l3 - lm)
    inv = 1.0 / (e1 + e2 + e3)

    def per_head(w):
        out = jnp.zeros((TF, MIX_W), F32)
        for h in range(ATT_HEADS):
            out = jnp.where(lane_head == h, w[:, h:h + 1], out)
        return out

    y_b = per_head(e1 * inv) * o1 + per_head(e2 * inv) * o2 + per_head(e3 * inv) * o3
    hsum_t = jnp.concatenate([hf_ref[0, c] + hb_ref[0, c] for c in range(TF // ML_CHUNK)], axis=1)
    per_head_rows = hsum_t.reshape(ML_HEADS, ML_HD, TF)
    mu = jnp.mean(per_head_rows, axis=1, keepdims=True)
    cen = per_head_rows - mu
    var = jnp.mean(cen * cen, axis=1, keepdims=True)
    hn_t = (cen * lax.rsqrt(var + LN_EPS)).reshape(MIX_W, TF)
    y_c_t = (jax.nn.sigmoid(jnp.transpose(co_ref[0])) * (hn_t * mng_ref[...])).astype(BF16)
    ys = (ya_ref[0], y_b.astype(BF16), None, yd_ref[0])
    merged = jnp.zeros((TF, D_MODEL), F32)
    for n in range(N_BRANCH):
        cols = slice(n * D_MODEL, (n + 1) * D_MODEL)
        gate = jax.nn.sigmoid(jnp.dot(xb, wg_ref[:, cols], preferred_element_type=F32) + bg_ref[:, cols])
        if ys[n] is None:
            proj = lax.dot_general(y_c_t, wbr_ref[n], (((0,), (0,)), ((), ())), preferred_element_type=F32)
        else:
            proj = jnp.dot(ys[n], wbr_ref[n], preferred_element_type=F32)
        merged = merged + gate * proj
    mix = jnp.dot(merged.astype(BF16), wout_ref[...], preferred_element_type=F32)
    x1 = _standardize(alpha * x + mix) * lng_ref[...] + lnb_ref[...]
    x1_ref[0] = x1
    logits = lax.dot_general(wr_ref[...], x1.astype(BF16), (((1,), (1,)), ((), ())),
                             preferred_element_type=F32)
    ex = jnp.exp(logits - jnp.max(logits, axis=0, keepdims=True))
    aff_ref[0] = ex / jnp.sum(ex, axis=0, keepdims=True)


def _merge_call(x, ya, o_list, l_list, hf, hb, co, yd, wg, bg, wbr, wout, mng, lng, lnb, wr_t, alpha):
    b, s, _ = x.shape
    tok = lambda w: pl.BlockSpec((1, TF, w), lambda b, i: (b, i, 0))
    grouped = lambda dil, w: pl.BlockSpec((1, dil, TF // dil, w), lambda b, i: (b, 0, i, 0))
    chunked = pl.BlockSpec((1, TF // ML_CHUNK, MIX_W, ML_CHUNK), lambda b, i: (b, i, 0, 0))
    const = lambda shp: pl.BlockSpec(shp, lambda b, i: (0,) * len(shp))
    return pl.pallas_call(
        functools.partial(_merge_kernel, alpha=alpha), name="merge_ln_router",
        grid=(b, s // TF),
        in_specs=[tok(D_MODEL), tok(MIX_W)] + [grouped(dil, MIX_W) for _, dil in DIL_PATTERNS]
                 + [grouped(dil, LANES) for _, dil in DIL_PATTERNS]
                 + [chunked, chunked, tok(MIX_W), tok(MIX_W)]
                 + [const((D_MODEL, N_BRANCH * D_MODEL)), const((1, N_BRANCH * D_MODEL)),
                    const((N_BRANCH, MIX_W, D_MODEL)), const((D_MODEL, D_MODEL)), const((MIX_W, 1)),
                    const((1, D_MODEL)), const((1, D_MODEL)), const((N_EXPERTS, D_MODEL))],
        out_specs=(tok(D_MODEL), pl.BlockSpec((1, N_EXPERTS, TF), lambda b, i: (b, 0, i))),
        out_shape=(jax.ShapeDtypeStruct((b, s, D_MODEL), F32),
                   jax.ShapeDtypeStruct((b, N_EXPERTS, s), F32)),
        scratch_shapes=[pltpu.VMEM((len(DIL_PATTERNS), MIX_W // LANES, TF, LANES), F32),
                        pltpu.VMEM((len(DIL_PATTERNS), 1, TF, LANES), F32)],
        compiler_params=_cparams(2),
    )(x, ya, *o_list, *l_list, hf, hb, co, yd, wg, bg, wbr, wout, mng, lng, lnb, wr_t)


TT = 256


def _select_kernel(aff_ref, slot_ref, *, cap):
    s = aff_ref.shape[2]
    bits = pltpu.bitcast(aff_ref[0], jnp.int32)

    def bit_step(i, thr):
        cand = thr | jnp.left_shift(jnp.int32(1), 30 - i)
        cnt = jnp.sum((bits >= cand).astype(jnp.int32), axis=1, keepdims=True)
        return jnp.where(cnt >= cap, cand, thr)

    thr = lax.fori_loop(0, 31, bit_step, jnp.zeros((N_EXPERTS, 1), jnp.int32))
    gt = bits > thr
    eq = bits == thr
    need = (cap - jnp.sum(gt.astype(jnp.int32), axis=1, keepdims=True)).astype(F32)
    upper = (lax.broadcasted_iota(jnp.int32, (TT, TT), 0)
             <= lax.broadcasted_iota(jnp.int32, (TT, TT), 1)).astype(BF16)
    eq_before = jnp.zeros((N_EXPERTS, 1), F32)
    sel_before = jnp.zeros((N_EXPERTS, 1), F32)
    for j in range(s // TT):
        cols = slice(j * TT, (j + 1) * TT)
        eq_j = eq[:, cols]
        eq_incl = eq_before + jnp.dot(eq_j.astype(BF16), upper, preferred_element_type=F32)
        sel_j = gt[:, cols] | (eq_j & (eq_incl <= need))
        sel_f = sel_j.astype(F32)
        sel_incl = sel_before + jnp.dot(sel_f.astype(BF16), upper, preferred_element_type=F32)
        slot_ref[0, :, cols] = jnp.where(sel_j, sel_incl - 1.0, -1.0).astype(jnp.int32)
        eq_before = eq_incl[:, TT - 1:TT]
        sel_before = sel_incl[:, TT - 1:TT]


def _select_call(aff_t, cap):
    b, e, s = aff_t.shape
    return pl.pallas_call(
        functools.partial(_select_kernel, cap=cap), name="expert_choice_select",
        grid=(b,),
        in_specs=[pl.BlockSpec((1, e, s), lambda i: (i, 0, 0))],
        out_specs=pl.BlockSpec((1, e, s), lambda i: (i, 0, 0)),
        out_shape=jax.ShapeDtypeStruct((b, e, s), jnp.int32),
        compiler_params=_cparams(1),
    )(aff_t)


SC_LANES = 16
SC_ROWS = 64
SC_IDX = 128
SC_SLAB = 128
SC_ZROWS = 64
CF = 512


def _sc_dispatch_call(x_flat, slot2, aff2, seq, cap):
    n_pairs = slot2.shape[0]
    d = x_flat.shape[1]
    info = plsc.get_sparse_core_info()
    n_workers = info.num_cores * info.num_subcores
    assert n_pairs % n_workers == 0 and seq % SC_LANES == 0 and cap % SC_ROWS == 0
    pairs_per_worker = n_pairs // n_workers
    mesh = plsc.VectorSubcoreMesh(core_axis_name="c", subcore_axis_name="s")

    @functools.partial(
        pl.kernel, mesh=mesh, name="expert_dispatch_sc",
        compiler_params=pltpu.CompilerParams(needs_layout_passes=False),
        out_type=(jax.ShapeDtypeStruct((n_pairs * cap, d), F32),
                  jax.ShapeDtypeStruct((n_pairs, cap), jnp.int32),
                  jax.ShapeDtypeStruct((n_pairs, cap), F32)),
        scratch_types=[pltpu.VMEM((seq,), jnp.int32), pltpu.VMEM((seq,), F32),
                       pltpu.VMEM((cap,), jnp.int32), pltpu.VMEM((cap,), jnp.int32),
                       pltpu.VMEM((cap,), F32),
                       pltpu.VMEM((SC_ROWS, d), F32), pltpu.SemaphoreType.DMA])
    def dispatch(x_hbm, slot_hbm, aff_hbm, xs_hbm, tok_hbm, gate_hbm,
                 slot_v, aff_v, idx_v, tok_v, gate_v, rows_v, sem):
        worker = lax.axis_index("s") * info.num_cores + lax.axis_index("c")
        lane = lax.iota(jnp.int32, SC_LANES)
        for k in range(pairs_per_worker):
            pair = worker * pairs_per_worker + k
            row0 = (pair // N_EXPERTS) * seq
            pltpu.sync_copy(slot_hbm.at[pair], slot_v)
            pltpu.sync_copy(aff_hbm.at[pair], aff_v)

            @pl.loop(0, seq, step=SC_LANES)
            def _(t0):
                sv = slot_v[pl.ds(t0, SC_LANES)]
                picked = sv >= 0
                plsc.store_scatter(tok_v, [sv], t0 + lane, mask=picked)
                plsc.store_scatter(idx_v, [sv], row0 + t0 + lane, mask=picked)
                plsc.store_scatter(gate_v, [sv], aff_v[pl.ds(t0, SC_LANES)], mask=picked)

            pltpu.sync_copy(tok_v, tok_hbm.at[pair])
            pltpu.sync_copy(gate_v, gate_hbm.at[pair])

            @pl.loop(0, cap, step=SC_ROWS)
            def _(c0):
                pltpu.async_copy(x_hbm.at[idx_v.at[pl.ds(c0, SC_ROWS)]], rows_v, sem).wait()
                pltpu.sync_copy(rows_v, xs_hbm.at[pl.ds(pair * cap + c0, SC_ROWS)])

    return dispatch(x_flat, slot2, aff2)


def _expert_kernel(xs_ref, g_ref, w1_ref, w3_ref, w2_ref, ye_ref, w1_bf, w3_bf, w2_bf):
    @pl.when((pl.program_id(1) == 0) & (pl.program_id(2) == 0))
    def _():
        w1_bf[...] = w1_ref[0, 0].astype(BF16)
        w3_bf[...] = w3_ref[0, 0].astype(BF16)
        w2_bf[...] = w2_ref[0, 0].astype(BF16)

    xs = xs_ref[0, 0].astype(BF16)
    hid = (jax.nn.silu(jnp.dot(xs, w1_bf[...], preferred_element_type=F32))
           * jnp.dot(xs, w3_bf[...], preferred_element_type=F32))
    ye_ref[0, 0] = jnp.dot(hid.astype(BF16), w2_bf[...], preferred_element_type=F32) * g_ref[0, 0]


def _expert_call(xs4, gate4, w1, w3, w2, layer):
    b, e, cap, d = xs4.shape
    ff = w1.shape[3]
    rows = lambda w: pl.BlockSpec((1, 1, CF, w), lambda e, b, j: (b, e, j, 0))
    wspec = lambda r, c: pl.BlockSpec((1, 1, r, c), lambda e, b, j: (layer, e, 0, 0))
    return pl.pallas_call(
        _expert_kernel, name="expert_ffn",
        grid=(e, b, cap // CF),
        in_specs=[rows(d), rows(1), wspec(d, ff), wspec(d, ff), wspec(ff, d)],
        out_specs=rows(d),
        out_shape=jax.ShapeDtypeStruct((b, e, cap, d), F32),
        scratch_shapes=[pltpu.VMEM((d, ff), BF16), pltpu.VMEM((d, ff), BF16), pltpu.VMEM((ff, d), BF16)],
        compiler_params=_cparams(3),
    )(xs4, gate4, w1, w3, w2)


def _sc_combine_call(ye_flat, tok3, seq):
    n_pairs, n_chunks, _ = tok3.shape
    cap = n_chunks * SC_IDX
    d = ye_flat.shape[1]
    nb = n_pairs // N_EXPERTS
    info = plsc.get_sparse_core_info()
    assert info.num_subcores == N_EXPERTS and nb % info.num_cores == 0
    assert seq % (info.num_subcores * SC_ZROWS) == 0 and d % SC_SLAB == 0
    batches_per_core = nb // info.num_cores
    own_rows = seq // info.num_subcores
    mesh = plsc.VectorSubcoreMesh(core_axis_name="c", subcore_axis_name="s")

    @functools.partial(
        pl.kernel, mesh=mesh, name="expert_combine_sc",
        compiler_params=pltpu.CompilerParams(needs_layout_passes=False),
        out_type=jax.ShapeDtypeStruct((nb * seq, d), F32),
        scratch_types=[pltpu.VMEM_SHARED((seq, SC_SLAB), F32),
                       pltpu.VMEM((n_chunks, SC_IDX), jnp.int32),
                       pltpu.VMEM((SC_IDX, SC_SLAB), F32),
                       pltpu.VMEM((SC_ZROWS, SC_SLAB), F32)])
    def combine(ye_hbm, tok_hbm, out_hbm, acc_sh, tok_v, rows_v, zero_v):
        core = lax.axis_index("c")
        sub = lax.axis_index("s")

        @pl.loop(0, SC_ZROWS)
        def _(r):
            for l0 in range(0, SC_SLAB, SC_LANES):
                zero_v[r, pl.ds(l0, SC_LANES)] = jnp.zeros((SC_LANES,), F32)

        for bb in range(batches_per_core):
            batch = core * batches_per_core + bb
            pair = batch * N_EXPERTS + sub
            pltpu.sync_copy(tok_hbm.at[pair], tok_v)

            @pl.loop(0, d // SC_SLAB)
            def _(slab):
                cols = pl.ds(pl.multiple_of(slab * SC_SLAB, SC_SLAB), SC_SLAB)

                @pl.loop(0, own_rows, step=SC_ZROWS)
                def _(r0):
                    pltpu.sync_copy(zero_v, acc_sh.at[pl.ds(sub * own_rows + r0, SC_ZROWS)])

                plsc.subcore_barrier()

                @pl.loop(0, n_chunks)
                def _(j):
                    pltpu.sync_copy(ye_hbm.at[pl.ds(pair * cap + j * SC_IDX, SC_IDX), cols], rows_v)
                    pltpu.sync_copy(rows_v, acc_sh.at[tok_v.at[j]], add=True)

                plsc.subcore_barrier()
                pltpu.sync_copy(acc_sh.at[pl.ds(sub * own_rows, own_rows)],
                                out_hbm.at[pl.ds(batch * seq + sub * own_rows, own_rows), cols])

    return combine(ye_flat, tok3)


TN = 512


def _resln_kernel(x_ref, y_ref, g_ref, b_ref, o_ref, *, alpha):
    o_ref[...] = _standardize(alpha * x_ref[...] + y_ref[...]) * g_ref[...] + b_ref[...]


def _resln_call(x2d, y2d, g, bta, alpha):
    n, d = x2d.shape
    tok = pl.BlockSpec((TN, d), lambda i: (i, 0))
    vec = pl.BlockSpec((1, d), lambda i: (0, 0))
    return pl.pallas_call(
        functools.partial(_resln_kernel, alpha=alpha), name="residual_layernorm",
        grid=(n // TN,), in_specs=[tok, tok, vec, vec], out_specs=tok,
        out_shape=jax.ShapeDtypeStruct((n, d), F32),
        compiler_params=_cparams(1),
    )(x2d, y2d, g, bta)


def _expert_choice_ffn(x1, aff_t, w1, w3, w2, layer):
    b, s, d = x1.shape
    cap = EC_FACTOR * s // N_EXPERTS
    slot = _select_call(aff_t, cap)
    xs, tok, gate = _sc_dispatch_call(x1.reshape(b * s, d), slot.reshape(b * N_EXPERTS, s),
                                      aff_t.reshape(b * N_EXPERTS, s), s, cap)
    ye = _expert_call(xs.reshape(b, N_EXPERTS, cap, d), gate.reshape(b, N_EXPERTS, cap, 1), w1, w3, w2,
                      layer)
    out = _sc_combine_call(ye.reshape(b * N_EXPERTS * cap, d),
                           tok.reshape(b * N_EXPERTS, cap // SC_IDX, SC_IDX), s)
    return out.reshape(b, s, d)


def _pack_pool(pool_w):
    g, gd, _ = pool_w.shape
    out = jnp.zeros((g * gd, g * gd), F32)
    for i in range(g):
        out = out.at[i * gd:(i + 1) * gd, i * gd:(i + 1) * gd].set(pool_w[i])
    return out.astype(BF16)


def _layer(layer, x, alpha, bias_tiles, w_in, b_in, gm_ln_g, gm_ws, gm_bs, ml_conv, ml_fbias, ml_norm_g,
           pool_w, pool_scale, w_branch, w_out, ln1_g, ln1_b, w_router, w_e1, w_e3, w_e2, ln2_g, ln2_b):
    b, s, d = x.shape
    n_small = 2576
    w_cat, b_cat = _pack_inproj_weights(w_in, b_in)
    wscat, bsfull = _pack_gmlp(gm_ws, gm_bs)
    ya, qkv1, qkv4, qkv16, cqk, cv, co, dx, ig, fg = _inproj_call(
        x.reshape(b * s, d), w_cat, b_cat, gm_ln_g[None], wscat, bsfull, b)
    r3 = lambda t: t.reshape(b, s, t.shape[-1])
    o_list, l_list = [], []
    for qkv, bias in zip((qkv1, qkv4, qkv16), bias_tiles):
        o, lse = _attn_call(qkv, bias)
        o_list.append(o)
        l_list.append(lse)
    hf, hb = _mlstm_branch(r3(cqk), r3(cv), r3(ig), r3(fg), ml_conv, ml_fbias)
    yd = _pool_call(r3(dx), _pack_pool(pool_w), pool_scale[None])
    x1, aff_t = _merge_call(
        x, r3(ya), o_list, l_list, hf, hb, r3(co), yd,
        w_in[:, n_small:].astype(BF16), b_in[None, n_small:], w_branch.astype(BF16), w_out.astype(BF16),
        ml_norm_g[:, None], ln1_g[None], ln1_b[None], jnp.transpose(w_router).astype(BF16), alpha)
    ffn = _expert_choice_ffn(x1, aff_t, w_e1, w_e3, w_e2, layer)
    x2 = _resln_call(x1.reshape(b * s, d), ffn.reshape(b * s, d), ln2_g[None], ln2_b[None], alpha)
    return x2.reshape(b, s, d)


def kernel(x, w_in, b_in, gm_ln_g, gm_ws, gm_bs, rel_bias, ml_conv, ml_fbias, ml_norm_g, pool_w,
           pool_scale, w_branch, w_out, ln1_g, ln1_b, w_router, w_e1, w_e3, w_e2, ln2_g, ln2_b):
    depth = w_in.shape[0]
    alpha = (2 * depth) ** 0.25
    bias_tiles = [_attn_bias_tile(rel_bias, window, dil) for window, dil in DIL_PATTERNS]
    for l in range(depth):
        x = _layer(l, x, alpha, bias_tiles, w_in[l], b_in[l], gm_ln_g[l], gm_ws[l], gm_bs[l], ml_conv[l],
                   ml_fbias[l], ml_norm_g[l], pool_w[l], pool_scale[l], w_branch[l], w_out[l],
                   ln1_g[l], ln1_b[l], w_router[l], w_e1, w_e3, w_e2, ln2_g[l], ln2_b[l])
    return x
```

```python
import functools
import math

import jax
import jax.numpy as jnp
import numpy as np
from jax import lax
from jax.experimental import pallas as pl
from jax.experimental.pallas import tpu as pltpu
from jax.experimental.pallas import tpu_sc as plsc

F32 = jnp.float32
BF16 = jnp.bfloat16

D_MODEL = 1024
MIX_W = 256
N_BRANCH = 4
GM_CHUNK = 128
GM_GROUPS = 4
ATT_HEADS = 4
ATT_HD = 64
DIL_PATTERNS = ((128, 1), (512, 4), (2048, 16))
ATT_BLOCK = 64
REL_BUCKETS = 32
REL_MAX_DIST = 1024
ML_HEADS = 4
ML_HD = 64
ML_CHUNK = 64
POOL_WINDOWS = (2, 4, 8, 16)
N_EXPERTS = 16
EXPERT_FF = 1024
EC_FACTOR = 2
LN_EPS = 1e-5
NEG_BIG = -1e30

V7X_VMEM_LIMIT = 56 * 1024 * 1024
LANES = 128
HALO = 8


def _cparams(n_grid, vmem=V7X_VMEM_LIMIT):
    return pltpu.CompilerParams(dimension_semantics=("arbitrary",) * n_grid,
                                vmem_limit_bytes=vmem)


def _pack_bf16_pair(lo, hi):
    lo_bits = lax.shift_right_logical(pltpu.bitcast(lo.astype(F32), jnp.int32), 16)
    return pltpu.bitcast(hi.astype(F32), jnp.int32) | lo_bits


def _unpack_bf16_pair(packed):
    lo = pltpu.bitcast(lax.shift_left(packed, 16), F32).astype(BF16)
    hi = pltpu.bitcast(packed & jnp.int32(-65536), F32).astype(BF16)
    return lo, hi


def _standardize(xf):
    mu = jnp.mean(xf, axis=-1, keepdims=True)
    var = jnp.mean(jnp.square(xf - mu), axis=-1, keepdims=True)
    return (xf - mu) * lax.rsqrt(var + LN_EPS)


TA = 512
A_COLS = 2560 + 2 * LANES


def _inproj_kernel(x_ref, w_ref, b_ref, lng_ref, wscat_ref, bsfull_ref,
                   ya_ref, qkv1_ref, qkv4_ref, qkv16_ref, cqk_ref, cv_ref, co_ref, dx_ref, ig_ref, fg_ref,
                   qkv_scr):
    xb = x_ref[...].astype(BF16)
    h = jnp.dot(xb, w_ref[...], preferred_element_type=F32) + b_ref[...]
    qkv1_ref[0, 0] = h[:, 512:1280].astype(BF16)
    for c in range(768 // LANES):
        qkv_scr[c] = h[:, 512 + c * LANES:512 + (c + 1) * LANES]
    for (_, dil), out_ref in zip(DIL_PATTERNS[1:], (qkv4_ref, qkv16_ref)):
        for r in range(dil):
            for c in range(768 // LANES):
                out_ref[0, r, :, c * LANES:(c + 1) * LANES] = (
                    qkv_scr[c, pl.ds(r, TA // dil, stride=dil), :].astype(BF16))
    cqk_ref[...] = h[:, 1280:1792]
    cv_ref[...] = h[:, 1792:2048].astype(BF16)
    co_ref[...] = h[:, 2048:2304]
    dx_ref[...] = h[:, 2304:2560]
    ig_ref[...] = h[:, 2560:2688]
    fg_ref[...] = h[:, 2688:2816]
    u = jax.nn.gelu(h[:, 0:256])
    v = jax.nn.gelu(h[:, 256:512])
    vn = _standardize(v) * lng_ref[...]
    lane_grp = lax.broadcasted_iota(jnp.int32, (GM_CHUNK, MIX_W), 1) // (MIX_W // GM_GROUPS)
    for c in range(TA // GM_CHUNK):
        vc = vn[c * GM_CHUNK:(c + 1) * GM_CHUNK]
        stacked = jnp.concatenate(
            [jnp.where(lane_grp == g, vc, 0.0).astype(BF16) for g in range(GM_GROUPS)], axis=0)
        mixed = jnp.dot(wscat_ref[...], stacked, preferred_element_type=F32) + bsfull_ref[...]
        ya_ref[c * GM_CHUNK:(c + 1) * GM_CHUNK, :] = (
            u[c * GM_CHUNK:(c + 1) * GM_CHUNK] * mixed).astype(BF16)


def _inproj_call(x2d, w_cat, b_cat, lng, wscat, bsfull, batch):
    n = x2d.shape[0]
    seq = n // batch
    tpb = seq // TA
    tok = lambda w: pl.BlockSpec((TA, w), lambda i: (i, 0))
    const = lambda s: pl.BlockSpec(s, lambda i: (0,) * len(s))
    regrouped = lambda dil: pl.BlockSpec((1, dil, TA // dil, 768), lambda i: (i // tpb, 0, i % tpb, 0))
    out_shape = (
        jax.ShapeDtypeStruct((n, 256), BF16),
    ) + tuple(jax.ShapeDtypeStruct((batch, dil, seq // dil, 768), BF16)
              for _, dil in DIL_PATTERNS) + (
        jax.ShapeDtypeStruct((n, 512), F32),
        jax.ShapeDtypeStruct((n, 256), BF16),
        jax.ShapeDtypeStruct((n, 256), F32),
        jax.ShapeDtypeStruct((n, 256), F32),
        jax.ShapeDtypeStruct((n, LANES), F32),
        jax.ShapeDtypeStruct((n, LANES), F32),
    )
    return pl.pallas_call(
        _inproj_kernel, name="inproj_gmlp",
        grid=(n // TA,),
        in_specs=[tok(D_MODEL), const((D_MODEL, A_COLS)), const((1, A_COLS)), const((1, MIX_W)),
                  const((GM_CHUNK, GM_GROUPS * GM_CHUNK)), const((GM_CHUNK, MIX_W))],
        out_specs=(tok(256),) + tuple(regrouped(dil) for _, dil in DIL_PATTERNS)
                  + (tok(512), tok(256), tok(256), tok(256), tok(LANES), tok(LANES)),
        out_shape=out_shape,
        scratch_shapes=[pltpu.VMEM((768 // LANES, TA, LANES), F32)],
        compiler_params=_cparams(1),
    )(x2d, w_cat, b_cat, lng, wscat, bsfull)


def _pack_inproj_weights(w_in, b_in):
    pad = lambda a: jnp.pad(a, ((0, 0), (0, LANES - 8)))
    w_cat = jnp.concatenate([w_in[:, 0:2304], w_in[:, 2320:2576],
                             pad(w_in[:, 2304:2312]), pad(w_in[:, 2312:2320])], axis=1)
    b2 = b_in[None, :]
    b_cat = jnp.concatenate([b2[:, 0:2304], b2[:, 2320:2576],
                             pad(b2[:, 2304:2312]), pad(b2[:, 2312:2320])], axis=1)
    return w_cat.astype(BF16), b_cat


def _pack_gmlp(gm_ws, gm_bs):
    wscat = jnp.transpose(gm_ws, (1, 0, 2)).reshape(GM_CHUNK, GM_GROUPS * GM_CHUNK).astype(BF16)
    bsfull = jnp.repeat(jnp.transpose(gm_bs), MIX_W // GM_GROUPS, axis=1)
    return wscat, bsfull


def _halo_specs(t, width, n_tiles):
    r = t // HALO
    main = pl.BlockSpec((1, t, width), lambda b, i: (b, i, 0))
    prev = pl.BlockSpec((1, HALO, width), lambda b, i: (b, jnp.maximum(i * r - 1, 0), 0))
    nxt = pl.BlockSpec((1, HALO, width), lambda b, i: (b, jnp.minimum((i + 1) * r, n_tiles * r - 1), 0))
    return main, prev, nxt


def _fill_halo_scratch(buf, x_ref, p_ref, n_ref, t):
    i = pl.program_id(1)
    last = pl.num_programs(1) - 1
    buf[0:HALO, :] = jnp.where(i > 0, p_ref[0], 0.0)
    buf[HALO:HALO + t, :] = x_ref[0]
    buf[HALO + t:2 * HALO + t, :] = jnp.where(i < last, n_ref[0], 0.0)


TP = 512


def _pool_kernel(x_ref, p_ref, n_ref, w_ref, sc_ref, o_ref, buf):
    _fill_halo_scratch(buf, x_ref, p_ref, n_ref, TP)
    seq = pl.num_programs(1) * TP
    pos = pl.program_id(1) * TP + lax.broadcasted_iota(jnp.int32, (TP, 1), 0)
    lane_grp = lax.broadcasted_iota(jnp.int32, (TP, MIX_W), 1) // (MIX_W // len(POOL_WINDOWS))
    x0 = buf[HALO:HALO + TP, :]
    pooled = jnp.zeros((TP, MIX_W), F32)
    acc = None
    half_done = 0
    for gi, win in enumerate(POOL_WINDOWS):
        half = win // 2
        for o in list(range(-half, -half_done)) + list(range(half_done, half)):
            term = buf[HALO + o:HALO + o + TP, :]
            acc = term if acc is None else acc + term
        half_done = half
        cnt = (jnp.minimum(pos + half, seq) - jnp.maximum(pos - half, 0)).astype(F32)
        pooled = jnp.where(lane_grp == gi, acc / cnt - x0, pooled)
    mixed = jnp.dot(pooled.astype(BF16), w_ref[...], preferred_element_type=F32)
    o_ref[0] = (mixed * sc_ref[...]).astype(BF16)


def _pool_call(dx, w_block, scale):
    b, s, _ = dx.shape
    nt = s // TP
    main, prev, nxt = _halo_specs(TP, MIX_W, nt)
    return pl.pallas_call(
        _pool_kernel, name="pool_mixer",
        grid=(b, nt),
        in_specs=[main, prev, nxt,
                  pl.BlockSpec((MIX_W, MIX_W), lambda b, i: (0, 0)),
                  pl.BlockSpec((1, MIX_W), lambda b, i: (0, 0))],
        out_specs=pl.BlockSpec((1, TP, MIX_W), lambda b, i: (b, i, 0)),
        out_shape=jax.ShapeDtypeStruct((b, s, MIX_W), BF16),
        scratch_shapes=[pltpu.VMEM((TP + 2 * HALO, MIX_W), F32)],
        compiler_params=_cparams(2),
    )(dx, dx, dx, w_block, scale)


TQ = 128
TQS = 512
TKEYS = TQ + 2 * ATT_BLOCK


def _attn_kernel(q_ref, kp_ref, km_ref, kn_ref, vp_ref, vm_ref, vn_ref, bias_ref, o_ref, lse_ref):
    i = pl.program_id(2)
    seq = pl.num_programs(2) * TQS
    q = q_ref[0, 0] * ATT_HD ** -0.5
    k = jnp.concatenate([kp_ref[0, 0], km_ref[0, 0], kn_ref[0, 0]], axis=0)
    v = jnp.concatenate([vp_ref[0, 0], vm_ref[0, 0], vn_ref[0, 0]], axis=0)
    lane = lax.broadcasted_iota(jnp.int32, (TQ, LANES), 1)
    for j in range(TQS // TQ):
        kpos = i * TQS + j * TQ - ATT_BLOCK + lax.broadcasted_iota(jnp.int32, (1, TKEYS), 1)
        kvalid = (kpos >= 0) & (kpos < seq)
        qrows = slice(j * TQ, (j + 1) * TQ)
        krows = slice(j * TQ, j * TQ + TKEYS)
        lse_tile = jnp.zeros((TQ, LANES), F32)
        for h in range(ATT_HEADS):
            sl = slice(h * ATT_HD, (h + 1) * ATT_HD)
            logits = lax.dot_general(q[qrows, sl], k[krows, sl], (((1,), (1,)), ((), ())),
                                     preferred_element_type=F32) + bias_ref[h]
            logits = jnp.where(kvalid, logits, NEG_BIG)
            m = jnp.max(logits, axis=-1, keepdims=True)
            p = jnp.exp(logits - m)
            ssum = jnp.sum(p, axis=-1, keepdims=True)
            o = jnp.dot(p.astype(BF16), v[krows, sl], preferred_element_type=F32) / ssum
            o_ref[0, 0, qrows, sl] = o
            lse_tile = jnp.where(lane == h, m + jnp.log(ssum), lse_tile)
        lse_ref[0, 0, qrows, :] = lse_tile


def _attn_call(qkv, bias):
    b, dil, l, _ = qkv.shape
    nt = l // TQS
    r64 = TQS // ATT_BLOCK
    main = lambda c: pl.BlockSpec((1, 1, TQS, MIX_W), lambda b, r, i: (b, r, i, c))
    prev = lambda c: pl.BlockSpec((1, 1, ATT_BLOCK, MIX_W),
                                  lambda b, r, i: (b, r, jnp.maximum(i * r64 - 1, 0), c))
    nxt = lambda c: pl.BlockSpec((1, 1, ATT_BLOCK, MIX_W),
                                 lambda b, r, i: (b, r, jnp.minimum((i + 1) * r64, nt * r64 - 1), c))
    return pl.pallas_call(
        _attn_kernel, name="band_attention",
        grid=(b, dil, nt),
        in_specs=[main(0), prev(1), main(1), nxt(1), prev(2), main(2), nxt(2),
                  pl.BlockSpec((ATT_HEADS, TQ, TKEYS), lambda b, r, i: (0, 0, 0))],
        out_specs=(pl.BlockSpec((1, 1, TQS, MIX_W), lambda b, r, i: (b, r, i, 0)),
                   pl.BlockSpec((1, 1, TQS, LANES), lambda b, r, i: (b, r, i, 0))),
        out_shape=(jax.ShapeDtypeStruct((b, dil, l, MIX_W), F32),
                   jax.ShapeDtypeStruct((b, dil, l, LANES), F32)),
        compiler_params=_cparams(3),
    )(qkv, qkv, qkv, qkv, qkv, qkv, qkv, bias)


def _t5_bucket_static(rel):
    half = REL_BUCKETS // 2
    max_exact = half // 2
    ret = np.where(rel > 0, half, 0)
    n = np.abs(rel)
    nf = np.maximum(n, 1).astype(np.float32)
    large = max_exact + (np.log(nf / np.float32(max_exact)) / np.float32(math.log(REL_MAX_DIST / max_exact))
                         * np.float32(half - max_exact)).astype(np.int32)
    large = np.minimum(large, half - 1)
    return ret + np.where(n < max_exact, n, large)


def _attn_bias_tile(rel_bias, window, dil):
    side = (window // 2) // dil
    rel = np.arange(TKEYS)[None, :] - ATT_BLOCK - np.arange(TQ)[:, None]
    onehot = jax.nn.one_hot(jnp.asarray(_t5_bucket_static(dil * rel), jnp.int32), REL_BUCKETS, dtype=F32)
    bias = jnp.einsum('qkr,rh->hqk', onehot, rel_bias, precision=lax.Precision.HIGHEST)
    return jnp.where(jnp.asarray(np.abs(rel) <= side)[None], bias, NEG_BIG)


TM = 512
VT_ROWS = ML_HD + 16


def _mlprep_kernel(x_ref, p_ref, n_ref, v_ref, w_ref, qt_out, k_out, vt_out, buf):
    _fill_halo_scratch(buf, x_ref, p_ref, n_ref, TM)
    conv = (buf[HALO - 1:HALO - 1 + TM, :] * w_ref[0:1, :] + buf[HALO:HALO + TM, :] * w_ref[1:2, :]
            + buf[HALO + 1:HALO + 1 + TM, :] * w_ref[2:3, :])
    qk = jax.nn.silu(conv)
    qt = jnp.transpose(qk[:, :MIX_W])
    vt = jnp.transpose(v_ref[0].astype(F32))
    ones_rows = jnp.where(lax.broadcasted_iota(jnp.int32, (VT_ROWS - ML_HD, ML_CHUNK), 0) == 0, 1.0, 0.0)
    for h in range(ML_HEADS):
        sl = slice(h * ML_HD, (h + 1) * ML_HD)
        k_out[0, h] = (qk[:, MIX_W + h * ML_HD:MIX_W + (h + 1) * ML_HD] * ML_HD ** -0.5).astype(BF16)
        for c in range(TM // ML_CHUNK):
            cl = slice(c * ML_CHUNK, (c + 1) * ML_CHUNK)
            qt_out[0, h, c] = qt[sl, cl].astype(BF16)
            vt_out[0, h, c] = jnp.concatenate([vt[sl, cl], ones_rows], axis=0).astype(BF16)


def _mlprep_call(cqk, cv, conv_w):
    b, s, _ = cqk.shape
    nt = s // TM
    nc = s // ML_CHUNK
    cpt = TM // ML_CHUNK
    main, prev, nxt = _halo_specs(TM, 2 * MIX_W, nt)
    return pl.pallas_call(
        _mlprep_kernel, name="mlstm_prep",
        grid=(b, nt),
        in_specs=[main, prev, nxt,
                  pl.BlockSpec((1, TM, MIX_W), lambda b, i: (b, i, 0)),
                  pl.BlockSpec((3, 2 * MIX_W), lambda b, i: (0, 0))],
        out_specs=(pl.BlockSpec((1, ML_HEADS, cpt, ML_HD, ML_CHUNK), lambda b, i: (b, 0, i, 0, 0)),
                   pl.BlockSpec((1, ML_HEADS, TM, ML_HD), lambda b, i: (b, 0, i, 0)),
                   pl.BlockSpec((1, ML_HEADS, cpt, VT_ROWS, ML_CHUNK), lambda b, i: (b, 0, i, 0, 0))),
        out_shape=(jax.ShapeDtypeStruct((b, ML_HEADS, nc, ML_HD, ML_CHUNK), BF16),
                   jax.ShapeDtypeStruct((b, ML_HEADS, s, ML_HD), BF16),
                   jax.ShapeDtypeStruct((b, ML_HEADS, nc, VT_ROWS, ML_CHUNK), BF16)),
        scratch_shapes=[pltpu.VMEM((TM + 2 * HALO, 2 * MIX_W), F32)],
        compiler_params=_cparams(2),
    )(cqk, cqk, cqk, cv, conv_w)


GP_BLK = 512


def _chunk_scan(x, fwd_lane, t_in_chunk, op, ident):
    n = x.shape[0]
    shift = 1
    while shift < ML_CHUNK:
        down = pltpu.roll(x, shift, 0)
        up = pltpu.roll(x, n - shift, 0)
        nb = jnp.where(fwd_lane,
                       jnp.where(t_in_chunk >= shift, down, ident),
                       jnp.where(t_in_chunk < ML_CHUNK - shift, up, ident))
        x = op(x, nb)
        shift *= 2
    return x


def _gatescan_kernel(ig_ref, fg_ref, fb_ref, b_ref, a_ref, cm_ref, g_ref, amax_ref):
    lane = lax.broadcasted_iota(jnp.int32, (1, LANES), 1)
    fwd_lane = lane < ML_HEADS
    t_in_chunk = lax.broadcasted_iota(jnp.int32, (GP_BLK, 1), 0) % ML_CHUNK
    z = fg_ref[0] + fb_ref[...]
    lf = jnp.minimum(z, 0.0) - jnp.log1p(jnp.exp(-jnp.abs(z)))
    b = _chunk_scan(lf, fwd_lane, t_in_chunk, jnp.add, 0.0)
    a = ig_ref[0] - b
    b_ref[0] = b
    a_ref[0] = a
    cm_ref[0] = _chunk_scan(a, fwd_lane, t_in_chunk, jnp.maximum, -jnp.inf)
    cpb = GP_BLK // ML_CHUNK
    last = pl.ds(ML_CHUNK - 1, cpb, stride=ML_CHUNK)
    first = pl.ds(0, cpb, stride=ML_CHUNK)
    g_ref[0] = jnp.where(fwd_lane, b_ref[0, last, :], b_ref[0, first, :])
    amax_ref[0] = jnp.where(fwd_lane, cm_ref[0, last, :], cm_ref[0, first, :])


def _gateout_kernel(b_ref, a_ref, cm_ref, g_ref, amax_ref, ws_ref, m_ref, iw_ref, en_ref, dec_ref,
                    mch_s, mlast_s):
    nc = g_ref.shape[1]
    j = pl.program_id(1)
    fwd_lane = lax.broadcasted_iota(jnp.int32, (1, LANES), 1) < ML_HEADS

    @pl.when(j == 0)
    def _():
        def m_step(i, carry):
            mf, mb = carry
            cf = pl.ds(i, 1)
            cb = pl.ds(nc - 1 - i, 1)
            mch_s[cf, :] = jnp.where(fwd_lane, mf, mch_s[cf, :])
            mch_s[cb, :] = jnp.where(fwd_lane, mch_s[cb, :], mb)
            mf = g_ref[0, cf, :] + jnp.maximum(mf, amax_ref[0, cf, :])
            mb = g_ref[0, cb, :] + jnp.maximum(mb, amax_ref[0, cb, :])
            return mf, mb

        mch_s[...] = jnp.zeros((nc, LANES), F32)
        zero = jnp.zeros((1, LANES), F32)
        lax.fori_loop(0, nc, m_step, (zero, zero))
        mlast = jnp.maximum(amax_ref[0], mch_s[...])
        mlast_s[...] = mlast
        dec_ref[0] = jnp.exp(mch_s[...] - mlast)

    cpb = GP_BLK // ML_CHUNK
    crow = pl.ds(pl.multiple_of(j * cpb, cpb), cpb)
    expand = lambda t: jnp.broadcast_to(t[:, None, :], (cpb, ML_CHUNK, LANES)).reshape(GP_BLK, LANES)
    m_tok = expand(mch_s[crow, :])
    mlast_tok = expand(mlast_s[crow, :])
    mt = jnp.maximum(cm_ref[0], m_tok)
    m_ref[0] = mt
    iw_ref[0] = jnp.exp(m_tok - mt)
    en_ref[0] = jnp.exp(-(b_ref[0] + mt))
    ws_ref[0] = jnp.exp(a_ref[0] - mlast_tok)


def _gateprep_call(ig, fg, fbias_row):
    b, s, _ = ig.shape
    nc = s // ML_CHUNK
    cpb = GP_BLK // ML_CHUNK
    tok = pl.BlockSpec((1, GP_BLK, LANES), lambda i, j: (i, j, 0))
    tok_shape = jax.ShapeDtypeStruct((b, s, LANES), F32)
    chunk_shape = jax.ShapeDtypeStruct((b, nc, LANES), F32)
    chunk_tile = pl.BlockSpec((1, cpb, LANES), lambda i, j: (i, j, 0))
    chunk_all = pl.BlockSpec((1, nc, LANES), lambda i, j: (i, 0, 0))
    bcum, a, cm, g, amax = pl.pallas_call(
        _gatescan_kernel, name="mlstm_gate_scan",
        grid=(b, s // GP_BLK),
        in_specs=[tok, tok, pl.BlockSpec((1, LANES), lambda i, j: (0, 0))],
        out_specs=(tok, tok, tok, chunk_tile, chunk_tile),
        out_shape=(tok_shape,) * 3 + (chunk_shape,) * 2,
        compiler_params=_cparams(2),
    )(ig, fg, fbias_row)
    ws, m_col, iw_col, en_col, decay = pl.pallas_call(
        _gateout_kernel, name="mlstm_gate_out",
        grid=(b, s // GP_BLK),
        in_specs=[tok, tok, tok, chunk_all, chunk_all],
        out_specs=(tok, tok, tok, tok, chunk_all),
        out_shape=(tok_shape,) * 4 + (chunk_shape,),
        scratch_shapes=[pltpu.VMEM((nc, LANES), F32)] * 2,
        compiler_params=_cparams(2),
    )(bcum, a, cm, g, amax)
    return a, ws, m_col, iw_col, en_col, decay


TE = 512


def _mlstm_kernel(*refs):
    fwd, bwd, (hf_ref, hb_ref, state) = refs[:9], refs[9:18], refs[18:]
    i = pl.program_id(1)

    @pl.when(i == 0)
    def _():
        state[...] = jnp.zeros(state.shape, F32)

    cpt = TE // ML_CHUNK
    s_idx = lax.broadcasted_iota(jnp.int32, (ML_CHUNK, ML_CHUNK), 0)
    t_idx = lax.broadcasted_iota(jnp.int32, (ML_CHUNK, ML_CHUNK), 1)

    def chunk_body(c, carry):
        jobs = []
        for d, (qt_r, k_r, vt_r, a_r, m_r, iw_r, en_r, ws_r, dec_r), out_r in ((0, fwd, hf_ref), (1, bwd, hb_ref)):
            cc = c if d == 0 else cpt - 1 - c
            rows = pl.ds(pl.multiple_of(cc * ML_CHUNK, ML_CHUNK), ML_CHUNK)
            crow = pl.ds(cc, 1)
            for h in range(ML_HEADS):
                ch = d * ML_HEADS + h
                row = lambda r: r[0, ch, crow, :]
                jobs.append(dict(
                    ch=ch, tri=(s_idx >= t_idx) if d else (s_idx <= t_idx),
                    k=k_r[0, h, rows, :], qt=qt_r[0, h, cc], vt=vt_r[0, h, cc], a=a_r[0, rows, ch:ch + 1],
                    m=row(m_r), iw=row(iw_r), en=row(en_r), ws=row(ws_r), dec=row(dec_r),
                    out=(out_r, cc, h)))
        for j in jobs:
            j["cst"] = state[j["ch"]]
            j["st"] = jnp.dot(j["k"], j["qt"], preferred_element_type=F32)
            j["inter"] = jnp.dot(j["cst"].astype(BF16), j["qt"], preferred_element_type=F32)
            j["upd"] = jnp.dot((j["vt"].astype(F32) * j["ws"]).astype(BF16), j["k"],
                               preferred_element_type=F32)
        for j in jobs:
            j["swt"] = j["st"] * jnp.exp(jnp.where(j["tri"], j["a"] - j["m"], NEG_BIG))
            j["intra"] = jnp.dot(j["vt"], j["swt"].astype(BF16), preferred_element_type=F32)
        for j in jobs:
            den = jnp.sum(j["swt"], axis=0, keepdims=True) + j["iw"] * j["inter"][ML_HD:ML_HD + 1]
            tot = j["intra"][:ML_HD] + j["iw"] * j["inter"][:ML_HD]
            out_r, cc, h = j["out"]
            out_r[0, cc, h * ML_HD:(h + 1) * ML_HD, :] = tot / jnp.maximum(jnp.abs(den), j["en"])
            state[j["ch"]] = j["dec"] * j["cst"] + j["upd"]
        return carry

    lax.fori_loop(0, cpt, chunk_body, 0)


def _mlstm_call(qt, k, vt, a_col, m_row, iw_row, en_row, ws_row, dec_row):
    b, _, s, _ = k.shape
    nt = s // TE
    nc = s // ML_CHUNK
    cpt = TE // ML_CHUNK

    def specs(rev):
        ti = (lambda i: nt - 1 - i) if rev else (lambda i: i)
        row = pl.BlockSpec((1, 2 * ML_HEADS, cpt, ML_CHUNK), lambda b, i: (b, 0, ti(i), 0))
        return [
            pl.BlockSpec((1, ML_HEADS, cpt, ML_HD, ML_CHUNK), lambda b, i: (b, 0, ti(i), 0, 0)),
            pl.BlockSpec((1, ML_HEADS, TE, ML_HD), lambda b, i: (b, 0, ti(i), 0)),
            pl.BlockSpec((1, ML_HEADS, cpt, VT_ROWS, ML_CHUNK), lambda b, i: (b, 0, ti(i), 0, 0)),
            pl.BlockSpec((1, TE, LANES), lambda b, i: (b, ti(i), 0)),
            row, row, row, row, row]

    args = [qt, k, vt, a_col, m_row, iw_row, en_row, ws_row, dec_row]
    out_f = pl.BlockSpec((1, cpt, MIX_W, ML_CHUNK), lambda b, i: (b, i, 0, 0))
    out_b = pl.BlockSpec((1, cpt, MIX_W, ML_CHUNK), lambda b, i: (b, nt - 1 - i, 0, 0))
    return pl.pallas_call(
        _mlstm_kernel, name="mlstm_scan",
        grid=(b, nt),
        in_specs=specs(False) + specs(True),
        out_specs=(out_f, out_b),
        out_shape=(jax.ShapeDtypeStruct((b, nc, MIX_W, ML_CHUNK), F32),) * 2,
        scratch_shapes=[pltpu.VMEM((2 * ML_HEADS, VT_ROWS, ML_HD), F32)],
        compiler_params=_cparams(2),
    )(*args, *args)


def _mlstm_branch(cqk, cv, ig, fg, conv_w, fbias):
    b, s, _ = cqk.shape
    nc = s // ML_CHUNK
    qt, k, vt = _mlprep_call(cqk, cv, conv_w)
    fb_row = jnp.pad(fbias.reshape(1, 2 * ML_HEADS), ((0, 0), (0, LANES - 2 * ML_HEADS)))
    a, ws, m_col, iw_col, en_col, decay = _gateprep_call(ig, fg, fb_row)
    rows = lambda t: jnp.transpose(t[..., :2 * ML_HEADS], (0, 2, 1)).reshape(b, 2 * ML_HEADS, nc, ML_CHUNK)
    dec_row = jnp.broadcast_to(jnp.transpose(decay[..., :2 * ML_HEADS], (0, 2, 1))[..., None],
                               (b, 2 * ML_HEADS, nc, ML_CHUNK))
    return _mlstm_call(qt, k, vt, a, rows(m_col), rows(iw_col), rows(en_col), rows(ws), dec_row)


TF = 512


def _merge_kernel(x_ref, ya_ref, o1_ref, o2_ref, o3_ref, l1_ref, l2_ref, l3_ref, hf_ref, hb_ref,
                  co_ref, yd_ref, wg_ref, bg_ref, wbr_ref, wout_ref, mng_ref, lng_ref, lnb_ref,
                  wr_ref, x1_ref, x1p_ref, aff_ref, o_scr, l_scr, *, alpha):
    x = x_ref[0]
    xb = x.astype(BF16)

    def natural_order(src_ref, scr):
        dil, width = src_ref.shape[1], src_ref.shape[3]
        if dil == 1:
            return src_ref[0, 0]
        for r in range(dil):
            for c in range(width // LANES):
                scr[c, pl.ds(r, TF // dil, stride=dil), :] = src_ref[0, r, :, c * LANES:(c + 1) * LANES]
        return jnp.concatenate([scr[c] for c in range(width // LANES)], axis=1)

    lane_head = lax.broadcasted_iota(jnp.int32, (TF, MIX_W), 1) // ML_HD
    l1, l2, l3 = [natural_order(r, l_scr.at[p]) for p, r in enumerate((l1_ref, l2_ref, l3_ref))]
    o1, o2, o3 = [natural_order(r, o_scr.at[p]) for p, r in enumerate((o1_ref, o2_ref, o3_ref))]
    lm = jnp.maximum(jnp.maximum(l1, l2), l3)
    e1, e2, e3 = jnp.exp(l1 - lm), jnp.exp(l2 - lm), jnp.exp(l3 - lm)
    inv = 1.0 / (e1 + e2 + e3)

    def per_head(w):
        out = jnp.zeros((TF, MIX_W), F32)
        for h in range(ATT_HEADS):
            out = jnp.where(lane_head == h, w[:, h:h + 1], out)
        return out

    y_b = per_head(e1 * inv) * o1 + per_head(e2 * inv) * o2 + per_head(e3 * inv) * o3
    hsum_t = jnp.concatenate([hf_ref[0, c] + hb_ref[0, c] for c in range(TF // ML_CHUNK)], axis=1)
    per_head_rows = hsum_t.reshape(ML_HEADS, ML_HD, TF)
    mu = jnp.mean(per_head_rows, axis=1, keepdims=True)
    cen = per_head_rows - mu
    var = jnp.mean(cen * cen, axis=1, keepdims=True)
    hn_t = (cen * lax.rsqrt(var + LN_EPS)).reshape(MIX_W, TF)
    y_c_t = (jax.nn.sigmoid(jnp.transpose(co_ref[0])) * (hn_t * mng_ref[...])).astype(BF16)
    ys = (ya_ref[0], y_b.astype(BF16), None, yd_ref[0])
    merged = jnp.zeros((TF, D_MODEL), F32)
    for n in range(N_BRANCH):
        cols = slice(n * D_MODEL, (n + 1) * D_MODEL)
        gate = jax.nn.sigmoid(jnp.dot(xb, wg_ref[:, cols], preferred_element_type=F32) + bg_ref[:, cols])
        if ys[n] is None:
            proj = lax.dot_general(y_c_t, wbr_ref[n], (((0,), (0,)), ((), ())), preferred_element_type=F32)
        else:
            proj = jnp.dot(ys[n], wbr_ref[n], preferred_element_type=F32)
        merged = merged + gate * proj
    mix = jnp.dot(merged.astype(BF16), wout_ref[...], preferred_element_type=F32)
    x1 = _standardize(alpha * x + mix) * lng_ref[...] + lnb_ref[...]
    x1_ref[0] = x1
    x1b = x1.astype(BF16)
    x1p_ref[0] = _pack_bf16_pair(x1b[:, :D_MODEL // 2], x1b[:, D_MODEL // 2:])
    logits = lax.dot_general(wr_ref[...], x1b, (((1,), (1,)), ((), ())),
                             preferred_element_type=F32)
    ex = jnp.exp(logits - jnp.max(logits, axis=0, keepdims=True))
    aff_ref[0] = ex / jnp.sum(ex, axis=0, keepdims=True)


def _merge_call(x, ya, o_list, l_list, hf, hb, co, yd, wg, bg, wbr, wout, mng, lng, lnb, wr_t, alpha):
    b, s, _ = x.shape
    tok = lambda w: pl.BlockSpec((1, TF, w), lambda b, i: (b, i, 0))
    grouped = lambda dil, w: pl.BlockSpec((1, dil, TF // dil, w), lambda b, i: (b, 0, i, 0))
    chunked = pl.BlockSpec((1, TF // ML_CHUNK, MIX_W, ML_CHUNK), lambda b, i: (b, i, 0, 0))
    const = lambda shp: pl.BlockSpec(shp, lambda b, i: (0,) * len(shp))
    return pl.pallas_call(
        functools.partial(_merge_kernel, alpha=alpha), name="merge_ln_router",
        grid=(b, s // TF),
        in_specs=[tok(D_MODEL), tok(MIX_W)] + [grouped(dil, MIX_W) for _, dil in DIL_PATTERNS]
                 + [grouped(dil, LANES) for _, dil in DIL_PATTERNS]
                 + [chunked, chunked, tok(MIX_W), tok(MIX_W)]
                 + [const((D_MODEL, N_BRANCH * D_MODEL)), const((1, N_BRANCH * D_MODEL)),
                    const((N_BRANCH, MIX_W, D_MODEL)), const((D_MODEL, D_MODEL)), const((MIX_W, 1)),
                    const((1, D_MODEL)), const((1, D_MODEL)), const((N_EXPERTS, D_MODEL))],
        out_specs=(tok(D_MODEL), tok(D_MODEL // 2), pl.BlockSpec((1, N_EXPERTS, TF), lambda b, i: (b, 0, i))),
        out_shape=(jax.ShapeDtypeStruct((b, s, D_MODEL), F32),
                   jax.ShapeDtypeStruct((b, s, D_MODEL // 2), jnp.int32),
                   jax.ShapeDtypeStruct((b, N_EXPERTS, s), F32)),
        scratch_shapes=[pltpu.VMEM((len(DIL_PATTERNS), MIX_W // LANES, TF, LANES), F32),
                        pltpu.VMEM((len(DIL_PATTERNS), 1, TF, LANES), F32)],
        compiler_params=_cparams(2),
    )(x, ya, *o_list, *l_list, hf, hb, co, yd, wg, bg, wbr, wout, mng, lng, lnb, wr_t)


TT = 256


def _select_kernel(aff_ref, slot_ref, *, cap):
    s = aff_ref.shape[2]
    bits = pltpu.bitcast(aff_ref[0], jnp.int32)

    def bit_step(i, thr):
        cand = thr | jnp.left_shift(jnp.int32(1), 30 - i)
        cnt = jnp.sum((bits >= cand).astype(jnp.int32), axis=1, keepdims=True)
        return jnp.where(cnt >= cap, cand, thr)

    thr = lax.fori_loop(0, 31, bit_step, jnp.zeros((N_EXPERTS, 1), jnp.int32))
    gt = bits > thr
    eq = bits == thr
    need = (cap - jnp.sum(gt.astype(jnp.int32), axis=1, keepdims=True)).astype(F32)
    upper = (lax.broadcasted_iota(jnp.int32, (TT, TT), 0)
             <= lax.broadcasted_iota(jnp.int32, (TT, TT), 1)).astype(BF16)
    eq_before = jnp.zeros((N_EXPERTS, 1), F32)
    sel_before = jnp.zeros((N_EXPERTS, 1), F32)
    for j in range(s // TT):
        cols = slice(j * TT, (j + 1) * TT)
        eq_j = eq[:, cols]
        eq_incl = eq_before + jnp.dot(eq_j.astype(BF16), upper, preferred_element_type=F32)
        sel_j = gt[:, cols] | (eq_j & (eq_incl <= need))
        sel_f = sel_j.astype(F32)
        sel_incl = sel_before + jnp.dot(sel_f.astype(BF16), upper, preferred_element_type=F32)
        slot_ref[0, :, cols] = jnp.where(sel_j, sel_incl - 1.0, -1.0).astype(jnp.int32)
        eq_before = eq_incl[:, TT - 1:TT]
        sel_before = sel_incl[:, TT - 1:TT]


def _select_call(aff_t, cap):
    b, e, s = aff_t.shape
    return pl.pallas_call(
        functools.partial(_select_kernel, cap=cap), name="expert_choice_select",
        grid=(b,),
        in_specs=[pl.BlockSpec((1, e, s), lambda i: (i, 0, 0))],
        out_specs=pl.BlockSpec((1, e, s), lambda i: (i, 0, 0)),
        out_shape=jax.ShapeDtypeStruct((b, e, s), jnp.int32),
        compiler_params=_cparams(1),
    )(aff_t)


SC_LANES = 16
SC_ROWS = 64
SC_IDX = 128
SC_SLAB = 128
SC_ZROWS = 64
CF = 1024


def _sc_dispatch_call(x_flat, slot2, aff2, seq, cap):
    n_pairs = slot2.shape[0]
    d = x_flat.shape[1]
    info = plsc.get_sparse_core_info()
    n_workers = info.num_cores * info.num_subcores
    assert n_pairs % n_workers == 0 and seq % SC_LANES == 0 and cap % (2 * SC_ROWS) == 0
    pairs_per_worker = n_pairs // n_workers
    mesh = plsc.VectorSubcoreMesh(core_axis_name="c", subcore_axis_name="s")

    @functools.partial(
        pl.kernel, mesh=mesh, name="expert_dispatch_sc",
        compiler_params=pltpu.CompilerParams(needs_layout_passes=False),
        out_type=(jax.ShapeDtypeStruct((n_pairs * cap, d), x_flat.dtype),
                  jax.ShapeDtypeStruct((n_pairs, cap), jnp.int32),
                  jax.ShapeDtypeStruct((n_pairs, cap), F32)),
        scratch_types=[pltpu.VMEM((seq,), jnp.int32), pltpu.VMEM((seq,), F32),
                       pltpu.VMEM((cap,), jnp.int32), pltpu.VMEM((cap,), jnp.int32),
                       pltpu.VMEM((cap,), F32),
                       pltpu.VMEM((2, SC_ROWS, d), x_flat.dtype), pltpu.SemaphoreType.DMA((2,))])
    def dispatch(x_hbm, slot_hbm, aff_hbm, xs_hbm, tok_hbm, gate_hbm,
                 slot_v, aff_v, idx_v, tok_v, gate_v, rows_v, sem):
        worker = lax.axis_index("s") * info.num_cores + lax.axis_index("c")
        lane = lax.iota(jnp.int32, SC_LANES)

        def gather(c0, buf):
            return pltpu.make_async_copy(x_hbm.at[idx_v.at[pl.ds(c0, SC_ROWS)]], rows_v.at[buf], sem.at[buf])
        for k in range(pairs_per_worker):
            pair = worker * pairs_per_worker + k
            row0 = (pair // N_EXPERTS) * seq
            pltpu.sync_copy(slot_hbm.at[pair], slot_v)
            pltpu.sync_copy(aff_hbm.at[pair], aff_v)

            @pl.loop(0, seq, step=SC_LANES)
            def _(t0):
                sv = slot_v[pl.ds(t0, SC_LANES)]
                picked = sv >= 0
                plsc.store_scatter(tok_v, [sv], t0 + lane, mask=picked)
                plsc.store_scatter(idx_v, [sv], row0 + t0 + lane, mask=picked)
                plsc.store_scatter(gate_v, [sv], aff_v[pl.ds(t0, SC_LANES)], mask=picked)

            pltpu.sync_copy(tok_v, tok_hbm.at[pair])
            pltpu.sync_copy(gate_v, gate_hbm.at[pair])

            gather(0, 0).start()

            @pl.loop(0, cap, step=2 * SC_ROWS)
            def _(c0):
                gather(c0 + SC_ROWS, 1).start()
                gather(c0, 0).wait()
                pltpu.sync_copy(rows_v.at[0], xs_hbm.at[pl.ds(pair * cap + c0, SC_ROWS)])

                @pl.when(c0 + 2 * SC_ROWS < cap)
                def _():
                    gather(c0 + 2 * SC_ROWS, 0).start()

                gather(c0 + SC_ROWS, 1).wait()
                pltpu.sync_copy(rows_v.at[1], xs_hbm.at[pl.ds(pair * cap + c0 + SC_ROWS, SC_ROWS)])

    return dispatch(x_flat, slot2, aff2)


def _expert_kernel(xs_ref, g_ref, w1_ref, w3_ref, w2_ref, ye_ref, w1_bf, w3_bf, w2_bf):
    @pl.when((pl.program_id(1) == 0) & (pl.program_id(2) == 0))
    def _():
        w1_bf[...] = w1_ref[0, 0].astype(BF16)
        w3_bf[...] = w3_ref[0, 0].astype(BF16)
        w2_bf[...] = w2_ref[0, 0].astype(BF16)

    xs = jnp.concatenate(_unpack_bf16_pair(xs_ref[0, 0]), axis=1)
    hid = (jax.nn.silu(jnp.dot(xs, w1_bf[...], preferred_element_type=F32))
           * jnp.dot(xs, w3_bf[...], preferred_element_type=F32))
    ye_ref[0, 0] = jnp.dot(hid.astype(BF16), w2_bf[...], preferred_element_type=F32) * g_ref[0, 0]


def _expert_call(xs4, gate4, w1, w3, w2, layer):
    b, e, cap, half = xs4.shape
    d, ff = w1.shape[2], w1.shape[3]
    assert d == 2 * half
    rows = lambda w: pl.BlockSpec((1, 1, CF, w), lambda e, b, j: (b, e, j, 0))
    wspec = lambda r, c: pl.BlockSpec((1, 1, r, c), lambda e, b, j: (layer, e, 0, 0))
    return pl.pallas_call(
        _expert_kernel, name="expert_ffn",
        grid=(e, b, cap // CF),
        in_specs=[rows(half), rows(1), wspec(d, ff), wspec(d, ff), wspec(ff, d)],
        out_specs=rows(d),
        out_shape=jax.ShapeDtypeStruct((b, e, cap, d), F32),
        scratch_shapes=[pltpu.VMEM((d, ff), BF16), pltpu.VMEM((d, ff), BF16), pltpu.VMEM((ff, d), BF16)],
        compiler_params=_cparams(3),
    )(xs4, gate4, w1, w3, w2)


def _sc_combine_call(ye_flat, tok3, seq):
    n_pairs, n_chunks, _ = tok3.shape
    cap = n_chunks * SC_IDX
    d = ye_flat.shape[1]
    nb = n_pairs // N_EXPERTS
    info = plsc.get_sparse_core_info()
    assert info.num_subcores == N_EXPERTS and nb % info.num_cores == 0
    assert seq % (info.num_subcores * SC_ZROWS) == 0 and d % SC_SLAB == 0
    batches_per_core = nb // info.num_cores
    own_rows = seq // info.num_subcores
    mesh = plsc.VectorSubcoreMesh(core_axis_name="c", subcore_axis_name="s")

    @functools.partial(
        pl.kernel, mesh=mesh, name="expert_combine_sc",
        compiler_params=pltpu.CompilerParams(needs_layout_passes=False),
        out_type=jax.ShapeDtypeStruct((nb * seq, d), F32),
        scratch_types=[pltpu.VMEM_SHARED((seq, SC_SLAB), F32),
                       pltpu.VMEM((n_chunks, SC_IDX), jnp.int32),
                       pltpu.VMEM((SC_IDX, SC_SLAB), F32),
                       pltpu.VMEM((SC_ZROWS, SC_SLAB), F32)])
    def combine(ye_hbm, tok_hbm, out_hbm, acc_sh, tok_v, rows_v, zero_v):
        core = lax.axis_index("c")
        sub = lax.axis_index("s")

        @pl.loop(0, SC_ZROWS)
        def _(r):
            for l0 in range(0, SC_SLAB, SC_LANES):
                zero_v[r, pl.ds(l0, SC_LANES)] = jnp.zeros((SC_LANES,), F32)

        for bb in range(batches_per_core):
            batch = core * batches_per_core + bb
            pair = batch * N_EXPERTS + sub
            pltpu.sync_copy(tok_hbm.at[pair], tok_v)

            @pl.loop(0, d // SC_SLAB)
            def _(slab):
                cols = pl.ds(pl.multiple_of(slab * SC_SLAB, SC_SLAB), SC_SLAB)

                @pl.loop(0, own_rows, step=SC_ZROWS)
                def _(r0):
                    pltpu.sync_copy(zero_v, acc_sh.at[pl.ds(sub * own_rows + r0, SC_ZROWS)])

                plsc.subcore_barrier()

                @pl.loop(0, n_chunks)
                def _(j):
                    pltpu.sync_copy(ye_hbm.at[pl.ds(pair * cap + j * SC_IDX, SC_IDX), cols], rows_v)
                    pltpu.sync_copy(rows_v, acc_sh.at[tok_v.at[j]], add=True)

                plsc.subcore_barrier()
                pltpu.sync_copy(acc_sh.at[pl.ds(sub * own_rows, own_rows)],
                                out_hbm.at[pl.ds(batch * seq + sub * own_rows, own_rows), cols])

    return combine(ye_flat, tok3)


TN = 512


def _resln_kernel(x_ref, y_ref, g_ref, b_ref, o_ref, *, alpha):
    o_ref[...] = _standardize(alpha * x_ref[...] + y_ref[...]) * g_ref[...] + b_ref[...]


def _resln_call(x2d, y2d, g, bta, alpha):
    n, d = x2d.shape
    tok = pl.BlockSpec((TN, d), lambda i: (i, 0))
    vec = pl.BlockSpec((1, d), lambda i: (0, 0))
    return pl.pallas_call(
        functools.partial(_resln_kernel, alpha=alpha), name="residual_layernorm",
        grid=(n // TN,), in_specs=[tok, tok, vec, vec], out_specs=tok,
        out_shape=jax.ShapeDtypeStruct((n, d), F32),
        compiler_params=_cparams(1),
    )(x2d, y2d, g, bta)


def _expert_choice_ffn(x1p, aff_t, w1, w3, w2, layer):
    b, s, half = x1p.shape
    d = 2 * half
    cap = EC_FACTOR * s // N_EXPERTS
    slot = _select_call(aff_t, cap)
    xs, tok, gate = _sc_dispatch_call(x1p.reshape(b * s, half), slot.reshape(b * N_EXPERTS, s),
                                      aff_t.reshape(b * N_EXPERTS, s), s, cap)
    ye = _expert_call(xs.reshape(b, N_EXPERTS, cap, half), gate.reshape(b, N_EXPERTS, cap, 1), w1, w3, w2,
                      layer)
    out = _sc_combine_call(ye.reshape(b * N_EXPERTS * cap, d),
                           tok.reshape(b * N_EXPERTS, cap // SC_IDX, SC_IDX), s)
    return out.reshape(b, s, d)


def _pack_pool(pool_w):
    g, gd, _ = pool_w.shape
    out = jnp.zeros((g * gd, g * gd), F32)
    for i in range(g):
        out = out.at[i * gd:(i + 1) * gd, i * gd:(i + 1) * gd].set(pool_w[i])
    return out.astype(BF16)


def _layer(layer, x, alpha, bias_tiles, w_in, b_in, gm_ln_g, gm_ws, gm_bs, ml_conv, ml_fbias, ml_norm_g,
           pool_w, pool_scale, w_branch, w_out, ln1_g, ln1_b, w_router, w_e1, w_e3, w_e2, ln2_g, ln2_b):
    b, s, d = x.shape
    n_small = 2576
    w_cat, b_cat = _pack_inproj_weights(w_in, b_in)
    wscat, bsfull = _pack_gmlp(gm_ws, gm_bs)
    ya, qkv1, qkv4, qkv16, cqk, cv, co, dx, ig, fg = _inproj_call(
        x.reshape(b * s, d), w_cat, b_cat, gm_ln_g[None], wscat, bsfull, b)
    r3 = lambda t: t.reshape(b, s, t.shape[-1])
    o_list, l_list = [], []
    for qkv, bias in zip((qkv1, qkv4, qkv16), bias_tiles):
        o, lse = _attn_call(qkv, bias)
        o_list.append(o)
        l_list.append(lse)
    hf, hb = _mlstm_branch(r3(cqk), r3(cv), r3(ig), r3(fg), ml_conv, ml_fbias)
    yd = _pool_call(r3(dx), _pack_pool(pool_w), pool_scale[None])
    x1, x1p, aff_t = _merge_call(
        x, r3(ya), o_list, l_list, hf, hb, r3(co), yd,
        w_in[:, n_small:].astype(BF16), b_in[None, n_small:], w_branch.astype(BF16), w_out.astype(BF16),
        ml_norm_g[:, None], ln1_g[None], ln1_b[None], jnp.transpose(w_router).astype(BF16), alpha)
    ffn = _expert_choice_ffn(x1p, aff_t, w_e1, w_e3, w_e2, layer)
    x2 = _resln_call(x1.reshape(b * s, d), ffn.reshape(b * s, d), ln2_g[None], ln2_b[None], alpha)
    return x2.reshape(b, s, d)


def kernel(x, w_in, b_in, gm_ln_g, gm_ws, gm_bs, rel_bias, ml_conv, ml_fbias, ml_norm_g, pool_w,
           pool_scale, w_branch, w_out, ln1_g, ln1_b, w_router, w_e1, w_e3, w_e2, ln2_g, ln2_b):
    depth = w_in.shape[0]
    alpha = (2 * depth) ** 0.25
    bias_tiles = [_attn_bias_tile(rel_bias, window, dil) for window, dil in DIL_PATTERNS]
    for l in range(depth):
        x = _layer(l, x, alpha, bias_tiles, w_in[l], b_in[l], gm_ln_g[l], gm_ws[l], gm_bs[l], ml_conv[l],
                   ml_fbias[l], ml_norm_g[l], pool_w[l], pool_scale[l], w_branch[l], w_out[l],
                   ln1_g[l], ln1_b[l], w_router[l], w_e1, w_e3, w_e2, ln2_g[l], ln2_b[l])
    return x
```

```python
import functools
import math

import jax
import jax.numpy as jnp
import numpy as np
from jax import lax
from jax.experimental import pallas as pl
from jax.experimental.pallas import tpu as pltpu
from jax.experimental.pallas import tpu_sc as plsc

F32 = jnp.float32
BF16 = jnp.bfloat16

D_MODEL = 1024
MIX_W = 256
N_BRANCH = 4
GM_CHUNK = 128
GM_GROUPS = 4
ATT_HEADS = 4
ATT_HD = 64
DIL_PATTERNS = ((128, 1), (512, 4), (2048, 16))
ATT_BLOCK = 64
REL_BUCKETS = 32
REL_MAX_DIST = 1024
ML_HEADS = 4
ML_HD = 64
ML_CHUNK = 64
POOL_WINDOWS = (2, 4, 8, 16)
N_EXPERTS = 16
EXPERT_FF = 1024
EC_FACTOR = 2
LN_EPS = 1e-5
NEG_BIG = -1e30

V7X_VMEM_LIMIT = 56 * 1024 * 1024
LANES = 128
HALO = 8


def _cparams(n_grid, vmem=V7X_VMEM_LIMIT):
    return pltpu.CompilerParams(dimension_semantics=("arbitrary",) * n_grid,
                                vmem_limit_bytes=vmem)


def _pack_bf16_pair(lo, hi):
    lo_bits = lax.shift_right_logical(pltpu.bitcast(lo.astype(F32), jnp.int32), 16)
    return pltpu.bitcast(hi.astype(F32), jnp.int32) | lo_bits


def _unpack_bf16_pair(packed):
    lo = pltpu.bitcast(lax.shift_left(packed, 16), F32).astype(BF16)
    hi = pltpu.bitcast(packed & jnp.int32(-65536), F32).astype(BF16)
    return lo, hi


def _standardize(xf):
    mu = jnp.mean(xf, axis=-1, keepdims=True)
    var = jnp.mean(jnp.square(xf - mu), axis=-1, keepdims=True)
    return (xf - mu) * lax.rsqrt(var + LN_EPS)


TA = 512
A_COLS = 2560 + 2 * LANES


def _inproj_kernel(x_ref, w_ref, b_ref, lng_ref, wscat_ref, bsfull_ref,
                   ya_ref, qkv1_ref, qkv4_ref, qkv16_ref, cqk_ref, cv_ref, co_ref, dx_ref, ig_ref, fg_ref,
                   qkv_scr):
    xb = x_ref[...].astype(BF16)
    h = jnp.dot(xb, w_ref[...], preferred_element_type=F32) + b_ref[...]
    qkv1_ref[0, 0] = h[:, 512:1280].astype(BF16)
    for c in range(768 // LANES):
        qkv_scr[c] = h[:, 512 + c * LANES:512 + (c + 1) * LANES]
    for (_, dil), out_ref in zip(DIL_PATTERNS[1:], (qkv4_ref, qkv16_ref)):
        for r in range(dil):
            for c in range(768 // LANES):
                out_ref[0, r, :, c * LANES:(c + 1) * LANES] = (
                    qkv_scr[c, pl.ds(r, TA // dil, stride=dil), :].astype(BF16))
    cqk_ref[...] = h[:, 1280:1792]
    cv_ref[...] = h[:, 1792:2048].astype(BF16)
    co_ref[...] = h[:, 2048:2304]
    dx_ref[...] = h[:, 2304:2560]
    ig_ref[...] = h[:, 2560:2688]
    fg_ref[...] = h[:, 2688:2816]
    u = jax.nn.gelu(h[:, 0:256])
    v = jax.nn.gelu(h[:, 256:512])
    vn = _standardize(v) * lng_ref[...]
    lane_grp = lax.broadcasted_iota(jnp.int32, (GM_CHUNK, MIX_W), 1) // (MIX_W // GM_GROUPS)
    for c in range(TA // GM_CHUNK):
        vc = vn[c * GM_CHUNK:(c + 1) * GM_CHUNK]
        stacked = jnp.concatenate(
            [jnp.where(lane_grp == g, vc, 0.0).astype(BF16) for g in range(GM_GROUPS)], axis=0)
        mixed = jnp.dot(wscat_ref[...], stacked, preferred_element_type=F32) + bsfull_ref[...]
        ya_ref[c * GM_CHUNK:(c + 1) * GM_CHUNK, :] = (
            u[c * GM_CHUNK:(c + 1) * GM_CHUNK] * mixed).astype(BF16)


def _inproj_call(x2d, w_cat, b_cat, lng, wscat, bsfull, batch):
    n = x2d.shape[0]
    seq = n // batch
    tpb = seq // TA
    tok = lambda w: pl.BlockSpec((TA, w), lambda i: (i, 0))
    const = lambda s: pl.BlockSpec(s, lambda i: (0,) * len(s))
    regrouped = lambda dil: pl.BlockSpec((1, dil, TA // dil, 768), lambda i: (i // tpb, 0, i % tpb, 0))
    out_shape = (
        jax.ShapeDtypeStruct((n, 256), BF16),
    ) + tuple(jax.ShapeDtypeStruct((batch, dil, seq // dil, 768), BF16)
              for _, dil in DIL_PATTERNS) + (
        jax.ShapeDtypeStruct((n, 512), F32),
        jax.ShapeDtypeStruct((n, 256), BF16),
        jax.ShapeDtypeStruct((n, 256), F32),
        jax.ShapeDtypeStruct((n, 256), F32),
        jax.ShapeDtypeStruct((n, LANES), F32),
        jax.ShapeDtypeStruct((n, LANES), F32),
    )
    return pl.pallas_call(
        _inproj_kernel, name="inproj_gmlp",
        grid=(n // TA,),
        in_specs=[tok(D_MODEL), const((D_MODEL, A_COLS)), const((1, A_COLS)), const((1, MIX_W)),
                  const((GM_CHUNK, GM_GROUPS * GM_CHUNK)), const((GM_CHUNK, MIX_W))],
        out_specs=(tok(256),) + tuple(regrouped(dil) for _, dil in DIL_PATTERNS)
                  + (tok(512), tok(256), tok(256), tok(256), tok(LANES), tok(LANES)),
        out_shape=out_shape,
        scratch_shapes=[pltpu.VMEM((768 // LANES, TA, LANES), F32)],
        compiler_params=_cparams(1),
    )(x2d, w_cat, b_cat, lng, wscat, bsfull)


def _pack_inproj_weights(w_in, b_in):
    pad = lambda a: jnp.pad(a, ((0, 0), (0, LANES - 8)))
    w_cat = jnp.concatenate([w_in[:, 0:2304], w_in[:, 2320:2576],
                             pad(w_in[:, 2304:2312]), pad(w_in[:, 2312:2320])], axis=1)
    b2 = b_in[None, :]
    b_cat = jnp.concatenate([b2[:, 0:2304], b2[:, 2320:2576],
                             pad(b2[:, 2304:2312]), pad(b2[:, 2312:2320])], axis=1)
    return w_cat.astype(BF16), b_cat


def _pack_gmlp(gm_ws, gm_bs):
    wscat = jnp.transpose(gm_ws, (1, 0, 2)).reshape(GM_CHUNK, GM_GROUPS * GM_CHUNK).astype(BF16)
    bsfull = jnp.repeat(jnp.transpose(gm_bs), MIX_W // GM_GROUPS, axis=1)
    return wscat, bsfull


def _halo_specs(t, width, n_tiles):
    r = t // HALO
    main = pl.BlockSpec((1, t, width), lambda b, i: (b, i, 0))
    prev = pl.BlockSpec((1, HALO, width), lambda b, i: (b, jnp.maximum(i * r - 1, 0), 0))
    nxt = pl.BlockSpec((1, HALO, width), lambda b, i: (b, jnp.minimum((i + 1) * r, n_tiles * r - 1), 0))
    return main, prev, nxt


def _fill_halo_scratch(buf, x_ref, p_ref, n_ref, t):
    i = pl.program_id(1)
    last = pl.num_programs(1) - 1
    buf[0:HALO, :] = jnp.where(i > 0, p_ref[0], 0.0)
    buf[HALO:HALO + t, :] = x_ref[0]
    buf[HALO + t:2 * HALO + t, :] = jnp.where(i < last, n_ref[0], 0.0)


TP = 512


def _pool_kernel(x_ref, p_ref, n_ref, w_ref, sc_ref, o_ref, buf):
    _fill_halo_scratch(buf, x_ref, p_ref, n_ref, TP)
    seq = pl.num_programs(1) * TP
    pos = pl.program_id(1) * TP + lax.broadcasted_iota(jnp.int32, (TP, 1), 0)
    lane_grp = lax.broadcasted_iota(jnp.int32, (TP, MIX_W), 1) // (MIX_W // len(POOL_WINDOWS))
    x0 = buf[HALO:HALO + TP, :]
    pooled = jnp.zeros((TP, MIX_W), F32)
    acc = None
    half_done = 0
    for gi, win in enumerate(POOL_WINDOWS):
        half = win // 2
        for o in list(range(-half, -half_done)) + list(range(half_done, half)):
            term = buf[HALO + o:HALO + o + TP, :]
            acc = term if acc is None else acc + term
        half_done = half
        cnt = (jnp.minimum(pos + half, seq) - jnp.maximum(pos - half, 0)).astype(F32)
        pooled = jnp.where(lane_grp == gi, acc / cnt - x0, pooled)
    mixed = jnp.dot(pooled.astype(BF16), w_ref[...], preferred_element_type=F32)
    o_ref[0] = (mixed * sc_ref[...]).astype(BF16)


def _pool_call(dx, w_block, scale):
    b, s, _ = dx.shape
    nt = s // TP
    main, prev, nxt = _halo_specs(TP, MIX_W, nt)
    return pl.pallas_call(
        _pool_kernel, name="pool_mixer",
        grid=(b, nt),
        in_specs=[main, prev, nxt,
                  pl.BlockSpec((MIX_W, MIX_W), lambda b, i: (0, 0)),
                  pl.BlockSpec((1, MIX_W), lambda b, i: (0, 0))],
        out_specs=pl.BlockSpec((1, TP, MIX_W), lambda b, i: (b, i, 0)),
        out_shape=jax.ShapeDtypeStruct((b, s, MIX_W), BF16),
        scratch_shapes=[pltpu.VMEM((TP + 2 * HALO, MIX_W), F32)],
        compiler_params=_cparams(2),
    )(dx, dx, dx, w_block, scale)


TQ = 128
TQS = 512
TKEYS = TQ + 2 * ATT_BLOCK


def _attn_kernel(q_ref, kp_ref, km_ref, kn_ref, vp_ref, vm_ref, vn_ref, bias_ref, o_ref, lse_ref):
    i = pl.program_id(2)
    seq = pl.num_programs(2) * TQS
    q = q_ref[0, 0] * ATT_HD ** -0.5
    k = jnp.concatenate([kp_ref[0, 0], km_ref[0, 0], kn_ref[0, 0]], axis=0)
    v = jnp.concatenate([vp_ref[0, 0], vm_ref[0, 0], vn_ref[0, 0]], axis=0)
    lane = lax.broadcasted_iota(jnp.int32, (TQ, LANES), 1)
    lane_half = lax.broadcasted_iota(jnp.int32, (1, LANES), 1) // ATT_HD
    keep = [jnp.where(lane_half == hh, 1.0, 0.0).astype(BF16) for hh in range(2)]
    for j in range(TQS // TQ):
        kpos = i * TQS + j * TQ - ATT_BLOCK + lax.broadcasted_iota(jnp.int32, (1, TKEYS), 1)
        kvalid = (kpos >= 0) & (kpos < seq)
        qrows = slice(j * TQ, (j + 1) * TQ)
        krows = slice(j * TQ, j * TQ + TKEYS)
        lse_tile = jnp.zeros((TQ, LANES), F32)
        for pair in range(ATT_HEADS // 2):
            grp = slice(pair * LANES, (pair + 1) * LANES)
            q_pair, k_pair, v_pair = q[qrows, grp], k[krows, grp], v[krows, grp]
            o_pair = jnp.zeros((TQ, LANES), F32)
            for hh in range(2):
                h = 2 * pair + hh
                logits = lax.dot_general(q_pair * keep[hh], k_pair, (((1,), (1,)), ((), ())),
                                         preferred_element_type=F32) + bias_ref[h]
                logits = jnp.where(kvalid, logits, NEG_BIG)
                m = jnp.max(logits, axis=-1, keepdims=True)
                p = jnp.exp(logits - m)
                ssum = jnp.sum(p, axis=-1, keepdims=True)
                o = jnp.dot(p.astype(BF16), v_pair, preferred_element_type=F32) / ssum
                o_pair = jnp.where(lane_half == hh, o, o_pair)
                lse_tile = jnp.where(lane == h, m + jnp.log(ssum), lse_tile)
            o_ref[0, 0, qrows, grp] = o_pair
        lse_ref[0, 0, qrows, :] = lse_tile


def _attn_call(qkv, bias):
    b, dil, l, _ = qkv.shape
    nt = l // TQS
    r64 = TQS // ATT_BLOCK
    main = lambda c: pl.BlockSpec((1, 1, TQS, MIX_W), lambda b, r, i: (b, r, i, c))
    prev = lambda c: pl.BlockSpec((1, 1, ATT_BLOCK, MIX_W),
                                  lambda b, r, i: (b, r, jnp.maximum(i * r64 - 1, 0), c))
    nxt = lambda c: pl.BlockSpec((1, 1, ATT_BLOCK, MIX_W),
                                 lambda b, r, i: (b, r, jnp.minimum((i + 1) * r64, nt * r64 - 1), c))
    return pl.pallas_call(
        _attn_kernel, name="band_attention",
        grid=(b, dil, nt),
        in_specs=[main(0), prev(1), main(1), nxt(1), prev(2), main(2), nxt(2),
                  pl.BlockSpec((ATT_HEADS, TQ, TKEYS), lambda b, r, i: (0, 0, 0))],
        out_specs=(pl.BlockSpec((1, 1, TQS, MIX_W), lambda b, r, i: (b, r, i, 0)),
                   pl.BlockSpec((1, 1, TQS, LANES), lambda b, r, i: (b, r, i, 0))),
        out_shape=(jax.ShapeDtypeStruct((b, dil, l, MIX_W), F32),
                   jax.ShapeDtypeStruct((b, dil, l, LANES), F32)),
        compiler_params=_cparams(3),
    )(qkv, qkv, qkv, qkv, qkv, qkv, qkv, bias)


def _t5_bucket_static(rel):
    half = REL_BUCKETS // 2
    max_exact = half // 2
    ret = np.where(rel > 0, half, 0)
    n = np.abs(rel)
    nf = np.maximum(n, 1).astype(np.float32)
    large = max_exact + (np.log(nf / np.float32(max_exact)) / np.float32(math.log(REL_MAX_DIST / max_exact))
                         * np.float32(half - max_exact)).astype(np.int32)
    large = np.minimum(large, half - 1)
    return ret + np.where(n < max_exact, n, large)


def _attn_bias_tile(rel_bias, window, dil):
    side = (window // 2) // dil
    rel = np.arange(TKEYS)[None, :] - ATT_BLOCK - np.arange(TQ)[:, None]
    onehot = jax.nn.one_hot(jnp.asarray(_t5_bucket_static(dil * rel), jnp.int32), REL_BUCKETS, dtype=F32)
    bias = jnp.einsum('qkr,rh->hqk', onehot, rel_bias, precision=lax.Precision.HIGHEST)
    return jnp.where(jnp.asarray(np.abs(rel) <= side)[None], bias, NEG_BIG)


TM = 512
VT_ROWS = ML_HD + 16


def _mlprep_kernel(x_ref, p_ref, n_ref, v_ref, w_ref, qt_out, k_out, vt_out, buf):
    _fill_halo_scratch(buf, x_ref, p_ref, n_ref, TM)
    conv = (buf[HALO - 1:HALO - 1 + TM, :] * w_ref[0:1, :] + buf[HALO:HALO + TM, :] * w_ref[1:2, :]
            + buf[HALO + 1:HALO + 1 + TM, :] * w_ref[2:3, :])
    qk = jax.nn.silu(conv)
    qt = jnp.transpose(qk[:, :MIX_W])
    vt = jnp.transpose(v_ref[0].astype(F32))
    ones_rows = jnp.where(lax.broadcasted_iota(jnp.int32, (VT_ROWS - ML_HD, ML_CHUNK), 0) == 0, 1.0, 0.0)
    for h in range(ML_HEADS):
        sl = slice(h * ML_HD, (h + 1) * ML_HD)
        k_out[0, h] = (qk[:, MIX_W + h * ML_HD:MIX_W + (h + 1) * ML_HD] * ML_HD ** -0.5).astype(BF16)
        for c in range(TM // ML_CHUNK):
            cl = slice(c * ML_CHUNK, (c + 1) * ML_CHUNK)
            qt_out[0, h, c] = qt[sl, cl].astype(BF16)
            vt_out[0, h, c] = jnp.concatenate([vt[sl, cl], ones_rows], axis=0).astype(BF16)


def _mlprep_call(cqk, cv, conv_w):
    b, s, _ = cqk.shape
    nt = s // TM
    nc = s // ML_CHUNK
    cpt = TM // ML_CHUNK
    main, prev, nxt = _halo_specs(TM, 2 * MIX_W, nt)
    return pl.pallas_call(
        _mlprep_kernel, name="mlstm_prep",
        grid=(b, nt),
        in_specs=[main, prev, nxt,
                  pl.BlockSpec((1, TM, MIX_W), lambda b, i: (b, i, 0)),
                  pl.BlockSpec((3, 2 * MIX_W), lambda b, i: (0, 0))],
        out_specs=(pl.BlockSpec((1, ML_HEADS, cpt, ML_HD, ML_CHUNK), lambda b, i: (b, 0, i, 0, 0)),
                   pl.BlockSpec((1, ML_HEADS, TM, ML_HD), lambda b, i: (b, 0, i, 0)),
                   pl.BlockSpec((1, ML_HEADS, cpt, VT_ROWS, ML_CHUNK), lambda b, i: (b, 0, i, 0, 0))),
        out_shape=(jax.ShapeDtypeStruct((b, ML_HEADS, nc, ML_HD, ML_CHUNK), BF16),
                   jax.ShapeDtypeStruct((b, ML_HEADS, s, ML_HD), BF16),
                   jax.ShapeDtypeStruct((b, ML_HEADS, nc, VT_ROWS, ML_CHUNK), BF16)),
        scratch_shapes=[pltpu.VMEM((TM + 2 * HALO, 2 * MIX_W), F32)],
        compiler_params=_cparams(2),
    )(cqk, cqk, cqk, cv, conv_w)


GP_BLK = 512


def _chunk_scan(x, fwd_lane, t_in_chunk, op, ident):
    n = x.shape[0]
    shift = 1
    while shift < ML_CHUNK:
        down = pltpu.roll(x, shift, 0)
        up = pltpu.roll(x, n - shift, 0)
        nb = jnp.where(fwd_lane,
                       jnp.where(t_in_chunk >= shift, down, ident),
                       jnp.where(t_in_chunk < ML_CHUNK - shift, up, ident))
        x = op(x, nb)
        shift *= 2
    return x


def _gatescan_kernel(ig_ref, fg_ref, fb_ref, b_ref, a_ref, cm_ref, g_ref, amax_ref):
    lane = lax.broadcasted_iota(jnp.int32, (1, LANES), 1)
    fwd_lane = lane < ML_HEADS
    t_in_chunk = lax.broadcasted_iota(jnp.int32, (GP_BLK, 1), 0) % ML_CHUNK
    z = fg_ref[0] + fb_ref[...]
    lf = jnp.minimum(z, 0.0) - jnp.log1p(jnp.exp(-jnp.abs(z)))
    b = _chunk_scan(lf, fwd_lane, t_in_chunk, jnp.add, 0.0)
    a = ig_ref[0] - b
    b_ref[0] = b
    a_ref[0] = a
    cm_ref[0] = _chunk_scan(a, fwd_lane, t_in_chunk, jnp.maximum, -jnp.inf)
    cpb = GP_BLK // ML_CHUNK
    last = pl.ds(ML_CHUNK - 1, cpb, stride=ML_CHUNK)
    first = pl.ds(0, cpb, stride=ML_CHUNK)
    g_ref[0] = jnp.where(fwd_lane, b_ref[0, last, :], b_ref[0, first, :])
    amax_ref[0] = jnp.where(fwd_lane, cm_ref[0, last, :], cm_ref[0, first, :])


def _gateout_kernel(b_ref, a_ref, cm_ref, g_ref, amax_ref, ws_ref, m_ref, iw_ref, en_ref, dec_ref,
                    mch_s, mlast_s):
    nc = g_ref.shape[1]
    j = pl.program_id(1)
    fwd_lane = lax.broadcasted_iota(jnp.int32, (1, LANES), 1) < ML_HEADS

    @pl.when(j == 0)
    def _():
        def m_step(i, carry):
            mf, mb = carry
            cf = pl.ds(i, 1)
            cb = pl.ds(nc - 1 - i, 1)
            mch_s[cf, :] = jnp.where(fwd_lane, mf, mch_s[cf, :])
            mch_s[cb, :] = jnp.where(fwd_lane, mch_s[cb, :], mb)
            mf = g_ref[0, cf, :] + jnp.maximum(mf, amax_ref[0, cf, :])
            mb = g_ref[0, cb, :] + jnp.maximum(mb, amax_ref[0, cb, :])
            return mf, mb

        mch_s[...] = jnp.zeros((nc, LANES), F32)
        zero = jnp.zeros((1, LANES), F32)
        lax.fori_loop(0, nc, m_step, (zero, zero))
        mlast = jnp.maximum(amax_ref[0], mch_s[...])
        mlast_s[...] = mlast
        dec_ref[0] = jnp.exp(mch_s[...] - mlast)

    cpb = GP_BLK // ML_CHUNK
    crow = pl.ds(pl.multiple_of(j * cpb, cpb), cpb)
    expand = lambda t: jnp.broadcast_to(t[:, None, :], (cpb, ML_CHUNK, LANES)).reshape(GP_BLK, LANES)
    m_tok = expand(mch_s[crow, :])
    mlast_tok = expand(mlast_s[crow, :])
    mt = jnp.maximum(cm_ref[0], m_tok)
    m_ref[0] = mt
    iw_ref[0] = jnp.exp(m_tok - mt)
    en_ref[0] = jnp.exp(-(b_ref[0] + mt))
    ws_ref[0] = jnp.exp(a_ref[0] - mlast_tok)


def _gateprep_call(ig, fg, fbias_row):
    b, s, _ = ig.shape
    nc = s // ML_CHUNK
    cpb = GP_BLK // ML_CHUNK
    tok = pl.BlockSpec((1, GP_BLK, LANES), lambda i, j: (i, j, 0))
    tok_shape = jax.ShapeDtypeStruct((b, s, LANES), F32)
    chunk_shape = jax.ShapeDtypeStruct((b, nc, LANES), F32)
    chunk_tile = pl.BlockSpec((1, cpb, LANES), lambda i, j: (i, j, 0))
    chunk_all = pl.BlockSpec((1, nc, LANES), lambda i, j: (i, 0, 0))
    bcum, a, cm, g, amax = pl.pallas_call(
        _gatescan_kernel, name="mlstm_gate_scan",
        grid=(b, s // GP_BLK),
        in_specs=[tok, tok, pl.BlockSpec((1, LANES), lambda i, j: (0, 0))],
        out_specs=(tok, tok, tok, chunk_tile, chunk_tile),
        out_shape=(tok_shape,) * 3 + (chunk_shape,) * 2,
        compiler_params=_cparams(2),
    )(ig, fg, fbias_row)
    ws, m_col, iw_col, en_col, decay = pl.pallas_call(
        _gateout_kernel, name="mlstm_gate_out",
        grid=(b, s // GP_BLK),
        in_specs=[tok, tok, tok, chunk_all, chunk_all],
        out_specs=(tok, tok, tok, tok, chunk_all),
        out_shape=(tok_shape,) * 4 + (chunk_shape,),
        scratch_shapes=[pltpu.VMEM((nc, LANES), F32)] * 2,
        compiler_params=_cparams(2),
    )(bcum, a, cm, g, amax)
    return a, ws, m_col, iw_col, en_col, decay


TE = 512


def _mlstm_kernel(*refs):
    fwd, bwd, (hf_ref, hb_ref, state) = refs[:9], refs[9:18], refs[18:]
    i = pl.program_id(1)

    @pl.when(i == 0)
    def _():
        state[...] = jnp.zeros(state.shape, F32)

    cpt = TE // ML_CHUNK
    s_idx = lax.broadcasted_iota(jnp.int32, (ML_CHUNK, ML_CHUNK), 0)
    t_idx = lax.broadcasted_iota(jnp.int32, (ML_CHUNK, ML_CHUNK), 1)

    def chunk_body(c, carry):
        jobs = []
        for d, (qt_r, k_r, vt_r, a_r, m_r, iw_r, en_r, ws_r, dec_r), out_r in ((0, fwd, hf_ref), (1, bwd, hb_ref)):
            cc = c if d == 0 else cpt - 1 - c
            rows = pl.ds(pl.multiple_of(cc * ML_CHUNK, ML_CHUNK), ML_CHUNK)
            crow = pl.ds(cc, 1)
            for h in range(ML_HEADS):
                ch = d * ML_HEADS + h
                row = lambda r: r[0, ch, crow, :]
                jobs.append(dict(
                    ch=ch, tri=(s_idx >= t_idx) if d else (s_idx <= t_idx),
                    k=k_r[0, h, rows, :], qt=qt_r[0, h, cc], vt=vt_r[0, h, cc], a=a_r[0, rows, ch:ch + 1],
                    m=row(m_r), iw=row(iw_r), en=row(en_r), ws=row(ws_r), dec=row(dec_r),
                    out=(out_r, cc, h)))
        for j in jobs:
            j["cst"] = state[j["ch"]]
            j["st"] = jnp.dot(j["k"], j["qt"], preferred_element_type=F32)
            j["inter"] = jnp.dot(j["cst"].astype(BF16), j["qt"], preferred_element_type=F32)
            j["upd"] = jnp.dot((j["vt"].astype(F32) * j["ws"]).astype(BF16), j["k"],
                               preferred_element_type=F32)
        for j in jobs:
            j["swt"] = j["st"] * jnp.exp(jnp.where(j["tri"], j["a"] - j["m"], NEG_BIG))
            j["intra"] = jnp.dot(j["vt"], j["swt"].astype(BF16), preferred_element_type=F32)
        for j in jobs:
            den = jnp.sum(j["swt"], axis=0, keepdims=True) + j["iw"] * j["inter"][ML_HD:ML_HD + 1]
            tot = j["intra"][:ML_HD] + j["iw"] * j["inter"][:ML_HD]
            out_r, cc, h = j["out"]
            out_r[0, cc, h * ML_HD:(h + 1) * ML_HD, :] = tot / jnp.maximum(jnp.abs(den), j["en"])
            state[j["ch"]] = j["dec"] * j["cst"] + j["upd"]
        return carry

    lax.fori_loop(0, cpt, chunk_body, 0)


def _mlstm_call(qt, k, vt, a_col, m_row, iw_row, en_row, ws_row, dec_row):
    b, _, s, _ = k.shape
    nt = s // TE
    nc = s // ML_CHUNK
    cpt = TE // ML_CHUNK

    def specs(rev):
        ti = (lambda i: nt - 1 - i) if rev else (lambda i: i)
        row = pl.BlockSpec((1, 2 * ML_HEADS, cpt, ML_CHUNK), lambda b, i: (b, 0, ti(i), 0))
        return [
            pl.BlockSpec((1, ML_HEADS, cpt, ML_HD, ML_CHUNK), lambda b, i: (b, 0, ti(i), 0, 0)),
            pl.BlockSpec((1, ML_HEADS, TE, ML_HD), lambda b, i: (b, 0, ti(i), 0)),
            pl.BlockSpec((1, ML_HEADS, cpt, VT_ROWS, ML_CHUNK), lambda b, i: (b, 0, ti(i), 0, 0)),
            pl.BlockSpec((1, TE, LANES), lambda b, i: (b, ti(i), 0)),
            row, row, row, row, row]

    args = [qt, k, vt, a_col, m_row, iw_row, en_row, ws_row, dec_row]
    out_f = pl.BlockSpec((1, cpt, MIX_W, ML_CHUNK), lambda b, i: (b, i, 0, 0))
    out_b = pl.BlockSpec((1, cpt, MIX_W, ML_CHUNK), lambda b, i: (b, nt - 1 - i, 0, 0))
    return pl.pallas_call(
        _mlstm_kernel, name="mlstm_scan",
        grid=(b, nt),
        in_specs=specs(False) + specs(True),
        out_specs=(out_f, out_b),
        out_shape=(jax.ShapeDtypeStruct((b, nc, MIX_W, ML_CHUNK), F32),) * 2,
        scratch_shapes=[pltpu.VMEM((2 * ML_HEADS, VT_ROWS, ML_HD), F32)],
        compiler_params=_cparams(2),
    )(*args, *args)


def _mlstm_branch(cqk, cv, ig, fg, conv_w, fbias):
    b, s, _ = cqk.shape
    nc = s // ML_CHUNK
    qt, k, vt = _mlprep_call(cqk, cv, conv_w)
    fb_row = jnp.pad(fbias.reshape(1, 2 * ML_HEADS), ((0, 0), (0, LANES - 2 * ML_HEADS)))
    a, ws, m_col, iw_col, en_col, decay = _gateprep_call(ig, fg, fb_row)
    rows = lambda t: jnp.transpose(t[..., :2 * ML_HEADS], (0, 2, 1)).reshape(b, 2 * ML_HEADS, nc, ML_CHUNK)
    dec_row = jnp.broadcast_to(jnp.transpose(decay[..., :2 * ML_HEADS], (0, 2, 1))[..., None],
                               (b, 2 * ML_HEADS, nc, ML_CHUNK))
    return _mlstm_call(qt, k, vt, a, rows(m_col), rows(iw_col), rows(en_col), rows(ws), dec_row)


TF = 512


def _merge_kernel(x_ref, ya_ref, o1_ref, o2_ref, o3_ref, l1_ref, l2_ref, l3_ref, hf_ref, hb_ref,
                  co_ref, yd_ref, wg_ref, bg_ref, wbr_ref, wout_ref, mng_ref, lng_ref, lnb_ref,
                  wr_ref, x1_ref, x1p_ref, aff_ref, o_scr, l_scr, *, alpha):
    x = x_ref[0]
    xb = x.astype(BF16)

    def natural_order(src_ref, scr):
        dil, width = src_ref.shape[1], src_ref.shape[3]
        if dil == 1:
            return src_ref[0, 0]
        for r in range(dil):
            for c in range(width // LANES):
                scr[c, pl.ds(r, TF // dil, stride=dil), :] = src_ref[0, r, :, c * LANES:(c + 1) * LANES]
        return jnp.concatenate([scr[c] for c in range(width // LANES)], axis=1)

    lane_head = lax.broadcasted_iota(jnp.int32, (TF, MIX_W), 1) // ML_HD
    l1, l2, l3 = [natural_order(r, l_scr.at[p]) for p, r in enumerate((l1_ref, l2_ref, l3_ref))]
    o1, o2, o3 = [natural_order(r, o_scr.at[p]) for p, r in enumerate((o1_ref, o2_ref, o3_ref))]
    lm = jnp.maximum(jnp.maximum(l1, l2), l3)
    e1, e2, e3 = jnp.exp(l1 - lm), jnp.exp(l2 - lm), jnp.exp(l3 - lm)
    inv = 1.0 / (e1 + e2 + e3)

    def per_head(w):
        out = jnp.zeros((TF, MIX_W), F32)
        for h in range(ATT_HEADS):
            out = jnp.where(lane_head == h, w[:, h:h + 1], out)
        return out

    y_b = per_head(e1 * inv) * o1 + per_head(e2 * inv) * o2 + per_head(e3 * inv) * o3
    hsum_t = jnp.concatenate([hf_ref[0, c] + hb_ref[0, c] for c in range(TF // ML_CHUNK)], axis=1)
    per_head_rows = hsum_t.reshape(ML_HEADS, ML_HD, TF)
    mu = jnp.mean(per_head_rows, axis=1, keepdims=True)
    cen = per_head_rows - mu
    var = jnp.mean(cen * cen, axis=1, keepdims=True)
    hn_t = (cen * lax.rsqrt(var + LN_EPS)).reshape(MIX_W, TF)
    y_c_t = (jax.nn.sigmoid(jnp.transpose(co_ref[0])) * (hn_t * mng_ref[...])).astype(BF16)
    ys = (ya_ref[0], y_b.astype(BF16), None, yd_ref[0])
    merged = jnp.zeros((TF, D_MODEL), F32)
    for n in range(N_BRANCH):
        cols = slice(n * D_MODEL, (n + 1) * D_MODEL)
        gate = jax.nn.sigmoid(jnp.dot(xb, wg_ref[:, cols], preferred_element_type=F32) + bg_ref[:, cols])
        if ys[n] is None:
            proj = lax.dot_general(y_c_t, wbr_ref[n], (((0,), (0,)), ((), ())), preferred_element_type=F32)
        else:
            proj = jnp.dot(ys[n], wbr_ref[n], preferred_element_type=F32)
        merged = merged + gate * proj
    mix = jnp.dot(merged.astype(BF16), wout_ref[...], preferred_element_type=F32)
    x1 = _standardize(alpha * x + mix) * lng_ref[...] + lnb_ref[...]
    x1_ref[0] = x1
    x1b = x1.astype(BF16)
    x1p_ref[0] = _pack_bf16_pair(x1b[:, :D_MODEL // 2], x1b[:, D_MODEL // 2:])
    logits = lax.dot_general(wr_ref[...], x1b, (((1,), (1,)), ((), ())),
                             preferred_element_type=F32)
    ex = jnp.exp(logits - jnp.max(logits, axis=0, keepdims=True))
    aff_ref[0] = ex / jnp.sum(ex, axis=0, keepdims=True)


def _merge_call(x, ya, o_list, l_list, hf, hb, co, yd, wg, bg, wbr, wout, mng, lng, lnb, wr_t, alpha):
    b, s, _ = x.shape
    tok = lambda w: pl.BlockSpec((1, TF, w), lambda b, i: (b, i, 0))
    grouped = lambda dil, w: pl.BlockSpec((1, dil, TF // dil, w), lambda b, i: (b, 0, i, 0))
    chunked = pl.BlockSpec((1, TF // ML_CHUNK, MIX_W, ML_CHUNK), lambda b, i: (b, i, 0, 0))
    const = lambda shp: pl.BlockSpec(shp, lambda b, i: (0,) * len(shp))
    return pl.pallas_call(
        functools.partial(_merge_kernel, alpha=alpha), name="merge_ln_router",
        grid=(b, s // TF),
        in_specs=[tok(D_MODEL), tok(MIX_W)] + [grouped(dil, MIX_W) for _, dil in DIL_PATTERNS]
                 + [grouped(dil, LANES) for _, dil in DIL_PATTERNS]
                 + [chunked, chunked, tok(MIX_W), tok(MIX_W)]
                 + [const((D_MODEL, N_BRANCH * D_MODEL)), const((1, N_BRANCH * D_MODEL)),
                    const((N_BRANCH, MIX_W, D_MODEL)), const((D_MODEL, D_MODEL)), const((MIX_W, 1)),
                    const((1, D_MODEL)), const((1, D_MODEL)), const((N_EXPERTS, D_MODEL))],
        out_specs=(tok(D_MODEL), tok(D_MODEL // 2), pl.BlockSpec((1, N_EXPERTS, TF), lambda b, i: (b, 0, i))),
        out_shape=(jax.ShapeDtypeStruct((b, s, D_MODEL), F32),
                   jax.ShapeDtypeStruct((b, s, D_MODEL // 2), jnp.int32),
                   jax.ShapeDtypeStruct((b, N_EXPERTS, s), F32)),
        scratch_shapes=[pltpu.VMEM((len(DIL_PATTERNS), MIX_W // LANES, TF, LANES), F32),
                        pltpu.VMEM((len(DIL_PATTERNS), 1, TF, LANES), F32)],
        compiler_params=_cparams(2),
    )(x, ya, *o_list, *l_list, hf, hb, co, yd, wg, bg, wbr, wout, mng, lng, lnb, wr_t)


TT = 256


def _select_kernel(aff_ref, slot_ref, *, cap):
    s = aff_ref.shape[2]
    bits = pltpu.bitcast(aff_ref[0], jnp.int32)

    def bit_step(i, thr):
        cand = thr | jnp.left_shift(jnp.int32(1), 30 - i)
        cnt = jnp.sum((bits >= cand).astype(jnp.int32), axis=1, keepdims=True)
        return jnp.where(cnt >= cap, cand, thr)

    thr = lax.fori_loop(0, 31, bit_step, jnp.zeros((N_EXPERTS, 1), jnp.int32))
    gt = bits > thr
    eq = bits == thr
    need = (cap - jnp.sum(gt.astype(jnp.int32), axis=1, keepdims=True)).astype(F32)
    upper = (lax.broadcasted_iota(jnp.int32, (TT, TT), 0)
             <= lax.broadcasted_iota(jnp.int32, (TT, TT), 1)).astype(BF16)
    eq_before = jnp.zeros((N_EXPERTS, 1), F32)
    sel_before = jnp.zeros((N_EXPERTS, 1), F32)
    for j in range(s // TT):
        cols = slice(j * TT, (j + 1) * TT)
        eq_j = eq[:, cols]
        eq_incl = eq_before + jnp.dot(eq_j.astype(BF16), upper, preferred_element_type=F32)
        sel_j = gt[:, cols] | (eq_j & (eq_incl <= need))
        sel_f = sel_j.astype(F32)
        sel_incl = sel_before + jnp.dot(sel_f.astype(BF16), upper, preferred_element_type=F32)
        slot_ref[0, :, cols] = jnp.where(sel_j, sel_incl - 1.0, -1.0).astype(jnp.int32)
        eq_before = eq_incl[:, TT - 1:TT]
        sel_before = sel_incl[:, TT - 1:TT]


def _select_call(aff_t, cap):
    b, e, s = aff_t.shape
    return pl.pallas_call(
        functools.partial(_select_kernel, cap=cap), name="expert_choice_select",
        grid=(b,),
        in_specs=[pl.BlockSpec((1, e, s), lambda i: (i, 0, 0))],
        out_specs=pl.BlockSpec((1, e, s), lambda i: (i, 0, 0)),
        out_shape=jax.ShapeDtypeStruct((b, e, s), jnp.int32),
        compiler_params=_cparams(1),
    )(aff_t)


SC_LANES = 16
SC_ROWS = 64
SC_IDX = 128
SC_SLAB = 128
SC_ZROWS = 64
CF = 1024


def _sc_dispatch_call(x_flat, slot2, aff2, seq, cap):
    n_pairs = slot2.shape[0]
    d = x_flat.shape[1]
    info = plsc.get_sparse_core_info()
    n_workers = info.num_cores * info.num_subcores
    assert n_pairs % n_workers == 0 and seq % SC_LANES == 0 and cap % (2 * SC_ROWS) == 0
    pairs_per_worker = n_pairs // n_workers
    mesh = plsc.VectorSubcoreMesh(core_axis_name="c", subcore_axis_name="s")

    @functools.partial(
        pl.kernel, mesh=mesh, name="expert_dispatch_sc",
        compiler_params=pltpu.CompilerParams(needs_layout_passes=False),
        out_type=(jax.ShapeDtypeStruct((n_pairs * cap, d), x_flat.dtype),
                  jax.ShapeDtypeStruct((n_pairs, cap), jnp.int32),
                  jax.ShapeDtypeStruct((n_pairs, cap), F32)),
        scratch_types=[pltpu.VMEM((seq,), jnp.int32), pltpu.VMEM((seq,), F32),
                       pltpu.VMEM((cap,), jnp.int32), pltpu.VMEM((cap,), jnp.int32),
                       pltpu.VMEM((cap,), F32),
                       pltpu.VMEM((2, SC_ROWS, d), x_flat.dtype), pltpu.SemaphoreType.DMA((2,))])
    def dispatch(x_hbm, slot_hbm, aff_hbm, xs_hbm, tok_hbm, gate_hbm,
                 slot_v, aff_v, idx_v, tok_v, gate_v, rows_v, sem):
        worker = lax.axis_index("s") * info.num_cores + lax.axis_index("c")
        lane = lax.iota(jnp.int32, SC_LANES)

        def gather(c0, buf):
            return pltpu.make_async_copy(x_hbm.at[idx_v.at[pl.ds(c0, SC_ROWS)]], rows_v.at[buf], sem.at[buf])
        for k in range(pairs_per_worker):
            pair = worker * pairs_per_worker + k
            row0 = (pair // N_EXPERTS) * seq
            pltpu.sync_copy(slot_hbm.at[pair], slot_v)
            pltpu.sync_copy(aff_hbm.at[pair], aff_v)

            @pl.loop(0, seq, step=SC_LANES)
            def _(t0):
                sv = slot_v[pl.ds(t0, SC_LANES)]
                picked = sv >= 0
                plsc.store_scatter(tok_v, [sv], t0 + lane, mask=picked)
                plsc.store_scatter(idx_v, [sv], row0 + t0 + lane, mask=picked)
                plsc.store_scatter(gate_v, [sv], aff_v[pl.ds(t0, SC_LANES)], mask=picked)

            pltpu.sync_copy(tok_v, tok_hbm.at[pair])
            pltpu.sync_copy(gate_v, gate_hbm.at[pair])

            gather(0, 0).start()

            @pl.loop(0, cap, step=2 * SC_ROWS)
            def _(c0):
                gather(c0 + SC_ROWS, 1).start()
                gather(c0, 0).wait()
                pltpu.sync_copy(rows_v.at[0], xs_hbm.at[pl.ds(pair * cap + c0, SC_ROWS)])

                @pl.when(c0 + 2 * SC_ROWS < cap)
                def _():
                    gather(c0 + 2 * SC_ROWS, 0).start()

                gather(c0 + SC_ROWS, 1).wait()
                pltpu.sync_copy(rows_v.at[1], xs_hbm.at[pl.ds(pair * cap + c0 + SC_ROWS, SC_ROWS)])

    return dispatch(x_flat, slot2, aff2)


def _expert_kernel(xs_ref, g_ref, w1_ref, w3_ref, w2_ref, ye_ref, w1_bf, w3_bf, w2_bf):
    @pl.when((pl.program_id(1) == 0) & (pl.program_id(2) == 0))
    def _():
        w1_bf[...] = w1_ref[0, 0].astype(BF16)
        w3_bf[...] = w3_ref[0, 0].astype(BF16)
        w2_bf[...] = w2_ref[0, 0].astype(BF16)

    xs = jnp.concatenate(_unpack_bf16_pair(xs_ref[0, 0]), axis=1)
    hid = (jax.nn.silu(jnp.dot(xs, w1_bf[...], preferred_element_type=F32))
           * jnp.dot(xs, w3_bf[...], preferred_element_type=F32))
    ye_ref[0, 0] = jnp.dot(hid.astype(BF16), w2_bf[...], preferred_element_type=F32) * g_ref[0, 0]


def _expert_call(xs4, gate4, w1, w3, w2, layer):
    b, e, cap, half = xs4.shape
    d, ff = w1.shape[2], w1.shape[3]
    assert d == 2 * half
    rows = lambda w: pl.BlockSpec((1, 1, CF, w), lambda e, b, j: (b, e, j, 0))
    wspec = lambda r, c: pl.BlockSpec((1, 1, r, c), lambda e, b, j: (layer, e, 0, 0))
    return pl.pallas_call(
        _expert_kernel, name="expert_ffn",
        grid=(e, b, cap // CF),
        in_specs=[rows(half), rows(1), wspec(d, ff), wspec(d, ff), wspec(ff, d)],
        out_specs=rows(d),
        out_shape=jax.ShapeDtypeStruct((b, e, cap, d), F32),
        scratch_shapes=[pltpu.VMEM((d, ff), BF16), pltpu.VMEM((d, ff), BF16), pltpu.VMEM((ff, d), BF16)],
        compiler_params=_cparams(3),
    )(xs4, gate4, w1, w3, w2)


def _sc_combine_call(ye_flat, tok3, seq):
    n_pairs, n_chunks, _ = tok3.shape
    cap = n_chunks * SC_IDX
    d = ye_flat.shape[1]
    nb = n_pairs // N_EXPERTS
    info = plsc.get_sparse_core_info()
    assert info.num_subcores == N_EXPERTS and nb % info.num_cores == 0 and n_chunks % 2 == 0
    assert seq % (info.num_subcores * SC_ZROWS) == 0 and d % SC_SLAB == 0
    batches_per_core = nb // info.num_cores
    own_rows = seq // info.num_subcores
    mesh = plsc.VectorSubcoreMesh(core_axis_name="c", subcore_axis_name="s")

    @functools.partial(
        pl.kernel, mesh=mesh, name="expert_combine_sc",
        compiler_params=pltpu.CompilerParams(needs_layout_passes=False),
        out_type=jax.ShapeDtypeStruct((nb * seq, d), F32),
        scratch_types=[pltpu.VMEM_SHARED((seq, SC_SLAB), F32),
                       pltpu.VMEM((n_chunks, SC_IDX), jnp.int32),
                       pltpu.VMEM((2, SC_IDX, SC_SLAB), F32),
                       pltpu.VMEM((SC_ZROWS, SC_SLAB), F32),
                       pltpu.SemaphoreType.DMA((2,))])
    def combine(ye_hbm, tok_hbm, out_hbm, acc_sh, tok_v, rows_v, zero_v, sem):
        core = lax.axis_index("c")
        sub = lax.axis_index("s")

        @pl.loop(0, SC_ZROWS)
        def _(r):
            for l0 in range(0, SC_SLAB, SC_LANES):
                zero_v[r, pl.ds(l0, SC_LANES)] = jnp.zeros((SC_LANES,), F32)

        for bb in range(batches_per_core):
            batch = core * batches_per_core + bb
            pair = batch * N_EXPERTS + sub
            pltpu.sync_copy(tok_hbm.at[pair], tok_v)

            @pl.loop(0, d // SC_SLAB)
            def _(slab):
                cols = pl.ds(pl.multiple_of(slab * SC_SLAB, SC_SLAB), SC_SLAB)

                @pl.loop(0, own_rows, step=SC_ZROWS)
                def _(r0):
                    pltpu.sync_copy(zero_v, acc_sh.at[pl.ds(sub * own_rows + r0, SC_ZROWS)])

                def load(j, buf):
                    return pltpu.make_async_copy(
                        ye_hbm.at[pl.ds(pair * cap + j * SC_IDX, SC_IDX), cols], rows_v.at[buf], sem.at[buf])

                load(0, 0).start()
                plsc.subcore_barrier()

                for j in range(0, n_chunks, 2):
                    load(j + 1, 1).start()
                    load(j, 0).wait()
                    pltpu.sync_copy(rows_v.at[0], acc_sh.at[tok_v.at[j]], add=True)
                    if j + 2 < n_chunks:
                        load(j + 2, 0).start()
                    load(j + 1, 1).wait()
                    pltpu.sync_copy(rows_v.at[1], acc_sh.at[tok_v.at[j + 1]], add=True)

                plsc.subcore_barrier()
                pltpu.sync_copy(acc_sh.at[pl.ds(sub * own_rows, own_rows)],
                                out_hbm.at[pl.ds(batch * seq + sub * own_rows, own_rows), cols])

    return combine(ye_flat, tok3)


TN = 512


def _resln_kernel(x_ref, y_ref, g_ref, b_ref, o_ref, *, alpha):
    o_ref[...] = _standardize(alpha * x_ref[...] + y_ref[...]) * g_ref[...] + b_ref[...]


def _resln_call(x2d, y2d, g, bta, alpha):
    n, d = x2d.shape
    tok = pl.BlockSpec((TN, d), lambda i: (i, 0))
    vec = pl.BlockSpec((1, d), lambda i: (0, 0))
    return pl.pallas_call(
        functools.partial(_resln_kernel, alpha=alpha), name="residual_layernorm",
        grid=(n // TN,), in_specs=[tok, tok, vec, vec], out_specs=tok,
        out_shape=jax.ShapeDtypeStruct((n, d), F32),
        compiler_params=_cparams(1),
    )(x2d, y2d, g, bta)


def _expert_choice_ffn(x1p, aff_t, w1, w3, w2, layer):
    b, s, half = x1p.shape
    d = 2 * half
    cap = EC_FACTOR * s // N_EXPERTS
    slot = _select_call(aff_t, cap)
    xs, tok, gate = _sc_dispatch_call(x1p.reshape(b * s, half), slot.reshape(b * N_EXPERTS, s),
                                      aff_t.reshape(b * N_EXPERTS, s), s, cap)
    ye = _expert_call(xs.reshape(b, N_EXPERTS, cap, half), gate.reshape(b, N_EXPERTS, cap, 1), w1, w3, w2,
                      layer)
    out = _sc_combine_call(ye.reshape(b * N_EXPERTS * cap, d),
                           tok.reshape(b * N_EXPERTS, cap // SC_IDX, SC_IDX), s)
    return out.reshape(b, s, d)


def _pack_pool(pool_w):
    g, gd, _ = pool_w.shape
    out = jnp.zeros((g * gd, g * gd), F32)
    for i in range(g):
        out = out.at[i * gd:(i + 1) * gd, i * gd:(i + 1) * gd].set(pool_w[i])
    return out.astype(BF16)


def _layer(layer, x, alpha, bias_tiles, w_in, b_in, gm_ln_g, gm_ws, gm_bs, ml_conv, ml_fbias, ml_norm_g,
           pool_w, pool_scale, w_branch, w_out, ln1_g, ln1_b, w_router, w_e1, w_e3, w_e2, ln2_g, ln2_b):
    b, s, d = x.shape
    n_small = 2576
    w_cat, b_cat = _pack_inproj_weights(w_in, b_in)
    wscat, bsfull = _pack_gmlp(gm_ws, gm_bs)
    ya, qkv1, qkv4, qkv16, cqk, cv, co, dx, ig, fg = _inproj_call(
        x.reshape(b * s, d), w_cat, b_cat, gm_ln_g[None], wscat, bsfull, b)
    r3 = lambda t: t.reshape(b, s, t.shape[-1])
    o_list, l_list = [], []
    for qkv, bias in zip((qkv1, qkv4, qkv16), bias_tiles):
        o, lse = _attn_call(qkv, bias)
        o_list.append(o)
        l_list.append(lse)
    hf, hb = _mlstm_branch(r3(cqk), r3(cv), r3(ig), r3(fg), ml_conv, ml_fbias)
    yd = _pool_call(r3(dx), _pack_pool(pool_w), pool_scale[None])
    x1, x1p, aff_t = _merge_call(
        x, r3(ya), o_list, l_list, hf, hb, r3(co), yd,
        w_in[:, n_small:].astype(BF16), b_in[None, n_small:], w_branch.astype(BF16), w_out.astype(BF16),
        ml_norm_g[:, None], ln1_g[None], ln1_b[None], jnp.transpose(w_router).astype(BF16), alpha)
    ffn = _expert_choice_ffn(x1p, aff_t, w_e1, w_e3, w_e2, layer)
    x2 = _resln_call(x1.reshape(b * s, d), ffn.reshape(b * s, d), ln2_g[None], ln2_b[None], alpha)
    return x2.reshape(b, s, d)


def kernel(x, w_in, b_in, gm_ln_g, gm_ws, gm_bs, rel_bias, ml_conv, ml_fbias, ml_norm_g, pool_w,
           pool_scale, w_branch, w_out, ln1_g, ln1_b, w_router, w_e1, w_e3, w_e2, ln2_g, ln2_b):
    depth = w_in.shape[0]
    alpha = (2 * depth) ** 0.25
    bias_tiles = [_attn_bias_tile(rel_bias, window, dil) for window, dil in DIL_PATTERNS]
    for l in range(depth):
        x = _layer(l, x, alpha, bias_tiles, w_in[l], b_in[l], gm_ln_g[l], gm_ws[l], gm_bs[l], ml_conv[l],
                   ml_fbias[l], ml_norm_g[l], pool_w[l], pool_scale[l], w_branch[l], w_out[l],
                   ln1_g[l], ln1_b[l], w_router[l], w_e1, w_e3, w_e2, ln2_g[l], ln2_b[l])
    return x
```

```python
import functools
import math

import jax
import jax.numpy as jnp
import numpy as np
from jax import lax
from jax.experimental import pallas as pl
from jax.experimental.pallas import tpu as pltpu
from jax.experimental.pallas import tpu_sc as plsc

F32 = jnp.float32
BF16 = jnp.bfloat16

D_MODEL = 1024
MIX_W = 256
N_BRANCH = 4
GM_CHUNK = 128
GM_GROUPS = 4
ATT_HEADS = 4
ATT_HD = 64
DIL_PATTERNS = ((128, 1), (512, 4), (2048, 16))
ATT_BLOCK = 64
REL_BUCKETS = 32
REL_MAX_DIST = 1024
ML_HEADS = 4
ML_HD = 64
ML_CHUNK = 64
POOL_WINDOWS = (2, 4, 8, 16)
N_EXPERTS = 16
EXPERT_FF = 1024
EC_FACTOR = 2
LN_EPS = 1e-5
NEG_BIG = -1e30

V7X_VMEM_LIMIT = 56 * 1024 * 1024
LANES = 128
HALO = 8


def _cparams(n_grid, vmem=V7X_VMEM_LIMIT):
    return pltpu.CompilerParams(dimension_semantics=("arbitrary",) * n_grid,
                                vmem_limit_bytes=vmem)


def _pack_bf16_pair(lo, hi):
    lo_bits = lax.shift_right_logical(pltpu.bitcast(lo.astype(F32), jnp.int32), 16)
    return pltpu.bitcast(hi.astype(F32), jnp.int32) | lo_bits


def _unpack_bf16_pair(packed):
    lo = pltpu.bitcast(lax.shift_left(packed, 16), F32).astype(BF16)
    hi = pltpu.bitcast(packed & jnp.int32(-65536), F32).astype(BF16)
    return lo, hi


def _standardize(xf):
    mu = jnp.mean(xf, axis=-1, keepdims=True)
    var = jnp.mean(jnp.square(xf - mu), axis=-1, keepdims=True)
    return (xf - mu) * lax.rsqrt(var + LN_EPS)


TA = 512
A_COLS = 2560 + 2 * LANES


def _inproj_kernel(*refs, alpha):
    if alpha is None:
        x_ref, *refs = refs
        x = x_ref[...]
    else:
        x1_ref, y_ref, g2_ref, b2_ref, *refs = refs
        x = _standardize(alpha * x1_ref[...] + y_ref[...]) * g2_ref[...] + b2_ref[...]
        refs[-2][...] = x
        refs = refs[:-2] + refs[-1:]
    (w_ref, b_ref, lng_ref, wscat_ref, bsfull_ref,
     ya_ref, qkv1_ref, qkv4_ref, qkv16_ref, cqk_ref, cv_ref, co_ref, dx_ref, ig_ref, fg_ref, qkv_scr) = refs
    xb = x.astype(BF16)
    h = jnp.dot(xb, w_ref[...], preferred_element_type=F32) + b_ref[...]
    qkv1_ref[0, 0] = h[:, 512:1280].astype(BF16)
    for c in range(768 // LANES):
        qkv_scr[c] = h[:, 512 + c * LANES:512 + (c + 1) * LANES]
    for (_, dil), out_ref in zip(DIL_PATTERNS[1:], (qkv4_ref, qkv16_ref)):
        for r in range(dil):
            for c in range(768 // LANES):
                out_ref[0, r, :, c * LANES:(c + 1) * LANES] = (
                    qkv_scr[c, pl.ds(r, TA // dil, stride=dil), :].astype(BF16))
    cqk_ref[...] = h[:, 1280:1792]
    cv_ref[...] = h[:, 1792:2048].astype(BF16)
    co_ref[...] = h[:, 2048:2304]
    dx_ref[...] = h[:, 2304:2560]
    ig_ref[...] = h[:, 2560:2688]
    fg_ref[...] = h[:, 2688:2816]
    u = jax.nn.gelu(h[:, 0:256])
    v = jax.nn.gelu(h[:, 256:512])
    vn = _standardize(v) * lng_ref[...]
    lane_grp = lax.broadcasted_iota(jnp.int32, (GM_CHUNK, MIX_W), 1) // (MIX_W // GM_GROUPS)
    for c in range(TA // GM_CHUNK):
        vc = vn[c * GM_CHUNK:(c + 1) * GM_CHUNK]
        stacked = jnp.concatenate(
            [jnp.where(lane_grp == g, vc, 0.0).astype(BF16) for g in range(GM_GROUPS)], axis=0)
        mixed = jnp.dot(wscat_ref[...], stacked, preferred_element_type=F32) + bsfull_ref[...]
        ya_ref[c * GM_CHUNK:(c + 1) * GM_CHUNK, :] = (
            u[c * GM_CHUNK:(c + 1) * GM_CHUNK] * mixed).astype(BF16)


def _inproj_call(x_in, w_cat, b_cat, lng, wscat, bsfull, batch, alpha=None):
    fused = alpha is not None
    n = (x_in[0] if fused else x_in).shape[0]
    seq = n // batch
    tpb = seq // TA
    tok = lambda w: pl.BlockSpec((TA, w), lambda i: (i, 0))
    const = lambda s: pl.BlockSpec(s, lambda i: (0,) * len(s))
    regrouped = lambda dil: pl.BlockSpec((1, dil, TA // dil, 768), lambda i: (i // tpb, 0, i % tpb, 0))
    out_shape = (
        jax.ShapeDtypeStruct((n, 256), BF16),
    ) + tuple(jax.ShapeDtypeStruct((batch, dil, seq // dil, 768), BF16)
              for _, dil in DIL_PATTERNS) + (
        jax.ShapeDtypeStruct((n, 512), F32),
        jax.ShapeDtypeStruct((n, 256), BF16),
        jax.ShapeDtypeStruct((n, 256), F32),
        jax.ShapeDtypeStruct((n, 256), F32),
        jax.ShapeDtypeStruct((n, LANES), F32),
        jax.ShapeDtypeStruct((n, LANES), F32),
    )
    x_specs = [tok(D_MODEL), tok(D_MODEL), const((1, D_MODEL)), const((1, D_MODEL))] if fused else [tok(D_MODEL)]
    out_specs = ((tok(256),) + tuple(regrouped(dil) for _, dil in DIL_PATTERNS)
                 + (tok(512), tok(256), tok(256), tok(256), tok(LANES), tok(LANES)))
    if fused:
        out_specs += (tok(D_MODEL),)
        out_shape += (jax.ShapeDtypeStruct((n, D_MODEL), F32),)
    return pl.pallas_call(
        functools.partial(_inproj_kernel, alpha=alpha), name="inproj_gmlp",
        grid=(n // TA,),
        in_specs=x_specs + [const((D_MODEL, A_COLS)), const((1, A_COLS)), const((1, MIX_W)),
                            const((GM_CHUNK, GM_GROUPS * GM_CHUNK)), const((GM_CHUNK, MIX_W))],
        out_specs=out_specs,
        out_shape=out_shape,
        scratch_shapes=[pltpu.VMEM((768 // LANES, TA, LANES), F32)],
        compiler_params=_cparams(1),
    )(*(x_in if fused else (x_in,)), w_cat, b_cat, lng, wscat, bsfull)


def _pack_inproj_weights(w_in, b_in):
    pad = lambda a: jnp.pad(a, ((0, 0), (0, LANES - 8)))
    w_cat = jnp.concatenate([w_in[:, 0:2304], w_in[:, 2320:2576],
                             pad(w_in[:, 2304:2312]), pad(w_in[:, 2312:2320])], axis=1)
    b2 = b_in[None, :]
    b_cat = jnp.concatenate([b2[:, 0:2304], b2[:, 2320:2576],
                             pad(b2[:, 2304:2312]), pad(b2[:, 2312:2320])], axis=1)
    return w_cat.astype(BF16), b_cat


def _pack_gmlp(gm_ws, gm_bs):
    wscat = jnp.transpose(gm_ws, (1, 0, 2)).reshape(GM_CHUNK, GM_GROUPS * GM_CHUNK).astype(BF16)
    bsfull = jnp.repeat(jnp.transpose(gm_bs), MIX_W // GM_GROUPS, axis=1)
    return wscat, bsfull


def _halo_specs(t, width, n_tiles):
    r = t // HALO
    main = pl.BlockSpec((1, t, width), lambda b, i: (b, i, 0))
    prev = pl.BlockSpec((1, HALO, width), lambda b, i: (b, jnp.maximum(i * r - 1, 0), 0))
    nxt = pl.BlockSpec((1, HALO, width), lambda b, i: (b, jnp.minimum((i + 1) * r, n_tiles * r - 1), 0))
    return main, prev, nxt


def _fill_halo_scratch(buf, x_ref, p_ref, n_ref, t):
    i = pl.program_id(1)
    last = pl.num_programs(1) - 1
    buf[0:HALO, :] = jnp.where(i > 0, p_ref[0], 0.0)
    buf[HALO:HALO + t, :] = x_ref[0]
    buf[HALO + t:2 * HALO + t, :] = jnp.where(i < last, n_ref[0], 0.0)


TP = 512


def _pool_kernel(x_ref, p_ref, n_ref, w_ref, sc_ref, o_ref, buf):
    _fill_halo_scratch(buf, x_ref, p_ref, n_ref, TP)
    seq = pl.num_programs(1) * TP
    pos = pl.program_id(1) * TP + lax.broadcasted_iota(jnp.int32, (TP, 1), 0)
    lane_grp = lax.broadcasted_iota(jnp.int32, (TP, MIX_W), 1) // (MIX_W // len(POOL_WINDOWS))
    x0 = buf[HALO:HALO + TP, :]
    pooled = jnp.zeros((TP, MIX_W), F32)
    acc = None
    half_done = 0
    for gi, win in enumerate(POOL_WINDOWS):
        half = win // 2
        for o in list(range(-half, -half_done)) + list(range(half_done, half)):
            term = buf[HALO + o:HALO + o + TP, :]
            acc = term if acc is None else acc + term
        half_done = half
        cnt = (jnp.minimum(pos + half, seq) - jnp.maximum(pos - half, 0)).astype(F32)
        pooled = jnp.where(lane_grp == gi, acc / cnt - x0, pooled)
    mixed = jnp.dot(pooled.astype(BF16), w_ref[...], preferred_element_type=F32)
    o_ref[0] = (mixed * sc_ref[...]).astype(BF16)


def _pool_call(dx, w_block, scale):
    b, s, _ = dx.shape
    nt = s // TP
    main, prev, nxt = _halo_specs(TP, MIX_W, nt)
    return pl.pallas_call(
        _pool_kernel, name="pool_mixer",
        grid=(b, nt),
        in_specs=[main, prev, nxt,
                  pl.BlockSpec((MIX_W, MIX_W), lambda b, i: (0, 0)),
                  pl.BlockSpec((1, MIX_W), lambda b, i: (0, 0))],
        out_specs=pl.BlockSpec((1, TP, MIX_W), lambda b, i: (b, i, 0)),
        out_shape=jax.ShapeDtypeStruct((b, s, MIX_W), BF16),
        scratch_shapes=[pltpu.VMEM((TP + 2 * HALO, MIX_W), F32)],
        compiler_params=_cparams(2),
    )(dx, dx, dx, w_block, scale)


TQ = 128
TQS = 512
TKEYS = TQ + 2 * ATT_BLOCK


def _attn_kernel(q_ref, kp_ref, km_ref, kn_ref, vp_ref, vm_ref, vn_ref, bias_ref, o_ref, lse_ref):
    i = pl.program_id(2)
    seq = pl.num_programs(2) * TQS
    q = q_ref[0, 0] * ATT_HD ** -0.5
    k = jnp.concatenate([kp_ref[0, 0], km_ref[0, 0], kn_ref[0, 0]], axis=0)
    v = jnp.concatenate([vp_ref[0, 0], vm_ref[0, 0], vn_ref[0, 0]], axis=0)
    lane = lax.broadcasted_iota(jnp.int32, (TQ, LANES), 1)
    lane_half = lax.broadcasted_iota(jnp.int32, (1, LANES), 1) // ATT_HD
    keep = [jnp.where(lane_half == hh, 1.0, 0.0).astype(BF16) for hh in range(2)]
    for j in range(TQS // TQ):
        kpos = i * TQS + j * TQ - ATT_BLOCK + lax.broadcasted_iota(jnp.int32, (1, TKEYS), 1)
        kvalid = (kpos >= 0) & (kpos < seq)
        qrows = slice(j * TQ, (j + 1) * TQ)
        krows = slice(j * TQ, j * TQ + TKEYS)
        lse_tile = jnp.zeros((TQ, LANES), F32)
        for pair in range(ATT_HEADS // 2):
            grp = slice(pair * LANES, (pair + 1) * LANES)
            q_pair, k_pair, v_pair = q[qrows, grp], k[krows, grp], v[krows, grp]
            o_pair = jnp.zeros((TQ, LANES), F32)
            for hh in range(2):
                h = 2 * pair + hh
                logits = lax.dot_general(q_pair * keep[hh], k_pair, (((1,), (1,)), ((), ())),
                                         preferred_element_type=F32) + bias_ref[h]
                logits = jnp.where(kvalid, logits, NEG_BIG)
                m = jnp.max(logits, axis=-1, keepdims=True)
                p = jnp.exp(logits - m)
                ssum = jnp.sum(p, axis=-1, keepdims=True)
                o = jnp.dot(p.astype(BF16), v_pair, preferred_element_type=F32) / ssum
                o_pair = jnp.where(lane_half == hh, o, o_pair)
                lse_tile = jnp.where(lane == h, m + jnp.log(ssum), lse_tile)
            o_ref[0, 0, qrows, grp] = o_pair
        lse_ref[0, 0, qrows, :] = lse_tile


def _attn_call(qkv, bias):
    b, dil, l, _ = qkv.shape
    nt = l // TQS
    r64 = TQS // ATT_BLOCK
    main = lambda c: pl.BlockSpec((1, 1, TQS, MIX_W), lambda b, r, i: (b, r, i, c))
    prev = lambda c: pl.BlockSpec((1, 1, ATT_BLOCK, MIX_W),
                                  lambda b, r, i: (b, r, jnp.maximum(i * r64 - 1, 0), c))
    nxt = lambda c: pl.BlockSpec((1, 1, ATT_BLOCK, MIX_W),
                                 lambda b, r, i: (b, r, jnp.minimum((i + 1) * r64, nt * r64 - 1), c))
    return pl.pallas_call(
        _attn_kernel, name="band_attention",
        grid=(b, dil, nt),
        in_specs=[main(0), prev(1), main(1), nxt(1), prev(2), main(2), nxt(2),
                  pl.BlockSpec((ATT_HEADS, TQ, TKEYS), lambda b, r, i: (0, 0, 0))],
        out_specs=(pl.BlockSpec((1, 1, TQS, MIX_W), lambda b, r, i: (b, r, i, 0)),
                   pl.BlockSpec((1, 1, TQS, LANES), lambda b, r, i: (b, r, i, 0))),
        out_shape=(jax.ShapeDtypeStruct((b, dil, l, MIX_W), F32),
                   jax.ShapeDtypeStruct((b, dil, l, LANES), F32)),
        compiler_params=_cparams(3),
    )(qkv, qkv, qkv, qkv, qkv, qkv, qkv, bias)


def _t5_bucket_static(rel):
    half = REL_BUCKETS // 2
    max_exact = half // 2
    ret = np.where(rel > 0, half, 0)
    n = np.abs(rel)
    nf = np.maximum(n, 1).astype(np.float32)
    large = max_exact + (np.log(nf / np.float32(max_exact)) / np.float32(math.log(REL_MAX_DIST / max_exact))
                         * np.float32(half - max_exact)).astype(np.int32)
    large = np.minimum(large, half - 1)
    return ret + np.where(n < max_exact, n, large)


def _attn_bias_tile(rel_bias, window, dil):
    side = (window // 2) // dil
    rel = np.arange(TKEYS)[None, :] - ATT_BLOCK - np.arange(TQ)[:, None]
    onehot = jax.nn.one_hot(jnp.asarray(_t5_bucket_static(dil * rel), jnp.int32), REL_BUCKETS, dtype=F32)
    bias = jnp.einsum('qkr,rh->hqk', onehot, rel_bias, precision=lax.Precision.HIGHEST)
    return jnp.where(jnp.asarray(np.abs(rel) <= side)[None], bias, NEG_BIG)


TM = 512
VT_ROWS = ML_HD + 16


def _mlprep_kernel(x_ref, p_ref, n_ref, v_ref, w_ref, qt_out, k_out, vt_out, buf):
    _fill_halo_scratch(buf, x_ref, p_ref, n_ref, TM)
    conv = (buf[HALO - 1:HALO - 1 + TM, :] * w_ref[0:1, :] + buf[HALO:HALO + TM, :] * w_ref[1:2, :]
            + buf[HALO + 1:HALO + 1 + TM, :] * w_ref[2:3, :])
    qk = jax.nn.silu(conv)
    qt = jnp.transpose(qk[:, :MIX_W])
    vt = jnp.transpose(v_ref[0].astype(F32))
    ones_rows = jnp.where(lax.broadcasted_iota(jnp.int32, (VT_ROWS - ML_HD, ML_CHUNK), 0) == 0, 1.0, 0.0)
    for h in range(ML_HEADS):
        sl = slice(h * ML_HD, (h + 1) * ML_HD)
        k_out[0, h] = (qk[:, MIX_W + h * ML_HD:MIX_W + (h + 1) * ML_HD] * ML_HD ** -0.5).astype(BF16)
        for c in range(TM // ML_CHUNK):
            cl = slice(c * ML_CHUNK, (c + 1) * ML_CHUNK)
            qt_out[0, h, c] = qt[sl, cl].astype(BF16)
            vt_out[0, h, c] = jnp.concatenate([vt[sl, cl], ones_rows], axis=0).astype(BF16)


def _mlprep_call(cqk, cv, conv_w):
    b, s, _ = cqk.shape
    nt = s // TM
    nc = s // ML_CHUNK
    cpt = TM // ML_CHUNK
    main, prev, nxt = _halo_specs(TM, 2 * MIX_W, nt)
    return pl.pallas_call(
        _mlprep_kernel, name="mlstm_prep",
        grid=(b, nt),
        in_specs=[main, prev, nxt,
                  pl.BlockSpec((1, TM, MIX_W), lambda b, i: (b, i, 0)),
                  pl.BlockSpec((3, 2 * MIX_W), lambda b, i: (0, 0))],
        out_specs=(pl.BlockSpec((1, ML_HEADS, cpt, ML_HD, ML_CHUNK), lambda b, i: (b, 0, i, 0, 0)),
                   pl.BlockSpec((1, ML_HEADS, TM, ML_HD), lambda b, i: (b, 0, i, 0)),
                   pl.BlockSpec((1, ML_HEADS, cpt, VT_ROWS, ML_CHUNK), lambda b, i: (b, 0, i, 0, 0))),
        out_shape=(jax.ShapeDtypeStruct((b, ML_HEADS, nc, ML_HD, ML_CHUNK), BF16),
                   jax.ShapeDtypeStruct((b, ML_HEADS, s, ML_HD), BF16),
                   jax.ShapeDtypeStruct((b, ML_HEADS, nc, VT_ROWS, ML_CHUNK), BF16)),
        scratch_shapes=[pltpu.VMEM((TM + 2 * HALO, 2 * MIX_W), F32)],
        compiler_params=_cparams(2),
    )(cqk, cqk, cqk, cv, conv_w)


GP_BLK = 512


def _chunk_scan(x, fwd_lane, t_in_chunk, op, ident):
    n = x.shape[0]
    shift = 1
    while shift < ML_CHUNK:
        down = pltpu.roll(x, shift, 0)
        up = pltpu.roll(x, n - shift, 0)
        nb = jnp.where(fwd_lane,
                       jnp.where(t_in_chunk >= shift, down, ident),
                       jnp.where(t_in_chunk < ML_CHUNK - shift, up, ident))
        x = op(x, nb)
        shift *= 2
    return x


def _gatescan_kernel(ig_ref, fg_ref, fb_ref, b_ref, a_ref, cm_ref, g_ref, amax_ref):
    lane = lax.broadcasted_iota(jnp.int32, (1, LANES), 1)
    fwd_lane = lane < ML_HEADS
    t_in_chunk = lax.broadcasted_iota(jnp.int32, (GP_BLK, 1), 0) % ML_CHUNK
    z = fg_ref[0] + fb_ref[...]
    lf = jnp.minimum(z, 0.0) - jnp.log1p(jnp.exp(-jnp.abs(z)))
    b = _chunk_scan(lf, fwd_lane, t_in_chunk, jnp.add, 0.0)
    a = ig_ref[0] - b
    b_ref[0] = b
    a_ref[0] = a
    cm_ref[0] = _chunk_scan(a, fwd_lane, t_in_chunk, jnp.maximum, -jnp.inf)
    cpb = GP_BLK // ML_CHUNK
    last = pl.ds(ML_CHUNK - 1, cpb, stride=ML_CHUNK)
    first = pl.ds(0, cpb, stride=ML_CHUNK)
    g_ref[0] = jnp.where(fwd_lane, b_ref[0, last, :], b_ref[0, first, :])
    amax_ref[0] = jnp.where(fwd_lane, cm_ref[0, last, :], cm_ref[0, first, :])


def _gateout_kernel(b_ref, a_ref, cm_ref, g_ref, amax_ref, ws_ref, m_ref, iw_ref, en_ref, dec_ref,
                    mch_s, mlast_s):
    nc = g_ref.shape[1]
    j = pl.program_id(1)
    fwd_lane = lax.broadcasted_iota(jnp.int32, (1, LANES), 1) < ML_HEADS

    @pl.when(j == 0)
    def _():
        def m_step(i, carry):
            mf, mb = carry
            cf = pl.ds(i, 1)
            cb = pl.ds(nc - 1 - i, 1)
            mch_s[cf, :] = jnp.where(fwd_lane, mf, mch_s[cf, :])
            mch_s[cb, :] = jnp.where(fwd_lane, mch_s[cb, :], mb)
            mf = g_ref[0, cf, :] + jnp.maximum(mf, amax_ref[0, cf, :])
            mb = g_ref[0, cb, :] + jnp.maximum(mb, amax_ref[0, cb, :])
            return mf, mb

        mch_s[...] = jnp.zeros((nc, LANES), F32)
        zero = jnp.zeros((1, LANES), F32)
        lax.fori_loop(0, nc, m_step, (zero, zero))
        mlast = jnp.maximum(amax_ref[0], mch_s[...])
        mlast_s[...] = mlast
        dec_ref[0] = jnp.exp(mch_s[...] - mlast)

    cpb = GP_BLK // ML_CHUNK
    crow = pl.ds(pl.multiple_of(j * cpb, cpb), cpb)
    expand = lambda t: jnp.broadcast_to(t[:, None, :], (cpb, ML_CHUNK, LANES)).reshape(GP_BLK, LANES)
    m_tok = expand(mch_s[crow, :])
    mlast_tok = expand(mlast_s[crow, :])
    mt = jnp.maximum(cm_ref[0], m_tok)
    m_ref[0] = mt
    iw_ref[0] = jnp.exp(m_tok - mt)
    en_ref[0] = jnp.exp(-(b_ref[0] + mt))
    ws_ref[0] = jnp.exp(a_ref[0] - mlast_tok)


def _gateprep_call(ig, fg, fbias_row):
    b, s, _ = ig.shape
    nc = s // ML_CHUNK
    cpb = GP_BLK // ML_CHUNK
    tok = pl.BlockSpec((1, GP_BLK, LANES), lambda i, j: (i, j, 0))
    tok_shape = jax.ShapeDtypeStruct((b, s, LANES), F32)
    chunk_shape = jax.ShapeDtypeStruct((b, nc, LANES), F32)
    chunk_tile = pl.BlockSpec((1, cpb, LANES), lambda i, j: (i, j, 0))
    chunk_all = pl.BlockSpec((1, nc, LANES), lambda i, j: (i, 0, 0))
    bcum, a, cm, g, amax = pl.pallas_call(
        _gatescan_kernel, name="mlstm_gate_scan",
        grid=(b, s // GP_BLK),
        in_specs=[tok, tok, pl.BlockSpec((1, LANES), lambda i, j: (0, 0))],
        out_specs=(tok, tok, tok, chunk_tile, chunk_tile),
        out_shape=(tok_shape,) * 3 + (chunk_shape,) * 2,
        compiler_params=_cparams(2),
    )(ig, fg, fbias_row)
    ws, m_col, iw_col, en_col, decay = pl.pallas_call(
        _gateout_kernel, name="mlstm_gate_out",
        grid=(b, s // GP_BLK),
        in_specs=[tok, tok, tok, chunk_all, chunk_all],
        out_specs=(tok, tok, tok, tok, chunk_all),
        out_shape=(tok_shape,) * 4 + (chunk_shape,),
        scratch_shapes=[pltpu.VMEM((nc, LANES), F32)] * 2,
        compiler_params=_cparams(2),
    )(bcum, a, cm, g, amax)
    return a, ws, m_col, iw_col, en_col, decay


TE = 1024


def _mlstm_kernel(*refs):
    fwd, bwd, (hf_ref, hb_ref, state) = refs[:9], refs[9:18], refs[18:]
    i = pl.program_id(1)

    @pl.when(i == 0)
    def _():
        state[...] = jnp.zeros(state.shape, F32)

    cpt = TE // ML_CHUNK
    s_idx = lax.broadcasted_iota(jnp.int32, (ML_CHUNK, ML_CHUNK), 0)
    t_idx = lax.broadcasted_iota(jnp.int32, (ML_CHUNK, ML_CHUNK), 1)

    def chunk_body(c, carry):
        jobs = []
        for d, (qt_r, k_r, vt_r, a_r, m_r, iw_r, en_r, ws_r, dec_r), out_r in ((0, fwd, hf_ref), (1, bwd, hb_ref)):
            cc = c if d == 0 else cpt - 1 - c
            rows = pl.ds(pl.multiple_of(cc * ML_CHUNK, ML_CHUNK), ML_CHUNK)
            crow = pl.ds(cc, 1)
            for h in range(ML_HEADS):
                ch = d * ML_HEADS + h
                row = lambda r: r[0, ch, crow, :]
                jobs.append(dict(
                    ch=ch, tri=(s_idx >= t_idx) if d else (s_idx <= t_idx),
                    k=k_r[0, h, rows, :], qt=qt_r[0, h, cc], vt=vt_r[0, h, cc], a=a_r[0, rows, ch:ch + 1],
                    m=row(m_r), iw=row(iw_r), en=row(en_r), ws=row(ws_r), dec=row(dec_r),
                    out=(out_r, cc, h)))
        for j in jobs:
            j["cst"] = state[j["ch"]]
            j["st"] = jnp.dot(j["k"], j["qt"], preferred_element_type=F32)
            j["inter"] = jnp.dot(j["cst"].astype(BF16), j["qt"], preferred_element_type=F32)
            j["upd"] = jnp.dot((j["vt"].astype(F32) * j["ws"]).astype(BF16), j["k"],
                               preferred_element_type=F32)
        for j in jobs:
            j["swt"] = j["st"] * jnp.exp(jnp.where(j["tri"], j["a"] - j["m"], NEG_BIG))
            j["intra"] = jnp.dot(j["vt"], j["swt"].astype(BF16), preferred_element_type=F32)
        for j in jobs:
            den = jnp.sum(j["swt"], axis=0, keepdims=True) + j["iw"] * j["inter"][ML_HD:ML_HD + 1]
            tot = j["intra"][:ML_HD] + j["iw"] * j["inter"][:ML_HD]
            out_r, cc, h = j["out"]
            out_r[0, cc, h * ML_HD:(h + 1) * ML_HD, :] = tot / jnp.maximum(jnp.abs(den), j["en"])
            state[j["ch"]] = j["dec"] * j["cst"] + j["upd"]
        return carry

    lax.fori_loop(0, cpt, chunk_body, 0)


def _mlstm_call(qt, k, vt, a_col, m_row, iw_row, en_row, ws_row, dec_row):
    b, _, s, _ = k.shape
    nt = s // TE
    nc = s // ML_CHUNK
    cpt = TE // ML_CHUNK

    def specs(rev):
        ti = (lambda i: nt - 1 - i) if rev else (lambda i: i)
        row = pl.BlockSpec((1, 2 * ML_HEADS, cpt, ML_CHUNK), lambda b, i: (b, 0, ti(i), 0))
        return [
            pl.BlockSpec((1, ML_HEADS, cpt, ML_HD, ML_CHUNK), lambda b, i: (b, 0, ti(i), 0, 0)),
            pl.BlockSpec((1, ML_HEADS, TE, ML_HD), lambda b, i: (b, 0, ti(i), 0)),
            pl.BlockSpec((1, ML_HEADS, cpt, VT_ROWS, ML_CHUNK), lambda b, i: (b, 0, ti(i), 0, 0)),
            pl.BlockSpec((1, TE, LANES), lambda b, i: (b, ti(i), 0)),
            row, row, row, row, row]

    args = [qt, k, vt, a_col, m_row, iw_row, en_row, ws_row, dec_row]
    out_f = pl.BlockSpec((1, cpt, MIX_W, ML_CHUNK), lambda b, i: (b, i, 0, 0))
    out_b = pl.BlockSpec((1, cpt, MIX_W, ML_CHUNK), lambda b, i: (b, nt - 1 - i, 0, 0))
    return pl.pallas_call(
        _mlstm_kernel, name="mlstm_scan",
        grid=(b, nt),
        in_specs=specs(False) + specs(True),
        out_specs=(out_f, out_b),
        out_shape=(jax.ShapeDtypeStruct((b, nc, MIX_W, ML_CHUNK), F32),) * 2,
        scratch_shapes=[pltpu.VMEM((2 * ML_HEADS, VT_ROWS, ML_HD), F32)],
        compiler_params=_cparams(2),
    )(*args, *args)


def _mlstm_branch(cqk, cv, ig, fg, conv_w, fbias):
    b, s, _ = cqk.shape
    nc = s // ML_CHUNK
    qt, k, vt = _mlprep_call(cqk, cv, conv_w)
    fb_row = jnp.pad(fbias.reshape(1, 2 * ML_HEADS), ((0, 0), (0, LANES - 2 * ML_HEADS)))
    a, ws, m_col, iw_col, en_col, decay = _gateprep_call(ig, fg, fb_row)
    rows = lambda t: jnp.transpose(t[..., :2 * ML_HEADS], (0, 2, 1)).reshape(b, 2 * ML_HEADS, nc, ML_CHUNK)
    dec_row = jnp.broadcast_to(jnp.transpose(decay[..., :2 * ML_HEADS], (0, 2, 1))[..., None],
                               (b, 2 * ML_HEADS, nc, ML_CHUNK))
    return _mlstm_call(qt, k, vt, a, rows(m_col), rows(iw_col), rows(en_col), rows(ws), dec_row)


TF = 512


def _merge_kernel(x_ref, ya_ref, o1_ref, o2_ref, o3_ref, l1_ref, l2_ref, l3_ref, hf_ref, hb_ref,
                  co_ref, yd_ref, wg_ref, bg_ref, wbr_ref, wout_ref, mng_ref, lng_ref, lnb_ref,
                  wr_ref, x1_ref, x1p_ref, aff_ref, o_scr, l_scr, *, alpha):
    x = x_ref[0]
    xb = x.astype(BF16)

    def natural_order(src_ref, scr):
        dil, width = src_ref.shape[1], src_ref.shape[3]
        if dil == 1:
            return src_ref[0, 0]
        for r in range(dil):
            for c in range(width // LANES):
                scr[c, pl.ds(r, TF // dil, stride=dil), :] = src_ref[0, r, :, c * LANES:(c + 1) * LANES]
        return jnp.concatenate([scr[c] for c in range(width // LANES)], axis=1)

    lane_head = lax.broadcasted_iota(jnp.int32, (TF, MIX_W), 1) // ML_HD
    l1, l2, l3 = [natural_order(r, l_scr.at[p]) for p, r in enumerate((l1_ref, l2_ref, l3_ref))]
    o1, o2, o3 = [natural_order(r, o_scr.at[p]) for p, r in enumerate((o1_ref, o2_ref, o3_ref))]
    lm = jnp.maximum(jnp.maximum(l1, l2), l3)
    e1, e2, e3 = jnp.exp(l1 - lm), jnp.exp(l2 - lm), jnp.exp(l3 - lm)
    inv = 1.0 / (e1 + e2 + e3)

    def per_head(w):
        out = jnp.zeros((TF, MIX_W), F32)
        for h in range(ATT_HEADS):
            out = jnp.where(lane_head == h, w[:, h:h + 1], out)
        return out

    y_b = per_head(e1 * inv) * o1 + per_head(e2 * inv) * o2 + per_head(e3 * inv) * o3
    hsum_t = jnp.concatenate([hf_ref[0, c] + hb_ref[0, c] for c in range(TF // ML_CHUNK)], axis=1)
    per_head_rows = hsum_t.reshape(ML_HEADS, ML_HD, TF)
    mu = jnp.mean(per_head_rows, axis=1, keepdims=True)
    cen = per_head_rows - mu
    var = jnp.mean(cen * cen, axis=1, keepdims=True)
    hn_t = (cen * lax.rsqrt(var + LN_EPS)).reshape(MIX_W, TF)
    y_c_t = (jax.nn.sigmoid(jnp.transpose(co_ref[0])) * (hn_t * mng_ref[...])).astype(BF16)
    ys = (ya_ref[0], y_b.astype(BF16), None, yd_ref[0])
    merged = jnp.zeros((TF, D_MODEL), F32)
    for n in range(N_BRANCH):
        cols = slice(n * D_MODEL, (n + 1) * D_MODEL)
        gate = jax.nn.sigmoid(jnp.dot(xb, wg_ref[:, cols], preferred_element_type=F32) + bg_ref[:, cols])
        if ys[n] is None:
            proj = lax.dot_general(y_c_t, wbr_ref[n], (((0,), (0,)), ((), ())), preferred_element_type=F32)
        else:
            proj = jnp.dot(ys[n], wbr_ref[n], preferred_element_type=F32)
        merged = merged + gate * proj
    mix = jnp.dot(merged.astype(BF16), wout_ref[...], preferred_element_type=F32)
    x1 = _standardize(alpha * x + mix) * lng_ref[...] + lnb_ref[...]
    x1_ref[0] = x1
    x1b = x1.astype(BF16)
    x1p_ref[0] = _pack_bf16_pair(x1b[:, :D_MODEL // 2], x1b[:, D_MODEL // 2:])
    logits = lax.dot_general(wr_ref[...], x1b, (((1,), (1,)), ((), ())),
                             preferred_element_type=F32)
    ex = jnp.exp(logits - jnp.max(logits, axis=0, keepdims=True))
    aff_ref[0] = ex / jnp.sum(ex, axis=0, keepdims=True)


def _merge_call(x, ya, o_list, l_list, hf, hb, co, yd, wg, bg, wbr, wout, mng, lng, lnb, wr_t, alpha):
    b, s, _ = x.shape
    tok = lambda w: pl.BlockSpec((1, TF, w), lambda b, i: (b, i, 0))
    grouped = lambda dil, w: pl.BlockSpec((1, dil, TF // dil, w), lambda b, i: (b, 0, i, 0))
    chunked = pl.BlockSpec((1, TF // ML_CHUNK, MIX_W, ML_CHUNK), lambda b, i: (b, i, 0, 0))
    const = lambda shp: pl.BlockSpec(shp, lambda b, i: (0,) * len(shp))
    return pl.pallas_call(
        functools.partial(_merge_kernel, alpha=alpha), name="merge_ln_router",
        grid=(b, s // TF),
        in_specs=[tok(D_MODEL), tok(MIX_W)] + [grouped(dil, MIX_W) for _, dil in DIL_PATTERNS]
                 + [grouped(dil, LANES) for _, dil in DIL_PATTERNS]
                 + [chunked, chunked, tok(MIX_W), tok(MIX_W)]
                 + [const((D_MODEL, N_BRANCH * D_MODEL)), const((1, N_BRANCH * D_MODEL)),
                    const((N_BRANCH, MIX_W, D_MODEL)), const((D_MODEL, D_MODEL)), const((MIX_W, 1)),
                    const((1, D_MODEL)), const((1, D_MODEL)), const((N_EXPERTS, D_MODEL))],
        out_specs=(tok(D_MODEL), tok(D_MODEL // 2), pl.BlockSpec((1, N_EXPERTS, TF), lambda b, i: (b, 0, i))),
        out_shape=(jax.ShapeDtypeStruct((b, s, D_MODEL), F32),
                   jax.ShapeDtypeStruct((b, s, D_MODEL // 2), jnp.int32),
                   jax.ShapeDtypeStruct((b, N_EXPERTS, s), F32)),
        scratch_shapes=[pltpu.VMEM((len(DIL_PATTERNS), MIX_W // LANES, TF, LANES), F32),
                        pltpu.VMEM((len(DIL_PATTERNS), 1, TF, LANES), F32)],
        compiler_params=_cparams(2),
    )(x, ya, *o_list, *l_list, hf, hb, co, yd, wg, bg, wbr, wout, mng, lng, lnb, wr_t)


TT = 256


def _select_kernel(aff_ref, slot_ref, *, cap):
    s = aff_ref.shape[2]
    bits = pltpu.bitcast(aff_ref[0], jnp.int32)

    def bit_step(i, thr):
        cand = thr | jnp.left_shift(jnp.int32(1), 30 - i)
        cnt = jnp.sum((bits >= cand).astype(jnp.int32), axis=1, keepdims=True)
        return jnp.where(cnt >= cap, cand, thr)

    thr = lax.fori_loop(0, 31, bit_step, jnp.zeros((N_EXPERTS, 1), jnp.int32))
    gt = bits > thr
    eq = bits == thr
    need = (cap - jnp.sum(gt.astype(jnp.int32), axis=1, keepdims=True)).astype(F32)
    upper = (lax.broadcasted_iota(jnp.int32, (TT, TT), 0)
             <= lax.broadcasted_iota(jnp.int32, (TT, TT), 1)).astype(BF16)
    eq_before = jnp.zeros((N_EXPERTS, 1), F32)
    sel_before = jnp.zeros((N_EXPERTS, 1), F32)
    for j in range(s // TT):
        cols = slice(j * TT, (j + 1) * TT)
        eq_j = eq[:, cols]
        eq_incl = eq_before + jnp.dot(eq_j.astype(BF16), upper, preferred_element_type=F32)
        sel_j = gt[:, cols] | (eq_j & (eq_incl <= need))
        sel_f = sel_j.astype(F32)
        sel_incl = sel_before + jnp.dot(sel_f.astype(BF16), upper, preferred_element_type=F32)
        slot_ref[0, :, cols] = jnp.where(sel_j, sel_incl - 1.0, -1.0).astype(jnp.int32)
        eq_before = eq_incl[:, TT - 1:TT]
        sel_before = sel_incl[:, TT - 1:TT]


def _select_call(aff_t, cap):
    b, e, s = aff_t.shape
    return pl.pallas_call(
        functools.partial(_select_kernel, cap=cap), name="expert_choice_select",
        grid=(b,),
        in_specs=[pl.BlockSpec((1, e, s), lambda i: (i, 0, 0))],
        out_specs=pl.BlockSpec((1, e, s), lambda i: (i, 0, 0)),
        out_shape=jax.ShapeDtypeStruct((b, e, s), jnp.int32),
        compiler_params=_cparams(1),
    )(aff_t)


SC_LANES = 16
SC_ROWS = 64
SC_IDX = 128
SC_SLAB = 128
SC_ZROWS = 64
CF = 1024


def _sc_dispatch_call(x_flat, slot2, aff2, seq, cap):
    n_pairs = slot2.shape[0]
    d = x_flat.shape[1]
    info = plsc.get_sparse_core_info()
    n_workers = info.num_cores * info.num_subcores
    assert n_pairs % n_workers == 0 and seq % SC_LANES == 0 and cap % (2 * SC_ROWS) == 0
    pairs_per_worker = n_pairs // n_workers
    mesh = plsc.VectorSubcoreMesh(core_axis_name="c", subcore_axis_name="s")

    @functools.partial(
        pl.kernel, mesh=mesh, name="expert_dispatch_sc",
        compiler_params=pltpu.CompilerParams(needs_layout_passes=False),
        out_type=(jax.ShapeDtypeStruct((n_pairs * cap, d), x_flat.dtype),
                  jax.ShapeDtypeStruct((n_pairs, cap), jnp.int32),
                  jax.ShapeDtypeStruct((n_pairs, cap), F32)),
        scratch_types=[pltpu.VMEM((seq,), jnp.int32), pltpu.VMEM((seq,), F32),
                       pltpu.VMEM((cap,), jnp.int32), pltpu.VMEM((cap,), jnp.int32),
                       pltpu.VMEM((cap,), F32),
                       pltpu.VMEM((2, SC_ROWS, d), x_flat.dtype), pltpu.SemaphoreType.DMA((2,))])
    def dispatch(x_hbm, slot_hbm, aff_hbm, xs_hbm, tok_hbm, gate_hbm,
                 slot_v, aff_v, idx_v, tok_v, gate_v, rows_v, sem):
        worker = lax.axis_index("s") * info.num_cores + lax.axis_index("c")
        lane = lax.iota(jnp.int32, SC_LANES)

        def gather(c0, buf):
            return pltpu.make_async_copy(x_hbm.at[idx_v.at[pl.ds(c0, SC_ROWS)]], rows_v.at[buf], sem.at[buf])
        for k in range(pairs_per_worker):
            pair = worker * pairs_per_worker + k
            row0 = (pair // N_EXPERTS) * seq
            pltpu.sync_copy(slot_hbm.at[pair], slot_v)
            pltpu.sync_copy(aff_hbm.at[pair], aff_v)

            @pl.loop(0, seq, step=SC_LANES)
            def _(t0):
                sv = slot_v[pl.ds(t0, SC_LANES)]
                picked = sv >= 0
                plsc.store_scatter(tok_v, [sv], t0 + lane, mask=picked)
                plsc.store_scatter(idx_v, [sv], row0 + t0 + lane, mask=picked)
                plsc.store_scatter(gate_v, [sv], aff_v[pl.ds(t0, SC_LANES)], mask=picked)

            pltpu.sync_copy(tok_v, tok_hbm.at[pair])
            pltpu.sync_copy(gate_v, gate_hbm.at[pair])

            gather(0, 0).start()

            @pl.loop(0, cap, step=2 * SC_ROWS)
            def _(c0):
                gather(c0 + SC_ROWS, 1).start()
                gather(c0, 0).wait()
                pltpu.sync_copy(rows_v.at[0], xs_hbm.at[pl.ds(pair * cap + c0, SC_ROWS)])

                @pl.when(c0 + 2 * SC_ROWS < cap)
                def _():
                    gather(c0 + 2 * SC_ROWS, 0).start()

                gather(c0 + SC_ROWS, 1).wait()
                pltpu.sync_copy(rows_v.at[1], xs_hbm.at[pl.ds(pair * cap + c0 + SC_ROWS, SC_ROWS)])

    return dispatch(x_flat, slot2, aff2)


def _expert_kernel(xs_ref, g_ref, w1_ref, w3_ref, w2_ref, ye_ref, w1_bf, w3_bf, w2_bf):
    @pl.when((pl.program_id(1) == 0) & (pl.program_id(2) == 0))
    def _():
        w1_bf[...] = w1_ref[0, 0].astype(BF16)
        w3_bf[...] = w3_ref[0, 0].astype(BF16)
        w2_bf[...] = w2_ref[0, 0].astype(BF16)

    xs = jnp.concatenate(_unpack_bf16_pair(xs_ref[0, 0]), axis=1)
    hid = (jax.nn.silu(jnp.dot(xs, w1_bf[...], preferred_element_type=F32))
           * jnp.dot(xs, w3_bf[...], preferred_element_type=F32))
    ye_ref[0, 0] = jnp.dot(hid.astype(BF16), w2_bf[...], preferred_element_type=F32) * g_ref[0, 0]


def _expert_call(xs4, gate4, w1, w3, w2, layer):
    b, e, cap, half = xs4.shape
    d, ff = w1.shape[2], w1.shape[3]
    assert d == 2 * half
    rows = lambda w: pl.BlockSpec((1, 1, CF, w), lambda e, b, j: (b, e, j, 0))
    wspec = lambda r, c: pl.BlockSpec((1, 1, r, c), lambda e, b, j: (layer, e, 0, 0))
    return pl.pallas_call(
        _expert_kernel, name="expert_ffn",
        grid=(e, b, cap // CF),
        in_specs=[rows(half), rows(1), wspec(d, ff), wspec(d, ff), wspec(ff, d)],
        out_specs=rows(d),
        out_shape=jax.ShapeDtypeStruct((b, e, cap, d), F32),
        scratch_shapes=[pltpu.VMEM((d, ff), BF16), pltpu.VMEM((d, ff), BF16), pltpu.VMEM((ff, d), BF16)],
        compiler_params=_cparams(3),
    )(xs4, gate4, w1, w3, w2)


def _sc_combine_call(ye_flat, tok3, seq):
    n_pairs, n_chunks, _ = tok3.shape
    cap = n_chunks * SC_IDX
    d = ye_flat.shape[1]
    nb = n_pairs // N_EXPERTS
    info = plsc.get_sparse_core_info()
    assert info.num_subcores == N_EXPERTS and nb % info.num_cores == 0 and n_chunks % 2 == 0
    assert seq % (info.num_subcores * SC_ZROWS) == 0 and d % SC_SLAB == 0
    batches_per_core = nb // info.num_cores
    own_rows = seq // info.num_subcores
    mesh = plsc.VectorSubcoreMesh(core_axis_name="c", subcore_axis_name="s")

    @functools.partial(
        pl.kernel, mesh=mesh, name="expert_combine_sc",
        compiler_params=pltpu.CompilerParams(needs_layout_passes=False),
        out_type=jax.ShapeDtypeStruct((nb * seq, d), F32),
        scratch_types=[pltpu.VMEM_SHARED((seq, SC_SLAB), F32),
                       pltpu.VMEM((n_chunks, SC_IDX), jnp.int32),
                       pltpu.VMEM((2, SC_IDX, SC_SLAB), F32),
                       pltpu.VMEM((SC_ZROWS, SC_SLAB), F32),
                       pltpu.SemaphoreType.DMA((2,))])
    def combine(ye_hbm, tok_hbm, out_hbm, acc_sh, tok_v, rows_v, zero_v, sem):
        core = lax.axis_index("c")
        sub = lax.axis_index("s")

        @pl.loop(0, SC_ZROWS)
        def _(r):
            for l0 in range(0, SC_SLAB, SC_LANES):
                zero_v[r, pl.ds(l0, SC_LANES)] = jnp.zeros((SC_LANES,), F32)

        for bb in range(batches_per_core):
            batch = core * batches_per_core + bb
            pair = batch * N_EXPERTS + sub
            pltpu.sync_copy(tok_hbm.at[pair], tok_v)

            @pl.loop(0, d // SC_SLAB)
            def _(slab):
                cols = pl.ds(pl.multiple_of(slab * SC_SLAB, SC_SLAB), SC_SLAB)

                @pl.loop(0, own_rows, step=SC_ZROWS)
                def _(r0):
                    pltpu.sync_copy(zero_v, acc_sh.at[pl.ds(sub * own_rows + r0, SC_ZROWS)])

                def load(j, buf):
                    return pltpu.make_async_copy(
                        ye_hbm.at[pl.ds(pair * cap + j * SC_IDX, SC_IDX), cols], rows_v.at[buf], sem.at[buf])

                load(0, 0).start()
                plsc.subcore_barrier()

                for j in range(0, n_chunks, 2):
                    load(j + 1, 1).start()
                    load(j, 0).wait()
                    pltpu.sync_copy(rows_v.at[0], acc_sh.at[tok_v.at[j]], add=True)
                    if j + 2 < n_chunks:
                        load(j + 2, 0).start()
                    load(j + 1, 1).wait()
                    pltpu.sync_copy(rows_v.at[1], acc_sh.at[tok_v.at[j + 1]], add=True)

                plsc.subcore_barrier()
                pltpu.sync_copy(acc_sh.at[pl.ds(sub * own_rows, own_rows)],
                                out_hbm.at[pl.ds(batch * seq + sub * own_rows, own_rows), cols])

    return combine(ye_flat, tok3)


TN = 512


def _resln_kernel(x_ref, y_ref, g_ref, b_ref, o_ref, *, alpha):
    o_ref[...] = _standardize(alpha * x_ref[...] + y_ref[...]) * g_ref[...] + b_ref[...]


def _resln_call(x2d, y2d, g, bta, alpha):
    n, d = x2d.shape
    tok = pl.BlockSpec((TN, d), lambda i: (i, 0))
    vec = pl.BlockSpec((1, d), lambda i: (0, 0))
    return pl.pallas_call(
        functools.partial(_resln_kernel, alpha=alpha), name="residual_layernorm",
        grid=(n // TN,), in_specs=[tok, tok, vec, vec], out_specs=tok,
        out_shape=jax.ShapeDtypeStruct((n, d), F32),
        compiler_params=_cparams(1),
    )(x2d, y2d, g, bta)


def _expert_choice_ffn(x1p, aff_t, w1, w3, w2, layer):
    b, s, half = x1p.shape
    d = 2 * half
    cap = EC_FACTOR * s // N_EXPERTS
    slot = _select_call(aff_t, cap)
    xs, tok, gate = _sc_dispatch_call(x1p.reshape(b * s, half), slot.reshape(b * N_EXPERTS, s),
                                      aff_t.reshape(b * N_EXPERTS, s), s, cap)
    ye = _expert_call(xs.reshape(b, N_EXPERTS, cap, half), gate.reshape(b, N_EXPERTS, cap, 1), w1, w3, w2,
                      layer)
    out = _sc_combine_call(ye.reshape(b * N_EXPERTS * cap, d),
                           tok.reshape(b * N_EXPERTS, cap // SC_IDX, SC_IDX), s)
    return out.reshape(b, s, d)


def _pack_pool(pool_w):
    g, gd, _ = pool_w.shape
    out = jnp.zeros((g * gd, g * gd), F32)
    for i in range(g):
        out = out.at[i * gd:(i + 1) * gd, i * gd:(i + 1) * gd].set(pool_w[i])
    return out.astype(BF16)


def _layer(layer, x, pending, alpha, bias_tiles, w_in, b_in, gm_ln_g, gm_ws, gm_bs, ml_conv, ml_fbias,
           ml_norm_g, pool_w, pool_scale, w_branch, w_out, ln1_g, ln1_b, w_router, w_e1, w_e3, w_e2):
    b, s, d = x.shape
    n_small = 2576
    w_cat, b_cat = _pack_inproj_weights(w_in, b_in)
    wscat, bsfull = _pack_gmlp(gm_ws, gm_bs)
    if pending is None:
        outs = _inproj_call(x.reshape(b * s, d), w_cat, b_cat, gm_ln_g[None], wscat, bsfull, b)
    else:
        *outs, x2 = _inproj_call(pending, w_cat, b_cat, gm_ln_g[None], wscat, bsfull, b, alpha)
        x = x2.reshape(b, s, d)
    ya, qkv1, qkv4, qkv16, cqk, cv, co, dx, ig, fg = outs
    r3 = lambda t: t.reshape(b, s, t.shape[-1])
    o_list, l_list = [], []
    for qkv, bias in zip((qkv1, qkv4, qkv16), bias_tiles):
        o, lse = _attn_call(qkv, bias)
        o_list.append(o)
        l_list.append(lse)
    hf, hb = _mlstm_branch(r3(cqk), r3(cv), r3(ig), r3(fg), ml_conv, ml_fbias)
    yd = _pool_call(r3(dx), _pack_pool(pool_w), pool_scale[None])
    x1, x1p, aff_t = _merge_call(
        x, r3(ya), o_list, l_list, hf, hb, r3(co), yd,
        w_in[:, n_small:].astype(BF16), b_in[None, n_small:], w_branch.astype(BF16), w_out.astype(BF16),
        ml_norm_g[:, None], ln1_g[None], ln1_b[None], jnp.transpose(w_router).astype(BF16), alpha)
    ffn = _expert_choice_ffn(x1p, aff_t, w_e1, w_e3, w_e2, layer)
    return x1.reshape(b * s, d), ffn.reshape(b * s, d)


def kernel(x, w_in, b_in, gm_ln_g, gm_ws, gm_bs, rel_bias, ml_conv, ml_fbias, ml_norm_g, pool_w,
           pool_scale, w_branch, w_out, ln1_g, ln1_b, w_router, w_e1, w_e3, w_e2, ln2_g, ln2_b):
    depth = w_in.shape[0]
    alpha = (2 * depth) ** 0.25
    bias_tiles = [_attn_bias_tile(rel_bias, window, dil) for window, dil in DIL_PATTERNS]
    b, s, d = x.shape
    pending = None
    for l in range(depth):
        x1, ffn = _layer(l, x, pending, alpha, bias_tiles, w_in[l], b_in[l], gm_ln_g[l], gm_ws[l], gm_bs[l],
                         ml_conv[l], ml_fbias[l], ml_norm_g[l], pool_w[l], pool_scale[l], w_branch[l],
                         w_out[l], ln1_g[l], ln1_b[l], w_router[l], w_e1, w_e3, w_e2)
        pending = (x1, ffn, ln2_g[l][None], ln2_b[l][None])
    return _resln_call(*pending, alpha).reshape(b, s, d)
```

```python
import functools
import math

import jax
import jax.numpy as jnp
import numpy as np
from jax import lax
from jax.experimental import pallas as pl
from jax.experimental.pallas import tpu as pltpu
from jax.experimental.pallas import tpu_sc as plsc

F32 = jnp.float32
BF16 = jnp.bfloat16

D_MODEL = 1024
MIX_W = 256
N_BRANCH = 4
GM_CHUNK = 128
GM_GROUPS = 4
ATT_HEADS = 4
ATT_HD = 64
DIL_PATTERNS = ((128, 1), (512, 4), (2048, 16))
ATT_BLOCK = 64
REL_BUCKETS = 32
REL_MAX_DIST = 1024
ML_HEADS = 4
ML_HD = 64
ML_CHUNK = 64
POOL_WINDOWS = (2, 4, 8, 16)
N_EXPERTS = 16
EXPERT_FF = 1024
EC_FACTOR = 2
LN_EPS = 1e-5
NEG_BIG = -1e30

V7X_VMEM_LIMIT = 56 * 1024 * 1024
LANES = 128
HALO = 8


def _cparams(n_grid, vmem=V7X_VMEM_LIMIT):
    return pltpu.CompilerParams(dimension_semantics=("arbitrary",) * n_grid,
                                vmem_limit_bytes=vmem)


def _pack_bf16_pair(lo, hi):
    lo_bits = lax.shift_right_logical(pltpu.bitcast(lo.astype(F32), jnp.int32), 16)
    return pltpu.bitcast(hi.astype(F32), jnp.int32) | lo_bits


def _unpack_bf16_pair(packed):
    lo = pltpu.bitcast(lax.shift_left(packed, 16), F32).astype(BF16)
    hi = pltpu.bitcast(packed & jnp.int32(-65536), F32).astype(BF16)
    return lo, hi


def _standardize(xf):
    mu = jnp.mean(xf, axis=-1, keepdims=True)
    var = jnp.mean(jnp.square(xf - mu), axis=-1, keepdims=True)
    return (xf - mu) * lax.rsqrt(var + LN_EPS)


TA = 512
A_COLS = 2560 + 2 * LANES


def _inproj_kernel(*refs, alpha):
    if alpha is None:
        x_ref, *refs = refs
        x = x_ref[...]
    else:
        n_grp = D_MODEL // GC
        x1_ref, y_refs, (g2_ref, b2_ref, *refs) = refs[0], refs[1:1 + n_grp], refs[1 + n_grp:]
        y = jnp.concatenate([r[...] for r in y_refs], axis=1)
        x = _standardize(alpha * x1_ref[...] + y) * g2_ref[...] + b2_ref[...]
        refs[-2][...] = x
        refs = refs[:-2] + refs[-1:]
    (w_ref, b_ref, lng_ref, wscat_ref, bsfull_ref,
     ya_ref, qkv1_ref, qkv4_ref, qkv16_ref, cqk_ref, cv_ref, co_ref, dx_ref, ig_ref, fg_ref, qkv_scr) = refs
    xb = x.astype(BF16)
    h = jnp.dot(xb, w_ref[...], preferred_element_type=F32) + b_ref[...]
    qkv1_ref[0, 0] = h[:, 512:1280].astype(BF16)
    for c in range(768 // LANES):
        qkv_scr[c] = h[:, 512 + c * LANES:512 + (c + 1) * LANES]
    for (_, dil), out_ref in zip(DIL_PATTERNS[1:], (qkv4_ref, qkv16_ref)):
        for r in range(dil):
            for c in range(768 // LANES):
                out_ref[0, r, :, c * LANES:(c + 1) * LANES] = (
                    qkv_scr[c, pl.ds(r, TA // dil, stride=dil), :].astype(BF16))
    cqk_ref[...] = h[:, 1280:1792]
    cv_ref[...] = h[:, 1792:2048].astype(BF16)
    co_ref[...] = h[:, 2048:2304]
    dx_ref[...] = h[:, 2304:2560]
    ig_ref[...] = h[:, 2560:2688]
    fg_ref[...] = h[:, 2688:2816]
    u = jax.nn.gelu(h[:, 0:256])
    v = jax.nn.gelu(h[:, 256:512])
    vn = _standardize(v) * lng_ref[...]
    lane_grp = lax.broadcasted_iota(jnp.int32, (GM_CHUNK, MIX_W), 1) // (MIX_W // GM_GROUPS)
    for c in range(TA // GM_CHUNK):
        vc = vn[c * GM_CHUNK:(c + 1) * GM_CHUNK]
        stacked = jnp.concatenate(
            [jnp.where(lane_grp == g, vc, 0.0).astype(BF16) for g in range(GM_GROUPS)], axis=0)
        mixed = jnp.dot(wscat_ref[...], stacked, preferred_element_type=F32) + bsfull_ref[...]
        ya_ref[c * GM_CHUNK:(c + 1) * GM_CHUNK, :] = (
            u[c * GM_CHUNK:(c + 1) * GM_CHUNK] * mixed).astype(BF16)


def _inproj_call(x_in, w_cat, b_cat, lng, wscat, bsfull, batch, alpha=None):
    fused = alpha is not None
    n = (x_in[0] if fused else x_in).shape[0]
    seq = n // batch
    tpb = seq // TA
    tok = lambda w: pl.BlockSpec((TA, w), lambda i: (i, 0))
    const = lambda s: pl.BlockSpec(s, lambda i: (0,) * len(s))
    regrouped = lambda dil: pl.BlockSpec((1, dil, TA // dil, 768), lambda i: (i // tpb, 0, i % tpb, 0))
    out_shape = (
        jax.ShapeDtypeStruct((n, 256), BF16),
    ) + tuple(jax.ShapeDtypeStruct((batch, dil, seq // dil, 768), BF16)
              for _, dil in DIL_PATTERNS) + (
        jax.ShapeDtypeStruct((n, 512), F32),
        jax.ShapeDtypeStruct((n, 256), BF16),
        jax.ShapeDtypeStruct((n, 256), F32),
        jax.ShapeDtypeStruct((n, 256), F32),
        jax.ShapeDtypeStruct((n, LANES), F32),
        jax.ShapeDtypeStruct((n, LANES), F32),
    )
    if fused:
        x1, ffn_groups, ln_g, ln_b = x_in
        x_args = (x1, *ffn_groups, ln_g, ln_b)
        x_specs = ([tok(D_MODEL)] + [tok(GC)] * len(ffn_groups) + [const((1, D_MODEL)), const((1, D_MODEL))])
    else:
        x_args, x_specs = (x_in,), [tok(D_MODEL)]
    out_specs = ((tok(256),) + tuple(regrouped(dil) for _, dil in DIL_PATTERNS)
                 + (tok(512), tok(256), tok(256), tok(256), tok(LANES), tok(LANES)))
    if fused:
        out_specs += (tok(D_MODEL),)
        out_shape += (jax.ShapeDtypeStruct((n, D_MODEL), F32),)
    return pl.pallas_call(
        functools.partial(_inproj_kernel, alpha=alpha), name="inproj_gmlp",
        grid=(n // TA,),
        in_specs=x_specs + [const((D_MODEL, A_COLS)), const((1, A_COLS)), const((1, MIX_W)),
                            const((GM_CHUNK, GM_GROUPS * GM_CHUNK)), const((GM_CHUNK, MIX_W))],
        out_specs=out_specs,
        out_shape=out_shape,
        scratch_shapes=[pltpu.VMEM((768 // LANES, TA, LANES), F32)],
        compiler_params=_cparams(1),
    )(*x_args, w_cat, b_cat, lng, wscat, bsfull)


def _pack_inproj_weights(w_in, b_in):
    pad = lambda a: jnp.pad(a, ((0, 0), (0, LANES - 8)))
    w_cat = jnp.concatenate([w_in[:, 0:2304], w_in[:, 2320:2576],
                             pad(w_in[:, 2304:2312]), pad(w_in[:, 2312:2320])], axis=1)
    b2 = b_in[None, :]
    b_cat = jnp.concatenate([b2[:, 0:2304], b2[:, 2320:2576],
                             pad(b2[:, 2304:2312]), pad(b2[:, 2312:2320])], axis=1)
    return w_cat.astype(BF16), b_cat


def _pack_gmlp(gm_ws, gm_bs):
    wscat = jnp.transpose(gm_ws, (1, 0, 2)).reshape(GM_CHUNK, GM_GROUPS * GM_CHUNK).astype(BF16)
    bsfull = jnp.repeat(jnp.transpose(gm_bs), MIX_W // GM_GROUPS, axis=1)
    return wscat, bsfull


def _halo_specs(t, width, n_tiles):
    r = t // HALO
    main = pl.BlockSpec((1, t, width), lambda b, i: (b, i, 0))
    prev = pl.BlockSpec((1, HALO, width), lambda b, i: (b, jnp.maximum(i * r - 1, 0), 0))
    nxt = pl.BlockSpec((1, HALO, width), lambda b, i: (b, jnp.minimum((i + 1) * r, n_tiles * r - 1), 0))
    return main, prev, nxt


def _fill_halo_scratch(buf, x_ref, p_ref, n_ref, t):
    i = pl.program_id(1)
    last = pl.num_programs(1) - 1
    buf[0:HALO, :] = jnp.where(i > 0, p_ref[0], 0.0)
    buf[HALO:HALO + t, :] = x_ref[0]
    buf[HALO + t:2 * HALO + t, :] = jnp.where(i < last, n_ref[0], 0.0)


TP = 512


def _pool_kernel(x_ref, p_ref, n_ref, w_ref, sc_ref, o_ref, buf):
    _fill_halo_scratch(buf, x_ref, p_ref, n_ref, TP)
    seq = pl.num_programs(1) * TP
    pos = pl.program_id(1) * TP + lax.broadcasted_iota(jnp.int32, (TP, 1), 0)
    lane_grp = lax.broadcasted_iota(jnp.int32, (TP, MIX_W), 1) // (MIX_W // len(POOL_WINDOWS))
    x0 = buf[HALO:HALO + TP, :]
    pooled = jnp.zeros((TP, MIX_W), F32)
    acc = None
    half_done = 0
    for gi, win in enumerate(POOL_WINDOWS):
        half = win // 2
        for o in list(range(-half, -half_done)) + list(range(half_done, half)):
            term = buf[HALO + o:HALO + o + TP, :]
            acc = term if acc is None else acc + term
        half_done = half
        cnt = (jnp.minimum(pos + half, seq) - jnp.maximum(pos - half, 0)).astype(F32)
        pooled = jnp.where(lane_grp == gi, acc / cnt - x0, pooled)
    mixed = jnp.dot(pooled.astype(BF16), w_ref[...], preferred_element_type=F32)
    o_ref[0] = (mixed * sc_ref[...]).astype(BF16)


def _pool_call(dx, w_block, scale):
    b, s, _ = dx.shape
    nt = s // TP
    main, prev, nxt = _halo_specs(TP, MIX_W, nt)
    return pl.pallas_call(
        _pool_kernel, name="pool_mixer",
        grid=(b, nt),
        in_specs=[main, prev, nxt,
                  pl.BlockSpec((MIX_W, MIX_W), lambda b, i: (0, 0)),
                  pl.BlockSpec((1, MIX_W), lambda b, i: (0, 0))],
        out_specs=pl.BlockSpec((1, TP, MIX_W), lambda b, i: (b, i, 0)),
        out_shape=jax.ShapeDtypeStruct((b, s, MIX_W), BF16),
        scratch_shapes=[pltpu.VMEM((TP + 2 * HALO, MIX_W), F32)],
        compiler_params=_cparams(2),
    )(dx, dx, dx, w_block, scale)


TQ = 128
TQS = 512
TKEYS = TQ + 2 * ATT_BLOCK


def _attn_kernel(q_ref, kp_ref, km_ref, kn_ref, vp_ref, vm_ref, vn_ref, bias_ref, o_ref, lse_ref):
    i = pl.program_id(2)
    seq = pl.num_programs(2) * TQS
    q = q_ref[0, 0] * ATT_HD ** -0.5
    k = jnp.concatenate([kp_ref[0, 0], km_ref[0, 0], kn_ref[0, 0]], axis=0)
    v = jnp.concatenate([vp_ref[0, 0], vm_ref[0, 0], vn_ref[0, 0]], axis=0)
    lane = lax.broadcasted_iota(jnp.int32, (TQ, LANES), 1)
    lane_half = lax.broadcasted_iota(jnp.int32, (1, LANES), 1) // ATT_HD
    keep = [jnp.where(lane_half == hh, 1.0, 0.0).astype(BF16) for hh in range(2)]
    for j in range(TQS // TQ):
        kpos = i * TQS + j * TQ - ATT_BLOCK + lax.broadcasted_iota(jnp.int32, (1, TKEYS), 1)
        kvalid = (kpos >= 0) & (kpos < seq)
        qrows = slice(j * TQ, (j + 1) * TQ)
        krows = slice(j * TQ, j * TQ + TKEYS)
        lse_tile = jnp.zeros((TQ, LANES), F32)
        for pair in range(ATT_HEADS // 2):
            grp = slice(pair * LANES, (pair + 1) * LANES)
            q_pair, k_pair, v_pair = q[qrows, grp], k[krows, grp], v[krows, grp]
            o_pair = jnp.zeros((TQ, LANES), F32)
            for hh in range(2):
                h = 2 * pair + hh
                logits = lax.dot_general(q_pair * keep[hh], k_pair, (((1,), (1,)), ((), ())),
                                         preferred_element_type=F32) + bias_ref[h]
                logits = jnp.where(kvalid, logits, NEG_BIG)
                m = jnp.max(logits, axis=-1, keepdims=True)
                p = jnp.exp(logits - m)
                ssum = jnp.sum(p, axis=-1, keepdims=True)
                o = jnp.dot(p.astype(BF16), v_pair, preferred_element_type=F32) / ssum
                o_pair = jnp.where(lane_half == hh, o, o_pair)
                lse_tile = jnp.where(lane == h, m + jnp.log(ssum), lse_tile)
            o_ref[0, 0, qrows, grp] = o_pair
        lse_ref[0, 0, qrows, :] = lse_tile


def _attn_call(qkv, bias):
    b, dil, l, _ = qkv.shape
    nt = l // TQS
    r64 = TQS // ATT_BLOCK
    main = lambda c: pl.BlockSpec((1, 1, TQS, MIX_W), lambda b, r, i: (b, r, i, c))
    prev = lambda c: pl.BlockSpec((1, 1, ATT_BLOCK, MIX_W),
                                  lambda b, r, i: (b, r, jnp.maximum(i * r64 - 1, 0), c))
    nxt = lambda c: pl.BlockSpec((1, 1, ATT_BLOCK, MIX_W),
                                 lambda b, r, i: (b, r, jnp.minimum((i + 1) * r64, nt * r64 - 1), c))
    return pl.pallas_call(
        _attn_kernel, name="band_attention",
        grid=(b, dil, nt),
        in_specs=[main(0), prev(1), main(1), nxt(1), prev(2), main(2), nxt(2),
                  pl.BlockSpec((ATT_HEADS, TQ, TKEYS), lambda b, r, i: (0, 0, 0))],
        out_specs=(pl.BlockSpec((1, 1, TQS, MIX_W), lambda b, r, i: (b, r, i, 0)),
                   pl.BlockSpec((1, 1, TQS, LANES), lambda b, r, i: (b, r, i, 0))),
        out_shape=(jax.ShapeDtypeStruct((b, dil, l, MIX_W), F32),
                   jax.ShapeDtypeStruct((b, dil, l, LANES), F32)),
        compiler_params=_cparams(3),
    )(qkv, qkv, qkv, qkv, qkv, qkv, qkv, bias)


def _t5_bucket_static(rel):
    half = REL_BUCKETS // 2
    max_exact = half // 2
    ret = np.where(rel > 0, half, 0)
    n = np.abs(rel)
    nf = np.maximum(n, 1).astype(np.float32)
    large = max_exact + (np.log(nf / np.float32(max_exact)) / np.float32(math.log(REL_MAX_DIST / max_exact))
                         * np.float32(half - max_exact)).astype(np.int32)
    large = np.minimum(large, half - 1)
    return ret + np.where(n < max_exact, n, large)


def _attn_bias_tile(rel_bias, window, dil):
    side = (window // 2) // dil
    rel = np.arange(TKEYS)[None, :] - ATT_BLOCK - np.arange(TQ)[:, None]
    onehot = jax.nn.one_hot(jnp.asarray(_t5_bucket_static(dil * rel), jnp.int32), REL_BUCKETS, dtype=F32)
    bias = jnp.einsum('qkr,rh->hqk', onehot, rel_bias, precision=lax.Precision.HIGHEST)
    return jnp.where(jnp.asarray(np.abs(rel) <= side)[None], bias, NEG_BIG)


TM = 512
VT_ROWS = ML_HD + 16


def _mlprep_kernel(x_ref, p_ref, n_ref, v_ref, w_ref, qt_out, k_out, vt_out, buf):
    _fill_halo_scratch(buf, x_ref, p_ref, n_ref, TM)
    conv = (buf[HALO - 1:HALO - 1 + TM, :] * w_ref[0:1, :] + buf[HALO:HALO + TM, :] * w_ref[1:2, :]
            + buf[HALO + 1:HALO + 1 + TM, :] * w_ref[2:3, :])
    qk = jax.nn.silu(conv)
    qt = jnp.transpose(qk[:, :MIX_W])
    vt = jnp.transpose(v_ref[0].astype(F32))
    ones_rows = jnp.where(lax.broadcasted_iota(jnp.int32, (VT_ROWS - ML_HD, ML_CHUNK), 0) == 0, 1.0, 0.0)
    for h in range(ML_HEADS):
        sl = slice(h * ML_HD, (h + 1) * ML_HD)
        k_out[0, h] = (qk[:, MIX_W + h * ML_HD:MIX_W + (h + 1) * ML_HD] * ML_HD ** -0.5).astype(BF16)
        for c in range(TM // ML_CHUNK):
            cl = slice(c * ML_CHUNK, (c + 1) * ML_CHUNK)
            qt_out[0, h, c] = qt[sl, cl].astype(BF16)
            vt_out[0, h, c] = jnp.concatenate([vt[sl, cl], ones_rows], axis=0).astype(BF16)


def _mlprep_call(cqk, cv, conv_w):
    b, s, _ = cqk.shape
    nt = s // TM
    nc = s // ML_CHUNK
    cpt = TM // ML_CHUNK
    main, prev, nxt = _halo_specs(TM, 2 * MIX_W, nt)
    return pl.pallas_call(
        _mlprep_kernel, name="mlstm_prep",
        grid=(b, nt),
        in_specs=[main, prev, nxt,
                  pl.BlockSpec((1, TM, MIX_W), lambda b, i: (b, i, 0)),
                  pl.BlockSpec((3, 2 * MIX_W), lambda b, i: (0, 0))],
        out_specs=(pl.BlockSpec((1, ML_HEADS, cpt, ML_HD, ML_CHUNK), lambda b, i: (b, 0, i, 0, 0)),
                   pl.BlockSpec((1, ML_HEADS, TM, ML_HD), lambda b, i: (b, 0, i, 0)),
                   pl.BlockSpec((1, ML_HEADS, cpt, VT_ROWS, ML_CHUNK), lambda b, i: (b, 0, i, 0, 0))),
        out_shape=(jax.ShapeDtypeStruct((b, ML_HEADS, nc, ML_HD, ML_CHUNK), BF16),
                   jax.ShapeDtypeStruct((b, ML_HEADS, s, ML_HD), BF16),
                   jax.ShapeDtypeStruct((b, ML_HEADS, nc, VT_ROWS, ML_CHUNK), BF16)),
        scratch_shapes=[pltpu.VMEM((TM + 2 * HALO, 2 * MIX_W), F32)],
        compiler_params=_cparams(2),
    )(cqk, cqk, cqk, cv, conv_w)


GP_BLK = 512


def _chunk_scan(x, fwd_lane, t_in_chunk, op, ident):
    n = x.shape[0]
    shift = 1
    while shift < ML_CHUNK:
        down = pltpu.roll(x, shift, 0)
        up = pltpu.roll(x, n - shift, 0)
        nb = jnp.where(fwd_lane,
                       jnp.where(t_in_chunk >= shift, down, ident),
                       jnp.where(t_in_chunk < ML_CHUNK - shift, up, ident))
        x = op(x, nb)
        shift *= 2
    return x


def _gatescan_kernel(ig_ref, fg_ref, fb_ref, b_ref, a_ref, cm_ref, g_ref, amax_ref):
    lane = lax.broadcasted_iota(jnp.int32, (1, LANES), 1)
    fwd_lane = lane < ML_HEADS
    t_in_chunk = lax.broadcasted_iota(jnp.int32, (GP_BLK, 1), 0) % ML_CHUNK
    z = fg_ref[0] + fb_ref[...]
    lf = jnp.minimum(z, 0.0) - jnp.log1p(jnp.exp(-jnp.abs(z)))
    b = _chunk_scan(lf, fwd_lane, t_in_chunk, jnp.add, 0.0)
    a = ig_ref[0] - b
    b_ref[0] = b
    a_ref[0] = a
    cm_ref[0] = _chunk_scan(a, fwd_lane, t_in_chunk, jnp.maximum, -jnp.inf)
    cpb = GP_BLK // ML_CHUNK
    last = pl.ds(ML_CHUNK - 1, cpb, stride=ML_CHUNK)
    first = pl.ds(0, cpb, stride=ML_CHUNK)
    g_ref[0] = jnp.where(fwd_lane, b_ref[0, last, :], b_ref[0, first, :])
    amax_ref[0] = jnp.where(fwd_lane, cm_ref[0, last, :], cm_ref[0, first, :])


def _gateout_kernel(b_ref, a_ref, cm_ref, g_ref, amax_ref, ws_ref, m_ref, iw_ref, en_ref, dec_ref,
                    mch_s, mlast_s):
    nc = g_ref.shape[1]
    j = pl.program_id(1)
    fwd_lane = lax.broadcasted_iota(jnp.int32, (1, LANES), 1) < ML_HEADS

    @pl.when(j == 0)
    def _():
        def m_step(i, carry):
            mf, mb = carry
            cf = pl.ds(i, 1)
            cb = pl.ds(nc - 1 - i, 1)
            mch_s[cf, :] = jnp.where(fwd_lane, mf, mch_s[cf, :])
            mch_s[cb, :] = jnp.where(fwd_lane, mch_s[cb, :], mb)
            mf = g_ref[0, cf, :] + jnp.maximum(mf, amax_ref[0, cf, :])
            mb = g_ref[0, cb, :] + jnp.maximum(mb, amax_ref[0, cb, :])
            return mf, mb

        mch_s[...] = jnp.zeros((nc, LANES), F32)
        zero = jnp.zeros((1, LANES), F32)
        lax.fori_loop(0, nc, m_step, (zero, zero))
        mlast = jnp.maximum(amax_ref[0], mch_s[...])
        mlast_s[...] = mlast
        dec_ref[0] = jnp.exp(mch_s[...] - mlast)

    cpb = GP_BLK // ML_CHUNK
    crow = pl.ds(pl.multiple_of(j * cpb, cpb), cpb)
    expand = lambda t: jnp.broadcast_to(t[:, None, :], (cpb, ML_CHUNK, LANES)).reshape(GP_BLK, LANES)
    m_tok = expand(mch_s[crow, :])
    mlast_tok = expand(mlast_s[crow, :])
    mt = jnp.maximum(cm_ref[0], m_tok)
    m_ref[0] = mt
    iw_ref[0] = jnp.exp(m_tok - mt)
    en_ref[0] = jnp.exp(-(b_ref[0] + mt))
    ws_ref[0] = jnp.exp(a_ref[0] - mlast_tok)


def _gateprep_call(ig, fg, fbias_row):
    b, s, _ = ig.shape
    nc = s // ML_CHUNK
    cpb = GP_BLK // ML_CHUNK
    tok = pl.BlockSpec((1, GP_BLK, LANES), lambda i, j: (i, j, 0))
    tok_shape = jax.ShapeDtypeStruct((b, s, LANES), F32)
    chunk_shape = jax.ShapeDtypeStruct((b, nc, LANES), F32)
    chunk_tile = pl.BlockSpec((1, cpb, LANES), lambda i, j: (i, j, 0))
    chunk_all = pl.BlockSpec((1, nc, LANES), lambda i, j: (i, 0, 0))
    bcum, a, cm, g, amax = pl.pallas_call(
        _gatescan_kernel, name="mlstm_gate_scan",
        grid=(b, s // GP_BLK),
        in_specs=[tok, tok, pl.BlockSpec((1, LANES), lambda i, j: (0, 0))],
        out_specs=(tok, tok, tok, chunk_tile, chunk_tile),
        out_shape=(tok_shape,) * 3 + (chunk_shape,) * 2,
        compiler_params=_cparams(2),
    )(ig, fg, fbias_row)
    ws, m_col, iw_col, en_col, decay = pl.pallas_call(
        _gateout_kernel, name="mlstm_gate_out",
        grid=(b, s // GP_BLK),
        in_specs=[tok, tok, tok, chunk_all, chunk_all],
        out_specs=(tok, tok, tok, tok, chunk_all),
        out_shape=(tok_shape,) * 4 + (chunk_shape,),
        scratch_shapes=[pltpu.VMEM((nc, LANES), F32)] * 2,
        compiler_params=_cparams(2),
    )(bcum, a, cm, g, amax)
    return a, ws, m_col, iw_col, en_col, decay


TE = 1024


def _mlstm_kernel(*refs):
    fwd, bwd, (hf_ref, hb_ref, state) = refs[:9], refs[9:18], refs[18:]
    i = pl.program_id(1)

    @pl.when(i == 0)
    def _():
        state[...] = jnp.zeros(state.shape, F32)

    cpt = TE // ML_CHUNK
    s_idx = lax.broadcasted_iota(jnp.int32, (ML_CHUNK, ML_CHUNK), 0)
    t_idx = lax.broadcasted_iota(jnp.int32, (ML_CHUNK, ML_CHUNK), 1)

    def chunk_body(c, carry):
        jobs = []
        for d, (qt_r, k_r, vt_r, a_r, m_r, iw_r, en_r, ws_r, dec_r), out_r in ((0, fwd, hf_ref), (1, bwd, hb_ref)):
            cc = c if d == 0 else cpt - 1 - c
            rows = pl.ds(pl.multiple_of(cc * ML_CHUNK, ML_CHUNK), ML_CHUNK)
            crow = pl.ds(cc, 1)
            for h in range(ML_HEADS):
                ch = d * ML_HEADS + h
                row = lambda r: r[0, ch, crow, :]
                jobs.append(dict(
                    ch=ch, tri=(s_idx >= t_idx) if d else (s_idx <= t_idx),
                    k=k_r[0, h, rows, :], qt=qt_r[0, h, cc], vt=vt_r[0, h, cc], a=a_r[0, rows, ch:ch + 1],
                    m=row(m_r), iw=row(iw_r), en=row(en_r), ws=row(ws_r), dec=row(dec_r),
                    out=(out_r, cc, h)))
        for j in jobs:
            j["cst"] = state[j["ch"]]
            j["st"] = jnp.dot(j["k"], j["qt"], preferred_element_type=F32)
            j["inter"] = jnp.dot(j["cst"].astype(BF16), j["qt"], preferred_element_type=F32)
            j["upd"] = jnp.dot((j["vt"].astype(F32) * j["ws"]).astype(BF16), j["k"],
                               preferred_element_type=F32)
        for j in jobs:
            j["swt"] = j["st"] * jnp.exp(jnp.where(j["tri"], j["a"] - j["m"], NEG_BIG))
            j["intra"] = jnp.dot(j["vt"], j["swt"].astype(BF16), preferred_element_type=F32)
        for j in jobs:
            den = jnp.sum(j["swt"], axis=0, keepdims=True) + j["iw"] * j["inter"][ML_HD:ML_HD + 1]
            tot = j["intra"][:ML_HD] + j["iw"] * j["inter"][:ML_HD]
            out_r, cc, h = j["out"]
            out_r[0, cc, h * ML_HD:(h + 1) * ML_HD, :] = tot / jnp.maximum(jnp.abs(den), j["en"])
            state[j["ch"]] = j["dec"] * j["cst"] + j["upd"]
        return carry

    lax.fori_loop(0, cpt, chunk_body, 0)


def _mlstm_call(qt, k, vt, a_col, m_row, iw_row, en_row, ws_row, dec_row):
    b, _, s, _ = k.shape
    nt = s // TE
    nc = s // ML_CHUNK
    cpt = TE // ML_CHUNK

    def specs(rev):
        ti = (lambda i: nt - 1 - i) if rev else (lambda i: i)
        row = pl.BlockSpec((1, 2 * ML_HEADS, cpt, ML_CHUNK), lambda b, i: (b, 0, ti(i), 0))
        return [
            pl.BlockSpec((1, ML_HEADS, cpt, ML_HD, ML_CHUNK), lambda b, i: (b, 0, ti(i), 0, 0)),
            pl.BlockSpec((1, ML_HEADS, TE, ML_HD), lambda b, i: (b, 0, ti(i), 0)),
            pl.BlockSpec((1, ML_HEADS, cpt, VT_ROWS, ML_CHUNK), lambda b, i: (b, 0, ti(i), 0, 0)),
            pl.BlockSpec((1, TE, LANES), lambda b, i: (b, ti(i), 0)),
            row, row, row, row, row]

    args = [qt, k, vt, a_col, m_row, iw_row, en_row, ws_row, dec_row]
    out_f = pl.BlockSpec((1, cpt, MIX_W, ML_CHUNK), lambda b, i: (b, i, 0, 0))
    out_b = pl.BlockSpec((1, cpt, MIX_W, ML_CHUNK), lambda b, i: (b, nt - 1 - i, 0, 0))
    return pl.pallas_call(
        _mlstm_kernel, name="mlstm_scan",
        grid=(b, nt),
        in_specs=specs(False) + specs(True),
        out_specs=(out_f, out_b),
        out_shape=(jax.ShapeDtypeStruct((b, nc, MIX_W, ML_CHUNK), F32),) * 2,
        scratch_shapes=[pltpu.VMEM((2 * ML_HEADS, VT_ROWS, ML_HD), F32)],
        compiler_params=_cparams(2),
    )(*args, *args)


def _mlstm_branch(cqk, cv, ig, fg, conv_w, fbias):
    b, s, _ = cqk.shape
    nc = s // ML_CHUNK
    qt, k, vt = _mlprep_call(cqk, cv, conv_w)
    fb_row = jnp.pad(fbias.reshape(1, 2 * ML_HEADS), ((0, 0), (0, LANES - 2 * ML_HEADS)))
    a, ws, m_col, iw_col, en_col, decay = _gateprep_call(ig, fg, fb_row)
    rows = lambda t: jnp.transpose(t[..., :2 * ML_HEADS], (0, 2, 1)).reshape(b, 2 * ML_HEADS, nc, ML_CHUNK)
    dec_row = jnp.broadcast_to(jnp.transpose(decay[..., :2 * ML_HEADS], (0, 2, 1))[..., None],
                               (b, 2 * ML_HEADS, nc, ML_CHUNK))
    return _mlstm_call(qt, k, vt, a, rows(m_col), rows(iw_col), rows(en_col), rows(ws), dec_row)


TF = 512


def _merge_kernel(x_ref, ya_ref, o1_ref, o2_ref, o3_ref, l1_ref, l2_ref, l3_ref, hf_ref, hb_ref,
                  co_ref, yd_ref, wg_ref, bg_ref, wbr_ref, wout_ref, mng_ref, lng_ref, lnb_ref,
                  wr_ref, x1_ref, x1p_ref, aff_ref, o_scr, l_scr, *, alpha):
    x = x_ref[0]
    xb = x.astype(BF16)

    def natural_order(src_ref, scr):
        dil, width = src_ref.shape[1], src_ref.shape[3]
        if dil == 1:
            return src_ref[0, 0]
        for r in range(dil):
            for c in range(width // LANES):
                scr[c, pl.ds(r, TF // dil, stride=dil), :] = src_ref[0, r, :, c * LANES:(c + 1) * LANES]
        return jnp.concatenate([scr[c] for c in range(width // LANES)], axis=1)

    lane_head = lax.broadcasted_iota(jnp.int32, (TF, MIX_W), 1) // ML_HD
    l1, l2, l3 = [natural_order(r, l_scr.at[p]) for p, r in enumerate((l1_ref, l2_ref, l3_ref))]
    o1, o2, o3 = [natural_order(r, o_scr.at[p]) for p, r in enumerate((o1_ref, o2_ref, o3_ref))]
    lm = jnp.maximum(jnp.maximum(l1, l2), l3)
    e1, e2, e3 = jnp.exp(l1 - lm), jnp.exp(l2 - lm), jnp.exp(l3 - lm)
    inv = 1.0 / (e1 + e2 + e3)

    def per_head(w):
        out = jnp.zeros((TF, MIX_W), F32)
        for h in range(ATT_HEADS):
            out = jnp.where(lane_head == h, w[:, h:h + 1], out)
        return out

    y_b = per_head(e1 * inv) * o1 + per_head(e2 * inv) * o2 + per_head(e3 * inv) * o3
    hsum_t = jnp.concatenate([hf_ref[0, c] + hb_ref[0, c] for c in range(TF // ML_CHUNK)], axis=1)
    per_head_rows = hsum_t.reshape(ML_HEADS, ML_HD, TF)
    mu = jnp.mean(per_head_rows, axis=1, keepdims=True)
    cen = per_head_rows - mu
    var = jnp.mean(cen * cen, axis=1, keepdims=True)
    hn_t = (cen * lax.rsqrt(var + LN_EPS)).reshape(MIX_W, TF)
    y_c_t = (jax.nn.sigmoid(jnp.transpose(co_ref[0])) * (hn_t * mng_ref[...])).astype(BF16)
    ys = (ya_ref[0], y_b.astype(BF16), None, yd_ref[0])
    merged = jnp.zeros((TF, D_MODEL), F32)
    for n in range(N_BRANCH):
        cols = slice(n * D_MODEL, (n + 1) * D_MODEL)
        gate = jax.nn.sigmoid(jnp.dot(xb, wg_ref[:, cols], preferred_element_type=F32) + bg_ref[:, cols])
        if ys[n] is None:
            proj = lax.dot_general(y_c_t, wbr_ref[n], (((0,), (0,)), ((), ())), preferred_element_type=F32)
        else:
            proj = jnp.dot(ys[n], wbr_ref[n], preferred_element_type=F32)
        merged = merged + gate * proj
    mix = jnp.dot(merged.astype(BF16), wout_ref[...], preferred_element_type=F32)
    x1 = _standardize(alpha * x + mix) * lng_ref[...] + lnb_ref[...]
    x1_ref[0] = x1
    x1b = x1.astype(BF16)
    x1p_ref[0] = _pack_bf16_pair(x1b[:, :D_MODEL // 2], x1b[:, D_MODEL // 2:])
    logits = lax.dot_general(wr_ref[...], x1b, (((1,), (1,)), ((), ())),
                             preferred_element_type=F32)
    ex = jnp.exp(logits - jnp.max(logits, axis=0, keepdims=True))
    aff_ref[0] = ex / jnp.sum(ex, axis=0, keepdims=True)


def _merge_call(x, ya, o_list, l_list, hf, hb, co, yd, wg, bg, wbr, wout, mng, lng, lnb, wr_t, alpha):
    b, s, _ = x.shape
    tok = lambda w: pl.BlockSpec((1, TF, w), lambda b, i: (b, i, 0))
    grouped = lambda dil, w: pl.BlockSpec((1, dil, TF // dil, w), lambda b, i: (b, 0, i, 0))
    chunked = pl.BlockSpec((1, TF // ML_CHUNK, MIX_W, ML_CHUNK), lambda b, i: (b, i, 0, 0))
    const = lambda shp: pl.BlockSpec(shp, lambda b, i: (0,) * len(shp))
    return pl.pallas_call(
        functools.partial(_merge_kernel, alpha=alpha), name="merge_ln_router",
        grid=(b, s // TF),
        in_specs=[tok(D_MODEL), tok(MIX_W)] + [grouped(dil, MIX_W) for _, dil in DIL_PATTERNS]
                 + [grouped(dil, LANES) for _, dil in DIL_PATTERNS]
                 + [chunked, chunked, tok(MIX_W), tok(MIX_W)]
                 + [const((D_MODEL, N_BRANCH * D_MODEL)), const((1, N_BRANCH * D_MODEL)),
                    const((N_BRANCH, MIX_W, D_MODEL)), const((D_MODEL, D_MODEL)), const((MIX_W, 1)),
                    const((1, D_MODEL)), const((1, D_MODEL)), const((N_EXPERTS, D_MODEL))],
        out_specs=(tok(D_MODEL), tok(D_MODEL // 2), pl.BlockSpec((1, N_EXPERTS, TF), lambda b, i: (b, 0, i))),
        out_shape=(jax.ShapeDtypeStruct((b, s, D_MODEL), F32),
                   jax.ShapeDtypeStruct((b, s, D_MODEL // 2), jnp.int32),
                   jax.ShapeDtypeStruct((b, N_EXPERTS, s), F32)),
        scratch_shapes=[pltpu.VMEM((len(DIL_PATTERNS), MIX_W // LANES, TF, LANES), F32),
                        pltpu.VMEM((len(DIL_PATTERNS), 1, TF, LANES), F32)],
        compiler_params=_cparams(2),
    )(x, ya, *o_list, *l_list, hf, hb, co, yd, wg, bg, wbr, wout, mng, lng, lnb, wr_t)


TT = 256


def _select_kernel(aff_ref, slot_ref, *, cap):
    s = aff_ref.shape[2]
    bits = pltpu.bitcast(aff_ref[0], jnp.int32)

    def bit_step(i, thr):
        cand = thr | jnp.left_shift(jnp.int32(1), 30 - i)
        cnt = jnp.sum((bits >= cand).astype(jnp.int32), axis=1, keepdims=True)
        return jnp.where(cnt >= cap, cand, thr)

    thr = lax.fori_loop(0, 31, bit_step, jnp.zeros((N_EXPERTS, 1), jnp.int32))
    gt = bits > thr
    eq = bits == thr
    need = (cap - jnp.sum(gt.astype(jnp.int32), axis=1, keepdims=True)).astype(F32)
    upper = (lax.broadcasted_iota(jnp.int32, (TT, TT), 0)
             <= lax.broadcasted_iota(jnp.int32, (TT, TT), 1)).astype(BF16)
    eq_before = jnp.zeros((N_EXPERTS, 1), F32)
    sel_before = jnp.zeros((N_EXPERTS, 1), F32)
    for j in range(s // TT):
        cols = slice(j * TT, (j + 1) * TT)
        eq_j = eq[:, cols]
        eq_incl = eq_before + jnp.dot(eq_j.astype(BF16), upper, preferred_element_type=F32)
        sel_j = gt[:, cols] | (eq_j & (eq_incl <= need))
        sel_f = sel_j.astype(F32)
        sel_incl = sel_before + jnp.dot(sel_f.astype(BF16), upper, preferred_element_type=F32)
        slot_ref[0, :, cols] = jnp.where(sel_j, sel_incl - 1.0, -1.0).astype(jnp.int32)
        eq_before = eq_incl[:, TT - 1:TT]
        sel_before = sel_incl[:, TT - 1:TT]


def _select_call(aff_t, cap):
    b, e, s = aff_t.shape
    return pl.pallas_call(
        functools.partial(_select_kernel, cap=cap), name="expert_choice_select",
        grid=(b,),
        in_specs=[pl.BlockSpec((1, e, s), lambda i: (i, 0, 0))],
        out_specs=pl.BlockSpec((1, e, s), lambda i: (i, 0, 0)),
        out_shape=jax.ShapeDtypeStruct((b, e, s), jnp.int32),
        compiler_params=_cparams(1),
    )(aff_t)


SC_LANES = 16
SC_ROWS = 64
SC_IDX = 128
SC_SLAB = 128
SC_ZROWS = 64
CF = 1024
GC = 256


def _sc_dispatch_call(x_flat, slot2, aff2, seq, cap):
    n_pairs = slot2.shape[0]
    d = x_flat.shape[1]
    info = plsc.get_sparse_core_info()
    n_workers = info.num_cores * info.num_subcores
    assert n_pairs % n_workers == 0 and seq % SC_LANES == 0 and cap % (2 * SC_ROWS) == 0
    pairs_per_worker = n_pairs // n_workers
    mesh = plsc.VectorSubcoreMesh(core_axis_name="c", subcore_axis_name="s")

    @functools.partial(
        pl.kernel, mesh=mesh, name="expert_dispatch_sc",
        compiler_params=pltpu.CompilerParams(needs_layout_passes=False),
        out_type=(jax.ShapeDtypeStruct((n_pairs * cap, d), x_flat.dtype),
                  jax.ShapeDtypeStruct((n_pairs, cap), jnp.int32),
                  jax.ShapeDtypeStruct((n_pairs, cap), F32)),
        scratch_types=[pltpu.VMEM((seq,), jnp.int32), pltpu.VMEM((seq,), F32),
                       pltpu.VMEM((cap,), jnp.int32), pltpu.VMEM((cap,), jnp.int32),
                       pltpu.VMEM((cap,), F32),
                       pltpu.VMEM((2, SC_ROWS, d), x_flat.dtype), pltpu.SemaphoreType.DMA((2,))])
    def dispatch(x_hbm, slot_hbm, aff_hbm, xs_hbm, tok_hbm, gate_hbm,
                 slot_v, aff_v, idx_v, tok_v, gate_v, rows_v, sem):
        worker = lax.axis_index("s") * info.num_cores + lax.axis_index("c")
        lane = lax.iota(jnp.int32, SC_LANES)

        def gather(c0, buf):
            return pltpu.make_async_copy(x_hbm.at[idx_v.at[pl.ds(c0, SC_ROWS)]], rows_v.at[buf], sem.at[buf])
        for k in range(pairs_per_worker):
            pair = worker * pairs_per_worker + k
            row0 = (pair // N_EXPERTS) * seq
            pltpu.sync_copy(slot_hbm.at[pair], slot_v)
            pltpu.sync_copy(aff_hbm.at[pair], aff_v)

            @pl.loop(0, seq, step=SC_LANES)
            def _(t0):
                sv = slot_v[pl.ds(t0, SC_LANES)]
                picked = sv >= 0
                plsc.store_scatter(tok_v, [sv], t0 + lane, mask=picked)
                plsc.store_scatter(idx_v, [sv], row0 + t0 + lane, mask=picked)
                plsc.store_scatter(gate_v, [sv], aff_v[pl.ds(t0, SC_LANES)], mask=picked)

            pltpu.sync_copy(tok_v, tok_hbm.at[pair])
            pltpu.sync_copy(gate_v, gate_hbm.at[pair])

            gather(0, 0).start()

            @pl.loop(0, cap, step=2 * SC_ROWS)
            def _(c0):
                gather(c0 + SC_ROWS, 1).start()
                gather(c0, 0).wait()
                pltpu.sync_copy(rows_v.at[0], xs_hbm.at[pl.ds(pair * cap + c0, SC_ROWS)])

                @pl.when(c0 + 2 * SC_ROWS < cap)
                def _():
                    gather(c0 + 2 * SC_ROWS, 0).start()

                gather(c0 + SC_ROWS, 1).wait()
                pltpu.sync_copy(rows_v.at[1], xs_hbm.at[pl.ds(pair * cap + c0 + SC_ROWS, SC_ROWS)])

    return dispatch(x_flat, slot2, aff2)


def _expert_up_kernel(xs_ref, w1_ref, w3_ref, hid_ref, w1_bf, w3_bf):
    @pl.when((pl.program_id(1) == 0) & (pl.program_id(2) == 0))
    def _():
        w1_bf[...] = w1_ref[0, 0].astype(BF16)
        w3_bf[...] = w3_ref[0, 0].astype(BF16)

    xs = jnp.concatenate(_unpack_bf16_pair(xs_ref[0, 0]), axis=1)
    hid_ref[0, 0] = (jax.nn.silu(jnp.dot(xs, w1_bf[...], preferred_element_type=F32))
                     * jnp.dot(xs, w3_bf[...], preferred_element_type=F32)).astype(BF16)


def _expert_down_kernel(hid_ref, g_ref, w2_ref, ye_ref):
    ye_ref[0, 0] = jnp.dot(hid_ref[0, 0], w2_ref[0, 0].astype(BF16),
                           preferred_element_type=F32) * g_ref[0, 0]


def _expert_up_call(xs4, w1, w3, layer):
    b, e, cap, half = xs4.shape
    d, ff = w1.shape[2], w1.shape[3]
    assert d == 2 * half
    rows = lambda w: pl.BlockSpec((1, 1, CF, w), lambda e, b, j: (b, e, j, 0))
    wspec = pl.BlockSpec((1, 1, d, ff), lambda e, b, j: (layer, e, 0, 0))
    return pl.pallas_call(
        _expert_up_kernel, name="expert_ffn_up",
        grid=(e, b, cap // CF),
        in_specs=[rows(half), wspec, wspec],
        out_specs=rows(ff),
        out_shape=jax.ShapeDtypeStruct((b, e, cap, ff), BF16),
        scratch_shapes=[pltpu.VMEM((d, ff), BF16), pltpu.VMEM((d, ff), BF16)],
        compiler_params=_cparams(3),
    )(xs4, w1, w3)


def _expert_down_call(hid4, gate4, w2, layer, group):
    b, e, cap, ff = hid4.shape
    rows = lambda w: pl.BlockSpec((1, 1, CF, w), lambda e, b, j: (b, e, j, 0))
    return pl.pallas_call(
        _expert_down_kernel, name="expert_ffn_down",
        grid=(e, b, cap // CF),
        in_specs=[rows(ff), rows(1), pl.BlockSpec((1, 1, ff, GC), lambda e, b, j: (layer, e, 0, group))],
        out_specs=rows(GC),
        out_shape=jax.ShapeDtypeStruct((b, e, cap, GC), F32),
        compiler_params=_cparams(3),
    )(hid4, gate4, w2)


def _sc_combine_call(ye_flat, tok3, seq):
    n_pairs, n_chunks, _ = tok3.shape
    cap = n_chunks * SC_IDX
    d = ye_flat.shape[1]
    nb = n_pairs // N_EXPERTS
    info = plsc.get_sparse_core_info()
    assert info.num_subcores == N_EXPERTS and nb % info.num_cores == 0 and n_chunks % 2 == 0
    assert seq % (info.num_subcores * SC_ZROWS) == 0 and d % SC_SLAB == 0
    batches_per_core = nb // info.num_cores
    own_rows = seq // info.num_subcores
    mesh = plsc.VectorSubcoreMesh(core_axis_name="c", subcore_axis_name="s")

    @functools.partial(
        pl.kernel, mesh=mesh, name="expert_combine_sc",
        compiler_params=pltpu.CompilerParams(needs_layout_passes=False),
        out_type=jax.ShapeDtypeStruct((nb * seq, d), F32),
        scratch_types=[pltpu.VMEM_SHARED((seq, SC_SLAB), F32),
                       pltpu.VMEM((n_chunks, SC_IDX), jnp.int32),
                       pltpu.VMEM((2, SC_IDX, SC_SLAB), F32),
                       pltpu.VMEM((SC_ZROWS, SC_SLAB), F32),
                       pltpu.SemaphoreType.DMA((2,))])
    def combine(ye_hbm, tok_hbm, out_hbm, acc_sh, tok_v, rows_v, zero_v, sem):
        core = lax.axis_index("c")
        sub = lax.axis_index("s")

        @pl.loop(0, SC_ZROWS)
        def _(r):
            for l0 in range(0, SC_SLAB, SC_LANES):
                zero_v[r, pl.ds(l0, SC_LANES)] = jnp.zeros((SC_LANES,), F32)

        for bb in range(batches_per_core):
            batch = core * batches_per_core + bb
            pair = batch * N_EXPERTS + sub
            pltpu.sync_copy(tok_hbm.at[pair], tok_v)

            @pl.loop(0, d // SC_SLAB)
            def _(slab):
                cols = pl.ds(pl.multiple_of(slab * SC_SLAB, SC_SLAB), SC_SLAB)

                @pl.loop(0, own_rows, step=SC_ZROWS)
                def _(r0):
                    pltpu.sync_copy(zero_v, acc_sh.at[pl.ds(sub * own_rows + r0, SC_ZROWS)])

                def load(j, buf):
                    return pltpu.make_async_copy(
                        ye_hbm.at[pl.ds(pair * cap + j * SC_IDX, SC_IDX), cols], rows_v.at[buf], sem.at[buf])

                load(0, 0).start()
                plsc.subcore_barrier()

                for j in range(0, n_chunks, 2):
                    load(j + 1, 1).start()
                    load(j, 0).wait()
                    pltpu.sync_copy(rows_v.at[0], acc_sh.at[tok_v.at[j]], add=True)
                    if j + 2 < n_chunks:
                        load(j + 2, 0).start()
                    load(j + 1, 1).wait()
                    pltpu.sync_copy(rows_v.at[1], acc_sh.at[tok_v.at[j + 1]], add=True)

                plsc.subcore_barrier()
                pltpu.sync_copy(acc_sh.at[pl.ds(sub * own_rows, own_rows)],
                                out_hbm.at[pl.ds(batch * seq + sub * own_rows, own_rows), cols])

    return combine(ye_flat, tok3)


TN = 512


def _resln_kernel(x_ref, *refs, alpha):
    *y_refs, g_ref, b_ref, o_ref = refs
    y = jnp.concatenate([r[...] for r in y_refs], axis=1)
    o_ref[...] = _standardize(alpha * x_ref[...] + y) * g_ref[...] + b_ref[...]


def _resln_call(x2d, y_groups, g, bta, alpha):
    n, d = x2d.shape
    tok = lambda w: pl.BlockSpec((TN, w), lambda i: (i, 0))
    vec = pl.BlockSpec((1, d), lambda i: (0, 0))
    return pl.pallas_call(
        functools.partial(_resln_kernel, alpha=alpha), name="residual_layernorm",
        grid=(n // TN,), in_specs=[tok(d)] + [tok(GC)] * len(y_groups) + [vec, vec], out_specs=tok(d),
        out_shape=jax.ShapeDtypeStruct((n, d), F32),
        compiler_params=_cparams(1),
    )(x2d, *y_groups, g, bta)


def _expert_choice_ffn(x1p, aff_t, w1, w3, w2, layer):
    b, s, half = x1p.shape
    d = 2 * half
    cap = EC_FACTOR * s // N_EXPERTS
    slot = _select_call(aff_t, cap)
    xs, tok, gate = _sc_dispatch_call(x1p.reshape(b * s, half), slot.reshape(b * N_EXPERTS, s),
                                      aff_t.reshape(b * N_EXPERTS, s), s, cap)
    hid = _expert_up_call(xs.reshape(b, N_EXPERTS, cap, half), w1, w3, layer)
    gate4 = gate.reshape(b, N_EXPERTS, cap, 1)
    tok3 = tok.reshape(b * N_EXPERTS, cap // SC_IDX, SC_IDX)
    outs = []
    for group in range(d // GC):
        ye = _expert_down_call(hid, gate4, w2, layer, group)
        outs.append(_sc_combine_call(ye.reshape(b * N_EXPERTS * cap, GC), tok3, s))
    return tuple(outs)


def _pack_pool(pool_w):
    g, gd, _ = pool_w.shape
    out = jnp.zeros((g * gd, g * gd), F32)
    for i in range(g):
        out = out.at[i * gd:(i + 1) * gd, i * gd:(i + 1) * gd].set(pool_w[i])
    return out.astype(BF16)


def _layer(layer, x, pending, alpha, bias_tiles, w_in, b_in, gm_ln_g, gm_ws, gm_bs, ml_conv, ml_fbias,
           ml_norm_g, pool_w, pool_scale, w_branch, w_out, ln1_g, ln1_b, w_router, w_e1, w_e3, w_e2):
    b, s, d = x.shape
    n_small = 2576
    w_cat, b_cat = _pack_inproj_weights(w_in, b_in)
    wscat, bsfull = _pack_gmlp(gm_ws, gm_bs)
    if pending is None:
        outs = _inproj_call(x.reshape(b * s, d), w_cat, b_cat, gm_ln_g[None], wscat, bsfull, b)
    else:
        *outs, x2 = _inproj_call(pending, w_cat, b_cat, gm_ln_g[None], wscat, bsfull, b, alpha)
        x = x2.reshape(b, s, d)
    ya, qkv1, qkv4, qkv16, cqk, cv, co, dx, ig, fg = outs
    r3 = lambda t: t.reshape(b, s, t.shape[-1])
    o_list, l_list = [], []
    for qkv, bias in zip((qkv1, qkv4, qkv16), bias_tiles):
        o, lse = _attn_call(qkv, bias)
        o_list.append(o)
        l_list.append(lse)
    hf, hb = _mlstm_branch(r3(cqk), r3(cv), r3(ig), r3(fg), ml_conv, ml_fbias)
    yd = _pool_call(r3(dx), _pack_pool(pool_w), pool_scale[None])
    x1, x1p, aff_t = _merge_call(
        x, r3(ya), o_list, l_list, hf, hb, r3(co), yd,
        w_in[:, n_small:].astype(BF16), b_in[None, n_small:], w_branch.astype(BF16), w_out.astype(BF16),
        ml_norm_g[:, None], ln1_g[None], ln1_b[None], jnp.transpose(w_router).astype(BF16), alpha)
    ffn = _expert_choice_ffn(x1p, aff_t, w_e1, w_e3, w_e2, layer)
    return x1.reshape(b * s, d), ffn


def kernel(x, w_in, b_in, gm_ln_g, gm_ws, gm_bs, rel_bias, ml_conv, ml_fbias, ml_norm_g, pool_w,
           pool_scale, w_branch, w_out, ln1_g, ln1_b, w_router, w_e1, w_e3, w_e2, ln2_g, ln2_b):
    depth = w_in.shape[0]
    alpha = (2 * depth) ** 0.25
    bias_tiles = [_attn_bias_tile(rel_bias, window, dil) for window, dil in DIL_PATTERNS]
    b, s, d = x.shape
    pending = None
    for l in range(depth):
        x1, ffn = _layer(l, x, pending, alpha, bias_tiles, w_in[l], b_in[l], gm_ln_g[l], gm_ws[l], gm_bs[l],
                         ml_conv[l], ml_fbias[l], ml_norm_g[l], pool_w[l], pool_scale[l], w_branch[l],
                         w_out[l], ln1_g[l], ln1_b[l], w_router[l], w_e1, w_e3, w_e2)
        pending = (x1, ffn, ln2_g[l][None], ln2_b[l][None])
    return _resln_call(*pending, alpha).reshape(b, s, d)
```

```python
import functools
import math

import jax
import jax.numpy as jnp
import numpy as np
from jax import lax
from jax.experimental import pallas as pl
from jax.experimental.pallas import tpu as pltpu
from jax.experimental.pallas import tpu_sc as plsc

F32 = jnp.float32
BF16 = jnp.bfloat16

D_MODEL = 1024
MIX_W = 256
N_BRANCH = 4
GM_CHUNK = 128
GM_GROUPS = 4
ATT_HEADS = 4
ATT_HD = 64
DIL_PATTERNS = ((128, 1), (512, 4), (2048, 16))
ATT_BLOCK = 64
REL_BUCKETS = 32
REL_MAX_DIST = 1024
ML_HEADS = 4
ML_HD = 64
ML_CHUNK = 64
POOL_WINDOWS = (2, 4, 8, 16)
N_EXPERTS = 16
EXPERT_FF = 1024
EC_FACTOR = 2
LN_EPS = 1e-5
NEG_BIG = -1e30

V7X_VMEM_LIMIT = 56 * 1024 * 1024
LANES = 128
HALO = 8


def _cparams(n_grid, vmem=V7X_VMEM_LIMIT):
    return pltpu.CompilerParams(dimension_semantics=("arbitrary",) * n_grid,
                                vmem_limit_bytes=vmem)


def _pack_bf16_pair(lo, hi):
    lo_bits = lax.shift_right_logical(pltpu.bitcast(lo.astype(F32), jnp.int32), 16)
    return pltpu.bitcast(hi.astype(F32), jnp.int32) | lo_bits


def _unpack_bf16_pair(packed):
    lo = pltpu.bitcast(lax.shift_left(packed, 16), F32).astype(BF16)
    hi = pltpu.bitcast(packed & jnp.int32(-65536), F32).astype(BF16)
    return lo, hi


def _standardize(xf):
    mu = jnp.mean(xf, axis=-1, keepdims=True)
    var = jnp.mean(jnp.square(xf - mu), axis=-1, keepdims=True)
    return (xf - mu) * lax.rsqrt(var + LN_EPS)


TA = 512
A_COLS = 2560 + 2 * LANES


def _inproj_kernel(*refs, alpha):
    if alpha is None:
        x_ref, *refs = refs
        x = x_ref[...]
    else:
        n_grp = D_MODEL // GC
        x1_ref, y_refs, (g2_ref, b2_ref, *refs) = refs[0], refs[1:1 + n_grp], refs[1 + n_grp:]
        y = jnp.concatenate([r[...] for r in y_refs], axis=1)
        x = _standardize(alpha * x1_ref[...] + y) * g2_ref[...] + b2_ref[...]
        refs[-2][...] = x
        refs = refs[:-2] + refs[-1:]
    (w_ref, b_ref, lng_ref, wscat_ref, bsfull_ref,
     ya_ref, qkv1_ref, qkv4_ref, qkv16_ref, cqk_ref, cv_ref, co_ref, dx_ref, ig_ref, fg_ref, qkv_scr) = refs
    xb = x.astype(BF16)
    h = jnp.dot(xb, w_ref[...], preferred_element_type=F32) + b_ref[...]
    qkv1_ref[0, 0] = h[:, 512:1280].astype(BF16)
    for c in range(768 // LANES):
        qkv_scr[c] = h[:, 512 + c * LANES:512 + (c + 1) * LANES]
    for (_, dil), out_ref in zip(DIL_PATTERNS[1:], (qkv4_ref, qkv16_ref)):
        for r in range(dil):
            for c in range(768 // LANES):
                out_ref[0, r, :, c * LANES:(c + 1) * LANES] = (
                    qkv_scr[c, pl.ds(r, TA // dil, stride=dil), :].astype(BF16))
    cqk_ref[...] = h[:, 1280:1792]
    cv_ref[...] = h[:, 1792:2048].astype(BF16)
    co_ref[...] = h[:, 2048:2304]
    dx_ref[...] = h[:, 2304:2560]
    ig_ref[...] = h[:, 2560:2688]
    fg_ref[...] = h[:, 2688:2816]
    u = jax.nn.gelu(h[:, 0:256])
    v = jax.nn.gelu(h[:, 256:512])
    vn = _standardize(v) * lng_ref[...]
    lane_grp = lax.broadcasted_iota(jnp.int32, (GM_CHUNK, MIX_W), 1) // (MIX_W // GM_GROUPS)
    for c in range(TA // GM_CHUNK):
        vc = vn[c * GM_CHUNK:(c + 1) * GM_CHUNK]
        stacked = jnp.concatenate(
            [jnp.where(lane_grp == g, vc, 0.0).astype(BF16) for g in range(GM_GROUPS)], axis=0)
        mixed = jnp.dot(wscat_ref[...], stacked, preferred_element_type=F32) + bsfull_ref[...]
        ya_ref[c * GM_CHUNK:(c + 1) * GM_CHUNK, :] = (
            u[c * GM_CHUNK:(c + 1) * GM_CHUNK] * mixed).astype(BF16)


def _inproj_call(x_in, w_cat, b_cat, lng, wscat, bsfull, batch, alpha=None):
    fused = alpha is not None
    n = (x_in[0] if fused else x_in).shape[0]
    seq = n // batch
    tpb = seq // TA
    tok = lambda w: pl.BlockSpec((TA, w), lambda i: (i, 0))
    const = lambda s: pl.BlockSpec(s, lambda i: (0,) * len(s))
    regrouped = lambda dil: pl.BlockSpec((1, dil, TA // dil, 768), lambda i: (i // tpb, 0, i % tpb, 0))
    out_shape = (
        jax.ShapeDtypeStruct((n, 256), BF16),
    ) + tuple(jax.ShapeDtypeStruct((batch, dil, seq // dil, 768), BF16)
              for _, dil in DIL_PATTERNS) + (
        jax.ShapeDtypeStruct((n, 512), F32),
        jax.ShapeDtypeStruct((n, 256), BF16),
        jax.ShapeDtypeStruct((n, 256), F32),
        jax.ShapeDtypeStruct((n, 256), F32),
        jax.ShapeDtypeStruct((n, LANES), F32),
        jax.ShapeDtypeStruct((n, LANES), F32),
    )
    if fused:
        x1, ffn_groups, ln_g, ln_b = x_in
        x_args = (x1, *ffn_groups, ln_g, ln_b)
        x_specs = ([tok(D_MODEL)] + [tok(GC)] * len(ffn_groups) + [const((1, D_MODEL)), const((1, D_MODEL))])
    else:
        x_args, x_specs = (x_in,), [tok(D_MODEL)]
    out_specs = ((tok(256),) + tuple(regrouped(dil) for _, dil in DIL_PATTERNS)
                 + (tok(512), tok(256), tok(256), tok(256), tok(LANES), tok(LANES)))
    if fused:
        out_specs += (tok(D_MODEL),)
        out_shape += (jax.ShapeDtypeStruct((n, D_MODEL), F32),)
    return pl.pallas_call(
        functools.partial(_inproj_kernel, alpha=alpha), name="inproj_gmlp",
        grid=(n // TA,),
        in_specs=x_specs + [const((D_MODEL, A_COLS)), const((1, A_COLS)), const((1, MIX_W)),
                            const((GM_CHUNK, GM_GROUPS * GM_CHUNK)), const((GM_CHUNK, MIX_W))],
        out_specs=out_specs,
        out_shape=out_shape,
        scratch_shapes=[pltpu.VMEM((768 // LANES, TA, LANES), F32)],
        compiler_params=_cparams(1),
    )(*x_args, w_cat, b_cat, lng, wscat, bsfull)


def _pack_inproj_weights(w_in, b_in):
    pad = lambda a: jnp.pad(a, ((0, 0), (0, LANES - 8)))
    w_cat = jnp.concatenate([w_in[:, 0:2304], w_in[:, 2320:2576],
                             pad(w_in[:, 2304:2312]), pad(w_in[:, 2312:2320])], axis=1)
    b2 = b_in[None, :]
    b_cat = jnp.concatenate([b2[:, 0:2304], b2[:, 2320:2576],
                             pad(b2[:, 2304:2312]), pad(b2[:, 2312:2320])], axis=1)
    return w_cat.astype(BF16), b_cat


def _pack_gmlp(gm_ws, gm_bs):
    wscat = jnp.transpose(gm_ws, (1, 0, 2)).reshape(GM_CHUNK, GM_GROUPS * GM_CHUNK).astype(BF16)
    bsfull = jnp.repeat(jnp.transpose(gm_bs), MIX_W // GM_GROUPS, axis=1)
    return wscat, bsfull


def _halo_specs(t, width, n_tiles):
    r = t // HALO
    main = pl.BlockSpec((1, t, width), lambda b, i: (b, i, 0))
    prev = pl.BlockSpec((1, HALO, width), lambda b, i: (b, jnp.maximum(i * r - 1, 0), 0))
    nxt = pl.BlockSpec((1, HALO, width), lambda b, i: (b, jnp.minimum((i + 1) * r, n_tiles * r - 1), 0))
    return main, prev, nxt


def _fill_halo_scratch(buf, x_ref, p_ref, n_ref, t):
    i = pl.program_id(1)
    last = pl.num_programs(1) - 1
    buf[0:HALO, :] = jnp.where(i > 0, p_ref[0], 0.0)
    buf[HALO:HALO + t, :] = x_ref[0]
    buf[HALO + t:2 * HALO + t, :] = jnp.where(i < last, n_ref[0], 0.0)


TP = 512


def _pool_kernel(x_ref, p_ref, n_ref, w_ref, sc_ref, o_ref, buf):
    _fill_halo_scratch(buf, x_ref, p_ref, n_ref, TP)
    seq = pl.num_programs(1) * TP
    pos = pl.program_id(1) * TP + lax.broadcasted_iota(jnp.int32, (TP, 1), 0)
    lane_grp = lax.broadcasted_iota(jnp.int32, (TP, MIX_W), 1) // (MIX_W // len(POOL_WINDOWS))
    x0 = buf[HALO:HALO + TP, :]
    pooled = jnp.zeros((TP, MIX_W), F32)
    acc = None
    half_done = 0
    for gi, win in enumerate(POOL_WINDOWS):
        half = win // 2
        for o in list(range(-half, -half_done)) + list(range(half_done, half)):
            term = buf[HALO + o:HALO + o + TP, :]
            acc = term if acc is None else acc + term
        half_done = half
        cnt = (jnp.minimum(pos + half, seq) - jnp.maximum(pos - half, 0)).astype(F32)
        pooled = jnp.where(lane_grp == gi, acc / cnt - x0, pooled)
    mixed = jnp.dot(pooled.astype(BF16), w_ref[...], preferred_element_type=F32)
    o_ref[0] = (mixed * sc_ref[...]).astype(BF16)


def _pool_call(dx, w_block, scale):
    b, s, _ = dx.shape
    nt = s // TP
    main, prev, nxt = _halo_specs(TP, MIX_W, nt)
    return pl.pallas_call(
        _pool_kernel, name="pool_mixer",
        grid=(b, nt),
        in_specs=[main, prev, nxt,
                  pl.BlockSpec((MIX_W, MIX_W), lambda b, i: (0, 0)),
                  pl.BlockSpec((1, MIX_W), lambda b, i: (0, 0))],
        out_specs=pl.BlockSpec((1, TP, MIX_W), lambda b, i: (b, i, 0)),
        out_shape=jax.ShapeDtypeStruct((b, s, MIX_W), BF16),
        scratch_shapes=[pltpu.VMEM((TP + 2 * HALO, MIX_W), F32)],
        compiler_params=_cparams(2),
    )(dx, dx, dx, w_block, scale)


TQ = 128
TQS = 512
TKEYS = TQ + 2 * ATT_BLOCK


def _attn_kernel(q_ref, kp_ref, km_ref, kn_ref, vp_ref, vm_ref, vn_ref, bias_ref, o_ref, lse_ref):
    i = pl.program_id(2)
    seq = pl.num_programs(2) * TQS
    q = q_ref[0, 0] * ATT_HD ** -0.5
    k = jnp.concatenate([kp_ref[0, 0], km_ref[0, 0], kn_ref[0, 0]], axis=0)
    v = jnp.concatenate([vp_ref[0, 0], vm_ref[0, 0], vn_ref[0, 0]], axis=0)
    lane = lax.broadcasted_iota(jnp.int32, (TQ, LANES), 1)
    lane_half = lax.broadcasted_iota(jnp.int32, (1, LANES), 1) // ATT_HD
    keep = [jnp.where(lane_half == hh, 1.0, 0.0).astype(BF16) for hh in range(2)]
    for j in range(TQS // TQ):
        kpos = i * TQS + j * TQ - ATT_BLOCK + lax.broadcasted_iota(jnp.int32, (1, TKEYS), 1)
        kvalid = (kpos >= 0) & (kpos < seq)
        qrows = slice(j * TQ, (j + 1) * TQ)
        krows = slice(j * TQ, j * TQ + TKEYS)
        lse_tile = jnp.zeros((TQ, LANES), F32)
        for pair in range(ATT_HEADS // 2):
            grp = slice(pair * LANES, (pair + 1) * LANES)
            q_pair, k_pair, v_pair = q[qrows, grp], k[krows, grp], v[krows, grp]
            o_pair = jnp.zeros((TQ, LANES), F32)
            for hh in range(2):
                h = 2 * pair + hh
                logits = lax.dot_general(q_pair * keep[hh], k_pair, (((1,), (1,)), ((), ())),
                                         preferred_element_type=F32) + bias_ref[h]
                logits = jnp.where(kvalid, logits, NEG_BIG)
                m = jnp.max(logits, axis=-1, keepdims=True)
                p = jnp.exp(logits - m)
                ssum = jnp.sum(p, axis=-1, keepdims=True)
                o = jnp.dot(p.astype(BF16), v_pair, preferred_element_type=F32) / ssum
                o_pair = jnp.where(lane_half == hh, o, o_pair)
                lse_tile = jnp.where(lane == h, m + jnp.log(ssum), lse_tile)
            o_ref[0, 0, qrows, grp] = o_pair
        lse_ref[0, 0, qrows, :] = lse_tile


def _attn_call(qkv, bias):
    b, dil, l, _ = qkv.shape
    nt = l // TQS
    r64 = TQS // ATT_BLOCK
    main = lambda c: pl.BlockSpec((1, 1, TQS, MIX_W), lambda b, r, i: (b, r, i, c))
    prev = lambda c: pl.BlockSpec((1, 1, ATT_BLOCK, MIX_W),
                                  lambda b, r, i: (b, r, jnp.maximum(i * r64 - 1, 0), c))
    nxt = lambda c: pl.BlockSpec((1, 1, ATT_BLOCK, MIX_W),
                                 lambda b, r, i: (b, r, jnp.minimum((i + 1) * r64, nt * r64 - 1), c))
    return pl.pallas_call(
        _attn_kernel, name="band_attention",
        grid=(b, dil, nt),
        in_specs=[main(0), prev(1), main(1), nxt(1), prev(2), main(2), nxt(2),
                  pl.BlockSpec((ATT_HEADS, TQ, TKEYS), lambda b, r, i: (0, 0, 0))],
        out_specs=(pl.BlockSpec((1, 1, TQS, MIX_W), lambda b, r, i: (b, r, i, 0)),
                   pl.BlockSpec((1, 1, TQS, LANES), lambda b, r, i: (b, r, i, 0))),
        out_shape=(jax.ShapeDtypeStruct((b, dil, l, MIX_W), F32),
                   jax.ShapeDtypeStruct((b, dil, l, LANES), F32)),
        compiler_params=_cparams(3),
    )(qkv, qkv, qkv, qkv, qkv, qkv, qkv, bias)


def _t5_bucket_static(rel):
    half = REL_BUCKETS // 2
    max_exact = half // 2
    ret = np.where(rel > 0, half, 0)
    n = np.abs(rel)
    nf = np.maximum(n, 1).astype(np.float32)
    large = max_exact + (np.log(nf / np.float32(max_exact)) / np.float32(math.log(REL_MAX_DIST / max_exact))
                         * np.float32(half - max_exact)).astype(np.int32)
    large = np.minimum(large, half - 1)
    return ret + np.where(n < max_exact, n, large)


def _attn_bias_tile(rel_bias, window, dil):
    side = (window // 2) // dil
    rel = np.arange(TKEYS)[None, :] - ATT_BLOCK - np.arange(TQ)[:, None]
    onehot = jax.nn.one_hot(jnp.asarray(_t5_bucket_static(dil * rel), jnp.int32), REL_BUCKETS, dtype=F32)
    bias = jnp.einsum('qkr,rh->hqk', onehot, rel_bias, precision=lax.Precision.HIGHEST)
    return jnp.where(jnp.asarray(np.abs(rel) <= side)[None], bias, NEG_BIG)


TM = 512
VT_ROWS = ML_HD + 16


def _mlprep_kernel(x_ref, p_ref, n_ref, v_ref, w_ref, qt_out, k_out, vt_out, buf):
    _fill_halo_scratch(buf, x_ref, p_ref, n_ref, TM)
    conv = (buf[HALO - 1:HALO - 1 + TM, :] * w_ref[0:1, :] + buf[HALO:HALO + TM, :] * w_ref[1:2, :]
            + buf[HALO + 1:HALO + 1 + TM, :] * w_ref[2:3, :])
    qk = jax.nn.silu(conv)
    qt = jnp.transpose(qk[:, :MIX_W])
    vt = jnp.transpose(v_ref[0].astype(F32))
    ones_rows = jnp.where(lax.broadcasted_iota(jnp.int32, (VT_ROWS - ML_HD, ML_CHUNK), 0) == 0, 1.0, 0.0)
    for h in range(ML_HEADS):
        sl = slice(h * ML_HD, (h + 1) * ML_HD)
        k_out[0, h] = (qk[:, MIX_W + h * ML_HD:MIX_W + (h + 1) * ML_HD] * ML_HD ** -0.5).astype(BF16)
        for c in range(TM // ML_CHUNK):
            cl = slice(c * ML_CHUNK, (c + 1) * ML_CHUNK)
            qt_out[0, h, c] = qt[sl, cl].astype(BF16)
            vt_out[0, h, c] = jnp.concatenate([vt[sl, cl], ones_rows], axis=0).astype(BF16)


def _mlprep_call(cqk, cv, conv_w):
    b, s, _ = cqk.shape
    nt = s // TM
    nc = s // ML_CHUNK
    cpt = TM // ML_CHUNK
    main, prev, nxt = _halo_specs(TM, 2 * MIX_W, nt)
    return pl.pallas_call(
        _mlprep_kernel, name="mlstm_prep",
        grid=(b, nt),
        in_specs=[main, prev, nxt,
                  pl.BlockSpec((1, TM, MIX_W), lambda b, i: (b, i, 0)),
                  pl.BlockSpec((3, 2 * MIX_W), lambda b, i: (0, 0))],
        out_specs=(pl.BlockSpec((1, ML_HEADS, cpt, ML_HD, ML_CHUNK), lambda b, i: (b, 0, i, 0, 0)),
                   pl.BlockSpec((1, ML_HEADS, TM, ML_HD), lambda b, i: (b, 0, i, 0)),
                   pl.BlockSpec((1, ML_HEADS, cpt, VT_ROWS, ML_CHUNK), lambda b, i: (b, 0, i, 0, 0))),
        out_shape=(jax.ShapeDtypeStruct((b, ML_HEADS, nc, ML_HD, ML_CHUNK), BF16),
                   jax.ShapeDtypeStruct((b, ML_HEADS, s, ML_HD), BF16),
                   jax.ShapeDtypeStruct((b, ML_HEADS, nc, VT_ROWS, ML_CHUNK), BF16)),
        scratch_shapes=[pltpu.VMEM((TM + 2 * HALO, 2 * MIX_W), F32)],
        compiler_params=_cparams(2),
    )(cqk, cqk, cqk, cv, conv_w)


GP_BLK = 512


def _chunk_scan(x, fwd_lane, t_in_chunk, op, ident):
    n = x.shape[0]
    shift = 1
    while shift < ML_CHUNK:
        down = pltpu.roll(x, shift, 0)
        up = pltpu.roll(x, n - shift, 0)
        nb = jnp.where(fwd_lane,
                       jnp.where(t_in_chunk >= shift, down, ident),
                       jnp.where(t_in_chunk < ML_CHUNK - shift, up, ident))
        x = op(x, nb)
        shift *= 2
    return x


def _gatescan_kernel(ig_ref, fg_ref, fb_ref, b_ref, a_ref, cm_ref, g_ref, amax_ref):
    lane = lax.broadcasted_iota(jnp.int32, (1, LANES), 1)
    fwd_lane = lane < ML_HEADS
    t_in_chunk = lax.broadcasted_iota(jnp.int32, (GP_BLK, 1), 0) % ML_CHUNK
    z = fg_ref[0] + fb_ref[...]
    lf = jnp.minimum(z, 0.0) - jnp.log1p(jnp.exp(-jnp.abs(z)))
    b = _chunk_scan(lf, fwd_lane, t_in_chunk, jnp.add, 0.0)
    a = ig_ref[0] - b
    b_ref[0] = b
    a_ref[0] = a
    cm_ref[0] = _chunk_scan(a, fwd_lane, t_in_chunk, jnp.maximum, -jnp.inf)
    cpb = GP_BLK // ML_CHUNK
    last = pl.ds(ML_CHUNK - 1, cpb, stride=ML_CHUNK)
    first = pl.ds(0, cpb, stride=ML_CHUNK)
    g_ref[0] = jnp.where(fwd_lane, b_ref[0, last, :], b_ref[0, first, :])
    amax_ref[0] = jnp.where(fwd_lane, cm_ref[0, last, :], cm_ref[0, first, :])


def _gateout_kernel(b_ref, a_ref, cm_ref, g_ref, amax_ref, ws_ref, m_ref, iw_ref, en_ref, dec_ref,
                    mch_s, mlast_s):
    nc = g_ref.shape[1]
    j = pl.program_id(1)
    fwd_lane = lax.broadcasted_iota(jnp.int32, (1, LANES), 1) < ML_HEADS

    @pl.when(j == 0)
    def _():
        def m_step(i, carry):
            mf, mb = carry
            cf = pl.ds(i, 1)
            cb = pl.ds(nc - 1 - i, 1)
            mch_s[cf, :] = jnp.where(fwd_lane, mf, mch_s[cf, :])
            mch_s[cb, :] = jnp.where(fwd_lane, mch_s[cb, :], mb)
            mf = g_ref[0, cf, :] + jnp.maximum(mf, amax_ref[0, cf, :])
            mb = g_ref[0, cb, :] + jnp.maximum(mb, amax_ref[0, cb, :])
            return mf, mb

        mch_s[...] = jnp.zeros((nc, LANES), F32)
        zero = jnp.zeros((1, LANES), F32)
        lax.fori_loop(0, nc, m_step, (zero, zero))
        mlast = jnp.maximum(amax_ref[0], mch_s[...])
        mlast_s[...] = mlast
        dec_ref[0] = jnp.exp(mch_s[...] - mlast)

    cpb = GP_BLK // ML_CHUNK
    crow = pl.ds(pl.multiple_of(j * cpb, cpb), cpb)
    expand = lambda t: jnp.broadcast_to(t[:, None, :], (cpb, ML_CHUNK, LANES)).reshape(GP_BLK, LANES)
    m_tok = expand(mch_s[crow, :])
    mlast_tok = expand(mlast_s[crow, :])
    mt = jnp.maximum(cm_ref[0], m_tok)
    m_ref[0] = mt
    iw_ref[0] = jnp.exp(m_tok - mt)
    en_ref[0] = jnp.exp(-(b_ref[0] + mt))
    ws_ref[0] = jnp.exp(a_ref[0] - mlast_tok)


def _gateprep_call(ig, fg, fbias_row):
    b, s, _ = ig.shape
    nc = s // ML_CHUNK
    cpb = GP_BLK // ML_CHUNK
    tok = pl.BlockSpec((1, GP_BLK, LANES), lambda i, j: (i, j, 0))
    tok_shape = jax.ShapeDtypeStruct((b, s, LANES), F32)
    chunk_shape = jax.ShapeDtypeStruct((b, nc, LANES), F32)
    chunk_tile = pl.BlockSpec((1, cpb, LANES), lambda i, j: (i, j, 0))
    chunk_all = pl.BlockSpec((1, nc, LANES), lambda i, j: (i, 0, 0))
    bcum, a, cm, g, amax = pl.pallas_call(
        _gatescan_kernel, name="mlstm_gate_scan",
        grid=(b, s // GP_BLK),
        in_specs=[tok, tok, pl.BlockSpec((1, LANES), lambda i, j: (0, 0))],
        out_specs=(tok, tok, tok, chunk_tile, chunk_tile),
        out_shape=(tok_shape,) * 3 + (chunk_shape,) * 2,
        compiler_params=_cparams(2),
    )(ig, fg, fbias_row)
    ws, m_col, iw_col, en_col, decay = pl.pallas_call(
        _gateout_kernel, name="mlstm_gate_out",
        grid=(b, s // GP_BLK),
        in_specs=[tok, tok, tok, chunk_all, chunk_all],
        out_specs=(tok, tok, tok, tok, chunk_all),
        out_shape=(tok_shape,) * 4 + (chunk_shape,),
        scratch_shapes=[pltpu.VMEM((nc, LANES), F32)] * 2,
        compiler_params=_cparams(2),
    )(bcum, a, cm, g, amax)
    return a, ws, m_col, iw_col, en_col, decay


TE = 1024


def _mlstm_kernel(*refs):
    fwd, bwd, (hf_ref, hb_ref, state) = refs[:9], refs[9:18], refs[18:]
    i = pl.program_id(1)

    @pl.when(i == 0)
    def _():
        state[...] = jnp.zeros(state.shape, F32)

    cpt = TE // ML_CHUNK
    s_idx = lax.broadcasted_iota(jnp.int32, (ML_CHUNK, ML_CHUNK), 0)
    t_idx = lax.broadcasted_iota(jnp.int32, (ML_CHUNK, ML_CHUNK), 1)

    def chunk_body(c, carry):
        jobs = []
        for d, (qt_r, k_r, vt_r, a_r, m_r, iw_r, en_r, ws_r, dec_r), out_r in ((0, fwd, hf_ref), (1, bwd, hb_ref)):
            cc = c if d == 0 else cpt - 1 - c
            rows = pl.ds(pl.multiple_of(cc * ML_CHUNK, ML_CHUNK), ML_CHUNK)
            crow = pl.ds(cc, 1)
            for h in range(ML_HEADS):
                ch = d * ML_HEADS + h
                row = lambda r: r[0, ch, crow, :]
                jobs.append(dict(
                    ch=ch, tri=(s_idx >= t_idx) if d else (s_idx <= t_idx),
                    k=k_r[0, h, rows, :], qt=qt_r[0, h, cc], vt=vt_r[0, h, cc], a=a_r[0, rows, ch:ch + 1],
                    m=row(m_r), iw=row(iw_r), en=row(en_r), ws=row(ws_r), dec=row(dec_r),
                    out=(out_r, cc, h)))
        for j in jobs:
            j["cst"] = state[j["ch"]]
            j["st"] = jnp.dot(j["k"], j["qt"], preferred_element_type=F32)
            j["inter"] = jnp.dot(j["cst"].astype(BF16), j["qt"], preferred_element_type=F32)
            j["upd"] = jnp.dot((j["vt"].astype(F32) * j["ws"]).astype(BF16), j["k"],
                               preferred_element_type=F32)
        for j in jobs:
            j["swt"] = j["st"] * jnp.exp(jnp.where(j["tri"], j["a"] - j["m"], NEG_BIG))
            j["intra"] = jnp.dot(j["vt"], j["swt"].astype(BF16), preferred_element_type=F32)
        for j in jobs:
            den = jnp.sum(j["swt"], axis=0, keepdims=True) + j["iw"] * j["inter"][ML_HD:ML_HD + 1]
            tot = j["intra"][:ML_HD] + j["iw"] * j["inter"][:ML_HD]
            out_r, cc, h = j["out"]
            out_r[0, cc, h * ML_HD:(h + 1) * ML_HD, :] = tot / jnp.maximum(jnp.abs(den), j["en"])
            state[j["ch"]] = j["dec"] * j["cst"] + j["upd"]
        return carry

    lax.fori_loop(0, cpt, chunk_body, 0)


def _mlstm_call(qt, k, vt, a_col, m_row, iw_row, en_row, ws_row, dec_row):
    b, _, s, _ = k.shape
    nt = s // TE
    nc = s // ML_CHUNK
    cpt = TE // ML_CHUNK

    def specs(rev):
        ti = (lambda i: nt - 1 - i) if rev else (lambda i: i)
        row = pl.BlockSpec((1, 2 * ML_HEADS, cpt, ML_CHUNK), lambda b, i: (b, 0, ti(i), 0))
        return [
            pl.BlockSpec((1, ML_HEADS, cpt, ML_HD, ML_CHUNK), lambda b, i: (b, 0, ti(i), 0, 0)),
            pl.BlockSpec((1, ML_HEADS, TE, ML_HD), lambda b, i: (b, 0, ti(i), 0)),
            pl.BlockSpec((1, ML_HEADS, cpt, VT_ROWS, ML_CHUNK), lambda b, i: (b, 0, ti(i), 0, 0)),
            pl.BlockSpec((1, TE, LANES), lambda b, i: (b, ti(i), 0)),
            row, row, row, row, row]

    args = [qt, k, vt, a_col, m_row, iw_row, en_row, ws_row, dec_row]
    out_f = pl.BlockSpec((1, cpt, MIX_W, ML_CHUNK), lambda b, i: (b, i, 0, 0))
    out_b = pl.BlockSpec((1, cpt, MIX_W, ML_CHUNK), lambda b, i: (b, nt - 1 - i, 0, 0))
    return pl.pallas_call(
        _mlstm_kernel, name="mlstm_scan",
        grid=(b, nt),
        in_specs=specs(False) + specs(True),
        out_specs=(out_f, out_b),
        out_shape=(jax.ShapeDtypeStruct((b, nc, MIX_W, ML_CHUNK), F32),) * 2,
        scratch_shapes=[pltpu.VMEM((2 * ML_HEADS, VT_ROWS, ML_HD), F32)],
        compiler_params=_cparams(2),
    )(*args, *args)


def _mlstm_branch(cqk, cv, ig, fg, conv_w, fbias):
    b, s, _ = cqk.shape
    nc = s // ML_CHUNK
    qt, k, vt = _mlprep_call(cqk, cv, conv_w)
    fb_row = jnp.pad(fbias.reshape(1, 2 * ML_HEADS), ((0, 0), (0, LANES - 2 * ML_HEADS)))
    a, ws, m_col, iw_col, en_col, decay = _gateprep_call(ig, fg, fb_row)
    rows = lambda t: jnp.transpose(t[..., :2 * ML_HEADS], (0, 2, 1)).reshape(b, 2 * ML_HEADS, nc, ML_CHUNK)
    dec_row = jnp.broadcast_to(jnp.transpose(decay[..., :2 * ML_HEADS], (0, 2, 1))[..., None],
                               (b, 2 * ML_HEADS, nc, ML_CHUNK))
    return _mlstm_call(qt, k, vt, a, rows(m_col), rows(iw_col), rows(en_col), rows(ws), dec_row)


TF = 512


def _merge_kernel(x_ref, ya_ref, o1_ref, o2_ref, o3_ref, l1_ref, l2_ref, l3_ref, hf_ref, hb_ref,
                  co_ref, yd_ref, wg_ref, bg_ref, wbr_ref, wout_ref, mng_ref, lng_ref, lnb_ref,
                  wr_ref, x1_ref, x1p_ref, aff_ref, o_scr, l_scr, *, alpha):
    x = x_ref[0]
    xb = x.astype(BF16)

    def natural_order(src_ref, scr):
        dil, width = src_ref.shape[1], src_ref.shape[3]
        if dil == 1:
            return src_ref[0, 0]
        for r in range(dil):
            for c in range(width // LANES):
                scr[c, pl.ds(r, TF // dil, stride=dil), :] = src_ref[0, r, :, c * LANES:(c + 1) * LANES]
        return jnp.concatenate([scr[c] for c in range(width // LANES)], axis=1)

    lane_head = lax.broadcasted_iota(jnp.int32, (TF, MIX_W), 1) // ML_HD
    l1, l2, l3 = [natural_order(r, l_scr.at[p]) for p, r in enumerate((l1_ref, l2_ref, l3_ref))]
    o1, o2, o3 = [natural_order(r, o_scr.at[p]) for p, r in enumerate((o1_ref, o2_ref, o3_ref))]
    lm = jnp.maximum(jnp.maximum(l1, l2), l3)
    e1, e2, e3 = jnp.exp(l1 - lm), jnp.exp(l2 - lm), jnp.exp(l3 - lm)
    inv = 1.0 / (e1 + e2 + e3)

    def per_head(w):
        out = jnp.zeros((TF, MIX_W), F32)
        for h in range(ATT_HEADS):
            out = jnp.where(lane_head == h, w[:, h:h + 1], out)
        return out

    y_b = per_head(e1 * inv) * o1 + per_head(e2 * inv) * o2 + per_head(e3 * inv) * o3
    hsum_t = jnp.concatenate([hf_ref[0, c] + hb_ref[0, c] for c in range(TF // ML_CHUNK)], axis=1)
    per_head_rows = hsum_t.reshape(ML_HEADS, ML_HD, TF)
    mu = jnp.mean(per_head_rows, axis=1, keepdims=True)
    cen = per_head_rows - mu
    var = jnp.mean(cen * cen, axis=1, keepdims=True)
    hn_t = (cen * lax.rsqrt(var + LN_EPS)).reshape(MIX_W, TF)
    y_c_t = (jax.nn.sigmoid(jnp.transpose(co_ref[0])) * (hn_t * mng_ref[...])).astype(BF16)
    ys = (ya_ref[0], y_b.astype(BF16), None, yd_ref[0])
    merged = jnp.zeros((TF, D_MODEL), F32)
    for n in range(N_BRANCH):
        cols = slice(n * D_MODEL, (n + 1) * D_MODEL)
        gate = jax.nn.sigmoid(jnp.dot(xb, wg_ref[:, cols], preferred_element_type=F32) + bg_ref[:, cols])
        if ys[n] is None:
            proj = lax.dot_general(y_c_t, wbr_ref[n], (((0,), (0,)), ((), ())), preferred_element_type=F32)
        else:
            proj = jnp.dot(ys[n], wbr_ref[n], preferred_element_type=F32)
        merged = merged + gate * proj
    mix = jnp.dot(merged.astype(BF16), wout_ref[...], preferred_element_type=F32)
    x1 = _standardize(alpha * x + mix) * lng_ref[...] + lnb_ref[...]
    x1_ref[0] = x1
    x1b = x1.astype(BF16)
    x1p_ref[0] = _pack_bf16_pair(x1b[:, :D_MODEL // 2], x1b[:, D_MODEL // 2:])
    logits = lax.dot_general(wr_ref[...], x1b, (((1,), (1,)), ((), ())),
                             preferred_element_type=F32)
    ex = jnp.exp(logits - jnp.max(logits, axis=0, keepdims=True))
    aff_ref[0] = ex / jnp.sum(ex, axis=0, keepdims=True)


def _merge_call(x, ya, o_list, l_list, hf, hb, co, yd, wg, bg, wbr, wout, mng, lng, lnb, wr_t, alpha):
    b, s, _ = x.shape
    tok = lambda w: pl.BlockSpec((1, TF, w), lambda b, i: (b, i, 0))
    grouped = lambda dil, w: pl.BlockSpec((1, dil, TF // dil, w), lambda b, i: (b, 0, i, 0))
    chunked = pl.BlockSpec((1, TF // ML_CHUNK, MIX_W, ML_CHUNK), lambda b, i: (b, i, 0, 0))
    const = lambda shp: pl.BlockSpec(shp, lambda b, i: (0,) * len(shp))
    return pl.pallas_call(
        functools.partial(_merge_kernel, alpha=alpha), name="merge_ln_router",
        grid=(b, s // TF),
        in_specs=[tok(D_MODEL), tok(MIX_W)] + [grouped(dil, MIX_W) for _, dil in DIL_PATTERNS]
                 + [grouped(dil, LANES) for _, dil in DIL_PATTERNS]
                 + [chunked, chunked, tok(MIX_W), tok(MIX_W)]
                 + [const((D_MODEL, N_BRANCH * D_MODEL)), const((1, N_BRANCH * D_MODEL)),
                    const((N_BRANCH, MIX_W, D_MODEL)), const((D_MODEL, D_MODEL)), const((MIX_W, 1)),
                    const((1, D_MODEL)), const((1, D_MODEL)), const((N_EXPERTS, D_MODEL))],
        out_specs=(tok(D_MODEL), tok(D_MODEL // 2), pl.BlockSpec((1, N_EXPERTS, TF), lambda b, i: (b, 0, i))),
        out_shape=(jax.ShapeDtypeStruct((b, s, D_MODEL), F32),
                   jax.ShapeDtypeStruct((b, s, D_MODEL // 2), jnp.int32),
                   jax.ShapeDtypeStruct((b, N_EXPERTS, s), F32)),
        scratch_shapes=[pltpu.VMEM((len(DIL_PATTERNS), MIX_W // LANES, TF, LANES), F32),
                        pltpu.VMEM((len(DIL_PATTERNS), 1, TF, LANES), F32)],
        compiler_params=_cparams(2),
    )(x, ya, *o_list, *l_list, hf, hb, co, yd, wg, bg, wbr, wout, mng, lng, lnb, wr_t)


TT = 256


def _select_kernel(aff_ref, slot_ref, *, cap):
    s = aff_ref.shape[2]
    bits = pltpu.bitcast(aff_ref[0], jnp.int32)

    def bit_step(i, thr):
        cand = thr | jnp.left_shift(jnp.int32(1), 30 - i)
        cnt = jnp.sum((bits >= cand).astype(jnp.int32), axis=1, keepdims=True)
        return jnp.where(cnt >= cap, cand, thr)

    thr = lax.fori_loop(0, 31, bit_step, jnp.zeros((N_EXPERTS, 1), jnp.int32))
    gt = bits > thr
    eq = bits == thr
    need = (cap - jnp.sum(gt.astype(jnp.int32), axis=1, keepdims=True)).astype(F32)
    upper = (lax.broadcasted_iota(jnp.int32, (TT, TT), 0)
             <= lax.broadcasted_iota(jnp.int32, (TT, TT), 1)).astype(BF16)
    eq_before = jnp.zeros((N_EXPERTS, 1), F32)
    sel_before = jnp.zeros((N_EXPERTS, 1), F32)
    for j in range(s // TT):
        cols = slice(j * TT, (j + 1) * TT)
        eq_j = eq[:, cols]
        eq_incl = eq_before + jnp.dot(eq_j.astype(BF16), upper, preferred_element_type=F32)
        sel_j = gt[:, cols] | (eq_j & (eq_incl <= need))
        sel_f = sel_j.astype(F32)
        sel_incl = sel_before + jnp.dot(sel_f.astype(BF16), upper, preferred_element_type=F32)
        slot_ref[0, :, cols] = jnp.where(sel_j, sel_incl - 1.0, -1.0).astype(jnp.int32)
        eq_before = eq_incl[:, TT - 1:TT]
        sel_before = sel_incl[:, TT - 1:TT]


def _select_call(aff_t, cap):
    b, e, s = aff_t.shape
    return pl.pallas_call(
        functools.partial(_select_kernel, cap=cap), name="expert_choice_select",
        grid=(b,),
        in_specs=[pl.BlockSpec((1, e, s), lambda i: (i, 0, 0))],
        out_specs=pl.BlockSpec((1, e, s), lambda i: (i, 0, 0)),
        out_shape=jax.ShapeDtypeStruct((b, e, s), jnp.int32),
        compiler_params=_cparams(1),
    )(aff_t)


SC_LANES = 16
SC_ROWS = 64
SC_IDX = 128
SC_SLAB = 128
SC_ZROWS = 64
CF = 1024
GC = 512


def _sc_dispatch_call(x_flat, slot2, aff2, seq, cap):
    n_pairs = slot2.shape[0]
    d = x_flat.shape[1]
    info = plsc.get_sparse_core_info()
    n_workers = info.num_cores * info.num_subcores
    assert n_pairs % n_workers == 0 and seq % SC_LANES == 0 and cap % (2 * SC_ROWS) == 0
    pairs_per_worker = n_pairs // n_workers
    mesh = plsc.VectorSubcoreMesh(core_axis_name="c", subcore_axis_name="s")

    @functools.partial(
        pl.kernel, mesh=mesh, name="expert_dispatch_sc",
        compiler_params=pltpu.CompilerParams(needs_layout_passes=False),
        out_type=(jax.ShapeDtypeStruct((n_pairs * cap, d), x_flat.dtype),
                  jax.ShapeDtypeStruct((n_pairs, cap), jnp.int32),
                  jax.ShapeDtypeStruct((n_pairs, cap), F32)),
        scratch_types=[pltpu.VMEM((seq,), jnp.int32), pltpu.VMEM((seq,), F32),
                       pltpu.VMEM((cap,), jnp.int32), pltpu.VMEM((cap,), jnp.int32),
                       pltpu.VMEM((cap,), F32),
                       pltpu.VMEM((2, SC_ROWS, d), x_flat.dtype), pltpu.SemaphoreType.DMA((2,))])
    def dispatch(x_hbm, slot_hbm, aff_hbm, xs_hbm, tok_hbm, gate_hbm,
                 slot_v, aff_v, idx_v, tok_v, gate_v, rows_v, sem):
        worker = lax.axis_index("s") * info.num_cores + lax.axis_index("c")
        lane = lax.iota(jnp.int32, SC_LANES)

        def gather(c0, buf):
            return pltpu.make_async_copy(x_hbm.at[idx_v.at[pl.ds(c0, SC_ROWS)]], rows_v.at[buf], sem.at[buf])
        for k in range(pairs_per_worker):
            pair = worker * pairs_per_worker + k
            row0 = (pair // N_EXPERTS) * seq
            pltpu.sync_copy(slot_hbm.at[pair], slot_v)
            pltpu.sync_copy(aff_hbm.at[pair], aff_v)

            @pl.loop(0, seq, step=SC_LANES)
            def _(t0):
                sv = slot_v[pl.ds(t0, SC_LANES)]
                picked = sv >= 0
                plsc.store_scatter(tok_v, [sv], t0 + lane, mask=picked)
                plsc.store_scatter(idx_v, [sv], row0 + t0 + lane, mask=picked)
                plsc.store_scatter(gate_v, [sv], aff_v[pl.ds(t0, SC_LANES)], mask=picked)

            pltpu.sync_copy(tok_v, tok_hbm.at[pair])
            pltpu.sync_copy(gate_v, gate_hbm.at[pair])

            gather(0, 0).start()

            @pl.loop(0, cap, step=2 * SC_ROWS)
            def _(c0):
                gather(c0 + SC_ROWS, 1).start()
                gather(c0, 0).wait()
                pltpu.sync_copy(rows_v.at[0], xs_hbm.at[pl.ds(pair * cap + c0, SC_ROWS)])

                @pl.when(c0 + 2 * SC_ROWS < cap)
                def _():
                    gather(c0 + 2 * SC_ROWS, 0).start()

                gather(c0 + SC_ROWS, 1).wait()
                pltpu.sync_copy(rows_v.at[1], xs_hbm.at[pl.ds(pair * cap + c0 + SC_ROWS, SC_ROWS)])

    return dispatch(x_flat, slot2, aff2)


def _expert_up_kernel(xs_ref, w1_ref, w3_ref, hid_ref, w1_bf, w3_bf):
    @pl.when((pl.program_id(1) == 0) & (pl.program_id(2) == 0))
    def _():
        w1_bf[...] = w1_ref[0, 0].astype(BF16)
        w3_bf[...] = w3_ref[0, 0].astype(BF16)

    xs = jnp.concatenate(_unpack_bf16_pair(xs_ref[0, 0]), axis=1)
    hid_ref[0, 0] = (jax.nn.silu(jnp.dot(xs, w1_bf[...], preferred_element_type=F32))
                     * jnp.dot(xs, w3_bf[...], preferred_element_type=F32)).astype(BF16)


def _expert_down_kernel(hid_ref, g_ref, w2_ref, ye_ref):
    ye_ref[0, 0] = jnp.dot(hid_ref[0, 0], w2_ref[0, 0].astype(BF16),
                           preferred_element_type=F32) * g_ref[0, 0]


def _expert_up_call(xs4, w1, w3, layer):
    b, e, cap, half = xs4.shape
    d, ff = w1.shape[2], w1.shape[3]
    assert d == 2 * half
    rows = lambda w: pl.BlockSpec((1, 1, CF, w), lambda e, b, j: (b, e, j, 0))
    wspec = pl.BlockSpec((1, 1, d, ff), lambda e, b, j: (layer, e, 0, 0))
    return pl.pallas_call(
        _expert_up_kernel, name="expert_ffn_up",
        grid=(e, b, cap // CF),
        in_specs=[rows(half), wspec, wspec],
        out_specs=rows(ff),
        out_shape=jax.ShapeDtypeStruct((b, e, cap, ff), BF16),
        scratch_shapes=[pltpu.VMEM((d, ff), BF16), pltpu.VMEM((d, ff), BF16)],
        compiler_params=_cparams(3),
    )(xs4, w1, w3)


def _expert_down_call(hid4, gate4, w2, layer, group):
    b, e, cap, ff = hid4.shape
    rows = lambda w: pl.BlockSpec((1, 1, CF, w), lambda e, b, j: (b, e, j, 0))
    return pl.pallas_call(
        _expert_down_kernel, name="expert_ffn_down",
        grid=(e, b, cap // CF),
        in_specs=[rows(ff), rows(1), pl.BlockSpec((1, 1, ff, GC), lambda e, b, j: (layer, e, 0, group))],
        out_specs=rows(GC),
        out_shape=jax.ShapeDtypeStruct((b, e, cap, GC), F32),
        compiler_params=_cparams(3),
    )(hid4, gate4, w2)


def _sc_combine_call(ye_flat, tok3, seq):
    n_pairs, n_chunks, _ = tok3.shape
    cap = n_chunks * SC_IDX
    d = ye_flat.shape[1]
    nb = n_pairs // N_EXPERTS
    info = plsc.get_sparse_core_info()
    assert info.num_subcores == N_EXPERTS and nb % info.num_cores == 0 and n_chunks % 2 == 0
    assert seq % (info.num_subcores * SC_ZROWS) == 0 and d % SC_SLAB == 0
    batches_per_core = nb // info.num_cores
    own_rows = seq // info.num_subcores
    mesh = plsc.VectorSubcoreMesh(core_axis_name="c", subcore_axis_name="s")

    @functools.partial(
        pl.kernel, mesh=mesh, name="expert_combine_sc",
        compiler_params=pltpu.CompilerParams(needs_layout_passes=False),
        out_type=jax.ShapeDtypeStruct((nb * seq, d), F32),
        scratch_types=[pltpu.VMEM_SHARED((seq, SC_SLAB), F32),
                       pltpu.VMEM((n_chunks, SC_IDX), jnp.int32),
                       pltpu.VMEM((2, SC_IDX, SC_SLAB), F32),
                       pltpu.VMEM((SC_ZROWS, SC_SLAB), F32),
                       pltpu.SemaphoreType.DMA((2,))])
    def combine(ye_hbm, tok_hbm, out_hbm, acc_sh, tok_v, rows_v, zero_v, sem):
        core = lax.axis_index("c")
        sub = lax.axis_index("s")

        @pl.loop(0, SC_ZROWS)
        def _(r):
            for l0 in range(0, SC_SLAB, SC_LANES):
                zero_v[r, pl.ds(l0, SC_LANES)] = jnp.zeros((SC_LANES,), F32)

        for bb in range(batches_per_core):
            batch = core * batches_per_core + bb
            pair = batch * N_EXPERTS + sub
            pltpu.sync_copy(tok_hbm.at[pair], tok_v)

            @pl.loop(0, d // SC_SLAB)
            def _(slab):
                cols = pl.ds(pl.multiple_of(slab * SC_SLAB, SC_SLAB), SC_SLAB)

                @pl.loop(0, own_rows, step=SC_ZROWS)
                def _(r0):
                    pltpu.sync_copy(zero_v, acc_sh.at[pl.ds(sub * own_rows + r0, SC_ZROWS)])

                def load(j, buf):
                    return pltpu.make_async_copy(
                        ye_hbm.at[pl.ds(pair * cap + j * SC_IDX, SC_IDX), cols], rows_v.at[buf], sem.at[buf])

                load(0, 0).start()
                plsc.subcore_barrier()

                for j in range(0, n_chunks, 2):
                    load(j + 1, 1).start()
                    load(j, 0).wait()
                    pltpu.sync_copy(rows_v.at[0], acc_sh.at[tok_v.at[j]], add=True)
                    if j + 2 < n_chunks:
                        load(j + 2, 0).start()
                    load(j + 1, 1).wait()
                    pltpu.sync_copy(rows_v.at[1], acc_sh.at[tok_v.at[j + 1]], add=True)

                plsc.subcore_barrier()
                pltpu.sync_copy(acc_sh.at[pl.ds(sub * own_rows, own_rows)],
                                out_hbm.at[pl.ds(batch * seq + sub * own_rows, own_rows), cols])

    return combine(ye_flat, tok3)


TN = 512


def _resln_kernel(x_ref, *refs, alpha):
    *y_refs, g_ref, b_ref, o_ref = refs
    y = jnp.concatenate([r[...] for r in y_refs], axis=1)
    o_ref[...] = _standardize(alpha * x_ref[...] + y) * g_ref[...] + b_ref[...]


def _resln_call(x2d, y_groups, g, bta, alpha):
    n, d = x2d.shape
    tok = lambda w: pl.BlockSpec((TN, w), lambda i: (i, 0))
    vec = pl.BlockSpec((1, d), lambda i: (0, 0))
    return pl.pallas_call(
        functools.partial(_resln_kernel, alpha=alpha), name="residual_layernorm",
        grid=(n // TN,), in_specs=[tok(d)] + [tok(GC)] * len(y_groups) + [vec, vec], out_specs=tok(d),
        out_shape=jax.ShapeDtypeStruct((n, d), F32),
        compiler_params=_cparams(1),
    )(x2d, *y_groups, g, bta)


def _expert_choice_ffn(x1p, aff_t, w1, w3, w2, layer):
    b, s, half = x1p.shape
    d = 2 * half
    cap = EC_FACTOR * s // N_EXPERTS
    slot = _select_call(aff_t, cap)
    xs, tok, gate = _sc_dispatch_call(x1p.reshape(b * s, half), slot.reshape(b * N_EXPERTS, s),
                                      aff_t.reshape(b * N_EXPERTS, s), s, cap)
    hid = _expert_up_call(xs.reshape(b, N_EXPERTS, cap, half), w1, w3, layer)
    gate4 = gate.reshape(b, N_EXPERTS, cap, 1)
    tok3 = tok.reshape(b * N_EXPERTS, cap // SC_IDX, SC_IDX)
    outs = []
    for group in range(d // GC):
        ye = _expert_down_call(hid, gate4, w2, layer, group)
        outs.append(_sc_combine_call(ye.reshape(b * N_EXPERTS * cap, GC), tok3, s))
    return tuple(outs)


def _pack_pool(pool_w):
    g, gd, _ = pool_w.shape
    out = jnp.zeros((g * gd, g * gd), F32)
    for i in range(g):
        out = out.at[i * gd:(i + 1) * gd, i * gd:(i + 1) * gd].set(pool_w[i])
    return out.astype(BF16)


def _layer(layer, x, pending, alpha, bias_tiles, w_in, b_in, gm_ln_g, gm_ws, gm_bs, ml_conv, ml_fbias,
           ml_norm_g, pool_w, pool_scale, w_branch, w_out, ln1_g, ln1_b, w_router, w_e1, w_e3, w_e2):
    b, s, d = x.shape
    n_small = 2576
    w_cat, b_cat = _pack_inproj_weights(w_in, b_in)
    wscat, bsfull = _pack_gmlp(gm_ws, gm_bs)
    if pending is None:
        outs = _inproj_call(x.reshape(b * s, d), w_cat, b_cat, gm_ln_g[None], wscat, bsfull, b)
    else:
        *outs, x2 = _inproj_call(pending, w_cat, b_cat, gm_ln_g[None], wscat, bsfull, b, alpha)
        x = x2.reshape(b, s, d)
    ya, qkv1, qkv4, qkv16, cqk, cv, co, dx, ig, fg = outs
    r3 = lambda t: t.reshape(b, s, t.shape[-1])
    o_list, l_list = [], []
    for qkv, bias in zip((qkv1, qkv4, qkv16), bias_tiles):
        o, lse = _attn_call(qkv, bias)
        o_list.append(o)
        l_list.append(lse)
    hf, hb = _mlstm_branch(r3(cqk), r3(cv), r3(ig), r3(fg), ml_conv, ml_fbias)
    yd = _pool_call(r3(dx), _pack_pool(pool_w), pool_scale[None])
    x1, x1p, aff_t = _merge_call(
        x, r3(ya), o_list, l_list, hf, hb, r3(co), yd,
        w_in[:, n_small:].astype(BF16), b_in[None, n_small:], w_branch.astype(BF16), w_out.astype(BF16),
        ml_norm_g[:, None], ln1_g[None], ln1_b[None], jnp.transpose(w_router).astype(BF16), alpha)
    ffn = _expert_choice_ffn(x1p, aff_t, w_e1, w_e3, w_e2, layer)
    return x1.reshape(b * s, d), ffn


def kernel(x, w_in, b_in, gm_ln_g, gm_ws, gm_bs, rel_bias, ml_conv, ml_fbias, ml_norm_g, pool_w,
           pool_scale, w_branch, w_out, ln1_g, ln1_b, w_router, w_e1, w_e3, w_e2, ln2_g, ln2_b):
    depth = w_in.shape[0]
    alpha = (2 * depth) ** 0.25
    bias_tiles = [_attn_bias_tile(rel_bias, window, dil) for window, dil in DIL_PATTERNS]
    b, s, d = x.shape
    pending = None
    for l in range(depth):
        x1, ffn = _layer(l, x, pending, alpha, bias_tiles, w_in[l], b_in[l], gm_ln_g[l], gm_ws[l], gm_bs[l],
                         ml_conv[l], ml_fbias[l], ml_norm_g[l], pool_w[l], pool_scale[l], w_branch[l],
                         w_out[l], ln1_g[l], ln1_b[l], w_router[l], w_e1, w_e3, w_e2)
        pending = (x1, ffn, ln2_g[l][None], ln2_b[l][None])
    return _resln_call(*pending, alpha).reshape(b, s, d)
```

```python
import functools
import math

import jax
import jax.numpy as jnp
import numpy as np
from jax import lax
from jax.experimental import pallas as pl
from jax.experimental.pallas import tpu as pltpu
from jax.experimental.pallas import tpu_sc as plsc

F32 = jnp.float32
BF16 = jnp.bfloat16

D_MODEL = 1024
MIX_W = 256
N_BRANCH = 4
GM_CHUNK = 128
GM_GROUPS = 4
ATT_HEADS = 4
ATT_HD = 64
DIL_PATTERNS = ((128, 1), (512, 4), (2048, 16))
ATT_BLOCK = 64
REL_BUCKETS = 32
REL_MAX_DIST = 1024
ML_HEADS = 4
ML_HD = 64
ML_CHUNK = 64
POOL_WINDOWS = (2, 4, 8, 16)
N_EXPERTS = 16
EXPERT_FF = 1024
EC_FACTOR = 2
LN_EPS = 1e-5
NEG_BIG = -1e30

V7X_VMEM_LIMIT = 56 * 1024 * 1024
LANES = 128
HALO = 8


def _cparams(n_grid, vmem=V7X_VMEM_LIMIT):
    return pltpu.CompilerParams(dimension_semantics=("arbitrary",) * n_grid,
                                vmem_limit_bytes=vmem)


def _pack_bf16_pair(lo, hi):
    lo_bits = lax.shift_right_logical(pltpu.bitcast(lo.astype(F32), jnp.int32), 16)
    return pltpu.bitcast(hi.astype(F32), jnp.int32) | lo_bits


def _unpack_bf16_pair(packed):
    lo = pltpu.bitcast(lax.shift_left(packed, 16), F32).astype(BF16)
    hi = pltpu.bitcast(packed & jnp.int32(-65536), F32).astype(BF16)
    return lo, hi


def _standardize(xf):
    mu = jnp.mean(xf, axis=-1, keepdims=True)
    var = jnp.mean(jnp.square(xf - mu), axis=-1, keepdims=True)
    return (xf - mu) * lax.rsqrt(var + LN_EPS)


TA = 512
A_COLS = 2560 + 2 * LANES


def _inproj_kernel(*refs, alpha):
    if alpha is None:
        x_ref, *refs = refs
        x = x_ref[...]
    else:
        n_grp = EXPERT_GROUPS
        x1_ref, y_refs, (g2_ref, b2_ref, *refs) = refs[0], refs[1:1 + n_grp], refs[1 + n_grp:]
        y = sum(r[...] for r in y_refs[1:]) + y_refs[0][...]
        x = _standardize(alpha * x1_ref[...] + y) * g2_ref[...] + b2_ref[...]
        refs[-2][...] = x
        refs = refs[:-2] + refs[-1:]
    (w_ref, b_ref, lng_ref, wscat_ref, bsfull_ref,
     ya_ref, qkv1_ref, qkv4_ref, qkv16_ref, cqk_ref, cv_ref, co_ref, dx_ref, ig_ref, fg_ref, qkv_scr) = refs
    xb = x.astype(BF16)
    h = jnp.dot(xb, w_ref[...], preferred_element_type=F32) + b_ref[...]
    qkv1_ref[0, 0] = h[:, 512:1280].astype(BF16)
    for c in range(768 // LANES):
        qkv_scr[c] = h[:, 512 + c * LANES:512 + (c + 1) * LANES]
    for (_, dil), out_ref in zip(DIL_PATTERNS[1:], (qkv4_ref, qkv16_ref)):
        for r in range(dil):
            for c in range(768 // LANES):
                out_ref[0, r, :, c * LANES:(c + 1) * LANES] = (
                    qkv_scr[c, pl.ds(r, TA // dil, stride=dil), :].astype(BF16))
    cqk_ref[...] = h[:, 1280:1792]
    cv_ref[...] = h[:, 1792:2048].astype(BF16)
    co_ref[...] = h[:, 2048:2304]
    dx_ref[...] = h[:, 2304:2560]
    ig_ref[...] = h[:, 2560:2688]
    fg_ref[...] = h[:, 2688:2816]
    u = jax.nn.gelu(h[:, 0:256])
    v = jax.nn.gelu(h[:, 256:512])
    vn = _standardize(v) * lng_ref[...]
    lane_grp = lax.broadcasted_iota(jnp.int32, (GM_CHUNK, MIX_W), 1) // (MIX_W // GM_GROUPS)
    for c in range(TA // GM_CHUNK):
        vc = vn[c * GM_CHUNK:(c + 1) * GM_CHUNK]
        stacked = jnp.concatenate(
            [jnp.where(lane_grp == g, vc, 0.0).astype(BF16) for g in range(GM_GROUPS)], axis=0)
        mixed = jnp.dot(wscat_ref[...], stacked, preferred_element_type=F32) + bsfull_ref[...]
        ya_ref[c * GM_CHUNK:(c + 1) * GM_CHUNK, :] = (
            u[c * GM_CHUNK:(c + 1) * GM_CHUNK] * mixed).astype(BF16)


def _inproj_call(x_in, w_cat, b_cat, lng, wscat, bsfull, batch, alpha=None):
    fused = alpha is not None
    n = (x_in[0] if fused else x_in).shape[0]
    seq = n // batch
    tpb = seq // TA
    tok = lambda w: pl.BlockSpec((TA, w), lambda i: (i, 0))
    const = lambda s: pl.BlockSpec(s, lambda i: (0,) * len(s))
    regrouped = lambda dil: pl.BlockSpec((1, dil, TA // dil, 768), lambda i: (i // tpb, 0, i % tpb, 0))
    out_shape = (
        jax.ShapeDtypeStruct((n, 256), BF16),
    ) + tuple(jax.ShapeDtypeStruct((batch, dil, seq // dil, 768), BF16)
              for _, dil in DIL_PATTERNS) + (
        jax.ShapeDtypeStruct((n, 512), F32),
        jax.ShapeDtypeStruct((n, 256), BF16),
        jax.ShapeDtypeStruct((n, 256), F32),
        jax.ShapeDtypeStruct((n, 256), F32),
        jax.ShapeDtypeStruct((n, LANES), F32),
        jax.ShapeDtypeStruct((n, LANES), F32),
    )
    if fused:
        x1, ffn_groups, ln_g, ln_b = x_in
        x_args = (x1, *ffn_groups, ln_g, ln_b)
        x_specs = ([tok(D_MODEL)] * (1 + len(ffn_groups)) + [const((1, D_MODEL)), const((1, D_MODEL))])
    else:
        x_args, x_specs = (x_in,), [tok(D_MODEL)]
    out_specs = ((tok(256),) + tuple(regrouped(dil) for _, dil in DIL_PATTERNS)
                 + (tok(512), tok(256), tok(256), tok(256), tok(LANES), tok(LANES)))
    if fused:
        out_specs += (tok(D_MODEL),)
        out_shape += (jax.ShapeDtypeStruct((n, D_MODEL), F32),)
    return pl.pallas_call(
        functools.partial(_inproj_kernel, alpha=alpha), name="inproj_gmlp",
        grid=(n // TA,),
        in_specs=x_specs + [const((D_MODEL, A_COLS)), const((1, A_COLS)), const((1, MIX_W)),
                            const((GM_CHUNK, GM_GROUPS * GM_CHUNK)), const((GM_CHUNK, MIX_W))],
        out_specs=out_specs,
        out_shape=out_shape,
        scratch_shapes=[pltpu.VMEM((768 // LANES, TA, LANES), F32)],
        compiler_params=_cparams(1),
    )(*x_args, w_cat, b_cat, lng, wscat, bsfull)


def _pack_inproj_weights(w_in, b_in):
    pad = lambda a: jnp.pad(a, ((0, 0), (0, LANES - 8)))
    w_cat = jnp.concatenate([w_in[:, 0:2304], w_in[:, 2320:2576],
                             pad(w_in[:, 2304:2312]), pad(w_in[:, 2312:2320])], axis=1)
    b2 = b_in[None, :]
    b_cat = jnp.concatenate([b2[:, 0:2304], b2[:, 2320:2576],
                             pad(b2[:, 2304:2312]), pad(b2[:, 2312:2320])], axis=1)
    return w_cat.astype(BF16), b_cat


def _pack_gmlp(gm_ws, gm_bs):
    wscat = jnp.transpose(gm_ws, (1, 0, 2)).reshape(GM_CHUNK, GM_GROUPS * GM_CHUNK).astype(BF16)
    bsfull = jnp.repeat(jnp.transpose(gm_bs), MIX_W // GM_GROUPS, axis=1)
    return wscat, bsfull


def _halo_specs(t, width, n_tiles):
    r = t // HALO
    main = pl.BlockSpec((1, t, width), lambda b, i: (b, i, 0))
    prev = pl.BlockSpec((1, HALO, width), lambda b, i: (b, jnp.maximum(i * r - 1, 0), 0))
    nxt = pl.BlockSpec((1, HALO, width), lambda b, i: (b, jnp.minimum((i + 1) * r, n_tiles * r - 1), 0))
    return main, prev, nxt


def _fill_halo_scratch(buf, x_ref, p_ref, n_ref, t):
    i = pl.program_id(1)
    last = pl.num_programs(1) - 1
    buf[0:HALO, :] = jnp.where(i > 0, p_ref[0], 0.0)
    buf[HALO:HALO + t, :] = x_ref[0]
    buf[HALO + t:2 * HALO + t, :] = jnp.where(i < last, n_ref[0], 0.0)


TP = 512


def _pool_kernel(x_ref, p_ref, n_ref, w_ref, sc_ref, o_ref, buf):
    _fill_halo_scratch(buf, x_ref, p_ref, n_ref, TP)
    seq = pl.num_programs(1) * TP
    pos = pl.program_id(1) * TP + lax.broadcasted_iota(jnp.int32, (TP, 1), 0)
    lane_grp = lax.broadcasted_iota(jnp.int32, (TP, MIX_W), 1) // (MIX_W // len(POOL_WINDOWS))
    x0 = buf[HALO:HALO + TP, :]
    pooled = jnp.zeros((TP, MIX_W), F32)
    acc = None
    half_done = 0
    for gi, win in enumerate(POOL_WINDOWS):
        half = win // 2
        for o in list(range(-half, -half_done)) + list(range(half_done, half)):
            term = buf[HALO + o:HALO + o + TP, :]
            acc = term if acc is None else acc + term
        half_done = half
        cnt = (jnp.minimum(pos + half, seq) - jnp.maximum(pos - half, 0)).astype(F32)
        pooled = jnp.where(lane_grp == gi, acc / cnt - x0, pooled)
    mixed = jnp.dot(pooled.astype(BF16), w_ref[...], preferred_element_type=F32)
    o_ref[0] = (mixed * sc_ref[...]).astype(BF16)


def _pool_call(dx, w_block, scale):
    b, s, _ = dx.shape
    nt = s // TP
    main, prev, nxt = _halo_specs(TP, MIX_W, nt)
    return pl.pallas_call(
        _pool_kernel, name="pool_mixer",
        grid=(b, nt),
        in_specs=[main, prev, nxt,
                  pl.BlockSpec((MIX_W, MIX_W), lambda b, i: (0, 0)),
                  pl.BlockSpec((1, MIX_W), lambda b, i: (0, 0))],
        out_specs=pl.BlockSpec((1, TP, MIX_W), lambda b, i: (b, i, 0)),
        out_shape=jax.ShapeDtypeStruct((b, s, MIX_W), BF16),
        scratch_shapes=[pltpu.VMEM((TP + 2 * HALO, MIX_W), F32)],
        compiler_params=_cparams(2),
    )(dx, dx, dx, w_block, scale)


TQ = 128
TQS = 512
TKEYS = TQ + 2 * ATT_BLOCK


def _attn_kernel(q_ref, kp_ref, km_ref, kn_ref, vp_ref, vm_ref, vn_ref, bias_ref, o_ref, lse_ref):
    i = pl.program_id(2)
    seq = pl.num_programs(2) * TQS
    q = q_ref[0, 0] * ATT_HD ** -0.5
    k = jnp.concatenate([kp_ref[0, 0], km_ref[0, 0], kn_ref[0, 0]], axis=0)
    v = jnp.concatenate([vp_ref[0, 0], vm_ref[0, 0], vn_ref[0, 0]], axis=0)
    lane = lax.broadcasted_iota(jnp.int32, (TQ, LANES), 1)
    lane_half = lax.broadcasted_iota(jnp.int32, (1, LANES), 1) // ATT_HD
    keep = [jnp.where(lane_half == hh, 1.0, 0.0).astype(BF16) for hh in range(2)]
    for j in range(TQS // TQ):
        kpos = i * TQS + j * TQ - ATT_BLOCK + lax.broadcasted_iota(jnp.int32, (1, TKEYS), 1)
        kvalid = (kpos >= 0) & (kpos < seq)
        qrows = slice(j * TQ, (j + 1) * TQ)
        krows = slice(j * TQ, j * TQ + TKEYS)
        lse_tile = jnp.zeros((TQ, LANES), F32)
        for pair in range(ATT_HEADS // 2):
            grp = slice(pair * LANES, (pair + 1) * LANES)
            q_pair, k_pair, v_pair = q[qrows, grp], k[krows, grp], v[krows, grp]
            o_pair = jnp.zeros((TQ, LANES), F32)
            for hh in range(2):
                h = 2 * pair + hh
                logits = lax.dot_general(q_pair * keep[hh], k_pair, (((1,), (1,)), ((), ())),
                                         preferred_element_type=F32) + bias_ref[h]
                logits = jnp.where(kvalid, logits, NEG_BIG)
                m = jnp.max(logits, axis=-1, keepdims=True)
                p = jnp.exp(logits - m)
                ssum = jnp.sum(p, axis=-1, keepdims=True)
                o = jnp.dot(p.astype(BF16), v_pair, preferred_element_type=F32) / ssum
                o_pair = jnp.where(lane_half == hh, o, o_pair)
                lse_tile = jnp.where(lane == h, m + jnp.log(ssum), lse_tile)
            o_ref[0, 0, qrows, grp] = o_pair
        lse_ref[0, 0, qrows, :] = lse_tile


def _attn_call(qkv, bias):
    b, dil, l, _ = qkv.shape
    nt = l // TQS
    r64 = TQS // ATT_BLOCK
    main = lambda c: pl.BlockSpec((1, 1, TQS, MIX_W), lambda b, r, i: (b, r, i, c))
    prev = lambda c: pl.BlockSpec((1, 1, ATT_BLOCK, MIX_W),
                                  lambda b, r, i: (b, r, jnp.maximum(i * r64 - 1, 0), c))
    nxt = lambda c: pl.BlockSpec((1, 1, ATT_BLOCK, MIX_W),
                                 lambda b, r, i: (b, r, jnp.minimum((i + 1) * r64, nt * r64 - 1), c))
    return pl.pallas_call(
        _attn_kernel, name="band_attention",
        grid=(b, dil, nt),
        in_specs=[main(0), prev(1), main(1), nxt(1), prev(2), main(2), nxt(2),
                  pl.BlockSpec((ATT_HEADS, TQ, TKEYS), lambda b, r, i: (0, 0, 0))],
        out_specs=(pl.BlockSpec((1, 1, TQS, MIX_W), lambda b, r, i: (b, r, i, 0)),
                   pl.BlockSpec((1, 1, TQS, LANES), lambda b, r, i: (b, r, i, 0))),
        out_shape=(jax.ShapeDtypeStruct((b, dil, l, MIX_W), F32),
                   jax.ShapeDtypeStruct((b, dil, l, LANES), F32)),
        compiler_params=_cparams(3),
    )(qkv, qkv, qkv, qkv, qkv, qkv, qkv, bias)


def _t5_bucket_static(rel):
    half = REL_BUCKETS // 2
    max_exact = half // 2
    ret = np.where(rel > 0, half, 0)
    n = np.abs(rel)
    nf = np.maximum(n, 1).astype(np.float32)
    large = max_exact + (np.log(nf / np.float32(max_exact)) / np.float32(math.log(REL_MAX_DIST / max_exact))
                         * np.float32(half - max_exact)).astype(np.int32)
    large = np.minimum(large, half - 1)
    return ret + np.where(n < max_exact, n, large)


def _attn_bias_tile(rel_bias, window, dil):
    side = (window // 2) // dil
    rel = np.arange(TKEYS)[None, :] - ATT_BLOCK - np.arange(TQ)[:, None]
    onehot = jax.nn.one_hot(jnp.asarray(_t5_bucket_static(dil * rel), jnp.int32), REL_BUCKETS, dtype=F32)
    bias = jnp.einsum('qkr,rh->hqk', onehot, rel_bias, precision=lax.Precision.HIGHEST)
    return jnp.where(jnp.asarray(np.abs(rel) <= side)[None], bias, NEG_BIG)


TM = 512
VT_ROWS = ML_HD + 16


def _mlprep_kernel(x_ref, p_ref, n_ref, v_ref, w_ref, qt_out, k_out, vt_out, buf):
    _fill_halo_scratch(buf, x_ref, p_ref, n_ref, TM)
    conv = (buf[HALO - 1:HALO - 1 + TM, :] * w_ref[0:1, :] + buf[HALO:HALO + TM, :] * w_ref[1:2, :]
            + buf[HALO + 1:HALO + 1 + TM, :] * w_ref[2:3, :])
    qk = jax.nn.silu(conv)
    qt = jnp.transpose(qk[:, :MIX_W])
    vt = jnp.transpose(v_ref[0].astype(F32))
    ones_rows = jnp.where(lax.broadcasted_iota(jnp.int32, (VT_ROWS - ML_HD, ML_CHUNK), 0) == 0, 1.0, 0.0)
    for h in range(ML_HEADS):
        sl = slice(h * ML_HD, (h + 1) * ML_HD)
        k_out[0, h] = (qk[:, MIX_W + h * ML_HD:MIX_W + (h + 1) * ML_HD] * ML_HD ** -0.5).astype(BF16)
        for c in range(TM // ML_CHUNK):
            cl = slice(c * ML_CHUNK, (c + 1) * ML_CHUNK)
            qt_out[0, h, c] = qt[sl, cl].astype(BF16)
            vt_out[0, h, c] = jnp.concatenate([vt[sl, cl], ones_rows], axis=0).astype(BF16)


def _mlprep_call(cqk, cv, conv_w):
    b, s, _ = cqk.shape
    nt = s // TM
    nc = s // ML_CHUNK
    cpt = TM // ML_CHUNK
    main, prev, nxt = _halo_specs(TM, 2 * MIX_W, nt)
    return pl.pallas_call(
        _mlprep_kernel, name="mlstm_prep",
        grid=(b, nt),
        in_specs=[main, prev, nxt,
                  pl.BlockSpec((1, TM, MIX_W), lambda b, i: (b, i, 0)),
                  pl.BlockSpec((3, 2 * MIX_W), lambda b, i: (0, 0))],
        out_specs=(pl.BlockSpec((1, ML_HEADS, cpt, ML_HD, ML_CHUNK), lambda b, i: (b, 0, i, 0, 0)),
                   pl.BlockSpec((1, ML_HEADS, TM, ML_HD), lambda b, i: (b, 0, i, 0)),
                   pl.BlockSpec((1, ML_HEADS, cpt, VT_ROWS, ML_CHUNK), lambda b, i: (b, 0, i, 0, 0))),
        out_shape=(jax.ShapeDtypeStruct((b, ML_HEADS, nc, ML_HD, ML_CHUNK), BF16),
                   jax.ShapeDtypeStruct((b, ML_HEADS, s, ML_HD), BF16),
                   jax.ShapeDtypeStruct((b, ML_HEADS, nc, VT_ROWS, ML_CHUNK), BF16)),
        scratch_shapes=[pltpu.VMEM((TM + 2 * HALO, 2 * MIX_W), F32)],
        compiler_params=_cparams(2),
    )(cqk, cqk, cqk, cv, conv_w)


GP_BLK = 512


def _chunk_scan(x, fwd_lane, t_in_chunk, op, ident):
    n = x.shape[0]
    shift = 1
    while shift < ML_CHUNK:
        down = pltpu.roll(x, shift, 0)
        up = pltpu.roll(x, n - shift, 0)
        nb = jnp.where(fwd_lane,
                       jnp.where(t_in_chunk >= shift, down, ident),
                       jnp.where(t_in_chunk < ML_CHUNK - shift, up, ident))
        x = op(x, nb)
        shift *= 2
    return x


def _gatescan_kernel(ig_ref, fg_ref, fb_ref, b_ref, a_ref, cm_ref, g_ref, amax_ref):
    lane = lax.broadcasted_iota(jnp.int32, (1, LANES), 1)
    fwd_lane = lane < ML_HEADS
    t_in_chunk = lax.broadcasted_iota(jnp.int32, (GP_BLK, 1), 0) % ML_CHUNK
    z = fg_ref[0] + fb_ref[...]
    lf = jnp.minimum(z, 0.0) - jnp.log1p(jnp.exp(-jnp.abs(z)))
    b = _chunk_scan(lf, fwd_lane, t_in_chunk, jnp.add, 0.0)
    a = ig_ref[0] - b
    b_ref[0] = b
    a_ref[0] = a
    cm_ref[0] = _chunk_scan(a, fwd_lane, t_in_chunk, jnp.maximum, -jnp.inf)
    cpb = GP_BLK // ML_CHUNK
    last = pl.ds(ML_CHUNK - 1, cpb, stride=ML_CHUNK)
    first = pl.ds(0, cpb, stride=ML_CHUNK)
    g_ref[0] = jnp.where(fwd_lane, b_ref[0, last, :], b_ref[0, first, :])
    amax_ref[0] = jnp.where(fwd_lane, cm_ref[0, last, :], cm_ref[0, first, :])


def _gateout_kernel(b_ref, a_ref, cm_ref, g_ref, amax_ref, ws_ref, m_ref, iw_ref, en_ref, dec_ref,
                    mch_s, mlast_s):
    nc = g_ref.shape[1]
    j = pl.program_id(1)
    fwd_lane = lax.broadcasted_iota(jnp.int32, (1, LANES), 1) < ML_HEADS

    @pl.when(j == 0)
    def _():
        def m_step(i, carry):
            mf, mb = carry
            cf = pl.ds(i, 1)
            cb = pl.ds(nc - 1 - i, 1)
            mch_s[cf, :] = jnp.where(fwd_lane, mf, mch_s[cf, :])
            mch_s[cb, :] = jnp.where(fwd_lane, mch_s[cb, :], mb)
            mf = g_ref[0, cf, :] + jnp.maximum(mf, amax_ref[0, cf, :])
            mb = g_ref[0, cb, :] + jnp.maximum(mb, amax_ref[0, cb, :])
            return mf, mb

        mch_s[...] = jnp.zeros((nc, LANES), F32)
        zero = jnp.zeros((1, LANES), F32)
        lax.fori_loop(0, nc, m_step, (zero, zero))
        mlast = jnp.maximum(amax_ref[0], mch_s[...])
        mlast_s[...] = mlast
        dec_ref[0] = jnp.exp(mch_s[...] - mlast)

    cpb = GP_BLK // ML_CHUNK
    crow = pl.ds(pl.multiple_of(j * cpb, cpb), cpb)
    expand = lambda t: jnp.broadcast_to(t[:, None, :], (cpb, ML_CHUNK, LANES)).reshape(GP_BLK, LANES)
    m_tok = expand(mch_s[crow, :])
    mlast_tok = expand(mlast_s[crow, :])
    mt = jnp.maximum(cm_ref[0], m_tok)
    m_ref[0] = mt
    iw_ref[0] = jnp.exp(m_tok - mt)
    en_ref[0] = jnp.exp(-(b_ref[0] + mt))
    ws_ref[0] = jnp.exp(a_ref[0] - mlast_tok)


def _gateprep_call(ig, fg, fbias_row):
    b, s, _ = ig.shape
    nc = s // ML_CHUNK
    cpb = GP_BLK // ML_CHUNK
    tok = pl.BlockSpec((1, GP_BLK, LANES), lambda i, j: (i, j, 0))
    tok_shape = jax.ShapeDtypeStruct((b, s, LANES), F32)
    chunk_shape = jax.ShapeDtypeStruct((b, nc, LANES), F32)
    chunk_tile = pl.BlockSpec((1, cpb, LANES), lambda i, j: (i, j, 0))
    chunk_all = pl.BlockSpec((1, nc, LANES), lambda i, j: (i, 0, 0))
    bcum, a, cm, g, amax = pl.pallas_call(
        _gatescan_kernel, name="mlstm_gate_scan",
        grid=(b, s // GP_BLK),
        in_specs=[tok, tok, pl.BlockSpec((1, LANES), lambda i, j: (0, 0))],
        out_specs=(tok, tok, tok, chunk_tile, chunk_tile),
        out_shape=(tok_shape,) * 3 + (chunk_shape,) * 2,
        compiler_params=_cparams(2),
    )(ig, fg, fbias_row)
    ws, m_col, iw_col, en_col, decay = pl.pallas_call(
        _gateout_kernel, name="mlstm_gate_out",
        grid=(b, s // GP_BLK),
        in_specs=[tok, tok, tok, chunk_all, chunk_all],
        out_specs=(tok, tok, tok, tok, chunk_all),
        out_shape=(tok_shape,) * 4 + (chunk_shape,),
        scratch_shapes=[pltpu.VMEM((nc, LANES), F32)] * 2,
        compiler_params=_cparams(2),
    )(bcum, a, cm, g, amax)
    return a, ws, m_col, iw_col, en_col, decay


TE = 1024


def _mlstm_kernel(*refs):
    fwd, bwd, (hf_ref, hb_ref, state) = refs[:9], refs[9:18], refs[18:]
    i = pl.program_id(1)

    @pl.when(i == 0)
    def _():
        state[...] = jnp.zeros(state.shape, F32)

    cpt = TE // ML_CHUNK
    s_idx = lax.broadcasted_iota(jnp.int32, (ML_CHUNK, ML_CHUNK), 0)
    t_idx = lax.broadcasted_iota(jnp.int32, (ML_CHUNK, ML_CHUNK), 1)

    def chunk_body(c, carry):
        jobs = []
        for d, (qt_r, k_r, vt_r, a_r, m_r, iw_r, en_r, ws_r, dec_r), out_r in ((0, fwd, hf_ref), (1, bwd, hb_ref)):
            cc = c if d == 0 else cpt - 1 - c
            rows = pl.ds(pl.multiple_of(cc * ML_CHUNK, ML_CHUNK), ML_CHUNK)
            crow = pl.ds(cc, 1)
            for h in range(ML_HEADS):
                ch = d * ML_HEADS + h
                row = lambda r: r[0, ch, crow, :]
                jobs.append(dict(
                    ch=ch, tri=(s_idx >= t_idx) if d else (s_idx <= t_idx),
                    k=k_r[0, h, rows, :], qt=qt_r[0, h, cc], vt=vt_r[0, h, cc], a=a_r[0, rows, ch:ch + 1],
                    m=row(m_r), iw=row(iw_r), en=row(en_r), ws=row(ws_r), dec=row(dec_r),
                    out=(out_r, cc, h)))
        for j in jobs:
            j["cst"] = state[j["ch"]]
            j["st"] = jnp.dot(j["k"], j["qt"], preferred_element_type=F32)
            j["inter"] = jnp.dot(j["cst"].astype(BF16), j["qt"], preferred_element_type=F32)
            j["upd"] = jnp.dot((j["vt"].astype(F32) * j["ws"]).astype(BF16), j["k"],
                               preferred_element_type=F32)
        for j in jobs:
            j["swt"] = j["st"] * jnp.exp(jnp.where(j["tri"], j["a"] - j["m"], NEG_BIG))
            j["intra"] = jnp.dot(j["vt"], j["swt"].astype(BF16), preferred_element_type=F32)
        for j in jobs:
            den = jnp.sum(j["swt"], axis=0, keepdims=True) + j["iw"] * j["inter"][ML_HD:ML_HD + 1]
            tot = j["intra"][:ML_HD] + j["iw"] * j["inter"][:ML_HD]
            out_r, cc, h = j["out"]
            out_r[0, cc, h * ML_HD:(h + 1) * ML_HD, :] = tot / jnp.maximum(jnp.abs(den), j["en"])
            state[j["ch"]] = j["dec"] * j["cst"] + j["upd"]
        return carry

    lax.fori_loop(0, cpt, chunk_body, 0)


def _mlstm_call(qt, k, vt, a_col, m_row, iw_row, en_row, ws_row, dec_row):
    b, _, s, _ = k.shape
    nt = s // TE
    nc = s // ML_CHUNK
    cpt = TE // ML_CHUNK

    def specs(rev):
        ti = (lambda i: nt - 1 - i) if rev else (lambda i: i)
        row = pl.BlockSpec((1, 2 * ML_HEADS, cpt, ML_CHUNK), lambda b, i: (b, 0, ti(i), 0))
        return [
            pl.BlockSpec((1, ML_HEADS, cpt, ML_HD, ML_CHUNK), lambda b, i: (b, 0, ti(i), 0, 0)),
            pl.BlockSpec((1, ML_HEADS, TE, ML_HD), lambda b, i: (b, 0, ti(i), 0)),
            pl.BlockSpec((1, ML_HEADS, cpt, VT_ROWS, ML_CHUNK), lambda b, i: (b, 0, ti(i), 0, 0)),
            pl.BlockSpec((1, TE, LANES), lambda b, i: (b, ti(i), 0)),
            row, row, row, row, row]

    args = [qt, k, vt, a_col, m_row, iw_row, en_row, ws_row, dec_row]
    out_f = pl.BlockSpec((1, cpt, MIX_W, ML_CHUNK), lambda b, i: (b, i, 0, 0))
    out_b = pl.BlockSpec((1, cpt, MIX_W, ML_CHUNK), lambda b, i: (b, nt - 1 - i, 0, 0))
    return pl.pallas_call(
        _mlstm_kernel, name="mlstm_scan",
        grid=(b, nt),
        in_specs=specs(False) + specs(True),
        out_specs=(out_f, out_b),
        out_shape=(jax.ShapeDtypeStruct((b, nc, MIX_W, ML_CHUNK), F32),) * 2,
        scratch_shapes=[pltpu.VMEM((2 * ML_HEADS, VT_ROWS, ML_HD), F32)],
        compiler_params=_cparams(2),
    )(*args, *args)


def _mlstm_branch(cqk, cv, ig, fg, conv_w, fbias):
    b, s, _ = cqk.shape
    nc = s // ML_CHUNK
    qt, k, vt = _mlprep_call(cqk, cv, conv_w)
    fb_row = jnp.pad(fbias.reshape(1, 2 * ML_HEADS), ((0, 0), (0, LANES - 2 * ML_HEADS)))
    a, ws, m_col, iw_col, en_col, decay = _gateprep_call(ig, fg, fb_row)
    rows = lambda t: jnp.transpose(t[..., :2 * ML_HEADS], (0, 2, 1)).reshape(b, 2 * ML_HEADS, nc, ML_CHUNK)
    dec_row = jnp.broadcast_to(jnp.transpose(decay[..., :2 * ML_HEADS], (0, 2, 1))[..., None],
                               (b, 2 * ML_HEADS, nc, ML_CHUNK))
    return _mlstm_call(qt, k, vt, a, rows(m_col), rows(iw_col), rows(en_col), rows(ws), dec_row)


TF = 512


def _merge_kernel(x_ref, ya_ref, o1_ref, o2_ref, o3_ref, l1_ref, l2_ref, l3_ref, hf_ref, hb_ref,
                  co_ref, yd_ref, wg_ref, bg_ref, wbr_ref, wout_ref, mng_ref, lng_ref, lnb_ref,
                  wr_ref, x1_ref, x1p_ref, aff_ref, o_scr, l_scr, *, alpha):
    x = x_ref[0]
    xb = x.astype(BF16)

    def natural_order(src_ref, scr):
        dil, width = src_ref.shape[1], src_ref.shape[3]
        if dil == 1:
            return src_ref[0, 0]
        for r in range(dil):
            for c in range(width // LANES):
                scr[c, pl.ds(r, TF // dil, stride=dil), :] = src_ref[0, r, :, c * LANES:(c + 1) * LANES]
        return jnp.concatenate([scr[c] for c in range(width // LANES)], axis=1)

    lane_head = lax.broadcasted_iota(jnp.int32, (TF, MIX_W), 1) // ML_HD
    l1, l2, l3 = [natural_order(r, l_scr.at[p]) for p, r in enumerate((l1_ref, l2_ref, l3_ref))]
    o1, o2, o3 = [natural_order(r, o_scr.at[p]) for p, r in enumerate((o1_ref, o2_ref, o3_ref))]
    lm = jnp.maximum(jnp.maximum(l1, l2), l3)
    e1, e2, e3 = jnp.exp(l1 - lm), jnp.exp(l2 - lm), jnp.exp(l3 - lm)
    inv = 1.0 / (e1 + e2 + e3)

    def per_head(w):
        out = jnp.zeros((TF, MIX_W), F32)
        for h in range(ATT_HEADS):
            out = jnp.where(lane_head == h, w[:, h:h + 1], out)
        return out

    y_b = per_head(e1 * inv) * o1 + per_head(e2 * inv) * o2 + per_head(e3 * inv) * o3
    hsum_t = jnp.concatenate([hf_ref[0, c] + hb_ref[0, c] for c in range(TF // ML_CHUNK)], axis=1)
    per_head_rows = hsum_t.reshape(ML_HEADS, ML_HD, TF)
    mu = jnp.mean(per_head_rows, axis=1, keepdims=True)
    cen = per_head_rows - mu
    var = jnp.mean(cen * cen, axis=1, keepdims=True)
    hn_t = (cen * lax.rsqrt(var + LN_EPS)).reshape(MIX_W, TF)
    y_c_t = (jax.nn.sigmoid(jnp.transpose(co_ref[0])) * (hn_t * mng_ref[...])).astype(BF16)
    ys = (ya_ref[0], y_b.astype(BF16), None, yd_ref[0])
    merged = jnp.zeros((TF, D_MODEL), F32)
    for n in range(N_BRANCH):
        cols = slice(n * D_MODEL, (n + 1) * D_MODEL)
        gate = jax.nn.sigmoid(jnp.dot(xb, wg_ref[:, cols], preferred_element_type=F32) + bg_ref[:, cols])
        if ys[n] is None:
            proj = lax.dot_general(y_c_t, wbr_ref[n], (((0,), (0,)), ((), ())), preferred_element_type=F32)
        else:
            proj = jnp.dot(ys[n], wbr_ref[n], preferred_element_type=F32)
        merged = merged + gate * proj
    mix = jnp.dot(merged.astype(BF16), wout_ref[...], preferred_element_type=F32)
    x1 = _standardize(alpha * x + mix) * lng_ref[...] + lnb_ref[...]
    x1_ref[0] = x1
    x1b = x1.astype(BF16)
    x1p_ref[0] = _pack_bf16_pair(x1b[:, :D_MODEL // 2], x1b[:, D_MODEL // 2:])
    logits = lax.dot_general(wr_ref[...], x1b, (((1,), (1,)), ((), ())),
                             preferred_element_type=F32)
    ex = jnp.exp(logits - jnp.max(logits, axis=0, keepdims=True))
    aff_ref[0] = ex / jnp.sum(ex, axis=0, keepdims=True)


def _merge_call(x, ya, o_list, l_list, hf, hb, co, yd, wg, bg, wbr, wout, mng, lng, lnb, wr_t, alpha):
    b, s, _ = x.shape
    tok = lambda w: pl.BlockSpec((1, TF, w), lambda b, i: (b, i, 0))
    grouped = lambda dil, w: pl.BlockSpec((1, dil, TF // dil, w), lambda b, i: (b, 0, i, 0))
    chunked = pl.BlockSpec((1, TF // ML_CHUNK, MIX_W, ML_CHUNK), lambda b, i: (b, i, 0, 0))
    const = lambda shp: pl.BlockSpec(shp, lambda b, i: (0,) * len(shp))
    return pl.pallas_call(
        functools.partial(_merge_kernel, alpha=alpha), name="merge_ln_router",
        grid=(b, s // TF),
        in_specs=[tok(D_MODEL), tok(MIX_W)] + [grouped(dil, MIX_W) for _, dil in DIL_PATTERNS]
                 + [grouped(dil, LANES) for _, dil in DIL_PATTERNS]
                 + [chunked, chunked, tok(MIX_W), tok(MIX_W)]
                 + [const((D_MODEL, N_BRANCH * D_MODEL)), const((1, N_BRANCH * D_MODEL)),
                    const((N_BRANCH, MIX_W, D_MODEL)), const((D_MODEL, D_MODEL)), const((MIX_W, 1)),
                    const((1, D_MODEL)), const((1, D_MODEL)), const((N_EXPERTS, D_MODEL))],
        out_specs=(tok(D_MODEL), tok(D_MODEL // 2), pl.BlockSpec((1, N_EXPERTS, TF), lambda b, i: (b, 0, i))),
        out_shape=(jax.ShapeDtypeStruct((b, s, D_MODEL), F32),
                   jax.ShapeDtypeStruct((b, s, D_MODEL // 2), jnp.int32),
                   jax.ShapeDtypeStruct((b, N_EXPERTS, s), F32)),
        scratch_shapes=[pltpu.VMEM((len(DIL_PATTERNS), MIX_W // LANES, TF, LANES), F32),
                        pltpu.VMEM((len(DIL_PATTERNS), 1, TF, LANES), F32)],
        compiler_params=_cparams(2),
    )(x, ya, *o_list, *l_list, hf, hb, co, yd, wg, bg, wbr, wout, mng, lng, lnb, wr_t)


TT = 256


def _select_kernel(aff_ref, slot_ref, *, cap):
    s = aff_ref.shape[2]
    bits = pltpu.bitcast(aff_ref[0], jnp.int32)

    def bit_step(i, thr):
        cand = thr | jnp.left_shift(jnp.int32(1), 30 - i)
        cnt = jnp.sum((bits >= cand).astype(jnp.int32), axis=1, keepdims=True)
        return jnp.where(cnt >= cap, cand, thr)

    thr = lax.fori_loop(0, 31, bit_step, jnp.zeros((N_EXPERTS, 1), jnp.int32))
    gt = bits > thr
    eq = bits == thr
    need = (cap - jnp.sum(gt.astype(jnp.int32), axis=1, keepdims=True)).astype(F32)
    upper = (lax.broadcasted_iota(jnp.int32, (TT, TT), 0)
             <= lax.broadcasted_iota(jnp.int32, (TT, TT), 1)).astype(BF16)
    eq_before = jnp.zeros((N_EXPERTS, 1), F32)
    sel_before = jnp.zeros((N_EXPERTS, 1), F32)
    for j in range(s // TT):
        cols = slice(j * TT, (j + 1) * TT)
        eq_j = eq[:, cols]
        eq_incl = eq_before + jnp.dot(eq_j.astype(BF16), upper, preferred_element_type=F32)
        sel_j = gt[:, cols] | (eq_j & (eq_incl <= need))
        sel_f = sel_j.astype(F32)
        sel_incl = sel_before + jnp.dot(sel_f.astype(BF16), upper, preferred_element_type=F32)
        slot_ref[0, :, cols] = jnp.where(sel_j, sel_incl - 1.0, -1.0).astype(jnp.int32)
        eq_before = eq_incl[:, TT - 1:TT]
        sel_before = sel_incl[:, TT - 1:TT]


def _select_call(aff_t, cap):
    b, e, s = aff_t.shape
    return pl.pallas_call(
        functools.partial(_select_kernel, cap=cap), name="expert_choice_select",
        grid=(b,),
        in_specs=[pl.BlockSpec((1, e, s), lambda i: (i, 0, 0))],
        out_specs=pl.BlockSpec((1, e, s), lambda i: (i, 0, 0)),
        out_shape=jax.ShapeDtypeStruct((b, e, s), jnp.int32),
        compiler_params=_cparams(1),
    )(aff_t)


SC_LANES = 16
SC_ROWS = 64
SC_IDX = 128
SC_SLAB = 128
SC_ZROWS = 64
CF = 1024
EXPERT_GROUPS = 2


def _sc_dispatch_call(x_flat, slot2, aff2, seq, cap, experts):
    n_pairs = slot2.shape[0]
    d = x_flat.shape[1]
    info = plsc.get_sparse_core_info()
    n_workers = info.num_cores * info.num_subcores
    assert n_workers % n_pairs == 0 and seq % SC_LANES == 0
    parts = n_workers // n_pairs
    part_rows = cap // parts
    assert part_rows % (2 * SC_ROWS) == 0
    mesh = plsc.VectorSubcoreMesh(core_axis_name="c", subcore_axis_name="s")

    @functools.partial(
        pl.kernel, mesh=mesh, name="expert_dispatch_sc",
        compiler_params=pltpu.CompilerParams(needs_layout_passes=False),
        out_type=(jax.ShapeDtypeStruct((n_pairs * cap, d), x_flat.dtype),
                  jax.ShapeDtypeStruct((n_pairs, cap), jnp.int32),
                  jax.ShapeDtypeStruct((n_pairs, cap), F32)),
        scratch_types=[pltpu.VMEM((seq,), jnp.int32), pltpu.VMEM((seq,), F32),
                       pltpu.VMEM((cap,), jnp.int32), pltpu.VMEM((cap,), jnp.int32),
                       pltpu.VMEM((cap,), F32),
                       pltpu.VMEM((2, SC_ROWS, d), x_flat.dtype), pltpu.SemaphoreType.DMA((2,))])
    def dispatch(x_hbm, slot_hbm, aff_hbm, xs_hbm, tok_hbm, gate_hbm,
                 slot_v, aff_v, idx_v, tok_v, gate_v, rows_v, sem):
        worker = lax.axis_index("s") * info.num_cores + lax.axis_index("c")
        lane = lax.iota(jnp.int32, SC_LANES)

        def gather(c0, buf):
            return pltpu.make_async_copy(x_hbm.at[idx_v.at[pl.ds(c0, SC_ROWS)]], rows_v.at[buf], sem.at[buf])
        pair = worker // parts
        first = (worker % parts) * part_rows
        row0 = (pair // experts) * seq
        pltpu.sync_copy(slot_hbm.at[pair], slot_v)
        pltpu.sync_copy(aff_hbm.at[pair], aff_v)

        @pl.loop(0, seq, step=SC_LANES)
        def _(t0):
            sv = slot_v[pl.ds(t0, SC_LANES)]
            picked = sv >= 0
            plsc.store_scatter(tok_v, [sv], t0 + lane, mask=picked)
            plsc.store_scatter(idx_v, [sv], row0 + t0 + lane, mask=picked)
            plsc.store_scatter(gate_v, [sv], aff_v[pl.ds(t0, SC_LANES)], mask=picked)

        @pl.when(first == 0)
        def _():
            pltpu.sync_copy(tok_v, tok_hbm.at[pair])
            pltpu.sync_copy(gate_v, gate_hbm.at[pair])

        gather(first, 0).start()

        @pl.loop(0, part_rows, step=2 * SC_ROWS)
        def _(r0):
            c0 = first + r0
            gather(c0 + SC_ROWS, 1).start()
            gather(c0, 0).wait()
            pltpu.sync_copy(rows_v.at[0], xs_hbm.at[pl.ds(pair * cap + c0, SC_ROWS)])

            @pl.when(r0 + 2 * SC_ROWS < part_rows)
            def _():
                gather(c0 + 2 * SC_ROWS, 0).start()

            gather(c0 + SC_ROWS, 1).wait()
            pltpu.sync_copy(rows_v.at[1], xs_hbm.at[pl.ds(pair * cap + c0 + SC_ROWS, SC_ROWS)])

    return dispatch(x_flat, slot2, aff2)


def _expert_kernel(xs_ref, g_ref, w1_ref, w3_ref, w2_ref, ye_ref, w1_bf, w3_bf, w2_bf):
    @pl.when((pl.program_id(1) == 0) & (pl.program_id(2) == 0))
    def _():
        w1_bf[...] = w1_ref[0, 0].astype(BF16)
        w3_bf[...] = w3_ref[0, 0].astype(BF16)
        w2_bf[...] = w2_ref[0, 0].astype(BF16)

    xs = jnp.concatenate(_unpack_bf16_pair(xs_ref[0, 0]), axis=1)
    hid = (jax.nn.silu(jnp.dot(xs, w1_bf[...], preferred_element_type=F32))
           * jnp.dot(xs, w3_bf[...], preferred_element_type=F32))
    ye_ref[0, 0] = jnp.dot(hid.astype(BF16), w2_bf[...], preferred_element_type=F32) * g_ref[0, 0]


def _expert_call(xs4, gate4, w1, w3, w2, layer, first_expert):
    b, e, cap, half = xs4.shape
    d, ff = w1.shape[2], w1.shape[3]
    assert d == 2 * half
    rows = lambda w: pl.BlockSpec((1, 1, CF, w), lambda e, b, j: (b, e, j, 0))
    wspec = lambda r, c: pl.BlockSpec((1, 1, r, c), lambda e, b, j: (layer, first_expert + e, 0, 0))
    return pl.pallas_call(
        _expert_kernel, name="expert_ffn",
        grid=(e, b, cap // CF),
        in_specs=[rows(half), rows(1), wspec(d, ff), wspec(d, ff), wspec(ff, d)],
        out_specs=rows(d),
        out_shape=jax.ShapeDtypeStruct((b, e, cap, d), F32),
        scratch_shapes=[pltpu.VMEM((d, ff), BF16), pltpu.VMEM((d, ff), BF16), pltpu.VMEM((ff, d), BF16)],
        compiler_params=_cparams(3),
    )(xs4, gate4, w1, w3, w2)


def _sc_combine_call(ye_flat, tok3, seq, experts):
    n_pairs, n_chunks_all, _ = tok3.shape
    cap = n_chunks_all * SC_IDX
    d = ye_flat.shape[1]
    nb = n_pairs // experts
    info = plsc.get_sparse_core_info()
    assert info.num_subcores % experts == 0 and nb % info.num_cores == 0
    parts = info.num_subcores // experts
    n_chunks = n_chunks_all // parts
    assert n_chunks_all % parts == 0 and n_chunks % 2 == 0
    assert seq % (info.num_subcores * SC_ZROWS) == 0 and d % SC_SLAB == 0
    batches_per_core = nb // info.num_cores
    own_rows = seq // info.num_subcores
    mesh = plsc.VectorSubcoreMesh(core_axis_name="c", subcore_axis_name="s")

    @functools.partial(
        pl.kernel, mesh=mesh, name="expert_combine_sc",
        compiler_params=pltpu.CompilerParams(needs_layout_passes=False),
        out_type=jax.ShapeDtypeStruct((nb * seq, d), F32),
        scratch_types=[pltpu.VMEM_SHARED((seq, SC_SLAB), F32),
                       pltpu.VMEM((n_chunks_all, SC_IDX), jnp.int32),
                       pltpu.VMEM((2, SC_IDX, SC_SLAB), F32),
                       pltpu.VMEM((SC_ZROWS, SC_SLAB), F32),
                       pltpu.SemaphoreType.DMA((2,))])
    def combine(ye_hbm, tok_hbm, out_hbm, acc_sh, tok_v, rows_v, zero_v, sem):
        core = lax.axis_index("c")
        sub = lax.axis_index("s")

        @pl.loop(0, SC_ZROWS)
        def _(r):
            for l0 in range(0, SC_SLAB, SC_LANES):
                zero_v[r, pl.ds(l0, SC_LANES)] = jnp.zeros((SC_LANES,), F32)

        for bb in range(batches_per_core):
            batch = core * batches_per_core + bb
            pair = batch * experts + sub // parts
            first = (sub % parts) * n_chunks
            pltpu.sync_copy(tok_hbm.at[pair], tok_v)

            @pl.loop(0, d // SC_SLAB)
            def _(slab):
                cols = pl.ds(pl.multiple_of(slab * SC_SLAB, SC_SLAB), SC_SLAB)

                @pl.loop(0, own_rows, step=SC_ZROWS)
                def _(r0):
                    pltpu.sync_copy(zero_v, acc_sh.at[pl.ds(sub * own_rows + r0, SC_ZROWS)])

                def load(j, buf):
                    return pltpu.make_async_copy(
                        ye_hbm.at[pl.ds(pair * cap + (first + j) * SC_IDX, SC_IDX), cols],
                        rows_v.at[buf], sem.at[buf])

                load(0, 0).start()
                plsc.subcore_barrier()

                for j in range(0, n_chunks, 2):
                    load(j + 1, 1).start()
                    load(j, 0).wait()
                    pltpu.sync_copy(rows_v.at[0], acc_sh.at[tok_v.at[first + j]], add=True)
                    if j + 2 < n_chunks:
                        load(j + 2, 0).start()
                    load(j + 1, 1).wait()
                    pltpu.sync_copy(rows_v.at[1], acc_sh.at[tok_v.at[first + j + 1]], add=True)

                plsc.subcore_barrier()
                pltpu.sync_copy(acc_sh.at[pl.ds(sub * own_rows, own_rows)],
                                out_hbm.at[pl.ds(batch * seq + sub * own_rows, own_rows), cols])

    return combine(ye_flat, tok3)


TN = 512


def _resln_kernel(x_ref, *refs, alpha):
    *y_refs, g_ref, b_ref, o_ref = refs
    y = sum(r[...] for r in y_refs[1:]) + y_refs[0][...]
    o_ref[...] = _standardize(alpha * x_ref[...] + y) * g_ref[...] + b_ref[...]


def _resln_call(x2d, y_groups, g, bta, alpha):
    n, d = x2d.shape
    tok = lambda w: pl.BlockSpec((TN, w), lambda i: (i, 0))
    vec = pl.BlockSpec((1, d), lambda i: (0, 0))
    return pl.pallas_call(
        functools.partial(_resln_kernel, alpha=alpha), name="residual_layernorm",
        grid=(n // TN,), in_specs=[tok(d)] * (1 + len(y_groups)) + [vec, vec], out_specs=tok(d),
        out_shape=jax.ShapeDtypeStruct((n, d), F32),
        compiler_params=_cparams(1),
    )(x2d, *y_groups, g, bta)


def _expert_choice_ffn(x1p, aff_t, w1, w3, w2, layer):
    b, s, half = x1p.shape
    d = 2 * half
    cap = EC_FACTOR * s // N_EXPERTS
    slot = _select_call(aff_t, cap)
    eg = N_EXPERTS // EXPERT_GROUPS
    partials = []
    for grp in range(EXPERT_GROUPS):
        sel = slice(grp * eg, (grp + 1) * eg)
        xs, tok, gate = _sc_dispatch_call(x1p.reshape(b * s, half), slot[:, sel].reshape(b * eg, s),
                                          aff_t[:, sel].reshape(b * eg, s), s, cap, eg)
        ye = _expert_call(xs.reshape(b, eg, cap, half), gate.reshape(b, eg, cap, 1), w1, w3, w2, layer,
                          grp * eg)
        partials.append(_sc_combine_call(ye.reshape(b * eg * cap, d),
                                         tok.reshape(b * eg, cap // SC_IDX, SC_IDX), s, eg))
    return tuple(partials)


def _pack_pool(pool_w):
    g, gd, _ = pool_w.shape
    out = jnp.zeros((g * gd, g * gd), F32)
    for i in range(g):
        out = out.at[i * gd:(i + 1) * gd, i * gd:(i + 1) * gd].set(pool_w[i])
    return out.astype(BF16)


def _layer(layer, x, pending, alpha, bias_tiles, w_in, b_in, gm_ln_g, gm_ws, gm_bs, ml_conv, ml_fbias,
           ml_norm_g, pool_w, pool_scale, w_branch, w_out, ln1_g, ln1_b, w_router, w_e1, w_e3, w_e2):
    b, s, d = x.shape
    n_small = 2576
    w_cat, b_cat = _pack_inproj_weights(w_in, b_in)
    wscat, bsfull = _pack_gmlp(gm_ws, gm_bs)
    if pending is None:
        outs = _inproj_call(x.reshape(b * s, d), w_cat, b_cat, gm_ln_g[None], wscat, bsfull, b)
    else:
        *outs, x2 = _inproj_call(pending, w_cat, b_cat, gm_ln_g[None], wscat, bsfull, b, alpha)
        x = x2.reshape(b, s, d)
    ya, qkv1, qkv4, qkv16, cqk, cv, co, dx, ig, fg = outs
    r3 = lambda t: t.reshape(b, s, t.shape[-1])
    o_list, l_list = [], []
    for qkv, bias in zip((qkv1, qkv4, qkv16), bias_tiles):
        o, lse = _attn_call(qkv, bias)
        o_list.append(o)
        l_list.append(lse)
    hf, hb = _mlstm_branch(r3(cqk), r3(cv), r3(ig), r3(fg), ml_conv, ml_fbias)
    yd = _pool_call(r3(dx), _pack_pool(pool_w), pool_scale[None])
    x1, x1p, aff_t = _merge_call(
        x, r3(ya), o_list, l_list, hf, hb, r3(co), yd,
        w_in[:, n_small:].astype(BF16), b_in[None, n_small:], w_branch.astype(BF16), w_out.astype(BF16),
        ml_norm_g[:, None], ln1_g[None], ln1_b[None], jnp.transpose(w_router).astype(BF16), alpha)
    ffn = _expert_choice_ffn(x1p, aff_t, w_e1, w_e3, w_e2, layer)
    return x1.reshape(b * s, d), ffn


def kernel(x, w_in, b_in, gm_ln_g, gm_ws, gm_bs, rel_bias, ml_conv, ml_fbias, ml_norm_g, pool_w,
           pool_scale, w_branch, w_out, ln1_g, ln1_b, w_router, w_e1, w_e3, w_e2, ln2_g, ln2_b):
    depth = w_in.shape[0]
    alpha = (2 * depth) ** 0.25
    bias_tiles = [_attn_bias_tile(rel_bias, window, dil) for window, dil in DIL_PATTERNS]
    b, s, d = x.shape
    pending = None
    for l in range(depth):
        x1, ffn = _layer(l, x, pending, alpha, bias_tiles, w_in[l], b_in[l], gm_ln_g[l], gm_ws[l], gm_bs[l],
                         ml_conv[l], ml_fbias[l], ml_norm_g[l], pool_w[l], pool_scale[l], w_branch[l],
                         w_out[l], ln1_g[l], ln1_b[l], w_router[l], w_e1, w_e3, w_e2)
        pending = (x1, ffn, ln2_g[l][None], ln2_b[l][None])
    return _resln_call(*pending, alpha).reshape(b, s, d)
```

```python
import functools
import math

import jax
import jax.numpy as jnp
import numpy as np
from jax import lax
from jax.experimental import pallas as pl
from jax.experimental.pallas import tpu as pltpu
from jax.experimental.pallas import tpu_sc as plsc

F32 = jnp.float32
BF16 = jnp.bfloat16

D_MODEL = 1024
MIX_W = 256
N_BRANCH = 4
GM_CHUNK = 128
GM_GROUPS = 4
ATT_HEADS = 4
ATT_HD = 64
DIL_PATTERNS = ((128, 1), (512, 4), (2048, 16))
ATT_BLOCK = 64
REL_BUCKETS = 32
REL_MAX_DIST = 1024
ML_HEADS = 4
ML_HD = 64
ML_CHUNK = 64
POOL_WINDOWS = (2, 4, 8, 16)
N_EXPERTS = 16
EXPERT_FF = 1024
EC_FACTOR = 2
LN_EPS = 1e-5
NEG_BIG = -1e30

V7X_VMEM_LIMIT = 56 * 1024 * 1024
LANES = 128
HALO = 8


def _cparams(n_grid, vmem=V7X_VMEM_LIMIT):
    return pltpu.CompilerParams(dimension_semantics=("arbitrary",) * n_grid,
                                vmem_limit_bytes=vmem)


def _pack_bf16_pair(lo, hi):
    lo_bits = lax.shift_right_logical(pltpu.bitcast(lo.astype(F32), jnp.int32), 16)
    return pltpu.bitcast(hi.astype(F32), jnp.int32) | lo_bits


def _unpack_bf16_pair(packed):
    lo = pltpu.bitcast(lax.shift_left(packed, 16), F32).astype(BF16)
    hi = pltpu.bitcast(packed & jnp.int32(-65536), F32).astype(BF16)
    return lo, hi


def _standardize(xf):
    mu = jnp.mean(xf, axis=-1, keepdims=True)
    var = jnp.mean(jnp.square(xf - mu), axis=-1, keepdims=True)
    return (xf - mu) * lax.rsqrt(var + LN_EPS)


TA = 512
A_COLS = 2560 + 2 * LANES
QKV_HALF = 3 * MIX_W // 2


def _inproj_kernel(*refs, alpha):
    if alpha is None:
        x_ref, *refs = refs
        x = x_ref[...]
    else:
        n_grp = EXPERT_GROUPS
        x1_ref, y_refs, (g2_ref, b2_ref, *refs) = refs[0], refs[1:1 + n_grp], refs[1 + n_grp:]
        y = sum(r[...] for r in y_refs[1:]) + y_refs[0][...]
        x = _standardize(alpha * x1_ref[...] + y) * g2_ref[...] + b2_ref[...]
        refs[-1][...] = x
        refs = refs[:-1]
    (w_ref, b_ref, lng_ref, wscat_ref, bsfull_ref,
     ya_ref, qkvp_ref, cqk_ref, cv_ref, co_ref, dx_ref, ig_ref, fg_ref) = refs
    xb = x.astype(BF16)
    h = jnp.dot(xb, w_ref[...], preferred_element_type=F32) + b_ref[...]
    qkv = h[:, 512:1280].astype(BF16)
    qkvp_ref[...] = _pack_bf16_pair(qkv[:, :QKV_HALF], qkv[:, QKV_HALF:])
    cqk_ref[...] = h[:, 1280:1792]
    cv_ref[...] = h[:, 1792:2048].astype(BF16)
    co_ref[...] = h[:, 2048:2304]
    dx_ref[...] = h[:, 2304:2560]
    ig_ref[...] = h[:, 2560:2688]
    fg_ref[...] = h[:, 2688:2816]
    u = jax.nn.gelu(h[:, 0:256])
    v = jax.nn.gelu(h[:, 256:512])
    vn = _standardize(v) * lng_ref[...]
    lane_grp = lax.broadcasted_iota(jnp.int32, (GM_CHUNK, MIX_W), 1) // (MIX_W // GM_GROUPS)
    for c in range(TA // GM_CHUNK):
        vc = vn[c * GM_CHUNK:(c + 1) * GM_CHUNK]
        stacked = jnp.concatenate(
            [jnp.where(lane_grp == g, vc, 0.0).astype(BF16) for g in range(GM_GROUPS)], axis=0)
        mixed = jnp.dot(wscat_ref[...], stacked, preferred_element_type=F32) + bsfull_ref[...]
        ya_ref[c * GM_CHUNK:(c + 1) * GM_CHUNK, :] = (
            u[c * GM_CHUNK:(c + 1) * GM_CHUNK] * mixed).astype(BF16)


def _inproj_call(x_in, w_cat, b_cat, lng, wscat, bsfull, batch, alpha=None):
    fused = alpha is not None
    n = (x_in[0] if fused else x_in).shape[0]
    tok = lambda w: pl.BlockSpec((TA, w), lambda i: (i, 0))
    const = lambda s: pl.BlockSpec(s, lambda i: (0,) * len(s))
    out_shape = (
        jax.ShapeDtypeStruct((n, 256), BF16),
        jax.ShapeDtypeStruct((n, QKV_HALF), jnp.int32),
        jax.ShapeDtypeStruct((n, 512), F32),
        jax.ShapeDtypeStruct((n, 256), BF16),
        jax.ShapeDtypeStruct((n, 256), F32),
        jax.ShapeDtypeStruct((n, 256), F32),
        jax.ShapeDtypeStruct((n, LANES), F32),
        jax.ShapeDtypeStruct((n, LANES), F32),
    )
    if fused:
        x1, ffn_groups, ln_g, ln_b = x_in
        x_args = (x1, *ffn_groups, ln_g, ln_b)
        x_specs = ([tok(D_MODEL)] * (1 + len(ffn_groups)) + [const((1, D_MODEL)), const((1, D_MODEL))])
    else:
        x_args, x_specs = (x_in,), [tok(D_MODEL)]
    out_specs = (tok(256), tok(QKV_HALF), tok(512), tok(256), tok(256), tok(256), tok(LANES), tok(LANES))
    if fused:
        out_specs += (tok(D_MODEL),)
        out_shape += (jax.ShapeDtypeStruct((n, D_MODEL), F32),)
    return pl.pallas_call(
        functools.partial(_inproj_kernel, alpha=alpha), name="inproj_gmlp",
        grid=(n // TA,),
        in_specs=x_specs + [const((D_MODEL, A_COLS)), const((1, A_COLS)), const((1, MIX_W)),
                            const((GM_CHUNK, GM_GROUPS * GM_CHUNK)), const((GM_CHUNK, MIX_W))],
        out_specs=out_specs,
        out_shape=out_shape,
        compiler_params=_cparams(1),
    )(*x_args, w_cat, b_cat, lng, wscat, bsfull)


def _pack_inproj_weights(w_in, b_in):
    pad = lambda a: jnp.pad(a, ((0, 0), (0, LANES - 8)))
    w_cat = jnp.concatenate([w_in[:, 0:2304], w_in[:, 2320:2576],
                             pad(w_in[:, 2304:2312]), pad(w_in[:, 2312:2320])], axis=1)
    b2 = b_in[None, :]
    b_cat = jnp.concatenate([b2[:, 0:2304], b2[:, 2320:2576],
                             pad(b2[:, 2304:2312]), pad(b2[:, 2312:2320])], axis=1)
    return w_cat.astype(BF16), b_cat


def _pack_gmlp(gm_ws, gm_bs):
    wscat = jnp.transpose(gm_ws, (1, 0, 2)).reshape(GM_CHUNK, GM_GROUPS * GM_CHUNK).astype(BF16)
    bsfull = jnp.repeat(jnp.transpose(gm_bs), MIX_W // GM_GROUPS, axis=1)
    return wscat, bsfull


def _halo_specs(t, width, n_tiles):
    r = t // HALO
    main = pl.BlockSpec((1, t, width), lambda b, i: (b, i, 0))
    prev = pl.BlockSpec((1, HALO, width), lambda b, i: (b, jnp.maximum(i * r - 1, 0), 0))
    nxt = pl.BlockSpec((1, HALO, width), lambda b, i: (b, jnp.minimum((i + 1) * r, n_tiles * r - 1), 0))
    return main, prev, nxt


def _fill_halo_scratch(buf, x_ref, p_ref, n_ref, t):
    i = pl.program_id(1)
    last = pl.num_programs(1) - 1
    buf[0:HALO, :] = jnp.where(i > 0, p_ref[0], 0.0)
    buf[HALO:HALO + t, :] = x_ref[0]
    buf[HALO + t:2 * HALO + t, :] = jnp.where(i < last, n_ref[0], 0.0)


TP = 512


def _pool_kernel(x_ref, p_ref, n_ref, w_ref, sc_ref, o_ref, buf):
    _fill_halo_scratch(buf, x_ref, p_ref, n_ref, TP)
    seq = pl.num_programs(1) * TP
    pos = pl.program_id(1) * TP + lax.broadcasted_iota(jnp.int32, (TP, 1), 0)
    lane_grp = lax.broadcasted_iota(jnp.int32, (TP, MIX_W), 1) // (MIX_W // len(POOL_WINDOWS))
    x0 = buf[HALO:HALO + TP, :]
    pooled = jnp.zeros((TP, MIX_W), F32)
    acc = None
    half_done = 0
    for gi, win in enumerate(POOL_WINDOWS):
        half = win // 2
        for o in list(range(-half, -half_done)) + list(range(half_done, half)):
            term = buf[HALO + o:HALO + o + TP, :]
            acc = term if acc is None else acc + term
        half_done = half
        cnt = (jnp.minimum(pos + half, seq) - jnp.maximum(pos - half, 0)).astype(F32)
        pooled = jnp.where(lane_grp == gi, acc / cnt - x0, pooled)
    mixed = jnp.dot(pooled.astype(BF16), w_ref[...], preferred_element_type=F32)
    o_ref[0] = (mixed * sc_ref[...]).astype(BF16)


def _pool_call(dx, w_block, scale):
    b, s, _ = dx.shape
    nt = s // TP
    main, prev, nxt = _halo_specs(TP, MIX_W, nt)
    return pl.pallas_call(
        _pool_kernel, name="pool_mixer",
        grid=(b, nt),
        in_specs=[main, prev, nxt,
                  pl.BlockSpec((MIX_W, MIX_W), lambda b, i: (0, 0)),
                  pl.BlockSpec((1, MIX_W), lambda b, i: (0, 0))],
        out_specs=pl.BlockSpec((1, TP, MIX_W), lambda b, i: (b, i, 0)),
        out_shape=jax.ShapeDtypeStruct((b, s, MIX_W), BF16),
        scratch_shapes=[pltpu.VMEM((TP + 2 * HALO, MIX_W), F32)],
        compiler_params=_cparams(2),
    )(dx, dx, dx, w_block, scale)


TQ = 128
TQS = 512
TKEYS = TQ + 2 * ATT_BLOCK


def _attn_kernel(prev_ref, main_ref, next_ref, bias_ref, out_ref):
    i = pl.program_id(2)
    seq = pl.num_programs(2) * TQS
    lo_m, hi_m = _unpack_bf16_pair(main_ref[0, 0])
    lo_p, hi_p = _unpack_bf16_pair(prev_ref[0, 0])
    lo_n, hi_n = _unpack_bf16_pair(next_ref[0, 0])
    q = lo_m[:, :MIX_W] * ATT_HD ** -0.5
    rows = lambda lo, hi: (jnp.concatenate([lo[:, MIX_W:], hi[:, :LANES]], axis=1), hi[:, LANES:])
    (k_p, v_p), (k_m, v_m), (k_n, v_n) = rows(lo_p, hi_p), rows(lo_m, hi_m), rows(lo_n, hi_n)
    k = jnp.concatenate([k_p, k_m, k_n], axis=0)
    v = jnp.concatenate([v_p, v_m, v_n], axis=0)
    lane = lax.broadcasted_iota(jnp.int32, (TQ, LANES), 1)
    lane_half = lax.broadcasted_iota(jnp.int32, (1, LANES), 1) // ATT_HD
    keep = [jnp.where(lane_half == hh, 1.0, 0.0).astype(BF16) for hh in range(2)]
    for j in range(TQS // TQ):
        kpos = i * TQS + j * TQ - ATT_BLOCK + lax.broadcasted_iota(jnp.int32, (1, TKEYS), 1)
        kvalid = (kpos >= 0) & (kpos < seq)
        qrows = slice(j * TQ, (j + 1) * TQ)
        krows = slice(j * TQ, j * TQ + TKEYS)
        lse_tile = jnp.zeros((TQ, LANES), F32)
        for pair in range(ATT_HEADS // 2):
            grp = slice(pair * LANES, (pair + 1) * LANES)
            q_pair, k_pair, v_pair = q[qrows, grp], k[krows, grp], v[krows, grp]
            o_pair = jnp.zeros((TQ, LANES), F32)
            for hh in range(2):
                h = 2 * pair + hh
                logits = lax.dot_general(q_pair * keep[hh], k_pair, (((1,), (1,)), ((), ())),
                                         preferred_element_type=F32) + bias_ref[h]
                logits = jnp.where(kvalid, logits, NEG_BIG)
                m = jnp.max(logits, axis=-1, keepdims=True)
                p = jnp.exp(logits - m)
                ssum = jnp.sum(p, axis=-1, keepdims=True)
                o = jnp.dot(p.astype(BF16), v_pair, preferred_element_type=F32) / ssum
                o_pair = jnp.where(lane_half == hh, o, o_pair)
                lse_tile = jnp.where(lane == h, m + jnp.log(ssum), lse_tile)
            out_ref[0, 0, qrows, grp] = o_pair
        out_ref[0, 0, qrows, MIX_W:] = lse_tile


def _attn_call(qkvp, bias):
    b, dil, l, _ = qkvp.shape
    nt = l // TQS
    r64 = TQS // ATT_BLOCK
    main = pl.BlockSpec((1, 1, TQS, QKV_HALF), lambda b, r, i: (b, r, i, 0))
    prev = pl.BlockSpec((1, 1, ATT_BLOCK, QKV_HALF), lambda b, r, i: (b, r, jnp.maximum(i * r64 - 1, 0), 0))
    nxt = pl.BlockSpec((1, 1, ATT_BLOCK, QKV_HALF),
                       lambda b, r, i: (b, r, jnp.minimum((i + 1) * r64, nt * r64 - 1), 0))
    return pl.pallas_call(
        _attn_kernel, name="band_attention",
        grid=(b, dil, nt),
        in_specs=[prev, main, nxt, pl.BlockSpec((ATT_HEADS, TQ, TKEYS), lambda b, r, i: (0, 0, 0))],
        out_specs=pl.BlockSpec((1, 1, TQS, MIX_W + LANES), lambda b, r, i: (b, r, i, 0)),
        out_shape=jax.ShapeDtypeStruct((b, dil, l, MIX_W + LANES), F32),
        compiler_params=_cparams(3),
    )(qkvp, qkvp, qkvp, bias)


def _t5_bucket_static(rel):
    half = REL_BUCKETS // 2
    max_exact = half // 2
    ret = np.where(rel > 0, half, 0)
    n = np.abs(rel)
    nf = np.maximum(n, 1).astype(np.float32)
    large = max_exact + (np.log(nf / np.float32(max_exact)) / np.float32(math.log(REL_MAX_DIST / max_exact))
                         * np.float32(half - max_exact)).astype(np.int32)
    large = np.minimum(large, half - 1)
    return ret + np.where(n < max_exact, n, large)


def _attn_bias_tile(rel_bias, window, dil):
    side = (window // 2) // dil
    rel = np.arange(TKEYS)[None, :] - ATT_BLOCK - np.arange(TQ)[:, None]
    onehot = jax.nn.one_hot(jnp.asarray(_t5_bucket_static(dil * rel), jnp.int32), REL_BUCKETS, dtype=F32)
    bias = jnp.einsum('qkr,rh->hqk', onehot, rel_bias, precision=lax.Precision.HIGHEST)
    return jnp.where(jnp.asarray(np.abs(rel) <= side)[None], bias, NEG_BIG)


TM = 512
VT_ROWS = ML_HD + 16


def _mlprep_kernel(x_ref, p_ref, n_ref, v_ref, w_ref, qt_out, k_out, vt_out, buf):
    _fill_halo_scratch(buf, x_ref, p_ref, n_ref, TM)
    conv = (buf[HALO - 1:HALO - 1 + TM, :] * w_ref[0:1, :] + buf[HALO:HALO + TM, :] * w_ref[1:2, :]
            + buf[HALO + 1:HALO + 1 + TM, :] * w_ref[2:3, :])
    qk = jax.nn.silu(conv)
    qt = jnp.transpose(qk[:, :MIX_W])
    vt = jnp.transpose(v_ref[0].astype(F32))
    ones_rows = jnp.where(lax.broadcasted_iota(jnp.int32, (VT_ROWS - ML_HD, ML_CHUNK), 0) == 0, 1.0, 0.0)
    for h in range(ML_HEADS):
        sl = slice(h * ML_HD, (h + 1) * ML_HD)
        k_out[0, h] = (qk[:, MIX_W + h * ML_HD:MIX_W + (h + 1) * ML_HD] * ML_HD ** -0.5).astype(BF16)
        for c in range(TM // ML_CHUNK):
            cl = slice(c * ML_CHUNK, (c + 1) * ML_CHUNK)
            qt_out[0, h, c] = qt[sl, cl].astype(BF16)
            vt_out[0, h, c] = jnp.concatenate([vt[sl, cl], ones_rows], axis=0).astype(BF16)


def _mlprep_call(cqk, cv, conv_w):
    b, s, _ = cqk.shape
    nt = s // TM
    nc = s // ML_CHUNK
    cpt = TM // ML_CHUNK
    main, prev, nxt = _halo_specs(TM, 2 * MIX_W, nt)
    return pl.pallas_call(
        _mlprep_kernel, name="mlstm_prep",
        grid=(b, nt),
        in_specs=[main, prev, nxt,
                  pl.BlockSpec((1, TM, MIX_W), lambda b, i: (b, i, 0)),
                  pl.BlockSpec((3, 2 * MIX_W), lambda b, i: (0, 0))],
        out_specs=(pl.BlockSpec((1, ML_HEADS, cpt, ML_HD, ML_CHUNK), lambda b, i: (b, 0, i, 0, 0)),
                   pl.BlockSpec((1, ML_HEADS, TM, ML_HD), lambda b, i: (b, 0, i, 0)),
                   pl.BlockSpec((1, ML_HEADS, cpt, VT_ROWS, ML_CHUNK), lambda b, i: (b, 0, i, 0, 0))),
        out_shape=(jax.ShapeDtypeStruct((b, ML_HEADS, nc, ML_HD, ML_CHUNK), BF16),
                   jax.ShapeDtypeStruct((b, ML_HEADS, s, ML_HD), BF16),
                   jax.ShapeDtypeStruct((b, ML_HEADS, nc, VT_ROWS, ML_CHUNK), BF16)),
        scratch_shapes=[pltpu.VMEM((TM + 2 * HALO, 2 * MIX_W), F32)],
        compiler_params=_cparams(2),
    )(cqk, cqk, cqk, cv, conv_w)


GP_BLK = 512


def _chunk_scan(x, fwd_lane, t_in_chunk, op, ident):
    n = x.shape[0]
    shift = 1
    while shift < ML_CHUNK:
        down = pltpu.roll(x, shift, 0)
        up = pltpu.roll(x, n - shift, 0)
        nb = jnp.where(fwd_lane,
                       jnp.where(t_in_chunk >= shift, down, ident),
                       jnp.where(t_in_chunk < ML_CHUNK - shift, up, ident))
        x = op(x, nb)
        shift *= 2
    return x


def _gatescan_kernel(ig_ref, fg_ref, fb_ref, b_ref, a_ref, cm_ref, g_ref, amax_ref):
    lane = lax.broadcasted_iota(jnp.int32, (1, LANES), 1)
    fwd_lane = lane < ML_HEADS
    t_in_chunk = lax.broadcasted_iota(jnp.int32, (GP_BLK, 1), 0) % ML_CHUNK
    z = fg_ref[0] + fb_ref[...]
    lf = jnp.minimum(z, 0.0) - jnp.log1p(jnp.exp(-jnp.abs(z)))
    b = _chunk_scan(lf, fwd_lane, t_in_chunk, jnp.add, 0.0)
    a = ig_ref[0] - b
    b_ref[0] = b
    a_ref[0] = a
    cm_ref[0] = _chunk_scan(a, fwd_lane, t_in_chunk, jnp.maximum, -jnp.inf)
    cpb = GP_BLK // ML_CHUNK
    last = pl.ds(ML_CHUNK - 1, cpb, stride=ML_CHUNK)
    first = pl.ds(0, cpb, stride=ML_CHUNK)
    g_ref[0] = jnp.where(fwd_lane, b_ref[0, last, :], b_ref[0, first, :])
    amax_ref[0] = jnp.where(fwd_lane, cm_ref[0, last, :], cm_ref[0, first, :])


def _gateout_kernel(b_ref, a_ref, cm_ref, g_ref, amax_ref, ws_ref, m_ref, iw_ref, en_ref, dec_ref,
                    mch_s, mlast_s):
    nc = g_ref.shape[1]
    j = pl.program_id(1)
    fwd_lane = lax.broadcasted_iota(jnp.int32, (1, LANES), 1) < ML_HEADS

    @pl.when(j == 0)
    def _():
        def m_step(i, carry):
            mf, mb = carry
            cf = pl.ds(i, 1)
            cb = pl.ds(nc - 1 - i, 1)
            mch_s[cf, :] = jnp.where(fwd_lane, mf, mch_s[cf, :])
            mch_s[cb, :] = jnp.where(fwd_lane, mch_s[cb, :], mb)
            mf = g_ref[0, cf, :] + jnp.maximum(mf, amax_ref[0, cf, :])
            mb = g_ref[0, cb, :] + jnp.maximum(mb, amax_ref[0, cb, :])
            return mf, mb

        mch_s[...] = jnp.zeros((nc, LANES), F32)
        zero = jnp.zeros((1, LANES), F32)
        lax.fori_loop(0, nc, m_step, (zero, zero))
        mlast = jnp.maximum(amax_ref[0], mch_s[...])
        mlast_s[...] = mlast
        dec_ref[0] = jnp.exp(mch_s[...] - mlast)

    cpb = GP_BLK // ML_CHUNK
    crow = pl.ds(pl.multiple_of(j * cpb, cpb), cpb)
    expand = lambda t: jnp.broadcast_to(t[:, None, :], (cpb, ML_CHUNK, LANES)).reshape(GP_BLK, LANES)
    m_tok = expand(mch_s[crow, :])
    mlast_tok = expand(mlast_s[crow, :])
    mt = jnp.maximum(cm_ref[0], m_tok)
    m_ref[0] = mt
    iw_ref[0] = jnp.exp(m_tok - mt)
    en_ref[0] = jnp.exp(-(b_ref[0] + mt))
    ws_ref[0] = jnp.exp(a_ref[0] - mlast_tok)


def _gateprep_call(ig, fg, fbias_row):
    b, s, _ = ig.shape
    nc = s // ML_CHUNK
    cpb = GP_BLK // ML_CHUNK
    tok = pl.BlockSpec((1, GP_BLK, LANES), lambda i, j: (i, j, 0))
    tok_shape = jax.ShapeDtypeStruct((b, s, LANES), F32)
    chunk_shape = jax.ShapeDtypeStruct((b, nc, LANES), F32)
    chunk_tile = pl.BlockSpec((1, cpb, LANES), lambda i, j: (i, j, 0))
    chunk_all = pl.BlockSpec((1, nc, LANES), lambda i, j: (i, 0, 0))
    bcum, a, cm, g, amax = pl.pallas_call(
        _gatescan_kernel, name="mlstm_gate_scan",
        grid=(b, s // GP_BLK),
        in_specs=[tok, tok, pl.BlockSpec((1, LANES), lambda i, j: (0, 0))],
        out_specs=(tok, tok, tok, chunk_tile, chunk_tile),
        out_shape=(tok_shape,) * 3 + (chunk_shape,) * 2,
        compiler_params=_cparams(2),
    )(ig, fg, fbias_row)
    ws, m_col, iw_col, en_col, decay = pl.pallas_call(
        _gateout_kernel, name="mlstm_gate_out",
        grid=(b, s // GP_BLK),
        in_specs=[tok, tok, tok, chunk_all, chunk_all],
        out_specs=(tok, tok, tok, tok, chunk_all),
        out_shape=(tok_shape,) * 4 + (chunk_shape,),
        scratch_shapes=[pltpu.VMEM((nc, LANES), F32)] * 2,
        compiler_params=_cparams(2),
    )(bcum, a, cm, g, amax)
    return a, ws, m_col, iw_col, en_col, decay


TE = 1024


def _mlstm_kernel(*refs):
    fwd, bwd, (hf_ref, hb_ref, state) = refs[:9], refs[9:18], refs[18:]
    i = pl.program_id(1)

    @pl.when(i == 0)
    def _():
        state[...] = jnp.zeros(state.shape, F32)

    cpt = TE // ML_CHUNK
    s_idx = lax.broadcasted_iota(jnp.int32, (ML_CHUNK, ML_CHUNK), 0)
    t_idx = lax.broadcasted_iota(jnp.int32, (ML_CHUNK, ML_CHUNK), 1)

    def chunk_body(c, carry):
        jobs = []
        for d, (qt_r, k_r, vt_r, a_r, m_r, iw_r, en_r, ws_r, dec_r), out_r in ((0, fwd, hf_ref), (1, bwd, hb_ref)):
            cc = c if d == 0 else cpt - 1 - c
            rows = pl.ds(pl.multiple_of(cc * ML_CHUNK, ML_CHUNK), ML_CHUNK)
            crow = pl.ds(cc, 1)
            for h in range(ML_HEADS):
                ch = d * ML_HEADS + h
                row = lambda r: r[0, ch, crow, :]
                jobs.append(dict(
                    ch=ch, tri=(s_idx >= t_idx) if d else (s_idx <= t_idx),
                    k=k_r[0, h, rows, :], qt=qt_r[0, h, cc], vt=vt_r[0, h, cc], a=a_r[0, rows, ch:ch + 1],
                    m=row(m_r), iw=row(iw_r), en=row(en_r), ws=row(ws_r), dec=row(dec_r),
                    out=(out_r, cc, h)))
        for j in jobs:
            j["cst"] = state[j["ch"]]
            j["st"] = jnp.dot(j["k"], j["qt"], preferred_element_type=F32)
            j["inter"] = jnp.dot(j["cst"].astype(BF16), j["qt"], preferred_element_type=F32)
            j["upd"] = jnp.dot((j["vt"].astype(F32) * j["ws"]).astype(BF16), j["k"],
                               preferred_element_type=F32)
        for j in jobs:
            j["swt"] = j["st"] * jnp.exp(jnp.where(j["tri"], j["a"] - j["m"], NEG_BIG))
            j["intra"] = jnp.dot(j["vt"], j["swt"].astype(BF16), preferred_element_type=F32)
        for j in jobs:
            den = jnp.sum(j["swt"], axis=0, keepdims=True) + j["iw"] * j["inter"][ML_HD:ML_HD + 1]
            tot = j["intra"][:ML_HD] + j["iw"] * j["inter"][:ML_HD]
            out_r, cc, h = j["out"]
            out_r[0, cc, h * ML_HD:(h + 1) * ML_HD, :] = tot / jnp.maximum(jnp.abs(den), j["en"])
            state[j["ch"]] = j["dec"] * j["cst"] + j["upd"]
        return carry

    lax.fori_loop(0, cpt, chunk_body, 0)


def _mlstm_call(qt, k, vt, a_col, m_row, iw_row, en_row, ws_row, dec_row):
    b, _, s, _ = k.shape
    nt = s // TE
    nc = s // ML_CHUNK
    cpt = TE // ML_CHUNK

    def specs(rev):
        ti = (lambda i: nt - 1 - i) if rev else (lambda i: i)
        row = pl.BlockSpec((1, 2 * ML_HEADS, cpt, ML_CHUNK), lambda b, i: (b, 0, ti(i), 0))
        return [
            pl.BlockSpec((1, ML_HEADS, cpt, ML_HD, ML_CHUNK), lambda b, i: (b, 0, ti(i), 0, 0)),
            pl.BlockSpec((1, ML_HEADS, TE, ML_HD), lambda b, i: (b, 0, ti(i), 0)),
            pl.BlockSpec((1, ML_HEADS, cpt, VT_ROWS, ML_CHUNK), lambda b, i: (b, 0, ti(i), 0, 0)),
            pl.BlockSpec((1, TE, LANES), lambda b, i: (b, ti(i), 0)),
            row, row, row, row, row]

    args = [qt, k, vt, a_col, m_row, iw_row, en_row, ws_row, dec_row]
    out_f = pl.BlockSpec((1, cpt, MIX_W, ML_CHUNK), lambda b, i: (b, i, 0, 0))
    out_b = pl.BlockSpec((1, cpt, MIX_W, ML_CHUNK), lambda b, i: (b, nt - 1 - i, 0, 0))
    return pl.pallas_call(
        _mlstm_kernel, name="mlstm_scan",
        grid=(b, nt),
        in_specs=specs(False) + specs(True),
        out_specs=(out_f, out_b),
        out_shape=(jax.ShapeDtypeStruct((b, nc, MIX_W, ML_CHUNK), F32),) * 2,
        scratch_shapes=[pltpu.VMEM((2 * ML_HEADS, VT_ROWS, ML_HD), F32)],
        compiler_params=_cparams(2),
    )(*args, *args)


def _mlstm_branch(cqk, cv, ig, fg, conv_w, fbias):
    b, s, _ = cqk.shape
    nc = s // ML_CHUNK
    qt, k, vt = _mlprep_call(cqk, cv, conv_w)
    fb_row = jnp.pad(fbias.reshape(1, 2 * ML_HEADS), ((0, 0), (0, LANES - 2 * ML_HEADS)))
    a, ws, m_col, iw_col, en_col, decay = _gateprep_call(ig, fg, fb_row)
    rows = lambda t: jnp.transpose(t[..., :2 * ML_HEADS], (0, 2, 1)).reshape(b, 2 * ML_HEADS, nc, ML_CHUNK)
    dec_row = jnp.broadcast_to(jnp.transpose(decay[..., :2 * ML_HEADS], (0, 2, 1))[..., None],
                               (b, 2 * ML_HEADS, nc, ML_CHUNK))
    return _mlstm_call(qt, k, vt, a, rows(m_col), rows(iw_col), rows(en_col), rows(ws), dec_row)


TF = 512


def _merge_kernel(x_ref, ya_ref, a1_ref, a2_ref, a3_ref, hf_ref, hb_ref,
                  co_ref, yd_ref, wg_ref, bg_ref, wbr_ref, wout_ref, mng_ref, lng_ref, lnb_ref,
                  wr_ref, x1_ref, x1p_ref, aff_ref, *, alpha):
    x = x_ref[0]
    xb = x.astype(BF16)
    lane_head = lax.broadcasted_iota(jnp.int32, (TF, MIX_W), 1) // ML_HD
    o1, o2, o3 = [r[0, :, :MIX_W] for r in (a1_ref, a2_ref, a3_ref)]
    l1, l2, l3 = [r[0, :, MIX_W:] for r in (a1_ref, a2_ref, a3_ref)]
    lm = jnp.maximum(jnp.maximum(l1, l2), l3)
    e1, e2, e3 = jnp.exp(l1 - lm), jnp.exp(l2 - lm), jnp.exp(l3 - lm)
    inv = 1.0 / (e1 + e2 + e3)

    def per_head(w):
        out = jnp.zeros((TF, MIX_W), F32)
        for h in range(ATT_HEADS):
            out = jnp.where(lane_head == h, w[:, h:h + 1], out)
        return out

    y_b = per_head(e1 * inv) * o1 + per_head(e2 * inv) * o2 + per_head(e3 * inv) * o3
    hsum_t = jnp.concatenate([hf_ref[0, c] + hb_ref[0, c] for c in range(TF // ML_CHUNK)], axis=1)
    per_head_rows = hsum_t.reshape(ML_HEADS, ML_HD, TF)
    mu = jnp.mean(per_head_rows, axis=1, keepdims=True)
    cen = per_head_rows - mu
    var = jnp.mean(cen * cen, axis=1, keepdims=True)
    hn_t = (cen * lax.rsqrt(var + LN_EPS)).reshape(MIX_W, TF)
    y_c_t = (jax.nn.sigmoid(jnp.transpose(co_ref[0])) * (hn_t * mng_ref[...])).astype(BF16)
    ys = (ya_ref[0], y_b.astype(BF16), None, yd_ref[0])
    merged = jnp.zeros((TF, D_MODEL), F32)
    for n in range(N_BRANCH):
        cols = slice(n * D_MODEL, (n + 1) * D_MODEL)
        gate = jax.nn.sigmoid(jnp.dot(xb, wg_ref[:, cols], preferred_element_type=F32) + bg_ref[:, cols])
        if ys[n] is None:
            proj = lax.dot_general(y_c_t, wbr_ref[n], (((0,), (0,)), ((), ())), preferred_element_type=F32)
        else:
            proj = jnp.dot(ys[n], wbr_ref[n], preferred_element_type=F32)
        merged = merged + gate * proj
    mix = jnp.dot(merged.astype(BF16), wout_ref[...], preferred_element_type=F32)
    x1 = _standardize(alpha * x + mix) * lng_ref[...] + lnb_ref[...]
    x1_ref[0] = x1
    x1b = x1.astype(BF16)
    x1p_ref[0] = _pack_bf16_pair(x1b[:, :D_MODEL // 2], x1b[:, D_MODEL // 2:])
    logits = lax.dot_general(wr_ref[...], x1b, (((1,), (1,)), ((), ())),
                             preferred_element_type=F32)
    ex = jnp.exp(logits - jnp.max(logits, axis=0, keepdims=True))
    aff_ref[0] = ex / jnp.sum(ex, axis=0, keepdims=True)


def _merge_call(x, ya, att_list, hf, hb, co, yd, wg, bg, wbr, wout, mng, lng, lnb, wr_t, alpha):
    b, s, _ = x.shape
    tok = lambda w: pl.BlockSpec((1, TF, w), lambda b, i: (b, i, 0))
    chunked = pl.BlockSpec((1, TF // ML_CHUNK, MIX_W, ML_CHUNK), lambda b, i: (b, i, 0, 0))
    const = lambda shp: pl.BlockSpec(shp, lambda b, i: (0,) * len(shp))
    return pl.pallas_call(
        functools.partial(_merge_kernel, alpha=alpha), name="merge_ln_router",
        grid=(b, s // TF),
        in_specs=[tok(D_MODEL), tok(MIX_W)] + [tok(MIX_W + LANES)] * len(att_list)
                 + [chunked, chunked, tok(MIX_W), tok(MIX_W)]
                 + [const((D_MODEL, N_BRANCH * D_MODEL)), const((1, N_BRANCH * D_MODEL)),
                    const((N_BRANCH, MIX_W, D_MODEL)), const((D_MODEL, D_MODEL)), const((MIX_W, 1)),
                    const((1, D_MODEL)), const((1, D_MODEL)), const((N_EXPERTS, D_MODEL))],
        out_specs=(tok(D_MODEL), tok(D_MODEL // 2), pl.BlockSpec((1, N_EXPERTS, TF), lambda b, i: (b, 0, i))),
        out_shape=(jax.ShapeDtypeStruct((b, s, D_MODEL), F32),
                   jax.ShapeDtypeStruct((b, s, D_MODEL // 2), jnp.int32),
                   jax.ShapeDtypeStruct((b, N_EXPERTS, s), F32)),
        compiler_params=_cparams(2),
    )(x, ya, *att_list, hf, hb, co, yd, wg, bg, wbr, wout, mng, lng, lnb, wr_t)


TT = 256


def _select_kernel(aff_ref, slot_ref, *, cap):
    s = aff_ref.shape[2]
    bits = pltpu.bitcast(aff_ref[0], jnp.int32)

    def bit_step(i, thr):
        cand = thr | jnp.left_shift(jnp.int32(1), 30 - i)
        cnt = jnp.sum((bits >= cand).astype(jnp.int32), axis=1, keepdims=True)
        return jnp.where(cnt >= cap, cand, thr)

    thr = lax.fori_loop(0, 31, bit_step, jnp.zeros((N_EXPERTS, 1), jnp.int32))
    gt = bits > thr
    eq = bits == thr
    need = (cap - jnp.sum(gt.astype(jnp.int32), axis=1, keepdims=True)).astype(F32)
    upper = (lax.broadcasted_iota(jnp.int32, (TT, TT), 0)
             <= lax.broadcasted_iota(jnp.int32, (TT, TT), 1)).astype(BF16)
    eq_before = jnp.zeros((N_EXPERTS, 1), F32)
    sel_before = jnp.zeros((N_EXPERTS, 1), F32)
    for j in range(s // TT):
        cols = slice(j * TT, (j + 1) * TT)
        eq_j = eq[:, cols]
        eq_incl = eq_before + jnp.dot(eq_j.astype(BF16), upper, preferred_element_type=F32)
        sel_j = gt[:, cols] | (eq_j & (eq_incl <= need))
        sel_f = sel_j.astype(F32)
        sel_incl = sel_before + jnp.dot(sel_f.astype(BF16), upper, preferred_element_type=F32)
        slot_ref[0, :, cols] = jnp.where(sel_j, sel_incl - 1.0, -1.0).astype(jnp.int32)
        eq_before = eq_incl[:, TT - 1:TT]
        sel_before = sel_incl[:, TT - 1:TT]


def _select_call(aff_t, cap):
    b, e, s = aff_t.shape
    return pl.pallas_call(
        functools.partial(_select_kernel, cap=cap), name="expert_choice_select",
        grid=(b,),
        in_specs=[pl.BlockSpec((1, e, s), lambda i: (i, 0, 0))],
        out_specs=pl.BlockSpec((1, e, s), lambda i: (i, 0, 0)),
        out_shape=jax.ShapeDtypeStruct((b, e, s), jnp.int32),
        compiler_params=_cparams(1),
    )(aff_t)


SC_LANES = 16
SC_ROWS = 64
SC_IDX = 128
SC_SLAB = 128
SC_ZROWS = 64
CF = 1024
EXPERT_GROUPS = 1


def _sc_dispatch_call(x_flat, slot2, aff2, seq, cap, experts):
    n_pairs = slot2.shape[0]
    d = x_flat.shape[1]
    info = plsc.get_sparse_core_info()
    n_workers = info.num_cores * info.num_subcores
    assert n_workers % n_pairs == 0 and seq % SC_LANES == 0
    parts = n_workers // n_pairs
    part_rows = cap // parts
    assert part_rows % (2 * SC_ROWS) == 0
    mesh = plsc.VectorSubcoreMesh(core_axis_name="c", subcore_axis_name="s")

    @functools.partial(
        pl.kernel, mesh=mesh, name="expert_dispatch_sc",
        compiler_params=pltpu.CompilerParams(needs_layout_passes=False),
        out_type=(jax.ShapeDtypeStruct((n_pairs * cap, d), x_flat.dtype),
                  jax.ShapeDtypeStruct((n_pairs, cap), jnp.int32),
                  jax.ShapeDtypeStruct((n_pairs, cap), F32)),
        scratch_types=[pltpu.VMEM((seq,), jnp.int32), pltpu.VMEM((seq,), F32),
                       pltpu.VMEM((cap,), jnp.int32), pltpu.VMEM((cap,), jnp.int32),
                       pltpu.VMEM((cap,), F32),
                       pltpu.VMEM((2, SC_ROWS, d), x_flat.dtype), pltpu.SemaphoreType.DMA((2,))])
    def dispatch(x_hbm, slot_hbm, aff_hbm, xs_hbm, tok_hbm, gate_hbm,
                 slot_v, aff_v, idx_v, tok_v, gate_v, rows_v, sem):
        worker = lax.axis_index("s") * info.num_cores + lax.axis_index("c")
        lane = lax.iota(jnp.int32, SC_LANES)

        def gather(c0, buf):
            return pltpu.make_async_copy(x_hbm.at[idx_v.at[pl.ds(c0, SC_ROWS)]], rows_v.at[buf], sem.at[buf])
        pair = worker // parts
        first = (worker % parts) * part_rows
        row0 = (pair // experts) * seq
        pltpu.sync_copy(slot_hbm.at[pair], slot_v)
        pltpu.sync_copy(aff_hbm.at[pair], aff_v)

        @pl.loop(0, seq, step=SC_LANES)
        def _(t0):
            sv = slot_v[pl.ds(t0, SC_LANES)]
            picked = sv >= 0
            plsc.store_scatter(tok_v, [sv], t0 + lane, mask=picked)
            plsc.store_scatter(idx_v, [sv], row0 + t0 + lane, mask=picked)
            plsc.store_scatter(gate_v, [sv], aff_v[pl.ds(t0, SC_LANES)], mask=picked)

        @pl.when(first == 0)
        def _():
            pltpu.sync_copy(tok_v, tok_hbm.at[pair])
            pltpu.sync_copy(gate_v, gate_hbm.at[pair])

        gather(first, 0).start()

        @pl.loop(0, part_rows, step=2 * SC_ROWS)
        def _(r0):
            c0 = first + r0
            gather(c0 + SC_ROWS, 1).start()
            gather(c0, 0).wait()
            pltpu.sync_copy(rows_v.at[0], xs_hbm.at[pl.ds(pair * cap + c0, SC_ROWS)])

            @pl.when(r0 + 2 * SC_ROWS < part_rows)
            def _():
                gather(c0 + 2 * SC_ROWS, 0).start()

            gather(c0 + SC_ROWS, 1).wait()
            pltpu.sync_copy(rows_v.at[1], xs_hbm.at[pl.ds(pair * cap + c0 + SC_ROWS, SC_ROWS)])

    return dispatch(x_flat, slot2, aff2)


def _sc_row_gather_call(srcs, idxs):
    info = plsc.get_sparse_core_info()
    n_workers = info.num_cores * info.num_subcores
    n = idxs[0].shape[0]
    width = srcs[0].shape[1]
    per_worker = n // n_workers
    assert n % n_workers == 0 and per_worker % (2 * SC_ROWS) == 0
    assert all(i.shape == (n,) for i in idxs) and all(s.shape[1] == width and s.dtype == srcs[0].dtype for s in srcs)
    k = len(srcs)
    mesh = plsc.VectorSubcoreMesh(core_axis_name="c", subcore_axis_name="s")

    @functools.partial(
        pl.kernel, mesh=mesh, name="row_gather_sc",
        compiler_params=pltpu.CompilerParams(needs_layout_passes=False),
        out_type=tuple(jax.ShapeDtypeStruct((n, width), srcs[0].dtype) for _ in range(k)),
        scratch_types=[pltpu.VMEM((per_worker,), jnp.int32),
                       pltpu.VMEM((2, SC_ROWS, width), srcs[0].dtype), pltpu.SemaphoreType.DMA((2,))])
    def gather_rows(*refs):
        src_refs, idx_refs, out_refs = refs[:k], refs[k:2 * k], refs[2 * k:3 * k]
        idx_v, rows_v, sem = refs[3 * k:]
        worker = lax.axis_index("s") * info.num_cores + lax.axis_index("c")
        base = worker * per_worker
        for src, idx_hbm, out in zip(src_refs, idx_refs, out_refs):
            pltpu.sync_copy(idx_hbm.at[pl.ds(base, per_worker)], idx_v)

            def gather(r0, buf):
                return pltpu.make_async_copy(src.at[idx_v.at[pl.ds(r0, SC_ROWS)]], rows_v.at[buf], sem.at[buf])

            gather(0, 0).start()

            @pl.loop(0, per_worker, step=2 * SC_ROWS)
            def _(r0):
                gather(r0 + SC_ROWS, 1).start()
                gather(r0, 0).wait()
                pltpu.sync_copy(rows_v.at[0], out.at[pl.ds(base + r0, SC_ROWS)])

                @pl.when(r0 + 2 * SC_ROWS < per_worker)
                def _():
                    gather(r0 + 2 * SC_ROWS, 0).start()

                gather(r0 + SC_ROWS, 1).wait()
                pltpu.sync_copy(rows_v.at[1], out.at[pl.ds(base + r0 + SC_ROWS, SC_ROWS)])

    return gather_rows(*srcs, *idxs)


def _residue_index_lists(batch, seq, dil):
    l = seq // dil
    b_i, r_i, l_i = np.meshgrid(np.arange(batch), np.arange(dil), np.arange(l), indexing="ij")
    to_grouped = (b_i * seq + l_i * dil + r_i).reshape(-1).astype(np.int32)
    b_j, t_j = np.meshgrid(np.arange(batch), np.arange(seq), indexing="ij")
    to_natural = (b_j * seq + (t_j % dil) * l + t_j // dil).reshape(-1).astype(np.int32)
    return jnp.asarray(to_grouped), jnp.asarray(to_natural)


def _expert_kernel(xs_ref, g_ref, w1_ref, w3_ref, w2_ref, ye_ref, w1_bf, w3_bf, w2_bf):
    @pl.when((pl.program_id(1) == 0) & (pl.program_id(2) == 0))
    def _():
        w1_bf[...] = w1_ref[0, 0].astype(BF16)
        w3_bf[...] = w3_ref[0, 0].astype(BF16)
        w2_bf[...] = w2_ref[0, 0].astype(BF16)

    xs = jnp.concatenate(_unpack_bf16_pair(xs_ref[0, 0]), axis=1)
    hid = (jax.nn.silu(jnp.dot(xs, w1_bf[...], preferred_element_type=F32))
           * jnp.dot(xs, w3_bf[...], preferred_element_type=F32))
    ye_ref[0, 0] = jnp.dot(hid.astype(BF16), w2_bf[...], preferred_element_type=F32) * g_ref[0, 0]


def _expert_call(xs4, gate4, w1, w3, w2, layer, first_expert):
    b, e, cap, half = xs4.shape
    d, ff = w1.shape[2], w1.shape[3]
    assert d == 2 * half
    rows = lambda w: pl.BlockSpec((1, 1, CF, w), lambda e, b, j: (b, e, j, 0))
    wspec = lambda r, c: pl.BlockSpec((1, 1, r, c), lambda e, b, j: (layer, first_expert + e, 0, 0))
    return pl.pallas_call(
        _expert_kernel, name="expert_ffn",
        grid=(e, b, cap // CF),
        in_specs=[rows(half), rows(1), wspec(d, ff), wspec(d, ff), wspec(ff, d)],
        out_specs=rows(d),
        out_shape=jax.ShapeDtypeStruct((b, e, cap, d), F32),
        scratch_shapes=[pltpu.VMEM((d, ff), BF16), pltpu.VMEM((d, ff), BF16), pltpu.VMEM((ff, d), BF16)],
        compiler_params=_cparams(3),
    )(xs4, gate4, w1, w3, w2)


def _sc_combine_call(ye_flat, tok3, seq, experts):
    n_pairs, n_chunks_all, _ = tok3.shape
    cap = n_chunks_all * SC_IDX
    d = ye_flat.shape[1]
    nb = n_pairs // experts
    info = plsc.get_sparse_core_info()
    assert info.num_subcores % experts == 0 and nb % info.num_cores == 0
    parts = info.num_subcores // experts
    n_chunks = n_chunks_all // parts
    assert n_chunks_all % parts == 0 and n_chunks % 2 == 0
    assert seq % (info.num_subcores * SC_ZROWS) == 0 and d % SC_SLAB == 0
    batches_per_core = nb // info.num_cores
    own_rows = seq // info.num_subcores
    mesh = plsc.VectorSubcoreMesh(core_axis_name="c", subcore_axis_name="s")

    @functools.partial(
        pl.kernel, mesh=mesh, name="expert_combine_sc",
        compiler_params=pltpu.CompilerParams(needs_layout_passes=False),
        out_type=jax.ShapeDtypeStruct((nb * seq, d), F32),
        scratch_types=[pltpu.VMEM_SHARED((seq, SC_SLAB), F32),
                       pltpu.VMEM((n_chunks_all, SC_IDX), jnp.int32),
                       pltpu.VMEM((2, SC_IDX, SC_SLAB), F32),
                       pltpu.VMEM((SC_ZROWS, SC_SLAB), F32),
                       pltpu.SemaphoreType.DMA((2,))])
    def combine(ye_hbm, tok_hbm, out_hbm, acc_sh, tok_v, rows_v, zero_v, sem):
        core = lax.axis_index("c")
        sub = lax.axis_index("s")

        @pl.loop(0, SC_ZROWS)
        def _(r):
            for l0 in range(0, SC_SLAB, SC_LANES):
                zero_v[r, pl.ds(l0, SC_LANES)] = jnp.zeros((SC_LANES,), F32)

        for bb in range(batches_per_core):
            batch = core * batches_per_core + bb
            pair = batch * experts + sub // parts
            first = (sub % parts) * n_chunks
            pltpu.sync_copy(tok_hbm.at[pair], tok_v)

            @pl.loop(0, d // SC_SLAB)
            def _(slab):
                cols = pl.ds(pl.multiple_of(slab * SC_SLAB, SC_SLAB), SC_SLAB)

                @pl.loop(0, own_rows, step=SC_ZROWS)
                def _(r0):
                    pltpu.sync_copy(zero_v, acc_sh.at[pl.ds(sub * own_rows + r0, SC_ZROWS)])

                def load(j, buf):
                    return pltpu.make_async_copy(
                        ye_hbm.at[pl.ds(pair * cap + (first + j) * SC_IDX, SC_IDX), cols],
                        rows_v.at[buf], sem.at[buf])

                load(0, 0).start()
                plsc.subcore_barrier()

                for j in range(0, n_chunks, 2):
                    load(j + 1, 1).start()
                    load(j, 0).wait()
                    pltpu.sync_copy(rows_v.at[0], acc_sh.at[tok_v.at[first + j]], add=True)
                    if j + 2 < n_chunks:
                        load(j + 2, 0).start()
                    load(j + 1, 1).wait()
                    pltpu.sync_copy(rows_v.at[1], acc_sh.at[tok_v.at[first + j + 1]], add=True)

                plsc.subcore_barrier()
                pltpu.sync_copy(acc_sh.at[pl.ds(sub * own_rows, own_rows)],
                                out_hbm.at[pl.ds(batch * seq + sub * own_rows, own_rows), cols])

    return combine(ye_flat, tok3)


TN = 512


def _resln_kernel(x_ref, *refs, alpha):
    *y_refs, g_ref, b_ref, o_ref = refs
    y = sum(r[...] for r in y_refs[1:]) + y_refs[0][...]
    o_ref[...] = _standardize(alpha * x_ref[...] + y) * g_ref[...] + b_ref[...]


def _resln_call(x2d, y_groups, g, bta, alpha):
    n, d = x2d.shape
    tok = lambda w: pl.BlockSpec((TN, w), lambda i: (i, 0))
    vec = pl.BlockSpec((1, d), lambda i: (0, 0))
    return pl.pallas_call(
        functools.partial(_resln_kernel, alpha=alpha), name="residual_layernorm",
        grid=(n // TN,), in_specs=[tok(d)] * (1 + len(y_groups)) + [vec, vec], out_specs=tok(d),
        out_shape=jax.ShapeDtypeStruct((n, d), F32),
        compiler_params=_cparams(1),
    )(x2d, *y_groups, g, bta)


def _expert_choice_ffn(x1p, aff_t, w1, w3, w2, layer):
    b, s, half = x1p.shape
    d = 2 * half
    cap = EC_FACTOR * s // N_EXPERTS
    slot = _select_call(aff_t, cap)
    eg = N_EXPERTS // EXPERT_GROUPS
    partials = []
    for grp in range(EXPERT_GROUPS):
        sel = slice(grp * eg, (grp + 1) * eg)
        xs, tok, gate = _sc_dispatch_call(x1p.reshape(b * s, half), slot[:, sel].reshape(b * eg, s),
                                          aff_t[:, sel].reshape(b * eg, s), s, cap, eg)
        ye = _expert_call(xs.reshape(b, eg, cap, half), gate.reshape(b, eg, cap, 1), w1, w3, w2, layer,
                          grp * eg)
        partials.append(_sc_combine_call(ye.reshape(b * eg * cap, d),
                                         tok.reshape(b * eg, cap // SC_IDX, SC_IDX), s, eg))
    return tuple(partials)


def _pack_pool(pool_w):
    g, gd, _ = pool_w.shape
    out = jnp.zeros((g * gd, g * gd), F32)
    for i in range(g):
        out = out.at[i * gd:(i + 1) * gd, i * gd:(i + 1) * gd].set(pool_w[i])
    return out.astype(BF16)


def _layer(layer, x, pending, alpha, bias_tiles, w_in, b_in, gm_ln_g, gm_ws, gm_bs, ml_conv, ml_fbias,
           ml_norm_g, pool_w, pool_scale, w_branch, w_out, ln1_g, ln1_b, w_router, w_e1, w_e3, w_e2):
    b, s, d = x.shape
    n_small = 2576
    w_cat, b_cat = _pack_inproj_weights(w_in, b_in)
    wscat, bsfull = _pack_gmlp(gm_ws, gm_bs)
    if pending is None:
        outs = _inproj_call(x.reshape(b * s, d), w_cat, b_cat, gm_ln_g[None], wscat, bsfull, b)
    else:
        *outs, x2 = _inproj_call(pending, w_cat, b_cat, gm_ln_g[None], wscat, bsfull, b, alpha)
        x = x2.reshape(b, s, d)
    ya, qkvp, cqk, cv, co, dx, ig, fg = outs
    r3 = lambda t: t.reshape(b, s, t.shape[-1])
    dils = [dil for _, dil in DIL_PATTERNS[1:]]
    lists = [_residue_index_lists(b, s, dil) for dil in dils]
    grouped = _sc_row_gather_call([qkvp] * len(dils), [to_g for to_g, _ in lists])
    att = [_attn_call(qkvp.reshape(b, 1, s, QKV_HALF), bias_tiles[0]).reshape(b, s, MIX_W + LANES)]
    att_grouped = [_attn_call(g.reshape(b, dil, s // dil, QKV_HALF), bias).reshape(b * s, MIX_W + LANES)
                   for g, dil, bias in zip(grouped, dils, bias_tiles[1:])]
    att += [t.reshape(b, s, MIX_W + LANES)
            for t in _sc_row_gather_call(att_grouped, [to_n for _, to_n in lists])]
    hf, hb = _mlstm_branch(r3(cqk), r3(cv), r3(ig), r3(fg), ml_conv, ml_fbias)
    yd = _pool_call(r3(dx), _pack_pool(pool_w), pool_scale[None])
    x1, x1p, aff_t = _merge_call(
        x, r3(ya), att, hf, hb, r3(co), yd,
        w_in[:, n_small:].astype(BF16), b_in[None, n_small:], w_branch.astype(BF16), w_out.astype(BF16),
        ml_norm_g[:, None], ln1_g[None], ln1_b[None], jnp.transpose(w_router).astype(BF16), alpha)
    ffn = _expert_choice_ffn(x1p, aff_t, w_e1, w_e3, w_e2, layer)
    return x1.reshape(b * s, d), ffn


def kernel(x, w_in, b_in, gm_ln_g, gm_ws, gm_bs, rel_bias, ml_conv, ml_fbias, ml_norm_g, pool_w,
           pool_scale, w_branch, w_out, ln1_g, ln1_b, w_router, w_e1, w_e3, w_e2, ln2_g, ln2_b):
    depth = w_in.shape[0]
    alpha = (2 * depth) ** 0.25
    bias_tiles = [_attn_bias_tile(rel_bias, window, dil) for window, dil in DIL_PATTERNS]
    b, s, d = x.shape
    pending = None
    for l in range(depth):
        x1, ffn = _layer(l, x, pending, alpha, bias_tiles, w_in[l], b_in[l], gm_ln_g[l], gm_ws[l], gm_bs[l],
                         ml_conv[l], ml_fbias[l], ml_norm_g[l], pool_w[l], pool_scale[l], w_branch[l],
                         w_out[l], ln1_g[l], ln1_b[l], w_router[l], w_e1, w_e3, w_e2)
        pending = (x1, ffn, ln2_g[l][None], ln2_b[l][None])
    return _resln_call(*pending, alpha).reshape(b, s, d)
```

```python
import functools
import math

import jax
import jax.numpy as jnp
import numpy as np
from jax import lax
from jax.experimental import pallas as pl
from jax.experimental.pallas import tpu as pltpu
from jax.experimental.pallas import tpu_sc as plsc

F32 = jnp.float32
BF16 = jnp.bfloat16

D_MODEL = 1024
MIX_W = 256
N_BRANCH = 4
GM_CHUNK = 128
GM_GROUPS = 4
ATT_HEADS = 4
ATT_HD = 64
DIL_PATTERNS = ((128, 1), (512, 4), (2048, 16))
ATT_BLOCK = 64
REL_BUCKETS = 32
REL_MAX_DIST = 1024
ML_HEADS = 4
ML_HD = 64
ML_CHUNK = 64
POOL_WINDOWS = (2, 4, 8, 16)
N_EXPERTS = 16
EXPERT_FF = 1024
EC_FACTOR = 2
LN_EPS = 1e-5
NEG_BIG = -1e30

V7X_VMEM_LIMIT = 56 * 1024 * 1024
LANES = 128
HALO = 8


def _cparams(n_grid, vmem=V7X_VMEM_LIMIT):
    return pltpu.CompilerParams(dimension_semantics=("arbitrary",) * n_grid,
                                vmem_limit_bytes=vmem)


def _pack_bf16_pair(lo, hi):
    lo_bits = lax.shift_right_logical(pltpu.bitcast(lo.astype(F32), jnp.int32), 16)
    return pltpu.bitcast(hi.astype(F32), jnp.int32) | lo_bits


def _unpack_bf16_pair(packed):
    lo = pltpu.bitcast(lax.shift_left(packed, 16), F32).astype(BF16)
    hi = pltpu.bitcast(packed & jnp.int32(-65536), F32).astype(BF16)
    return lo, hi


def _standardize(xf):
    mu = jnp.mean(xf, axis=-1, keepdims=True)
    var = jnp.mean(jnp.square(xf - mu), axis=-1, keepdims=True)
    return (xf - mu) * lax.rsqrt(var + LN_EPS)


TA = 512
A_COLS = 2560 + LANES


def _inproj_kernel(*refs, alpha):
    if alpha is None:
        x_ref, *refs = refs
        x = x_ref[...]
    else:
        x1_ref, y_ref, g2_ref, b2_ref, *refs = refs
        x = _standardize(alpha * x1_ref[...] + y_ref[...]) * g2_ref[...] + b2_ref[...]
        refs[-2][...] = x
        refs = refs[:-2] + refs[-1:]
    (w_ref, b_ref, lng_ref, wscat_ref, bsfull_ref,
     ya_ref, qkv1_ref, qkv4_ref, qkv16_ref, cqk_ref, cv_ref, co_ref, dx_ref, gt_ref, qkv_scr) = refs
    xb = x.astype(BF16)
    h = jnp.dot(xb, w_ref[...], preferred_element_type=F32) + b_ref[...]
    qkv1_ref[0, 0] = h[:, 512:1280].astype(BF16)
    for c in range(768 // LANES):
        qkv_scr[c] = h[:, 512 + c * LANES:512 + (c + 1) * LANES]
    for (_, dil), out_ref in zip(DIL_PATTERNS[1:], (qkv4_ref, qkv16_ref)):
        for r in range(dil):
            for c in range(768 // LANES):
                out_ref[0, r, :, c * LANES:(c + 1) * LANES] = (
                    qkv_scr[c, pl.ds(r, TA // dil, stride=dil), :].astype(BF16))
    cqk_ref[...] = h[:, 1280:1792]
    cv_ref[...] = h[:, 1792:2048].astype(BF16)
    co_ref[...] = h[:, 2048:2304]
    dx_ref[...] = h[:, 2304:2560]
    gates_t = jnp.transpose(h[:, 2560:2688])
    for j in range(TA // LANES):
        gt_ref[j] = gates_t[0:4 * ML_HEADS, j * LANES:(j + 1) * LANES]
    u = jax.nn.gelu(h[:, 0:256])
    v = jax.nn.gelu(h[:, 256:512])
    vn = _standardize(v) * lng_ref[...]
    lane_grp = lax.broadcasted_iota(jnp.int32, (GM_CHUNK, MIX_W), 1) // (MIX_W // GM_GROUPS)
    for c in range(TA // GM_CHUNK):
        vc = vn[c * GM_CHUNK:(c + 1) * GM_CHUNK]
        stacked = jnp.concatenate(
            [jnp.where(lane_grp == g, vc, 0.0).astype(BF16) for g in range(GM_GROUPS)], axis=0)
        mixed = jnp.dot(wscat_ref[...], stacked, preferred_element_type=F32) + bsfull_ref[...]
        ya_ref[c * GM_CHUNK:(c + 1) * GM_CHUNK, :] = (
            u[c * GM_CHUNK:(c + 1) * GM_CHUNK] * mixed).astype(BF16)


def _inproj_call(x_in, w_cat, b_cat, lng, wscat, bsfull, batch, alpha=None):
    fused = alpha is not None
    n = (x_in[0] if fused else x_in).shape[0]
    seq = n // batch
    tpb = seq // TA
    tok = lambda w: pl.BlockSpec((TA, w), lambda i: (i, 0))
    const = lambda s: pl.BlockSpec(s, lambda i: (0,) * len(s))
    regrouped = lambda dil: pl.BlockSpec((1, dil, TA // dil, 768), lambda i: (i // tpb, 0, i % tpb, 0))
    out_shape = (
        jax.ShapeDtypeStruct((n, 256), BF16),
    ) + tuple(jax.ShapeDtypeStruct((batch, dil, seq // dil, 768), BF16)
              for _, dil in DIL_PATTERNS) + (
        jax.ShapeDtypeStruct((n, 512), F32),
        jax.ShapeDtypeStruct((n, 256), BF16),
        jax.ShapeDtypeStruct((n, 256), F32),
        jax.ShapeDtypeStruct((n, 256), F32),
        jax.ShapeDtypeStruct((n // LANES, 4 * ML_HEADS, LANES), F32),
    )
    x_specs = [tok(D_MODEL), tok(D_MODEL), const((1, D_MODEL)), const((1, D_MODEL))] if fused else [tok(D_MODEL)]
    out_specs = ((tok(256),) + tuple(regrouped(dil) for _, dil in DIL_PATTERNS)
                 + (tok(512), tok(256), tok(256), tok(256),
                    pl.BlockSpec((TA // LANES, 4 * ML_HEADS, LANES), lambda i: (i, 0, 0))))
    if fused:
        out_specs += (tok(D_MODEL),)
        out_shape += (jax.ShapeDtypeStruct((n, D_MODEL), F32),)
    return pl.pallas_call(
        functools.partial(_inproj_kernel, alpha=alpha), name="inproj_gmlp",
        grid=(n // TA,),
        in_specs=x_specs + [const((D_MODEL, A_COLS)), const((1, A_COLS)), const((1, MIX_W)),
                            const((GM_CHUNK, GM_GROUPS * GM_CHUNK)), const((GM_CHUNK, MIX_W))],
        out_specs=out_specs,
        out_shape=out_shape,
        scratch_shapes=[pltpu.VMEM((768 // LANES, TA, LANES), F32)],
        compiler_params=_cparams(1),
    )(*(x_in if fused else (x_in,)), w_cat, b_cat, lng, wscat, bsfull)


def _pack_inproj_weights(w_in, b_in):
    pad = lambda a: jnp.pad(a, ((0, 0), (0, LANES - 4 * ML_HEADS)))
    w_cat = jnp.concatenate([w_in[:, 0:2304], w_in[:, 2320:2576], pad(w_in[:, 2304:2320])], axis=1)
    b2 = b_in[None, :]
    b_cat = jnp.concatenate([b2[:, 0:2304], b2[:, 2320:2576], pad(b2[:, 2304:2320])], axis=1)
    return w_cat.astype(BF16), b_cat


def _pack_gmlp(gm_ws, gm_bs):
    wscat = jnp.transpose(gm_ws, (1, 0, 2)).reshape(GM_CHUNK, GM_GROUPS * GM_CHUNK).astype(BF16)
    bsfull = jnp.repeat(jnp.transpose(gm_bs), MIX_W // GM_GROUPS, axis=1)
    return wscat, bsfull


def _halo_specs(t, width, n_tiles):
    r = t // HALO
    main = pl.BlockSpec((1, t, width), lambda b, i: (b, i, 0))
    prev = pl.BlockSpec((1, HALO, width), lambda b, i: (b, jnp.maximum(i * r - 1, 0), 0))
    nxt = pl.BlockSpec((1, HALO, width), lambda b, i: (b, jnp.minimum((i + 1) * r, n_tiles * r - 1), 0))
    return main, prev, nxt


def _fill_halo_scratch(buf, x_ref, p_ref, n_ref, t):
    i = pl.program_id(1)
    last = pl.num_programs(1) - 1
    buf[0:HALO, :] = jnp.where(i > 0, p_ref[0], 0.0)
    buf[HALO:HALO + t, :] = x_ref[0]
    buf[HALO + t:2 * HALO + t, :] = jnp.where(i < last, n_ref[0], 0.0)


TP = 512


def _pool_kernel(x_ref, p_ref, n_ref, w_ref, sc_ref, o_ref, buf):
    _fill_halo_scratch(buf, x_ref, p_ref, n_ref, TP)
    seq = pl.num_programs(1) * TP
    pos = pl.program_id(1) * TP + lax.broadcasted_iota(jnp.int32, (TP, 1), 0)
    lane_grp = lax.broadcasted_iota(jnp.int32, (TP, MIX_W), 1) // (MIX_W // len(POOL_WINDOWS))
    x0 = buf[HALO:HALO + TP, :]
    pooled = jnp.zeros((TP, MIX_W), F32)
    acc = None
    half_done = 0
    for gi, win in enumerate(POOL_WINDOWS):
        half = win // 2
        for o in list(range(-half, -half_done)) + list(range(half_done, half)):
            term = buf[HALO + o:HALO + o + TP, :]
            acc = term if acc is None else acc + term
        half_done = half
        cnt = (jnp.minimum(pos + half, seq) - jnp.maximum(pos - half, 0)).astype(F32)
        pooled = jnp.where(lane_grp == gi, acc / cnt - x0, pooled)
    mixed = jnp.dot(pooled.astype(BF16), w_ref[...], preferred_element_type=F32)
    o_ref[0] = (mixed * sc_ref[...]).astype(BF16)


def _pool_call(dx, w_block, scale):
    b, s, _ = dx.shape
    nt = s // TP
    main, prev, nxt = _halo_specs(TP, MIX_W, nt)
    return pl.pallas_call(
        _pool_kernel, name="pool_mixer",
        grid=(b, nt),
        in_specs=[main, prev, nxt,
                  pl.BlockSpec((MIX_W, MIX_W), lambda b, i: (0, 0)),
                  pl.BlockSpec((1, MIX_W), lambda b, i: (0, 0))],
        out_specs=pl.BlockSpec((1, TP, MIX_W), lambda b, i: (b, i, 0)),
        out_shape=jax.ShapeDtypeStruct((b, s, MIX_W), BF16),
        scratch_shapes=[pltpu.VMEM((TP + 2 * HALO, MIX_W), F32)],
        compiler_params=_cparams(2),
    )(dx, dx, dx, w_block, scale)


TQ = 128
TQS = 512
TKEYS = TQ + 2 * ATT_BLOCK


def _attn_kernel(q_ref, kp_ref, km_ref, kn_ref, vp_ref, vm_ref, vn_ref, bias_ref, o_ref, lse_ref):
    i = pl.program_id(2)
    seq = pl.num_programs(2) * TQS
    q = q_ref[0, 0] * ATT_HD ** -0.5
    k = jnp.concatenate([kp_ref[0, 0], km_ref[0, 0], kn_ref[0, 0]], axis=0)
    v = jnp.concatenate([vp_ref[0, 0], vm_ref[0, 0], vn_ref[0, 0]], axis=0)
    lane = lax.broadcasted_iota(jnp.int32, (TQ, LANES), 1)
    lane_half = lax.broadcasted_iota(jnp.int32, (1, LANES), 1) // ATT_HD
    keep = [jnp.where(lane_half == hh, 1.0, 0.0).astype(BF16) for hh in range(2)]
    for j in range(TQS // TQ):
        kpos = i * TQS + j * TQ - ATT_BLOCK + lax.broadcasted_iota(jnp.int32, (1, TKEYS), 1)
        kvalid = (kpos >= 0) & (kpos < seq)
        qrows = slice(j * TQ, (j + 1) * TQ)
        krows = slice(j * TQ, j * TQ + TKEYS)
        lse_tile = jnp.zeros((TQ, LANES), F32)
        for pair in range(ATT_HEADS // 2):
            grp = slice(pair * LANES, (pair + 1) * LANES)
            q_pair, k_pair, v_pair = q[qrows, grp], k[krows, grp], v[krows, grp]
            o_pair = jnp.zeros((TQ, LANES), F32)
            for hh in range(2):
                h = 2 * pair + hh
                logits = lax.dot_general(q_pair * keep[hh], k_pair, (((1,), (1,)), ((), ())),
                                         preferred_element_type=F32) + bias_ref[h]
                logits = jnp.where(kvalid, logits, NEG_BIG)
                m = jnp.max(logits, axis=-1, keepdims=True)
                p = jnp.exp(logits - m)
                ssum = jnp.sum(p, axis=-1, keepdims=True)
                o = jnp.dot(p.astype(BF16), v_pair, preferred_element_type=F32) / ssum
                o_pair = jnp.where(lane_half == hh, o, o_pair)
                lse_tile = jnp.where(lane == h, m + jnp.log(ssum), lse_tile)
            o_ref[0, 0, qrows, grp] = o_pair
        lse_ref[0, 0, qrows, :] = lse_tile


def _attn_call(qkv, bias):
    b, dil, l, _ = qkv.shape
    nt = l // TQS
    r64 = TQS // ATT_BLOCK
    main = lambda c: pl.BlockSpec((1, 1, TQS, MIX_W), lambda b, r, i: (b, r, i, c))
    prev = lambda c: pl.BlockSpec((1, 1, ATT_BLOCK, MIX_W),
                                  lambda b, r, i: (b, r, jnp.maximum(i * r64 - 1, 0), c))
    nxt = lambda c: pl.BlockSpec((1, 1, ATT_BLOCK, MIX_W),
                                 lambda b, r, i: (b, r, jnp.minimum((i + 1) * r64, nt * r64 - 1), c))
    return pl.pallas_call(
        _attn_kernel, name="band_attention",
        grid=(b, dil, nt),
        in_specs=[main(0), prev(1), main(1), nxt(1), prev(2), main(2), nxt(2),
                  pl.BlockSpec((ATT_HEADS, TQ, TKEYS), lambda b, r, i: (0, 0, 0))],
        out_specs=(pl.BlockSpec((1, 1, TQS, MIX_W), lambda b, r, i: (b, r, i, 0)),
                   pl.BlockSpec((1, 1, TQS, LANES), lambda b, r, i: (b, r, i, 0))),
        out_shape=(jax.ShapeDtypeStruct((b, dil, l, MIX_W), F32),
                   jax.ShapeDtypeStruct((b, dil, l, LANES), F32)),
        compiler_params=_cparams(3),
    )(qkv, qkv, qkv, qkv, qkv, qkv, qkv, bias)


def _t5_bucket_static(rel):
    half = REL_BUCKETS // 2
    max_exact = half // 2
    ret = np.where(rel > 0, half, 0)
    n = np.abs(rel)
    nf = np.maximum(n, 1).astype(np.float32)
    large = max_exact + (np.log(nf / np.float32(max_exact)) / np.float32(math.log(REL_MAX_DIST / max_exact))
                         * np.float32(half - max_exact)).astype(np.int32)
    large = np.minimum(large, half - 1)
    return ret + np.where(n < max_exact, n, large)


def _attn_bias_tile(rel_bias, window, dil):
    side = (window // 2) // dil
    rel = np.arange(TKEYS)[None, :] - ATT_BLOCK - np.arange(TQ)[:, None]
    onehot = jax.nn.one_hot(jnp.asarray(_t5_bucket_static(dil * rel), jnp.int32), REL_BUCKETS, dtype=F32)
    bias = jnp.einsum('qkr,rh->hqk', onehot, rel_bias, precision=lax.Precision.HIGHEST)
    return jnp.where(jnp.asarray(np.abs(rel) <= side)[None], bias, NEG_BIG)


TM = 512
VT_ROWS = ML_HD + 16


def _mlprep_kernel(x_ref, p_ref, n_ref, v_ref, w_ref, qt_out, k_out, vt_out, buf):
    _fill_halo_scratch(buf, x_ref, p_ref, n_ref, TM)
    conv = (buf[HALO - 1:HALO - 1 + TM, :] * w_ref[0:1, :] + buf[HALO:HALO + TM, :] * w_ref[1:2, :]
            + buf[HALO + 1:HALO + 1 + TM, :] * w_ref[2:3, :])
    qk = jax.nn.silu(conv)
    qt = jnp.transpose(qk[:, :MIX_W])
    vt = jnp.transpose(v_ref[0].astype(F32))
    ones_rows = jnp.where(lax.broadcasted_iota(jnp.int32, (VT_ROWS - ML_HD, ML_CHUNK), 0) == 0, 1.0, 0.0)
    for h in range(ML_HEADS):
        sl = slice(h * ML_HD, (h + 1) * ML_HD)
        k_out[0, h] = (qk[:, MIX_W + h * ML_HD:MIX_W + (h + 1) * ML_HD] * ML_HD ** -0.5).astype(BF16)
        for c in range(TM // ML_CHUNK):
            cl = slice(c * ML_CHUNK, (c + 1) * ML_CHUNK)
            qt_out[0, h, c] = qt[sl, cl].astype(BF16)
            vt_out[0, h, c] = jnp.concatenate([vt[sl, cl], ones_rows], axis=0).astype(BF16)


def _mlprep_call(cqk, cv, conv_w):
    b, s, _ = cqk.shape
    nt = s // TM
    nc = s // ML_CHUNK
    cpt = TM // ML_CHUNK
    main, prev, nxt = _halo_specs(TM, 2 * MIX_W, nt)
    return pl.pallas_call(
        _mlprep_kernel, name="mlstm_prep",
        grid=(b, nt),
        in_specs=[main, prev, nxt,
                  pl.BlockSpec((1, TM, MIX_W), lambda b, i: (b, i, 0)),
                  pl.BlockSpec((3, 2 * MIX_W), lambda b, i: (0, 0))],
        out_specs=(pl.BlockSpec((1, ML_HEADS, cpt, ML_HD, ML_CHUNK), lambda b, i: (b, 0, i, 0, 0)),
                   pl.BlockSpec((1, ML_HEADS, TM, ML_HD), lambda b, i: (b, 0, i, 0)),
                   pl.BlockSpec((1, ML_HEADS, cpt, VT_ROWS, ML_CHUNK), lambda b, i: (b, 0, i, 0, 0))),
        out_shape=(jax.ShapeDtypeStruct((b, ML_HEADS, nc, ML_HD, ML_CHUNK), BF16),
                   jax.ShapeDtypeStruct((b, ML_HEADS, s, ML_HD), BF16),
                   jax.ShapeDtypeStruct((b, ML_HEADS, nc, VT_ROWS, ML_CHUNK), BF16)),
        scratch_shapes=[pltpu.VMEM((TM + 2 * HALO, 2 * MIX_W), F32)],
        compiler_params=_cparams(2),
    )(cqk, cqk, cqk, cv, conv_w)


def _gate_kernel(g_ref, fb_ref, a_ref, m_ref, iw_ref, en_ref, ws_ref, dec_ref):
    x = g_ref[...]
    n_pairs = x.shape[0]
    n_ch = 2 * ML_HEADS
    lane = lax.broadcasted_iota(jnp.int32, (1, 1, LANES), 2)
    t_in = lane % ML_CHUNK
    second = lane >= ML_CHUNK
    fwd_row = lax.broadcasted_iota(jnp.int32, (1, n_ch, 1), 1) < ML_HEADS
    li = x[:, 0:n_ch, :]
    z = x[:, n_ch:2 * n_ch, :] + fb_ref[...]
    lf = jnp.minimum(z, 0.0) - jnp.log1p(jnp.exp(-jnp.abs(z)))

    def within_chunk(v, op, ident, prefix):
        s = 1
        while s < ML_CHUNK:
            nb = pltpu.roll(v, s if prefix else LANES - s, 2)
            ok = (t_in >= s) if prefix else (t_in < ML_CHUNK - s)
            v = op(v, jnp.where(ok, nb, ident))
            s *= 2
        return v

    pre = within_chunk(lf, jnp.add, 0.0, True)
    suf = within_chunk(lf, jnp.add, 0.0, False)
    g = pre + suf - lf
    b = jnp.where(fwd_row, pre, suf)
    a = li - b
    cm_pre = within_chunk(a, jnp.maximum, -jnp.inf, True)
    cm_suf = within_chunk(a, jnp.maximum, -jnp.inf, False)
    cm = jnp.where(fwd_row, cm_pre, cm_suf)
    amax = jnp.maximum(cm_pre, cm_suf)

    def shift_pairs(v, k, fill):
        pad = jnp.full((abs(k),) + v.shape[1:], fill, F32)
        return (jnp.concatenate([pad, v[:n_pairs - k]], axis=0) if k > 0
                else jnp.concatenate([v[-k:], pad], axis=0))

    def from_chunk(v, dist, fill, forward):
        if dist == 1:
            y = pltpu.roll(v, ML_CHUNK, 2)
            if forward:
                return jnp.where(second, y, shift_pairs(y, 1, fill))
            return jnp.where(second, shift_pairs(y, -1, fill), y)
        return shift_pairs(v, dist // 2 if forward else -(dist // 2), fill)

    def running_stabiliser(forward):
        big_g, big_a = g, amax + g
        dist = 1
        while dist < 2 * n_pairs:
            gp = from_chunk(big_g, dist, 0.0, forward)
            ap = from_chunk(big_a, dist, -jnp.inf, forward)
            big_g, big_a = gp + big_g, jnp.maximum(ap + big_g, big_a)
            dist *= 2
        ge = from_chunk(big_g, 1, 0.0, forward)
        ae = from_chunk(big_a, 1, -jnp.inf, forward)
        return jnp.maximum(ge, ae)

    m_chunk = jnp.where(fwd_row, running_stabiliser(True), running_stabiliser(False))
    m_t = jnp.maximum(cm, m_chunk)
    m_last = jnp.maximum(amax, m_chunk)
    a_ref[0] = a
    m_ref[0] = m_t
    iw_ref[0] = jnp.exp(m_chunk - m_t)
    en_ref[0] = jnp.exp(-(b + m_t))
    ws_ref[0] = jnp.exp(a - m_last)
    dec_ref[0] = jnp.exp(m_chunk - m_last)


def _gate_call(gates_t, fbias_col, batch):
    n_pairs = gates_t.shape[0] // batch
    n_ch = 2 * ML_HEADS
    out = pl.BlockSpec((1, n_pairs, n_ch, LANES), lambda i: (i, 0, 0, 0))
    return pl.pallas_call(
        _gate_kernel, name="mlstm_gates",
        grid=(batch,),
        in_specs=[pl.BlockSpec((n_pairs, 2 * n_ch, LANES), lambda i: (i, 0, 0)),
                  pl.BlockSpec((n_ch, 1), lambda i: (0, 0))],
        out_specs=(out,) * 6,
        out_shape=(jax.ShapeDtypeStruct((batch, n_pairs, n_ch, LANES), F32),) * 6,
        compiler_params=_cparams(1),
    )(gates_t, fbias_col)


TE = 1024


def _mlstm_kernel(*refs):
    fwd, bwd, (hf_ref, hb_ref, state) = refs[:9], refs[9:18], refs[18:]
    i = pl.program_id(1)

    @pl.when(i == 0)
    def _():
        state[...] = jnp.zeros(state.shape, F32)

    n_pairs = TE // LANES
    s_idx = lax.broadcasted_iota(jnp.int32, (ML_CHUNK, ML_CHUNK), 0)
    t_idx = lax.broadcasted_iota(jnp.int32, (ML_CHUNK, ML_CHUNK), 1)

    def chunk_step(jobs):
        for j in jobs:
            j["cst"] = state[j["ch"]]
            j["st"] = jnp.dot(j["k"], j["qt"], preferred_element_type=F32)
            j["inter"] = jnp.dot(j["cst"].astype(BF16), j["qt"], preferred_element_type=F32)
            j["upd"] = jnp.dot((j["vt"].astype(F32) * j["ws"]).astype(BF16), j["k"],
                               preferred_element_type=F32)
        for j in jobs:
            j["swt"] = j["st"] * jnp.exp(jnp.where(j["tri"], j["a"] - j["m"], NEG_BIG))
            j["intra"] = jnp.dot(j["vt"], j["swt"].astype(BF16), preferred_element_type=F32)
        for j in jobs:
            den = jnp.sum(j["swt"], axis=0, keepdims=True) + j["iw"] * j["inter"][ML_HD:ML_HD + 1]
            tot = j["intra"][:ML_HD] + j["iw"] * j["inter"][:ML_HD]
            out_r, cc, h = j["out"]
            out_r[0, cc, h * ML_HD:(h + 1) * ML_HD, :] = tot / jnp.maximum(jnp.abs(den), j["en"])
            state[j["ch"]] = j["dec"] * j["cst"] + j["upd"]

    def pair_body(p, carry):
        tiles = []
        for d, (qt_r, k_r, vt_r, a_r, m_r, iw_r, en_r, ws_r, dec_r) in ((0, fwd), (1, bwd)):
            pp = p if d == 0 else n_pairs - 1 - p
            a_t = jnp.transpose(jnp.concatenate(
                [a_r[0, pp], jnp.zeros((LANES - 2 * ML_HEADS, LANES), F32)], axis=0))
            tiles.append((pp, a_t, m_r[0, pp], iw_r[0, pp], en_r[0, pp], ws_r[0, pp], dec_r[0, pp]))
        for step in range(2):
            jobs = []
            for d, (refs_d, out_r) in enumerate(((fwd, hf_ref), (bwd, hb_ref))):
                qt_r, k_r, vt_r = refs_d[:3]
                pp, a_t, m_t, iw_t, en_t, ws_t, dec_t = tiles[d]
                half = step if d == 0 else 1 - step
                cc = 2 * pp + half
                hs = slice(half * ML_CHUNK, (half + 1) * ML_CHUNK)
                rows = pl.ds(pl.multiple_of(cc * ML_CHUNK, ML_CHUNK), ML_CHUNK)
                for h in range(ML_HEADS):
                    ch = d * ML_HEADS + h
                    jobs.append(dict(
                        ch=ch, tri=(s_idx >= t_idx) if d else (s_idx <= t_idx),
                        k=k_r[0, h, rows, :], qt=qt_r[0, h, cc], vt=vt_r[0, h, cc], a=a_t[hs, ch:ch + 1],
                        m=m_t[ch:ch + 1, hs], iw=iw_t[ch:ch + 1, hs], en=en_t[ch:ch + 1, hs],
                        ws=ws_t[ch:ch + 1, hs], dec=dec_t[ch:ch + 1, hs], out=(out_r, cc, h)))
            chunk_step(jobs)
        return carry

    lax.fori_loop(0, n_pairs, pair_body, 0)


def _mlstm_call(qt, k, vt, a_t, m_t, iw_t, en_t, ws_t, dec_t):
    b, _, s, _ = k.shape
    nt = s // TE
    nc = s // ML_CHUNK
    cpt = TE // ML_CHUNK

    def specs(rev):
        ti = (lambda i: nt - 1 - i) if rev else (lambda i: i)
        tile = pl.BlockSpec((1, TE // LANES, 2 * ML_HEADS, LANES), lambda b, i: (b, ti(i), 0, 0))
        return [
            pl.BlockSpec((1, ML_HEADS, cpt, ML_HD, ML_CHUNK), lambda b, i: (b, 0, ti(i), 0, 0)),
            pl.BlockSpec((1, ML_HEADS, TE, ML_HD), lambda b, i: (b, 0, ti(i), 0)),
            pl.BlockSpec((1, ML_HEADS, cpt, VT_ROWS, ML_CHUNK), lambda b, i: (b, 0, ti(i), 0, 0)),
            tile, tile, tile, tile, tile, tile]

    args = [qt, k, vt, a_t, m_t, iw_t, en_t, ws_t, dec_t]
    out_f = pl.BlockSpec((1, cpt, MIX_W, ML_CHUNK), lambda b, i: (b, i, 0, 0))
    out_b = pl.BlockSpec((1, cpt, MIX_W, ML_CHUNK), lambda b, i: (b, nt - 1 - i, 0, 0))
    return pl.pallas_call(
        _mlstm_kernel, name="mlstm_scan",
        grid=(b, nt),
        in_specs=specs(False) + specs(True),
        out_specs=(out_f, out_b),
        out_shape=(jax.ShapeDtypeStruct((b, nc, MIX_W, ML_CHUNK), F32),) * 2,
        scratch_shapes=[pltpu.VMEM((2 * ML_HEADS, VT_ROWS, ML_HD), F32)],
        compiler_params=_cparams(2),
    )(*args, *args)


def _mlstm_branch(cqk, cv, gates_t, conv_w, fbias):
    qt, k, vt = _mlprep_call(cqk, cv, conv_w)
    factors = _gate_call(gates_t, fbias.reshape(2 * ML_HEADS, 1), cqk.shape[0])
    return _mlstm_call(qt, k, vt, *factors)


TF = 512


def _merge_kernel(x_ref, ya_ref, o1_ref, o2_ref, o3_ref, l1_ref, l2_ref, l3_ref, hf_ref, hb_ref,
                  co_ref, yd_ref, wg_ref, bg_ref, wbr_ref, wout_ref, mng_ref, lng_ref, lnb_ref,
                  wr_ref, x1_ref, x1p_ref, aff_ref, o_scr, l_scr, *, alpha):
    x = x_ref[0]
    xb = x.astype(BF16)

    def natural_order(src_ref, scr):
        dil, width = src_ref.shape[1], src_ref.shape[3]
        if dil == 1:
            return src_ref[0, 0]
        for r in range(dil):
            for c in range(width // LANES):
                scr[c, pl.ds(r, TF // dil, stride=dil), :] = src_ref[0, r, :, c * LANES:(c + 1) * LANES]
        return jnp.concatenate([scr[c] for c in range(width // LANES)], axis=1)

    lane_head = lax.broadcasted_iota(jnp.int32, (TF, MIX_W), 1) // ML_HD
    l1, l2, l3 = [natural_order(r, l_scr.at[p]) for p, r in enumerate((l1_ref, l2_ref, l3_ref))]
    o1, o2, o3 = [natural_order(r, o_scr.at[p]) for p, r in enumerate((o1_ref, o2_ref, o3_ref))]
    lm = jnp.maximum(jnp.maximum(l1, l2), l3)
    e1, e2, e3 = jnp.exp(l1 - lm), jnp.exp(l2 - lm), jnp.exp(l3 - lm)
    inv = 1.0 / (e1 + e2 + e3)

    def per_head(w):
        out = jnp.zeros((TF, MIX_W), F32)
        for h in range(ATT_HEADS):
            out = jnp.where(lane_head == h, w[:, h:h + 1], out)
        return out

    y_b = per_head(e1 * inv) * o1 + per_head(e2 * inv) * o2 + per_head(e3 * inv) * o3
    hsum_t = jnp.concatenate([hf_ref[0, c] + hb_ref[0, c] for c in range(TF // ML_CHUNK)], axis=1)
    per_head_rows = hsum_t.reshape(ML_HEADS, ML_HD, TF)
    mu = jnp.mean(per_head_rows, axis=1, keepdims=True)
    cen = per_head_rows - mu
    var = jnp.mean(cen * cen, axis=1, keepdims=True)
    hn_t = (cen * lax.rsqrt(var + LN_EPS)).reshape(MIX_W, TF)
    y_c_t = (jax.nn.sigmoid(jnp.transpose(co_ref[0])) * (hn_t * mng_ref[...])).astype(BF16)
    ys = (ya_ref[0], y_b.astype(BF16), None, yd_ref[0])
    merged = jnp.zeros((TF, D_MODEL), F32)
    for n in range(N_BRANCH):
        cols = slice(n * D_MODEL, (n + 1) * D_MODEL)
        gate = jax.nn.sigmoid(jnp.dot(xb, wg_ref[:, cols], preferred_element_type=F32) + bg_ref[:, cols])
        if ys[n] is None:
            proj = lax.dot_general(y_c_t, wbr_ref[n], (((0,), (0,)), ((), ())), preferred_element_type=F32)
        else:
            proj = jnp.dot(ys[n], wbr_ref[n], preferred_element_type=F32)
        merged = merged + gate * proj
    mix = jnp.dot(merged.astype(BF16), wout_ref[...], preferred_element_type=F32)
    x1 = _standardize(alpha * x + mix) * lng_ref[...] + lnb_ref[...]
    x1_ref[0] = x1
    x1b = x1.astype(BF16)
    x1p_ref[0] = _pack_bf16_pair(x1b[:, :D_MODEL // 2], x1b[:, D_MODEL // 2:])
    logits = lax.dot_general(wr_ref[...], x1b, (((1,), (1,)), ((), ())),
                             preferred_element_type=F32)
    ex = jnp.exp(logits - jnp.max(logits, axis=0, keepdims=True))
    aff_ref[0] = ex / jnp.sum(ex, axis=0, keepdims=True)


def _merge_call(x, ya, o_list, l_list, hf, hb, co, yd, wg, bg, wbr, wout, mng, lng, lnb, wr_t, alpha):
    b, s, _ = x.shape
    tok = lambda w: pl.BlockSpec((1, TF, w), lambda b, i: (b, i, 0))
    grouped = lambda dil, w: pl.BlockSpec((1, dil, TF // dil, w), lambda b, i: (b, 0, i, 0))
    chunked = pl.BlockSpec((1, TF // ML_CHUNK, MIX_W, ML_CHUNK), lambda b, i: (b, i, 0, 0))
    const = lambda shp: pl.BlockSpec(shp, lambda b, i: (0,) * len(shp))
    return pl.pallas_call(
        functools.partial(_merge_kernel, alpha=alpha), name="merge_ln_router",
        grid=(b, s // TF),
        in_specs=[tok(D_MODEL), tok(MIX_W)] + [grouped(dil, MIX_W) for _, dil in DIL_PATTERNS]
                 + [grouped(dil, LANES) for _, dil in DIL_PATTERNS]
                 + [chunked, chunked, tok(MIX_W), tok(MIX_W)]
                 + [const((D_MODEL, N_BRANCH * D_MODEL)), const((1, N_BRANCH * D_MODEL)),
                    const((N_BRANCH, MIX_W, D_MODEL)), const((D_MODEL, D_MODEL)), const((MIX_W, 1)),
                    const((1, D_MODEL)), const((1, D_MODEL)), const((N_EXPERTS, D_MODEL))],
        out_specs=(tok(D_MODEL), tok(D_MODEL // 2), pl.BlockSpec((1, N_EXPERTS, TF), lambda b, i: (b, 0, i))),
        out_shape=(jax.ShapeDtypeStruct((b, s, D_MODEL), F32),
                   jax.ShapeDtypeStruct((b, s, D_MODEL // 2), jnp.int32),
                   jax.ShapeDtypeStruct((b, N_EXPERTS, s), F32)),
        scratch_shapes=[pltpu.VMEM((len(DIL_PATTERNS), MIX_W // LANES, TF, LANES), F32),
                        pltpu.VMEM((len(DIL_PATTERNS), 1, TF, LANES), F32)],
        compiler_params=_cparams(2),
    )(x, ya, *o_list, *l_list, hf, hb, co, yd, wg, bg, wbr, wout, mng, lng, lnb, wr_t)


TT = 256


def _select_kernel(aff_ref, slot_ref, *, cap):
    s = aff_ref.shape[2]
    bits = pltpu.bitcast(aff_ref[0], jnp.int32)

    def bit_step(i, thr):
        cand = thr | jnp.left_shift(jnp.int32(1), 30 - i)
        cnt = jnp.sum((bits >= cand).astype(jnp.int32), axis=1, keepdims=True)
        return jnp.where(cnt >= cap, cand, thr)

    thr = lax.fori_loop(0, 31, bit_step, jnp.zeros((N_EXPERTS, 1), jnp.int32))
    gt = bits > thr
    eq = bits == thr
    need = (cap - jnp.sum(gt.astype(jnp.int32), axis=1, keepdims=True)).astype(F32)
    upper = (lax.broadcasted_iota(jnp.int32, (TT, TT), 0)
             <= lax.broadcasted_iota(jnp.int32, (TT, TT), 1)).astype(BF16)
    eq_before = jnp.zeros((N_EXPERTS, 1), F32)
    sel_before = jnp.zeros((N_EXPERTS, 1), F32)
    for j in range(s // TT):
        cols = slice(j * TT, (j + 1) * TT)
        eq_j = eq[:, cols]
        eq_incl = eq_before + jnp.dot(eq_j.astype(BF16), upper, preferred_element_type=F32)
        sel_j = gt[:, cols] | (eq_j & (eq_incl <= need))
        sel_f = sel_j.astype(F32)
        sel_incl = sel_before + jnp.dot(sel_f.astype(BF16), upper, preferred_element_type=F32)
        slot_ref[0, :, cols] = jnp.where(sel_j, sel_incl - 1.0, -1.0).astype(jnp.int32)
        eq_before = eq_incl[:, TT - 1:TT]
        sel_before = sel_incl[:, TT - 1:TT]


def _select_call(aff_t, cap):
    b, e, s = aff_t.shape
    return pl.pallas_call(
        functools.partial(_select_kernel, cap=cap), name="expert_choice_select",
        grid=(b,),
        in_specs=[pl.BlockSpec((1, e, s), lambda i: (i, 0, 0))],
        out_specs=pl.BlockSpec((1, e, s), lambda i: (i, 0, 0)),
        out_shape=jax.ShapeDtypeStruct((b, e, s), jnp.int32),
        compiler_params=_cparams(1),
    )(aff_t)


SC_LANES = 16
SC_ROWS = 64
SC_IDX = 128
SC_SLAB = 128
SC_ZROWS = 64
CF = 1024


def _sc_dispatch_call(x_flat, slot2, aff2, seq, cap):
    n_pairs = slot2.shape[0]
    d = x_flat.shape[1]
    info = plsc.get_sparse_core_info()
    n_workers = info.num_cores * info.num_subcores
    assert n_pairs % n_workers == 0 and seq % SC_LANES == 0 and cap % (2 * SC_ROWS) == 0
    pairs_per_worker = n_pairs // n_workers
    mesh = plsc.VectorSubcoreMesh(core_axis_name="c", subcore_axis_name="s")

    @functools.partial(
        pl.kernel, mesh=mesh, name="expert_dispatch_sc",
        compiler_params=pltpu.CompilerParams(needs_layout_passes=False),
        out_type=(jax.ShapeDtypeStruct((n_pairs * cap, d), x_flat.dtype),
                  jax.ShapeDtypeStruct((n_pairs, cap), jnp.int32),
                  jax.ShapeDtypeStruct((n_pairs, cap), F32)),
        scratch_types=[pltpu.VMEM((seq,), jnp.int32), pltpu.VMEM((seq,), F32),
                       pltpu.VMEM((cap,), jnp.int32), pltpu.VMEM((cap,), jnp.int32),
                       pltpu.VMEM((cap,), F32),
                       pltpu.VMEM((2, SC_ROWS, d), x_flat.dtype), pltpu.SemaphoreType.DMA((2,))])
    def dispatch(x_hbm, slot_hbm, aff_hbm, xs_hbm, tok_hbm, gate_hbm,
                 slot_v, aff_v, idx_v, tok_v, gate_v, rows_v, sem):
        worker = lax.axis_index("s") * info.num_cores + lax.axis_index("c")
        lane = lax.iota(jnp.int32, SC_LANES)

        def gather(c0, buf):
            return pltpu.make_async_copy(x_hbm.at[idx_v.at[pl.ds(c0, SC_ROWS)]], rows_v.at[buf], sem.at[buf])
        for k in range(pairs_per_worker):
            pair = worker * pairs_per_worker + k
            row0 = (pair // N_EXPERTS) * seq
            pltpu.sync_copy(slot_hbm.at[pair], slot_v)
            pltpu.sync_copy(aff_hbm.at[pair], aff_v)

            @pl.loop(0, seq, step=SC_LANES)
            def _(t0):
                sv = slot_v[pl.ds(t0, SC_LANES)]
                picked = sv >= 0
                plsc.store_scatter(tok_v, [sv], t0 + lane, mask=picked)
                plsc.store_scatter(idx_v, [sv], row0 + t0 + lane, mask=picked)
                plsc.store_scatter(gate_v, [sv], aff_v[pl.ds(t0, SC_LANES)], mask=picked)

            pltpu.sync_copy(tok_v, tok_hbm.at[pair])
            pltpu.sync_copy(gate_v, gate_hbm.at[pair])

            gather(0, 0).start()

            @pl.loop(0, cap, step=2 * SC_ROWS)
            def _(c0):
                gather(c0 + SC_ROWS, 1).start()
                gather(c0, 0).wait()
                pltpu.sync_copy(rows_v.at[0], xs_hbm.at[pl.ds(pair * cap + c0, SC_ROWS)])

                @pl.when(c0 + 2 * SC_ROWS < cap)
                def _():
                    gather(c0 + 2 * SC_ROWS, 0).start()

                gather(c0 + SC_ROWS, 1).wait()
                pltpu.sync_copy(rows_v.at[1], xs_hbm.at[pl.ds(pair * cap + c0 + SC_ROWS, SC_ROWS)])

    return dispatch(x_flat, slot2, aff2)


def _expert_kernel(xs_ref, g_ref, w1_ref, w3_ref, w2_ref, ye_ref, w1_bf, w3_bf, w2_bf):
    @pl.when((pl.program_id(1) == 0) & (pl.program_id(2) == 0))
    def _():
        w1_bf[...] = w1_ref[0, 0].astype(BF16)
        w3_bf[...] = w3_ref[0, 0].astype(BF16)
        w2_bf[...] = w2_ref[0, 0].astype(BF16)

    xs = jnp.concatenate(_unpack_bf16_pair(xs_ref[0, 0]), axis=1)
    hid = (jax.nn.silu(jnp.dot(xs, w1_bf[...], preferred_element_type=F32))
           * jnp.dot(xs, w3_bf[...], preferred_element_type=F32))
    ye_ref[0, 0] = jnp.dot(hid.astype(BF16), w2_bf[...], preferred_element_type=F32) * g_ref[0, 0]


def _expert_call(xs4, gate4, w1, w3, w2, layer):
    b, e, cap, half = xs4.shape
    d, ff = w1.shape[2], w1.shape[3]
    assert d == 2 * half
    rows = lambda w: pl.BlockSpec((1, 1, CF, w), lambda e, b, j: (b, e, j, 0))
    wspec = lambda r, c: pl.BlockSpec((1, 1, r, c), lambda e, b, j: (layer, e, 0, 0))
    return pl.pallas_call(
        _expert_kernel, name="expert_ffn",
        grid=(e, b, cap // CF),
        in_specs=[rows(half), rows(1), wspec(d, ff), wspec(d, ff), wspec(ff, d)],
        out_specs=rows(d),
        out_shape=jax.ShapeDtypeStruct((b, e, cap, d), F32),
        scratch_shapes=[pltpu.VMEM((d, ff), BF16), pltpu.VMEM((d, ff), BF16), pltpu.VMEM((ff, d), BF16)],
        compiler_params=_cparams(3),
    )(xs4, gate4, w1, w3, w2)


def _sc_combine_call(ye_flat, tok3, seq):
    n_pairs, n_chunks, _ = tok3.shape
    cap = n_chunks * SC_IDX
    d = ye_flat.shape[1]
    nb = n_pairs // N_EXPERTS
    info = plsc.get_sparse_core_info()
    assert info.num_subcores == N_EXPERTS and nb % info.num_cores == 0 and n_chunks % 2 == 0
    assert seq % (info.num_subcores * SC_ZROWS) == 0 and d % SC_SLAB == 0
    batches_per_core = nb // info.num_cores
    own_rows = seq // info.num_subcores
    mesh = plsc.VectorSubcoreMesh(core_axis_name="c", subcore_axis_name="s")

    @functools.partial(
        pl.kernel, mesh=mesh, name="expert_combine_sc",
        compiler_params=pltpu.CompilerParams(needs_layout_passes=False),
        out_type=jax.ShapeDtypeStruct((nb * seq, d), F32),
        scratch_types=[pltpu.VMEM_SHARED((seq, SC_SLAB), F32),
                       pltpu.VMEM((n_chunks, SC_IDX), jnp.int32),
                       pltpu.VMEM((2, SC_IDX, SC_SLAB), F32),
                       pltpu.VMEM((SC_ZROWS, SC_SLAB), F32),
                       pltpu.SemaphoreType.DMA((2,))])
    def combine(ye_hbm, tok_hbm, out_hbm, acc_sh, tok_v, rows_v, zero_v, sem):
        core = lax.axis_index("c")
        sub = lax.axis_index("s")

        @pl.loop(0, SC_ZROWS)
        def _(r):
            for l0 in range(0, SC_SLAB, SC_LANES):
                zero_v[r, pl.ds(l0, SC_LANES)] = jnp.zeros((SC_LANES,), F32)

        for bb in range(batches_per_core):
            batch = core * batches_per_core + bb
            pair = batch * N_EXPERTS + sub
            pltpu.sync_copy(tok_hbm.at[pair], tok_v)

            @pl.loop(0, d // SC_SLAB)
            def _(slab):
                cols = pl.ds(pl.multiple_of(slab * SC_SLAB, SC_SLAB), SC_SLAB)

                @pl.loop(0, own_rows, step=SC_ZROWS)
                def _(r0):
                    pltpu.sync_copy(zero_v, acc_sh.at[pl.ds(sub * own_rows + r0, SC_ZROWS)])

                def load(j, buf):
                    return pltpu.make_async_copy(
                        ye_hbm.at[pl.ds(pair * cap + j * SC_IDX, SC_IDX), cols], rows_v.at[buf], sem.at[buf])

                load(0, 0).start()
                plsc.subcore_barrier()

                for j in range(0, n_chunks, 2):
                    load(j + 1, 1).start()
                    load(j, 0).wait()
                    pltpu.sync_copy(rows_v.at[0], acc_sh.at[tok_v.at[j]], add=True)
                    if j + 2 < n_chunks:
                        load(j + 2, 0).start()
                    load(j + 1, 1).wait()
                    pltpu.sync_copy(rows_v.at[1], acc_sh.at[tok_v.at[j + 1]], add=True)

                plsc.subcore_barrier()
                pltpu.sync_copy(acc_sh.at[pl.ds(sub * own_rows, own_rows)],
                                out_hbm.at[pl.ds(batch * seq + sub * own_rows, own_rows), cols])

    return combine(ye_flat, tok3)


TN = 512


def _resln_kernel(x_ref, y_ref, g_ref, b_ref, o_ref, *, alpha):
    o_ref[...] = _standardize(alpha * x_ref[...] + y_ref[...]) * g_ref[...] + b_ref[...]


def _resln_call(x2d, y2d, g, bta, alpha):
    n, d = x2d.shape
    tok = pl.BlockSpec((TN, d), lambda i: (i, 0))
    vec = pl.BlockSpec((1, d), lambda i: (0, 0))
    return pl.pallas_call(
        functools.partial(_resln_kernel, alpha=alpha), name="residual_layernorm",
        grid=(n // TN,), in_specs=[tok, tok, vec, vec], out_specs=tok,
        out_shape=jax.ShapeDtypeStruct((n, d), F32),
        compiler_params=_cparams(1),
    )(x2d, y2d, g, bta)


def _expert_choice_ffn(x1p, aff_t, w1, w3, w2, layer):
    b, s, half = x1p.shape
    d = 2 * half
    cap = EC_FACTOR * s // N_EXPERTS
    slot = _select_call(aff_t, cap)
    xs, tok, gate = _sc_dispatch_call(x1p.reshape(b * s, half), slot.reshape(b * N_EXPERTS, s),
                                      aff_t.reshape(b * N_EXPERTS, s), s, cap)
    ye = _expert_call(xs.reshape(b, N_EXPERTS, cap, half), gate.reshape(b, N_EXPERTS, cap, 1), w1, w3, w2,
                      layer)
    out = _sc_combine_call(ye.reshape(b * N_EXPERTS * cap, d),
                           tok.reshape(b * N_EXPERTS, cap // SC_IDX, SC_IDX), s)
    return out.reshape(b, s, d)


def _pack_pool(pool_w):
    g, gd, _ = pool_w.shape
    out = jnp.zeros((g * gd, g * gd), F32)
    for i in range(g):
        out = out.at[i * gd:(i + 1) * gd, i * gd:(i + 1) * gd].set(pool_w[i])
    return out.astype(BF16)


def _layer(layer, x, pending, alpha, bias_tiles, w_in, b_in, gm_ln_g, gm_ws, gm_bs, ml_conv, ml_fbias,
           ml_norm_g, pool_w, pool_scale, w_branch, w_out, ln1_g, ln1_b, w_router, w_e1, w_e3, w_e2):
    b, s, d = x.shape
    n_small = 2576
    w_cat, b_cat = _pack_inproj_weights(w_in, b_in)
    wscat, bsfull = _pack_gmlp(gm_ws, gm_bs)
    if pending is None:
        outs = _inproj_call(x.reshape(b * s, d), w_cat, b_cat, gm_ln_g[None], wscat, bsfull, b)
    else:
        *outs, x2 = _inproj_call(pending, w_cat, b_cat, gm_ln_g[None], wscat, bsfull, b, alpha)
        x = x2.reshape(b, s, d)
    ya, qkv1, qkv4, qkv16, cqk, cv, co, dx, gates_t = outs
    r3 = lambda t: t.reshape(b, s, t.shape[-1])
    o_list, l_list = [], []
    for qkv, bias in zip((qkv1, qkv4, qkv16), bias_tiles):
        o, lse = _attn_call(qkv, bias)
        o_list.append(o)
        l_list.append(lse)
    hf, hb = _mlstm_branch(r3(cqk), r3(cv), gates_t, ml_conv, ml_fbias)
    yd = _pool_call(r3(dx), _pack_pool(pool_w), pool_scale[None])
    x1, x1p, aff_t = _merge_call(
        x, r3(ya), o_list, l_list, hf, hb, r3(co), yd,
        w_in[:, n_small:].astype(BF16), b_in[None, n_small:], w_branch.astype(BF16), w_out.astype(BF16),
        ml_norm_g[:, None], ln1_g[None], ln1_b[None], jnp.transpose(w_router).astype(BF16), alpha)
    ffn = _expert_choice_ffn(x1p, aff_t, w_e1, w_e3, w_e2, layer)
    return x1.reshape(b * s, d), ffn.reshape(b * s, d)


def kernel(x, w_in, b_in, gm_ln_g, gm_ws, gm_bs, rel_bias, ml_conv, ml_fbias, ml_norm_g, pool_w,
           pool_scale, w_branch, w_out, ln1_g, ln1_b, w_router, w_e1, w_e3, w_e2, ln2_g, ln2_b):
    depth = w_in.shape[0]
    alpha = (2 * depth) ** 0.25
    bias_tiles = [_attn_bias_tile(rel_bias, window, dil) for window, dil in DIL_PATTERNS]
    b, s, d = x.shape
    pending = None
    for l in range(depth):
        x1, ffn = _layer(l, x, pending, alpha, bias_tiles, w_in[l], b_in[l], gm_ln_g[l], gm_ws[l], gm_bs[l],
                         ml_conv[l], ml_fbias[l], ml_norm_g[l], pool_w[l], pool_scale[l], w_branch[l],
                         w_out[l], ln1_g[l], ln1_b[l], w_router[l], w_e1, w_e3, w_e2)
        pending = (x1, ffn, ln2_g[l][None], ln2_b[l][None])
    return _resln_call(*pending, alpha).reshape(b, s, d)
```

```python
import functools
import math

import jax
import jax.numpy as jnp
import numpy as np
from jax import lax
from jax.experimental import pallas as pl
from jax.experimental.pallas import tpu as pltpu
from jax.experimental.pallas import tpu_sc as plsc

F32 = jnp.float32
BF16 = jnp.bfloat16

D_MODEL = 1024
MIX_W = 256
N_BRANCH = 4
GM_CHUNK = 128
GM_GROUPS = 4
ATT_HEADS = 4
ATT_HD = 64
DIL_PATTERNS = ((128, 1), (512, 4), (2048, 16))
ATT_BLOCK = 64
REL_BUCKETS = 32
REL_MAX_DIST = 1024
ML_HEADS = 4
ML_HD = 64
ML_CHUNK = 64
POOL_WINDOWS = (2, 4, 8, 16)
N_EXPERTS = 16
EXPERT_FF = 1024
EC_FACTOR = 2
LN_EPS = 1e-5
NEG_BIG = -1e30

V7X_VMEM_LIMIT = 56 * 1024 * 1024
LANES = 128
HALO = 8


def _cparams(n_grid, vmem=V7X_VMEM_LIMIT):
    return pltpu.CompilerParams(dimension_semantics=("arbitrary",) * n_grid,
                                vmem_limit_bytes=vmem)


def _pack_bf16_pair(lo, hi):
    lo_bits = lax.shift_right_logical(pltpu.bitcast(lo.astype(F32), jnp.int32), 16)
    return pltpu.bitcast(hi.astype(F32), jnp.int32) | lo_bits


def _unpack_bf16_pair(packed):
    lo = pltpu.bitcast(lax.shift_left(packed, 16), F32).astype(BF16)
    hi = pltpu.bitcast(packed & jnp.int32(-65536), F32).astype(BF16)
    return lo, hi


def _standardize(xf):
    mu = jnp.mean(xf, axis=-1, keepdims=True)
    var = jnp.mean(jnp.square(xf - mu), axis=-1, keepdims=True)
    return (xf - mu) * lax.rsqrt(var + LN_EPS)


TA = 512
A_COLS = 2560 + LANES


def _inproj_kernel(*refs, alpha):
    if alpha is None:
        x_ref, *refs = refs
        x = x_ref[...]
    else:
        x1_ref, y_ref, g2_ref, b2_ref, *refs = refs
        x = _standardize(alpha * x1_ref[...] + y_ref[...]) * g2_ref[...] + b2_ref[...]
        refs[-2][...] = x
        refs = refs[:-2] + refs[-1:]
    (w_ref, b_ref, lng_ref, wscat_ref, bsfull_ref,
     ya_ref, qkv1_ref, qkv4_ref, qkv16_ref, cqk_ref, cv_ref, co_ref, dx_ref, gt_ref, qkv_scr) = refs
    xb = x.astype(BF16)
    h = jnp.dot(xb, w_ref[...], preferred_element_type=F32) + b_ref[...]
    qkv1_ref[0, 0] = h[:, 512:1280].astype(BF16)
    for c in range(768 // LANES):
        qkv_scr[c] = h[:, 512 + c * LANES:512 + (c + 1) * LANES]
    for (_, dil), out_ref in zip(DIL_PATTERNS[1:], (qkv4_ref, qkv16_ref)):
        for r in range(dil):
            for c in range(768 // LANES):
                out_ref[0, r, :, c * LANES:(c + 1) * LANES] = (
                    qkv_scr[c, pl.ds(r, TA // dil, stride=dil), :].astype(BF16))
    cqk_ref[...] = h[:, 1280:1792]
    cv_ref[...] = h[:, 1792:2048].astype(BF16)
    co_ref[...] = h[:, 2048:2304]
    dx_ref[...] = h[:, 2304:2560]
    gates_t = jnp.transpose(h[:, 2560:2688])
    for j in range(TA // LANES):
        gt_ref[j] = gates_t[0:4 * ML_HEADS, j * LANES:(j + 1) * LANES]
    u = jax.nn.gelu(h[:, 0:256])
    v = jax.nn.gelu(h[:, 256:512])
    vn = _standardize(v) * lng_ref[...]
    lane_grp = lax.broadcasted_iota(jnp.int32, (GM_CHUNK, MIX_W), 1) // (MIX_W // GM_GROUPS)
    for c in range(TA // GM_CHUNK):
        vc = vn[c * GM_CHUNK:(c + 1) * GM_CHUNK]
        stacked = jnp.concatenate(
            [jnp.where(lane_grp == g, vc, 0.0).astype(BF16) for g in range(GM_GROUPS)], axis=0)
        mixed = jnp.dot(wscat_ref[...], stacked, preferred_element_type=F32) + bsfull_ref[...]
        ya_ref[c * GM_CHUNK:(c + 1) * GM_CHUNK, :] = (
            u[c * GM_CHUNK:(c + 1) * GM_CHUNK] * mixed).astype(BF16)


def _inproj_call(x_in, w_cat, b_cat, lng, wscat, bsfull, batch, alpha=None):
    fused = alpha is not None
    n = (x_in[0] if fused else x_in).shape[0]
    seq = n // batch
    tpb = seq // TA
    tok = lambda w: pl.BlockSpec((TA, w), lambda i: (i, 0))
    const = lambda s: pl.BlockSpec(s, lambda i: (0,) * len(s))
    regrouped = lambda dil: pl.BlockSpec((1, dil, TA // dil, 768), lambda i: (i // tpb, 0, i % tpb, 0))
    out_shape = (
        jax.ShapeDtypeStruct((n, 256), BF16),
    ) + tuple(jax.ShapeDtypeStruct((batch, dil, seq // dil, 768), BF16)
              for _, dil in DIL_PATTERNS) + (
        jax.ShapeDtypeStruct((n, 512), F32),
        jax.ShapeDtypeStruct((n, 256), BF16),
        jax.ShapeDtypeStruct((n, 256), F32),
        jax.ShapeDtypeStruct((n, 256), F32),
        jax.ShapeDtypeStruct((n // LANES, 4 * ML_HEADS, LANES), F32),
    )
    x_specs = [tok(D_MODEL), tok(D_MODEL), const((1, D_MODEL)), const((1, D_MODEL))] if fused else [tok(D_MODEL)]
    out_specs = ((tok(256),) + tuple(regrouped(dil) for _, dil in DIL_PATTERNS)
                 + (tok(512), tok(256), tok(256), tok(256),
                    pl.BlockSpec((TA // LANES, 4 * ML_HEADS, LANES), lambda i: (i, 0, 0))))
    if fused:
        out_specs += (tok(D_MODEL),)
        out_shape += (jax.ShapeDtypeStruct((n, D_MODEL), F32),)
    return pl.pallas_call(
        functools.partial(_inproj_kernel, alpha=alpha), name="inproj_gmlp",
        grid=(n // TA,),
        in_specs=x_specs + [const((D_MODEL, A_COLS)), const((1, A_COLS)), const((1, MIX_W)),
                            const((GM_CHUNK, GM_GROUPS * GM_CHUNK)), const((GM_CHUNK, MIX_W))],
        out_specs=out_specs,
        out_shape=out_shape,
        scratch_shapes=[pltpu.VMEM((768 // LANES, TA, LANES), F32)],
        compiler_params=_cparams(1),
    )(*(x_in if fused else (x_in,)), w_cat, b_cat, lng, wscat, bsfull)


def _pack_inproj_weights(w_in, b_in):
    pad = lambda a: jnp.pad(a, ((0, 0), (0, LANES - 4 * ML_HEADS)))
    w_cat = jnp.concatenate([w_in[:, 0:2304], w_in[:, 2320:2576], pad(w_in[:, 2304:2320])], axis=1)
    b2 = b_in[None, :]
    b_cat = jnp.concatenate([b2[:, 0:2304], b2[:, 2320:2576], pad(b2[:, 2304:2320])], axis=1)
    return w_cat.astype(BF16), b_cat


def _pack_gmlp(gm_ws, gm_bs):
    wscat = jnp.transpose(gm_ws, (1, 0, 2)).reshape(GM_CHUNK, GM_GROUPS * GM_CHUNK).astype(BF16)
    bsfull = jnp.repeat(jnp.transpose(gm_bs), MIX_W // GM_GROUPS, axis=1)
    return wscat, bsfull


def _halo_specs(t, width, n_tiles):
    r = t // HALO
    main = pl.BlockSpec((1, t, width), lambda b, i: (b, i, 0))
    prev = pl.BlockSpec((1, HALO, width), lambda b, i: (b, jnp.maximum(i * r - 1, 0), 0))
    nxt = pl.BlockSpec((1, HALO, width), lambda b, i: (b, jnp.minimum((i + 1) * r, n_tiles * r - 1), 0))
    return main, prev, nxt


def _fill_halo_scratch(buf, x_ref, p_ref, n_ref, t):
    i = pl.program_id(1)
    last = pl.num_programs(1) - 1
    buf[0:HALO, :] = jnp.where(i > 0, p_ref[0], 0.0)
    buf[HALO:HALO + t, :] = x_ref[0]
    buf[HALO + t:2 * HALO + t, :] = jnp.where(i < last, n_ref[0], 0.0)


TP = 512


def _pool_kernel(x_ref, p_ref, n_ref, w_ref, sc_ref, o_ref, buf):
    _fill_halo_scratch(buf, x_ref, p_ref, n_ref, TP)
    seq = pl.num_programs(1) * TP
    pos = pl.program_id(1) * TP + lax.broadcasted_iota(jnp.int32, (TP, 1), 0)
    lane_grp = lax.broadcasted_iota(jnp.int32, (TP, MIX_W), 1) // (MIX_W // len(POOL_WINDOWS))
    x0 = buf[HALO:HALO + TP, :]
    pooled = jnp.zeros((TP, MIX_W), F32)
    acc = None
    half_done = 0
    for gi, win in enumerate(POOL_WINDOWS):
        half = win // 2
        for o in list(range(-half, -half_done)) + list(range(half_done, half)):
            term = buf[HALO + o:HALO + o + TP, :]
            acc = term if acc is None else acc + term
        half_done = half
        cnt = (jnp.minimum(pos + half, seq) - jnp.maximum(pos - half, 0)).astype(F32)
        pooled = jnp.where(lane_grp == gi, acc / cnt - x0, pooled)
    mixed = jnp.dot(pooled.astype(BF16), w_ref[...], preferred_element_type=F32)
    o_ref[0] = (mixed * sc_ref[...]).astype(BF16)


def _pool_call(dx, w_block, scale):
    b, s, _ = dx.shape
    nt = s // TP
    main, prev, nxt = _halo_specs(TP, MIX_W, nt)
    return pl.pallas_call(
        _pool_kernel, name="pool_mixer",
        grid=(b, nt),
        in_specs=[main, prev, nxt,
                  pl.BlockSpec((MIX_W, MIX_W), lambda b, i: (0, 0)),
                  pl.BlockSpec((1, MIX_W), lambda b, i: (0, 0))],
        out_specs=pl.BlockSpec((1, TP, MIX_W), lambda b, i: (b, i, 0)),
        out_shape=jax.ShapeDtypeStruct((b, s, MIX_W), BF16),
        scratch_shapes=[pltpu.VMEM((TP + 2 * HALO, MIX_W), F32)],
        compiler_params=_cparams(2),
    )(dx, dx, dx, w_block, scale)


TQ = 128
TQS = 512
TKEYS = TQ + 2 * ATT_BLOCK


def _attn_kernel(q_ref, kp_ref, km_ref, kn_ref, vp_ref, vm_ref, vn_ref, bias_ref, o_ref, lse_ref):
    i = pl.program_id(2)
    seq = pl.num_programs(2) * TQS
    q = q_ref[0, 0] * ATT_HD ** -0.5
    k = jnp.concatenate([kp_ref[0, 0], km_ref[0, 0], kn_ref[0, 0]], axis=0)
    v = jnp.concatenate([vp_ref[0, 0], vm_ref[0, 0], vn_ref[0, 0]], axis=0)
    lane = lax.broadcasted_iota(jnp.int32, (TQ, LANES), 1)
    lane_half = lax.broadcasted_iota(jnp.int32, (1, LANES), 1) // ATT_HD
    keep = [jnp.where(lane_half == hh, 1.0, 0.0).astype(BF16) for hh in range(2)]
    for j in range(TQS // TQ):
        kpos = i * TQS + j * TQ - ATT_BLOCK + lax.broadcasted_iota(jnp.int32, (1, TKEYS), 1)
        kvalid = (kpos >= 0) & (kpos < seq)
        qrows = slice(j * TQ, (j + 1) * TQ)
        krows = slice(j * TQ, j * TQ + TKEYS)
        lse_tile = jnp.zeros((TQ, LANES), F32)
        for pair in range(ATT_HEADS // 2):
            grp = slice(pair * LANES, (pair + 1) * LANES)
            q_pair, k_pair, v_pair = q[qrows, grp], k[krows, grp], v[krows, grp]
            o_pair = jnp.zeros((TQ, LANES), F32)
            for hh in range(2):
                h = 2 * pair + hh
                logits = lax.dot_general(q_pair * keep[hh], k_pair, (((1,), (1,)), ((), ())),
                                         preferred_element_type=F32) + bias_ref[h]
                logits = jnp.where(kvalid, logits, NEG_BIG)
                m = jnp.max(logits, axis=-1, keepdims=True)
                p = jnp.exp(logits - m)
                ssum = jnp.sum(p, axis=-1, keepdims=True)
                o = jnp.dot(p.astype(BF16), v_pair, preferred_element_type=F32) / ssum
                o_pair = jnp.where(lane_half == hh, o, o_pair)
                lse_tile = jnp.where(lane == h, m + jnp.log(ssum), lse_tile)
            o_ref[0, 0, qrows, grp] = o_pair
        lse_ref[0, 0, qrows, :] = lse_tile


def _attn_call(qkv, bias):
    b, dil, l, _ = qkv.shape
    nt = l // TQS
    r64 = TQS // ATT_BLOCK
    main = lambda c: pl.BlockSpec((1, 1, TQS, MIX_W), lambda b, r, i: (b, r, i, c))
    prev = lambda c: pl.BlockSpec((1, 1, ATT_BLOCK, MIX_W),
                                  lambda b, r, i: (b, r, jnp.maximum(i * r64 - 1, 0), c))
    nxt = lambda c: pl.BlockSpec((1, 1, ATT_BLOCK, MIX_W),
                                 lambda b, r, i: (b, r, jnp.minimum((i + 1) * r64, nt * r64 - 1), c))
    return pl.pallas_call(
        _attn_kernel, name="band_attention",
        grid=(b, dil, nt),
        in_specs=[main(0), prev(1), main(1), nxt(1), prev(2), main(2), nxt(2),
                  pl.BlockSpec((ATT_HEADS, TQ, TKEYS), lambda b, r, i: (0, 0, 0))],
        out_specs=(pl.BlockSpec((1, 1, TQS, MIX_W), lambda b, r, i: (b, r, i, 0)),
                   pl.BlockSpec((1, 1, TQS, LANES), lambda b, r, i: (b, r, i, 0))),
        out_shape=(jax.ShapeDtypeStruct((b, dil, l, MIX_W), F32),
                   jax.ShapeDtypeStruct((b, dil, l, LANES), F32)),
        compiler_params=_cparams(3),
    )(qkv, qkv, qkv, qkv, qkv, qkv, qkv, bias)


def _t5_bucket_static(rel):
    half = REL_BUCKETS // 2
    max_exact = half // 2
    ret = np.where(rel > 0, half, 0)
    n = np.abs(rel)
    nf = np.maximum(n, 1).astype(np.float32)
    large = max_exact + (np.log(nf / np.float32(max_exact)) / np.float32(math.log(REL_MAX_DIST / max_exact))
                         * np.float32(half - max_exact)).astype(np.int32)
    large = np.minimum(large, half - 1)
    return ret + np.where(n < max_exact, n, large)


def _attn_bias_tile(rel_bias, window, dil):
    side = (window // 2) // dil
    rel = np.arange(TKEYS)[None, :] - ATT_BLOCK - np.arange(TQ)[:, None]
    onehot = jax.nn.one_hot(jnp.asarray(_t5_bucket_static(dil * rel), jnp.int32), REL_BUCKETS, dtype=F32)
    bias = jnp.einsum('qkr,rh->hqk', onehot, rel_bias, precision=lax.Precision.HIGHEST)
    return jnp.where(jnp.asarray(np.abs(rel) <= side)[None], bias, NEG_BIG)


TM = 512
VT_ROWS = ML_HD + 16


def _mlprep_kernel(x_ref, p_ref, n_ref, v_ref, w_ref, qt_out, k_out, vt_out, buf):
    _fill_halo_scratch(buf, x_ref, p_ref, n_ref, TM)
    conv = (buf[HALO - 1:HALO - 1 + TM, :] * w_ref[0:1, :] + buf[HALO:HALO + TM, :] * w_ref[1:2, :]
            + buf[HALO + 1:HALO + 1 + TM, :] * w_ref[2:3, :])
    qk = jax.nn.silu(conv)
    qt = jnp.transpose(qk[:, :MIX_W])
    vt = jnp.transpose(v_ref[0].astype(F32))
    ones_rows = jnp.where(lax.broadcasted_iota(jnp.int32, (VT_ROWS - ML_HD, ML_CHUNK), 0) == 0, 1.0, 0.0)
    for h in range(ML_HEADS):
        sl = slice(h * ML_HD, (h + 1) * ML_HD)
        k_out[0, h] = (qk[:, MIX_W + h * ML_HD:MIX_W + (h + 1) * ML_HD] * ML_HD ** -0.5).astype(BF16)
        for c in range(TM // ML_CHUNK):
            cl = slice(c * ML_CHUNK, (c + 1) * ML_CHUNK)
            qt_out[0, h, c] = qt[sl, cl].astype(BF16)
            vt_out[0, h, c] = jnp.concatenate([vt[sl, cl], ones_rows], axis=0).astype(BF16)


def _mlprep_call(cqk, cv, conv_w):
    b, s, _ = cqk.shape
    nt = s // TM
    nc = s // ML_CHUNK
    cpt = TM // ML_CHUNK
    main, prev, nxt = _halo_specs(TM, 2 * MIX_W, nt)
    return pl.pallas_call(
        _mlprep_kernel, name="mlstm_prep",
        grid=(b, nt),
        in_specs=[main, prev, nxt,
                  pl.BlockSpec((1, TM, MIX_W), lambda b, i: (b, i, 0)),
                  pl.BlockSpec((3, 2 * MIX_W), lambda b, i: (0, 0))],
        out_specs=(pl.BlockSpec((1, ML_HEADS, cpt, ML_HD, ML_CHUNK), lambda b, i: (b, 0, i, 0, 0)),
                   pl.BlockSpec((1, ML_HEADS, TM, ML_HD), lambda b, i: (b, 0, i, 0)),
                   pl.BlockSpec((1, ML_HEADS, cpt, VT_ROWS, ML_CHUNK), lambda b, i: (b, 0, i, 0, 0))),
        out_shape=(jax.ShapeDtypeStruct((b, ML_HEADS, nc, ML_HD, ML_CHUNK), BF16),
                   jax.ShapeDtypeStruct((b, ML_HEADS, s, ML_HD), BF16),
                   jax.ShapeDtypeStruct((b, ML_HEADS, nc, VT_ROWS, ML_CHUNK), BF16)),
        scratch_shapes=[pltpu.VMEM((TM + 2 * HALO, 2 * MIX_W), F32)],
        compiler_params=_cparams(2),
    )(cqk, cqk, cqk, cv, conv_w)


def _gate_kernel(g_ref, fb_ref, a_ref, m_ref, iw_ref, en_ref, ws_ref, dec_ref):
    x = g_ref[...]
    n_pairs = x.shape[0]
    n_ch = 2 * ML_HEADS
    lane = lax.broadcasted_iota(jnp.int32, (1, 1, LANES), 2)
    t_in = lane % ML_CHUNK
    second = lane >= ML_CHUNK
    fwd_row = lax.broadcasted_iota(jnp.int32, (1, n_ch, 1), 1) < ML_HEADS
    li = x[:, 0:n_ch, :]
    z = x[:, n_ch:2 * n_ch, :] + fb_ref[...]
    lf = jnp.minimum(z, 0.0) - jnp.log1p(jnp.exp(-jnp.abs(z)))

    def within_chunk(v, op, ident, prefix):
        s = 1
        while s < ML_CHUNK:
            nb = pltpu.roll(v, s if prefix else LANES - s, 2)
            ok = (t_in >= s) if prefix else (t_in < ML_CHUNK - s)
            v = op(v, jnp.where(ok, nb, ident))
            s *= 2
        return v

    pre = within_chunk(lf, jnp.add, 0.0, True)
    suf = within_chunk(lf, jnp.add, 0.0, False)
    g = pre + suf - lf
    b = jnp.where(fwd_row, pre, suf)
    a = li - b
    cm_pre = within_chunk(a, jnp.maximum, -jnp.inf, True)
    cm_suf = within_chunk(a, jnp.maximum, -jnp.inf, False)
    cm = jnp.where(fwd_row, cm_pre, cm_suf)
    amax = jnp.maximum(cm_pre, cm_suf)

    def shift_pairs(v, k, fill):
        pad = jnp.full((abs(k),) + v.shape[1:], fill, F32)
        return (jnp.concatenate([pad, v[:n_pairs - k]], axis=0) if k > 0
                else jnp.concatenate([v[-k:], pad], axis=0))

    def from_chunk(v, dist, fill, forward):
        if dist == 1:
            y = pltpu.roll(v, ML_CHUNK, 2)
            if forward:
                return jnp.where(second, y, shift_pairs(y, 1, fill))
            return jnp.where(second, shift_pairs(y, -1, fill), y)
        return shift_pairs(v, dist // 2 if forward else -(dist // 2), fill)

    def running_stabiliser(forward):
        big_g, big_a = g, amax + g
        dist = 1
        while dist < 2 * n_pairs:
            gp = from_chunk(big_g, dist, 0.0, forward)
            ap = from_chunk(big_a, dist, -jnp.inf, forward)
            big_g, big_a = gp + big_g, jnp.maximum(ap + big_g, big_a)
            dist *= 2
        ge = from_chunk(big_g, 1, 0.0, forward)
        ae = from_chunk(big_a, 1, -jnp.inf, forward)
        return jnp.maximum(ge, ae)

    m_chunk = jnp.where(fwd_row, running_stabiliser(True), running_stabiliser(False))
    m_t = jnp.maximum(cm, m_chunk)
    m_last = jnp.maximum(amax, m_chunk)
    a_ref[0] = a
    m_ref[0] = m_t
    iw_ref[0] = jnp.exp(m_chunk - m_t)
    en_ref[0] = jnp.exp(-(b + m_t))
    ws_ref[0] = jnp.exp(a - m_last)
    dec_ref[0] = jnp.exp(m_chunk - m_last)


def _gate_call(gates_t, fbias_col, batch):
    n_pairs = gates_t.shape[0] // batch
    n_ch = 2 * ML_HEADS
    out = pl.BlockSpec((1, n_pairs, n_ch, LANES), lambda i: (i, 0, 0, 0))
    return pl.pallas_call(
        _gate_kernel, name="mlstm_gates",
        grid=(batch,),
        in_specs=[pl.BlockSpec((n_pairs, 2 * n_ch, LANES), lambda i: (i, 0, 0)),
                  pl.BlockSpec((n_ch, 1), lambda i: (0, 0))],
        out_specs=(out,) * 6,
        out_shape=(jax.ShapeDtypeStruct((batch, n_pairs, n_ch, LANES), F32),) * 6,
        compiler_params=_cparams(1),
    )(gates_t, fbias_col)


TE = 1024


def _mlstm_kernel(*refs):
    fwd, bwd, (hf_ref, hb_ref, state) = refs[:9], refs[9:18], refs[18:]
    i = pl.program_id(1)

    @pl.when(i == 0)
    def _():
        state[...] = jnp.zeros(state.shape, F32)

    n_pairs = TE // LANES
    s_idx = lax.broadcasted_iota(jnp.int32, (ML_CHUNK, ML_CHUNK), 0)
    t_idx = lax.broadcasted_iota(jnp.int32, (ML_CHUNK, ML_CHUNK), 1)

    def chunk_step(jobs):
        for j in jobs:
            j["cst"] = state[j["ch"]]
            j["st"] = jnp.dot(j["k"], j["qt"], preferred_element_type=F32)
            j["inter"] = jnp.dot(j["cst"].astype(BF16), j["qt"], preferred_element_type=F32)
            j["upd"] = jnp.dot((j["vt"].astype(F32) * j["ws"]).astype(BF16), j["k"],
                               preferred_element_type=F32)
        for j in jobs:
            j["swt"] = j["st"] * jnp.exp(jnp.where(j["tri"], j["a"] - j["m"], NEG_BIG))
            j["intra"] = jnp.dot(j["vt"], j["swt"].astype(BF16), preferred_element_type=F32)
        for j in jobs:
            den = jnp.sum(j["swt"], axis=0, keepdims=True) + j["iw"] * j["inter"][ML_HD:ML_HD + 1]
            tot = j["intra"][:ML_HD] + j["iw"] * j["inter"][:ML_HD]
            out_r, cc, h = j["out"]
            out_r[0, cc, h * ML_HD:(h + 1) * ML_HD, :] = tot / jnp.maximum(jnp.abs(den), j["en"])
            state[j["ch"]] = j["dec"] * j["cst"] + j["upd"]

    def pair_body(p, carry):
        tiles = []
        for d, (qt_r, k_r, vt_r, a_r, m_r, iw_r, en_r, ws_r, dec_r) in ((0, fwd), (1, bwd)):
            pp = p if d == 0 else n_pairs - 1 - p
            a_t = jnp.transpose(jnp.concatenate(
                [a_r[0, pp], jnp.zeros((LANES - 2 * ML_HEADS, LANES), F32)], axis=0))
            rows = [r[0, pp] for r in (m_r, iw_r, en_r, ws_r, dec_r)]
            tiles.append((pp, a_t, rows, [pltpu.roll(t, ML_CHUNK, 1) for t in rows]))
        for step in range(2):
            jobs = []
            for d, (refs_d, out_r) in enumerate(((fwd, hf_ref), (bwd, hb_ref))):
                qt_r, k_r, vt_r = refs_d[:3]
                pp, a_t, first, second = tiles[d]
                half = step if d == 0 else 1 - step
                cc = 2 * pp + half
                hs = slice(half * ML_CHUNK, (half + 1) * ML_CHUNK)
                srows = pl.ds(pl.multiple_of(cc * ML_CHUNK, ML_CHUNK), ML_CHUNK)
                m_t, iw_t, en_t, ws_t, dec_t = [t[:, :ML_CHUNK] for t in (second if half else first)]
                for h in range(ML_HEADS):
                    ch = d * ML_HEADS + h
                    jobs.append(dict(
                        ch=ch, tri=(s_idx >= t_idx) if d else (s_idx <= t_idx),
                        k=k_r[0, h, srows, :], qt=qt_r[0, h, cc], vt=vt_r[0, h, cc], a=a_t[hs, ch:ch + 1],
                        m=m_t[ch:ch + 1], iw=iw_t[ch:ch + 1], en=en_t[ch:ch + 1],
                        ws=ws_t[ch:ch + 1], dec=dec_t[ch:ch + 1], out=(out_r, cc, h)))
            chunk_step(jobs)
        return carry

    lax.fori_loop(0, n_pairs, pair_body, 0)


def _mlstm_call(qt, k, vt, a_t, m_t, iw_t, en_t, ws_t, dec_t):
    b, _, s, _ = k.shape
    nt = s // TE
    nc = s // ML_CHUNK
    cpt = TE // ML_CHUNK

    def specs(rev):
        ti = (lambda i: nt - 1 - i) if rev else (lambda i: i)
        tile = pl.BlockSpec((1, TE // LANES, 2 * ML_HEADS, LANES), lambda b, i: (b, ti(i), 0, 0))
        return [
            pl.BlockSpec((1, ML_HEADS, cpt, ML_HD, ML_CHUNK), lambda b, i: (b, 0, ti(i), 0, 0)),
            pl.BlockSpec((1, ML_HEADS, TE, ML_HD), lambda b, i: (b, 0, ti(i), 0)),
            pl.BlockSpec((1, ML_HEADS, cpt, VT_ROWS, ML_CHUNK), lambda b, i: (b, 0, ti(i), 0, 0)),
            tile, tile, tile, tile, tile, tile]

    args = [qt, k, vt, a_t, m_t, iw_t, en_t, ws_t, dec_t]
    out_f = pl.BlockSpec((1, cpt, MIX_W, ML_CHUNK), lambda b, i: (b, i, 0, 0))
    out_b = pl.BlockSpec((1, cpt, MIX_W, ML_CHUNK), lambda b, i: (b, nt - 1 - i, 0, 0))
    return pl.pallas_call(
        _mlstm_kernel, name="mlstm_scan",
        grid=(b, nt),
        in_specs=specs(False) + specs(True),
        out_specs=(out_f, out_b),
        out_shape=(jax.ShapeDtypeStruct((b, nc, MIX_W, ML_CHUNK), F32),) * 2,
        scratch_shapes=[pltpu.VMEM((2 * ML_HEADS, VT_ROWS, ML_HD), F32)],
        compiler_params=_cparams(2),
    )(*args, *args)


def _mlstm_branch(cqk, cv, gates_t, conv_w, fbias):
    qt, k, vt = _mlprep_call(cqk, cv, conv_w)
    factors = _gate_call(gates_t, fbias.reshape(2 * ML_HEADS, 1), cqk.shape[0])
    return _mlstm_call(qt, k, vt, *factors)


TF = 512


def _merge_kernel(x_ref, ya_ref, o1_ref, o2_ref, o3_ref, l1_ref, l2_ref, l3_ref, hf_ref, hb_ref,
                  co_ref, yd_ref, wg_ref, bg_ref, wbr_ref, wout_ref, mng_ref, lng_ref, lnb_ref,
                  wr_ref, x1_ref, x1p_ref, aff_ref, o_scr, l_scr, *, alpha):
    x = x_ref[0]
    xb = x.astype(BF16)

    def natural_order(src_ref, scr):
        dil, width = src_ref.shape[1], src_ref.shape[3]
        if dil == 1:
            return src_ref[0, 0]
        for r in range(dil):
            for c in range(width // LANES):
                scr[c, pl.ds(r, TF // dil, stride=dil), :] = src_ref[0, r, :, c * LANES:(c + 1) * LANES]
        return jnp.concatenate([scr[c] for c in range(width // LANES)], axis=1)

    lane_head = lax.broadcasted_iota(jnp.int32, (TF, MIX_W), 1) // ML_HD
    l1, l2, l3 = [natural_order(r, l_scr.at[p]) for p, r in enumerate((l1_ref, l2_ref, l3_ref))]
    o1, o2, o3 = [natural_order(r, o_scr.at[p]) for p, r in enumerate((o1_ref, o2_ref, o3_ref))]
    lm = jnp.maximum(jnp.maximum(l1, l2), l3)
    e1, e2, e3 = jnp.exp(l1 - lm), jnp.exp(l2 - lm), jnp.exp(l3 - lm)
    inv = 1.0 / (e1 + e2 + e3)

    def per_head(w):
        out = jnp.zeros((TF, MIX_W), F32)
        for h in range(ATT_HEADS):
            out = jnp.where(lane_head == h, w[:, h:h + 1], out)
        return out

    y_b = per_head(e1 * inv) * o1 + per_head(e2 * inv) * o2 + per_head(e3 * inv) * o3
    hsum_t = jnp.concatenate([hf_ref[0, c] + hb_ref[0, c] for c in range(TF // ML_CHUNK)], axis=1)
    per_head_rows = hsum_t.reshape(ML_HEADS, ML_HD, TF)
    mu = jnp.mean(per_head_rows, axis=1, keepdims=True)
    cen = per_head_rows - mu
    var = jnp.mean(cen * cen, axis=1, keepdims=True)
    hn_t = (cen * lax.rsqrt(var + LN_EPS)).reshape(MIX_W, TF)
    y_c_t = (jax.nn.sigmoid(jnp.transpose(co_ref[0])) * (hn_t * mng_ref[...])).astype(BF16)
    ys = (ya_ref[0], y_b.astype(BF16), None, yd_ref[0])
    merged = jnp.zeros((TF, D_MODEL), F32)
    for n in range(N_BRANCH):
        cols = slice(n * D_MODEL, (n + 1) * D_MODEL)
        gate = jax.nn.sigmoid(jnp.dot(xb, wg_ref[:, cols], preferred_element_type=F32) + bg_ref[:, cols])
        if ys[n] is None:
            proj = lax.dot_general(y_c_t, wbr_ref[n], (((0,), (0,)), ((), ())), preferred_element_type=F32)
        else:
            proj = jnp.dot(ys[n], wbr_ref[n], preferred_element_type=F32)
        merged = merged + gate * proj
    mix = jnp.dot(merged.astype(BF16), wout_ref[...], preferred_element_type=F32)
    x1 = _standardize(alpha * x + mix) * lng_ref[...] + lnb_ref[...]
    x1_ref[0] = x1
    x1b = x1.astype(BF16)
    x1p_ref[0] = _pack_bf16_pair(x1b[:, :D_MODEL // 2], x1b[:, D_MODEL // 2:])
    logits = lax.dot_general(wr_ref[...], x1b, (((1,), (1,)), ((), ())),
                             preferred_element_type=F32)
    ex = jnp.exp(logits - jnp.max(logits, axis=0, keepdims=True))
    aff_ref[0] = ex / jnp.sum(ex, axis=0, keepdims=True)


def _merge_call(x, ya, o_list, l_list, hf, hb, co, yd, wg, bg, wbr, wout, mng, lng, lnb, wr_t, alpha):
    b, s, _ = x.shape
    tok = lambda w: pl.BlockSpec((1, TF, w), lambda b, i: (b, i, 0))
    grouped = lambda dil, w: pl.BlockSpec((1, dil, TF // dil, w), lambda b, i: (b, 0, i, 0))
    chunked = pl.BlockSpec((1, TF // ML_CHUNK, MIX_W, ML_CHUNK), lambda b, i: (b, i, 0, 0))
    const = lambda shp: pl.BlockSpec(shp, lambda b, i: (0,) * len(shp))
    return pl.pallas_call(
        functools.partial(_merge_kernel, alpha=alpha), name="merge_ln_router",
        grid=(b, s // TF),
        in_specs=[tok(D_MODEL), tok(MIX_W)] + [grouped(dil, MIX_W) for _, dil in DIL_PATTERNS]
                 + [grouped(dil, LANES) for _, dil in DIL_PATTERNS]
                 + [chunked, chunked, tok(MIX_W), tok(MIX_W)]
                 + [const((D_MODEL, N_BRANCH * D_MODEL)), const((1, N_BRANCH * D_MODEL)),
                    const((N_BRANCH, MIX_W, D_MODEL)), const((D_MODEL, D_MODEL)), const((MIX_W, 1)),
                    const((1, D_MODEL)), const((1, D_MODEL)), const((N_EXPERTS, D_MODEL))],
        out_specs=(tok(D_MODEL), tok(D_MODEL // 2), pl.BlockSpec((1, N_EXPERTS, TF), lambda b, i: (b, 0, i))),
        out_shape=(jax.ShapeDtypeStruct((b, s, D_MODEL), F32),
                   jax.ShapeDtypeStruct((b, s, D_MODEL // 2), jnp.int32),
                   jax.ShapeDtypeStruct((b, N_EXPERTS, s), F32)),
        scratch_shapes=[pltpu.VMEM((len(DIL_PATTERNS), MIX_W // LANES, TF, LANES), F32),
                        pltpu.VMEM((len(DIL_PATTERNS), 1, TF, LANES), F32)],
        compiler_params=_cparams(2),
    )(x, ya, *o_list, *l_list, hf, hb, co, yd, wg, bg, wbr, wout, mng, lng, lnb, wr_t)


TT = 256


def _select_kernel(aff_ref, slot_ref, *, cap):
    s = aff_ref.shape[2]
    bits = pltpu.bitcast(aff_ref[0], jnp.int32)

    def bit_step(i, thr):
        cand = thr | jnp.left_shift(jnp.int32(1), 30 - i)
        cnt = jnp.sum((bits >= cand).astype(jnp.int32), axis=1, keepdims=True)
        return jnp.where(cnt >= cap, cand, thr)

    thr = lax.fori_loop(0, 31, bit_step, jnp.zeros((N_EXPERTS, 1), jnp.int32))
    gt = bits > thr
    eq = bits == thr
    need = (cap - jnp.sum(gt.astype(jnp.int32), axis=1, keepdims=True)).astype(F32)
    upper = (lax.broadcasted_iota(jnp.int32, (TT, TT), 0)
             <= lax.broadcasted_iota(jnp.int32, (TT, TT), 1)).astype(BF16)
    eq_before = jnp.zeros((N_EXPERTS, 1), F32)
    sel_before = jnp.zeros((N_EXPERTS, 1), F32)
    for j in range(s // TT):
        cols = slice(j * TT, (j + 1) * TT)
        eq_j = eq[:, cols]
        eq_incl = eq_before + jnp.dot(eq_j.astype(BF16), upper, preferred_element_type=F32)
        sel_j = gt[:, cols] | (eq_j & (eq_incl <= need))
        sel_f = sel_j.astype(F32)
        sel_incl = sel_before + jnp.dot(sel_f.astype(BF16), upper, preferred_element_type=F32)
        slot_ref[0, :, cols] = jnp.where(sel_j, sel_incl - 1.0, -1.0).astype(jnp.int32)
        eq_before = eq_incl[:, TT - 1:TT]
        sel_before = sel_incl[:, TT - 1:TT]


def _select_call(aff_t, cap):
    b, e, s = aff_t.shape
    return pl.pallas_call(
        functools.partial(_select_kernel, cap=cap), name="expert_choice_select",
        grid=(b,),
        in_specs=[pl.BlockSpec((1, e, s), lambda i: (i, 0, 0))],
        out_specs=pl.BlockSpec((1, e, s), lambda i: (i, 0, 0)),
        out_shape=jax.ShapeDtypeStruct((b, e, s), jnp.int32),
        compiler_params=_cparams(1),
    )(aff_t)


SC_LANES = 16
SC_ROWS = 64
SC_IDX = 128
SC_SLAB = 128
SC_ZROWS = 64
CF = 1024


def _sc_dispatch_call(x_flat, slot2, aff2, seq, cap):
    n_pairs = slot2.shape[0]
    d = x_flat.shape[1]
    info = plsc.get_sparse_core_info()
    n_workers = info.num_cores * info.num_subcores
    assert n_pairs % n_workers == 0 and seq % SC_LANES == 0 and cap % (2 * SC_ROWS) == 0
    pairs_per_worker = n_pairs // n_workers
    mesh = plsc.VectorSubcoreMesh(core_axis_name="c", subcore_axis_name="s")

    @functools.partial(
        pl.kernel, mesh=mesh, name="expert_dispatch_sc",
        compiler_params=pltpu.CompilerParams(needs_layout_passes=False),
        out_type=(jax.ShapeDtypeStruct((n_pairs * cap, d), x_flat.dtype),
                  jax.ShapeDtypeStruct((n_pairs, cap), jnp.int32),
                  jax.ShapeDtypeStruct((n_pairs, cap), F32)),
        scratch_types=[pltpu.VMEM((seq,), jnp.int32), pltpu.VMEM((seq,), F32),
                       pltpu.VMEM((cap,), jnp.int32), pltpu.VMEM((cap,), jnp.int32),
                       pltpu.VMEM((cap,), F32),
                       pltpu.VMEM((2, SC_ROWS, d), x_flat.dtype), pltpu.SemaphoreType.DMA((2,))])
    def dispatch(x_hbm, slot_hbm, aff_hbm, xs_hbm, tok_hbm, gate_hbm,
                 slot_v, aff_v, idx_v, tok_v, gate_v, rows_v, sem):
        worker = lax.axis_index("s") * info.num_cores + lax.axis_index("c")
        lane = lax.iota(jnp.int32, SC_LANES)

        def gather(c0, buf):
            return pltpu.make_async_copy(x_hbm.at[idx_v.at[pl.ds(c0, SC_ROWS)]], rows_v.at[buf], sem.at[buf])
        for k in range(pairs_per_worker):
            pair = worker * pairs_per_worker + k
            row0 = (pair // N_EXPERTS) * seq
            pltpu.sync_copy(slot_hbm.at[pair], slot_v)
            pltpu.sync_copy(aff_hbm.at[pair], aff_v)

            @pl.loop(0, seq, step=SC_LANES)
            def _(t0):
                sv = slot_v[pl.ds(t0, SC_LANES)]
                picked = sv >= 0
                plsc.store_scatter(tok_v, [sv], t0 + lane, mask=picked)
                plsc.store_scatter(idx_v, [sv], row0 + t0 + lane, mask=picked)
                plsc.store_scatter(gate_v, [sv], aff_v[pl.ds(t0, SC_LANES)], mask=picked)

            pltpu.sync_copy(tok_v, tok_hbm.at[pair])
            pltpu.sync_copy(gate_v, gate_hbm.at[pair])

            gather(0, 0).start()

            @pl.loop(0, cap, step=2 * SC_ROWS)
            def _(c0):
                gather(c0 + SC_ROWS, 1).start()
                gather(c0, 0).wait()
                pltpu.sync_copy(rows_v.at[0], xs_hbm.at[pl.ds(pair * cap + c0, SC_ROWS)])

                @pl.when(c0 + 2 * SC_ROWS < cap)
                def _():
                    gather(c0 + 2 * SC_ROWS, 0).start()

                gather(c0 + SC_ROWS, 1).wait()
                pltpu.sync_copy(rows_v.at[1], xs_hbm.at[pl.ds(pair * cap + c0 + SC_ROWS, SC_ROWS)])

    return dispatch(x_flat, slot2, aff2)


def _expert_kernel(xs_ref, g_ref, w1_ref, w3_ref, w2_ref, ye_ref, w1_bf, w3_bf, w2_bf):
    @pl.when((pl.program_id(1) == 0) & (pl.program_id(2) == 0))
    def _():
        w1_bf[...] = w1_ref[0, 0].astype(BF16)
        w3_bf[...] = w3_ref[0, 0].astype(BF16)
        w2_bf[...] = w2_ref[0, 0].astype(BF16)

    xs = jnp.concatenate(_unpack_bf16_pair(xs_ref[0, 0]), axis=1)
    hid = (jax.nn.silu(jnp.dot(xs, w1_bf[...], preferred_element_type=F32))
           * jnp.dot(xs, w3_bf[...], preferred_element_type=F32))
    ye_ref[0, 0] = jnp.dot(hid.astype(BF16), w2_bf[...], preferred_element_type=F32) * g_ref[0, 0]


def _expert_call(xs4, gate4, w1, w3, w2, layer):
    b, e, cap, half = xs4.shape
    d, ff = w1.shape[2], w1.shape[3]
    assert d == 2 * half
    rows = lambda w: pl.BlockSpec((1, 1, CF, w), lambda e, b, j: (b, e, j, 0))
    wspec = lambda r, c: pl.BlockSpec((1, 1, r, c), lambda e, b, j: (layer, e, 0, 0))
    return pl.pallas_call(
        _expert_kernel, name="expert_ffn",
        grid=(e, b, cap // CF),
        in_specs=[rows(half), rows(1), wspec(d, ff), wspec(d, ff), wspec(ff, d)],
        out_specs=rows(d),
        out_shape=jax.ShapeDtypeStruct((b, e, cap, d), F32),
        scratch_shapes=[pltpu.VMEM((d, ff), BF16), pltpu.VMEM((d, ff), BF16), pltpu.VMEM((ff, d), BF16)],
        compiler_params=_cparams(3),
    )(xs4, gate4, w1, w3, w2)


def _sc_combine_call(ye_flat, tok3, seq):
    n_pairs, n_chunks, _ = tok3.shape
    cap = n_chunks * SC_IDX
    d = ye_flat.shape[1]
    nb = n_pairs // N_EXPERTS
    info = plsc.get_sparse_core_info()
    assert info.num_subcores == N_EXPERTS and nb % info.num_cores == 0 and n_chunks % 2 == 0
    assert seq % (info.num_subcores * SC_ZROWS) == 0 and d % SC_SLAB == 0
    batches_per_core = nb // info.num_cores
    own_rows = seq // info.num_subcores
    mesh = plsc.VectorSubcoreMesh(core_axis_name="c", subcore_axis_name="s")

    @functools.partial(
        pl.kernel, mesh=mesh, name="expert_combine_sc",
        compiler_params=pltpu.CompilerParams(needs_layout_passes=False),
        out_type=jax.ShapeDtypeStruct((nb * seq, d), F32),
        scratch_types=[pltpu.VMEM_SHARED((seq, SC_SLAB), F32),
                       pltpu.VMEM((n_chunks, SC_IDX), jnp.int32),
                       pltpu.VMEM((2, SC_IDX, SC_SLAB), F32),
                       pltpu.VMEM((SC_ZROWS, SC_SLAB), F32),
                       pltpu.SemaphoreType.DMA((2,))])
    def combine(ye_hbm, tok_hbm, out_hbm, acc_sh, tok_v, rows_v, zero_v, sem):
        core = lax.axis_index("c")
        sub = lax.axis_index("s")

        @pl.loop(0, SC_ZROWS)
        def _(r):
            for l0 in range(0, SC_SLAB, SC_LANES):
                zero_v[r, pl.ds(l0, SC_LANES)] = jnp.zeros((SC_LANES,), F32)

        for bb in range(batches_per_core):
            batch = core * batches_per_core + bb
            pair = batch * N_EXPERTS + sub
            pltpu.sync_copy(tok_hbm.at[pair], tok_v)

            @pl.loop(0, d // SC_SLAB)
            def _(slab):
                cols = pl.ds(pl.multiple_of(slab * SC_SLAB, SC_SLAB), SC_SLAB)

                @pl.loop(0, own_rows, step=SC_ZROWS)
                def _(r0):
                    pltpu.sync_copy(zero_v, acc_sh.at[pl.ds(sub * own_rows + r0, SC_ZROWS)])

                def load(j, buf):
                    return pltpu.make_async_copy(
                        ye_hbm.at[pl.ds(pair * cap + j * SC_IDX, SC_IDX), cols], rows_v.at[buf], sem.at[buf])

                load(0, 0).start()
                plsc.subcore_barrier()

                for j in range(0, n_chunks, 2):
                    load(j + 1, 1).start()
                    load(j, 0).wait()
                    pltpu.sync_copy(rows_v.at[0], acc_sh.at[tok_v.at[j]], add=True)
                    if j + 2 < n_chunks:
                        load(j + 2, 0).start()
                    load(j + 1, 1).wait()
                    pltpu.sync_copy(rows_v.at[1], acc_sh.at[tok_v.at[j + 1]], add=True)

                plsc.subcore_barrier()
                pltpu.sync_copy(acc_sh.at[pl.ds(sub * own_rows, own_rows)],
                                out_hbm.at[pl.ds(batch * seq + sub * own_rows, own_rows), cols])

    return combine(ye_flat, tok3)


TN = 512


def _resln_kernel(x_ref, y_ref, g_ref, b_ref, o_ref, *, alpha):
    o_ref[...] = _standardize(alpha * x_ref[...] + y_ref[...]) * g_ref[...] + b_ref[...]


def _resln_call(x2d, y2d, g, bta, alpha):
    n, d = x2d.shape
    tok = pl.BlockSpec((TN, d), lambda i: (i, 0))
    vec = pl.BlockSpec((1, d), lambda i: (0, 0))
    return pl.pallas_call(
        functools.partial(_resln_kernel, alpha=alpha), name="residual_layernorm",
        grid=(n // TN,), in_specs=[tok, tok, vec, vec], out_specs=tok,
        out_shape=jax.ShapeDtypeStruct((n, d), F32),
        compiler_params=_cparams(1),
    )(x2d, y2d, g, bta)


def _expert_choice_ffn(x1p, aff_t, w1, w3, w2, layer):
    b, s, half = x1p.shape
    d = 2 * half
    cap = EC_FACTOR * s // N_EXPERTS
    slot = _select_call(aff_t, cap)
    xs, tok, gate = _sc_dispatch_call(x1p.reshape(b * s, half), slot.reshape(b * N_EXPERTS, s),
                                      aff_t.reshape(b * N_EXPERTS, s), s, cap)
    ye = _expert_call(xs.reshape(b, N_EXPERTS, cap, half), gate.reshape(b, N_EXPERTS, cap, 1), w1, w3, w2,
                      layer)
    out = _sc_combine_call(ye.reshape(b * N_EXPERTS * cap, d),
                           tok.reshape(b * N_EXPERTS, cap // SC_IDX, SC_IDX), s)
    return out.reshape(b, s, d)


def _pack_pool(pool_w):
    g, gd, _ = pool_w.shape
    out = jnp.zeros((g * gd, g * gd), F32)
    for i in range(g):
        out = out.at[i * gd:(i + 1) * gd, i * gd:(i + 1) * gd].set(pool_w[i])
    return out.astype(BF16)


def _layer(layer, x, pending, alpha, bias_tiles, w_in, b_in, gm_ln_g, gm_ws, gm_bs, ml_conv, ml_fbias,
           ml_norm_g, pool_w, pool_scale, w_branch, w_out, ln1_g, ln1_b, w_router, w_e1, w_e3, w_e2):
    b, s, d = x.shape
    n_small = 2576
    w_cat, b_cat = _pack_inproj_weights(w_in, b_in)
    wscat, bsfull = _pack_gmlp(gm_ws, gm_bs)
    if pending is None:
        outs = _inproj_call(x.reshape(b * s, d), w_cat, b_cat, gm_ln_g[None], wscat, bsfull, b)
    else:
        *outs, x2 = _inproj_call(pending, w_cat, b_cat, gm_ln_g[None], wscat, bsfull, b, alpha)
        x = x2.reshape(b, s, d)
    ya, qkv1, qkv4, qkv16, cqk, cv, co, dx, gates_t = outs
    r3 = lambda t: t.reshape(b, s, t.shape[-1])
    o_list, l_list = [], []
    for qkv, bias in zip((qkv1, qkv4, qkv16), bias_tiles):
        o, lse = _attn_call(qkv, bias)
        o_list.append(o)
        l_list.append(lse)
    hf, hb = _mlstm_branch(r3(cqk), r3(cv), gates_t, ml_conv, ml_fbias)
    yd = _pool_call(r3(dx), _pack_pool(pool_w), pool_scale[None])
    x1, x1p, aff_t = _merge_call(
        x, r3(ya), o_list, l_list, hf, hb, r3(co), yd,
        w_in[:, n_small:].astype(BF16), b_in[None, n_small:], w_branch.astype(BF16), w_out.astype(BF16),
        ml_norm_g[:, None], ln1_g[None], ln1_b[None], jnp.transpose(w_router).astype(BF16), alpha)
    ffn = _expert_choice_ffn(x1p, aff_t, w_e1, w_e3, w_e2, layer)
    return x1.reshape(b * s, d), ffn.reshape(b * s, d)


def kernel(x, w_in, b_in, gm_ln_g, gm_ws, gm_bs, rel_bias, ml_conv, ml_fbias, ml_norm_g, pool_w,
           pool_scale, w_branch, w_out, ln1_g, ln1_b, w_router, w_e1, w_e3, w_e2, ln2_g, ln2_b):
    depth = w_in.shape[0]
    alpha = (2 * depth) ** 0.25
    bias_tiles = [_attn_bias_tile(rel_bias, window, dil) for window, dil in DIL_PATTERNS]
    b, s, d = x.shape
    pending = None
    for l in range(depth):
        x1, ffn = _layer(l, x, pending, alpha, bias_tiles, w_in[l], b_in[l], gm_ln_g[l], gm_ws[l], gm_bs[l],
                         ml_conv[l], ml_fbias[l], ml_norm_g[l], pool_w[l], pool_scale[l], w_branch[l],
                         w_out[l], ln1_g[l], ln1_b[l], w_router[l], w_e1, w_e3, w_e2)
        pending = (x1, ffn, ln2_g[l][None], ln2_b[l][None])
    return _resln_call(*pending, alpha).reshape(b, s, d)
```

```python
import functools
import math

import jax
import jax.numpy as jnp
import numpy as np
from jax import lax
from jax.experimental import pallas as pl
from jax.experimental.pallas import tpu as pltpu
from jax.experimental.pallas import tpu_sc as plsc

F32 = jnp.float32
BF16 = jnp.bfloat16

D_MODEL = 1024
MIX_W = 256
N_BRANCH = 4
GM_CHUNK = 128
GM_GROUPS = 4
ATT_HEADS = 4
ATT_HD = 64
DIL_PATTERNS = ((128, 1), (512, 4), (2048, 16))
ATT_BLOCK = 64
REL_BUCKETS = 32
REL_MAX_DIST = 1024
ML_HEADS = 4
ML_HD = 64
ML_CHUNK = 64
POOL_WINDOWS = (2, 4, 8, 16)
N_EXPERTS = 16
EXPERT_FF = 1024
EC_FACTOR = 2
LN_EPS = 1e-5
NEG_BIG = -1e30

V7X_VMEM_LIMIT = 56 * 1024 * 1024
LANES = 128
HALO = 8


def _cparams(n_grid, vmem=V7X_VMEM_LIMIT):
    return pltpu.CompilerParams(dimension_semantics=("arbitrary",) * n_grid,
                                vmem_limit_bytes=vmem)


def _pack_bf16_pair(lo, hi):
    lo_bits = lax.shift_right_logical(pltpu.bitcast(lo.astype(F32), jnp.int32), 16)
    return pltpu.bitcast(hi.astype(F32), jnp.int32) | lo_bits


def _unpack_bf16_pair(packed):
    lo = pltpu.bitcast(lax.shift_left(packed, 16), F32).astype(BF16)
    hi = pltpu.bitcast(packed & jnp.int32(-65536), F32).astype(BF16)
    return lo, hi


def _standardize(xf):
    mu = jnp.mean(xf, axis=-1, keepdims=True)
    var = jnp.mean(jnp.square(xf - mu), axis=-1, keepdims=True)
    return (xf - mu) * lax.rsqrt(var + LN_EPS)


TA = 512
A_COLS = 2560 + LANES


def _inproj_kernel(*refs, alpha):
    if alpha is None:
        x_ref, *refs = refs
        x = x_ref[...]
    else:
        x1_ref, y_ref, g2_ref, b2_ref, *refs = refs
        x = _standardize(alpha * x1_ref[...] + y_ref[...]) * g2_ref[...] + b2_ref[...]
        refs[-2][...] = x
        refs = refs[:-2] + refs[-1:]
    (w_ref, b_ref, lng_ref, wscat_ref, bsfull_ref,
     ya_ref, qkv1_ref, qkv4_ref, qkv16_ref, cqk_ref, cv_ref, co_ref, dx_ref, gt_ref, qkv_scr) = refs
    xb = x.astype(BF16)
    h = jnp.dot(xb, w_ref[...], preferred_element_type=F32) + b_ref[...]
    qkv1_ref[0, 0] = h[:, 512:1280].astype(BF16)
    for c in range(768 // LANES):
        qkv_scr[c] = h[:, 512 + c * LANES:512 + (c + 1) * LANES]
    for (_, dil), out_ref in zip(DIL_PATTERNS[1:], (qkv4_ref, qkv16_ref)):
        for r in range(dil):
            for c in range(768 // LANES):
                out_ref[0, r, :, c * LANES:(c + 1) * LANES] = (
                    qkv_scr[c, pl.ds(r, TA // dil, stride=dil), :].astype(BF16))
    cqk_ref[...] = h[:, 1280:1792]
    cv_ref[...] = h[:, 1792:2048].astype(BF16)
    co_ref[...] = h[:, 2048:2304]
    dx_ref[...] = h[:, 2304:2560]
    gates_t = jnp.transpose(h[:, 2560:2688])
    for j in range(TA // LANES):
        gt_ref[j] = gates_t[0:4 * ML_HEADS, j * LANES:(j + 1) * LANES]
    u = jax.nn.gelu(h[:, 0:256])
    v = jax.nn.gelu(h[:, 256:512])
    vn = _standardize(v) * lng_ref[...]
    lane_grp = lax.broadcasted_iota(jnp.int32, (GM_CHUNK, MIX_W), 1) // (MIX_W // GM_GROUPS)
    for c in range(TA // GM_CHUNK):
        vc = vn[c * GM_CHUNK:(c + 1) * GM_CHUNK]
        stacked = jnp.concatenate(
            [jnp.where(lane_grp == g, vc, 0.0).astype(BF16) for g in range(GM_GROUPS)], axis=0)
        mixed = jnp.dot(wscat_ref[...], stacked, preferred_element_type=F32) + bsfull_ref[...]
        ya_ref[c * GM_CHUNK:(c + 1) * GM_CHUNK, :] = (
            u[c * GM_CHUNK:(c + 1) * GM_CHUNK] * mixed).astype(BF16)


def _inproj_call(x_in, w_cat, b_cat, lng, wscat, bsfull, batch, alpha=None):
    fused = alpha is not None
    n = (x_in[0] if fused else x_in).shape[0]
    seq = n // batch
    tpb = seq // TA
    tok = lambda w: pl.BlockSpec((TA, w), lambda i: (i, 0))
    const = lambda s: pl.BlockSpec(s, lambda i: (0,) * len(s))
    regrouped = lambda dil: pl.BlockSpec((1, dil, TA // dil, 768), lambda i: (i // tpb, 0, i % tpb, 0))
    out_shape = (
        jax.ShapeDtypeStruct((n, 256), BF16),
    ) + tuple(jax.ShapeDtypeStruct((batch, dil, seq // dil, 768), BF16)
              for _, dil in DIL_PATTERNS) + (
        jax.ShapeDtypeStruct((n, 512), F32),
        jax.ShapeDtypeStruct((n, 256), BF16),
        jax.ShapeDtypeStruct((n, 256), F32),
        jax.ShapeDtypeStruct((n, 256), F32),
        jax.ShapeDtypeStruct((n // LANES, 4 * ML_HEADS, LANES), F32),
    )
    x_specs = [tok(D_MODEL), tok(D_MODEL), const((1, D_MODEL)), const((1, D_MODEL))] if fused else [tok(D_MODEL)]
    out_specs = ((tok(256),) + tuple(regrouped(dil) for _, dil in DIL_PATTERNS)
                 + (tok(512), tok(256), tok(256), tok(256),
                    pl.BlockSpec((TA // LANES, 4 * ML_HEADS, LANES), lambda i: (i, 0, 0))))
    if fused:
        out_specs += (tok(D_MODEL),)
        out_shape += (jax.ShapeDtypeStruct((n, D_MODEL), F32),)
    return pl.pallas_call(
        functools.partial(_inproj_kernel, alpha=alpha), name="inproj_gmlp",
        grid=(n // TA,),
        in_specs=x_specs + [const((D_MODEL, A_COLS)), const((1, A_COLS)), const((1, MIX_W)),
                            const((GM_CHUNK, GM_GROUPS * GM_CHUNK)), const((GM_CHUNK, MIX_W))],
        out_specs=out_specs,
        out_shape=out_shape,
        scratch_shapes=[pltpu.VMEM((768 // LANES, TA, LANES), F32)],
        compiler_params=_cparams(1),
    )(*(x_in if fused else (x_in,)), w_cat, b_cat, lng, wscat, bsfull)


def _pack_inproj_weights(w_in, b_in):
    pad = lambda a: jnp.pad(a, ((0, 0), (0, LANES - 4 * ML_HEADS)))
    w_cat = jnp.concatenate([w_in[:, 0:2304], w_in[:, 2320:2576], pad(w_in[:, 2304:2320])], axis=1)
    b2 = b_in[None, :]
    b_cat = jnp.concatenate([b2[:, 0:2304], b2[:, 2320:2576], pad(b2[:, 2304:2320])], axis=1)
    return w_cat.astype(BF16), b_cat


def _pack_gmlp(gm_ws, gm_bs):
    wscat = jnp.transpose(gm_ws, (1, 0, 2)).reshape(GM_CHUNK, GM_GROUPS * GM_CHUNK).astype(BF16)
    bsfull = jnp.repeat(jnp.transpose(gm_bs), MIX_W // GM_GROUPS, axis=1)
    return wscat, bsfull


def _halo_specs(t, width, n_tiles):
    r = t // HALO
    main = pl.BlockSpec((1, t, width), lambda b, i: (b, i, 0))
    prev = pl.BlockSpec((1, HALO, width), lambda b, i: (b, jnp.maximum(i * r - 1, 0), 0))
    nxt = pl.BlockSpec((1, HALO, width), lambda b, i: (b, jnp.minimum((i + 1) * r, n_tiles * r - 1), 0))
    return main, prev, nxt


def _fill_halo_scratch(buf, x_ref, p_ref, n_ref, t):
    i = pl.program_id(1)
    last = pl.num_programs(1) - 1
    buf[0:HALO, :] = jnp.where(i > 0, p_ref[0], 0.0)
    buf[HALO:HALO + t, :] = x_ref[0]
    buf[HALO + t:2 * HALO + t, :] = jnp.where(i < last, n_ref[0], 0.0)


TP = 512


def _pool_kernel(x_ref, p_ref, n_ref, w_ref, sc_ref, o_ref, buf, lvl):
    _fill_halo_scratch(buf, x_ref, p_ref, n_ref, TP)
    seq = pl.num_programs(1) * TP
    pos = pl.program_id(1) * TP + lax.broadcasted_iota(jnp.int32, (TP, 1), 0)
    lane_grp = lax.broadcasted_iota(jnp.int32, (TP, MIX_W), 1) // (MIX_W // len(POOL_WINDOWS))
    x0 = buf[HALO:HALO + TP, :]
    sums = []
    src, rows = buf, TP + 2 * HALO
    for k, win in enumerate(POOL_WINDOWS):
        half = win // 2
        rows -= half
        cur = src[0:rows, :] + src[half:rows + half, :] if k else buf[0:rows, :] + buf[1:rows + 1, :]
        if k + 1 < len(POOL_WINDOWS):
            lvl[k, 0:rows, :] = cur
            sums.append(lvl[k, HALO - half:HALO - half + TP, :])
            src = lvl.at[k]
        else:
            sums.append(cur[0:TP])
    pooled = jnp.zeros((TP, MIX_W), F32)
    for gi, win in enumerate(POOL_WINDOWS):
        half = win // 2
        cnt = (jnp.minimum(pos + half, seq) - jnp.maximum(pos - half, 0)).astype(F32)
        pooled = jnp.where(lane_grp == gi, sums[gi] / cnt - x0, pooled)
    mixed = jnp.dot(pooled.astype(BF16), w_ref[...], preferred_element_type=F32)
    o_ref[0] = (mixed * sc_ref[...]).astype(BF16)


def _pool_call(dx, w_block, scale):
    b, s, _ = dx.shape
    nt = s // TP
    main, prev, nxt = _halo_specs(TP, MIX_W, nt)
    return pl.pallas_call(
        _pool_kernel, name="pool_mixer",
        grid=(b, nt),
        in_specs=[main, prev, nxt,
                  pl.BlockSpec((MIX_W, MIX_W), lambda b, i: (0, 0)),
                  pl.BlockSpec((1, MIX_W), lambda b, i: (0, 0))],
        out_specs=pl.BlockSpec((1, TP, MIX_W), lambda b, i: (b, i, 0)),
        out_shape=jax.ShapeDtypeStruct((b, s, MIX_W), BF16),
        scratch_shapes=[pltpu.VMEM((TP + 2 * HALO, MIX_W), F32),
                        pltpu.VMEM((len(POOL_WINDOWS) - 1, TP + 2 * HALO, MIX_W), F32)],
        compiler_params=_cparams(2),
    )(dx, dx, dx, w_block, scale)


TQ = 128
TQS = 512
TKEYS = TQ + 2 * ATT_BLOCK


def _attn_kernel(q_ref, kp_ref, km_ref, kn_ref, vp_ref, vm_ref, vn_ref, bias_ref, o_ref, lse_ref):
    i = pl.program_id(2)
    q = q_ref[0, 0] * ATT_HD ** -0.5
    k = jnp.concatenate([kp_ref[0, 0], km_ref[0, 0], kn_ref[0, 0]], axis=0)
    v = jnp.concatenate([vp_ref[0, 0], vm_ref[0, 0], vn_ref[0, 0]], axis=0)
    lane = lax.broadcasted_iota(jnp.int32, (TQ, LANES), 1)
    lane_half = lax.broadcasted_iota(jnp.int32, (1, LANES), 1) // ATT_HD
    keep = [jnp.where(lane_half == hh, 1.0, 0.0).astype(BF16) for hh in range(2)]
    n_sub = TQS // TQ
    last_step = pl.num_programs(2) - 1
    for j in range(n_sub):
        if j == 0:
            variant = jnp.where(i == 0, 0, 1)
        elif j == n_sub - 1:
            variant = jnp.where(i == last_step, 2, 1)
        else:
            variant = 1
        qrows = slice(j * TQ, (j + 1) * TQ)
        krows = slice(j * TQ, j * TQ + TKEYS)
        lse_tile = jnp.zeros((TQ, LANES), F32)
        for pair in range(ATT_HEADS // 2):
            grp = slice(pair * LANES, (pair + 1) * LANES)
            q_pair, k_pair, v_pair = q[qrows, grp], k[krows, grp], v[krows, grp]
            o_pair = jnp.zeros((TQ, LANES), F32)
            for hh in range(2):
                h = 2 * pair + hh
                logits = lax.dot_general(q_pair * keep[hh], k_pair, (((1,), (1,)), ((), ())),
                                         preferred_element_type=F32) + bias_ref[variant, h]
                m = jnp.max(logits, axis=-1, keepdims=True)
                p = jnp.exp(logits - m)
                ssum = jnp.sum(p, axis=-1, keepdims=True)
                o = jnp.dot(p.astype(BF16), v_pair, preferred_element_type=F32) / ssum
                o_pair = jnp.where(lane_half == hh, o, o_pair)
                lse_tile = jnp.where(lane == h, m + jnp.log(ssum), lse_tile)
            o_ref[0, 0, qrows, grp] = o_pair
        lse_ref[0, 0, qrows, :] = lse_tile


def _attn_call(qkv, bias):
    b, dil, l, _ = qkv.shape
    nt = l // TQS
    r64 = TQS // ATT_BLOCK
    main = lambda c: pl.BlockSpec((1, 1, TQS, MIX_W), lambda b, r, i: (b, r, i, c))
    prev = lambda c: pl.BlockSpec((1, 1, ATT_BLOCK, MIX_W),
                                  lambda b, r, i: (b, r, jnp.maximum(i * r64 - 1, 0), c))
    nxt = lambda c: pl.BlockSpec((1, 1, ATT_BLOCK, MIX_W),
                                 lambda b, r, i: (b, r, jnp.minimum((i + 1) * r64, nt * r64 - 1), c))
    return pl.pallas_call(
        _attn_kernel, name="band_attention",
        grid=(b, dil, nt),
        in_specs=[main(0), prev(1), main(1), nxt(1), prev(2), main(2), nxt(2),
                  pl.BlockSpec((3, ATT_HEADS, TQ, TKEYS), lambda b, r, i: (0, 0, 0, 0))],
        out_specs=(pl.BlockSpec((1, 1, TQS, MIX_W), lambda b, r, i: (b, r, i, 0)),
                   pl.BlockSpec((1, 1, TQS, LANES), lambda b, r, i: (b, r, i, 0))),
        out_shape=(jax.ShapeDtypeStruct((b, dil, l, MIX_W), F32),
                   jax.ShapeDtypeStruct((b, dil, l, LANES), F32)),
        compiler_params=_cparams(3),
    )(qkv, qkv, qkv, qkv, qkv, qkv, qkv, bias)


def _t5_bucket_static(rel):
    half = REL_BUCKETS // 2
    max_exact = half // 2
    ret = np.where(rel > 0, half, 0)
    n = np.abs(rel)
    nf = np.maximum(n, 1).astype(np.float32)
    large = max_exact + (np.log(nf / np.float32(max_exact)) / np.float32(math.log(REL_MAX_DIST / max_exact))
                         * np.float32(half - max_exact)).astype(np.int32)
    large = np.minimum(large, half - 1)
    return ret + np.where(n < max_exact, n, large)


def _attn_bias_tile(rel_bias, window, dil):
    side = (window // 2) // dil
    rel = np.arange(TKEYS)[None, :] - ATT_BLOCK - np.arange(TQ)[:, None]
    onehot = jax.nn.one_hot(jnp.asarray(_t5_bucket_static(dil * rel), jnp.int32), REL_BUCKETS, dtype=F32)
    bias = jnp.einsum('qkr,rh->hqk', onehot, rel_bias, precision=lax.Precision.HIGHEST)
    key = np.arange(TKEYS)[None, :]
    inside = np.abs(rel) <= side
    masks = np.stack([inside & (key >= ATT_BLOCK), inside, inside & (key < ATT_BLOCK + TQ)])
    return jnp.where(jnp.asarray(masks)[:, None], bias[None], NEG_BIG)


TM = 512
VT_ROWS = ML_HD + 16


def _mlprep_kernel(x_ref, p_ref, n_ref, v_ref, w_ref, qt_out, k_out, vt_out, buf):
    _fill_halo_scratch(buf, x_ref, p_ref, n_ref, TM)
    conv = (buf[HALO - 1:HALO - 1 + TM, :] * w_ref[0:1, :] + buf[HALO:HALO + TM, :] * w_ref[1:2, :]
            + buf[HALO + 1:HALO + 1 + TM, :] * w_ref[2:3, :])
    qk = jax.nn.silu(conv)
    qt = jnp.transpose(qk[:, :MIX_W])
    vt = jnp.transpose(v_ref[0].astype(F32))
    ones_rows = jnp.where(lax.broadcasted_iota(jnp.int32, (VT_ROWS - ML_HD, ML_CHUNK), 0) == 0, 1.0, 0.0)
    for h in range(ML_HEADS):
        sl = slice(h * ML_HD, (h + 1) * ML_HD)
        k_out[0, h] = (qk[:, MIX_W + h * ML_HD:MIX_W + (h + 1) * ML_HD] * ML_HD ** -0.5).astype(BF16)
        for c in range(TM // ML_CHUNK):
            cl = slice(c * ML_CHUNK, (c + 1) * ML_CHUNK)
            qt_out[0, h, c] = qt[sl, cl].astype(BF16)
            vt_out[0, h, c] = jnp.concatenate([vt[sl, cl], ones_rows], axis=0).astype(BF16)


def _mlprep_call(cqk, cv, conv_w):
    b, s, _ = cqk.shape
    nt = s // TM
    nc = s // ML_CHUNK
    cpt = TM // ML_CHUNK
    main, prev, nxt = _halo_specs(TM, 2 * MIX_W, nt)
    return pl.pallas_call(
        _mlprep_kernel, name="mlstm_prep",
        grid=(b, nt),
        in_specs=[main, prev, nxt,
                  pl.BlockSpec((1, TM, MIX_W), lambda b, i: (b, i, 0)),
                  pl.BlockSpec((3, 2 * MIX_W), lambda b, i: (0, 0))],
        out_specs=(pl.BlockSpec((1, ML_HEADS, cpt, ML_HD, ML_CHUNK), lambda b, i: (b, 0, i, 0, 0)),
                   pl.BlockSpec((1, ML_HEADS, TM, ML_HD), lambda b, i: (b, 0, i, 0)),
                   pl.BlockSpec((1, ML_HEADS, cpt, VT_ROWS, ML_CHUNK), lambda b, i: (b, 0, i, 0, 0))),
        out_shape=(jax.ShapeDtypeStruct((b, ML_HEADS, nc, ML_HD, ML_CHUNK), BF16),
                   jax.ShapeDtypeStruct((b, ML_HEADS, s, ML_HD), BF16),
                   jax.ShapeDtypeStruct((b, ML_HEADS, nc, VT_ROWS, ML_CHUNK), BF16)),
        scratch_shapes=[pltpu.VMEM((TM + 2 * HALO, 2 * MIX_W), F32)],
        compiler_params=_cparams(2),
    )(cqk, cqk, cqk, cv, conv_w)


def _gate_kernel(g_ref, fb_ref, a_ref, m_ref, iw_ref, en_ref, ws_ref, dec_ref):
    x = g_ref[...]
    n_pairs = x.shape[0]
    n_ch = 2 * ML_HEADS
    lane = lax.broadcasted_iota(jnp.int32, (1, 1, LANES), 2)
    t_in = lane % ML_CHUNK
    second = lane >= ML_CHUNK
    fwd_row = lax.broadcasted_iota(jnp.int32, (1, n_ch, 1), 1) < ML_HEADS
    li = x[:, 0:n_ch, :]
    z = x[:, n_ch:2 * n_ch, :] + fb_ref[...]
    lf = jnp.minimum(z, 0.0) - jnp.log1p(jnp.exp(-jnp.abs(z)))

    def within_chunk(v, op, ident, prefix):
        s = 1
        while s < ML_CHUNK:
            nb = pltpu.roll(v, s if prefix else LANES - s, 2)
            ok = (t_in >= s) if prefix else (t_in < ML_CHUNK - s)
            v = op(v, jnp.where(ok, nb, ident))
            s *= 2
        return v

    pre = within_chunk(lf, jnp.add, 0.0, True)
    suf = within_chunk(lf, jnp.add, 0.0, False)
    g = pre + suf - lf
    b = jnp.where(fwd_row, pre, suf)
    a = li - b
    cm_pre = within_chunk(a, jnp.maximum, -jnp.inf, True)
    cm_suf = within_chunk(a, jnp.maximum, -jnp.inf, False)
    cm = jnp.where(fwd_row, cm_pre, cm_suf)
    amax = jnp.maximum(cm_pre, cm_suf)

    def shift_pairs(v, k, fill):
        pad = jnp.full((abs(k),) + v.shape[1:], fill, F32)
        return (jnp.concatenate([pad, v[:n_pairs - k]], axis=0) if k > 0
                else jnp.concatenate([v[-k:], pad], axis=0))

    def from_chunk(v, dist, fill, forward):
        if dist == 1:
            y = pltpu.roll(v, ML_CHUNK, 2)
            if forward:
                return jnp.where(second, y, shift_pairs(y, 1, fill))
            return jnp.where(second, shift_pairs(y, -1, fill), y)
        return shift_pairs(v, dist // 2 if forward else -(dist // 2), fill)

    def running_stabiliser(forward):
        big_g, big_a = g, amax + g
        dist = 1
        while dist < 2 * n_pairs:
            gp = from_chunk(big_g, dist, 0.0, forward)
            ap = from_chunk(big_a, dist, -jnp.inf, forward)
            big_g, big_a = gp + big_g, jnp.maximum(ap + big_g, big_a)
            dist *= 2
        ge = from_chunk(big_g, 1, 0.0, forward)
        ae = from_chunk(big_a, 1, -jnp.inf, forward)
        return jnp.maximum(ge, ae)

    m_chunk = jnp.where(fwd_row, running_stabiliser(True), running_stabiliser(False))
    m_t = jnp.maximum(cm, m_chunk)
    m_last = jnp.maximum(amax, m_chunk)
    a_ref[0] = a
    m_ref[0] = m_t
    iw_ref[0] = jnp.exp(m_chunk - m_t)
    en_ref[0] = jnp.exp(-(b + m_t))
    ws_ref[0] = jnp.exp(a - m_last)
    dec_ref[0] = jnp.exp(m_chunk - m_last)


def _gate_call(gates_t, fbias_col, batch):
    n_pairs = gates_t.shape[0] // batch
    n_ch = 2 * ML_HEADS
    out = pl.BlockSpec((1, n_pairs, n_ch, LANES), lambda i: (i, 0, 0, 0))
    return pl.pallas_call(
        _gate_kernel, name="mlstm_gates",
        grid=(batch,),
        in_specs=[pl.BlockSpec((n_pairs, 2 * n_ch, LANES), lambda i: (i, 0, 0)),
                  pl.BlockSpec((n_ch, 1), lambda i: (0, 0))],
        out_specs=(out,) * 6,
        out_shape=(jax.ShapeDtypeStruct((batch, n_pairs, n_ch, LANES), F32),) * 6,
        compiler_params=_cparams(1),
    )(gates_t, fbias_col)


TE = 1024


def _mlstm_kernel(*refs):
    fwd, bwd, (hf_ref, hb_ref, state) = refs[:9], refs[9:18], refs[18:]
    i = pl.program_id(1)

    @pl.when(i == 0)
    def _():
        state[...] = jnp.zeros(state.shape, F32)

    n_pairs = TE // LANES
    s_idx = lax.broadcasted_iota(jnp.int32, (ML_CHUNK, ML_CHUNK), 0)
    t_idx = lax.broadcasted_iota(jnp.int32, (ML_CHUNK, ML_CHUNK), 1)

    def chunk_step(jobs):
        for j in jobs:
            j["cst"] = state[j["ch"]]
            j["st"] = jnp.dot(j["k"], j["qt"], preferred_element_type=F32)
            j["inter"] = jnp.dot(j["cst"].astype(BF16), j["qt"], preferred_element_type=F32)
            j["upd"] = jnp.dot((j["vt"].astype(F32) * j["ws"]).astype(BF16), j["k"],
                               preferred_element_type=F32)
        for j in jobs:
            j["swt"] = j["st"] * jnp.exp(jnp.where(j["tri"], j["a"] - j["m"], NEG_BIG))
            j["intra"] = jnp.dot(j["vt"], j["swt"].astype(BF16), preferred_element_type=F32)
        for j in jobs:
            den = jnp.sum(j["swt"], axis=0, keepdims=True) + j["iw"] * j["inter"][ML_HD:ML_HD + 1]
            tot = j["intra"][:ML_HD] + j["iw"] * j["inter"][:ML_HD]
            out_r, cc, h = j["out"]
            out_r[0, cc, h * ML_HD:(h + 1) * ML_HD, :] = tot / jnp.maximum(jnp.abs(den), j["en"])
            state[j["ch"]] = j["dec"] * j["cst"] + j["upd"]

    def pair_body(p, carry):
        tiles = []
        for d, (qt_r, k_r, vt_r, a_r, m_r, iw_r, en_r, ws_r, dec_r) in ((0, fwd), (1, bwd)):
            pp = p if d == 0 else n_pairs - 1 - p
            a_t = jnp.transpose(jnp.concatenate(
                [a_r[0, pp], jnp.zeros((LANES - 2 * ML_HEADS, LANES), F32)], axis=0))
            rows = [r[0, pp] for r in (m_r, iw_r, en_r, ws_r, dec_r)]
            tiles.append((pp, a_t, rows, [pltpu.roll(t, ML_CHUNK, 1) for t in rows]))
        for step in range(2):
            jobs = []
            for d, (refs_d, out_r) in enumerate(((fwd, hf_ref), (bwd, hb_ref))):
                qt_r, k_r, vt_r = refs_d[:3]
                pp, a_t, first, second = tiles[d]
                half = step if d == 0 else 1 - step
                cc = 2 * pp + half
                hs = slice(half * ML_CHUNK, (half + 1) * ML_CHUNK)
                srows = pl.ds(pl.multiple_of(cc * ML_CHUNK, ML_CHUNK), ML_CHUNK)
                m_t, iw_t, en_t, ws_t, dec_t = [t[:, :ML_CHUNK] for t in (second if half else first)]
                for h in range(ML_HEADS):
                    ch = d * ML_HEADS + h
                    jobs.append(dict(
                        ch=ch, tri=(s_idx >= t_idx) if d else (s_idx <= t_idx),
                        k=k_r[0, h, srows, :], qt=qt_r[0, h, cc], vt=vt_r[0, h, cc], a=a_t[hs, ch:ch + 1],
                        m=m_t[ch:ch + 1], iw=iw_t[ch:ch + 1], en=en_t[ch:ch + 1],
                        ws=ws_t[ch:ch + 1], dec=dec_t[ch:ch + 1], out=(out_r, cc, h)))
            chunk_step(jobs)
        return carry

    lax.fori_loop(0, n_pairs, pair_body, 0)


def _mlstm_call(qt, k, vt, a_t, m_t, iw_t, en_t, ws_t, dec_t):
    b, _, s, _ = k.shape
    nt = s // TE
    nc = s // ML_CHUNK
    cpt = TE // ML_CHUNK

    def specs(rev):
        ti = (lambda i: nt - 1 - i) if rev else (lambda i: i)
        tile = pl.BlockSpec((1, TE // LANES, 2 * ML_HEADS, LANES), lambda b, i: (b, ti(i), 0, 0))
        return [
            pl.BlockSpec((1, ML_HEADS, cpt, ML_HD, ML_CHUNK), lambda b, i: (b, 0, ti(i), 0, 0)),
            pl.BlockSpec((1, ML_HEADS, TE, ML_HD), lambda b, i: (b, 0, ti(i), 0)),
            pl.BlockSpec((1, ML_HEADS, cpt, VT_ROWS, ML_CHUNK), lambda b, i: (b, 0, ti(i), 0, 0)),
            tile, tile, tile, tile, tile, tile]

    args = [qt, k, vt, a_t, m_t, iw_t, en_t, ws_t, dec_t]
    out_f = pl.BlockSpec((1, cpt, MIX_W, ML_CHUNK), lambda b, i: (b, i, 0, 0))
    out_b = pl.BlockSpec((1, cpt, MIX_W, ML_CHUNK), lambda b, i: (b, nt - 1 - i, 0, 0))
    return pl.pallas_call(
        _mlstm_kernel, name="mlstm_scan",
        grid=(b, nt),
        in_specs=specs(False) + specs(True),
        out_specs=(out_f, out_b),
        out_shape=(jax.ShapeDtypeStruct((b, nc, MIX_W, ML_CHUNK), F32),) * 2,
        scratch_shapes=[pltpu.VMEM((2 * ML_HEADS, VT_ROWS, ML_HD), F32)],
        compiler_params=_cparams(2),
    )(*args, *args)


def _mlstm_branch(cqk, cv, gates_t, conv_w, fbias):
    qt, k, vt = _mlprep_call(cqk, cv, conv_w)
    factors = _gate_call(gates_t, fbias.reshape(2 * ML_HEADS, 1), cqk.shape[0])
    return _mlstm_call(qt, k, vt, *factors)


TF = 512


def _merge_kernel(x_ref, ya_ref, o1_ref, o2_ref, o3_ref, l1_ref, l2_ref, l3_ref, hf_ref, hb_ref,
                  co_ref, yd_ref, wg_ref, bg_ref, wbr_ref, wout_ref, mng_ref, lng_ref, lnb_ref,
                  wr_ref, x1_ref, x1p_ref, aff_ref, o_scr, l_scr, *, alpha):
    x = x_ref[0]
    xb = x.astype(BF16)

    def natural_order(src_ref, scr):
        dil, width = src_ref.shape[1], src_ref.shape[3]
        if dil == 1:
            return src_ref[0, 0]
        for r in range(dil):
            for c in range(width // LANES):
                scr[c, pl.ds(r, TF // dil, stride=dil), :] = src_ref[0, r, :, c * LANES:(c + 1) * LANES]
        return jnp.concatenate([scr[c] for c in range(width // LANES)], axis=1)

    lane_head = lax.broadcasted_iota(jnp.int32, (TF, MIX_W), 1) // ML_HD
    l1, l2, l3 = [natural_order(r, l_scr.at[p]) for p, r in enumerate((l1_ref, l2_ref, l3_ref))]
    o1, o2, o3 = [natural_order(r, o_scr.at[p]) for p, r in enumerate((o1_ref, o2_ref, o3_ref))]
    lm = jnp.maximum(jnp.maximum(l1, l2), l3)
    e1, e2, e3 = jnp.exp(l1 - lm), jnp.exp(l2 - lm), jnp.exp(l3 - lm)
    inv = 1.0 / (e1 + e2 + e3)

    def per_head(w):
        out = jnp.zeros((TF, MIX_W), F32)
        for h in range(ATT_HEADS):
            out = jnp.where(lane_head == h, w[:, h:h + 1], out)
        return out

    y_b = per_head(e1 * inv) * o1 + per_head(e2 * inv) * o2 + per_head(e3 * inv) * o3
    hsum_t = jnp.concatenate([hf_ref[0, c] + hb_ref[0, c] for c in range(TF // ML_CHUNK)], axis=1)
    per_head_rows = hsum_t.reshape(ML_HEADS, ML_HD, TF)
    mu = jnp.mean(per_head_rows, axis=1, keepdims=True)
    cen = per_head_rows - mu
    var = jnp.mean(cen * cen, axis=1, keepdims=True)
    hn_t = (cen * lax.rsqrt(var + LN_EPS)).reshape(MIX_W, TF)
    y_c_t = (jax.nn.sigmoid(jnp.transpose(co_ref[0])) * (hn_t * mng_ref[...])).astype(BF16)
    ys = (ya_ref[0], y_b.astype(BF16), None, yd_ref[0])
    merged = jnp.zeros((TF, D_MODEL), F32)
    for n in range(N_BRANCH):
        cols = slice(n * D_MODEL, (n + 1) * D_MODEL)
        gate = jax.nn.sigmoid(jnp.dot(xb, wg_ref[:, cols], preferred_element_type=F32) + bg_ref[:, cols])
        if ys[n] is None:
            proj = lax.dot_general(y_c_t, wbr_ref[n], (((0,), (0,)), ((), ())), preferred_element_type=F32)
        else:
            proj = jnp.dot(ys[n], wbr_ref[n], preferred_element_type=F32)
        merged = merged + gate * proj
    mix = jnp.dot(merged.astype(BF16), wout_ref[...], preferred_element_type=F32)
    x1 = _standardize(alpha * x + mix) * lng_ref[...] + lnb_ref[...]
    x1_ref[0] = x1
    x1b = x1.astype(BF16)
    x1p_ref[0] = _pack_bf16_pair(x1b[:, :D_MODEL // 2], x1b[:, D_MODEL // 2:])
    logits = lax.dot_general(wr_ref[...], x1b, (((1,), (1,)), ((), ())),
                             preferred_element_type=F32)
    ex = jnp.exp(logits - jnp.max(logits, axis=0, keepdims=True))
    aff_ref[0] = ex / jnp.sum(ex, axis=0, keepdims=True)


def _merge_call(x, ya, o_list, l_list, hf, hb, co, yd, wg, bg, wbr, wout, mng, lng, lnb, wr_t, alpha):
    b, s, _ = x.shape
    tok = lambda w: pl.BlockSpec((1, TF, w), lambda b, i: (b, i, 0))
    grouped = lambda dil, w: pl.BlockSpec((1, dil, TF // dil, w), lambda b, i: (b, 0, i, 0))
    chunked = pl.BlockSpec((1, TF // ML_CHUNK, MIX_W, ML_CHUNK), lambda b, i: (b, i, 0, 0))
    const = lambda shp: pl.BlockSpec(shp, lambda b, i: (0,) * len(shp))
    return pl.pallas_call(
        functools.partial(_merge_kernel, alpha=alpha), name="merge_ln_router",
        grid=(b, s // TF),
        in_specs=[tok(D_MODEL), tok(MIX_W)] + [grouped(dil, MIX_W) for _, dil in DIL_PATTERNS]
                 + [grouped(dil, LANES) for _, dil in DIL_PATTERNS]
                 + [chunked, chunked, tok(MIX_W), tok(MIX_W)]
                 + [const((D_MODEL, N_BRANCH * D_MODEL)), const((1, N_BRANCH * D_MODEL)),
                    const((N_BRANCH, MIX_W, D_MODEL)), const((D_MODEL, D_MODEL)), const((MIX_W, 1)),
                    const((1, D_MODEL)), const((1, D_MODEL)), const((N_EXPERTS, D_MODEL))],
        out_specs=(tok(D_MODEL), tok(D_MODEL // 2), pl.BlockSpec((1, N_EXPERTS, TF), lambda b, i: (b, 0, i))),
        out_shape=(jax.ShapeDtypeStruct((b, s, D_MODEL), F32),
                   jax.ShapeDtypeStruct((b, s, D_MODEL // 2), jnp.int32),
                   jax.ShapeDtypeStruct((b, N_EXPERTS, s), F32)),
        scratch_shapes=[pltpu.VMEM((len(DIL_PATTERNS), MIX_W // LANES, TF, LANES), F32),
                        pltpu.VMEM((len(DIL_PATTERNS), 1, TF, LANES), F32)],
        compiler_params=_cparams(2),
    )(x, ya, *o_list, *l_list, hf, hb, co, yd, wg, bg, wbr, wout, mng, lng, lnb, wr_t)


TT = 256


def _select_kernel(aff_ref, slot_ref, *, cap):
    s = aff_ref.shape[2]
    bits = pltpu.bitcast(aff_ref[0], jnp.int32)

    def bit_step(i, thr):
        cand = thr | jnp.left_shift(jnp.int32(1), 30 - i)
        cnt = jnp.sum((bits >= cand).astype(jnp.int32), axis=1, keepdims=True)
        return jnp.where(cnt >= cap, cand, thr)

    thr = lax.fori_loop(0, 31, bit_step, jnp.zeros((N_EXPERTS, 1), jnp.int32))
    gt = bits > thr
    eq = bits == thr
    need = (cap - jnp.sum(gt.astype(jnp.int32), axis=1, keepdims=True)).astype(F32)
    upper = (lax.broadcasted_iota(jnp.int32, (TT, TT), 0)
             <= lax.broadcasted_iota(jnp.int32, (TT, TT), 1)).astype(BF16)
    eq_before = jnp.zeros((N_EXPERTS, 1), F32)
    sel_before = jnp.zeros((N_EXPERTS, 1), F32)
    for j in range(s // TT):
        cols = slice(j * TT, (j + 1) * TT)
        eq_j = eq[:, cols]
        eq_incl = eq_before + jnp.dot(eq_j.astype(BF16), upper, preferred_element_type=F32)
        sel_j = gt[:, cols] | (eq_j & (eq_incl <= need))
        sel_f = sel_j.astype(F32)
        sel_incl = sel_before + jnp.dot(sel_f.astype(BF16), upper, preferred_element_type=F32)
        slot_ref[0, :, cols] = jnp.where(sel_j, sel_incl - 1.0, -1.0).astype(jnp.int32)
        eq_before = eq_incl[:, TT - 1:TT]
        sel_before = sel_incl[:, TT - 1:TT]


def _select_call(aff_t, cap):
    b, e, s = aff_t.shape
    return pl.pallas_call(
        functools.partial(_select_kernel, cap=cap), name="expert_choice_select",
        grid=(b,),
        in_specs=[pl.BlockSpec((1, e, s), lambda i: (i, 0, 0))],
        out_specs=pl.BlockSpec((1, e, s), lambda i: (i, 0, 0)),
        out_shape=jax.ShapeDtypeStruct((b, e, s), jnp.int32),
        compiler_params=_cparams(1),
    )(aff_t)


SC_LANES = 16
SC_ROWS = 64
SC_IDX = 128
SC_SLAB = 128
SC_ZROWS = 64
CF = 1024


def _sc_dispatch_call(x_flat, slot2, aff2, seq, cap):
    n_pairs = slot2.shape[0]
    d = x_flat.shape[1]
    info = plsc.get_sparse_core_info()
    n_workers = info.num_cores * info.num_subcores
    assert n_pairs % n_workers == 0 and seq % SC_LANES == 0 and cap % (2 * SC_ROWS) == 0
    pairs_per_worker = n_pairs // n_workers
    mesh = plsc.VectorSubcoreMesh(core_axis_name="c", subcore_axis_name="s")

    @functools.partial(
        pl.kernel, mesh=mesh, name="expert_dispatch_sc",
        compiler_params=pltpu.CompilerParams(needs_layout_passes=False),
        out_type=(jax.ShapeDtypeStruct((n_pairs * cap, d), x_flat.dtype),
                  jax.ShapeDtypeStruct((n_pairs, cap // SC_IDX, SC_IDX), jnp.int32),
                  jax.ShapeDtypeStruct((n_pairs, cap // SC_IDX, SC_IDX), F32)),
        scratch_types=[pltpu.VMEM((seq,), jnp.int32), pltpu.VMEM((seq,), F32),
                       pltpu.VMEM((cap,), jnp.int32), pltpu.VMEM((cap // SC_IDX, SC_IDX), jnp.int32),
                       pltpu.VMEM((cap // SC_IDX, SC_IDX), F32),
                       pltpu.VMEM((2, SC_ROWS, d), x_flat.dtype), pltpu.SemaphoreType.DMA((2,))])
    def dispatch(x_hbm, slot_hbm, aff_hbm, xs_hbm, tok_hbm, gate_hbm,
                 slot_v, aff_v, idx_v, tok_v, gate_v, rows_v, sem):
        worker = lax.axis_index("s") * info.num_cores + lax.axis_index("c")
        lane = lax.iota(jnp.int32, SC_LANES)

        def gather(c0, buf):
            return pltpu.make_async_copy(x_hbm.at[idx_v.at[pl.ds(c0, SC_ROWS)]], rows_v.at[buf], sem.at[buf])
        for k in range(pairs_per_worker):
            pair = worker * pairs_per_worker + k
            row0 = (pair // N_EXPERTS) * seq
            pltpu.sync_copy(slot_hbm.at[pair], slot_v)
            pltpu.sync_copy(aff_hbm.at[pair], aff_v)

            @pl.loop(0, seq, step=SC_LANES)
            def _(t0):
                sv = slot_v[pl.ds(t0, SC_LANES)]
                picked = sv >= 0
                hi, lo = lax.shift_right_logical(sv, 7), sv & (SC_IDX - 1)
                plsc.store_scatter(tok_v, [hi, lo], t0 + lane, mask=picked)
                plsc.store_scatter(idx_v, [sv], row0 + t0 + lane, mask=picked)
                plsc.store_scatter(gate_v, [hi, lo], aff_v[pl.ds(t0, SC_LANES)], mask=picked)

            pltpu.sync_copy(tok_v, tok_hbm.at[pair])
            pltpu.sync_copy(gate_v, gate_hbm.at[pair])

            gather(0, 0).start()

            @pl.loop(0, cap, step=2 * SC_ROWS)
            def _(c0):
                gather(c0 + SC_ROWS, 1).start()
                gather(c0, 0).wait()
                pltpu.sync_copy(rows_v.at[0], xs_hbm.at[pl.ds(pair * cap + c0, SC_ROWS)])

                @pl.when(c0 + 2 * SC_ROWS < cap)
                def _():
                    gather(c0 + 2 * SC_ROWS, 0).start()

                gather(c0 + SC_ROWS, 1).wait()
                pltpu.sync_copy(rows_v.at[1], xs_hbm.at[pl.ds(pair * cap + c0 + SC_ROWS, SC_ROWS)])

    return dispatch(x_flat, slot2, aff2)


def _expert_kernel(xs_ref, g_ref, w1_ref, w3_ref, w2_ref, ye_ref, w1_bf, w3_bf, w2_bf):
    @pl.when((pl.program_id(1) == 0) & (pl.program_id(2) == 0))
    def _():
        w1_bf[...] = w1_ref[0, 0].astype(BF16)
        w3_bf[...] = w3_ref[0, 0].astype(BF16)
        w2_bf[...] = w2_ref[0, 0].astype(BF16)

    xs = jnp.concatenate(_unpack_bf16_pair(xs_ref[0, 0]), axis=1)
    hid = (jax.nn.silu(jnp.dot(xs, w1_bf[...], preferred_element_type=F32))
           * jnp.dot(xs, w3_bf[...], preferred_element_type=F32))
    g_rows = g_ref[0, 0]
    n_rows = g_rows.shape[0]
    g_t = jnp.transpose(jnp.concatenate([g_rows, jnp.zeros((LANES - n_rows, LANES), F32)], axis=0))
    g_col = jnp.concatenate([g_t[:, r:r + 1] for r in range(n_rows)], axis=0)
    ye_ref[0, 0] = jnp.dot(hid.astype(BF16), w2_bf[...], preferred_element_type=F32) * g_col


def _expert_call(xs4, gate4, w1, w3, w2, layer):
    b, e, cap, half = xs4.shape
    d, ff = w1.shape[2], w1.shape[3]
    assert d == 2 * half
    rows = lambda w: pl.BlockSpec((1, 1, CF, w), lambda e, b, j: (b, e, j, 0))
    wspec = lambda r, c: pl.BlockSpec((1, 1, r, c), lambda e, b, j: (layer, e, 0, 0))
    return pl.pallas_call(
        _expert_kernel, name="expert_ffn",
        grid=(e, b, cap // CF),
        in_specs=[rows(half), pl.BlockSpec((1, 1, CF // LANES, LANES), lambda e, b, j: (b, e, j, 0)),
                  wspec(d, ff), wspec(d, ff), wspec(ff, d)],
        out_specs=rows(d),
        out_shape=jax.ShapeDtypeStruct((b, e, cap, d), F32),
        scratch_shapes=[pltpu.VMEM((d, ff), BF16), pltpu.VMEM((d, ff), BF16), pltpu.VMEM((ff, d), BF16)],
        compiler_params=_cparams(3),
    )(xs4, gate4, w1, w3, w2)


def _sc_combine_call(ye_flat, tok3, seq):
    n_pairs, n_chunks, _ = tok3.shape
    cap = n_chunks * SC_IDX
    d = ye_flat.shape[1]
    nb = n_pairs // N_EXPERTS
    info = plsc.get_sparse_core_info()
    assert info.num_subcores == N_EXPERTS and nb % info.num_cores == 0 and n_chunks % 2 == 0
    assert seq % (info.num_subcores * SC_ZROWS) == 0 and d % SC_SLAB == 0
    batches_per_core = nb // info.num_cores
    own_rows = seq // info.num_subcores
    mesh = plsc.VectorSubcoreMesh(core_axis_name="c", subcore_axis_name="s")

    @functools.partial(
        pl.kernel, mesh=mesh, name="expert_combine_sc",
        compiler_params=pltpu.CompilerParams(needs_layout_passes=False),
        out_type=jax.ShapeDtypeStruct((nb * seq, d), F32),
        scratch_types=[pltpu.VMEM_SHARED((seq, SC_SLAB), F32),
                       pltpu.VMEM((n_chunks, SC_IDX), jnp.int32),
                       pltpu.VMEM((2, SC_IDX, SC_SLAB), F32),
                       pltpu.VMEM((SC_ZROWS, SC_SLAB), F32),
                       pltpu.SemaphoreType.DMA((2,))])
    def combine(ye_hbm, tok_hbm, out_hbm, acc_sh, tok_v, rows_v, zero_v, sem):
        core = lax.axis_index("c")
        sub = lax.axis_index("s")

        @pl.loop(0, SC_ZROWS)
        def _(r):
            for l0 in range(0, SC_SLAB, SC_LANES):
                zero_v[r, pl.ds(l0, SC_LANES)] = jnp.zeros((SC_LANES,), F32)

        for bb in range(batches_per_core):
            batch = core * batches_per_core + bb
            pair = batch * N_EXPERTS + sub
            pltpu.sync_copy(tok_hbm.at[pair], tok_v)

            @pl.loop(0, d // SC_SLAB)
            def _(slab):
                cols = pl.ds(pl.multiple_of(slab * SC_SLAB, SC_SLAB), SC_SLAB)

                @pl.loop(0, own_rows, step=SC_ZROWS)
                def _(r0):
                    pltpu.sync_copy(zero_v, acc_sh.at[pl.ds(sub * own_rows + r0, SC_ZROWS)])

                def load(j, buf):
                    return pltpu.make_async_copy(
                        ye_hbm.at[pl.ds(pair * cap + j * SC_IDX, SC_IDX), cols], rows_v.at[buf], sem.at[buf])

                load(0, 0).start()
                plsc.subcore_barrier()

                for j in range(0, n_chunks, 2):
                    load(j + 1, 1).start()
                    load(j, 0).wait()
                    pltpu.sync_copy(rows_v.at[0], acc_sh.at[tok_v.at[j]], add=True)
                    if j + 2 < n_chunks:
                        load(j + 2, 0).start()
                    load(j + 1, 1).wait()
                    pltpu.sync_copy(rows_v.at[1], acc_sh.at[tok_v.at[j + 1]], add=True)

                plsc.subcore_barrier()
                pltpu.sync_copy(acc_sh.at[pl.ds(sub * own_rows, own_rows)],
                                out_hbm.at[pl.ds(batch * seq + sub * own_rows, own_rows), cols])

    return combine(ye_flat, tok3)


TN = 512


def _resln_kernel(x_ref, y_ref, g_ref, b_ref, o_ref, *, alpha):
    o_ref[...] = _standardize(alpha * x_ref[...] + y_ref[...]) * g_ref[...] + b_ref[...]


def _resln_call(x2d, y2d, g, bta, alpha):
    n, d = x2d.shape
    tok = pl.BlockSpec((TN, d), lambda i: (i, 0))
    vec = pl.BlockSpec((1, d), lambda i: (0, 0))
    return pl.pallas_call(
        functools.partial(_resln_kernel, alpha=alpha), name="residual_layernorm",
        grid=(n // TN,), in_specs=[tok, tok, vec, vec], out_specs=tok,
        out_shape=jax.ShapeDtypeStruct((n, d), F32),
        compiler_params=_cparams(1),
    )(x2d, y2d, g, bta)


def _expert_choice_ffn(x1p, aff_t, w1, w3, w2, layer):
    b, s, half = x1p.shape
    d = 2 * half
    cap = EC_FACTOR * s // N_EXPERTS
    slot = _select_call(aff_t, cap)
    xs, tok, gate = _sc_dispatch_call(x1p.reshape(b * s, half), slot.reshape(b * N_EXPERTS, s),
                                      aff_t.reshape(b * N_EXPERTS, s), s, cap)
    ye = _expert_call(xs.reshape(b, N_EXPERTS, cap, half), gate.reshape(b, N_EXPERTS, cap // SC_IDX, SC_IDX), w1, w3, w2,
                      layer)
    out = _sc_combine_call(ye.reshape(b * N_EXPERTS * cap, d),
                           tok, s)
    return out.reshape(b, s, d)


def _pack_pool(pool_w):
    g, gd, _ = pool_w.shape
    out = jnp.zeros((g * gd, g * gd), F32)
    for i in range(g):
        out = out.at[i * gd:(i + 1) * gd, i * gd:(i + 1) * gd].set(pool_w[i])
    return out.astype(BF16)


def _layer(layer, x, pending, alpha, bias_tiles, w_in, b_in, gm_ln_g, gm_ws, gm_bs, ml_conv, ml_fbias,
           ml_norm_g, pool_w, pool_scale, w_branch, w_out, ln1_g, ln1_b, w_router, w_e1, w_e3, w_e2):
    b, s, d = x.shape
    n_small = 2576
    w_cat, b_cat = _pack_inproj_weights(w_in, b_in)
    wscat, bsfull = _pack_gmlp(gm_ws, gm_bs)
    if pending is None:
        outs = _inproj_call(x.reshape(b * s, d), w_cat, b_cat, gm_ln_g[None], wscat, bsfull, b)
    else:
        *outs, x2 = _inproj_call(pending, w_cat, b_cat, gm_ln_g[None], wscat, bsfull, b, alpha)
        x = x2.reshape(b, s, d)
    ya, qkv1, qkv4, qkv16, cqk, cv, co, dx, gates_t = outs
    r3 = lambda t: t.reshape(b, s, t.shape[-1])
    o_list, l_list = [], []
    for qkv, bias in zip((qkv1, qkv4, qkv16), bias_tiles):
        o, lse = _attn_call(qkv, bias)
        o_list.append(o)
        l_list.append(lse)
    hf, hb = _mlstm_branch(r3(cqk), r3(cv), gates_t, ml_conv, ml_fbias)
    yd = _pool_call(r3(dx), _pack_pool(pool_w), pool_scale[None])
    x1, x1p, aff_t = _merge_call(
        x, r3(ya), o_list, l_list, hf, hb, r3(co), yd,
        w_in[:, n_small:].astype(BF16), b_in[None, n_small:], w_branch.astype(BF16), w_out.astype(BF16),
        ml_norm_g[:, None], ln1_g[None], ln1_b[None], jnp.transpose(w_router).astype(BF16), alpha)
    ffn = _expert_choice_ffn(x1p, aff_t, w_e1, w_e3, w_e2, layer)
    return x1.reshape(b * s, d), ffn.reshape(b * s, d)


def kernel(x, w_in, b_in, gm_ln_g, gm_ws, gm_bs, rel_bias, ml_conv, ml_fbias, ml_norm_g, pool_w,
           pool_scale, w_branch, w_out, ln1_g, ln1_b, w_router, w_e1, w_e3, w_e2, ln2_g, ln2_b):
    depth = w_in.shape[0]
    alpha = (2 * depth) ** 0.25
    bias_tiles = [_attn_bias_tile(rel_bias, window, dil) for window, dil in DIL_PATTERNS]
    b, s, d = x.shape
    pending = None
    for l in range(depth):
        x1, ffn = _layer(l, x, pending, alpha, bias_tiles, w_in[l], b_in[l], gm_ln_g[l], gm_ws[l], gm_bs[l],
                         ml_conv[l], ml_fbias[l], ml_norm_g[l], pool_w[l], pool_scale[l], w_branch[l],
                         w_out[l], ln1_g[l], ln1_b[l], w_router[l], w_e1, w_e3, w_e2)
        pending = (x1, ffn, ln2_g[l][None], ln2_b[l][None])
    return _resln_call(*pending, alpha).reshape(b, s, d)
```

```python
import functools
import math

import jax
import jax.numpy as jnp
import numpy as np
from jax import lax
from jax.experimental import pallas as pl
from jax.experimental.pallas import tpu as pltpu
from jax.experimental.pallas import tpu_sc as plsc

F32 = jnp.float32
BF16 = jnp.bfloat16

D_MODEL = 1024
MIX_W = 256
N_BRANCH = 4
GM_CHUNK = 128
GM_GROUPS = 4
ATT_HEADS = 4
ATT_HD = 64
DIL_PATTERNS = ((128, 1), (512, 4), (2048, 16))
ATT_BLOCK = 64
REL_BUCKETS = 32
REL_MAX_DIST = 1024
ML_HEADS = 4
ML_HD = 64
ML_CHUNK = 64
POOL_WINDOWS = (2, 4, 8, 16)
N_EXPERTS = 16
EXPERT_FF = 1024
EC_FACTOR = 2
LN_EPS = 1e-5
NEG_BIG = -1e30

V7X_VMEM_LIMIT = 56 * 1024 * 1024
LANES = 128
HALO = 8


def _cparams(n_grid, vmem=V7X_VMEM_LIMIT):
    return pltpu.CompilerParams(dimension_semantics=("arbitrary",) * n_grid,
                                vmem_limit_bytes=vmem)


def _pack_bf16_pair(lo, hi):
    lo_bits = lax.shift_right_logical(pltpu.bitcast(lo.astype(F32), jnp.int32), 16)
    return pltpu.bitcast(hi.astype(F32), jnp.int32) | lo_bits


def _unpack_bf16_pair(packed):
    lo = pltpu.bitcast(lax.shift_left(packed, 16), F32).astype(BF16)
    hi = pltpu.bitcast(packed & jnp.int32(-65536), F32).astype(BF16)
    return lo, hi


def _standardize(xf):
    mu = jnp.mean(xf, axis=-1, keepdims=True)
    var = jnp.mean(jnp.square(xf - mu), axis=-1, keepdims=True)
    return (xf - mu) * lax.rsqrt(var + LN_EPS)


TA = 512
A_COLS = 2560 + LANES


def _inproj_kernel(*refs, alpha):
    if alpha is None:
        x_ref, *refs = refs
        x = x_ref[...]
    else:
        x1_ref, y_ref, g2_ref, b2_ref, *refs = refs
        x = _standardize(alpha * x1_ref[...] + y_ref[...]) * g2_ref[...] + b2_ref[...]
        refs[-2][...] = x
        refs = refs[:-2] + refs[-1:]
    (w_ref, b_ref, lng_ref, wscat_ref, bsfull_ref,
     ya_ref, qkv1_ref, qkv4_ref, qkv16_ref, cqk_ref, cv_ref, co_ref, dx_ref, gt_ref, qkv_scr) = refs
    xb = x.astype(BF16)
    h = jnp.dot(xb, w_ref[...], preferred_element_type=F32) + b_ref[...]
    qkv1_ref[0, 0] = h[:, 512:1280].astype(BF16)
    for c in range(768 // LANES):
        qkv_scr[c] = h[:, 512 + c * LANES:512 + (c + 1) * LANES]
    for (_, dil), out_ref in zip(DIL_PATTERNS[1:], (qkv4_ref, qkv16_ref)):
        for r in range(dil):
            for c in range(768 // LANES):
                out_ref[0, r, :, c * LANES:(c + 1) * LANES] = (
                    qkv_scr[c, pl.ds(r, TA // dil, stride=dil), :].astype(BF16))
    cqk_ref[...] = h[:, 1280:1792]
    cv_ref[...] = h[:, 1792:2048].astype(BF16)
    co_ref[...] = h[:, 2048:2304]
    dx_ref[...] = h[:, 2304:2560]
    gates_t = jnp.transpose(h[:, 2560:2688])
    for j in range(TA // LANES):
        gt_ref[j] = gates_t[0:4 * ML_HEADS, j * LANES:(j + 1) * LANES]
    u = jax.nn.gelu(h[:, 0:256])
    v = jax.nn.gelu(h[:, 256:512])
    vn = _standardize(v) * lng_ref[...]
    lane_grp = lax.broadcasted_iota(jnp.int32, (GM_CHUNK, MIX_W), 1) // (MIX_W // GM_GROUPS)
    for c in range(TA // GM_CHUNK):
        vc = vn[c * GM_CHUNK:(c + 1) * GM_CHUNK]
        stacked = jnp.concatenate(
            [jnp.where(lane_grp == g, vc, 0.0).astype(BF16) for g in range(GM_GROUPS)], axis=0)
        mixed = jnp.dot(wscat_ref[...], stacked, preferred_element_type=F32) + bsfull_ref[...]
        ya_ref[c * GM_CHUNK:(c + 1) * GM_CHUNK, :] = (
            u[c * GM_CHUNK:(c + 1) * GM_CHUNK] * mixed).astype(BF16)


def _inproj_call(x_in, w_cat, b_cat, lng, wscat, bsfull, batch, alpha=None):
    fused = alpha is not None
    n = (x_in[0] if fused else x_in).shape[0]
    seq = n // batch
    tpb = seq // TA
    tok = lambda w: pl.BlockSpec((TA, w), lambda i: (i, 0))
    const = lambda s: pl.BlockSpec(s, lambda i: (0,) * len(s))
    regrouped = lambda dil: pl.BlockSpec((1, dil, TA // dil, 768), lambda i: (i // tpb, 0, i % tpb, 0))
    out_shape = (
        jax.ShapeDtypeStruct((n, 256), BF16),
    ) + tuple(jax.ShapeDtypeStruct((batch, dil, seq // dil, 768), BF16)
              for _, dil in DIL_PATTERNS) + (
        jax.ShapeDtypeStruct((n, 512), F32),
        jax.ShapeDtypeStruct((n, 256), BF16),
        jax.ShapeDtypeStruct((n, 256), F32),
        jax.ShapeDtypeStruct((n, 256), F32),
        jax.ShapeDtypeStruct((n // LANES, 4 * ML_HEADS, LANES), F32),
    )
    x_specs = [tok(D_MODEL), tok(D_MODEL), const((1, D_MODEL)), const((1, D_MODEL))] if fused else [tok(D_MODEL)]
    out_specs = ((tok(256),) + tuple(regrouped(dil) for _, dil in DIL_PATTERNS)
                 + (tok(512), tok(256), tok(256), tok(256),
                    pl.BlockSpec((TA // LANES, 4 * ML_HEADS, LANES), lambda i: (i, 0, 0))))
    if fused:
        out_specs += (tok(D_MODEL),)
        out_shape += (jax.ShapeDtypeStruct((n, D_MODEL), F32),)
    return pl.pallas_call(
        functools.partial(_inproj_kernel, alpha=alpha), name="inproj_gmlp",
        grid=(n // TA,),
        in_specs=x_specs + [const((D_MODEL, A_COLS)), const((1, A_COLS)), const((1, MIX_W)),
                            const((GM_CHUNK, GM_GROUPS * GM_CHUNK)), const((GM_CHUNK, MIX_W))],
        out_specs=out_specs,
        out_shape=out_shape,
        scratch_shapes=[pltpu.VMEM((768 // LANES, TA, LANES), F32)],
        compiler_params=_cparams(1),
    )(*(x_in if fused else (x_in,)), w_cat, b_cat, lng, wscat, bsfull)


def _pack_inproj_weights(w_in, b_in):
    pad = lambda a: jnp.pad(a, ((0, 0), (0, LANES - 4 * ML_HEADS)))
    w_cat = jnp.concatenate([w_in[:, 0:2304], w_in[:, 2320:2576], pad(w_in[:, 2304:2320])], axis=1)
    b2 = b_in[None, :]
    b_cat = jnp.concatenate([b2[:, 0:2304], b2[:, 2320:2576], pad(b2[:, 2304:2320])], axis=1)
    return w_cat.astype(BF16), b_cat


def _pack_gmlp(gm_ws, gm_bs):
    wscat = jnp.transpose(gm_ws, (1, 0, 2)).reshape(GM_CHUNK, GM_GROUPS * GM_CHUNK).astype(BF16)
    bsfull = jnp.repeat(jnp.transpose(gm_bs), MIX_W // GM_GROUPS, axis=1)
    return wscat, bsfull


def _halo_specs(t, width, n_tiles):
    r = t // HALO
    main = pl.BlockSpec((1, t, width), lambda b, i: (b, i, 0))
    prev = pl.BlockSpec((1, HALO, width), lambda b, i: (b, jnp.maximum(i * r - 1, 0), 0))
    nxt = pl.BlockSpec((1, HALO, width), lambda b, i: (b, jnp.minimum((i + 1) * r, n_tiles * r - 1), 0))
    return main, prev, nxt


def _fill_halo_scratch(buf, x_ref, p_ref, n_ref, t):
    i = pl.program_id(1)
    last = pl.num_programs(1) - 1
    buf[0:HALO, :] = jnp.where(i > 0, p_ref[0], 0.0)
    buf[HALO:HALO + t, :] = x_ref[0]
    buf[HALO + t:2 * HALO + t, :] = jnp.where(i < last, n_ref[0], 0.0)


TP = 512


def _pool_kernel(x_ref, p_ref, n_ref, w_ref, sc_ref, o_ref, buf, lvl):
    _fill_halo_scratch(buf, x_ref, p_ref, n_ref, TP)
    seq = pl.num_programs(1) * TP
    pos = pl.program_id(1) * TP + lax.broadcasted_iota(jnp.int32, (TP, 1), 0)
    lane_grp = lax.broadcasted_iota(jnp.int32, (TP, MIX_W), 1) // (MIX_W // len(POOL_WINDOWS))
    x0 = buf[HALO:HALO + TP, :]
    sums = []
    src, rows = buf, TP + 2 * HALO
    for k, win in enumerate(POOL_WINDOWS):
        half = win // 2
        rows -= half
        cur = src[0:rows, :] + src[half:rows + half, :] if k else buf[0:rows, :] + buf[1:rows + 1, :]
        if k + 1 < len(POOL_WINDOWS):
            lvl[k, 0:rows, :] = cur
            sums.append(lvl[k, HALO - half:HALO - half + TP, :])
            src = lvl.at[k]
        else:
            sums.append(cur[0:TP])
    pooled = jnp.zeros((TP, MIX_W), F32)
    for gi, win in enumerate(POOL_WINDOWS):
        half = win // 2
        cnt = (jnp.minimum(pos + half, seq) - jnp.maximum(pos - half, 0)).astype(F32)
        pooled = jnp.where(lane_grp == gi, sums[gi] / cnt - x0, pooled)
    mixed = jnp.dot(pooled.astype(BF16), w_ref[...], preferred_element_type=F32)
    o_ref[0] = (mixed * sc_ref[...]).astype(BF16)


def _pool_call(dx, w_block, scale):
    b, s, _ = dx.shape
    nt = s // TP
    main, prev, nxt = _halo_specs(TP, MIX_W, nt)
    return pl.pallas_call(
        _pool_kernel, name="pool_mixer",
        grid=(b, nt),
        in_specs=[main, prev, nxt,
                  pl.BlockSpec((MIX_W, MIX_W), lambda b, i: (0, 0)),
                  pl.BlockSpec((1, MIX_W), lambda b, i: (0, 0))],
        out_specs=pl.BlockSpec((1, TP, MIX_W), lambda b, i: (b, i, 0)),
        out_shape=jax.ShapeDtypeStruct((b, s, MIX_W), BF16),
        scratch_shapes=[pltpu.VMEM((TP + 2 * HALO, MIX_W), F32),
                        pltpu.VMEM((len(POOL_WINDOWS) - 1, TP + 2 * HALO, MIX_W), F32)],
        compiler_params=_cparams(2),
    )(dx, dx, dx, w_block, scale)


TQ = 128
TQS = 512
TKEYS = TQ + 2 * ATT_BLOCK


def _attn_kernel(q_ref, kp_ref, km_ref, kn_ref, vp_ref, vm_ref, vn_ref, bias_ref, o_ref, lse_ref):
    i = pl.program_id(2)
    q = q_ref[0, 0] * ATT_HD ** -0.5
    k = jnp.concatenate([kp_ref[0, 0], km_ref[0, 0], kn_ref[0, 0]], axis=0)
    v = jnp.concatenate([vp_ref[0, 0], vm_ref[0, 0], vn_ref[0, 0]], axis=0)
    lane = lax.broadcasted_iota(jnp.int32, (TQ, LANES), 1)
    lane_half = lax.broadcasted_iota(jnp.int32, (1, LANES), 1) // ATT_HD
    keep = [jnp.where(lane_half == hh, 1.0, 0.0).astype(BF16) for hh in range(2)]
    n_sub = TQS // TQ
    last_step = pl.num_programs(2) - 1
    for j in range(n_sub):
        if j == 0:
            variant = jnp.where(i == 0, 0, 1)
        elif j == n_sub - 1:
            variant = jnp.where(i == last_step, 2, 1)
        else:
            variant = 1
        qrows = slice(j * TQ, (j + 1) * TQ)
        krows = slice(j * TQ, j * TQ + TKEYS)
        lse_tile = jnp.zeros((TQ, LANES), F32)
        for pair in range(ATT_HEADS // 2):
            grp = slice(pair * LANES, (pair + 1) * LANES)
            q_pair, k_pair, v_pair = q[qrows, grp], k[krows, grp], v[krows, grp]
            o_pair = jnp.zeros((TQ, LANES), F32)
            for hh in range(2):
                h = 2 * pair + hh
                logits = lax.dot_general(q_pair * keep[hh], k_pair, (((1,), (1,)), ((), ())),
                                         preferred_element_type=F32) + bias_ref[variant, h]
                m = jnp.max(logits, axis=-1, keepdims=True)
                p = jnp.exp(logits - m)
                ssum = jnp.sum(p, axis=-1, keepdims=True)
                o = jnp.dot(p.astype(BF16), v_pair, preferred_element_type=F32) / ssum
                o_pair = jnp.where(lane_half == hh, o, o_pair)
                lse_tile = jnp.where(lane == h, m + jnp.log(ssum), lse_tile)
            o_ref[0, 0, qrows, grp] = o_pair
        lse_ref[0, 0, qrows, :] = lse_tile


def _attn_call(qkv, bias):
    b, dil, l, _ = qkv.shape
    nt = l // TQS
    r64 = TQS // ATT_BLOCK
    main = lambda c: pl.BlockSpec((1, 1, TQS, MIX_W), lambda b, r, i: (b, r, i, c))
    prev = lambda c: pl.BlockSpec((1, 1, ATT_BLOCK, MIX_W),
                                  lambda b, r, i: (b, r, jnp.maximum(i * r64 - 1, 0), c))
    nxt = lambda c: pl.BlockSpec((1, 1, ATT_BLOCK, MIX_W),
                                 lambda b, r, i: (b, r, jnp.minimum((i + 1) * r64, nt * r64 - 1), c))
    return pl.pallas_call(
        _attn_kernel, name="band_attention",
        grid=(b, dil, nt),
        in_specs=[main(0), prev(1), main(1), nxt(1), prev(2), main(2), nxt(2),
                  pl.BlockSpec((3, ATT_HEADS, TQ, TKEYS), lambda b, r, i: (0, 0, 0, 0))],
        out_specs=(pl.BlockSpec((1, 1, TQS, MIX_W), lambda b, r, i: (b, r, i, 0)),
                   pl.BlockSpec((1, 1, TQS, LANES), lambda b, r, i: (b, r, i, 0))),
        out_shape=(jax.ShapeDtypeStruct((b, dil, l, MIX_W), F32),
                   jax.ShapeDtypeStruct((b, dil, l, LANES), F32)),
        compiler_params=_cparams(3),
    )(qkv, qkv, qkv, qkv, qkv, qkv, qkv, bias)


def _t5_bucket_static(rel):
    half = REL_BUCKETS // 2
    max_exact = half // 2
    ret = np.where(rel > 0, half, 0)
    n = np.abs(rel)
    nf = np.maximum(n, 1).astype(np.float32)
    large = max_exact + (np.log(nf / np.float32(max_exact)) / np.float32(math.log(REL_MAX_DIST / max_exact))
                         * np.float32(half - max_exact)).astype(np.int32)
    large = np.minimum(large, half - 1)
    return ret + np.where(n < max_exact, n, large)


def _attn_bias_tile(rel_bias, window, dil):
    side = (window // 2) // dil
    rel = np.arange(TKEYS)[None, :] - ATT_BLOCK - np.arange(TQ)[:, None]
    onehot = jax.nn.one_hot(jnp.asarray(_t5_bucket_static(dil * rel), jnp.int32), REL_BUCKETS, dtype=F32)
    bias = jnp.einsum('qkr,rh->hqk', onehot, rel_bias, precision=lax.Precision.HIGHEST)
    key = np.arange(TKEYS)[None, :]
    inside = np.abs(rel) <= side
    masks = np.stack([inside & (key >= ATT_BLOCK), inside, inside & (key < ATT_BLOCK + TQ)])
    return jnp.where(jnp.asarray(masks)[:, None], bias[None], NEG_BIG)


TM = 512
VT_ROWS = ML_HD + 16


def _mlprep_kernel(x_ref, p_ref, n_ref, v_ref, w_ref, qt_out, k_out, vt_out, buf):
    _fill_halo_scratch(buf, x_ref, p_ref, n_ref, TM)
    conv = (buf[HALO - 1:HALO - 1 + TM, :] * w_ref[0:1, :] + buf[HALO:HALO + TM, :] * w_ref[1:2, :]
            + buf[HALO + 1:HALO + 1 + TM, :] * w_ref[2:3, :])
    qk = jax.nn.silu(conv)
    qt = jnp.transpose(qk[:, :MIX_W])
    vt = jnp.transpose(v_ref[0].astype(F32))
    ones_rows = jnp.where(lax.broadcasted_iota(jnp.int32, (VT_ROWS - ML_HD, LANES), 0) == 0, 1.0, 0.0)
    for h in range(ML_HEADS):
        sl = slice(h * ML_HD, (h + 1) * ML_HD)
        k_out[0, h] = (qk[:, MIX_W + h * ML_HD:MIX_W + (h + 1) * ML_HD] * ML_HD ** -0.5).astype(BF16)
        for p in range(TM // LANES):
            pl_ = slice(p * LANES, (p + 1) * LANES)
            qt_out[0, h, p] = qt[sl, pl_].astype(BF16)
            vt_out[0, h, p] = jnp.concatenate([vt[sl, pl_], ones_rows], axis=0).astype(BF16)


def _mlprep_call(cqk, cv, conv_w):
    b, s, _ = cqk.shape
    nt = s // TM
    n_pairs = s // LANES
    ppt = TM // LANES
    main, prev, nxt = _halo_specs(TM, 2 * MIX_W, nt)
    return pl.pallas_call(
        _mlprep_kernel, name="mlstm_prep",
        grid=(b, nt),
        in_specs=[main, prev, nxt,
                  pl.BlockSpec((1, TM, MIX_W), lambda b, i: (b, i, 0)),
                  pl.BlockSpec((3, 2 * MIX_W), lambda b, i: (0, 0))],
        out_specs=(pl.BlockSpec((1, ML_HEADS, ppt, ML_HD, LANES), lambda b, i: (b, 0, i, 0, 0)),
                   pl.BlockSpec((1, ML_HEADS, TM, ML_HD), lambda b, i: (b, 0, i, 0)),
                   pl.BlockSpec((1, ML_HEADS, ppt, VT_ROWS, LANES), lambda b, i: (b, 0, i, 0, 0))),
        out_shape=(jax.ShapeDtypeStruct((b, ML_HEADS, n_pairs, ML_HD, LANES), BF16),
                   jax.ShapeDtypeStruct((b, ML_HEADS, s, ML_HD), BF16),
                   jax.ShapeDtypeStruct((b, ML_HEADS, n_pairs, VT_ROWS, LANES), BF16)),
        scratch_shapes=[pltpu.VMEM((TM + 2 * HALO, 2 * MIX_W), F32)],
        compiler_params=_cparams(2),
    )(cqk, cqk, cqk, cv, conv_w)


def _gate_kernel(g_ref, fb_ref, a_ref, m_ref, iw_ref, en_ref, ws_ref, dec_ref):
    x = g_ref[...]
    n_pairs = x.shape[0]
    n_ch = 2 * ML_HEADS
    lane = lax.broadcasted_iota(jnp.int32, (1, 1, LANES), 2)
    t_in = lane % ML_CHUNK
    second = lane >= ML_CHUNK
    fwd_row = lax.broadcasted_iota(jnp.int32, (1, n_ch, 1), 1) < ML_HEADS
    li = x[:, 0:n_ch, :]
    z = x[:, n_ch:2 * n_ch, :] + fb_ref[...]
    lf = jnp.minimum(z, 0.0) - jnp.log1p(jnp.exp(-jnp.abs(z)))

    def within_chunk(v, op, ident, prefix):
        s = 1
        while s < ML_CHUNK:
            nb = pltpu.roll(v, s if prefix else LANES - s, 2)
            ok = (t_in >= s) if prefix else (t_in < ML_CHUNK - s)
            v = op(v, jnp.where(ok, nb, ident))
            s *= 2
        return v

    pre = within_chunk(lf, jnp.add, 0.0, True)
    suf = within_chunk(lf, jnp.add, 0.0, False)
    g = pre + suf - lf
    b = jnp.where(fwd_row, pre, suf)
    a = li - b
    cm_pre = within_chunk(a, jnp.maximum, -jnp.inf, True)
    cm_suf = within_chunk(a, jnp.maximum, -jnp.inf, False)
    cm = jnp.where(fwd_row, cm_pre, cm_suf)
    amax = jnp.maximum(cm_pre, cm_suf)

    def shift_pairs(v, k, fill):
        pad = jnp.full((abs(k),) + v.shape[1:], fill, F32)
        return (jnp.concatenate([pad, v[:n_pairs - k]], axis=0) if k > 0
                else jnp.concatenate([v[-k:], pad], axis=0))

    def from_chunk(v, dist, fill, forward):
        if dist == 1:
            y = pltpu.roll(v, ML_CHUNK, 2)
            if forward:
                return jnp.where(second, y, shift_pairs(y, 1, fill))
            return jnp.where(second, shift_pairs(y, -1, fill), y)
        return shift_pairs(v, dist // 2 if forward else -(dist // 2), fill)

    def running_stabiliser(forward):
        big_g, big_a = g, amax + g
        dist = 1
        while dist < 2 * n_pairs:
            gp = from_chunk(big_g, dist, 0.0, forward)
            ap = from_chunk(big_a, dist, -jnp.inf, forward)
            big_g, big_a = gp + big_g, jnp.maximum(ap + big_g, big_a)
            dist *= 2
        ge = from_chunk(big_g, 1, 0.0, forward)
        ae = from_chunk(big_a, 1, -jnp.inf, forward)
        return jnp.maximum(ge, ae)

    m_chunk = jnp.where(fwd_row, running_stabiliser(True), running_stabiliser(False))
    m_t = jnp.maximum(cm, m_chunk)
    m_last = jnp.maximum(amax, m_chunk)
    a_ref[0] = a
    m_ref[0] = m_t
    iw_ref[0] = jnp.exp(m_chunk - m_t)
    en_ref[0] = jnp.exp(-(b + m_t))
    ws_ref[0] = jnp.exp(a - m_last)
    dec_ref[0] = jnp.exp(m_chunk - m_last)


def _gate_call(gates_t, fbias_col, batch):
    n_pairs = gates_t.shape[0] // batch
    n_ch = 2 * ML_HEADS
    out = pl.BlockSpec((1, n_pairs, n_ch, LANES), lambda i: (i, 0, 0, 0))
    return pl.pallas_call(
        _gate_kernel, name="mlstm_gates",
        grid=(batch,),
        in_specs=[pl.BlockSpec((n_pairs, 2 * n_ch, LANES), lambda i: (i, 0, 0)),
                  pl.BlockSpec((n_ch, 1), lambda i: (0, 0))],
        out_specs=(out,) * 6,
        out_shape=(jax.ShapeDtypeStruct((batch, n_pairs, n_ch, LANES), F32),) * 6,
        compiler_params=_cparams(1),
    )(gates_t, fbias_col)


TE = 1024


def _mlstm_kernel(*refs):
    fwd, bwd, (hf_ref, hb_ref, state) = refs[:9], refs[9:18], refs[18:]
    i = pl.program_id(1)

    @pl.when(i == 0)
    def _():
        state[...] = jnp.zeros(state.shape, F32)

    n_pairs = TE // LANES
    s_idx = lax.broadcasted_iota(jnp.int32, (LANES, LANES), 0)
    t_idx = lax.broadcasted_iota(jnp.int32, (LANES, LANES), 1)
    same_chunk = (s_idx >= ML_CHUNK) == (t_idx >= ML_CHUNK)
    upper_lanes = lax.broadcasted_iota(jnp.int32, (1, LANES), 1) >= ML_CHUNK

    def pair_body(p, carry):
        jobs = []
        for d, ((qt_r, k_r, vt_r, a_r, m_r, iw_r, en_r, ws_r, dec_r), out_r) in enumerate(
                ((fwd, hf_ref), (bwd, hb_ref))):
            pp = p if d == 0 else n_pairs - 1 - p
            srows = pl.ds(pl.multiple_of(pp * LANES, LANES), LANES)
            a_t = jnp.transpose(jnp.concatenate(
                [a_r[0, pp], jnp.zeros((LANES - 2 * ML_HEADS, LANES), F32)], axis=0))
            m_t, iw_t, en_t, ws_t, dec_t = [r[0, pp] for r in (m_r, iw_r, en_r, ws_r, dec_r)]
            dec_lo, dec_hi = dec_t[:, :ML_CHUNK], pltpu.roll(dec_t, ML_CHUNK, 1)[:, :ML_CHUNK]
            in_first = upper_lanes if d else ~upper_lanes
            tri = same_chunk & ((s_idx >= t_idx) if d else (s_idx <= t_idx))
            for h in range(ML_HEADS):
                ch = d * ML_HEADS + h
                row = lambda t: t[ch:ch + 1]
                jobs.append(dict(
                    ch=ch, tri=tri, in_first=in_first, k=k_r[0, h, srows, :], qt=qt_r[0, h, pp],
                    vt=vt_r[0, h, pp], a=a_t[:, ch:ch + 1], m=row(m_t), iw=row(iw_t), en=row(en_t), ws=row(ws_t),
                    dec_first=row(dec_hi if d else dec_lo), dec_second=row(dec_lo if d else dec_hi),
                    out=(out_r, pp, h)))
        for j in jobs:
            vt_f = j["vt"].astype(F32)
            j["c0"] = state[j["ch"]]
            j["st"] = jnp.dot(j["k"], j["qt"], preferred_element_type=F32)
            j["inter1"] = jnp.dot(j["c0"].astype(BF16), j["qt"], preferred_element_type=F32)
            j["upd1"] = jnp.dot((vt_f * jnp.where(j["in_first"], j["ws"], 0.0)).astype(BF16), j["k"],
                                preferred_element_type=F32)
            j["upd2"] = jnp.dot((vt_f * jnp.where(j["in_first"], 0.0, j["ws"])).astype(BF16), j["k"],
                                preferred_element_type=F32)
        for j in jobs:
            j["swt"] = j["st"] * jnp.exp(jnp.where(j["tri"], j["a"] - j["m"], NEG_BIG))
            j["intra"] = jnp.dot(j["vt"], j["swt"].astype(BF16), preferred_element_type=F32)
            j["c1"] = j["dec_first"] * j["c0"] + j["upd1"]
            j["inter2"] = jnp.dot(j["c1"].astype(BF16), j["qt"], preferred_element_type=F32)
        for j in jobs:
            inter = jnp.where(j["in_first"], j["inter1"], j["inter2"])
            den = jnp.sum(j["swt"], axis=0, keepdims=True) + j["iw"] * inter[ML_HD:ML_HD + 1]
            tot = j["intra"][:ML_HD] + j["iw"] * inter[:ML_HD]
            out_r, pp, h = j["out"]
            out_r[0, pp, h * ML_HD:(h + 1) * ML_HD, :] = tot / jnp.maximum(jnp.abs(den), j["en"])
            state[j["ch"]] = j["dec_second"] * j["c1"] + j["upd2"]
        return carry

    lax.fori_loop(0, n_pairs, pair_body, 0)


def _mlstm_call(qt, k, vt, a_t, m_t, iw_t, en_t, ws_t, dec_t):
    b, _, s, _ = k.shape
    nt = s // TE
    ppt = TE // LANES

    def specs(rev):
        ti = (lambda i: nt - 1 - i) if rev else (lambda i: i)
        tile = pl.BlockSpec((1, ppt, 2 * ML_HEADS, LANES), lambda b, i: (b, ti(i), 0, 0))
        return [
            pl.BlockSpec((1, ML_HEADS, ppt, ML_HD, LANES), lambda b, i: (b, 0, ti(i), 0, 0)),
            pl.BlockSpec((1, ML_HEADS, TE, ML_HD), lambda b, i: (b, 0, ti(i), 0)),
            pl.BlockSpec((1, ML_HEADS, ppt, VT_ROWS, LANES), lambda b, i: (b, 0, ti(i), 0, 0)),
            tile, tile, tile, tile, tile, tile]

    args = [qt, k, vt, a_t, m_t, iw_t, en_t, ws_t, dec_t]
    out_f = pl.BlockSpec((1, ppt, MIX_W, LANES), lambda b, i: (b, i, 0, 0))
    out_b = pl.BlockSpec((1, ppt, MIX_W, LANES), lambda b, i: (b, nt - 1 - i, 0, 0))
    return pl.pallas_call(
        _mlstm_kernel, name="mlstm_scan",
        grid=(b, nt),
        in_specs=specs(False) + specs(True),
        out_specs=(out_f, out_b),
        out_shape=(jax.ShapeDtypeStruct((b, s // LANES, MIX_W, LANES), F32),) * 2,
        scratch_shapes=[pltpu.VMEM((2 * ML_HEADS, VT_ROWS, ML_HD), F32)],
        compiler_params=_cparams(2),
    )(*args, *args)


def _mlstm_branch(cqk, cv, gates_t, conv_w, fbias):
    qt, k, vt = _mlprep_call(cqk, cv, conv_w)
    factors = _gate_call(gates_t, fbias.reshape(2 * ML_HEADS, 1), cqk.shape[0])
    return _mlstm_call(qt, k, vt, *factors)


TF = 512


def _merge_kernel(x_ref, ya_ref, o1_ref, o2_ref, o3_ref, l1_ref, l2_ref, l3_ref, hf_ref, hb_ref,
                  co_ref, yd_ref, wg_ref, bg_ref, wbr_ref, wout_ref, mng_ref, lng_ref, lnb_ref,
                  wr_ref, x1_ref, x1p_ref, aff_ref, o_scr, l_scr, *, alpha):
    x = x_ref[0]
    xb = x.astype(BF16)

    def natural_order(src_ref, scr):
        dil, width = src_ref.shape[1], src_ref.shape[3]
        if dil == 1:
            return src_ref[0, 0]
        for r in range(dil):
            for c in range(width // LANES):
                scr[c, pl.ds(r, TF // dil, stride=dil), :] = src_ref[0, r, :, c * LANES:(c + 1) * LANES]
        return jnp.concatenate([scr[c] for c in range(width // LANES)], axis=1)

    lane_head = lax.broadcasted_iota(jnp.int32, (TF, MIX_W), 1) // ML_HD
    l1, l2, l3 = [natural_order(r, l_scr.at[p]) for p, r in enumerate((l1_ref, l2_ref, l3_ref))]
    o1, o2, o3 = [natural_order(r, o_scr.at[p]) for p, r in enumerate((o1_ref, o2_ref, o3_ref))]
    lm = jnp.maximum(jnp.maximum(l1, l2), l3)
    e1, e2, e3 = jnp.exp(l1 - lm), jnp.exp(l2 - lm), jnp.exp(l3 - lm)
    inv = 1.0 / (e1 + e2 + e3)

    def per_head(w):
        out = jnp.zeros((TF, MIX_W), F32)
        for h in range(ATT_HEADS):
            out = jnp.where(lane_head == h, w[:, h:h + 1], out)
        return out

    y_b = per_head(e1 * inv) * o1 + per_head(e2 * inv) * o2 + per_head(e3 * inv) * o3
    hsum_t = jnp.concatenate([hf_ref[0, p] + hb_ref[0, p] for p in range(TF // LANES)], axis=1)
    per_head_rows = hsum_t.reshape(ML_HEADS, ML_HD, TF)
    mu = jnp.mean(per_head_rows, axis=1, keepdims=True)
    cen = per_head_rows - mu
    var = jnp.mean(cen * cen, axis=1, keepdims=True)
    hn_t = (cen * lax.rsqrt(var + LN_EPS)).reshape(MIX_W, TF)
    y_c_t = (jax.nn.sigmoid(jnp.transpose(co_ref[0])) * (hn_t * mng_ref[...])).astype(BF16)
    ys = (ya_ref[0], y_b.astype(BF16), None, yd_ref[0])
    merged = jnp.zeros((TF, D_MODEL), F32)
    for n in range(N_BRANCH):
        cols = slice(n * D_MODEL, (n + 1) * D_MODEL)
        gate = jax.nn.sigmoid(jnp.dot(xb, wg_ref[:, cols], preferred_element_type=F32) + bg_ref[:, cols])
        if ys[n] is None:
            proj = lax.dot_general(y_c_t, wbr_ref[n], (((0,), (0,)), ((), ())), preferred_element_type=F32)
        else:
            proj = jnp.dot(ys[n], wbr_ref[n], preferred_element_type=F32)
        merged = merged + gate * proj
    mix = jnp.dot(merged.astype(BF16), wout_ref[...], preferred_element_type=F32)
    x1 = _standardize(alpha * x + mix) * lng_ref[...] + lnb_ref[...]
    x1_ref[0] = x1
    x1b = x1.astype(BF16)
    x1p_ref[0] = _pack_bf16_pair(x1b[:, :D_MODEL // 2], x1b[:, D_MODEL // 2:])
    logits = lax.dot_general(wr_ref[...], x1b, (((1,), (1,)), ((), ())),
                             preferred_element_type=F32)
    ex = jnp.exp(logits - jnp.max(logits, axis=0, keepdims=True))
    aff_ref[0] = ex / jnp.sum(ex, axis=0, keepdims=True)


def _merge_call(x, ya, o_list, l_list, hf, hb, co, yd, wg, bg, wbr, wout, mng, lng, lnb, wr_t, alpha):
    b, s, _ = x.shape
    tok = lambda w: pl.BlockSpec((1, TF, w), lambda b, i: (b, i, 0))
    grouped = lambda dil, w: pl.BlockSpec((1, dil, TF // dil, w), lambda b, i: (b, 0, i, 0))
    chunked = pl.BlockSpec((1, TF // LANES, MIX_W, LANES), lambda b, i: (b, i, 0, 0))
    const = lambda shp: pl.BlockSpec(shp, lambda b, i: (0,) * len(shp))
    return pl.pallas_call(
        functools.partial(_merge_kernel, alpha=alpha), name="merge_ln_router",
        grid=(b, s // TF),
        in_specs=[tok(D_MODEL), tok(MIX_W)] + [grouped(dil, MIX_W) for _, dil in DIL_PATTERNS]
                 + [grouped(dil, LANES) for _, dil in DIL_PATTERNS]
                 + [chunked, chunked, tok(MIX_W), tok(MIX_W)]
                 + [const((D_MODEL, N_BRANCH * D_MODEL)), const((1, N_BRANCH * D_MODEL)),
                    const((N_BRANCH, MIX_W, D_MODEL)), const((D_MODEL, D_MODEL)), const((MIX_W, 1)),
                    const((1, D_MODEL)), const((1, D_MODEL)), const((N_EXPERTS, D_MODEL))],
        out_specs=(tok(D_MODEL), tok(D_MODEL // 2), pl.BlockSpec((1, N_EXPERTS, TF), lambda b, i: (b, 0, i))),
        out_shape=(jax.ShapeDtypeStruct((b, s, D_MODEL), F32),
                   jax.ShapeDtypeStruct((b, s, D_MODEL // 2), jnp.int32),
                   jax.ShapeDtypeStruct((b, N_EXPERTS, s), F32)),
        scratch_shapes=[pltpu.VMEM((len(DIL_PATTERNS), MIX_W // LANES, TF, LANES), F32),
                        pltpu.VMEM((len(DIL_PATTERNS), 1, TF, LANES), F32)],
        compiler_params=_cparams(2),
    )(x, ya, *o_list, *l_list, hf, hb, co, yd, wg, bg, wbr, wout, mng, lng, lnb, wr_t)


TT = 256


def _select_kernel(aff_ref, slot_ref, *, cap):
    s = aff_ref.shape[2]
    bits = pltpu.bitcast(aff_ref[0], jnp.int32)

    def bit_step(i, thr):
        cand = thr | jnp.left_shift(jnp.int32(1), 30 - i)
        cnt = jnp.sum((bits >= cand).astype(jnp.int32), axis=1, keepdims=True)
        return jnp.where(cnt >= cap, cand, thr)

    thr = lax.fori_loop(0, 31, bit_step, jnp.zeros((N_EXPERTS, 1), jnp.int32))
    gt = bits > thr
    eq = bits == thr
    need = (cap - jnp.sum(gt.astype(jnp.int32), axis=1, keepdims=True)).astype(F32)
    upper = (lax.broadcasted_iota(jnp.int32, (TT, TT), 0)
             <= lax.broadcasted_iota(jnp.int32, (TT, TT), 1)).astype(BF16)
    eq_before = jnp.zeros((N_EXPERTS, 1), F32)
    sel_before = jnp.zeros((N_EXPERTS, 1), F32)
    for j in range(s // TT):
        cols = slice(j * TT, (j + 1) * TT)
        eq_j = eq[:, cols]
        eq_incl = eq_before + jnp.dot(eq_j.astype(BF16), upper, preferred_element_type=F32)
        sel_j = gt[:, cols] | (eq_j & (eq_incl <= need))
        sel_f = sel_j.astype(F32)
        sel_incl = sel_before + jnp.dot(sel_f.astype(BF16), upper, preferred_element_type=F32)
        slot_ref[0, :, cols] = jnp.where(sel_j, sel_incl - 1.0, -1.0).astype(jnp.int32)
        eq_before = eq_incl[:, TT - 1:TT]
        sel_before = sel_incl[:, TT - 1:TT]


def _select_call(aff_t, cap):
    b, e, s = aff_t.shape
    return pl.pallas_call(
        functools.partial(_select_kernel, cap=cap), name="expert_choice_select",
        grid=(b,),
        in_specs=[pl.BlockSpec((1, e, s), lambda i: (i, 0, 0))],
        out_specs=pl.BlockSpec((1, e, s), lambda i: (i, 0, 0)),
        out_shape=jax.ShapeDtypeStruct((b, e, s), jnp.int32),
        compiler_params=_cparams(1),
    )(aff_t)


SC_LANES = 16
SC_ROWS = 64
SC_IDX = 128
SC_SLAB = 128
SC_ZROWS = 64
CF = 1024


def _sc_dispatch_call(x_flat, slot2, aff2, seq, cap):
    n_pairs = slot2.shape[0]
    d = x_flat.shape[1]
    info = plsc.get_sparse_core_info()
    n_workers = info.num_cores * info.num_subcores
    assert n_pairs % n_workers == 0 and seq % SC_LANES == 0 and cap % (2 * SC_ROWS) == 0
    pairs_per_worker = n_pairs // n_workers
    mesh = plsc.VectorSubcoreMesh(core_axis_name="c", subcore_axis_name="s")

    @functools.partial(
        pl.kernel, mesh=mesh, name="expert_dispatch_sc",
        compiler_params=pltpu.CompilerParams(needs_layout_passes=False),
        out_type=(jax.ShapeDtypeStruct((n_pairs * cap, d), x_flat.dtype),
                  jax.ShapeDtypeStruct((n_pairs, cap // SC_IDX, SC_IDX), jnp.int32),
                  jax.ShapeDtypeStruct((n_pairs, cap // SC_IDX, SC_IDX), F32)),
        scratch_types=[pltpu.VMEM((seq,), jnp.int32), pltpu.VMEM((seq,), F32),
                       pltpu.VMEM((cap,), jnp.int32), pltpu.VMEM((cap // SC_IDX, SC_IDX), jnp.int32),
                       pltpu.VMEM((cap // SC_IDX, SC_IDX), F32),
                       pltpu.VMEM((2, SC_ROWS, d), x_flat.dtype), pltpu.SemaphoreType.DMA((2,))])
    def dispatch(x_hbm, slot_hbm, aff_hbm, xs_hbm, tok_hbm, gate_hbm,
                 slot_v, aff_v, idx_v, tok_v, gate_v, rows_v, sem):
        worker = lax.axis_index("s") * info.num_cores + lax.axis_index("c")
        lane = lax.iota(jnp.int32, SC_LANES)

        def gather(c0, buf):
            return pltpu.make_async_copy(x_hbm.at[idx_v.at[pl.ds(c0, SC_ROWS)]], rows_v.at[buf], sem.at[buf])
        for k in range(pairs_per_worker):
            pair = worker * pairs_per_worker + k
            row0 = (pair // N_EXPERTS) * seq
            pltpu.sync_copy(slot_hbm.at[pair], slot_v)
            pltpu.sync_copy(aff_hbm.at[pair], aff_v)

            @pl.loop(0, seq, step=SC_LANES)
            def _(t0):
                sv = slot_v[pl.ds(t0, SC_LANES)]
                picked = sv >= 0
                hi, lo = lax.shift_right_logical(sv, 7), sv & (SC_IDX - 1)
                plsc.store_scatter(tok_v, [hi, lo], t0 + lane, mask=picked)
                plsc.store_scatter(idx_v, [sv], row0 + t0 + lane, mask=picked)
                plsc.store_scatter(gate_v, [hi, lo], aff_v[pl.ds(t0, SC_LANES)], mask=picked)

            pltpu.sync_copy(tok_v, tok_hbm.at[pair])
            pltpu.sync_copy(gate_v, gate_hbm.at[pair])

            gather(0, 0).start()

            @pl.loop(0, cap, step=2 * SC_ROWS)
            def _(c0):
                gather(c0 + SC_ROWS, 1).start()
                gather(c0, 0).wait()
                pltpu.sync_copy(rows_v.at[0], xs_hbm.at[pl.ds(pair * cap + c0, SC_ROWS)])

                @pl.when(c0 + 2 * SC_ROWS < cap)
                def _():
                    gather(c0 + 2 * SC_ROWS, 0).start()

                gather(c0 + SC_ROWS, 1).wait()
                pltpu.sync_copy(rows_v.at[1], xs_hbm.at[pl.ds(pair * cap + c0 + SC_ROWS, SC_ROWS)])

    return dispatch(x_flat, slot2, aff2)


def _expert_kernel(xs_ref, g_ref, w1_ref, w3_ref, w2_ref, ye_ref, w1_bf, w3_bf, w2_bf):
    @pl.when((pl.program_id(1) == 0) & (pl.program_id(2) == 0))
    def _():
        w1_bf[...] = w1_ref[0, 0].astype(BF16)
        w3_bf[...] = w3_ref[0, 0].astype(BF16)
        w2_bf[...] = w2_ref[0, 0].astype(BF16)

    xs = jnp.concatenate(_unpack_bf16_pair(xs_ref[0, 0]), axis=1)
    hid = (jax.nn.silu(jnp.dot(xs, w1_bf[...], preferred_element_type=F32))
           * jnp.dot(xs, w3_bf[...], preferred_element_type=F32))
    g_rows = g_ref[0, 0]
    n_rows = g_rows.shape[0]
    g_t = jnp.transpose(jnp.concatenate([g_rows, jnp.zeros((LANES - n_rows, LANES), F32)], axis=0))
    g_col = jnp.concatenate([g_t[:, r:r + 1] for r in range(n_rows)], axis=0)
    ye_ref[0, 0] = jnp.dot(hid.astype(BF16), w2_bf[...], preferred_element_type=F32) * g_col


def _expert_call(xs4, gate4, w1, w3, w2, layer):
    b, e, cap, half = xs4.shape
    d, ff = w1.shape[2], w1.shape[3]
    assert d == 2 * half
    rows = lambda w: pl.BlockSpec((1, 1, CF, w), lambda e, b, j: (b, e, j, 0))
    wspec = lambda r, c: pl.BlockSpec((1, 1, r, c), lambda e, b, j: (layer, e, 0, 0))
    return pl.pallas_call(
        _expert_kernel, name="expert_ffn",
        grid=(e, b, cap // CF),
        in_specs=[rows(half), pl.BlockSpec((1, 1, CF // LANES, LANES), lambda e, b, j: (b, e, j, 0)),
                  wspec(d, ff), wspec(d, ff), wspec(ff, d)],
        out_specs=rows(d),
        out_shape=jax.ShapeDtypeStruct((b, e, cap, d), F32),
        scratch_shapes=[pltpu.VMEM((d, ff), BF16), pltpu.VMEM((d, ff), BF16), pltpu.VMEM((ff, d), BF16)],
        compiler_params=_cparams(3),
    )(xs4, gate4, w1, w3, w2)


def _sc_combine_call(ye_flat, tok3, seq):
    n_pairs, n_chunks, _ = tok3.shape
    cap = n_chunks * SC_IDX
    d = ye_flat.shape[1]
    nb = n_pairs // N_EXPERTS
    info = plsc.get_sparse_core_info()
    assert info.num_subcores == N_EXPERTS and nb % info.num_cores == 0 and n_chunks % 2 == 0
    assert seq % (info.num_subcores * SC_ZROWS) == 0 and d % SC_SLAB == 0
    batches_per_core = nb // info.num_cores
    own_rows = seq // info.num_subcores
    mesh = plsc.VectorSubcoreMesh(core_axis_name="c", subcore_axis_name="s")

    @functools.partial(
        pl.kernel, mesh=mesh, name="expert_combine_sc",
        compiler_params=pltpu.CompilerParams(needs_layout_passes=False),
        out_type=jax.ShapeDtypeStruct((nb * seq, d), F32),
        scratch_types=[pltpu.VMEM_SHARED((seq, SC_SLAB), F32),
                       pltpu.VMEM((n_chunks, SC_IDX), jnp.int32),
                       pltpu.VMEM((2, SC_IDX, SC_SLAB), F32),
                       pltpu.VMEM((SC_ZROWS, SC_SLAB), F32),
                       pltpu.SemaphoreType.DMA((2,))])
    def combine(ye_hbm, tok_hbm, out_hbm, acc_sh, tok_v, rows_v, zero_v, sem):
        core = lax.axis_index("c")
        sub = lax.axis_index("s")

        @pl.loop(0, SC_ZROWS)
        def _(r):
            for l0 in range(0, SC_SLAB, SC_LANES):
                zero_v[r, pl.ds(l0, SC_LANES)] = jnp.zeros((SC_LANES,), F32)

        for bb in range(batches_per_core):
            batch = core * batches_per_core + bb
            pair = batch * N_EXPERTS + sub
            pltpu.sync_copy(tok_hbm.at[pair], tok_v)

            @pl.loop(0, d // SC_SLAB)
            def _(slab):
                cols = pl.ds(pl.multiple_of(slab * SC_SLAB, SC_SLAB), SC_SLAB)

                @pl.loop(0, own_rows, step=SC_ZROWS)
                def _(r0):
                    pltpu.sync_copy(zero_v, acc_sh.at[pl.ds(sub * own_rows + r0, SC_ZROWS)])

                def load(j, buf):
                    return pltpu.make_async_copy(
                        ye_hbm.at[pl.ds(pair * cap + j * SC_IDX, SC_IDX), cols], rows_v.at[buf], sem.at[buf])

                load(0, 0).start()
                plsc.subcore_barrier()

                for j in range(0, n_chunks, 2):
                    load(j + 1, 1).start()
                    load(j, 0).wait()
                    pltpu.sync_copy(rows_v.at[0], acc_sh.at[tok_v.at[j]], add=True)
                    if j + 2 < n_chunks:
                        load(j + 2, 0).start()
                    load(j + 1, 1).wait()
                    pltpu.sync_copy(rows_v.at[1], acc_sh.at[tok_v.at[j + 1]], add=True)

                plsc.subcore_barrier()
                pltpu.sync_copy(acc_sh.at[pl.ds(sub * own_rows, own_rows)],
                                out_hbm.at[pl.ds(batch * seq + sub * own_rows, own_rows), cols])

    return combine(ye_flat, tok3)


TN = 512


def _resln_kernel(x_ref, y_ref, g_ref, b_ref, o_ref, *, alpha):
    o_ref[...] = _standardize(alpha * x_ref[...] + y_ref[...]) * g_ref[...] + b_ref[...]


def _resln_call(x2d, y2d, g, bta, alpha):
    n, d = x2d.shape
    tok = pl.BlockSpec((TN, d), lambda i: (i, 0))
    vec = pl.BlockSpec((1, d), lambda i: (0, 0))
    return pl.pallas_call(
        functools.partial(_resln_kernel, alpha=alpha), name="residual_layernorm",
        grid=(n // TN,), in_specs=[tok, tok, vec, vec], out_specs=tok,
        out_shape=jax.ShapeDtypeStruct((n, d), F32),
        compiler_params=_cparams(1),
    )(x2d, y2d, g, bta)


def _expert_choice_ffn(x1p, aff_t, w1, w3, w2, layer):
    b, s, half = x1p.shape
    d = 2 * half
    cap = EC_FACTOR * s // N_EXPERTS
    slot = _select_call(aff_t, cap)
    xs, tok, gate = _sc_dispatch_call(x1p.reshape(b * s, half), slot.reshape(b * N_EXPERTS, s),
                                      aff_t.reshape(b * N_EXPERTS, s), s, cap)
    ye = _expert_call(xs.reshape(b, N_EXPERTS, cap, half), gate.reshape(b, N_EXPERTS, cap // SC_IDX, SC_IDX), w1, w3, w2,
                      layer)
    out = _sc_combine_call(ye.reshape(b * N_EXPERTS * cap, d),
                           tok, s)
    return out.reshape(b, s, d)


def _pack_pool(pool_w):
    g, gd, _ = pool_w.shape
    out = jnp.zeros((g * gd, g * gd), F32)
    for i in range(g):
        out = out.at[i * gd:(i + 1) * gd, i * gd:(i + 1) * gd].set(pool_w[i])
    return out.astype(BF16)


def _layer(layer, x, pending, alpha, bias_tiles, w_in, b_in, gm_ln_g, gm_ws, gm_bs, ml_conv, ml_fbias,
           ml_norm_g, pool_w, pool_scale, w_branch, w_out, ln1_g, ln1_b, w_router, w_e1, w_e3, w_e2):
    b, s, d = x.shape
    n_small = 2576
    w_cat, b_cat = _pack_inproj_weights(w_in, b_in)
    wscat, bsfull = _pack_gmlp(gm_ws, gm_bs)
    if pending is None:
        outs = _inproj_call(x.reshape(b * s, d), w_cat, b_cat, gm_ln_g[None], wscat, bsfull, b)
    else:
        *outs, x2 = _inproj_call(pending, w_cat, b_cat, gm_ln_g[None], wscat, bsfull, b, alpha)
        x = x2.reshape(b, s, d)
    ya, qkv1, qkv4, qkv16, cqk, cv, co, dx, gates_t = outs
    r3 = lambda t: t.reshape(b, s, t.shape[-1])
    o_list, l_list = [], []
    for qkv, bias in zip((qkv1, qkv4, qkv16), bias_tiles):
        o, lse = _attn_call(qkv, bias)
        o_list.append(o)
        l_list.append(lse)
    hf, hb = _mlstm_branch(r3(cqk), r3(cv), gates_t, ml_conv, ml_fbias)
    yd = _pool_call(r3(dx), _pack_pool(pool_w), pool_scale[None])
    x1, x1p, aff_t = _merge_call(
        x, r3(ya), o_list, l_list, hf, hb, r3(co), yd,
        w_in[:, n_small:].astype(BF16), b_in[None, n_small:], w_branch.astype(BF16), w_out.astype(BF16),
        ml_norm_g[:, None], ln1_g[None], ln1_b[None], jnp.transpose(w_router).astype(BF16), alpha)
    ffn = _expert_choice_ffn(x1p, aff_t, w_e1, w_e3, w_e2, layer)
    return x1.reshape(b * s, d), ffn.reshape(b * s, d)


def kernel(x, w_in, b_in, gm_ln_g, gm_ws, gm_bs, rel_bias, ml_conv, ml_fbias, ml_norm_g, pool_w,
           pool_scale, w_branch, w_out, ln1_g, ln1_b, w_router, w_e1, w_e3, w_e2, ln2_g, ln2_b):
    depth = w_in.shape[0]
    alpha = (2 * depth) ** 0.25
    bias_tiles = [_attn_bias_tile(rel_bias, window, dil) for window, dil in DIL_PATTERNS]
    b, s, d = x.shape
    pending = None
    for l in range(depth):
        x1, ffn = _layer(l, x, pending, alpha, bias_tiles, w_in[l], b_in[l], gm_ln_g[l], gm_ws[l], gm_bs[l],
                         ml_conv[l], ml_fbias[l], ml_norm_g[l], pool_w[l], pool_scale[l], w_branch[l],
                         w_out[l], ln1_g[l], ln1_b[l], w_router[l], w_e1, w_e3, w_e2)
        pending = (x1, ffn, ln2_g[l][None], ln2_b[l][None])
    return _resln_call(*pending, alpha).reshape(b, s, d)
```

```python
import functools
import math

import jax
import jax.numpy as jnp
import numpy as np
from jax import lax
from jax.experimental import pallas as pl
from jax.experimental.pallas import tpu as pltpu
from jax.experimental.pallas import tpu_sc as plsc

F32 = jnp.float32
BF16 = jnp.bfloat16

D_MODEL = 1024
MIX_W = 256
N_BRANCH = 4
GM_CHUNK = 128
GM_GROUPS = 4
ATT_HEADS = 4
ATT_HD = 64
DIL_PATTERNS = ((128, 1), (512, 4), (2048, 16))
ATT_BLOCK = 64
REL_BUCKETS = 32
REL_MAX_DIST = 1024
ML_HEADS = 4
ML_HD = 64
ML_CHUNK = 64
POOL_WINDOWS = (2, 4, 8, 16)
N_EXPERTS = 16
EXPERT_FF = 1024
EC_FACTOR = 2
LN_EPS = 1e-5
NEG_BIG = -1e30

V7X_VMEM_LIMIT = 56 * 1024 * 1024
LANES = 128
HALO = 8


def _cparams(n_grid, vmem=V7X_VMEM_LIMIT):
    return pltpu.CompilerParams(dimension_semantics=("arbitrary",) * n_grid,
                                vmem_limit_bytes=vmem)


def _pack_bf16_pair(lo, hi):
    lo_bits = lax.shift_right_logical(pltpu.bitcast(lo.astype(F32), jnp.int32), 16)
    return pltpu.bitcast(hi.astype(F32), jnp.int32) | lo_bits


def _unpack_bf16_pair(packed):
    lo = pltpu.bitcast(lax.shift_left(packed, 16), F32).astype(BF16)
    hi = pltpu.bitcast(packed & jnp.int32(-65536), F32).astype(BF16)
    return lo, hi


def _standardize(xf):
    mu = jnp.mean(xf, axis=-1, keepdims=True)
    var = jnp.mean(jnp.square(xf - mu), axis=-1, keepdims=True)
    return (xf - mu) * lax.rsqrt(var + LN_EPS)


TA = 512
A_COLS = 2560 + LANES


def _inproj_kernel(*refs, alpha):
    if alpha is None:
        x_ref, *refs = refs
        x = x_ref[...]
    else:
        x1_ref, y_ref, g2_ref, b2_ref, *refs = refs
        x = _standardize(alpha * x1_ref[...] + y_ref[...]) * g2_ref[...] + b2_ref[...]
        refs[-2][...] = x
        refs = refs[:-2] + refs[-1:]
    (w_ref, b_ref, lng_ref, wscat_ref, bsfull_ref,
     ya_ref, qkv1_ref, qkv4_ref, qkv16_ref, cqk_ref, cv_ref, co_ref, dx_ref, gt_ref, qkv_scr) = refs
    xb = x.astype(BF16)
    h = jnp.dot(xb, w_ref[...], preferred_element_type=F32) + b_ref[...]
    qkv1_ref[0, 0] = h[:, 512:1280].astype(BF16)
    for c in range(768 // LANES):
        qkv_scr[c] = h[:, 512 + c * LANES:512 + (c + 1) * LANES]
    for (_, dil), out_ref in zip(DIL_PATTERNS[1:], (qkv4_ref, qkv16_ref)):
        for r in range(dil):
            for c in range(768 // LANES):
                out_ref[0, r, :, c * LANES:(c + 1) * LANES] = (
                    qkv_scr[c, pl.ds(r, TA // dil, stride=dil), :].astype(BF16))
    cqk_ref[...] = h[:, 1280:1792]
    cv_ref[...] = h[:, 1792:2048].astype(BF16)
    co_ref[...] = h[:, 2048:2304]
    n_gates = 4 * ML_HEADS
    dx_ref[...] = h[:, 2304 + n_gates:2560 + n_gates]
    gates_t = jnp.transpose(h[:, 2304:2304 + LANES])
    for j in range(TA // LANES):
        gt_ref[j] = gates_t[0:4 * ML_HEADS, j * LANES:(j + 1) * LANES]
    u = jax.nn.gelu(h[:, 0:256])
    v = jax.nn.gelu(h[:, 256:512])
    vn = _standardize(v) * lng_ref[...]
    lane_grp = lax.broadcasted_iota(jnp.int32, (GM_CHUNK, MIX_W), 1) // (MIX_W // GM_GROUPS)
    for c in range(TA // GM_CHUNK):
        vc = vn[c * GM_CHUNK:(c + 1) * GM_CHUNK]
        stacked = jnp.concatenate(
            [jnp.where(lane_grp == g, vc, 0.0).astype(BF16) for g in range(GM_GROUPS)], axis=0)
        mixed = jnp.dot(wscat_ref[...], stacked, preferred_element_type=F32) + bsfull_ref[...]
        ya_ref[c * GM_CHUNK:(c + 1) * GM_CHUNK, :] = (
            u[c * GM_CHUNK:(c + 1) * GM_CHUNK] * mixed).astype(BF16)


def _inproj_call(x_in, w_cat, b_cat, lng, wscat, bsfull, batch, alpha=None):
    fused = alpha is not None
    n = (x_in[0] if fused else x_in).shape[0]
    seq = n // batch
    tpb = seq // TA
    tok = lambda w: pl.BlockSpec((TA, w), lambda i: (i, 0))
    const = lambda s: pl.BlockSpec(s, lambda i: (0,) * len(s))
    regrouped = lambda dil: pl.BlockSpec((1, dil, TA // dil, 768), lambda i: (i // tpb, 0, i % tpb, 0))
    out_shape = (
        jax.ShapeDtypeStruct((n, 256), BF16),
    ) + tuple(jax.ShapeDtypeStruct((batch, dil, seq // dil, 768), BF16)
              for _, dil in DIL_PATTERNS) + (
        jax.ShapeDtypeStruct((n, 512), F32),
        jax.ShapeDtypeStruct((n, 256), BF16),
        jax.ShapeDtypeStruct((n, 256), F32),
        jax.ShapeDtypeStruct((n, 256), F32),
        jax.ShapeDtypeStruct((n // LANES, 4 * ML_HEADS, LANES), F32),
    )
    x_specs = [tok(D_MODEL), tok(D_MODEL), const((1, D_MODEL)), const((1, D_MODEL))] if fused else [tok(D_MODEL)]
    out_specs = ((tok(256),) + tuple(regrouped(dil) for _, dil in DIL_PATTERNS)
                 + (tok(512), tok(256), tok(256), tok(256),
                    pl.BlockSpec((TA // LANES, 4 * ML_HEADS, LANES), lambda i: (i, 0, 0))))
    if fused:
        out_specs += (tok(D_MODEL),)
        out_shape += (jax.ShapeDtypeStruct((n, D_MODEL), F32),)
    return pl.pallas_call(
        functools.partial(_inproj_kernel, alpha=alpha), name="inproj_gmlp",
        grid=(n // TA,),
        in_specs=x_specs + [const((D_MODEL, A_COLS)), const((1, A_COLS)), const((1, MIX_W)),
                            const((GM_CHUNK, GM_GROUPS * GM_CHUNK)), const((GM_CHUNK, MIX_W))],
        out_specs=out_specs,
        out_shape=out_shape,
        scratch_shapes=[pltpu.VMEM((768 // LANES, TA, LANES), F32)],
        compiler_params=_cparams(1),
    )(*(x_in if fused else (x_in,)), w_cat, b_cat, lng, wscat, bsfull)


def _pack_inproj_weights(w_in, b_in):
    return w_in[:, :A_COLS].astype(BF16), b_in[None, :A_COLS]


def _pack_gmlp(gm_ws, gm_bs):
    wscat = jnp.transpose(gm_ws, (1, 0, 2)).reshape(GM_CHUNK, GM_GROUPS * GM_CHUNK).astype(BF16)
    bsfull = jnp.repeat(jnp.transpose(gm_bs), MIX_W // GM_GROUPS, axis=1)
    return wscat, bsfull


def _halo_specs(t, width, n_tiles):
    r = t // HALO
    main = pl.BlockSpec((1, t, width), lambda b, i: (b, i, 0))
    prev = pl.BlockSpec((1, HALO, width), lambda b, i: (b, jnp.maximum(i * r - 1, 0), 0))
    nxt = pl.BlockSpec((1, HALO, width), lambda b, i: (b, jnp.minimum((i + 1) * r, n_tiles * r - 1), 0))
    return main, prev, nxt


def _fill_halo_scratch(buf, x_ref, p_ref, n_ref, t):
    i = pl.program_id(1)
    last = pl.num_programs(1) - 1
    buf[0:HALO, :] = jnp.where(i > 0, p_ref[0], 0.0)
    buf[HALO:HALO + t, :] = x_ref[0]
    buf[HALO + t:2 * HALO + t, :] = jnp.where(i < last, n_ref[0], 0.0)


TP = 512


def _pool_kernel(x_ref, p_ref, n_ref, w_ref, sc_ref, o_ref, buf, lvl):
    _fill_halo_scratch(buf, x_ref, p_ref, n_ref, TP)
    seq = pl.num_programs(1) * TP
    pos = pl.program_id(1) * TP + lax.broadcasted_iota(jnp.int32, (TP, 1), 0)
    lane_grp = lax.broadcasted_iota(jnp.int32, (TP, MIX_W), 1) // (MIX_W // len(POOL_WINDOWS))
    x0 = buf[HALO:HALO + TP, :]
    sums = []
    src, rows = buf, TP + 2 * HALO
    for k, win in enumerate(POOL_WINDOWS):
        half = win // 2
        rows -= half
        cur = src[0:rows, :] + src[half:rows + half, :] if k else buf[0:rows, :] + buf[1:rows + 1, :]
        if k + 1 < len(POOL_WINDOWS):
            lvl[k, 0:rows, :] = cur
            sums.append(lvl[k, HALO - half:HALO - half + TP, :])
            src = lvl.at[k]
        else:
            sums.append(cur[0:TP])
    pooled = jnp.zeros((TP, MIX_W), F32)
    for gi, win in enumerate(POOL_WINDOWS):
        half = win // 2
        cnt = (jnp.minimum(pos + half, seq) - jnp.maximum(pos - half, 0)).astype(F32)
        pooled = jnp.where(lane_grp == gi, sums[gi] / cnt - x0, pooled)
    mixed = jnp.dot(pooled.astype(BF16), w_ref[...], preferred_element_type=F32)
    o_ref[0] = (mixed * sc_ref[...]).astype(BF16)


def _pool_call(dx, w_block, scale):
    b, s, _ = dx.shape
    nt = s // TP
    main, prev, nxt = _halo_specs(TP, MIX_W, nt)
    return pl.pallas_call(
        _pool_kernel, name="pool_mixer",
        grid=(b, nt),
        in_specs=[main, prev, nxt,
                  pl.BlockSpec((MIX_W, MIX_W), lambda b, i: (0, 0)),
                  pl.BlockSpec((1, MIX_W), lambda b, i: (0, 0))],
        out_specs=pl.BlockSpec((1, TP, MIX_W), lambda b, i: (b, i, 0)),
        out_shape=jax.ShapeDtypeStruct((b, s, MIX_W), BF16),
        scratch_shapes=[pltpu.VMEM((TP + 2 * HALO, MIX_W), F32),
                        pltpu.VMEM((len(POOL_WINDOWS) - 1, TP + 2 * HALO, MIX_W), F32)],
        compiler_params=_cparams(2),
    )(dx, dx, dx, w_block, scale)


TQ = 128
TQS = 512
TKEYS = TQ + 2 * ATT_BLOCK


def _attn_kernel(q_ref, kp_ref, km_ref, kn_ref, vp_ref, vm_ref, vn_ref, bias_ref, o_ref, lse_ref):
    i = pl.program_id(2)
    q = q_ref[0, 0] * ATT_HD ** -0.5
    k = jnp.concatenate([kp_ref[0, 0], km_ref[0, 0], kn_ref[0, 0]], axis=0)
    v = jnp.concatenate([vp_ref[0, 0], vm_ref[0, 0], vn_ref[0, 0]], axis=0)
    lane = lax.broadcasted_iota(jnp.int32, (TQ, LANES), 1)
    lane_half = lax.broadcasted_iota(jnp.int32, (1, LANES), 1) // ATT_HD
    keep = [jnp.where(lane_half == hh, 1.0, 0.0).astype(BF16) for hh in range(2)]
    n_sub = TQS // TQ
    last_step = pl.num_programs(2) - 1
    for j in range(n_sub):
        if j == 0:
            variant = jnp.where(i == 0, 0, 1)
        elif j == n_sub - 1:
            variant = jnp.where(i == last_step, 2, 1)
        else:
            variant = 1
        qrows = slice(j * TQ, (j + 1) * TQ)
        krows = slice(j * TQ, j * TQ + TKEYS)
        lse_tile = jnp.zeros((TQ, LANES), F32)
        for pair in range(ATT_HEADS // 2):
            grp = slice(pair * LANES, (pair + 1) * LANES)
            q_pair, k_pair, v_pair = q[qrows, grp], k[krows, grp], v[krows, grp]
            o_pair = jnp.zeros((TQ, LANES), F32)
            for hh in range(2):
                h = 2 * pair + hh
                logits = lax.dot_general(q_pair * keep[hh], k_pair, (((1,), (1,)), ((), ())),
                                         preferred_element_type=F32) + bias_ref[variant, h]
                m = jnp.max(logits, axis=-1, keepdims=True)
                p = jnp.exp(logits - m)
                ssum = jnp.sum(p, axis=-1, keepdims=True)
                o = jnp.dot(p.astype(BF16), v_pair, preferred_element_type=F32) / ssum
                o_pair = jnp.where(lane_half == hh, o, o_pair)
                lse_tile = jnp.where(lane == h, m + jnp.log(ssum), lse_tile)
            o_ref[0, 0, qrows, grp] = o_pair
        lse_ref[0, 0, qrows, :] = lse_tile


def _attn_call(qkv, bias):
    b, dil, l, _ = qkv.shape
    nt = l // TQS
    r64 = TQS // ATT_BLOCK
    main = lambda c: pl.BlockSpec((1, 1, TQS, MIX_W), lambda b, r, i: (b, r, i, c))
    prev = lambda c: pl.BlockSpec((1, 1, ATT_BLOCK, MIX_W),
                                  lambda b, r, i: (b, r, jnp.maximum(i * r64 - 1, 0), c))
    nxt = lambda c: pl.BlockSpec((1, 1, ATT_BLOCK, MIX_W),
                                 lambda b, r, i: (b, r, jnp.minimum((i + 1) * r64, nt * r64 - 1), c))
    return pl.pallas_call(
        _attn_kernel, name="band_attention",
        grid=(b, dil, nt),
        in_specs=[main(0), prev(1), main(1), nxt(1), prev(2), main(2), nxt(2),
                  pl.BlockSpec((3, ATT_HEADS, TQ, TKEYS), lambda b, r, i: (0, 0, 0, 0))],
        out_specs=(pl.BlockSpec((1, 1, TQS, MIX_W), lambda b, r, i: (b, r, i, 0)),
                   pl.BlockSpec((1, 1, TQS, LANES), lambda b, r, i: (b, r, i, 0))),
        out_shape=(jax.ShapeDtypeStruct((b, dil, l, MIX_W), F32),
                   jax.ShapeDtypeStruct((b, dil, l, LANES), F32)),
        compiler_params=_cparams(3),
    )(qkv, qkv, qkv, qkv, qkv, qkv, qkv, bias)


def _t5_bucket_static(rel):
    half = REL_BUCKETS // 2
    max_exact = half // 2
    ret = np.where(rel > 0, half, 0)
    n = np.abs(rel)
    nf = np.maximum(n, 1).astype(np.float32)
    large = max_exact + (np.log(nf / np.float32(max_exact)) / np.float32(math.log(REL_MAX_DIST / max_exact))
                         * np.float32(half - max_exact)).astype(np.int32)
    large = np.minimum(large, half - 1)
    return ret + np.where(n < max_exact, n, large)


def _attn_bias_tile(rel_bias, window, dil):
    side = (window // 2) // dil
    rel = np.arange(TKEYS)[None, :] - ATT_BLOCK - np.arange(TQ)[:, None]
    onehot = jax.nn.one_hot(jnp.asarray(_t5_bucket_static(dil * rel), jnp.int32), REL_BUCKETS, dtype=F32)
    bias = jnp.einsum('qkr,rh->hqk', onehot, rel_bias, precision=lax.Precision.HIGHEST)
    key = np.arange(TKEYS)[None, :]
    inside = np.abs(rel) <= side
    masks = np.stack([inside & (key >= ATT_BLOCK), inside, inside & (key < ATT_BLOCK + TQ)])
    return jnp.where(jnp.asarray(masks)[:, None], bias[None], NEG_BIG)


TM = 512
VT_ROWS = ML_HD + 16


def _mlprep_kernel(x_ref, p_ref, n_ref, v_ref, w_ref, qt_out, k_out, vt_out, buf):
    _fill_halo_scratch(buf, x_ref, p_ref, n_ref, TM)
    conv = (buf[HALO - 1:HALO - 1 + TM, :] * w_ref[0:1, :] + buf[HALO:HALO + TM, :] * w_ref[1:2, :]
            + buf[HALO + 1:HALO + 1 + TM, :] * w_ref[2:3, :])
    qk = jax.nn.silu(conv)
    qt = jnp.transpose(qk[:, :MIX_W])
    vt = jnp.transpose(v_ref[0].astype(F32))
    ones_rows = jnp.where(lax.broadcasted_iota(jnp.int32, (VT_ROWS - ML_HD, LANES), 0) == 0, 1.0, 0.0)
    for h in range(ML_HEADS):
        sl = slice(h * ML_HD, (h + 1) * ML_HD)
        k_out[0, h] = (qk[:, MIX_W + h * ML_HD:MIX_W + (h + 1) * ML_HD] * ML_HD ** -0.5).astype(BF16)
        for p in range(TM // LANES):
            pl_ = slice(p * LANES, (p + 1) * LANES)
            qt_out[0, h, p] = qt[sl, pl_].astype(BF16)
            vt_out[0, h, p] = jnp.concatenate([vt[sl, pl_], ones_rows], axis=0).astype(BF16)


def _mlprep_call(cqk, cv, conv_w):
    b, s, _ = cqk.shape
    nt = s // TM
    n_pairs = s // LANES
    ppt = TM // LANES
    main, prev, nxt = _halo_specs(TM, 2 * MIX_W, nt)
    return pl.pallas_call(
        _mlprep_kernel, name="mlstm_prep",
        grid=(b, nt),
        in_specs=[main, prev, nxt,
                  pl.BlockSpec((1, TM, MIX_W), lambda b, i: (b, i, 0)),
                  pl.BlockSpec((3, 2 * MIX_W), lambda b, i: (0, 0))],
        out_specs=(pl.BlockSpec((1, ML_HEADS, ppt, ML_HD, LANES), lambda b, i: (b, 0, i, 0, 0)),
                   pl.BlockSpec((1, ML_HEADS, TM, ML_HD), lambda b, i: (b, 0, i, 0)),
                   pl.BlockSpec((1, ML_HEADS, ppt, VT_ROWS, LANES), lambda b, i: (b, 0, i, 0, 0))),
        out_shape=(jax.ShapeDtypeStruct((b, ML_HEADS, n_pairs, ML_HD, LANES), BF16),
                   jax.ShapeDtypeStruct((b, ML_HEADS, s, ML_HD), BF16),
                   jax.ShapeDtypeStruct((b, ML_HEADS, n_pairs, VT_ROWS, LANES), BF16)),
        scratch_shapes=[pltpu.VMEM((TM + 2 * HALO, 2 * MIX_W), F32)],
        compiler_params=_cparams(2),
    )(cqk, cqk, cqk, cv, conv_w)


def _gate_kernel(g_ref, fb_ref, a_ref, m_ref, iw_ref, en_ref, ws_ref, dec_ref):
    x = g_ref[...]
    n_pairs = x.shape[0]
    n_ch = 2 * ML_HEADS
    lane = lax.broadcasted_iota(jnp.int32, (1, 1, LANES), 2)
    t_in = lane % ML_CHUNK
    second = lane >= ML_CHUNK
    fwd_row = lax.broadcasted_iota(jnp.int32, (1, n_ch, 1), 1) < ML_HEADS
    li = x[:, 0:n_ch, :]
    z = x[:, n_ch:2 * n_ch, :] + fb_ref[...]
    lf = jnp.minimum(z, 0.0) - jnp.log1p(jnp.exp(-jnp.abs(z)))

    def within_chunk(v, op, ident, prefix):
        s = 1
        while s < ML_CHUNK:
            nb = pltpu.roll(v, s if prefix else LANES - s, 2)
            ok = (t_in >= s) if prefix else (t_in < ML_CHUNK - s)
            v = op(v, jnp.where(ok, nb, ident))
            s *= 2
        return v

    pre = within_chunk(lf, jnp.add, 0.0, True)
    suf = within_chunk(lf, jnp.add, 0.0, False)
    g = pre + suf - lf
    b = jnp.where(fwd_row, pre, suf)
    a = li - b
    cm_pre = within_chunk(a, jnp.maximum, -jnp.inf, True)
    cm_suf = within_chunk(a, jnp.maximum, -jnp.inf, False)
    cm = jnp.where(fwd_row, cm_pre, cm_suf)
    amax = jnp.maximum(cm_pre, cm_suf)

    def shift_pairs(v, k, fill):
        pad = jnp.full((abs(k),) + v.shape[1:], fill, F32)
        return (jnp.concatenate([pad, v[:n_pairs - k]], axis=0) if k > 0
                else jnp.concatenate([v[-k:], pad], axis=0))

    def from_chunk(v, dist, fill, forward):
        if dist == 1:
            y = pltpu.roll(v, ML_CHUNK, 2)
            if forward:
                return jnp.where(second, y, shift_pairs(y, 1, fill))
            return jnp.where(second, shift_pairs(y, -1, fill), y)
        return shift_pairs(v, dist // 2 if forward else -(dist // 2), fill)

    def running_stabiliser(forward):
        big_g, big_a = g, amax + g
        dist = 1
        while dist < 2 * n_pairs:
            gp = from_chunk(big_g, dist, 0.0, forward)
            ap = from_chunk(big_a, dist, -jnp.inf, forward)
            big_g, big_a = gp + big_g, jnp.maximum(ap + big_g, big_a)
            dist *= 2
        ge = from_chunk(big_g, 1, 0.0, forward)
        ae = from_chunk(big_a, 1, -jnp.inf, forward)
        return jnp.maximum(ge, ae)

    m_chunk = jnp.where(fwd_row, running_stabiliser(True), running_stabiliser(False))
    m_t = jnp.maximum(cm, m_chunk)
    m_last = jnp.maximum(amax, m_chunk)
    a_ref[0] = a
    m_ref[0] = m_t
    iw_ref[0] = jnp.exp(m_chunk - m_t)
    en_ref[0] = jnp.exp(-(b + m_t))
    ws_ref[0] = jnp.exp(a - m_last)
    dec_ref[0] = jnp.exp(m_chunk - m_last)


def _gate_call(gates_t, fbias_col, batch):
    n_pairs = gates_t.shape[0] // batch
    n_ch = 2 * ML_HEADS
    out = pl.BlockSpec((1, n_pairs, n_ch, LANES), lambda i: (i, 0, 0, 0))
    return pl.pallas_call(
        _gate_kernel, name="mlstm_gates",
        grid=(batch,),
        in_specs=[pl.BlockSpec((n_pairs, 2 * n_ch, LANES), lambda i: (i, 0, 0)),
                  pl.BlockSpec((n_ch, 1), lambda i: (0, 0))],
        out_specs=(out,) * 6,
        out_shape=(jax.ShapeDtypeStruct((batch, n_pairs, n_ch, LANES), F32),) * 6,
        compiler_params=_cparams(1),
    )(gates_t, fbias_col)


TE = 1024


def _mlstm_kernel(*refs):
    fwd, bwd, (hf_ref, hb_ref, state) = refs[:9], refs[9:18], refs[18:]
    i = pl.program_id(1)

    @pl.when(i == 0)
    def _():
        state[...] = jnp.zeros(state.shape, F32)

    n_pairs = TE // LANES
    s_idx = lax.broadcasted_iota(jnp.int32, (LANES, LANES), 0)
    t_idx = lax.broadcasted_iota(jnp.int32, (LANES, LANES), 1)
    same_chunk = (s_idx >= ML_CHUNK) == (t_idx >= ML_CHUNK)
    upper_lanes = lax.broadcasted_iota(jnp.int32, (1, LANES), 1) >= ML_CHUNK

    def pair_body(p, carry):
        jobs = []
        for d, ((qt_r, k_r, vt_r, a_r, m_r, iw_r, en_r, ws_r, dec_r), out_r) in enumerate(
                ((fwd, hf_ref), (bwd, hb_ref))):
            pp = p if d == 0 else n_pairs - 1 - p
            srows = pl.ds(pl.multiple_of(pp * LANES, LANES), LANES)
            a_t = jnp.transpose(jnp.concatenate(
                [a_r[0, pp], jnp.zeros((LANES - 2 * ML_HEADS, LANES), F32)], axis=0))
            m_t, iw_t, en_t, ws_t, dec_t = [r[0, pp] for r in (m_r, iw_r, en_r, ws_r, dec_r)]
            dec_lo, dec_hi = dec_t[:, :ML_CHUNK], pltpu.roll(dec_t, ML_CHUNK, 1)[:, :ML_CHUNK]
            in_first = upper_lanes if d else ~upper_lanes
            tri = same_chunk & ((s_idx >= t_idx) if d else (s_idx <= t_idx))
            for h in range(ML_HEADS):
                ch = d * ML_HEADS + h
                row = lambda t: t[ch:ch + 1]
                jobs.append(dict(
                    ch=ch, tri=tri, in_first=in_first, k=k_r[0, h, srows, :], qt=qt_r[0, h, pp],
                    vt=vt_r[0, h, pp], a=a_t[:, ch:ch + 1], m=row(m_t), iw=row(iw_t), en=row(en_t), ws=row(ws_t),
                    dec_first=row(dec_hi if d else dec_lo), dec_second=row(dec_lo if d else dec_hi),
                    out=(out_r, pp, h)))
        for j in jobs:
            vt_f = j["vt"].astype(F32)
            j["c0"] = state[j["ch"]]
            j["st"] = jnp.dot(j["k"], j["qt"], preferred_element_type=F32)
            j["inter1"] = jnp.dot(j["c0"].astype(BF16), j["qt"], preferred_element_type=F32)
            j["upd1"] = jnp.dot((vt_f * jnp.where(j["in_first"], j["ws"], 0.0)).astype(BF16), j["k"],
                                preferred_element_type=F32)
            j["upd2"] = jnp.dot((vt_f * jnp.where(j["in_first"], 0.0, j["ws"])).astype(BF16), j["k"],
                                preferred_element_type=F32)
        for j in jobs:
            j["swt"] = j["st"] * jnp.exp(jnp.where(j["tri"], j["a"] - j["m"], NEG_BIG))
            j["intra"] = jnp.dot(j["vt"], j["swt"].astype(BF16), preferred_element_type=F32)
            j["c1"] = j["dec_first"] * j["c0"] + j["upd1"]
            j["inter2"] = jnp.dot(j["c1"].astype(BF16), j["qt"], preferred_element_type=F32)
        for j in jobs:
            inter = jnp.where(j["in_first"], j["inter1"], j["inter2"])
            den = jnp.sum(j["swt"], axis=0, keepdims=True) + j["iw"] * inter[ML_HD:ML_HD + 1]
            tot = j["intra"][:ML_HD] + j["iw"] * inter[:ML_HD]
            out_r, pp, h = j["out"]
            out_r[0, pp, h * ML_HD:(h + 1) * ML_HD, :] = tot / jnp.maximum(jnp.abs(den), j["en"])
            state[j["ch"]] = j["dec_second"] * j["c1"] + j["upd2"]
        return carry

    lax.fori_loop(0, n_pairs, pair_body, 0)


def _mlstm_call(qt, k, vt, a_t, m_t, iw_t, en_t, ws_t, dec_t):
    b, _, s, _ = k.shape
    nt = s // TE
    ppt = TE // LANES

    def specs(rev):
        ti = (lambda i: nt - 1 - i) if rev else (lambda i: i)
        tile = pl.BlockSpec((1, ppt, 2 * ML_HEADS, LANES), lambda b, i: (b, ti(i), 0, 0))
        return [
            pl.BlockSpec((1, ML_HEADS, ppt, ML_HD, LANES), lambda b, i: (b, 0, ti(i), 0, 0)),
            pl.BlockSpec((1, ML_HEADS, TE, ML_HD), lambda b, i: (b, 0, ti(i), 0)),
            pl.BlockSpec((1, ML_HEADS, ppt, VT_ROWS, LANES), lambda b, i: (b, 0, ti(i), 0, 0)),
            tile, tile, tile, tile, tile, tile]

    args = [qt, k, vt, a_t, m_t, iw_t, en_t, ws_t, dec_t]
    out_f = pl.BlockSpec((1, ppt, MIX_W, LANES), lambda b, i: (b, i, 0, 0))
    out_b = pl.BlockSpec((1, ppt, MIX_W, LANES), lambda b, i: (b, nt - 1 - i, 0, 0))
    return pl.pallas_call(
        _mlstm_kernel, name="mlstm_scan",
        grid=(b, nt),
        in_specs=specs(False) + specs(True),
        out_specs=(out_f, out_b),
        out_shape=(jax.ShapeDtypeStruct((b, s // LANES, MIX_W, LANES), F32),) * 2,
        scratch_shapes=[pltpu.VMEM((2 * ML_HEADS, VT_ROWS, ML_HD), F32)],
        compiler_params=_cparams(2),
    )(*args, *args)


def _mlstm_branch(cqk, cv, gates_t, conv_w, fbias):
    qt, k, vt = _mlprep_call(cqk, cv, conv_w)
    factors = _gate_call(gates_t, fbias.reshape(2 * ML_HEADS, 1), cqk.shape[0])
    return _mlstm_call(qt, k, vt, *factors)


TF = 512


def _merge_kernel(x_ref, ya_ref, o1_ref, o2_ref, o3_ref, l1_ref, l2_ref, l3_ref, hf_ref, hb_ref,
                  co_ref, yd_ref, wg_ref, bg_ref, wbr_ref, wout_ref, mng_ref, lng_ref, lnb_ref,
                  wr_ref, x1_ref, x1p_ref, aff_ref, o_scr, l_scr, *, alpha):
    x = x_ref[0]
    xb = x.astype(BF16)

    def natural_order(src_ref, scr):
        dil, width = src_ref.shape[1], src_ref.shape[3]
        if dil == 1:
            return src_ref[0, 0]
        for r in range(dil):
            for c in range(width // LANES):
                scr[c, pl.ds(r, TF // dil, stride=dil), :] = src_ref[0, r, :, c * LANES:(c + 1) * LANES]
        return jnp.concatenate([scr[c] for c in range(width // LANES)], axis=1)

    lane_head = lax.broadcasted_iota(jnp.int32, (TF, MIX_W), 1) // ML_HD
    l1, l2, l3 = [natural_order(r, l_scr.at[p]) for p, r in enumerate((l1_ref, l2_ref, l3_ref))]
    o1, o2, o3 = [natural_order(r, o_scr.at[p]) for p, r in enumerate((o1_ref, o2_ref, o3_ref))]
    lm = jnp.maximum(jnp.maximum(l1, l2), l3)
    e1, e2, e3 = jnp.exp(l1 - lm), jnp.exp(l2 - lm), jnp.exp(l3 - lm)
    inv = 1.0 / (e1 + e2 + e3)

    def per_head(w):
        out = jnp.zeros((TF, MIX_W), F32)
        for h in range(ATT_HEADS):
            out = jnp.where(lane_head == h, w[:, h:h + 1], out)
        return out

    y_b = per_head(e1 * inv) * o1 + per_head(e2 * inv) * o2 + per_head(e3 * inv) * o3
    hsum_t = jnp.concatenate([hf_ref[0, p] + hb_ref[0, p] for p in range(TF // LANES)], axis=1)
    per_head_rows = hsum_t.reshape(ML_HEADS, ML_HD, TF)
    mu = jnp.mean(per_head_rows, axis=1, keepdims=True)
    cen = per_head_rows - mu
    var = jnp.mean(cen * cen, axis=1, keepdims=True)
    hn_t = (cen * lax.rsqrt(var + LN_EPS)).reshape(MIX_W, TF)
    y_c_t = (jax.nn.sigmoid(jnp.transpose(co_ref[0])) * (hn_t * mng_ref[...])).astype(BF16)
    ys = (ya_ref[0], y_b.astype(BF16), None, yd_ref[0])
    merged = jnp.zeros((TF, D_MODEL), F32)
    for n in range(N_BRANCH):
        cols = slice(n * D_MODEL, (n + 1) * D_MODEL)
        gate = jax.nn.sigmoid(jnp.dot(xb, wg_ref[:, cols], preferred_element_type=F32) + bg_ref[:, cols])
        if ys[n] is None:
            proj = lax.dot_general(y_c_t, wbr_ref[n], (((0,), (0,)), ((), ())), preferred_element_type=F32)
        else:
            proj = jnp.dot(ys[n], wbr_ref[n], preferred_element_type=F32)
        merged = merged + gate * proj
    mix = jnp.dot(merged.astype(BF16), wout_ref[...], preferred_element_type=F32)
    x1 = _standardize(alpha * x + mix) * lng_ref[...] + lnb_ref[...]
    x1_ref[0] = x1
    x1b = x1.astype(BF16)
    x1p_ref[0] = _pack_bf16_pair(x1b[:, :D_MODEL // 2], x1b[:, D_MODEL // 2:])
    logits = lax.dot_general(wr_ref[...], x1b, (((1,), (1,)), ((), ())),
                             preferred_element_type=F32)
    ex = jnp.exp(logits - jnp.max(logits, axis=0, keepdims=True))
    aff_ref[0] = ex / jnp.sum(ex, axis=0, keepdims=True)


def _merge_call(x, ya, o_list, l_list, hf, hb, co, yd, wg, bg, wbr, wout, mng, lng, lnb, wr_t, alpha):
    b, s, _ = x.shape
    tok = lambda w: pl.BlockSpec((1, TF, w), lambda b, i: (b, i, 0))
    grouped = lambda dil, w: pl.BlockSpec((1, dil, TF // dil, w), lambda b, i: (b, 0, i, 0))
    chunked = pl.BlockSpec((1, TF // LANES, MIX_W, LANES), lambda b, i: (b, i, 0, 0))
    const = lambda shp: pl.BlockSpec(shp, lambda b, i: (0,) * len(shp))
    return pl.pallas_call(
        functools.partial(_merge_kernel, alpha=alpha), name="merge_ln_router",
        grid=(b, s // TF),
        in_specs=[tok(D_MODEL), tok(MIX_W)] + [grouped(dil, MIX_W) for _, dil in DIL_PATTERNS]
                 + [grouped(dil, LANES) for _, dil in DIL_PATTERNS]
                 + [chunked, chunked, tok(MIX_W), tok(MIX_W)]
                 + [const((D_MODEL, N_BRANCH * D_MODEL)), const((1, N_BRANCH * D_MODEL)),
                    const((N_BRANCH, MIX_W, D_MODEL)), const((D_MODEL, D_MODEL)), const((MIX_W, 1)),
                    const((1, D_MODEL)), const((1, D_MODEL)), const((N_EXPERTS, D_MODEL))],
        out_specs=(tok(D_MODEL), tok(D_MODEL // 2), pl.BlockSpec((1, N_EXPERTS, TF), lambda b, i: (b, 0, i))),
        out_shape=(jax.ShapeDtypeStruct((b, s, D_MODEL), F32),
                   jax.ShapeDtypeStruct((b, s, D_MODEL // 2), jnp.int32),
                   jax.ShapeDtypeStruct((b, N_EXPERTS, s), F32)),
        scratch_shapes=[pltpu.VMEM((len(DIL_PATTERNS), MIX_W // LANES, TF, LANES), F32),
                        pltpu.VMEM((len(DIL_PATTERNS), 1, TF, LANES), F32)],
        compiler_params=_cparams(2),
    )(x, ya, *o_list, *l_list, hf, hb, co, yd, wg, bg, wbr, wout, mng, lng, lnb, wr_t)


TT = 256


def _select_kernel(aff_ref, slot_ref, *, cap):
    s = aff_ref.shape[2]
    bits = pltpu.bitcast(aff_ref[0], jnp.int32)

    def bit_step(i, thr):
        cand = thr | jnp.left_shift(jnp.int32(1), 30 - i)
        cnt = jnp.sum((bits >= cand).astype(jnp.int32), axis=1, keepdims=True)
        return jnp.where(cnt >= cap, cand, thr)

    thr = lax.fori_loop(0, 31, bit_step, jnp.zeros((N_EXPERTS, 1), jnp.int32))
    gt = bits > thr
    eq = bits == thr
    need = (cap - jnp.sum(gt.astype(jnp.int32), axis=1, keepdims=True)).astype(F32)
    upper = (lax.broadcasted_iota(jnp.int32, (TT, TT), 0)
             <= lax.broadcasted_iota(jnp.int32, (TT, TT), 1)).astype(BF16)
    eq_before = jnp.zeros((N_EXPERTS, 1), F32)
    sel_before = jnp.zeros((N_EXPERTS, 1), F32)
    for j in range(s // TT):
        cols = slice(j * TT, (j + 1) * TT)
        eq_j = eq[:, cols]
        eq_incl = eq_before + jnp.dot(eq_j.astype(BF16), upper, preferred_element_type=F32)
        sel_j = gt[:, cols] | (eq_j & (eq_incl <= need))
        sel_f = sel_j.astype(F32)
        sel_incl = sel_before + jnp.dot(sel_f.astype(BF16), upper, preferred_element_type=F32)
        slot_ref[0, :, cols] = jnp.where(sel_j, sel_incl - 1.0, -1.0).astype(jnp.int32)
        eq_before = eq_incl[:, TT - 1:TT]
        sel_before = sel_incl[:, TT - 1:TT]


def _select_call(aff_t, cap):
    b, e, s = aff_t.shape
    return pl.pallas_call(
        functools.partial(_select_kernel, cap=cap), name="expert_choice_select",
        grid=(b,),
        in_specs=[pl.BlockSpec((1, e, s), lambda i: (i, 0, 0))],
        out_specs=pl.BlockSpec((1, e, s), lambda i: (i, 0, 0)),
        out_shape=jax.ShapeDtypeStruct((b, e, s), jnp.int32),
        compiler_params=_cparams(1),
    )(aff_t)


SC_LANES = 16
SC_ROWS = 64
SC_IDX = 128
SC_SLAB = 128
SC_ZROWS = 64
CF = 1024


def _sc_dispatch_call(x_flat, slot2, aff2, seq, cap):
    n_pairs = slot2.shape[0]
    d = x_flat.shape[1]
    info = plsc.get_sparse_core_info()
    n_workers = info.num_cores * info.num_subcores
    assert n_pairs % n_workers == 0 and seq % SC_LANES == 0 and cap % (2 * SC_ROWS) == 0
    pairs_per_worker = n_pairs // n_workers
    mesh = plsc.VectorSubcoreMesh(core_axis_name="c", subcore_axis_name="s")

    @functools.partial(
        pl.kernel, mesh=mesh, name="expert_dispatch_sc",
        compiler_params=pltpu.CompilerParams(needs_layout_passes=False),
        out_type=(jax.ShapeDtypeStruct((n_pairs * cap, d), x_flat.dtype),
                  jax.ShapeDtypeStruct((n_pairs, cap // SC_IDX, SC_IDX), jnp.int32),
                  jax.ShapeDtypeStruct((n_pairs, cap // SC_IDX, SC_IDX), F32)),
        scratch_types=[pltpu.VMEM((seq,), jnp.int32), pltpu.VMEM((seq,), F32),
                       pltpu.VMEM((cap,), jnp.int32), pltpu.VMEM((cap // SC_IDX, SC_IDX), jnp.int32),
                       pltpu.VMEM((cap // SC_IDX, SC_IDX), F32),
                       pltpu.VMEM((2, SC_ROWS, d), x_flat.dtype), pltpu.SemaphoreType.DMA((2,))])
    def dispatch(x_hbm, slot_hbm, aff_hbm, xs_hbm, tok_hbm, gate_hbm,
                 slot_v, aff_v, idx_v, tok_v, gate_v, rows_v, sem):
        worker = lax.axis_index("s") * info.num_cores + lax.axis_index("c")
        lane = lax.iota(jnp.int32, SC_LANES)

        def gather(c0, buf):
            return pltpu.make_async_copy(x_hbm.at[idx_v.at[pl.ds(c0, SC_ROWS)]], rows_v.at[buf], sem.at[buf])
        for k in range(pairs_per_worker):
            pair = worker * pairs_per_worker + k
            row0 = (pair // N_EXPERTS) * seq
            pltpu.sync_copy(slot_hbm.at[pair], slot_v)
            pltpu.sync_copy(aff_hbm.at[pair], aff_v)

            @pl.loop(0, seq, step=SC_LANES)
            def _(t0):
                sv = slot_v[pl.ds(t0, SC_LANES)]
                picked = sv >= 0
                hi, lo = lax.shift_right_logical(sv, 7), sv & (SC_IDX - 1)
                plsc.store_scatter(tok_v, [hi, lo], t0 + lane, mask=picked)
                plsc.store_scatter(idx_v, [sv], row0 + t0 + lane, mask=picked)
                plsc.store_scatter(gate_v, [hi, lo], aff_v[pl.ds(t0, SC_LANES)], mask=picked)

            pltpu.sync_copy(tok_v, tok_hbm.at[pair])
            pltpu.sync_copy(gate_v, gate_hbm.at[pair])

            gather(0, 0).start()

            @pl.loop(0, cap, step=2 * SC_ROWS)
            def _(c0):
                gather(c0 + SC_ROWS, 1).start()
                gather(c0, 0).wait()
                pltpu.sync_copy(rows_v.at[0], xs_hbm.at[pl.ds(pair * cap + c0, SC_ROWS)])

                @pl.when(c0 + 2 * SC_ROWS < cap)
                def _():
                    gather(c0 + 2 * SC_ROWS, 0).start()

                gather(c0 + SC_ROWS, 1).wait()
                pltpu.sync_copy(rows_v.at[1], xs_hbm.at[pl.ds(pair * cap + c0 + SC_ROWS, SC_ROWS)])

    return dispatch(x_flat, slot2, aff2)


def _expert_kernel(xs_ref, g_ref, w1_ref, w3_ref, w2_ref, ye_ref, w1_bf, w3_bf, w2_bf):
    @pl.when((pl.program_id(1) == 0) & (pl.program_id(2) == 0))
    def _():
        w1_bf[...] = w1_ref[0, 0].astype(BF16)
        w3_bf[...] = w3_ref[0, 0].astype(BF16)
        w2_bf[...] = w2_ref[0, 0].astype(BF16)

    xs = jnp.concatenate(_unpack_bf16_pair(xs_ref[0, 0]), axis=1)
    hid = (jax.nn.silu(jnp.dot(xs, w1_bf[...], preferred_element_type=F32))
           * jnp.dot(xs, w3_bf[...], preferred_element_type=F32))
    g_rows = g_ref[0, 0]
    n_rows = g_rows.shape[0]
    g_t = jnp.transpose(jnp.concatenate([g_rows, jnp.zeros((LANES - n_rows, LANES), F32)], axis=0))
    g_col = jnp.concatenate([g_t[:, r:r + 1] for r in range(n_rows)], axis=0)
    ye_ref[0, 0] = jnp.dot(hid.astype(BF16), w2_bf[...], preferred_element_type=F32) * g_col


def _expert_call(xs4, gate4, w1, w3, w2, layer):
    b, e, cap, half = xs4.shape
    d, ff = w1.shape[2], w1.shape[3]
    assert d == 2 * half
    rows = lambda w: pl.BlockSpec((1, 1, CF, w), lambda e, b, j: (b, e, j, 0))
    wspec = lambda r, c: pl.BlockSpec((1, 1, r, c), lambda e, b, j: (layer, e, 0, 0))
    return pl.pallas_call(
        _expert_kernel, name="expert_ffn",
        grid=(e, b, cap // CF),
        in_specs=[rows(half), pl.BlockSpec((1, 1, CF // LANES, LANES), lambda e, b, j: (b, e, j, 0)),
                  wspec(d, ff), wspec(d, ff), wspec(ff, d)],
        out_specs=rows(d),
        out_shape=jax.ShapeDtypeStruct((b, e, cap, d), F32),
        scratch_shapes=[pltpu.VMEM((d, ff), BF16), pltpu.VMEM((d, ff), BF16), pltpu.VMEM((ff, d), BF16)],
        compiler_params=_cparams(3),
    )(xs4, gate4, w1, w3, w2)


def _sc_combine_call(ye_flat, tok3, seq):
    n_pairs, n_chunks, _ = tok3.shape
    cap = n_chunks * SC_IDX
    d = ye_flat.shape[1]
    nb = n_pairs // N_EXPERTS
    info = plsc.get_sparse_core_info()
    assert info.num_subcores == N_EXPERTS and nb % info.num_cores == 0 and n_chunks % 2 == 0
    assert seq % (info.num_subcores * SC_ZROWS) == 0 and d % SC_SLAB == 0
    batches_per_core = nb // info.num_cores
    own_rows = seq // info.num_subcores
    mesh = plsc.VectorSubcoreMesh(core_axis_name="c", subcore_axis_name="s")

    @functools.partial(
        pl.kernel, mesh=mesh, name="expert_combine_sc",
        compiler_params=pltpu.CompilerParams(needs_layout_passes=False),
        out_type=jax.ShapeDtypeStruct((nb * seq, d), F32),
        scratch_types=[pltpu.VMEM_SHARED((seq, SC_SLAB), F32),
                       pltpu.VMEM((n_chunks, SC_IDX), jnp.int32),
                       pltpu.VMEM((2, SC_IDX, SC_SLAB), F32),
                       pltpu.VMEM((SC_ZROWS, SC_SLAB), F32),
                       pltpu.SemaphoreType.DMA((2,))])
    def combine(ye_hbm, tok_hbm, out_hbm, acc_sh, tok_v, rows_v, zero_v, sem):
        core = lax.axis_index("c")
        sub = lax.axis_index("s")

        @pl.loop(0, SC_ZROWS)
        def _(r):
            for l0 in range(0, SC_SLAB, SC_LANES):
                zero_v[r, pl.ds(l0, SC_LANES)] = jnp.zeros((SC_LANES,), F32)

        for bb in range(batches_per_core):
            batch = core * batches_per_core + bb
            pair = batch * N_EXPERTS + sub
            pltpu.sync_copy(tok_hbm.at[pair], tok_v)

            @pl.loop(0, d // SC_SLAB)
            def _(slab):
                cols = pl.ds(pl.multiple_of(slab * SC_SLAB, SC_SLAB), SC_SLAB)

                @pl.loop(0, own_rows, step=SC_ZROWS)
                def _(r0):
                    pltpu.sync_copy(zero_v, acc_sh.at[pl.ds(sub * own_rows + r0, SC_ZROWS)])

                def load(j, buf):
                    return pltpu.make_async_copy(
                        ye_hbm.at[pl.ds(pair * cap + j * SC_IDX, SC_IDX), cols], rows_v.at[buf], sem.at[buf])

                load(0, 0).start()
                plsc.subcore_barrier()

                for j in range(0, n_chunks, 2):
                    load(j + 1, 1).start()
                    load(j, 0).wait()
                    pltpu.sync_copy(rows_v.at[0], acc_sh.at[tok_v.at[j]], add=True)
                    if j + 2 < n_chunks:
                        load(j + 2, 0).start()
                    load(j + 1, 1).wait()
                    pltpu.sync_copy(rows_v.at[1], acc_sh.at[tok_v.at[j + 1]], add=True)

                plsc.subcore_barrier()
                pltpu.sync_copy(acc_sh.at[pl.ds(sub * own_rows, own_rows)],
                                out_hbm.at[pl.ds(batch * seq + sub * own_rows, own_rows), cols])

    return combine(ye_flat, tok3)


TN = 512


def _resln_kernel(x_ref, y_ref, g_ref, b_ref, o_ref, *, alpha):
    o_ref[...] = _standardize(alpha * x_ref[...] + y_ref[...]) * g_ref[...] + b_ref[...]


def _resln_call(x2d, y2d, g, bta, alpha):
    n, d = x2d.shape
    tok = pl.BlockSpec((TN, d), lambda i: (i, 0))
    vec = pl.BlockSpec((1, d), lambda i: (0, 0))
    return pl.pallas_call(
        functools.partial(_resln_kernel, alpha=alpha), name="residual_layernorm",
        grid=(n // TN,), in_specs=[tok, tok, vec, vec], out_specs=tok,
        out_shape=jax.ShapeDtypeStruct((n, d), F32),
        compiler_params=_cparams(1),
    )(x2d, y2d, g, bta)


def _expert_choice_ffn(x1p, aff_t, w1, w3, w2, layer):
    b, s, half = x1p.shape
    d = 2 * half
    cap = EC_FACTOR * s // N_EXPERTS
    slot = _select_call(aff_t, cap)
    xs, tok, gate = _sc_dispatch_call(x1p.reshape(b * s, half), slot.reshape(b * N_EXPERTS, s),
                                      aff_t.reshape(b * N_EXPERTS, s), s, cap)
    ye = _expert_call(xs.reshape(b, N_EXPERTS, cap, half), gate.reshape(b, N_EXPERTS, cap // SC_IDX, SC_IDX), w1, w3, w2,
                      layer)
    out = _sc_combine_call(ye.reshape(b * N_EXPERTS * cap, d),
                           tok, s)
    return out.reshape(b, s, d)


def _pack_pool(pool_w):
    g, gd, _ = pool_w.shape
    out = jnp.zeros((g * gd, g * gd), F32)
    for i in range(g):
        out = out.at[i * gd:(i + 1) * gd, i * gd:(i + 1) * gd].set(pool_w[i])
    return out.astype(BF16)


def _layer(layer, x, pending, alpha, bias_tiles, w_in, b_in, gm_ln_g, gm_ws, gm_bs, ml_conv, ml_fbias,
           ml_norm_g, pool_w, pool_scale, w_branch, w_out, ln1_g, ln1_b, w_router, w_e1, w_e3, w_e2):
    b, s, d = x.shape
    n_small = 2576
    w_cat, b_cat = _pack_inproj_weights(w_in, b_in)
    wscat, bsfull = _pack_gmlp(gm_ws, gm_bs)
    if pending is None:
        outs = _inproj_call(x.reshape(b * s, d), w_cat, b_cat, gm_ln_g[None], wscat, bsfull, b)
    else:
        *outs, x2 = _inproj_call(pending, w_cat, b_cat, gm_ln_g[None], wscat, bsfull, b, alpha)
        x = x2.reshape(b, s, d)
    ya, qkv1, qkv4, qkv16, cqk, cv, co, dx, gates_t = outs
    r3 = lambda t: t.reshape(b, s, t.shape[-1])
    o_list, l_list = [], []
    for qkv, bias in zip((qkv1, qkv4, qkv16), bias_tiles):
        o, lse = _attn_call(qkv, bias)
        o_list.append(o)
        l_list.append(lse)
    hf, hb = _mlstm_branch(r3(cqk), r3(cv), gates_t, ml_conv, ml_fbias)
    yd = _pool_call(r3(dx), _pack_pool(pool_w), pool_scale[None])
    x1, x1p, aff_t = _merge_call(
        x, r3(ya), o_list, l_list, hf, hb, r3(co), yd,
        w_in[:, n_small:].astype(BF16), b_in[None, n_small:], w_branch.astype(BF16), w_out.astype(BF16),
        ml_norm_g[:, None], ln1_g[None], ln1_b[None], jnp.transpose(w_router).astype(BF16), alpha)
    ffn = _expert_choice_ffn(x1p, aff_t, w_e1, w_e3, w_e2, layer)
    return x1.reshape(b * s, d), ffn.reshape(b * s, d)


def kernel(x, w_in, b_in, gm_ln_g, gm_ws, gm_bs, rel_bias, ml_conv, ml_fbias, ml_norm_g, pool_w,
           pool_scale, w_branch, w_out, ln1_g, ln1_b, w_router, w_e1, w_e3, w_e2, ln2_g, ln2_b):
    depth = w_in.shape[0]
    alpha = (2 * depth) ** 0.25
    bias_tiles = [_attn_bias_tile(rel_bias, window, dil) for window, dil in DIL_PATTERNS]
    b, s, d = x.shape
    pending = None
    for l in range(depth):
        x1, ffn = _layer(l, x, pending, alpha, bias_tiles, w_in[l], b_in[l], gm_ln_g[l], gm_ws[l], gm_bs[l],
                         ml_conv[l], ml_fbias[l], ml_norm_g[l], pool_w[l], pool_scale[l], w_branch[l],
                         w_out[l], ln1_g[l], ln1_b[l], w_router[l], w_e1, w_e3, w_e2)
        pending = (x1, ffn, ln2_g[l][None], ln2_b[l][None])
    return _resln_call(*pending, alpha).reshape(b, s, d)
```

```python
import functools
import math

import jax
import jax.numpy as jnp
import numpy as np
from jax import lax
from jax.experimental import pallas as pl
from jax.experimental.pallas import tpu as pltpu
from jax.experimental.pallas import tpu_sc as plsc

F32 = jnp.float32
BF16 = jnp.bfloat16

D_MODEL = 1024
MIX_W = 256
N_BRANCH = 4
GM_CHUNK = 128
GM_GROUPS = 4
ATT_HEADS = 4
ATT_HD = 64
DIL_PATTERNS = ((128, 1), (512, 4), (2048, 16))
ATT_BLOCK = 64
REL_BUCKETS = 32
REL_MAX_DIST = 1024
ML_HEADS = 4
ML_HD = 64
ML_CHUNK = 64
POOL_WINDOWS = (2, 4, 8, 16)
N_EXPERTS = 16
EXPERT_FF = 1024
EC_FACTOR = 2
LN_EPS = 1e-5
NEG_BIG = -1e30

V7X_VMEM_LIMIT = 56 * 1024 * 1024
LANES = 128
HALO = 8


def _cparams(n_grid, vmem=V7X_VMEM_LIMIT):
    return pltpu.CompilerParams(dimension_semantics=("arbitrary",) * n_grid,
                                vmem_limit_bytes=vmem)


def _pack_bf16_pair(lo, hi):
    lo_bits = lax.shift_right_logical(pltpu.bitcast(lo.astype(F32), jnp.int32), 16)
    return pltpu.bitcast(hi.astype(F32), jnp.int32) | lo_bits


def _unpack_bf16_pair(packed):
    lo = pltpu.bitcast(lax.shift_left(packed, 16), F32).astype(BF16)
    hi = pltpu.bitcast(packed & jnp.int32(-65536), F32).astype(BF16)
    return lo, hi


def _standardize(xf):
    mu = jnp.mean(xf, axis=-1, keepdims=True)
    var = jnp.mean(jnp.square(xf - mu), axis=-1, keepdims=True)
    return (xf - mu) * lax.rsqrt(var + LN_EPS)


TA = 512
A_COLS = 2560 + LANES


def _inproj_kernel(*refs, alpha):
    if alpha is None:
        x_ref, *refs = refs
        x = x_ref[...]
    else:
        x1_ref, y_ref, g2_ref, b2_ref, *refs = refs
        x = _standardize(alpha * x1_ref[...] + y_ref[...]) * g2_ref[...] + b2_ref[...]
        refs[-2][...] = x
        refs = refs[:-2] + refs[-1:]
    (w_ref, b_ref, lng_ref, wscat_ref, bsfull_ref,
     ya_ref, qkv1_ref, qkv4_ref, qkv16_ref, cqk_ref, cv_ref, co_ref, dx_ref, gt_ref, qkv_scr) = refs
    xb = x.astype(BF16)
    h = jnp.dot(xb, w_ref[...], preferred_element_type=F32) + b_ref[...]
    qkv1_ref[0, 0] = h[:, 512:1280].astype(BF16)
    for c in range(768 // LANES):
        qkv_scr[c] = h[:, 512 + c * LANES:512 + (c + 1) * LANES]
    for (_, dil), out_ref in zip(DIL_PATTERNS[1:], (qkv4_ref, qkv16_ref)):
        for r in range(dil):
            for c in range(768 // LANES):
                out_ref[0, r, :, c * LANES:(c + 1) * LANES] = (
                    qkv_scr[c, pl.ds(r, TA // dil, stride=dil), :].astype(BF16))
    cqk_ref[...] = h[:, 1280:1792]
    cv_ref[...] = h[:, 1792:2048].astype(BF16)
    co_ref[...] = h[:, 2048:2304]
    dx_ref[...] = h[:, 2304:2560]
    gates_t = jnp.transpose(h[:, 2560:2688])
    for j in range(TA // LANES):
        gt_ref[j] = gates_t[0:4 * ML_HEADS, j * LANES:(j + 1) * LANES]
    u = jax.nn.gelu(h[:, 0:256])
    v = jax.nn.gelu(h[:, 256:512])
    vn = _standardize(v) * lng_ref[...]
    lane_grp = lax.broadcasted_iota(jnp.int32, (GM_CHUNK, MIX_W), 1) // (MIX_W // GM_GROUPS)
    for c in range(TA // GM_CHUNK):
        vc = vn[c * GM_CHUNK:(c + 1) * GM_CHUNK]
        stacked = jnp.concatenate(
            [jnp.where(lane_grp == g, vc, 0.0).astype(BF16) for g in range(GM_GROUPS)], axis=0)
        mixed = jnp.dot(wscat_ref[...], stacked, preferred_element_type=F32) + bsfull_ref[...]
        ya_ref[c * GM_CHUNK:(c + 1) * GM_CHUNK, :] = (
            u[c * GM_CHUNK:(c + 1) * GM_CHUNK] * mixed).astype(BF16)


def _inproj_call(x_in, w_cat, b_cat, lng, wscat, bsfull, batch, alpha=None):
    fused = alpha is not None
    n = (x_in[0] if fused else x_in).shape[0]
    seq = n // batch
    tpb = seq // TA
    tok = lambda w: pl.BlockSpec((TA, w), lambda i: (i, 0))
    const = lambda s: pl.BlockSpec(s, lambda i: (0,) * len(s))
    regrouped = lambda dil: pl.BlockSpec((1, dil, TA // dil, 768), lambda i: (i // tpb, 0, i % tpb, 0))
    out_shape = (
        jax.ShapeDtypeStruct((n, 256), BF16),
    ) + tuple(jax.ShapeDtypeStruct((batch, dil, seq // dil, 768), BF16)
              for _, dil in DIL_PATTERNS) + (
        jax.ShapeDtypeStruct((n, 512), F32),
        jax.ShapeDtypeStruct((n, 256), BF16),
        jax.ShapeDtypeStruct((n, 256), F32),
        jax.ShapeDtypeStruct((n, 256), F32),
        jax.ShapeDtypeStruct((n // LANES, 4 * ML_HEADS, LANES), F32),
    )
    x_specs = [tok(D_MODEL), tok(D_MODEL), const((1, D_MODEL)), const((1, D_MODEL))] if fused else [tok(D_MODEL)]
    out_specs = ((tok(256),) + tuple(regrouped(dil) for _, dil in DIL_PATTERNS)
                 + (tok(512), tok(256), tok(256), tok(256),
                    pl.BlockSpec((TA // LANES, 4 * ML_HEADS, LANES), lambda i: (i, 0, 0))))
    if fused:
        out_specs += (tok(D_MODEL),)
        out_shape += (jax.ShapeDtypeStruct((n, D_MODEL), F32),)
    return pl.pallas_call(
        functools.partial(_inproj_kernel, alpha=alpha), name="inproj_gmlp",
        grid=(n // TA,),
        in_specs=x_specs + [const((D_MODEL, A_COLS)), const((1, A_COLS)), const((1, MIX_W)),
                            const((GM_CHUNK, GM_GROUPS * GM_CHUNK)), const((GM_CHUNK, MIX_W))],
        out_specs=out_specs,
        out_shape=out_shape,
        scratch_shapes=[pltpu.VMEM((768 // LANES, TA, LANES), F32)],
        compiler_params=_cparams(1),
    )(*(x_in if fused else (x_in,)), w_cat, b_cat, lng, wscat, bsfull)


def _pack_inproj_weights(w_in, b_in):
    pad = lambda a: jnp.pad(a, ((0, 0), (0, LANES - 4 * ML_HEADS)))
    w_cat = jnp.concatenate([w_in[:, 0:2304], w_in[:, 2320:2576], pad(w_in[:, 2304:2320])], axis=1)
    b2 = b_in[None, :]
    b_cat = jnp.concatenate([b2[:, 0:2304], b2[:, 2320:2576], pad(b2[:, 2304:2320])], axis=1)
    return w_cat.astype(BF16), b_cat


def _pack_gmlp(gm_ws, gm_bs):
    wscat = jnp.transpose(gm_ws, (1, 0, 2)).reshape(GM_CHUNK, GM_GROUPS * GM_CHUNK).astype(BF16)
    bsfull = jnp.repeat(jnp.transpose(gm_bs), MIX_W // GM_GROUPS, axis=1)
    return wscat, bsfull


def _halo_specs(t, width, n_tiles):
    r = t // HALO
    main = pl.BlockSpec((1, t, width), lambda b, i: (b, i, 0))
    prev = pl.BlockSpec((1, HALO, width), lambda b, i: (b, jnp.maximum(i * r - 1, 0), 0))
    nxt = pl.BlockSpec((1, HALO, width), lambda b, i: (b, jnp.minimum((i + 1) * r, n_tiles * r - 1), 0))
    return main, prev, nxt


def _fill_halo_scratch(buf, x_ref, p_ref, n_ref, t):
    i = pl.program_id(1)
    last = pl.num_programs(1) - 1
    buf[0:HALO, :] = jnp.where(i > 0, p_ref[0], 0.0)
    buf[HALO:HALO + t, :] = x_ref[0]
    buf[HALO + t:2 * HALO + t, :] = jnp.where(i < last, n_ref[0], 0.0)


TP = 512


def _pool_kernel(x_ref, p_ref, n_ref, w_ref, sc_ref, o_ref, buf, lvl):
    _fill_halo_scratch(buf, x_ref, p_ref, n_ref, TP)
    seq = pl.num_programs(1) * TP
    pos = pl.program_id(1) * TP + lax.broadcasted_iota(jnp.int32, (TP, 1), 0)
    lane_grp = lax.broadcasted_iota(jnp.int32, (TP, MIX_W), 1) // (MIX_W // len(POOL_WINDOWS))
    x0 = buf[HALO:HALO + TP, :]
    sums = []
    src, rows = buf, TP + 2 * HALO
    for k, win in enumerate(POOL_WINDOWS):
        half = win // 2
        rows -= half
        cur = src[0:rows, :] + src[half:rows + half, :] if k else buf[0:rows, :] + buf[1:rows + 1, :]
        if k + 1 < len(POOL_WINDOWS):
            lvl[k, 0:rows, :] = cur
            sums.append(lvl[k, HALO - half:HALO - half + TP, :])
            src = lvl.at[k]
        else:
            sums.append(cur[0:TP])
    pooled = jnp.zeros((TP, MIX_W), F32)
    for gi, win in enumerate(POOL_WINDOWS):
        half = win // 2
        cnt = (jnp.minimum(pos + half, seq) - jnp.maximum(pos - half, 0)).astype(F32)
        pooled = jnp.where(lane_grp == gi, sums[gi] / cnt - x0, pooled)
    mixed = jnp.dot(pooled.astype(BF16), w_ref[...], preferred_element_type=F32)
    o_ref[0] = (mixed * sc_ref[...]).astype(BF16)


def _pool_call(dx, w_block, scale):
    b, s, _ = dx.shape
    nt = s // TP
    main, prev, nxt = _halo_specs(TP, MIX_W, nt)
    return pl.pallas_call(
        _pool_kernel, name="pool_mixer",
        grid=(b, nt),
        in_specs=[main, prev, nxt,
                  pl.BlockSpec((MIX_W, MIX_W), lambda b, i: (0, 0)),
                  pl.BlockSpec((1, MIX_W), lambda b, i: (0, 0))],
        out_specs=pl.BlockSpec((1, TP, MIX_W), lambda b, i: (b, i, 0)),
        out_shape=jax.ShapeDtypeStruct((b, s, MIX_W), BF16),
        scratch_shapes=[pltpu.VMEM((TP + 2 * HALO, MIX_W), F32),
                        pltpu.VMEM((len(POOL_WINDOWS) - 1, TP + 2 * HALO, MIX_W), F32)],
        compiler_params=_cparams(2),
    )(dx, dx, dx, w_block, scale)


TQ = 128
TQS = 512
TKEYS = TQ + 2 * ATT_BLOCK


def _attn_kernel(q_ref, kp_ref, km_ref, kn_ref, vp_ref, vm_ref, vn_ref, bias_ref, o_ref, lse_ref):
    i = pl.program_id(2)
    q = q_ref[0, 0] * ATT_HD ** -0.5
    k = jnp.concatenate([kp_ref[0, 0], km_ref[0, 0], kn_ref[0, 0]], axis=0)
    v = jnp.concatenate([vp_ref[0, 0], vm_ref[0, 0], vn_ref[0, 0]], axis=0)
    lane = lax.broadcasted_iota(jnp.int32, (TQ, LANES), 1)
    lane_half = lax.broadcasted_iota(jnp.int32, (1, LANES), 1) // ATT_HD
    keep = [jnp.where(lane_half == hh, 1.0, 0.0).astype(BF16) for hh in range(2)]
    n_sub = TQS // TQ
    last_step = pl.num_programs(2) - 1
    for j in range(n_sub):
        if j == 0:
            variant = jnp.where(i == 0, 0, 1)
        elif j == n_sub - 1:
            variant = jnp.where(i == last_step, 2, 1)
        else:
            variant = 1
        qrows = slice(j * TQ, (j + 1) * TQ)
        krows = slice(j * TQ, j * TQ + TKEYS)
        lse_tile = jnp.zeros((TQ, LANES), F32)
        for pair in range(ATT_HEADS // 2):
            grp = slice(pair * LANES, (pair + 1) * LANES)
            q_pair, k_pair, v_pair = q[qrows, grp], k[krows, grp], v[krows, grp]
            o_pair = jnp.zeros((TQ, LANES), F32)
            for hh in range(2):
                h = 2 * pair + hh
                logits = lax.dot_general(q_pair * keep[hh], k_pair, (((1,), (1,)), ((), ())),
                                         preferred_element_type=F32) + bias_ref[variant, h]
                m = jnp.max(logits, axis=-1, keepdims=True)
                p = jnp.exp(logits - m)
                ssum = jnp.sum(p, axis=-1, keepdims=True)
                o = jnp.dot(p.astype(BF16), v_pair, preferred_element_type=F32) / ssum
                o_pair = jnp.where(lane_half == hh, o, o_pair)
                lse_tile = jnp.where(lane == h, m + jnp.log(ssum), lse_tile)
            o_ref[0, 0, qrows, grp] = o_pair
        lse_ref[0, 0, qrows, :] = lse_tile


def _attn_call(qkv, bias):
    b, dil, l, _ = qkv.shape
    nt = l // TQS
    r64 = TQS // ATT_BLOCK
    main = lambda c: pl.BlockSpec((1, 1, TQS, MIX_W), lambda b, r, i: (b, r, i, c))
    prev = lambda c: pl.BlockSpec((1, 1, ATT_BLOCK, MIX_W),
                                  lambda b, r, i: (b, r, jnp.maximum(i * r64 - 1, 0), c))
    nxt = lambda c: pl.BlockSpec((1, 1, ATT_BLOCK, MIX_W),
                                 lambda b, r, i: (b, r, jnp.minimum((i + 1) * r64, nt * r64 - 1), c))
    return pl.pallas_call(
        _attn_kernel, name="band_attention",
        grid=(b, dil, nt),
        in_specs=[main(0), prev(1), main(1), nxt(1), prev(2), main(2), nxt(2),
                  pl.BlockSpec((3, ATT_HEADS, TQ, TKEYS), lambda b, r, i: (0, 0, 0, 0))],
        out_specs=(pl.BlockSpec((1, 1, TQS, MIX_W), lambda b, r, i: (b, r, i, 0)),
                   pl.BlockSpec((1, 1, TQS, LANES), lambda b, r, i: (b, r, i, 0))),
        out_shape=(jax.ShapeDtypeStruct((b, dil, l, MIX_W), F32),
                   jax.ShapeDtypeStruct((b, dil, l, LANES), F32)),
        compiler_params=_cparams(3),
    )(qkv, qkv, qkv, qkv, qkv, qkv, qkv, bias)


def _t5_bucket_static(rel):
    half = REL_BUCKETS // 2
    max_exact = half // 2
    ret = np.where(rel > 0, half, 0)
    n = np.abs(rel)
    nf = np.maximum(n, 1).astype(np.float32)
    large = max_exact + (np.log(nf / np.float32(max_exact)) / np.float32(math.log(REL_MAX_DIST / max_exact))
                         * np.float32(half - max_exact)).astype(np.int32)
    large = np.minimum(large, half - 1)
    return ret + np.where(n < max_exact, n, large)


def _attn_bias_tile(rel_bias, window, dil):
    side = (window // 2) // dil
    rel = np.arange(TKEYS)[None, :] - ATT_BLOCK - np.arange(TQ)[:, None]
    onehot = jax.nn.one_hot(jnp.asarray(_t5_bucket_static(dil * rel), jnp.int32), REL_BUCKETS, dtype=F32)
    bias = jnp.einsum('qkr,rh->hqk', onehot, rel_bias, precision=lax.Precision.HIGHEST)
    key = np.arange(TKEYS)[None, :]
    inside = np.abs(rel) <= side
    masks = np.stack([inside & (key >= ATT_BLOCK), inside, inside & (key < ATT_BLOCK + TQ)])
    return jnp.where(jnp.asarray(masks)[:, None], bias[None], NEG_BIG)


TM = 512
VT_ROWS = ML_HD + 16


def _mlprep_kernel(x_ref, p_ref, n_ref, v_ref, w_ref, qt_out, k_out, vt_out, buf):
    _fill_halo_scratch(buf, x_ref, p_ref, n_ref, TM)
    conv = (buf[HALO - 1:HALO - 1 + TM, :] * w_ref[0:1, :] + buf[HALO:HALO + TM, :] * w_ref[1:2, :]
            + buf[HALO + 1:HALO + 1 + TM, :] * w_ref[2:3, :])
    qk = jax.nn.silu(conv)
    qt = jnp.transpose(qk[:, :MIX_W])
    vt = jnp.transpose(v_ref[0].astype(F32))
    ones_rows = jnp.where(lax.broadcasted_iota(jnp.int32, (VT_ROWS - ML_HD, LANES), 0) == 0, 1.0, 0.0)
    for h in range(ML_HEADS):
        sl = slice(h * ML_HD, (h + 1) * ML_HD)
        k_out[0, h] = (qk[:, MIX_W + h * ML_HD:MIX_W + (h + 1) * ML_HD] * ML_HD ** -0.5).astype(BF16)
        for p in range(TM // LANES):
            pl_ = slice(p * LANES, (p + 1) * LANES)
            qt_out[0, h, p] = qt[sl, pl_].astype(BF16)
            vt_out[0, h, p] = jnp.concatenate([vt[sl, pl_], ones_rows], axis=0).astype(BF16)


def _mlprep_call(cqk, cv, conv_w):
    b, s, _ = cqk.shape
    nt = s // TM
    n_pairs = s // LANES
    ppt = TM // LANES
    main, prev, nxt = _halo_specs(TM, 2 * MIX_W, nt)
    return pl.pallas_call(
        _mlprep_kernel, name="mlstm_prep",
        grid=(b, nt),
        in_specs=[main, prev, nxt,
                  pl.BlockSpec((1, TM, MIX_W), lambda b, i: (b, i, 0)),
                  pl.BlockSpec((3, 2 * MIX_W), lambda b, i: (0, 0))],
        out_specs=(pl.BlockSpec((1, ML_HEADS, ppt, ML_HD, LANES), lambda b, i: (b, 0, i, 0, 0)),
                   pl.BlockSpec((1, ML_HEADS, TM, ML_HD), lambda b, i: (b, 0, i, 0)),
                   pl.BlockSpec((1, ML_HEADS, ppt, VT_ROWS, LANES), lambda b, i: (b, 0, i, 0, 0))),
        out_shape=(jax.ShapeDtypeStruct((b, ML_HEADS, n_pairs, ML_HD, LANES), BF16),
                   jax.ShapeDtypeStruct((b, ML_HEADS, s, ML_HD), BF16),
                   jax.ShapeDtypeStruct((b, ML_HEADS, n_pairs, VT_ROWS, LANES), BF16)),
        scratch_shapes=[pltpu.VMEM((TM + 2 * HALO, 2 * MIX_W), F32)],
        compiler_params=_cparams(2),
    )(cqk, cqk, cqk, cv, conv_w)


def _gate_kernel(g_ref, fb_ref, a_ref, m_ref, iw_ref, en_ref, ws_ref, dec_ref):
    x = g_ref[...]
    n_pairs = x.shape[0]
    n_ch = 2 * ML_HEADS
    lane = lax.broadcasted_iota(jnp.int32, (1, 1, LANES), 2)
    t_in = lane % ML_CHUNK
    second = lane >= ML_CHUNK
    fwd_row = lax.broadcasted_iota(jnp.int32, (1, n_ch, 1), 1) < ML_HEADS
    li = x[:, 0:n_ch, :]
    z = x[:, n_ch:2 * n_ch, :] + fb_ref[...]
    lf = jnp.minimum(z, 0.0) - jnp.log1p(jnp.exp(-jnp.abs(z)))

    def within_chunk(v, op, ident, prefix):
        s = 1
        while s < ML_CHUNK:
            nb = pltpu.roll(v, s if prefix else LANES - s, 2)
            ok = (t_in >= s) if prefix else (t_in < ML_CHUNK - s)
            v = op(v, jnp.where(ok, nb, ident))
            s *= 2
        return v

    pre = within_chunk(lf, jnp.add, 0.0, True)
    suf = within_chunk(lf, jnp.add, 0.0, False)
    g = pre + suf - lf
    b = jnp.where(fwd_row, pre, suf)
    a = li - b
    cm_pre = within_chunk(a, jnp.maximum, -jnp.inf, True)
    cm_suf = within_chunk(a, jnp.maximum, -jnp.inf, False)
    cm = jnp.where(fwd_row, cm_pre, cm_suf)
    amax = jnp.maximum(cm_pre, cm_suf)

    def shift_pairs(v, k, fill):
        pad = jnp.full((abs(k),) + v.shape[1:], fill, F32)
        return (jnp.concatenate([pad, v[:n_pairs - k]], axis=0) if k > 0
                else jnp.concatenate([v[-k:], pad], axis=0))

    def from_chunk(v, dist, fill, forward):
        if dist == 1:
            y = pltpu.roll(v, ML_CHUNK, 2)
            if forward:
                return jnp.where(second, y, shift_pairs(y, 1, fill))
            return jnp.where(second, shift_pairs(y, -1, fill), y)
        return shift_pairs(v, dist // 2 if forward else -(dist // 2), fill)

    def running_stabiliser(forward):
        big_g, big_a = g, amax + g
        dist = 1
        while dist < 2 * n_pairs:
            gp = from_chunk(big_g, dist, 0.0, forward)
            ap = from_chunk(big_a, dist, -jnp.inf, forward)
            big_g, big_a = gp + big_g, jnp.maximum(ap + big_g, big_a)
            dist *= 2
        ge = from_chunk(big_g, 1, 0.0, forward)
        ae = from_chunk(big_a, 1, -jnp.inf, forward)
        return jnp.maximum(ge, ae)

    m_chunk = jnp.where(fwd_row, running_stabiliser(True), running_stabiliser(False))
    m_t = jnp.maximum(cm, m_chunk)
    m_last = jnp.maximum(amax, m_chunk)
    a_ref[0] = a
    m_ref[0] = m_t
    iw_ref[0] = jnp.exp(m_chunk - m_t)
    en_ref[0] = jnp.exp(-(b + m_t))
    ws_ref[0] = jnp.exp(a - m_last)
    dec_ref[0] = jnp.exp(m_chunk - m_last)


def _gate_call(gates_t, fbias_col, batch):
    n_pairs = gates_t.shape[0] // batch
    n_ch = 2 * ML_HEADS
    out = pl.BlockSpec((1, n_pairs, n_ch, LANES), lambda i: (i, 0, 0, 0))
    return pl.pallas_call(
        _gate_kernel, name="mlstm_gates",
        grid=(batch,),
        in_specs=[pl.BlockSpec((n_pairs, 2 * n_ch, LANES), lambda i: (i, 0, 0)),
                  pl.BlockSpec((n_ch, 1), lambda i: (0, 0))],
        out_specs=(out,) * 6,
        out_shape=(jax.ShapeDtypeStruct((batch, n_pairs, n_ch, LANES), F32),) * 6,
        compiler_params=_cparams(1),
    )(gates_t, fbias_col)


TE = 1024


def _mlstm_kernel(*refs):
    fwd, bwd, (hf_ref, hb_ref, state) = refs[:9], refs[9:18], refs[18:]
    i = pl.program_id(1)

    @pl.when(i == 0)
    def _():
        state[...] = jnp.zeros(state.shape, F32)

    n_pairs = TE // LANES
    s_idx = lax.broadcasted_iota(jnp.int32, (LANES, LANES), 0)
    t_idx = lax.broadcasted_iota(jnp.int32, (LANES, LANES), 1)
    same_chunk = (s_idx >= ML_CHUNK) == (t_idx >= ML_CHUNK)
    upper_lanes = lax.broadcasted_iota(jnp.int32, (1, LANES), 1) >= ML_CHUNK

    def pair_body(p, carry):
        jobs = []
        for d, ((qt_r, k_r, vt_r, a_r, m_r, iw_r, en_r, ws_r, dec_r), out_r) in enumerate(
                ((fwd, hf_ref), (bwd, hb_ref))):
            pp = p if d == 0 else n_pairs - 1 - p
            srows = pl.ds(pl.multiple_of(pp * LANES, LANES), LANES)
            a_t = jnp.transpose(jnp.concatenate(
                [a_r[0, pp], jnp.zeros((LANES - 2 * ML_HEADS, LANES), F32)], axis=0))
            m_t, iw_t, en_t, ws_t, dec_t = [r[0, pp] for r in (m_r, iw_r, en_r, ws_r, dec_r)]
            dec_lo, dec_hi = dec_t[:, :ML_CHUNK], pltpu.roll(dec_t, ML_CHUNK, 1)[:, :ML_CHUNK]
            in_first = upper_lanes if d else ~upper_lanes
            tri = same_chunk & ((s_idx >= t_idx) if d else (s_idx <= t_idx))
            for h in range(ML_HEADS):
                ch = d * ML_HEADS + h
                row = lambda t: t[ch:ch + 1]
                jobs.append(dict(
                    ch=ch, tri=tri, in_first=in_first, k=k_r[0, h, srows, :], qt=qt_r[0, h, pp],
                    vt=vt_r[0, h, pp], a=a_t[:, ch:ch + 1], m=row(m_t), iw=row(iw_t), en=row(en_t), ws=row(ws_t),
                    dec_first=row(dec_hi if d else dec_lo), dec_second=row(dec_lo if d else dec_hi),
                    out=(out_r, pp, h)))
        for j in jobs:
            vt_f = j["vt"].astype(F32)
            j["c0"] = state[j["ch"]]
            j["st"] = jnp.dot(j["k"], j["qt"], preferred_element_type=F32)
            j["inter1"] = jnp.dot(j["c0"].astype(BF16), j["qt"], preferred_element_type=F32)
            j["upd1"] = jnp.dot((vt_f * jnp.where(j["in_first"], j["ws"], 0.0)).astype(BF16), j["k"],
                                preferred_element_type=F32)
            j["upd2"] = jnp.dot((vt_f * jnp.where(j["in_first"], 0.0, j["ws"])).astype(BF16), j["k"],
                                preferred_element_type=F32)
        for j in jobs:
            j["swt"] = j["st"] * jnp.exp(jnp.where(j["tri"], j["a"] - j["m"], NEG_BIG))
            j["intra"] = jnp.dot(j["vt"], j["swt"].astype(BF16), preferred_element_type=F32)
            j["c1"] = j["dec_first"] * j["c0"] + j["upd1"]
            j["inter2"] = jnp.dot(j["c1"].astype(BF16), j["qt"], preferred_element_type=F32)
        for j in jobs:
            inter = jnp.where(j["in_first"], j["inter1"], j["inter2"])
            den = jnp.sum(j["swt"], axis=0, keepdims=True) + j["iw"] * inter[ML_HD:ML_HD + 1]
            tot = j["intra"][:ML_HD] + j["iw"] * inter[:ML_HD]
            out_r, pp, h = j["out"]
            out_r[0, pp, h * ML_HD:(h + 1) * ML_HD, :] = tot / jnp.maximum(jnp.abs(den), j["en"])
            state[j["ch"]] = j["dec_second"] * j["c1"] + j["upd2"]
        return carry

    lax.fori_loop(0, n_pairs, pair_body, 0)


def _mlstm_call(qt, k, vt, a_t, m_t, iw_t, en_t, ws_t, dec_t):
    b, _, s, _ = k.shape
    nt = s // TE
    ppt = TE // LANES

    def specs(rev):
        ti = (lambda i: nt - 1 - i) if rev else (lambda i: i)
        tile = pl.BlockSpec((1, ppt, 2 * ML_HEADS, LANES), lambda b, i: (b, ti(i), 0, 0))
        return [
            pl.BlockSpec((1, ML_HEADS, ppt, ML_HD, LANES), lambda b, i: (b, 0, ti(i), 0, 0)),
            pl.BlockSpec((1, ML_HEADS, TE, ML_HD), lambda b, i: (b, 0, ti(i), 0)),
            pl.BlockSpec((1, ML_HEADS, ppt, VT_ROWS, LANES), lambda b, i: (b, 0, ti(i), 0, 0)),
            tile, tile, tile, tile, tile, tile]

    args = [qt, k, vt, a_t, m_t, iw_t, en_t, ws_t, dec_t]
    out_f = pl.BlockSpec((1, ppt, MIX_W, LANES), lambda b, i: (b, i, 0, 0))
    out_b = pl.BlockSpec((1, ppt, MIX_W, LANES), lambda b, i: (b, nt - 1 - i, 0, 0))
    return pl.pallas_call(
        _mlstm_kernel, name="mlstm_scan",
        grid=(b, nt),
        in_specs=specs(False) + specs(True),
        out_specs=(out_f, out_b),
        out_shape=(jax.ShapeDtypeStruct((b, s // LANES, MIX_W, LANES), F32),) * 2,
        scratch_shapes=[pltpu.VMEM((2 * ML_HEADS, VT_ROWS, ML_HD), F32)],
        compiler_params=_cparams(2),
    )(*args, *args)


def _mlstm_branch(cqk, cv, gates_t, conv_w, fbias):
    qt, k, vt = _mlprep_call(cqk, cv, conv_w)
    factors = _gate_call(gates_t, fbias.reshape(2 * ML_HEADS, 1), cqk.shape[0])
    return _mlstm_call(qt, k, vt, *factors)


TF = 512


def _merge_kernel(x_ref, ya_ref, o1_ref, o2_ref, o3_ref, l1_ref, l2_ref, l3_ref, hf_ref, hb_ref,
                  co_ref, yd_ref, wg_ref, bg_ref, wbr_ref, wout_ref, mng_ref, lng_ref, lnb_ref,
                  wr_ref, x1_ref, x1p_ref, aff_ref, o_scr, l_scr, *, alpha):
    x = x_ref[0]
    xb = x.astype(BF16)

    def natural_order(src_ref, scr):
        dil, width = src_ref.shape[1], src_ref.shape[3]
        if dil == 1:
            return src_ref[0, 0]
        for r in range(dil):
            for c in range(width // LANES):
                scr[c, pl.ds(r, TF // dil, stride=dil), :] = src_ref[0, r, :, c * LANES:(c + 1) * LANES]
        return jnp.concatenate([scr[c] for c in range(width // LANES)], axis=1)

    lane_head = lax.broadcasted_iota(jnp.int32, (TF, MIX_W), 1) // ML_HD
    l1, l2, l3 = [natural_order(r, l_scr.at[p]) for p, r in enumerate((l1_ref, l2_ref, l3_ref))]
    o1, o2, o3 = [natural_order(r, o_scr.at[p]) for p, r in enumerate((o1_ref, o2_ref, o3_ref))]
    lm = jnp.maximum(jnp.maximum(l1, l2), l3)
    e1, e2, e3 = jnp.exp(l1 - lm), jnp.exp(l2 - lm), jnp.exp(l3 - lm)
    inv = 1.0 / (e1 + e2 + e3)

    def per_head(w):
        out = jnp.zeros((TF, MIX_W), F32)
        for h in range(ATT_HEADS):
            out = jnp.where(lane_head == h, w[:, h:h + 1], out)
        return out

    y_b = per_head(e1 * inv) * o1 + per_head(e2 * inv) * o2 + per_head(e3 * inv) * o3
    hsum_t = jnp.concatenate([hf_ref[0, p] + hb_ref[0, p] for p in range(TF // LANES)], axis=1)
    per_head_rows = hsum_t.reshape(ML_HEADS, ML_HD, TF)
    mu = jnp.mean(per_head_rows, axis=1, keepdims=True)
    cen = per_head_rows - mu
    var = jnp.mean(cen * cen, axis=1, keepdims=True)
    hn_t = (cen * lax.rsqrt(var + LN_EPS)).reshape(MIX_W, TF)
    y_c_t = (jax.nn.sigmoid(jnp.transpose(co_ref[0])) * (hn_t * mng_ref[...])).astype(BF16)
    ys = (ya_ref[0], y_b.astype(BF16), None, yd_ref[0])
    merged = jnp.zeros((TF, D_MODEL), F32)
    for n in range(N_BRANCH):
        cols = slice(n * D_MODEL, (n + 1) * D_MODEL)
        gate = jax.nn.sigmoid(jnp.dot(xb, wg_ref[:, cols], preferred_element_type=F32) + bg_ref[:, cols])
        if ys[n] is None:
            proj = lax.dot_general(y_c_t, wbr_ref[n], (((0,), (0,)), ((), ())), preferred_element_type=F32)
        else:
            proj = jnp.dot(ys[n], wbr_ref[n], preferred_element_type=F32)
        merged = merged + gate * proj
    mix = jnp.dot(merged.astype(BF16), wout_ref[...], preferred_element_type=F32)
    x1 = _standardize(alpha * x + mix) * lng_ref[...] + lnb_ref[...]
    x1_ref[0] = x1
    x1b = x1.astype(BF16)
    x1p_ref[0] = _pack_bf16_pair(x1b[:, :D_MODEL // 2], x1b[:, D_MODEL // 2:])
    logits = lax.dot_general(wr_ref[...], x1b, (((1,), (1,)), ((), ())),
                             preferred_element_type=F32)
    ex = jnp.exp(logits - jnp.max(logits, axis=0, keepdims=True))
    aff_ref[0] = ex / jnp.sum(ex, axis=0, keepdims=True)


def _merge_call(x, ya, o_list, l_list, hf, hb, co, yd, wg, bg, wbr, wout, mng, lng, lnb, wr_t, alpha):
    b, s, _ = x.shape
    tok = lambda w: pl.BlockSpec((1, TF, w), lambda b, i: (b, i, 0))
    grouped = lambda dil, w: pl.BlockSpec((1, dil, TF // dil, w), lambda b, i: (b, 0, i, 0))
    chunked = pl.BlockSpec((1, TF // LANES, MIX_W, LANES), lambda b, i: (b, i, 0, 0))
    const = lambda shp: pl.BlockSpec(shp, lambda b, i: (0,) * len(shp))
    return pl.pallas_call(
        functools.partial(_merge_kernel, alpha=alpha), name="merge_ln_router",
        grid=(b, s // TF),
        in_specs=[tok(D_MODEL), tok(MIX_W)] + [grouped(dil, MIX_W) for _, dil in DIL_PATTERNS]
                 + [grouped(dil, LANES) for _, dil in DIL_PATTERNS]
                 + [chunked, chunked, tok(MIX_W), tok(MIX_W)]
                 + [const((D_MODEL, N_BRANCH * D_MODEL)), const((1, N_BRANCH * D_MODEL)),
                    const((N_BRANCH, MIX_W, D_MODEL)), const((D_MODEL, D_MODEL)), const((MIX_W, 1)),
                    const((1, D_MODEL)), const((1, D_MODEL)), const((N_EXPERTS, D_MODEL))],
        out_specs=(tok(D_MODEL), tok(D_MODEL // 2), pl.BlockSpec((1, N_EXPERTS, TF), lambda b, i: (b, 0, i))),
        out_shape=(jax.ShapeDtypeStruct((b, s, D_MODEL), F32),
                   jax.ShapeDtypeStruct((b, s, D_MODEL // 2), jnp.int32),
                   jax.ShapeDtypeStruct((b, N_EXPERTS, s), F32)),
        scratch_shapes=[pltpu.VMEM((len(DIL_PATTERNS), MIX_W // LANES, TF, LANES), F32),
                        pltpu.VMEM((len(DIL_PATTERNS), 1, TF, LANES), F32)],
        compiler_params=_cparams(2),
    )(x, ya, *o_list, *l_list, hf, hb, co, yd, wg, bg, wbr, wout, mng, lng, lnb, wr_t)


TT = 256


def _select_kernel(aff_ref, slot_ref, *, cap):
    s = aff_ref.shape[2]
    bits = pltpu.bitcast(aff_ref[0], jnp.int32)

    def bit_step(i, thr):
        cand = thr | jnp.left_shift(jnp.int32(1), 30 - i)
        cnt = jnp.sum((bits >= cand).astype(jnp.int32), axis=1, keepdims=True)
        return jnp.where(cnt >= cap, cand, thr)

    thr = lax.fori_loop(0, 31, bit_step, jnp.zeros((N_EXPERTS, 1), jnp.int32))
    gt = bits > thr
    eq = bits == thr
    need = (cap - jnp.sum(gt.astype(jnp.int32), axis=1, keepdims=True)).astype(F32)
    upper = (lax.broadcasted_iota(jnp.int32, (TT, TT), 0)
             <= lax.broadcasted_iota(jnp.int32, (TT, TT), 1)).astype(BF16)
    eq_before = jnp.zeros((N_EXPERTS, 1), F32)
    sel_before = jnp.zeros((N_EXPERTS, 1), F32)
    for j in range(s // TT):
        cols = slice(j * TT, (j + 1) * TT)
        eq_j = eq[:, cols]
        eq_incl = eq_before + jnp.dot(eq_j.astype(BF16), upper, preferred_element_type=F32)
        sel_j = gt[:, cols] | (eq_j & (eq_incl <= need))
        sel_f = sel_j.astype(F32)
        sel_incl = sel_before + jnp.dot(sel_f.astype(BF16), upper, preferred_element_type=F32)
        slot_ref[0, :, cols] = jnp.where(sel_j, sel_incl - 1.0, -1.0).astype(jnp.int32)
        eq_before = eq_incl[:, TT - 1:TT]
        sel_before = sel_incl[:, TT - 1:TT]


def _select_call(aff_t, cap):
    b, e, s = aff_t.shape
    return pl.pallas_call(
        functools.partial(_select_kernel, cap=cap), name="expert_choice_select",
        grid=(b,),
        in_specs=[pl.BlockSpec((1, e, s), lambda i: (i, 0, 0))],
        out_specs=pl.BlockSpec((1, e, s), lambda i: (i, 0, 0)),
        out_shape=jax.ShapeDtypeStruct((b, e, s), jnp.int32),
        compiler_params=_cparams(1),
    )(aff_t)


SC_LANES = 16
SC_ROWS = 64
SC_IDX = 128
SC_SLAB = 128
SC_ZROWS = 64
CF = 1024


def _sc_dispatch_call(x_flat, slot2, aff2, seq, cap):
    n_pairs = slot2.shape[0]
    d = x_flat.shape[1]
    info = plsc.get_sparse_core_info()
    n_workers = info.num_cores * info.num_subcores
    assert n_pairs % n_workers == 0 and seq % SC_LANES == 0 and cap % (2 * SC_ROWS) == 0
    pairs_per_worker = n_pairs // n_workers
    mesh = plsc.VectorSubcoreMesh(core_axis_name="c", subcore_axis_name="s")

    @functools.partial(
        pl.kernel, mesh=mesh, name="expert_dispatch_sc",
        compiler_params=pltpu.CompilerParams(needs_layout_passes=False),
        out_type=(jax.ShapeDtypeStruct((n_pairs * cap, d), x_flat.dtype),
                  jax.ShapeDtypeStruct((n_pairs, cap // SC_IDX, SC_IDX), jnp.int32),
                  jax.ShapeDtypeStruct((n_pairs, cap // SC_IDX, SC_IDX), F32)),
        scratch_types=[pltpu.VMEM((seq,), jnp.int32), pltpu.VMEM((seq,), F32),
                       pltpu.VMEM((cap,), jnp.int32), pltpu.VMEM((cap // SC_IDX, SC_IDX), jnp.int32),
                       pltpu.VMEM((cap // SC_IDX, SC_IDX), F32),
                       pltpu.VMEM((2, SC_ROWS, d), x_flat.dtype), pltpu.SemaphoreType.DMA((2,))])
    def dispatch(x_hbm, slot_hbm, aff_hbm, xs_hbm, tok_hbm, gate_hbm,
                 slot_v, aff_v, idx_v, tok_v, gate_v, rows_v, sem):
        worker = lax.axis_index("s") * info.num_cores + lax.axis_index("c")
        lane = lax.iota(jnp.int32, SC_LANES)

        def gather(c0, buf):
            return pltpu.make_async_copy(x_hbm.at[idx_v.at[pl.ds(c0, SC_ROWS)]], rows_v.at[buf], sem.at[buf])
        for k in range(pairs_per_worker):
            pair = worker * pairs_per_worker + k
            row0 = (pair // N_EXPERTS) * seq
            pltpu.sync_copy(slot_hbm.at[pair], slot_v)
            pltpu.sync_copy(aff_hbm.at[pair], aff_v)

            @plsc.parallel_loop(0, seq, step=SC_LANES, unroll=4)
            def _(t0):
                sv = slot_v[pl.ds(t0, SC_LANES)]
                picked = sv >= 0
                hi, lo = lax.shift_right_logical(sv, 7), sv & (SC_IDX - 1)
                plsc.store_scatter(tok_v, [hi, lo], t0 + lane, mask=picked)
                plsc.store_scatter(idx_v, [sv], row0 + t0 + lane, mask=picked)
                plsc.store_scatter(gate_v, [hi, lo], aff_v[pl.ds(t0, SC_LANES)], mask=picked)

            pltpu.sync_copy(tok_v, tok_hbm.at[pair])
            pltpu.sync_copy(gate_v, gate_hbm.at[pair])

            gather(0, 0).start()

            @pl.loop(0, cap, step=2 * SC_ROWS)
            def _(c0):
                gather(c0 + SC_ROWS, 1).start()
                gather(c0, 0).wait()
                pltpu.sync_copy(rows_v.at[0], xs_hbm.at[pl.ds(pair * cap + c0, SC_ROWS)])

                @pl.when(c0 + 2 * SC_ROWS < cap)
                def _():
                    gather(c0 + 2 * SC_ROWS, 0).start()

                gather(c0 + SC_ROWS, 1).wait()
                pltpu.sync_copy(rows_v.at[1], xs_hbm.at[pl.ds(pair * cap + c0 + SC_ROWS, SC_ROWS)])

    return dispatch(x_flat, slot2, aff2)


def _expert_kernel(xs_ref, g_ref, w1_ref, w3_ref, w2_ref, ye_ref, w1_bf, w3_bf, w2_bf):
    @pl.when((pl.program_id(1) == 0) & (pl.program_id(2) == 0))
    def _():
        w1_bf[...] = w1_ref[0, 0].astype(BF16)
        w3_bf[...] = w3_ref[0, 0].astype(BF16)
        w2_bf[...] = w2_ref[0, 0].astype(BF16)

    xs = jnp.concatenate(_unpack_bf16_pair(xs_ref[0, 0]), axis=1)
    hid = (jax.nn.silu(jnp.dot(xs, w1_bf[...], preferred_element_type=F32))
           * jnp.dot(xs, w3_bf[...], preferred_element_type=F32))
    g_rows = g_ref[0, 0]
    n_rows = g_rows.shape[0]
    g_t = jnp.transpose(jnp.concatenate([g_rows, jnp.zeros((LANES - n_rows, LANES), F32)], axis=0))
    g_col = jnp.concatenate([g_t[:, r:r + 1] for r in range(n_rows)], axis=0)
    ye_ref[0, 0] = jnp.dot(hid.astype(BF16), w2_bf[...], preferred_element_type=F32) * g_col


def _expert_call(xs4, gate4, w1, w3, w2, layer):
    b, e, cap, half = xs4.shape
    d, ff = w1.shape[2], w1.shape[3]
    assert d == 2 * half
    rows = lambda w: pl.BlockSpec((1, 1, CF, w), lambda e, b, j: (b, e, j, 0))
    wspec = lambda r, c: pl.BlockSpec((1, 1, r, c), lambda e, b, j: (layer, e, 0, 0))
    return pl.pallas_call(
        _expert_kernel, name="expert_ffn",
        grid=(e, b, cap // CF),
        in_specs=[rows(half), pl.BlockSpec((1, 1, CF // LANES, LANES), lambda e, b, j: (b, e, j, 0)),
                  wspec(d, ff), wspec(d, ff), wspec(ff, d)],
        out_specs=rows(d),
        out_shape=jax.ShapeDtypeStruct((b, e, cap, d), F32),
        scratch_shapes=[pltpu.VMEM((d, ff), BF16), pltpu.VMEM((d, ff), BF16), pltpu.VMEM((ff, d), BF16)],
        compiler_params=_cparams(3),
    )(xs4, gate4, w1, w3, w2)


def _sc_combine_call(ye_flat, tok3, seq):
    n_pairs, n_chunks, _ = tok3.shape
    cap = n_chunks * SC_IDX
    d = ye_flat.shape[1]
    nb = n_pairs // N_EXPERTS
    info = plsc.get_sparse_core_info()
    assert info.num_subcores == N_EXPERTS and nb % info.num_cores == 0 and n_chunks % 2 == 0
    assert seq % (info.num_subcores * SC_ZROWS) == 0 and d % SC_SLAB == 0
    batches_per_core = nb // info.num_cores
    own_rows = seq // info.num_subcores
    mesh = plsc.VectorSubcoreMesh(core_axis_name="c", subcore_axis_name="s")

    @functools.partial(
        pl.kernel, mesh=mesh, name="expert_combine_sc",
        compiler_params=pltpu.CompilerParams(needs_layout_passes=False),
        out_type=jax.ShapeDtypeStruct((nb * seq, d), F32),
        scratch_types=[pltpu.VMEM_SHARED((seq, SC_SLAB), F32),
                       pltpu.VMEM((n_chunks, SC_IDX), jnp.int32),
                       pltpu.VMEM((2, SC_IDX, SC_SLAB), F32),
                       pltpu.VMEM((SC_ZROWS, SC_SLAB), F32),
                       pltpu.SemaphoreType.DMA((2,))])
    def combine(ye_hbm, tok_hbm, out_hbm, acc_sh, tok_v, rows_v, zero_v, sem):
        core = lax.axis_index("c")
        sub = lax.axis_index("s")

        @pl.loop(0, SC_ZROWS)
        def _(r):
            for l0 in range(0, SC_SLAB, SC_LANES):
                zero_v[r, pl.ds(l0, SC_LANES)] = jnp.zeros((SC_LANES,), F32)

        for bb in range(batches_per_core):
            batch = core * batches_per_core + bb
            pair = batch * N_EXPERTS + sub
            pltpu.sync_copy(tok_hbm.at[pair], tok_v)

            @pl.loop(0, d // SC_SLAB)
            def _(slab):
                cols = pl.ds(pl.multiple_of(slab * SC_SLAB, SC_SLAB), SC_SLAB)

                @pl.loop(0, own_rows, step=SC_ZROWS)
                def _(r0):
                    pltpu.sync_copy(zero_v, acc_sh.at[pl.ds(sub * own_rows + r0, SC_ZROWS)])

                def load(j, buf):
                    return pltpu.make_async_copy(
                        ye_hbm.at[pl.ds(pair * cap + j * SC_IDX, SC_IDX), cols], rows_v.at[buf], sem.at[buf])

                load(0, 0).start()
                plsc.subcore_barrier()

                for j in range(0, n_chunks, 2):
                    load(j + 1, 1).start()
                    load(j, 0).wait()
                    pltpu.sync_copy(rows_v.at[0], acc_sh.at[tok_v.at[j]], add=True)
                    if j + 2 < n_chunks:
                        load(j + 2, 0).start()
                    load(j + 1, 1).wait()
                    pltpu.sync_copy(rows_v.at[1], acc_sh.at[tok_v.at[j + 1]], add=True)

                plsc.subcore_barrier()
                pltpu.sync_copy(acc_sh.at[pl.ds(sub * own_rows, own_rows)],
                                out_hbm.at[pl.ds(batch * seq + sub * own_rows, own_rows), cols])

    return combine(ye_flat, tok3)


TN = 512


def _resln_kernel(x_ref, y_ref, g_ref, b_ref, o_ref, *, alpha):
    o_ref[...] = _standardize(alpha * x_ref[...] + y_ref[...]) * g_ref[...] + b_ref[...]


def _resln_call(x2d, y2d, g, bta, alpha):
    n, d = x2d.shape
    tok = pl.BlockSpec((TN, d), lambda i: (i, 0))
    vec = pl.BlockSpec((1, d), lambda i: (0, 0))
    return pl.pallas_call(
        functools.partial(_resln_kernel, alpha=alpha), name="residual_layernorm",
        grid=(n // TN,), in_specs=[tok, tok, vec, vec], out_specs=tok,
        out_shape=jax.ShapeDtypeStruct((n, d), F32),
        compiler_params=_cparams(1),
    )(x2d, y2d, g, bta)


def _expert_choice_ffn(x1p, aff_t, w1, w3, w2, layer):
    b, s, half = x1p.shape
    d = 2 * half
    cap = EC_FACTOR * s // N_EXPERTS
    slot = _select_call(aff_t, cap)
    xs, tok, gate = _sc_dispatch_call(x1p.reshape(b * s, half), slot.reshape(b * N_EXPERTS, s),
                                      aff_t.reshape(b * N_EXPERTS, s), s, cap)
    ye = _expert_call(xs.reshape(b, N_EXPERTS, cap, half), gate.reshape(b, N_EXPERTS, cap // SC_IDX, SC_IDX), w1, w3, w2,
                      layer)
    out = _sc_combine_call(ye.reshape(b * N_EXPERTS * cap, d),
                           tok, s)
    return out.reshape(b, s, d)


def _pack_pool(pool_w):
    g, gd, _ = pool_w.shape
    out = jnp.zeros((g * gd, g * gd), F32)
    for i in range(g):
        out = out.at[i * gd:(i + 1) * gd, i * gd:(i + 1) * gd].set(pool_w[i])
    return out.astype(BF16)


def _layer(layer, x, pending, alpha, bias_tiles, w_in, b_in, gm_ln_g, gm_ws, gm_bs, ml_conv, ml_fbias,
           ml_norm_g, pool_w, pool_scale, w_branch, w_out, ln1_g, ln1_b, w_router, w_e1, w_e3, w_e2):
    b, s, d = x.shape
    n_small = 2576
    w_cat, b_cat = _pack_inproj_weights(w_in, b_in)
    wscat, bsfull = _pack_gmlp(gm_ws, gm_bs)
    if pending is None:
        outs = _inproj_call(x.reshape(b * s, d), w_cat, b_cat, gm_ln_g[None], wscat, bsfull, b)
    else:
        *outs, x2 = _inproj_call(pending, w_cat, b_cat, gm_ln_g[None], wscat, bsfull, b, alpha)
        x = x2.reshape(b, s, d)
    ya, qkv1, qkv4, qkv16, cqk, cv, co, dx, gates_t = outs
    r3 = lambda t: t.reshape(b, s, t.shape[-1])
    o_list, l_list = [], []
    for qkv, bias in zip((qkv1, qkv4, qkv16), bias_tiles):
        o, lse = _attn_call(qkv, bias)
        o_list.append(o)
        l_list.append(lse)
    hf, hb = _mlstm_branch(r3(cqk), r3(cv), gates_t, ml_conv, ml_fbias)
    yd = _pool_call(r3(dx), _pack_pool(pool_w), pool_scale[None])
    x1, x1p, aff_t = _merge_call(
        x, r3(ya), o_list, l_list, hf, hb, r3(co), yd,
        w_in[:, n_small:].astype(BF16), b_in[None, n_small:], w_branch.astype(BF16), w_out.astype(BF16),
        ml_norm_g[:, None], ln1_g[None], ln1_b[None], jnp.transpose(w_router).astype(BF16), alpha)
    ffn = _expert_choice_ffn(x1p, aff_t, w_e1, w_e3, w_e2, layer)
    return x1.reshape(b * s, d), ffn.reshape(b * s, d)


def kernel(x, w_in, b_in, gm_ln_g, gm_ws, gm_bs, rel_bias, ml_conv, ml_fbias, ml_norm_g, pool_w,
           pool_scale, w_branch, w_out, ln1_g, ln1_b, w_router, w_e1, w_e3, w_e2, ln2_g, ln2_b):
    depth = w_in.shape[0]
    alpha = (2 * depth) ** 0.25
    bias_tiles = [_attn_bias_tile(rel_bias, window, dil) for window, dil in DIL_PATTERNS]
    b, s, d = x.shape
    pending = None
    for l in range(depth):
        x1, ffn = _layer(l, x, pending, alpha, bias_tiles, w_in[l], b_in[l], gm_ln_g[l], gm_ws[l], gm_bs[l],
                         ml_conv[l], ml_fbias[l], ml_norm_g[l], pool_w[l], pool_scale[l], w_branch[l],
                         w_out[l], ln1_g[l], ln1_b[l], w_router[l], w_e1, w_e3, w_e2)
        pending = (x1, ffn, ln2_g[l][None], ln2_b[l][None])
    return _resln_call(*pending, alpha).reshape(b, s, d)
```

```python
import functools
import math

import jax
import jax.numpy as jnp
import numpy as np
from jax import lax
from jax.experimental import pallas as pl
from jax.experimental.pallas import tpu as pltpu
from jax.experimental.pallas import tpu_sc as plsc

F32 = jnp.float32
BF16 = jnp.bfloat16

D_MODEL = 1024
MIX_W = 256
N_BRANCH = 4
GM_CHUNK = 128
GM_GROUPS = 4
ATT_HEADS = 4
ATT_HD = 64
DIL_PATTERNS = ((128, 1), (512, 4), (2048, 16))
ATT_BLOCK = 64
REL_BUCKETS = 32
REL_MAX_DIST = 1024
ML_HEADS = 4
ML_HD = 64
ML_CHUNK = 64
POOL_WINDOWS = (2, 4, 8, 16)
N_EXPERTS = 16
EXPERT_FF = 1024
EC_FACTOR = 2
LN_EPS = 1e-5
NEG_BIG = -1e30

V7X_VMEM_LIMIT = 56 * 1024 * 1024
LANES = 128
HALO = 8


def _cparams(n_grid, vmem=V7X_VMEM_LIMIT):
    return pltpu.CompilerParams(dimension_semantics=("arbitrary",) * n_grid,
                                vmem_limit_bytes=vmem)


def _pack_bf16_pair(lo, hi):
    lo_bits = lax.shift_right_logical(pltpu.bitcast(lo.astype(F32), jnp.int32), 16)
    return pltpu.bitcast(hi.astype(F32), jnp.int32) | lo_bits


def _unpack_bf16_pair(packed):
    lo = pltpu.bitcast(lax.shift_left(packed, 16), F32).astype(BF16)
    hi = pltpu.bitcast(packed & jnp.int32(-65536), F32).astype(BF16)
    return lo, hi


def _standardize(xf):
    mu = jnp.mean(xf, axis=-1, keepdims=True)
    var = jnp.mean(jnp.square(xf - mu), axis=-1, keepdims=True)
    return (xf - mu) * lax.rsqrt(var + LN_EPS)


TA = 512
A_COLS = 2560 + LANES


def _inproj_kernel(*refs, alpha):
    if alpha is None:
        x_ref, *refs = refs
        x = x_ref[...]
    else:
        x1_ref, y_ref, g2_ref, b2_ref, *refs = refs
        x = _standardize(alpha * x1_ref[...] + y_ref[...]) * g2_ref[...] + b2_ref[...]
        refs[-2][...] = x
        refs = refs[:-2] + refs[-1:]
    (w_ref, b_ref, lng_ref, wscat_ref, bsfull_ref,
     ya_ref, qkv1_ref, qkv4_ref, qkv16_ref, cqk_ref, cv_ref, co_ref, dx_ref, gt_ref, qkv_scr) = refs
    xb = x.astype(BF16)
    h = jnp.dot(xb, w_ref[...], preferred_element_type=F32) + b_ref[...]
    qkv1_ref[0, 0] = h[:, 512:1280].astype(BF16)
    for c in range(768 // LANES):
        qkv_scr[c] = h[:, 512 + c * LANES:512 + (c + 1) * LANES]
    for (_, dil), out_ref in zip(DIL_PATTERNS[1:], (qkv4_ref, qkv16_ref)):
        for r in range(dil):
            for c in range(768 // LANES):
                out_ref[0, r, :, c * LANES:(c + 1) * LANES] = (
                    qkv_scr[c, pl.ds(r, TA // dil, stride=dil), :].astype(BF16))
    cqk_ref[...] = h[:, 1280:1792]
    cv_ref[...] = h[:, 1792:2048].astype(BF16)
    co_ref[...] = h[:, 2048:2304]
    dx_ref[...] = h[:, 2304:2560]
    gates_t = jnp.transpose(h[:, 2560:2688])
    for j in range(TA // LANES):
        gt_ref[j] = gates_t[0:4 * ML_HEADS, j * LANES:(j + 1) * LANES]
    u = jax.nn.gelu(h[:, 0:256])
    v = jax.nn.gelu(h[:, 256:512])
    vn = _standardize(v) * lng_ref[...]
    lane_grp = lax.broadcasted_iota(jnp.int32, (GM_CHUNK, MIX_W), 1) // (MIX_W // GM_GROUPS)
    for c in range(TA // GM_CHUNK):
        vc = vn[c * GM_CHUNK:(c + 1) * GM_CHUNK]
        stacked = jnp.concatenate(
            [jnp.where(lane_grp == g, vc, 0.0).astype(BF16) for g in range(GM_GROUPS)], axis=0)
        mixed = jnp.dot(wscat_ref[...], stacked, preferred_element_type=F32) + bsfull_ref[...]
        ya_ref[c * GM_CHUNK:(c + 1) * GM_CHUNK, :] = (
            u[c * GM_CHUNK:(c + 1) * GM_CHUNK] * mixed).astype(BF16)


def _inproj_call(x_in, w_cat, b_cat, lng, wscat, bsfull, batch, alpha=None):
    fused = alpha is not None
    n = (x_in[0] if fused else x_in).shape[0]
    seq = n // batch
    tpb = seq // TA
    tok = lambda w: pl.BlockSpec((TA, w), lambda i: (i, 0))
    const = lambda s: pl.BlockSpec(s, lambda i: (0,) * len(s))
    regrouped = lambda dil: pl.BlockSpec((1, dil, TA // dil, 768), lambda i: (i // tpb, 0, i % tpb, 0))
    out_shape = (
        jax.ShapeDtypeStruct((n, 256), BF16),
    ) + tuple(jax.ShapeDtypeStruct((batch, dil, seq // dil, 768), BF16)
              for _, dil in DIL_PATTERNS) + (
        jax.ShapeDtypeStruct((n, 512), F32),
        jax.ShapeDtypeStruct((n, 256), BF16),
        jax.ShapeDtypeStruct((n, 256), F32),
        jax.ShapeDtypeStruct((n, 256), F32),
        jax.ShapeDtypeStruct((n // LANES, 4 * ML_HEADS, LANES), F32),
    )
    x_specs = [tok(D_MODEL), tok(D_MODEL), const((1, D_MODEL)), const((1, D_MODEL))] if fused else [tok(D_MODEL)]
    out_specs = ((tok(256),) + tuple(regrouped(dil) for _, dil in DIL_PATTERNS)
                 + (tok(512), tok(256), tok(256), tok(256),
                    pl.BlockSpec((TA // LANES, 4 * ML_HEADS, LANES), lambda i: (i, 0, 0))))
    if fused:
        out_specs += (tok(D_MODEL),)
        out_shape += (jax.ShapeDtypeStruct((n, D_MODEL), F32),)
    return pl.pallas_call(
        functools.partial(_inproj_kernel, alpha=alpha), name="inproj_gmlp",
        grid=(n // TA,),
        in_specs=x_specs + [const((D_MODEL, A_COLS)), const((1, A_COLS)), const((1, MIX_W)),
                            const((GM_CHUNK, GM_GROUPS * GM_CHUNK)), const((GM_CHUNK, MIX_W))],
        out_specs=out_specs,
        out_shape=out_shape,
        scratch_shapes=[pltpu.VMEM((768 // LANES, TA, LANES), F32)],
        compiler_params=_cparams(1),
    )(*(x_in if fused else (x_in,)), w_cat, b_cat, lng, wscat, bsfull)


def _pack_inproj_weights(w_in, b_in):
    pad = lambda a: jnp.pad(a, ((0, 0), (0, LANES - 4 * ML_HEADS)))
    w_cat = jnp.concatenate([w_in[:, 0:2304], w_in[:, 2320:2576], pad(w_in[:, 2304:2320])], axis=1)
    b2 = b_in[None, :]
    b_cat = jnp.concatenate([b2[:, 0:2304], b2[:, 2320:2576], pad(b2[:, 2304:2320])], axis=1)
    return w_cat.astype(BF16), b_cat


def _pack_gmlp(gm_ws, gm_bs):
    wscat = jnp.transpose(gm_ws, (1, 0, 2)).reshape(GM_CHUNK, GM_GROUPS * GM_CHUNK).astype(BF16)
    bsfull = jnp.repeat(jnp.transpose(gm_bs), MIX_W // GM_GROUPS, axis=1)
    return wscat, bsfull


def _halo_specs(t, width, n_tiles):
    r = t // HALO
    main = pl.BlockSpec((1, t, width), lambda b, i: (b, i, 0))
    prev = pl.BlockSpec((1, HALO, width), lambda b, i: (b, jnp.maximum(i * r - 1, 0), 0))
    nxt = pl.BlockSpec((1, HALO, width), lambda b, i: (b, jnp.minimum((i + 1) * r, n_tiles * r - 1), 0))
    return main, prev, nxt


def _fill_halo_scratch(buf, x_ref, p_ref, n_ref, t):
    i = pl.program_id(1)
    last = pl.num_programs(1) - 1
    buf[0:HALO, :] = jnp.where(i > 0, p_ref[0], 0.0)
    buf[HALO:HALO + t, :] = x_ref[0]
    buf[HALO + t:2 * HALO + t, :] = jnp.where(i < last, n_ref[0], 0.0)


TP = 512


def _pool_kernel(x_ref, p_ref, n_ref, w_ref, sc_ref, o_ref, buf, lvl):
    _fill_halo_scratch(buf, x_ref, p_ref, n_ref, TP)
    seq = pl.num_programs(1) * TP
    pos = pl.program_id(1) * TP + lax.broadcasted_iota(jnp.int32, (TP, 1), 0)
    lane_grp = lax.broadcasted_iota(jnp.int32, (TP, MIX_W), 1) // (MIX_W // len(POOL_WINDOWS))
    x0 = buf[HALO:HALO + TP, :]
    sums = []
    src, rows = buf, TP + 2 * HALO
    for k, win in enumerate(POOL_WINDOWS):
        half = win // 2
        rows -= half
        cur = src[0:rows, :] + src[half:rows + half, :] if k else buf[0:rows, :] + buf[1:rows + 1, :]
        if k + 1 < len(POOL_WINDOWS):
            lvl[k, 0:rows, :] = cur
            sums.append(lvl[k, HALO - half:HALO - half + TP, :])
            src = lvl.at[k]
        else:
            sums.append(cur[0:TP])
    pooled = jnp.zeros((TP, MIX_W), F32)
    for gi, win in enumerate(POOL_WINDOWS):
        half = win // 2
        cnt = (jnp.minimum(pos + half, seq) - jnp.maximum(pos - half, 0)).astype(F32)
        pooled = jnp.where(lane_grp == gi, sums[gi] / cnt - x0, pooled)
    mixed = jnp.dot(pooled.astype(BF16), w_ref[...], preferred_element_type=F32)
    o_ref[0] = (mixed * sc_ref[...]).astype(BF16)


def _pool_call(dx, w_block, scale):
    b, s, _ = dx.shape
    nt = s // TP
    main, prev, nxt = _halo_specs(TP, MIX_W, nt)
    return pl.pallas_call(
        _pool_kernel, name="pool_mixer",
        grid=(b, nt),
        in_specs=[main, prev, nxt,
                  pl.BlockSpec((MIX_W, MIX_W), lambda b, i: (0, 0)),
                  pl.BlockSpec((1, MIX_W), lambda b, i: (0, 0))],
        out_specs=pl.BlockSpec((1, TP, MIX_W), lambda b, i: (b, i, 0)),
        out_shape=jax.ShapeDtypeStruct((b, s, MIX_W), BF16),
        scratch_shapes=[pltpu.VMEM((TP + 2 * HALO, MIX_W), F32),
                        pltpu.VMEM((len(POOL_WINDOWS) - 1, TP + 2 * HALO, MIX_W), F32)],
        compiler_params=_cparams(2),
    )(dx, dx, dx, w_block, scale)


TQ = 128
TQS = 512
TKEYS = TQ + 2 * ATT_BLOCK


def _attn_kernel(q_ref, kp_ref, km_ref, kn_ref, vp_ref, vm_ref, vn_ref, bias_ref, o_ref, lse_ref):
    i = pl.program_id(2)
    q = q_ref[0, 0] * ATT_HD ** -0.5
    k = jnp.concatenate([kp_ref[0, 0], km_ref[0, 0], kn_ref[0, 0]], axis=0)
    v = jnp.concatenate([vp_ref[0, 0], vm_ref[0, 0], vn_ref[0, 0]], axis=0)
    lane = lax.broadcasted_iota(jnp.int32, (TQ, LANES), 1)
    lane_half = lax.broadcasted_iota(jnp.int32, (1, LANES), 1) // ATT_HD
    keep = [jnp.where(lane_half == hh, 1.0, 0.0).astype(BF16) for hh in range(2)]
    n_sub = TQS // TQ
    last_step = pl.num_programs(2) - 1
    for j in range(n_sub):
        if j == 0:
            variant = jnp.where(i == 0, 0, 1)
        elif j == n_sub - 1:
            variant = jnp.where(i == last_step, 2, 1)
        else:
            variant = 1
        qrows = slice(j * TQ, (j + 1) * TQ)
        krows = slice(j * TQ, j * TQ + TKEYS)
        lse_tile = jnp.zeros((TQ, LANES), F32)
        for pair in range(ATT_HEADS // 2):
            grp = slice(pair * LANES, (pair + 1) * LANES)
            q_pair, k_pair, v_pair = q[qrows, grp], k[krows, grp], v[krows, grp]
            o_pair = jnp.zeros((TQ, LANES), F32)
            for hh in range(2):
                h = 2 * pair + hh
                logits = lax.dot_general(q_pair * keep[hh], k_pair, (((1,), (1,)), ((), ())),
                                         preferred_element_type=F32) + bias_ref[variant, h]
                m = jnp.max(logits, axis=-1, keepdims=True)
                p = jnp.exp(logits - m)
                ssum = jnp.sum(p, axis=-1, keepdims=True)
                o = jnp.dot(p.astype(BF16), v_pair, preferred_element_type=F32) / ssum
                o_pair = jnp.where(lane_half == hh, o, o_pair)
                lse_tile = jnp.where(lane == h, m + jnp.log(ssum), lse_tile)
            o_ref[0, 0, qrows, grp] = o_pair
        lse_ref[0, 0, qrows, :] = lse_tile


def _attn_call(qkv, bias):
    b, dil, l, _ = qkv.shape
    nt = l // TQS
    r64 = TQS // ATT_BLOCK
    main = lambda c: pl.BlockSpec((1, 1, TQS, MIX_W), lambda b, r, i: (b, r, i, c))
    prev = lambda c: pl.BlockSpec((1, 1, ATT_BLOCK, MIX_W),
                                  lambda b, r, i: (b, r, jnp.maximum(i * r64 - 1, 0), c))
    nxt = lambda c: pl.BlockSpec((1, 1, ATT_BLOCK, MIX_W),
                                 lambda b, r, i: (b, r, jnp.minimum((i + 1) * r64, nt * r64 - 1), c))
    return pl.pallas_call(
        _attn_kernel, name="band_attention",
        grid=(b, dil, nt),
        in_specs=[main(0), prev(1), main(1), nxt(1), prev(2), main(2), nxt(2),
                  pl.BlockSpec((3, ATT_HEADS, TQ, TKEYS), lambda b, r, i: (0, 0, 0, 0))],
        out_specs=(pl.BlockSpec((1, 1, TQS, MIX_W), lambda b, r, i: (b, r, i, 0)),
                   pl.BlockSpec((1, 1, TQS, LANES), lambda b, r, i: (b, r, i, 0))),
        out_shape=(jax.ShapeDtypeStruct((b, dil, l, MIX_W), F32),
                   jax.ShapeDtypeStruct((b, dil, l, LANES), F32)),
        compiler_params=_cparams(3),
    )(qkv, qkv, qkv, qkv, qkv, qkv, qkv, bias)


def _t5_bucket_static(rel):
    half = REL_BUCKETS // 2
    max_exact = half // 2
    ret = np.where(rel > 0, half, 0)
    n = np.abs(rel)
    nf = np.maximum(n, 1).astype(np.float32)
    large = max_exact + (np.log(nf / np.float32(max_exact)) / np.float32(math.log(REL_MAX_DIST / max_exact))
                         * np.float32(half - max_exact)).astype(np.int32)
    large = np.minimum(large, half - 1)
    return ret + np.where(n < max_exact, n, large)


def _attn_bias_tile(rel_bias, window, dil):
    side = (window // 2) // dil
    rel = np.arange(TKEYS)[None, :] - ATT_BLOCK - np.arange(TQ)[:, None]
    n_rel = TKEYS + TQ - 1
    rel_values = np.arange(n_rel) - (ATT_BLOCK + TQ - 1)
    onehot = jax.nn.one_hot(jnp.asarray(_t5_bucket_static(dil * rel_values), jnp.int32), REL_BUCKETS, dtype=F32)
    table = jnp.einsum('nr,rh->hn', onehot, rel_bias, precision=lax.Precision.HIGHEST)
    periodic = jnp.tile(jnp.pad(table, ((0, 0), (0, 1))), (1, TQ))[:, :TQ * n_rel]
    bias = periodic.reshape(ATT_HEADS, TQ, n_rel)[:, :, TQ - 1:]
    key = np.arange(TKEYS)[None, :]
    inside = np.abs(rel) <= side
    masks = np.stack([inside & (key >= ATT_BLOCK), inside, inside & (key < ATT_BLOCK + TQ)])
    return jnp.where(jnp.asarray(masks)[:, None], bias[None], NEG_BIG)


TM = 512
VT_ROWS = ML_HD + 16


def _mlprep_kernel(x_ref, p_ref, n_ref, v_ref, w_ref, qt_out, k_out, vt_out, buf):
    _fill_halo_scratch(buf, x_ref, p_ref, n_ref, TM)
    conv = (buf[HALO - 1:HALO - 1 + TM, :] * w_ref[0:1, :] + buf[HALO:HALO + TM, :] * w_ref[1:2, :]
            + buf[HALO + 1:HALO + 1 + TM, :] * w_ref[2:3, :])
    qk = jax.nn.silu(conv)
    qt = jnp.transpose(qk[:, :MIX_W])
    vt = jnp.transpose(v_ref[0].astype(F32))
    ones_rows = jnp.where(lax.broadcasted_iota(jnp.int32, (VT_ROWS - ML_HD, LANES), 0) == 0, 1.0, 0.0)
    for h in range(ML_HEADS):
        sl = slice(h * ML_HD, (h + 1) * ML_HD)
        k_out[0, h] = (qk[:, MIX_W + h * ML_HD:MIX_W + (h + 1) * ML_HD] * ML_HD ** -0.5).astype(BF16)
        for p in range(TM // LANES):
            pl_ = slice(p * LANES, (p + 1) * LANES)
            qt_out[0, h, p] = qt[sl, pl_].astype(BF16)
            vt_out[0, h, p] = jnp.concatenate([vt[sl, pl_], ones_rows], axis=0).astype(BF16)


def _mlprep_call(cqk, cv, conv_w):
    b, s, _ = cqk.shape
    nt = s // TM
    n_pairs = s // LANES
    ppt = TM // LANES
    main, prev, nxt = _halo_specs(TM, 2 * MIX_W, nt)
    return pl.pallas_call(
        _mlprep_kernel, name="mlstm_prep",
        grid=(b, nt),
        in_specs=[main, prev, nxt,
                  pl.BlockSpec((1, TM, MIX_W), lambda b, i: (b, i, 0)),
                  pl.BlockSpec((3, 2 * MIX_W), lambda b, i: (0, 0))],
        out_specs=(pl.BlockSpec((1, ML_HEADS, ppt, ML_HD, LANES), lambda b, i: (b, 0, i, 0, 0)),
                   pl.BlockSpec((1, ML_HEADS, TM, ML_HD), lambda b, i: (b, 0, i, 0)),
                   pl.BlockSpec((1, ML_HEADS, ppt, VT_ROWS, LANES), lambda b, i: (b, 0, i, 0, 0))),
        out_shape=(jax.ShapeDtypeStruct((b, ML_HEADS, n_pairs, ML_HD, LANES), BF16),
                   jax.ShapeDtypeStruct((b, ML_HEADS, s, ML_HD), BF16),
                   jax.ShapeDtypeStruct((b, ML_HEADS, n_pairs, VT_ROWS, LANES), BF16)),
        scratch_shapes=[pltpu.VMEM((TM + 2 * HALO, 2 * MIX_W), F32)],
        compiler_params=_cparams(2),
    )(cqk, cqk, cqk, cv, conv_w)


def _gate_kernel(g_ref, fb_ref, a_ref, m_ref, iw_ref, en_ref, ws_ref, dec_ref):
    x = g_ref[...]
    n_pairs = x.shape[0]
    n_ch = 2 * ML_HEADS
    lane = lax.broadcasted_iota(jnp.int32, (1, 1, LANES), 2)
    t_in = lane % ML_CHUNK
    second = lane >= ML_CHUNK
    fwd_row = lax.broadcasted_iota(jnp.int32, (1, n_ch, 1), 1) < ML_HEADS
    li = x[:, 0:n_ch, :]
    z = x[:, n_ch:2 * n_ch, :] + fb_ref[...]
    lf = jnp.minimum(z, 0.0) - jnp.log1p(jnp.exp(-jnp.abs(z)))

    def within_chunk(v, op, ident, prefix):
        s = 1
        while s < ML_CHUNK:
            nb = pltpu.roll(v, s if prefix else LANES - s, 2)
            ok = (t_in >= s) if prefix else (t_in < ML_CHUNK - s)
            v = op(v, jnp.where(ok, nb, ident))
            s *= 2
        return v

    pre = within_chunk(lf, jnp.add, 0.0, True)
    suf = within_chunk(lf, jnp.add, 0.0, False)
    g = pre + suf - lf
    b = jnp.where(fwd_row, pre, suf)
    a = li - b
    cm_pre = within_chunk(a, jnp.maximum, -jnp.inf, True)
    cm_suf = within_chunk(a, jnp.maximum, -jnp.inf, False)
    cm = jnp.where(fwd_row, cm_pre, cm_suf)
    amax = jnp.maximum(cm_pre, cm_suf)

    def shift_pairs(v, k, fill):
        pad = jnp.full((abs(k),) + v.shape[1:], fill, F32)
        return (jnp.concatenate([pad, v[:n_pairs - k]], axis=0) if k > 0
                else jnp.concatenate([v[-k:], pad], axis=0))

    def from_chunk(v, dist, fill, forward):
        if dist == 1:
            y = pltpu.roll(v, ML_CHUNK, 2)
            if forward:
                return jnp.where(second, y, shift_pairs(y, 1, fill))
            return jnp.where(second, shift_pairs(y, -1, fill), y)
        return shift_pairs(v, dist // 2 if forward else -(dist // 2), fill)

    def running_stabiliser(forward):
        big_g, big_a = g, amax + g
        dist = 1
        while dist < 2 * n_pairs:
            gp = from_chunk(big_g, dist, 0.0, forward)
            ap = from_chunk(big_a, dist, -jnp.inf, forward)
            big_g, big_a = gp + big_g, jnp.maximum(ap + big_g, big_a)
            dist *= 2
        ge = from_chunk(big_g, 1, 0.0, forward)
        ae = from_chunk(big_a, 1, -jnp.inf, forward)
        return jnp.maximum(ge, ae)

    m_chunk = jnp.where(fwd_row, running_stabiliser(True), running_stabiliser(False))
    m_t = jnp.maximum(cm, m_chunk)
    m_last = jnp.maximum(amax, m_chunk)
    a_ref[0] = a
    m_ref[0] = m_t
    iw_ref[0] = jnp.exp(m_chunk - m_t)
    en_ref[0] = jnp.exp(-(b + m_t))
    ws_ref[0] = jnp.exp(a - m_last)
    dec_ref[0] = jnp.exp(m_chunk - m_last)


def _gate_call(gates_t, fbias_col, batch):
    n_pairs = gates_t.shape[0] // batch
    n_ch = 2 * ML_HEADS
    out = pl.BlockSpec((1, n_pairs, n_ch, LANES), lambda i: (i, 0, 0, 0))
    return pl.pallas_call(
        _gate_kernel, name="mlstm_gates",
        grid=(batch,),
        in_specs=[pl.BlockSpec((n_pairs, 2 * n_ch, LANES), lambda i: (i, 0, 0)),
                  pl.BlockSpec((n_ch, 1), lambda i: (0, 0))],
        out_specs=(out,) * 6,
        out_shape=(jax.ShapeDtypeStruct((batch, n_pairs, n_ch, LANES), F32),) * 6,
        compiler_params=_cparams(1),
    )(gates_t, fbias_col)


TE = 1024


def _mlstm_kernel(*refs):
    fwd, bwd, (hf_ref, hb_ref, state) = refs[:9], refs[9:18], refs[18:]
    i = pl.program_id(1)

    @pl.when(i == 0)
    def _():
        state[...] = jnp.zeros(state.shape, F32)

    n_pairs = TE // LANES
    s_idx = lax.broadcasted_iota(jnp.int32, (LANES, LANES), 0)
    t_idx = lax.broadcasted_iota(jnp.int32, (LANES, LANES), 1)
    same_chunk = (s_idx >= ML_CHUNK) == (t_idx >= ML_CHUNK)
    upper_lanes = lax.broadcasted_iota(jnp.int32, (1, LANES), 1) >= ML_CHUNK

    def pair_body(p, carry):
        jobs = []
        for d, ((qt_r, k_r, vt_r, a_r, m_r, iw_r, en_r, ws_r, dec_r), out_r) in enumerate(
                ((fwd, hf_ref), (bwd, hb_ref))):
            pp = p if d == 0 else n_pairs - 1 - p
            srows = pl.ds(pl.multiple_of(pp * LANES, LANES), LANES)
            a_t = jnp.transpose(jnp.concatenate(
                [a_r[0, pp], jnp.zeros((LANES - 2 * ML_HEADS, LANES), F32)], axis=0))
            m_t, iw_t, en_t, ws_t, dec_t = [r[0, pp] for r in (m_r, iw_r, en_r, ws_r, dec_r)]
            dec_lo, dec_hi = dec_t[:, :ML_CHUNK], pltpu.roll(dec_t, ML_CHUNK, 1)[:, :ML_CHUNK]
            in_first = upper_lanes if d else ~upper_lanes
            tri = same_chunk & ((s_idx >= t_idx) if d else (s_idx <= t_idx))
            for h in range(ML_HEADS):
                ch = d * ML_HEADS + h
                row = lambda t: t[ch:ch + 1]
                jobs.append(dict(
                    ch=ch, tri=tri, in_first=in_first, k=k_r[0, h, srows, :], qt=qt_r[0, h, pp],
                    vt=vt_r[0, h, pp], a=a_t[:, ch:ch + 1], m=row(m_t), iw=row(iw_t), en=row(en_t), ws=row(ws_t),
                    dec_first=row(dec_hi if d else dec_lo), dec_second=row(dec_lo if d else dec_hi),
                    out=(out_r, pp, h)))
        for j in jobs:
            vt_f = j["vt"].astype(F32)
            j["c0"] = state[j["ch"]]
            j["st"] = jnp.dot(j["k"], j["qt"], preferred_element_type=F32)
            j["inter1"] = jnp.dot(j["c0"].astype(BF16), j["qt"], preferred_element_type=F32)
            j["upd1"] = jnp.dot((vt_f * jnp.where(j["in_first"], j["ws"], 0.0)).astype(BF16), j["k"],
                                preferred_element_type=F32)
            j["upd2"] = jnp.dot((vt_f * jnp.where(j["in_first"], 0.0, j["ws"])).astype(BF16), j["k"],
                                preferred_element_type=F32)
        for j in jobs:
            j["swt"] = j["st"] * jnp.exp(jnp.where(j["tri"], j["a"] - j["m"], NEG_BIG))
            j["intra"] = jnp.dot(j["vt"], j["swt"].astype(BF16), preferred_element_type=F32)
            j["c1"] = j["dec_first"] * j["c0"] + j["upd1"]
            j["inter2"] = jnp.dot(j["c1"].astype(BF16), j["qt"], preferred_element_type=F32)
        for j in jobs:
            inter = jnp.where(j["in_first"], j["inter1"], j["inter2"])
            den = jnp.sum(j["swt"], axis=0, keepdims=True) + j["iw"] * inter[ML_HD:ML_HD + 1]
            tot = j["intra"][:ML_HD] + j["iw"] * inter[:ML_HD]
            out_r, pp, h = j["out"]
            out_r[0, pp, h * ML_HD:(h + 1) * ML_HD, :] = tot / jnp.maximum(jnp.abs(den), j["en"])
            state[j["ch"]] = j["dec_second"] * j["c1"] + j["upd2"]
        return carry

    lax.fori_loop(0, n_pairs, pair_body, 0)


def _mlstm_call(qt, k, vt, a_t, m_t, iw_t, en_t, ws_t, dec_t):
    b, _, s, _ = k.shape
    nt = s // TE
    ppt = TE // LANES

    def specs(rev):
        ti = (lambda i: nt - 1 - i) if rev else (lambda i: i)
        tile = pl.BlockSpec((1, ppt, 2 * ML_HEADS, LANES), lambda b, i: (b, ti(i), 0, 0))
        return [
            pl.BlockSpec((1, ML_HEADS, ppt, ML_HD, LANES), lambda b, i: (b, 0, ti(i), 0, 0)),
            pl.BlockSpec((1, ML_HEADS, TE, ML_HD), lambda b, i: (b, 0, ti(i), 0)),
            pl.BlockSpec((1, ML_HEADS, ppt, VT_ROWS, LANES), lambda b, i: (b, 0, ti(i), 0, 0)),
            tile, tile, tile, tile, tile, tile]

    args = [qt, k, vt, a_t, m_t, iw_t, en_t, ws_t, dec_t]
    out_f = pl.BlockSpec((1, ppt, MIX_W, LANES), lambda b, i: (b, i, 0, 0))
    out_b = pl.BlockSpec((1, ppt, MIX_W, LANES), lambda b, i: (b, nt - 1 - i, 0, 0))
    return pl.pallas_call(
        _mlstm_kernel, name="mlstm_scan",
        grid=(b, nt),
        in_specs=specs(False) + specs(True),
        out_specs=(out_f, out_b),
        out_shape=(jax.ShapeDtypeStruct((b, s // LANES, MIX_W, LANES), F32),) * 2,
        scratch_shapes=[pltpu.VMEM((2 * ML_HEADS, VT_ROWS, ML_HD), F32)],
        compiler_params=_cparams(2),
    )(*args, *args)


def _mlstm_branch(cqk, cv, gates_t, conv_w, fbias):
    qt, k, vt = _mlprep_call(cqk, cv, conv_w)
    factors = _gate_call(gates_t, fbias.reshape(2 * ML_HEADS, 1), cqk.shape[0])
    return _mlstm_call(qt, k, vt, *factors)


TF = 512


def _merge_kernel(x_ref, ya_ref, o1_ref, o2_ref, o3_ref, l1_ref, l2_ref, l3_ref, hf_ref, hb_ref,
                  co_ref, yd_ref, wg_ref, bg_ref, wbr_ref, wout_ref, mng_ref, lng_ref, lnb_ref,
                  wr_ref, x1_ref, x1p_ref, aff_ref, o_scr, l_scr, *, alpha):
    x = x_ref[0]
    xb = x.astype(BF16)

    def natural_order(src_ref, scr):
        dil, width = src_ref.shape[1], src_ref.shape[3]
        if dil == 1:
            return src_ref[0, 0]
        for r in range(dil):
            for c in range(width // LANES):
                scr[c, pl.ds(r, TF // dil, stride=dil), :] = src_ref[0, r, :, c * LANES:(c + 1) * LANES]
        return jnp.concatenate([scr[c] for c in range(width // LANES)], axis=1)

    lane_head = lax.broadcasted_iota(jnp.int32, (TF, MIX_W), 1) // ML_HD
    l1, l2, l3 = [natural_order(r, l_scr.at[p]) for p, r in enumerate((l1_ref, l2_ref, l3_ref))]
    o1, o2, o3 = [natural_order(r, o_scr.at[p]) for p, r in enumerate((o1_ref, o2_ref, o3_ref))]
    lm = jnp.maximum(jnp.maximum(l1, l2), l3)
    e1, e2, e3 = jnp.exp(l1 - lm), jnp.exp(l2 - lm), jnp.exp(l3 - lm)
    inv = 1.0 / (e1 + e2 + e3)

    def per_head(w):
        out = jnp.zeros((TF, MIX_W), F32)
        for h in range(ATT_HEADS):
            out = jnp.where(lane_head == h, w[:, h:h + 1], out)
        return out

    y_b = per_head(e1 * inv) * o1 + per_head(e2 * inv) * o2 + per_head(e3 * inv) * o3
    hsum_t = jnp.concatenate([hf_ref[0, p] + hb_ref[0, p] for p in range(TF // LANES)], axis=1)
    per_head_rows = hsum_t.reshape(ML_HEADS, ML_HD, TF)
    mu = jnp.mean(per_head_rows, axis=1, keepdims=True)
    cen = per_head_rows - mu
    var = jnp.mean(cen * cen, axis=1, keepdims=True)
    hn_t = (cen * lax.rsqrt(var + LN_EPS)).reshape(MIX_W, TF)
    y_c_t = (jax.nn.sigmoid(jnp.transpose(co_ref[0])) * (hn_t * mng_ref[...])).astype(BF16)
    ys = (ya_ref[0], y_b.astype(BF16), None, yd_ref[0])
    merged = jnp.zeros((TF, D_MODEL), F32)
    for n in range(N_BRANCH):
        cols = slice(n * D_MODEL, (n + 1) * D_MODEL)
        gate = jax.nn.sigmoid(jnp.dot(xb, wg_ref[:, cols], preferred_element_type=F32) + bg_ref[:, cols])
        if ys[n] is None:
            proj = lax.dot_general(y_c_t, wbr_ref[n], (((0,), (0,)), ((), ())), preferred_element_type=F32)
        else:
            proj = jnp.dot(ys[n], wbr_ref[n], preferred_element_type=F32)
        merged = merged + gate * proj
    mix = jnp.dot(merged.astype(BF16), wout_ref[...], preferred_element_type=F32)
    x1 = _standardize(alpha * x + mix) * lng_ref[...] + lnb_ref[...]
    x1_ref[0] = x1
    x1b = x1.astype(BF16)
    x1p_ref[0] = _pack_bf16_pair(x1b[:, :D_MODEL // 2], x1b[:, D_MODEL // 2:])
    logits = lax.dot_general(wr_ref[...], x1b, (((1,), (1,)), ((), ())),
                             preferred_element_type=F32)
    ex = jnp.exp(logits - jnp.max(logits, axis=0, keepdims=True))
    aff_ref[0] = ex / jnp.sum(ex, axis=0, keepdims=True)


def _merge_call(x, ya, o_list, l_list, hf, hb, co, yd, wg, bg, wbr, wout, mng, lng, lnb, wr_t, alpha):
    b, s, _ = x.shape
    tok = lambda w: pl.BlockSpec((1, TF, w), lambda b, i: (b, i, 0))
    grouped = lambda dil, w: pl.BlockSpec((1, dil, TF // dil, w), lambda b, i: (b, 0, i, 0))
    chunked = pl.BlockSpec((1, TF // LANES, MIX_W, LANES), lambda b, i: (b, i, 0, 0))
    const = lambda shp: pl.BlockSpec(shp, lambda b, i: (0,) * len(shp))
    return pl.pallas_call(
        functools.partial(_merge_kernel, alpha=alpha), name="merge_ln_router",
        grid=(b, s // TF),
        in_specs=[tok(D_MODEL), tok(MIX_W)] + [grouped(dil, MIX_W) for _, dil in DIL_PATTERNS]
                 + [grouped(dil, LANES) for _, dil in DIL_PATTERNS]
                 + [chunked, chunked, tok(MIX_W), tok(MIX_W)]
                 + [const((D_MODEL, N_BRANCH * D_MODEL)), const((1, N_BRANCH * D_MODEL)),
                    const((N_BRANCH, MIX_W, D_MODEL)), const((D_MODEL, D_MODEL)), const((MIX_W, 1)),
                    const((1, D_MODEL)), const((1, D_MODEL)), const((N_EXPERTS, D_MODEL))],
        out_specs=(tok(D_MODEL), tok(D_MODEL // 2), pl.BlockSpec((1, N_EXPERTS, TF), lambda b, i: (b, 0, i))),
        out_shape=(jax.ShapeDtypeStruct((b, s, D_MODEL), F32),
                   jax.ShapeDtypeStruct((b, s, D_MODEL // 2), jnp.int32),
                   jax.ShapeDtypeStruct((b, N_EXPERTS, s), F32)),
        scratch_shapes=[pltpu.VMEM((len(DIL_PATTERNS), MIX_W // LANES, TF, LANES), F32),
                        pltpu.VMEM((len(DIL_PATTERNS), 1, TF, LANES), F32)],
        compiler_params=_cparams(2),
    )(x, ya, *o_list, *l_list, hf, hb, co, yd, wg, bg, wbr, wout, mng, lng, lnb, wr_t)


TT = 256


def _select_kernel(aff_ref, slot_ref, *, cap):
    s = aff_ref.shape[2]
    bits = pltpu.bitcast(aff_ref[0], jnp.int32)

    def bit_step(i, thr):
        cand = thr | jnp.left_shift(jnp.int32(1), 30 - i)
        cnt = jnp.sum((bits >= cand).astype(jnp.int32), axis=1, keepdims=True)
        return jnp.where(cnt >= cap, cand, thr)

    thr = lax.fori_loop(0, 31, bit_step, jnp.zeros((N_EXPERTS, 1), jnp.int32))
    gt = bits > thr
    eq = bits == thr
    need = (cap - jnp.sum(gt.astype(jnp.int32), axis=1, keepdims=True)).astype(F32)
    upper = (lax.broadcasted_iota(jnp.int32, (TT, TT), 0)
             <= lax.broadcasted_iota(jnp.int32, (TT, TT), 1)).astype(BF16)
    eq_before = jnp.zeros((N_EXPERTS, 1), F32)
    sel_before = jnp.zeros((N_EXPERTS, 1), F32)
    for j in range(s // TT):
        cols = slice(j * TT, (j + 1) * TT)
        eq_j = eq[:, cols]
        eq_incl = eq_before + jnp.dot(eq_j.astype(BF16), upper, preferred_element_type=F32)
        sel_j = gt[:, cols] | (eq_j & (eq_incl <= need))
        sel_f = sel_j.astype(F32)
        sel_incl = sel_before + jnp.dot(sel_f.astype(BF16), upper, preferred_element_type=F32)
        slot_ref[0, :, cols] = jnp.where(sel_j, sel_incl - 1.0, -1.0).astype(jnp.int32)
        eq_before = eq_incl[:, TT - 1:TT]
        sel_before = sel_incl[:, TT - 1:TT]


def _select_call(aff_t, cap):
    b, e, s = aff_t.shape
    return pl.pallas_call(
        functools.partial(_select_kernel, cap=cap), name="expert_choice_select",
        grid=(b,),
        in_specs=[pl.BlockSpec((1, e, s), lambda i: (i, 0, 0))],
        out_specs=pl.BlockSpec((1, e, s), lambda i: (i, 0, 0)),
        out_shape=jax.ShapeDtypeStruct((b, e, s), jnp.int32),
        compiler_params=_cparams(1),
    )(aff_t)


SC_LANES = 16
SC_ROWS = 64
SC_IDX = 128
SC_SLAB = 128
SC_ZROWS = 64
CF = 1024


def _sc_dispatch_call(x_flat, slot2, aff2, seq, cap):
    n_pairs = slot2.shape[0]
    d = x_flat.shape[1]
    info = plsc.get_sparse_core_info()
    n_workers = info.num_cores * info.num_subcores
    assert n_pairs % n_workers == 0 and seq % SC_LANES == 0 and cap % (2 * SC_ROWS) == 0
    pairs_per_worker = n_pairs // n_workers
    mesh = plsc.VectorSubcoreMesh(core_axis_name="c", subcore_axis_name="s")

    @functools.partial(
        pl.kernel, mesh=mesh, name="expert_dispatch_sc",
        compiler_params=pltpu.CompilerParams(needs_layout_passes=False),
        out_type=(jax.ShapeDtypeStruct((n_pairs * cap, d), x_flat.dtype),
                  jax.ShapeDtypeStruct((n_pairs, cap // SC_IDX, SC_IDX), jnp.int32),
                  jax.ShapeDtypeStruct((n_pairs, cap // SC_IDX, SC_IDX), F32)),
        scratch_types=[pltpu.VMEM((seq,), jnp.int32), pltpu.VMEM((seq,), F32),
                       pltpu.VMEM((cap,), jnp.int32), pltpu.VMEM((cap // SC_IDX, SC_IDX), jnp.int32),
                       pltpu.VMEM((cap // SC_IDX, SC_IDX), F32),
                       pltpu.VMEM((2, SC_ROWS, d), x_flat.dtype), pltpu.SemaphoreType.DMA((2,))])
    def dispatch(x_hbm, slot_hbm, aff_hbm, xs_hbm, tok_hbm, gate_hbm,
                 slot_v, aff_v, idx_v, tok_v, gate_v, rows_v, sem):
        worker = lax.axis_index("s") * info.num_cores + lax.axis_index("c")
        lane = lax.iota(jnp.int32, SC_LANES)

        def gather(c0, buf):
            return pltpu.make_async_copy(x_hbm.at[idx_v.at[pl.ds(c0, SC_ROWS)]], rows_v.at[buf], sem.at[buf])
        for k in range(pairs_per_worker):
            pair = worker * pairs_per_worker + k
            row0 = (pair // N_EXPERTS) * seq
            pltpu.sync_copy(slot_hbm.at[pair], slot_v)
            pltpu.sync_copy(aff_hbm.at[pair], aff_v)

            @plsc.parallel_loop(0, seq, step=SC_LANES, unroll=4)
            def _(t0):
                sv = slot_v[pl.ds(t0, SC_LANES)]
                picked = sv >= 0
                hi, lo = lax.shift_right_logical(sv, 7), sv & (SC_IDX - 1)
                plsc.store_scatter(tok_v, [hi, lo], t0 + lane, mask=picked)
                plsc.store_scatter(idx_v, [sv], row0 + t0 + lane, mask=picked)
                plsc.store_scatter(gate_v, [hi, lo], aff_v[pl.ds(t0, SC_LANES)], mask=picked)

            pltpu.sync_copy(tok_v, tok_hbm.at[pair])
            pltpu.sync_copy(gate_v, gate_hbm.at[pair])

            gather(0, 0).start()

            @pl.loop(0, cap, step=2 * SC_ROWS)
            def _(c0):
                gather(c0 + SC_ROWS, 1).start()
                gather(c0, 0).wait()
                pltpu.sync_copy(rows_v.at[0], xs_hbm.at[pl.ds(pair * cap + c0, SC_ROWS)])

                @pl.when(c0 + 2 * SC_ROWS < cap)
                def _():
                    gather(c0 + 2 * SC_ROWS, 0).start()

                gather(c0 + SC_ROWS, 1).wait()
                pltpu.sync_copy(rows_v.at[1], xs_hbm.at[pl.ds(pair * cap + c0 + SC_ROWS, SC_ROWS)])

    return dispatch(x_flat, slot2, aff2)


def _expert_kernel(xs_ref, g_ref, w1_ref, w3_ref, w2_ref, ye_ref, w1_bf, w3_bf, w2_bf):
    @pl.when((pl.program_id(1) == 0) & (pl.program_id(2) == 0))
    def _():
        w1_bf[...] = w1_ref[0, 0].astype(BF16)
        w3_bf[...] = w3_ref[0, 0].astype(BF16)
        w2_bf[...] = w2_ref[0, 0].astype(BF16)

    xs = jnp.concatenate(_unpack_bf16_pair(xs_ref[0, 0]), axis=1)
    hid = (jax.nn.silu(jnp.dot(xs, w1_bf[...], preferred_element_type=F32))
           * jnp.dot(xs, w3_bf[...], preferred_element_type=F32))
    g_rows = g_ref[0, 0]
    n_rows = g_rows.shape[0]
    g_t = jnp.transpose(jnp.concatenate([g_rows, jnp.zeros((LANES - n_rows, LANES), F32)], axis=0))
    g_col = jnp.concatenate([g_t[:, r:r + 1] for r in range(n_rows)], axis=0)
    ye_ref[0, 0] = jnp.dot(hid.astype(BF16), w2_bf[...], preferred_element_type=F32) * g_col


def _expert_call(xs4, gate4, w1, w3, w2, layer):
    b, e, cap, half = xs4.shape
    d, ff = w1.shape[2], w1.shape[3]
    assert d == 2 * half
    rows = lambda w: pl.BlockSpec((1, 1, CF, w), lambda e, b, j: (b, e, j, 0))
    wspec = lambda r, c: pl.BlockSpec((1, 1, r, c), lambda e, b, j: (layer, e, 0, 0))
    return pl.pallas_call(
        _expert_kernel, name="expert_ffn",
        grid=(e, b, cap // CF),
        in_specs=[rows(half), pl.BlockSpec((1, 1, CF // LANES, LANES), lambda e, b, j: (b, e, j, 0)),
                  wspec(d, ff), wspec(d, ff), wspec(ff, d)],
        out_specs=rows(d),
        out_shape=jax.ShapeDtypeStruct((b, e, cap, d), F32),
        scratch_shapes=[pltpu.VMEM((d, ff), BF16), pltpu.VMEM((d, ff), BF16), pltpu.VMEM((ff, d), BF16)],
        compiler_params=_cparams(3),
    )(xs4, gate4, w1, w3, w2)


def _sc_combine_call(ye_flat, tok3, seq):
    n_pairs, n_chunks, _ = tok3.shape
    cap = n_chunks * SC_IDX
    d = ye_flat.shape[1]
    nb = n_pairs // N_EXPERTS
    info = plsc.get_sparse_core_info()
    assert info.num_subcores == N_EXPERTS and nb % info.num_cores == 0 and n_chunks % 2 == 0
    assert seq % (info.num_subcores * SC_ZROWS) == 0 and d % SC_SLAB == 0
    batches_per_core = nb // info.num_cores
    own_rows = seq // info.num_subcores
    mesh = plsc.VectorSubcoreMesh(core_axis_name="c", subcore_axis_name="s")

    @functools.partial(
        pl.kernel, mesh=mesh, name="expert_combine_sc",
        compiler_params=pltpu.CompilerParams(needs_layout_passes=False),
        out_type=jax.ShapeDtypeStruct((nb * seq, d), F32),
        scratch_types=[pltpu.VMEM_SHARED((seq, SC_SLAB), F32),
                       pltpu.VMEM((n_chunks, SC_IDX), jnp.int32),
                       pltpu.VMEM((2, SC_IDX, SC_SLAB), F32),
                       pltpu.VMEM((SC_ZROWS, SC_SLAB), F32),
                       pltpu.SemaphoreType.DMA((2,))])
    def combine(ye_hbm, tok_hbm, out_hbm, acc_sh, tok_v, rows_v, zero_v, sem):
        core = lax.axis_index("c")
        sub = lax.axis_index("s")

        @pl.loop(0, SC_ZROWS)
        def _(r):
            for l0 in range(0, SC_SLAB, SC_LANES):
                zero_v[r, pl.ds(l0, SC_LANES)] = jnp.zeros((SC_LANES,), F32)

        for bb in range(batches_per_core):
            batch = core * batches_per_core + bb
            pair = batch * N_EXPERTS + sub
            pltpu.sync_copy(tok_hbm.at[pair], tok_v)

            @pl.loop(0, d // SC_SLAB)
            def _(slab):
                cols = pl.ds(pl.multiple_of(slab * SC_SLAB, SC_SLAB), SC_SLAB)

                @pl.loop(0, own_rows, step=SC_ZROWS)
                def _(r0):
                    pltpu.sync_copy(zero_v, acc_sh.at[pl.ds(sub * own_rows + r0, SC_ZROWS)])

                def load(j, buf):
                    return pltpu.make_async_copy(
                        ye_hbm.at[pl.ds(pair * cap + j * SC_IDX, SC_IDX), cols], rows_v.at[buf], sem.at[buf])

                load(0, 0).start()
                plsc.subcore_barrier()

                for j in range(0, n_chunks, 2):
                    load(j + 1, 1).start()
                    load(j, 0).wait()
                    pltpu.sync_copy(rows_v.at[0], acc_sh.at[tok_v.at[j]], add=True)
                    if j + 2 < n_chunks:
                        load(j + 2, 0).start()
                    load(j + 1, 1).wait()
                    pltpu.sync_copy(rows_v.at[1], acc_sh.at[tok_v.at[j + 1]], add=True)

                plsc.subcore_barrier()
                pltpu.sync_copy(acc_sh.at[pl.ds(sub * own_rows, own_rows)],
                                out_hbm.at[pl.ds(batch * seq + sub * own_rows, own_rows), cols])

    return combine(ye_flat, tok3)


TN = 512


def _resln_kernel(x_ref, y_ref, g_ref, b_ref, o_ref, *, alpha):
    o_ref[...] = _standardize(alpha * x_ref[...] + y_ref[...]) * g_ref[...] + b_ref[...]


def _resln_call(x2d, y2d, g, bta, alpha):
    n, d = x2d.shape
    tok = pl.BlockSpec((TN, d), lambda i: (i, 0))
    vec = pl.BlockSpec((1, d), lambda i: (0, 0))
    return pl.pallas_call(
        functools.partial(_resln_kernel, alpha=alpha), name="residual_layernorm",
        grid=(n // TN,), in_specs=[tok, tok, vec, vec], out_specs=tok,
        out_shape=jax.ShapeDtypeStruct((n, d), F32),
        compiler_params=_cparams(1),
    )(x2d, y2d, g, bta)


def _expert_choice_ffn(x1p, aff_t, w1, w3, w2, layer):
    b, s, half = x1p.shape
    d = 2 * half
    cap = EC_FACTOR * s // N_EXPERTS
    slot = _select_call(aff_t, cap)
    xs, tok, gate = _sc_dispatch_call(x1p.reshape(b * s, half), slot.reshape(b * N_EXPERTS, s),
                                      aff_t.reshape(b * N_EXPERTS, s), s, cap)
    ye = _expert_call(xs.reshape(b, N_EXPERTS, cap, half), gate.reshape(b, N_EXPERTS, cap // SC_IDX, SC_IDX), w1, w3, w2,
                      layer)
    out = _sc_combine_call(ye.reshape(b * N_EXPERTS * cap, d),
                           tok, s)
    return out.reshape(b, s, d)


def _pack_pool(pool_w):
    g, gd, _ = pool_w.shape
    out = jnp.zeros((g * gd, g * gd), F32)
    for i in range(g):
        out = out.at[i * gd:(i + 1) * gd, i * gd:(i + 1) * gd].set(pool_w[i])
    return out.astype(BF16)


def _layer(layer, x, pending, alpha, bias_tiles, w_in, b_in, gm_ln_g, gm_ws, gm_bs, ml_conv, ml_fbias,
           ml_norm_g, pool_w, pool_scale, w_branch, w_out, ln1_g, ln1_b, w_router, w_e1, w_e3, w_e2):
    b, s, d = x.shape
    n_small = 2576
    w_cat, b_cat = _pack_inproj_weights(w_in, b_in)
    wscat, bsfull = _pack_gmlp(gm_ws, gm_bs)
    if pending is None:
        outs = _inproj_call(x.reshape(b * s, d), w_cat, b_cat, gm_ln_g[None], wscat, bsfull, b)
    else:
        *outs, x2 = _inproj_call(pending, w_cat, b_cat, gm_ln_g[None], wscat, bsfull, b, alpha)
        x = x2.reshape(b, s, d)
    ya, qkv1, qkv4, qkv16, cqk, cv, co, dx, gates_t = outs
    r3 = lambda t: t.reshape(b, s, t.shape[-1])
    o_list, l_list = [], []
    for qkv, bias in zip((qkv1, qkv4, qkv16), bias_tiles):
        o, lse = _attn_call(qkv, bias)
        o_list.append(o)
        l_list.append(lse)
    hf, hb = _mlstm_branch(r3(cqk), r3(cv), gates_t, ml_conv, ml_fbias)
    yd = _pool_call(r3(dx), _pack_pool(pool_w), pool_scale[None])
    x1, x1p, aff_t = _merge_call(
        x, r3(ya), o_list, l_list, hf, hb, r3(co), yd,
        w_in[:, n_small:].astype(BF16), b_in[None, n_small:], w_branch.astype(BF16), w_out.astype(BF16),
        ml_norm_g[:, None], ln1_g[None], ln1_b[None], jnp.transpose(w_router).astype(BF16), alpha)
    ffn = _expert_choice_ffn(x1p, aff_t, w_e1, w_e3, w_e2, layer)
    return x1.reshape(b * s, d), ffn.reshape(b * s, d)


def kernel(x, w_in, b_in, gm_ln_g, gm_ws, gm_bs, rel_bias, ml_conv, ml_fbias, ml_norm_g, pool_w,
           pool_scale, w_branch, w_out, ln1_g, ln1_b, w_router, w_e1, w_e3, w_e2, ln2_g, ln2_b):
    depth = w_in.shape[0]
    alpha = (2 * depth) ** 0.25
    bias_tiles = [_attn_bias_tile(rel_bias, window, dil) for window, dil in DIL_PATTERNS]
    b, s, d = x.shape
    pending = None
    for l in range(depth):
        x1, ffn = _layer(l, x, pending, alpha, bias_tiles, w_in[l], b_in[l], gm_ln_g[l], gm_ws[l], gm_bs[l],
                         ml_conv[l], ml_fbias[l], ml_norm_g[l], pool_w[l], pool_scale[l], w_branch[l],
                         w_out[l], ln1_g[l], ln1_b[l], w_router[l], w_e1, w_e3, w_e2)
        pending = (x1, ffn, ln2_g[l][None], ln2_b[l][None])
    return _resln_call(*pending, alpha).reshape(b, s, d)
```

```python
import functools
import math

import jax
import jax.numpy as jnp
import numpy as np
from jax import lax
from jax.experimental import pallas as pl
from jax.experimental.pallas import tpu as pltpu
from jax.experimental.pallas import tpu_sc as plsc

F32 = jnp.float32
BF16 = jnp.bfloat16

D_MODEL = 1024
MIX_W = 256
N_BRANCH = 4
GM_CHUNK = 128
GM_GROUPS = 4
ATT_HEADS = 4
ATT_HD = 64
DIL_PATTERNS = ((128, 1), (512, 4), (2048, 16))
ATT_BLOCK = 64
REL_BUCKETS = 32
REL_MAX_DIST = 1024
ML_HEADS = 4
ML_HD = 64
ML_CHUNK = 64
POOL_WINDOWS = (2, 4, 8, 16)
N_EXPERTS = 16
EXPERT_FF = 1024
EC_FACTOR = 2
LN_EPS = 1e-5
NEG_BIG = -1e30

V7X_VMEM_LIMIT = 56 * 1024 * 1024
LANES = 128
HALO = 8


def _cparams(n_grid, vmem=V7X_VMEM_LIMIT):
    return pltpu.CompilerParams(dimension_semantics=("arbitrary",) * n_grid,
                                vmem_limit_bytes=vmem)


def _pack_bf16_pair(lo, hi):
    lo_bits = lax.shift_right_logical(pltpu.bitcast(lo.astype(F32), jnp.int32), 16)
    return pltpu.bitcast(hi.astype(F32), jnp.int32) | lo_bits


def _unpack_bf16_pair(packed):
    lo = pltpu.bitcast(lax.shift_left(packed, 16), F32).astype(BF16)
    hi = pltpu.bitcast(packed & jnp.int32(-65536), F32).astype(BF16)
    return lo, hi


def _standardize(xf):
    mu = jnp.mean(xf, axis=-1, keepdims=True)
    var = jnp.mean(jnp.square(xf - mu), axis=-1, keepdims=True)
    return (xf - mu) * lax.rsqrt(var + LN_EPS)


TA = 512
A_COLS = 2560 + LANES


def _inproj_kernel(*refs, alpha):
    if alpha is None:
        x_ref, *refs = refs
        x = x_ref[...]
    else:
        x1_ref, y_ref, g2_ref, b2_ref, *refs = refs
        x = _standardize(alpha * x1_ref[...] + y_ref[...]) * g2_ref[...] + b2_ref[...]
        refs[-2][...] = x
        refs = refs[:-2] + refs[-1:]
    (w_ref, b_ref, lng_ref, wscat_ref, bsfull_ref,
     ya_ref, qkv1_ref, qkv4_ref, qkv16_ref, cqk_ref, cv_ref, co_ref, dx_ref, gt_ref, qkv_scr) = refs
    xb = x.astype(BF16)
    h = jnp.dot(xb, w_ref[...], preferred_element_type=F32) + b_ref[...]
    qkv1_ref[0, 0] = h[:, 512:1280].astype(BF16)
    for c in range(768 // LANES):
        qkv_scr[c] = h[:, 512 + c * LANES:512 + (c + 1) * LANES]
    for (_, dil), out_ref in zip(DIL_PATTERNS[1:], (qkv4_ref, qkv16_ref)):
        for r in range(dil):
            for c in range(768 // LANES):
                out_ref[0, r, :, c * LANES:(c + 1) * LANES] = (
                    qkv_scr[c, pl.ds(r, TA // dil, stride=dil), :].astype(BF16))
    cqk_ref[...] = h[:, 1280:1792]
    cv_ref[...] = h[:, 1792:2048].astype(BF16)
    co_ref[...] = h[:, 2048:2304]
    dx_ref[...] = h[:, 2304:2560]
    gates_t = jnp.transpose(h[:, 2560:2688])
    for j in range(TA // LANES):
        gt_ref[j] = gates_t[0:4 * ML_HEADS, j * LANES:(j + 1) * LANES]
    u = jax.nn.gelu(h[:, 0:256])
    v = jax.nn.gelu(h[:, 256:512])
    vn = _standardize(v) * lng_ref[...]
    lane_grp = lax.broadcasted_iota(jnp.int32, (GM_CHUNK, MIX_W), 1) // (MIX_W // GM_GROUPS)
    for c in range(TA // GM_CHUNK):
        vc = vn[c * GM_CHUNK:(c + 1) * GM_CHUNK]
        stacked = jnp.concatenate(
            [jnp.where(lane_grp == g, vc, 0.0).astype(BF16) for g in range(GM_GROUPS)], axis=0)
        mixed = jnp.dot(wscat_ref[...], stacked, preferred_element_type=F32) + bsfull_ref[...]
        ya_ref[c * GM_CHUNK:(c + 1) * GM_CHUNK, :] = (
            u[c * GM_CHUNK:(c + 1) * GM_CHUNK] * mixed).astype(BF16)


def _inproj_call(x_in, w_cat, b_cat, lng, wscat, bsfull, batch, layer, alpha=None):
    fused = alpha is not None
    per_layer = lambda s: pl.BlockSpec((None,) + s, lambda i: (layer,) + (0,) * len(s))
    n = (x_in[0] if fused else x_in).shape[0]
    seq = n // batch
    tpb = seq // TA
    tok = lambda w: pl.BlockSpec((TA, w), lambda i: (i, 0))
    const = lambda s: pl.BlockSpec(s, lambda i: (0,) * len(s))
    regrouped = lambda dil: pl.BlockSpec((1, dil, TA // dil, 768), lambda i: (i // tpb, 0, i % tpb, 0))
    out_shape = (
        jax.ShapeDtypeStruct((n, 256), BF16),
    ) + tuple(jax.ShapeDtypeStruct((batch, dil, seq // dil, 768), BF16)
              for _, dil in DIL_PATTERNS) + (
        jax.ShapeDtypeStruct((n, 512), F32),
        jax.ShapeDtypeStruct((n, 256), BF16),
        jax.ShapeDtypeStruct((n, 256), F32),
        jax.ShapeDtypeStruct((n, 256), F32),
        jax.ShapeDtypeStruct((n // LANES, 4 * ML_HEADS, LANES), F32),
    )
    x_specs = [tok(D_MODEL), tok(D_MODEL), const((1, D_MODEL)), const((1, D_MODEL))] if fused else [tok(D_MODEL)]
    out_specs = ((tok(256),) + tuple(regrouped(dil) for _, dil in DIL_PATTERNS)
                 + (tok(512), tok(256), tok(256), tok(256),
                    pl.BlockSpec((TA // LANES, 4 * ML_HEADS, LANES), lambda i: (i, 0, 0))))
    if fused:
        out_specs += (tok(D_MODEL),)
        out_shape += (jax.ShapeDtypeStruct((n, D_MODEL), F32),)
    return pl.pallas_call(
        functools.partial(_inproj_kernel, alpha=alpha), name="inproj_gmlp",
        grid=(n // TA,),
        in_specs=x_specs + [per_layer((D_MODEL, A_COLS)), per_layer((1, A_COLS)), const((1, MIX_W)),
                            const((GM_CHUNK, GM_GROUPS * GM_CHUNK)), const((GM_CHUNK, MIX_W))],
        out_specs=out_specs,
        out_shape=out_shape,
        scratch_shapes=[pltpu.VMEM((768 // LANES, TA, LANES), F32)],
        compiler_params=_cparams(1),
    )(*(x_in if fused else (x_in,)), w_cat, b_cat, lng, wscat, bsfull)


def _pack_inproj_weights(w_in, b_in):
    pad = lambda a: jnp.pad(a, ((0, 0), (0, 0), (0, LANES - 4 * ML_HEADS)))
    w_cat = jnp.concatenate([w_in[..., 0:2304], w_in[..., 2320:2576], pad(w_in[..., 2304:2320])], axis=-1)
    b2 = b_in[:, None, :]
    b_cat = jnp.concatenate([b2[..., 0:2304], b2[..., 2320:2576], pad(b2[..., 2304:2320])], axis=-1)
    return w_cat.astype(BF16), b_cat


def _pack_gmlp(gm_ws, gm_bs):
    wscat = jnp.transpose(gm_ws, (1, 0, 2)).reshape(GM_CHUNK, GM_GROUPS * GM_CHUNK).astype(BF16)
    bsfull = jnp.repeat(jnp.transpose(gm_bs), MIX_W // GM_GROUPS, axis=1)
    return wscat, bsfull


def _halo_specs(t, width, n_tiles):
    r = t // HALO
    main = pl.BlockSpec((1, t, width), lambda b, i: (b, i, 0))
    prev = pl.BlockSpec((1, HALO, width), lambda b, i: (b, jnp.maximum(i * r - 1, 0), 0))
    nxt = pl.BlockSpec((1, HALO, width), lambda b, i: (b, jnp.minimum((i + 1) * r, n_tiles * r - 1), 0))
    return main, prev, nxt


def _fill_halo_scratch(buf, x_ref, p_ref, n_ref, t):
    i = pl.program_id(1)
    last = pl.num_programs(1) - 1
    buf[0:HALO, :] = jnp.where(i > 0, p_ref[0], 0.0)
    buf[HALO:HALO + t, :] = x_ref[0]
    buf[HALO + t:2 * HALO + t, :] = jnp.where(i < last, n_ref[0], 0.0)


TP = 512


def _pool_kernel(x_ref, p_ref, n_ref, w_ref, sc_ref, o_ref, buf, lvl):
    _fill_halo_scratch(buf, x_ref, p_ref, n_ref, TP)
    seq = pl.num_programs(1) * TP
    pos = pl.program_id(1) * TP + lax.broadcasted_iota(jnp.int32, (TP, 1), 0)
    lane_grp = lax.broadcasted_iota(jnp.int32, (TP, MIX_W), 1) // (MIX_W // len(POOL_WINDOWS))
    x0 = buf[HALO:HALO + TP, :]
    sums = []
    src, rows = buf, TP + 2 * HALO
    for k, win in enumerate(POOL_WINDOWS):
        half = win // 2
        rows -= half
        cur = src[0:rows, :] + src[half:rows + half, :] if k else buf[0:rows, :] + buf[1:rows + 1, :]
        if k + 1 < len(POOL_WINDOWS):
            lvl[k, 0:rows, :] = cur
            sums.append(lvl[k, HALO - half:HALO - half + TP, :])
            src = lvl.at[k]
        else:
            sums.append(cur[0:TP])
    pooled = jnp.zeros((TP, MIX_W), F32)
    for gi, win in enumerate(POOL_WINDOWS):
        half = win // 2
        cnt = (jnp.minimum(pos + half, seq) - jnp.maximum(pos - half, 0)).astype(F32)
        pooled = jnp.where(lane_grp == gi, sums[gi] / cnt - x0, pooled)
    mixed = jnp.dot(pooled.astype(BF16), w_ref[...], preferred_element_type=F32)
    o_ref[0] = (mixed * sc_ref[...]).astype(BF16)


def _pool_call(dx, w_block, scale):
    b, s, _ = dx.shape
    nt = s // TP
    main, prev, nxt = _halo_specs(TP, MIX_W, nt)
    return pl.pallas_call(
        _pool_kernel, name="pool_mixer",
        grid=(b, nt),
        in_specs=[main, prev, nxt,
                  pl.BlockSpec((MIX_W, MIX_W), lambda b, i: (0, 0)),
                  pl.BlockSpec((1, MIX_W), lambda b, i: (0, 0))],
        out_specs=pl.BlockSpec((1, TP, MIX_W), lambda b, i: (b, i, 0)),
        out_shape=jax.ShapeDtypeStruct((b, s, MIX_W), BF16),
        scratch_shapes=[pltpu.VMEM((TP + 2 * HALO, MIX_W), F32),
                        pltpu.VMEM((len(POOL_WINDOWS) - 1, TP + 2 * HALO, MIX_W), F32)],
        compiler_params=_cparams(2),
    )(dx, dx, dx, w_block, scale)


TQ = 128
TQS = 512
TKEYS = TQ + 2 * ATT_BLOCK


def _attn_kernel(q_ref, kp_ref, km_ref, kn_ref, vp_ref, vm_ref, vn_ref, bias_ref, o_ref, lse_ref):
    i = pl.program_id(2)
    q = q_ref[0, 0] * ATT_HD ** -0.5
    k = jnp.concatenate([kp_ref[0, 0], km_ref[0, 0], kn_ref[0, 0]], axis=0)
    v = jnp.concatenate([vp_ref[0, 0], vm_ref[0, 0], vn_ref[0, 0]], axis=0)
    lane = lax.broadcasted_iota(jnp.int32, (TQ, LANES), 1)
    lane_half = lax.broadcasted_iota(jnp.int32, (1, LANES), 1) // ATT_HD
    keep = [jnp.where(lane_half == hh, 1.0, 0.0).astype(BF16) for hh in range(2)]
    n_sub = TQS // TQ
    last_step = pl.num_programs(2) - 1
    for j in range(n_sub):
        if j == 0:
            variant = jnp.where(i == 0, 0, 1)
        elif j == n_sub - 1:
            variant = jnp.where(i == last_step, 2, 1)
        else:
            variant = 1
        qrows = slice(j * TQ, (j + 1) * TQ)
        krows = slice(j * TQ, j * TQ + TKEYS)
        lse_tile = jnp.zeros((TQ, LANES), F32)
        for pair in range(ATT_HEADS // 2):
            grp = slice(pair * LANES, (pair + 1) * LANES)
            q_pair, k_pair, v_pair = q[qrows, grp], k[krows, grp], v[krows, grp]
            o_pair = jnp.zeros((TQ, LANES), F32)
            for hh in range(2):
                h = 2 * pair + hh
                logits = lax.dot_general(q_pair * keep[hh], k_pair, (((1,), (1,)), ((), ())),
                                         preferred_element_type=F32) + bias_ref[variant, h]
                m = jnp.max(logits, axis=-1, keepdims=True)
                p = jnp.exp(logits - m)
                ssum = jnp.sum(p, axis=-1, keepdims=True)
                o = jnp.dot(p.astype(BF16), v_pair, preferred_element_type=F32) / ssum
                o_pair = jnp.where(lane_half == hh, o, o_pair)
                lse_tile = jnp.where(lane == h, m + jnp.log(ssum), lse_tile)
            o_ref[0, 0, qrows, grp] = o_pair
        lse_ref[0, 0, qrows, :] = lse_tile


def _attn_call(qkv, bias):
    b, dil, l, _ = qkv.shape
    nt = l // TQS
    r64 = TQS // ATT_BLOCK
    main = lambda c: pl.BlockSpec((1, 1, TQS, MIX_W), lambda b, r, i: (b, r, i, c))
    prev = lambda c: pl.BlockSpec((1, 1, ATT_BLOCK, MIX_W),
                                  lambda b, r, i: (b, r, jnp.maximum(i * r64 - 1, 0), c))
    nxt = lambda c: pl.BlockSpec((1, 1, ATT_BLOCK, MIX_W),
                                 lambda b, r, i: (b, r, jnp.minimum((i + 1) * r64, nt * r64 - 1), c))
    return pl.pallas_call(
        _attn_kernel, name="band_attention",
        grid=(b, dil, nt),
        in_specs=[main(0), prev(1), main(1), nxt(1), prev(2), main(2), nxt(2),
                  pl.BlockSpec((3, ATT_HEADS, TQ, TKEYS), lambda b, r, i: (0, 0, 0, 0))],
        out_specs=(pl.BlockSpec((1, 1, TQS, MIX_W), lambda b, r, i: (b, r, i, 0)),
                   pl.BlockSpec((1, 1, TQS, LANES), lambda b, r, i: (b, r, i, 0))),
        out_shape=(jax.ShapeDtypeStruct((b, dil, l, MIX_W), F32),
                   jax.ShapeDtypeStruct((b, dil, l, LANES), F32)),
        compiler_params=_cparams(3),
    )(qkv, qkv, qkv, qkv, qkv, qkv, qkv, bias)


def _t5_bucket_static(rel):
    half = REL_BUCKETS // 2
    max_exact = half // 2
    ret = np.where(rel > 0, half, 0)
    n = np.abs(rel)
    nf = np.maximum(n, 1).astype(np.float32)
    large = max_exact + (np.log(nf / np.float32(max_exact)) / np.float32(math.log(REL_MAX_DIST / max_exact))
                         * np.float32(half - max_exact)).astype(np.int32)
    large = np.minimum(large, half - 1)
    return ret + np.where(n < max_exact, n, large)


def _attn_bias_tile(rel_bias, window, dil):
    side = (window // 2) // dil
    rel = np.arange(TKEYS)[None, :] - ATT_BLOCK - np.arange(TQ)[:, None]
    n_rel = TKEYS + TQ - 1
    rel_values = np.arange(n_rel) - (ATT_BLOCK + TQ - 1)
    onehot = jax.nn.one_hot(jnp.asarray(_t5_bucket_static(dil * rel_values), jnp.int32), REL_BUCKETS, dtype=F32)
    table = jnp.einsum('nr,rh->hn', onehot, rel_bias, precision=lax.Precision.HIGHEST)
    periodic = jnp.tile(jnp.pad(table, ((0, 0), (0, 1))), (1, TQ))[:, :TQ * n_rel]
    bias = periodic.reshape(ATT_HEADS, TQ, n_rel)[:, :, TQ - 1:]
    key = np.arange(TKEYS)[None, :]
    inside = np.abs(rel) <= side
    masks = np.stack([inside & (key >= ATT_BLOCK), inside, inside & (key < ATT_BLOCK + TQ)])
    return jnp.where(jnp.asarray(masks)[:, None], bias[None], NEG_BIG)


TM = 512
VT_ROWS = ML_HD + 16


def _mlprep_kernel(x_ref, p_ref, n_ref, v_ref, w_ref, qt_out, k_out, vt_out, buf):
    _fill_halo_scratch(buf, x_ref, p_ref, n_ref, TM)
    conv = (buf[HALO - 1:HALO - 1 + TM, :] * w_ref[0:1, :] + buf[HALO:HALO + TM, :] * w_ref[1:2, :]
            + buf[HALO + 1:HALO + 1 + TM, :] * w_ref[2:3, :])
    qk = jax.nn.silu(conv)
    qt = jnp.transpose(qk[:, :MIX_W])
    vt = jnp.transpose(v_ref[0].astype(F32))
    ones_rows = jnp.where(lax.broadcasted_iota(jnp.int32, (VT_ROWS - ML_HD, LANES), 0) == 0, 1.0, 0.0)
    for h in range(ML_HEADS):
        sl = slice(h * ML_HD, (h + 1) * ML_HD)
        k_out[0, h] = (qk[:, MIX_W + h * ML_HD:MIX_W + (h + 1) * ML_HD] * ML_HD ** -0.5).astype(BF16)
        for p in range(TM // LANES):
            pl_ = slice(p * LANES, (p + 1) * LANES)
            qt_out[0, h, p] = qt[sl, pl_].astype(BF16)
            vt_out[0, h, p] = jnp.concatenate([vt[sl, pl_], ones_rows], axis=0).astype(BF16)


def _mlprep_call(cqk, cv, conv_w):
    b, s, _ = cqk.shape
    nt = s // TM
    n_pairs = s // LANES
    ppt = TM // LANES
    main, prev, nxt = _halo_specs(TM, 2 * MIX_W, nt)
    return pl.pallas_call(
        _mlprep_kernel, name="mlstm_prep",
        grid=(b, nt),
        in_specs=[main, prev, nxt,
                  pl.BlockSpec((1, TM, MIX_W), lambda b, i: (b, i, 0)),
                  pl.BlockSpec((3, 2 * MIX_W), lambda b, i: (0, 0))],
        out_specs=(pl.BlockSpec((1, ML_HEADS, ppt, ML_HD, LANES), lambda b, i: (b, 0, i, 0, 0)),
                   pl.BlockSpec((1, ML_HEADS, TM, ML_HD), lambda b, i: (b, 0, i, 0)),
                   pl.BlockSpec((1, ML_HEADS, ppt, VT_ROWS, LANES), lambda b, i: (b, 0, i, 0, 0))),
        out_shape=(jax.ShapeDtypeStruct((b, ML_HEADS, n_pairs, ML_HD, LANES), BF16),
                   jax.ShapeDtypeStruct((b, ML_HEADS, s, ML_HD), BF16),
                   jax.ShapeDtypeStruct((b, ML_HEADS, n_pairs, VT_ROWS, LANES), BF16)),
        scratch_shapes=[pltpu.VMEM((TM + 2 * HALO, 2 * MIX_W), F32)],
        compiler_params=_cparams(2),
    )(cqk, cqk, cqk, cv, conv_w)


def _gate_kernel(g_ref, fb_ref, a_ref, m_ref, iw_ref, en_ref, ws_ref, dec_ref):
    x = g_ref[...]
    n_pairs = x.shape[0]
    n_ch = 2 * ML_HEADS
    lane = lax.broadcasted_iota(jnp.int32, (1, 1, LANES), 2)
    t_in = lane % ML_CHUNK
    second = lane >= ML_CHUNK
    fwd_row = lax.broadcasted_iota(jnp.int32, (1, n_ch, 1), 1) < ML_HEADS
    li = x[:, 0:n_ch, :]
    z = x[:, n_ch:2 * n_ch, :] + fb_ref[...]
    lf = jnp.minimum(z, 0.0) - jnp.log1p(jnp.exp(-jnp.abs(z)))

    def within_chunk(v, op, ident, prefix):
        s = 1
        while s < ML_CHUNK:
            nb = pltpu.roll(v, s if prefix else LANES - s, 2)
            ok = (t_in >= s) if prefix else (t_in < ML_CHUNK - s)
            v = op(v, jnp.where(ok, nb, ident))
            s *= 2
        return v

    pre = within_chunk(lf, jnp.add, 0.0, True)
    suf = within_chunk(lf, jnp.add, 0.0, False)
    g = pre + suf - lf
    b = jnp.where(fwd_row, pre, suf)
    a = li - b
    cm_pre = within_chunk(a, jnp.maximum, -jnp.inf, True)
    cm_suf = within_chunk(a, jnp.maximum, -jnp.inf, False)
    cm = jnp.where(fwd_row, cm_pre, cm_suf)
    amax = jnp.maximum(cm_pre, cm_suf)

    def shift_pairs(v, k, fill):
        pad = jnp.full((abs(k),) + v.shape[1:], fill, F32)
        return (jnp.concatenate([pad, v[:n_pairs - k]], axis=0) if k > 0
                else jnp.concatenate([v[-k:], pad], axis=0))

    def from_chunk(v, dist, fill, forward):
        if dist == 1:
            y = pltpu.roll(v, ML_CHUNK, 2)
            if forward:
                return jnp.where(second, y, shift_pairs(y, 1, fill))
            return jnp.where(second, shift_pairs(y, -1, fill), y)
        return shift_pairs(v, dist // 2 if forward else -(dist // 2), fill)

    def running_stabiliser(forward):
        big_g, big_a = g, amax + g
        dist = 1
        while dist < 2 * n_pairs:
            gp = from_chunk(big_g, dist, 0.0, forward)
            ap = from_chunk(big_a, dist, -jnp.inf, forward)
            big_g, big_a = gp + big_g, jnp.maximum(ap + big_g, big_a)
            dist *= 2
        ge = from_chunk(big_g, 1, 0.0, forward)
        ae = from_chunk(big_a, 1, -jnp.inf, forward)
        return jnp.maximum(ge, ae)

    m_chunk = jnp.where(fwd_row, running_stabiliser(True), running_stabiliser(False))
    m_t = jnp.maximum(cm, m_chunk)
    m_last = jnp.maximum(amax, m_chunk)
    a_ref[0] = a
    m_ref[0] = m_t
    iw_ref[0] = jnp.exp(m_chunk - m_t)
    en_ref[0] = jnp.exp(-(b + m_t))
    ws_ref[0] = jnp.exp(a - m_last)
    dec_ref[0] = jnp.exp(m_chunk - m_last)


def _gate_call(gates_t, fbias_col, batch):
    n_pairs = gates_t.shape[0] // batch
    n_ch = 2 * ML_HEADS
    out = pl.BlockSpec((1, n_pairs, n_ch, LANES), lambda i: (i, 0, 0, 0))
    return pl.pallas_call(
        _gate_kernel, name="mlstm_gates",
        grid=(batch,),
        in_specs=[pl.BlockSpec((n_pairs, 2 * n_ch, LANES), lambda i: (i, 0, 0)),
                  pl.BlockSpec((n_ch, 1), lambda i: (0, 0))],
        out_specs=(out,) * 6,
        out_shape=(jax.ShapeDtypeStruct((batch, n_pairs, n_ch, LANES), F32),) * 6,
        compiler_params=_cparams(1),
    )(gates_t, fbias_col)


TE = 1024


def _mlstm_kernel(*refs):
    fwd, bwd, (hf_ref, hb_ref, state) = refs[:9], refs[9:18], refs[18:]
    i = pl.program_id(1)

    @pl.when(i == 0)
    def _():
        state[...] = jnp.zeros(state.shape, F32)

    n_pairs = TE // LANES
    s_idx = lax.broadcasted_iota(jnp.int32, (LANES, LANES), 0)
    t_idx = lax.broadcasted_iota(jnp.int32, (LANES, LANES), 1)
    same_chunk = (s_idx >= ML_CHUNK) == (t_idx >= ML_CHUNK)
    upper_lanes = lax.broadcasted_iota(jnp.int32, (1, LANES), 1) >= ML_CHUNK

    def pair_body(p, carry):
        jobs = []
        for d, ((qt_r, k_r, vt_r, a_r, m_r, iw_r, en_r, ws_r, dec_r), out_r) in enumerate(
                ((fwd, hf_ref), (bwd, hb_ref))):
            pp = p if d == 0 else n_pairs - 1 - p
            srows = pl.ds(pl.multiple_of(pp * LANES, LANES), LANES)
            a_t = jnp.transpose(jnp.concatenate(
                [a_r[0, pp], jnp.zeros((LANES - 2 * ML_HEADS, LANES), F32)], axis=0))
            m_t, iw_t, en_t, ws_t, dec_t = [r[0, pp] for r in (m_r, iw_r, en_r, ws_r, dec_r)]
            dec_lo, dec_hi = dec_t[:, :ML_CHUNK], pltpu.roll(dec_t, ML_CHUNK, 1)[:, :ML_CHUNK]
            in_first = upper_lanes if d else ~upper_lanes
            tri = same_chunk & ((s_idx >= t_idx) if d else (s_idx <= t_idx))
            for h in range(ML_HEADS):
                ch = d * ML_HEADS + h
                row = lambda t: t[ch:ch + 1]
                jobs.append(dict(
                    ch=ch, tri=tri, in_first=in_first, k=k_r[0, h, srows, :], qt=qt_r[0, h, pp],
                    vt=vt_r[0, h, pp], a=a_t[:, ch:ch + 1], m=row(m_t), iw=row(iw_t), en=row(en_t), ws=row(ws_t),
                    dec_first=row(dec_hi if d else dec_lo), dec_second=row(dec_lo if d else dec_hi),
                    out=(out_r, pp, h)))
        for j in jobs:
            vt_f = j["vt"].astype(F32)
            j["c0"] = state[j["ch"]]
            j["st"] = jnp.dot(j["k"], j["qt"], preferred_element_type=F32)
            j["inter1"] = jnp.dot(j["c0"].astype(BF16), j["qt"], preferred_element_type=F32)
            j["upd1"] = jnp.dot((vt_f * jnp.where(j["in_first"], j["ws"], 0.0)).astype(BF16), j["k"],
                                preferred_element_type=F32)
            j["upd2"] = jnp.dot((vt_f * jnp.where(j["in_first"], 0.0, j["ws"])).astype(BF16), j["k"],
                                preferred_element_type=F32)
        for j in jobs:
            j["swt"] = j["st"] * jnp.exp(jnp.where(j["tri"], j["a"] - j["m"], NEG_BIG))
            j["intra"] = jnp.dot(j["vt"], j["swt"].astype(BF16), preferred_element_type=F32)
            j["c1"] = j["dec_first"] * j["c0"] + j["upd1"]
            j["inter2"] = jnp.dot(j["c1"].astype(BF16), j["qt"], preferred_element_type=F32)
        for j in jobs:
            inter = jnp.where(j["in_first"], j["inter1"], j["inter2"])
            den = jnp.sum(j["swt"], axis=0, keepdims=True) + j["iw"] * inter[ML_HD:ML_HD + 1]
            tot = j["intra"][:ML_HD] + j["iw"] * inter[:ML_HD]
            out_r, pp, h = j["out"]
            out_r[0, pp, h * ML_HD:(h + 1) * ML_HD, :] = tot / jnp.maximum(jnp.abs(den), j["en"])
            state[j["ch"]] = j["dec_second"] * j["c1"] + j["upd2"]
        return carry

    lax.fori_loop(0, n_pairs, pair_body, 0)


def _mlstm_call(qt, k, vt, a_t, m_t, iw_t, en_t, ws_t, dec_t):
    b, _, s, _ = k.shape
    nt = s // TE
    ppt = TE // LANES

    def specs(rev):
        ti = (lambda i: nt - 1 - i) if rev else (lambda i: i)
        tile = pl.BlockSpec((1, ppt, 2 * ML_HEADS, LANES), lambda b, i: (b, ti(i), 0, 0))
        return [
            pl.BlockSpec((1, ML_HEADS, ppt, ML_HD, LANES), lambda b, i: (b, 0, ti(i), 0, 0)),
            pl.BlockSpec((1, ML_HEADS, TE, ML_HD), lambda b, i: (b, 0, ti(i), 0)),
            pl.BlockSpec((1, ML_HEADS, ppt, VT_ROWS, LANES), lambda b, i: (b, 0, ti(i), 0, 0)),
            tile, tile, tile, tile, tile, tile]

    args = [qt, k, vt, a_t, m_t, iw_t, en_t, ws_t, dec_t]
    out_f = pl.BlockSpec((1, ppt, MIX_W, LANES), lambda b, i: (b, i, 0, 0))
    out_b = pl.BlockSpec((1, ppt, MIX_W, LANES), lambda b, i: (b, nt - 1 - i, 0, 0))
    return pl.pallas_call(
        _mlstm_kernel, name="mlstm_scan",
        grid=(b, nt),
        in_specs=specs(False) + specs(True),
        out_specs=(out_f, out_b),
        out_shape=(jax.ShapeDtypeStruct((b, s // LANES, MIX_W, LANES), F32),) * 2,
        scratch_shapes=[pltpu.VMEM((2 * ML_HEADS, VT_ROWS, ML_HD), F32)],
        compiler_params=_cparams(2),
    )(*args, *args)


def _mlstm_branch(cqk, cv, gates_t, conv_w, fbias):
    qt, k, vt = _mlprep_call(cqk, cv, conv_w)
    factors = _gate_call(gates_t, fbias.reshape(2 * ML_HEADS, 1), cqk.shape[0])
    return _mlstm_call(qt, k, vt, *factors)


TF = 512


def _merge_kernel(x_ref, ya_ref, o1_ref, o2_ref, o3_ref, l1_ref, l2_ref, l3_ref, hf_ref, hb_ref,
                  co_ref, yd_ref, wg_ref, bg_ref, wbr_ref, wout_ref, mng_ref, lng_ref, lnb_ref,
                  wr_ref, x1_ref, x1p_ref, aff_ref, o_scr, l_scr, *, alpha):
    x = x_ref[0]
    xb = x.astype(BF16)

    def natural_order(src_ref, scr):
        dil, width = src_ref.shape[1], src_ref.shape[3]
        if dil == 1:
            return src_ref[0, 0]
        for r in range(dil):
            for c in range(width // LANES):
                scr[c, pl.ds(r, TF // dil, stride=dil), :] = src_ref[0, r, :, c * LANES:(c + 1) * LANES]
        return jnp.concatenate([scr[c] for c in range(width // LANES)], axis=1)

    lane_head = lax.broadcasted_iota(jnp.int32, (TF, MIX_W), 1) // ML_HD
    l1, l2, l3 = [natural_order(r, l_scr.at[p]) for p, r in enumerate((l1_ref, l2_ref, l3_ref))]
    o1, o2, o3 = [natural_order(r, o_scr.at[p]) for p, r in enumerate((o1_ref, o2_ref, o3_ref))]
    lm = jnp.maximum(jnp.maximum(l1, l2), l3)
    e1, e2, e3 = jnp.exp(l1 - lm), jnp.exp(l2 - lm), jnp.exp(l3 - lm)
    inv = 1.0 / (e1 + e2 + e3)

    def per_head(w):
        out = jnp.zeros((TF, MIX_W), F32)
        for h in range(ATT_HEADS):
            out = jnp.where(lane_head == h, w[:, h:h + 1], out)
        return out

    y_b = per_head(e1 * inv) * o1 + per_head(e2 * inv) * o2 + per_head(e3 * inv) * o3
    hsum_t = jnp.concatenate([hf_ref[0, p] + hb_ref[0, p] for p in range(TF // LANES)], axis=1)
    per_head_rows = hsum_t.reshape(ML_HEADS, ML_HD, TF)
    mu = jnp.mean(per_head_rows, axis=1, keepdims=True)
    cen = per_head_rows - mu
    var = jnp.mean(cen * cen, axis=1, keepdims=True)
    hn_t = (cen * lax.rsqrt(var + LN_EPS)).reshape(MIX_W, TF)
    y_c_t = (jax.nn.sigmoid(jnp.transpose(co_ref[0])) * (hn_t * mng_ref[...])).astype(BF16)
    ys = (ya_ref[0], y_b.astype(BF16), None, yd_ref[0])
    merged = jnp.zeros((TF, D_MODEL), F32)
    for n in range(N_BRANCH):
        cols = slice(n * D_MODEL, (n + 1) * D_MODEL)
        gate = jax.nn.sigmoid(jnp.dot(xb, wg_ref[:, cols], preferred_element_type=F32) + bg_ref[:, cols])
        if ys[n] is None:
            proj = lax.dot_general(y_c_t, wbr_ref[n], (((0,), (0,)), ((), ())), preferred_element_type=F32)
        else:
            proj = jnp.dot(ys[n], wbr_ref[n], preferred_element_type=F32)
        merged = merged + gate * proj
    mix = jnp.dot(merged.astype(BF16), wout_ref[...], preferred_element_type=F32)
    x1 = _standardize(alpha * x + mix) * lng_ref[...] + lnb_ref[...]
    x1_ref[0] = x1
    x1b = x1.astype(BF16)
    x1p_ref[0] = _pack_bf16_pair(x1b[:, :D_MODEL // 2], x1b[:, D_MODEL // 2:])
    logits = lax.dot_general(wr_ref[...], x1b, (((1,), (1,)), ((), ())),
                             preferred_element_type=F32)
    ex = jnp.exp(logits - jnp.max(logits, axis=0, keepdims=True))
    aff_ref[0] = ex / jnp.sum(ex, axis=0, keepdims=True)


def _merge_call(x, ya, o_list, l_list, hf, hb, co, yd, wg, bg, wbr, wout, mng, lng, lnb, wr_t, alpha, layer):
    b, s, _ = x.shape
    per_layer = lambda shp: pl.BlockSpec((None,) + shp, lambda b, i: (layer,) + (0,) * len(shp))
    tok = lambda w: pl.BlockSpec((1, TF, w), lambda b, i: (b, i, 0))
    grouped = lambda dil, w: pl.BlockSpec((1, dil, TF // dil, w), lambda b, i: (b, 0, i, 0))
    chunked = pl.BlockSpec((1, TF // LANES, MIX_W, LANES), lambda b, i: (b, i, 0, 0))
    const = lambda shp: pl.BlockSpec(shp, lambda b, i: (0,) * len(shp))
    return pl.pallas_call(
        functools.partial(_merge_kernel, alpha=alpha), name="merge_ln_router",
        grid=(b, s // TF),
        in_specs=[tok(D_MODEL), tok(MIX_W)] + [grouped(dil, MIX_W) for _, dil in DIL_PATTERNS]
                 + [grouped(dil, LANES) for _, dil in DIL_PATTERNS]
                 + [chunked, chunked, tok(MIX_W), tok(MIX_W)]
                 + [per_layer((D_MODEL, N_BRANCH * D_MODEL)), per_layer((1, N_BRANCH * D_MODEL)),
                    per_layer((N_BRANCH, MIX_W, D_MODEL)), per_layer((D_MODEL, D_MODEL)), const((MIX_W, 1)),
                    const((1, D_MODEL)), const((1, D_MODEL)), per_layer((N_EXPERTS, D_MODEL))],
        out_specs=(tok(D_MODEL), tok(D_MODEL // 2), pl.BlockSpec((1, N_EXPERTS, TF), lambda b, i: (b, 0, i))),
        out_shape=(jax.ShapeDtypeStruct((b, s, D_MODEL), F32),
                   jax.ShapeDtypeStruct((b, s, D_MODEL // 2), jnp.int32),
                   jax.ShapeDtypeStruct((b, N_EXPERTS, s), F32)),
        scratch_shapes=[pltpu.VMEM((len(DIL_PATTERNS), MIX_W // LANES, TF, LANES), F32),
                        pltpu.VMEM((len(DIL_PATTERNS), 1, TF, LANES), F32)],
        compiler_params=_cparams(2),
    )(x, ya, *o_list, *l_list, hf, hb, co, yd, wg, bg, wbr, wout, mng, lng, lnb, wr_t)


TT = 256


def _select_kernel(aff_ref, slot_ref, *, cap):
    s = aff_ref.shape[2]
    bits = pltpu.bitcast(aff_ref[0], jnp.int32)

    def bit_step(i, thr):
        cand = thr | jnp.left_shift(jnp.int32(1), 30 - i)
        cnt = jnp.sum((bits >= cand).astype(jnp.int32), axis=1, keepdims=True)
        return jnp.where(cnt >= cap, cand, thr)

    thr = lax.fori_loop(0, 31, bit_step, jnp.zeros((N_EXPERTS, 1), jnp.int32))
    gt = bits > thr
    eq = bits == thr
    need = (cap - jnp.sum(gt.astype(jnp.int32), axis=1, keepdims=True)).astype(F32)
    upper = (lax.broadcasted_iota(jnp.int32, (TT, TT), 0)
             <= lax.broadcasted_iota(jnp.int32, (TT, TT), 1)).astype(BF16)
    eq_before = jnp.zeros((N_EXPERTS, 1), F32)
    sel_before = jnp.zeros((N_EXPERTS, 1), F32)
    for j in range(s // TT):
        cols = slice(j * TT, (j + 1) * TT)
        eq_j = eq[:, cols]
        eq_incl = eq_before + jnp.dot(eq_j.astype(BF16), upper, preferred_element_type=F32)
        sel_j = gt[:, cols] | (eq_j & (eq_incl <= need))
        sel_f = sel_j.astype(F32)
        sel_incl = sel_before + jnp.dot(sel_f.astype(BF16), upper, preferred_element_type=F32)
        slot_ref[0, :, cols] = jnp.where(sel_j, sel_incl - 1.0, -1.0).astype(jnp.int32)
        eq_before = eq_incl[:, TT - 1:TT]
        sel_before = sel_incl[:, TT - 1:TT]


def _select_call(aff_t, cap):
    b, e, s = aff_t.shape
    return pl.pallas_call(
        functools.partial(_select_kernel, cap=cap), name="expert_choice_select",
        grid=(b,),
        in_specs=[pl.BlockSpec((1, e, s), lambda i: (i, 0, 0))],
        out_specs=pl.BlockSpec((1, e, s), lambda i: (i, 0, 0)),
        out_shape=jax.ShapeDtypeStruct((b, e, s), jnp.int32),
        compiler_params=_cparams(1),
    )(aff_t)


SC_LANES = 16
SC_ROWS = 64
SC_IDX = 128
SC_SLAB = 128
SC_ZROWS = 64
CF = 1024


def _sc_dispatch_call(x_flat, slot2, aff2, seq, cap):
    n_pairs = slot2.shape[0]
    d = x_flat.shape[1]
    info = plsc.get_sparse_core_info()
    n_workers = info.num_cores * info.num_subcores
    assert n_pairs % n_workers == 0 and seq % SC_LANES == 0 and cap % (2 * SC_ROWS) == 0
    pairs_per_worker = n_pairs // n_workers
    mesh = plsc.VectorSubcoreMesh(core_axis_name="c", subcore_axis_name="s")

    @functools.partial(
        pl.kernel, mesh=mesh, name="expert_dispatch_sc",
        compiler_params=pltpu.CompilerParams(needs_layout_passes=False),
        out_type=(jax.ShapeDtypeStruct((n_pairs * cap, d), x_flat.dtype),
                  jax.ShapeDtypeStruct((n_pairs, cap // SC_IDX, SC_IDX), jnp.int32),
                  jax.ShapeDtypeStruct((n_pairs, cap // SC_IDX, SC_IDX), F32)),
        scratch_types=[pltpu.VMEM((seq,), jnp.int32), pltpu.VMEM((seq,), F32),
                       pltpu.VMEM((cap,), jnp.int32), pltpu.VMEM((cap // SC_IDX, SC_IDX), jnp.int32),
                       pltpu.VMEM((cap // SC_IDX, SC_IDX), F32),
                       pltpu.VMEM((2, SC_ROWS, d), x_flat.dtype), pltpu.SemaphoreType.DMA((2,))])
    def dispatch(x_hbm, slot_hbm, aff_hbm, xs_hbm, tok_hbm, gate_hbm,
                 slot_v, aff_v, idx_v, tok_v, gate_v, rows_v, sem):
        worker = lax.axis_index("s") * info.num_cores + lax.axis_index("c")
        lane = lax.iota(jnp.int32, SC_LANES)

        def gather(c0, buf):
            return pltpu.make_async_copy(x_hbm.at[idx_v.at[pl.ds(c0, SC_ROWS)]], rows_v.at[buf], sem.at[buf])
        for k in range(pairs_per_worker):
            pair = worker * pairs_per_worker + k
            row0 = (pair // N_EXPERTS) * seq
            pltpu.sync_copy(slot_hbm.at[pair], slot_v)
            pltpu.sync_copy(aff_hbm.at[pair], aff_v)

            @plsc.parallel_loop(0, seq, step=SC_LANES, unroll=4)
            def _(t0):
                sv = slot_v[pl.ds(t0, SC_LANES)]
                picked = sv >= 0
                hi, lo = lax.shift_right_logical(sv, 7), sv & (SC_IDX - 1)
                plsc.store_scatter(tok_v, [hi, lo], t0 + lane, mask=picked)
                plsc.store_scatter(idx_v, [sv], row0 + t0 + lane, mask=picked)
                plsc.store_scatter(gate_v, [hi, lo], aff_v[pl.ds(t0, SC_LANES)], mask=picked)

            pltpu.sync_copy(tok_v, tok_hbm.at[pair])
            pltpu.sync_copy(gate_v, gate_hbm.at[pair])

            gather(0, 0).start()

            @pl.loop(0, cap, step=2 * SC_ROWS)
            def _(c0):
                gather(c0 + SC_ROWS, 1).start()
                gather(c0, 0).wait()
                pltpu.sync_copy(rows_v.at[0], xs_hbm.at[pl.ds(pair * cap + c0, SC_ROWS)])

                @pl.when(c0 + 2 * SC_ROWS < cap)
                def _():
                    gather(c0 + 2 * SC_ROWS, 0).start()

                gather(c0 + SC_ROWS, 1).wait()
                pltpu.sync_copy(rows_v.at[1], xs_hbm.at[pl.ds(pair * cap + c0 + SC_ROWS, SC_ROWS)])

    return dispatch(x_flat, slot2, aff2)


def _expert_kernel(xs_ref, g_ref, w1_ref, w3_ref, w2_ref, ye_ref, w1_bf, w3_bf, w2_bf):
    @pl.when((pl.program_id(1) == 0) & (pl.program_id(2) == 0))
    def _():
        w1_bf[...] = w1_ref[0, 0].astype(BF16)
        w3_bf[...] = w3_ref[0, 0].astype(BF16)
        w2_bf[...] = w2_ref[0, 0].astype(BF16)

    xs = jnp.concatenate(_unpack_bf16_pair(xs_ref[0, 0]), axis=1)
    hid = (jax.nn.silu(jnp.dot(xs, w1_bf[...], preferred_element_type=F32))
           * jnp.dot(xs, w3_bf[...], preferred_element_type=F32))
    g_rows = g_ref[0, 0]
    n_rows = g_rows.shape[0]
    g_t = jnp.transpose(jnp.concatenate([g_rows, jnp.zeros((LANES - n_rows, LANES), F32)], axis=0))
    g_col = jnp.concatenate([g_t[:, r:r + 1] for r in range(n_rows)], axis=0)
    ye_ref[0, 0] = jnp.dot(hid.astype(BF16), w2_bf[...], preferred_element_type=F32) * g_col


def _expert_call(xs4, gate4, w1, w3, w2, layer):
    b, e, cap, half = xs4.shape
    d, ff = w1.shape[2], w1.shape[3]
    assert d == 2 * half
    rows = lambda w: pl.BlockSpec((1, 1, CF, w), lambda e, b, j: (b, e, j, 0))
    wspec = lambda r, c: pl.BlockSpec((1, 1, r, c), lambda e, b, j: (layer, e, 0, 0))
    return pl.pallas_call(
        _expert_kernel, name="expert_ffn",
        grid=(e, b, cap // CF),
        in_specs=[rows(half), pl.BlockSpec((1, 1, CF // LANES, LANES), lambda e, b, j: (b, e, j, 0)),
                  wspec(d, ff), wspec(d, ff), wspec(ff, d)],
        out_specs=rows(d),
        out_shape=jax.ShapeDtypeStruct((b, e, cap, d), F32),
        scratch_shapes=[pltpu.VMEM((d, ff), BF16), pltpu.VMEM((d, ff), BF16), pltpu.VMEM((ff, d), BF16)],
        compiler_params=_cparams(3),
    )(xs4, gate4, w1, w3, w2)


def _sc_combine_call(ye_flat, tok3, seq):
    n_pairs, n_chunks, _ = tok3.shape
    cap = n_chunks * SC_IDX
    d = ye_flat.shape[1]
    nb = n_pairs // N_EXPERTS
    info = plsc.get_sparse_core_info()
    assert info.num_subcores == N_EXPERTS and nb % info.num_cores == 0 and n_chunks % 2 == 0
    assert seq % (info.num_subcores * SC_ZROWS) == 0 and d % SC_SLAB == 0
    batches_per_core = nb // info.num_cores
    own_rows = seq // info.num_subcores
    mesh = plsc.VectorSubcoreMesh(core_axis_name="c", subcore_axis_name="s")

    @functools.partial(
        pl.kernel, mesh=mesh, name="expert_combine_sc",
        compiler_params=pltpu.CompilerParams(needs_layout_passes=False),
        out_type=jax.ShapeDtypeStruct((nb * seq, d), F32),
        scratch_types=[pltpu.VMEM_SHARED((seq, SC_SLAB), F32),
                       pltpu.VMEM((n_chunks, SC_IDX), jnp.int32),
                       pltpu.VMEM((2, SC_IDX, SC_SLAB), F32),
                       pltpu.VMEM((SC_ZROWS, SC_SLAB), F32),
                       pltpu.SemaphoreType.DMA((2,))])
    def combine(ye_hbm, tok_hbm, out_hbm, acc_sh, tok_v, rows_v, zero_v, sem):
        core = lax.axis_index("c")
        sub = lax.axis_index("s")

        @pl.loop(0, SC_ZROWS)
        def _(r):
            for l0 in range(0, SC_SLAB, SC_LANES):
                zero_v[r, pl.ds(l0, SC_LANES)] = jnp.zeros((SC_LANES,), F32)

        for bb in range(batches_per_core):
            batch = core * batches_per_core + bb
            pair = batch * N_EXPERTS + sub
            pltpu.sync_copy(tok_hbm.at[pair], tok_v)

            @pl.loop(0, d // SC_SLAB)
            def _(slab):
                cols = pl.ds(pl.multiple_of(slab * SC_SLAB, SC_SLAB), SC_SLAB)

                @pl.loop(0, own_rows, step=SC_ZROWS)
                def _(r0):
                    pltpu.sync_copy(zero_v, acc_sh.at[pl.ds(sub * own_rows + r0, SC_ZROWS)])

                def load(j, buf):
                    return pltpu.make_async_copy(
                        ye_hbm.at[pl.ds(pair * cap + j * SC_IDX, SC_IDX), cols], rows_v.at[buf], sem.at[buf])

                load(0, 0).start()
                plsc.subcore_barrier()

                for j in range(0, n_chunks, 2):
                    load(j + 1, 1).start()
                    load(j, 0).wait()
                    pltpu.sync_copy(rows_v.at[0], acc_sh.at[tok_v.at[j]], add=True)
                    if j + 2 < n_chunks:
                        load(j + 2, 0).start()
                    load(j + 1, 1).wait()
                    pltpu.sync_copy(rows_v.at[1], acc_sh.at[tok_v.at[j + 1]], add=True)

                plsc.subcore_barrier()
                pltpu.sync_copy(acc_sh.at[pl.ds(sub * own_rows, own_rows)],
                                out_hbm.at[pl.ds(batch * seq + sub * own_rows, own_rows), cols])

    return combine(ye_flat, tok3)


TN = 512


def _resln_kernel(x_ref, y_ref, g_ref, b_ref, o_ref, *, alpha):
    o_ref[...] = _standardize(alpha * x_ref[...] + y_ref[...]) * g_ref[...] + b_ref[...]


def _resln_call(x2d, y2d, g, bta, alpha):
    n, d = x2d.shape
    tok = pl.BlockSpec((TN, d), lambda i: (i, 0))
    vec = pl.BlockSpec((1, d), lambda i: (0, 0))
    return pl.pallas_call(
        functools.partial(_resln_kernel, alpha=alpha), name="residual_layernorm",
        grid=(n // TN,), in_specs=[tok, tok, vec, vec], out_specs=tok,
        out_shape=jax.ShapeDtypeStruct((n, d), F32),
        compiler_params=_cparams(1),
    )(x2d, y2d, g, bta)


def _expert_choice_ffn(x1p, aff_t, w1, w3, w2, layer):
    b, s, half = x1p.shape
    d = 2 * half
    cap = EC_FACTOR * s // N_EXPERTS
    slot = _select_call(aff_t, cap)
    xs, tok, gate = _sc_dispatch_call(x1p.reshape(b * s, half), slot.reshape(b * N_EXPERTS, s),
                                      aff_t.reshape(b * N_EXPERTS, s), s, cap)
    ye = _expert_call(xs.reshape(b, N_EXPERTS, cap, half), gate.reshape(b, N_EXPERTS, cap // SC_IDX, SC_IDX), w1, w3, w2,
                      layer)
    out = _sc_combine_call(ye.reshape(b * N_EXPERTS * cap, d),
                           tok, s)
    return out.reshape(b, s, d)


def _pack_pool(pool_w):
    g, gd, _ = pool_w.shape
    out = jnp.zeros((g * gd, g * gd), F32)
    for i in range(g):
        out = out.at[i * gd:(i + 1) * gd, i * gd:(i + 1) * gd].set(pool_w[i])
    return out.astype(BF16)


def _layer(layer, x, pending, alpha, bias_tiles, stacked, gm_ln_g, gm_ws, gm_bs, ml_conv, ml_fbias,
           ml_norm_g, pool_w, pool_scale, ln1_g, ln1_b, w_e1, w_e3, w_e2):
    b, s, d = x.shape
    w_cat, b_cat, wg, bg, wbr, wout, wr_t = stacked
    wscat, bsfull = _pack_gmlp(gm_ws, gm_bs)
    if pending is None:
        outs = _inproj_call(x.reshape(b * s, d), w_cat, b_cat, gm_ln_g[None], wscat, bsfull, b, layer)
    else:
        *outs, x2 = _inproj_call(pending, w_cat, b_cat, gm_ln_g[None], wscat, bsfull, b, layer, alpha)
        x = x2.reshape(b, s, d)
    ya, qkv1, qkv4, qkv16, cqk, cv, co, dx, gates_t = outs
    r3 = lambda t: t.reshape(b, s, t.shape[-1])
    o_list, l_list = [], []
    for qkv, bias in zip((qkv1, qkv4, qkv16), bias_tiles):
        o, lse = _attn_call(qkv, bias)
        o_list.append(o)
        l_list.append(lse)
    hf, hb = _mlstm_branch(r3(cqk), r3(cv), gates_t, ml_conv, ml_fbias)
    yd = _pool_call(r3(dx), _pack_pool(pool_w), pool_scale[None])
    x1, x1p, aff_t = _merge_call(
        x, r3(ya), o_list, l_list, hf, hb, r3(co), yd, wg, bg, wbr, wout,
        ml_norm_g[:, None], ln1_g[None], ln1_b[None], wr_t, alpha, layer)
    ffn = _expert_choice_ffn(x1p, aff_t, w_e1, w_e3, w_e2, layer)
    return x1.reshape(b * s, d), ffn.reshape(b * s, d)


def kernel(x, w_in, b_in, gm_ln_g, gm_ws, gm_bs, rel_bias, ml_conv, ml_fbias, ml_norm_g, pool_w,
           pool_scale, w_branch, w_out, ln1_g, ln1_b, w_router, w_e1, w_e3, w_e2, ln2_g, ln2_b):
    depth = w_in.shape[0]
    alpha = (2 * depth) ** 0.25
    bias_tiles = [_attn_bias_tile(rel_bias, window, dil) for window, dil in DIL_PATTERNS]
    b, s, d = x.shape
    n_small = 2576
    stacked = (*_pack_inproj_weights(w_in, b_in),
               w_in[:, :, n_small:].astype(BF16), b_in[:, None, n_small:], w_branch.astype(BF16),
               w_out.astype(BF16), jnp.transpose(w_router, (0, 2, 1)).astype(BF16))
    pending = None
    for l in range(depth):
        x1, ffn = _layer(l, x, pending, alpha, bias_tiles, stacked, gm_ln_g[l], gm_ws[l], gm_bs[l],
                         ml_conv[l], ml_fbias[l], ml_norm_g[l], pool_w[l], pool_scale[l],
                         ln1_g[l], ln1_b[l], w_e1, w_e3, w_e2)
        pending = (x1, ffn, ln2_g[l][None], ln2_b[l][None])
    return _resln_call(*pending, alpha).reshape(b, s, d)
```

```python
import functools
import math

import jax
import jax.numpy as jnp
import numpy as np
from jax import lax
from jax.experimental import pallas as pl
from jax.experimental.pallas import tpu as pltpu
from jax.experimental.pallas import tpu_sc as plsc

F32 = jnp.float32
BF16 = jnp.bfloat16

D_MODEL = 1024
MIX_W = 256
N_BRANCH = 4
GM_CHUNK = 128
GM_GROUPS = 4
ATT_HEADS = 4
ATT_HD = 64
DIL_PATTERNS = ((128, 1), (512, 4), (2048, 16))
ATT_BLOCK = 64
REL_BUCKETS = 32
REL_MAX_DIST = 1024
ML_HEADS = 4
ML_HD = 64
ML_CHUNK = 64
POOL_WINDOWS = (2, 4, 8, 16)
N_EXPERTS = 16
EXPERT_FF = 1024
EC_FACTOR = 2
LN_EPS = 1e-5
NEG_BIG = -1e30

V7X_VMEM_LIMIT = 56 * 1024 * 1024
LANES = 128
HALO = 8


def _cparams(n_grid, vmem=V7X_VMEM_LIMIT):
    return pltpu.CompilerParams(dimension_semantics=("arbitrary",) * n_grid,
                                vmem_limit_bytes=vmem)


def _pack_bf16_pair(lo, hi):
    lo_bits = lax.shift_right_logical(pltpu.bitcast(lo.astype(F32), jnp.int32), 16)
    return pltpu.bitcast(hi.astype(F32), jnp.int32) | lo_bits


def _unpack_bf16_pair(packed):
    lo = pltpu.bitcast(lax.shift_left(packed, 16), F32).astype(BF16)
    hi = pltpu.bitcast(packed & jnp.int32(-65536), F32).astype(BF16)
    return lo, hi


def _standardize(xf):
    mu = jnp.mean(xf, axis=-1, keepdims=True)
    var = jnp.mean(jnp.square(xf - mu), axis=-1, keepdims=True)
    return (xf - mu) * lax.rsqrt(var + LN_EPS)


TA = 512
A_COLS = 2560 + LANES


def _inproj_kernel(*refs, alpha):
    if alpha is None:
        x_ref, *refs = refs
        x = x_ref[...]
    else:
        x1_ref, y_ref, g2_ref, b2_ref, *refs = refs
        x = _standardize(alpha * x1_ref[...] + y_ref[...]) * g2_ref[...] + b2_ref[...]
        refs[-2][...] = x
        refs = refs[:-2] + refs[-1:]
    (w_ref, b_ref, lng_ref, wscat_ref, bsfull_ref,
     ya_ref, qkv1_ref, qkv4_ref, qkv16_ref, cqk_ref, cv_ref, co_ref, dx_ref, gt_ref, qkv_scr) = refs
    xb = x.astype(BF16)
    h = jnp.dot(xb, w_ref[...], preferred_element_type=F32) + b_ref[...]
    qkv1_ref[0, 0] = h[:, 512:1280].astype(BF16)
    for c in range(768 // LANES):
        qkv_scr[c] = h[:, 512 + c * LANES:512 + (c + 1) * LANES]
    for (_, dil), out_ref in zip(DIL_PATTERNS[1:], (qkv4_ref, qkv16_ref)):
        for r in range(dil):
            for c in range(768 // LANES):
                out_ref[0, r, :, c * LANES:(c + 1) * LANES] = (
                    qkv_scr[c, pl.ds(r, TA // dil, stride=dil), :].astype(BF16))
    cqk_ref[...] = h[:, 1280:1792]
    cv_ref[...] = h[:, 1792:2048].astype(BF16)
    co_ref[...] = h[:, 2048:2304]
    dx_ref[...] = h[:, 2304:2560]
    gates_t = jnp.transpose(h[:, 2560:2688])
    for j in range(TA // LANES):
        gt_ref[j] = gates_t[0:4 * ML_HEADS, j * LANES:(j + 1) * LANES]
    u = jax.nn.gelu(h[:, 0:256])
    v = jax.nn.gelu(h[:, 256:512])
    vn = _standardize(v) * lng_ref[...]
    lane_grp = lax.broadcasted_iota(jnp.int32, (GM_CHUNK, MIX_W), 1) // (MIX_W // GM_GROUPS)
    for c in range(TA // GM_CHUNK):
        vc = vn[c * GM_CHUNK:(c + 1) * GM_CHUNK]
        stacked = jnp.concatenate(
            [jnp.where(lane_grp == g, vc, 0.0).astype(BF16) for g in range(GM_GROUPS)], axis=0)
        mixed = jnp.dot(wscat_ref[...], stacked, preferred_element_type=F32) + bsfull_ref[...]
        ya_ref[c * GM_CHUNK:(c + 1) * GM_CHUNK, :] = (
            u[c * GM_CHUNK:(c + 1) * GM_CHUNK] * mixed).astype(BF16)


def _inproj_call(x_in, w_cat, b_cat, lng, wscat, bsfull, batch, alpha=None):
    fused = alpha is not None
    n = (x_in[0] if fused else x_in).shape[0]
    seq = n // batch
    tpb = seq // TA
    tok = lambda w: pl.BlockSpec((TA, w), lambda i: (i, 0))
    const = lambda s: pl.BlockSpec(s, lambda i: (0,) * len(s))
    regrouped = lambda dil: pl.BlockSpec((1, dil, TA // dil, 768), lambda i: (i // tpb, 0, i % tpb, 0))
    out_shape = (
        jax.ShapeDtypeStruct((n, 256), BF16),
    ) + tuple(jax.ShapeDtypeStruct((batch, dil, seq // dil, 768), BF16)
              for _, dil in DIL_PATTERNS) + (
        jax.ShapeDtypeStruct((n, 512), F32),
        jax.ShapeDtypeStruct((n, 256), BF16),
        jax.ShapeDtypeStruct((n, 256), F32),
        jax.ShapeDtypeStruct((n, 256), F32),
        jax.ShapeDtypeStruct((n // LANES, 4 * ML_HEADS, LANES), F32),
    )
    x_specs = [tok(D_MODEL), tok(D_MODEL), const((1, D_MODEL)), const((1, D_MODEL))] if fused else [tok(D_MODEL)]
    out_specs = ((tok(256),) + tuple(regrouped(dil) for _, dil in DIL_PATTERNS)
                 + (tok(512), tok(256), tok(256), tok(256),
                    pl.BlockSpec((TA // LANES, 4 * ML_HEADS, LANES), lambda i: (i, 0, 0))))
    if fused:
        out_specs += (tok(D_MODEL),)
        out_shape += (jax.ShapeDtypeStruct((n, D_MODEL), F32),)
    return pl.pallas_call(
        functools.partial(_inproj_kernel, alpha=alpha), name="inproj_gmlp",
        grid=(n // TA,),
        in_specs=x_specs + [const((D_MODEL, A_COLS)), const((1, A_COLS)), const((1, MIX_W)),
                            const((GM_CHUNK, GM_GROUPS * GM_CHUNK)), const((GM_CHUNK, MIX_W))],
        out_specs=out_specs,
        out_shape=out_shape,
        scratch_shapes=[pltpu.VMEM((768 // LANES, TA, LANES), F32)],
        compiler_params=_cparams(1),
    )(*(x_in if fused else (x_in,)), w_cat, b_cat, lng, wscat, bsfull)


def _pack_inproj_weights(w_in, b_in):
    pad = lambda a: jnp.pad(a, ((0, 0), (0, LANES - 4 * ML_HEADS)))
    w_cat = jnp.concatenate([w_in[:, 0:2304], w_in[:, 2320:2576], pad(w_in[:, 2304:2320])], axis=1)
    b2 = b_in[None, :]
    b_cat = jnp.concatenate([b2[:, 0:2304], b2[:, 2320:2576], pad(b2[:, 2304:2320])], axis=1)
    return w_cat.astype(BF16), b_cat


def _pack_gmlp(gm_ws, gm_bs):
    wscat = jnp.transpose(gm_ws, (1, 0, 2)).reshape(GM_CHUNK, GM_GROUPS * GM_CHUNK).astype(BF16)
    bsfull = jnp.repeat(jnp.transpose(gm_bs), MIX_W // GM_GROUPS, axis=1)
    return wscat, bsfull


def _halo_specs(t, width, n_tiles):
    r = t // HALO
    main = pl.BlockSpec((1, t, width), lambda b, i: (b, i, 0))
    prev = pl.BlockSpec((1, HALO, width), lambda b, i: (b, jnp.maximum(i * r - 1, 0), 0))
    nxt = pl.BlockSpec((1, HALO, width), lambda b, i: (b, jnp.minimum((i + 1) * r, n_tiles * r - 1), 0))
    return main, prev, nxt


def _fill_halo_scratch(buf, x_ref, p_ref, n_ref, t):
    i = pl.program_id(1)
    last = pl.num_programs(1) - 1
    buf[0:HALO, :] = jnp.where(i > 0, p_ref[0], 0.0)
    buf[HALO:HALO + t, :] = x_ref[0]
    buf[HALO + t:2 * HALO + t, :] = jnp.where(i < last, n_ref[0], 0.0)


TP = 512


def _pool_kernel(x_ref, p_ref, n_ref, w_ref, sc_ref, o_ref, buf, lvl):
    _fill_halo_scratch(buf, x_ref, p_ref, n_ref, TP)
    seq = pl.num_programs(1) * TP
    pos = pl.program_id(1) * TP + lax.broadcasted_iota(jnp.int32, (TP, 1), 0)
    lane_grp = lax.broadcasted_iota(jnp.int32, (TP, MIX_W), 1) // (MIX_W // len(POOL_WINDOWS))
    x0 = buf[HALO:HALO + TP, :]
    sums = []
    src, rows = buf, TP + 2 * HALO
    for k, win in enumerate(POOL_WINDOWS):
        half = win // 2
        rows -= half
        cur = src[0:rows, :] + src[half:rows + half, :] if k else buf[0:rows, :] + buf[1:rows + 1, :]
        if k + 1 < len(POOL_WINDOWS):
            lvl[k, 0:rows, :] = cur
            sums.append(lvl[k, HALO - half:HALO - half + TP, :])
            src = lvl.at[k]
        else:
            sums.append(cur[0:TP])
    pooled = jnp.zeros((TP, MIX_W), F32)
    for gi, win in enumerate(POOL_WINDOWS):
        half = win // 2
        cnt = (jnp.minimum(pos + half, seq) - jnp.maximum(pos - half, 0)).astype(F32)
        pooled = jnp.where(lane_grp == gi, sums[gi] / cnt - x0, pooled)
    mixed = jnp.dot(pooled.astype(BF16), w_ref[...], preferred_element_type=F32)
    o_ref[0] = (mixed * sc_ref[...]).astype(BF16)


def _pool_call(dx, w_block, scale):
    b, s, _ = dx.shape
    nt = s // TP
    main, prev, nxt = _halo_specs(TP, MIX_W, nt)
    return pl.pallas_call(
        _pool_kernel, name="pool_mixer",
        grid=(b, nt),
        in_specs=[main, prev, nxt,
                  pl.BlockSpec((MIX_W, MIX_W), lambda b, i: (0, 0)),
                  pl.BlockSpec((1, MIX_W), lambda b, i: (0, 0))],
        out_specs=pl.BlockSpec((1, TP, MIX_W), lambda b, i: (b, i, 0)),
        out_shape=jax.ShapeDtypeStruct((b, s, MIX_W), BF16),
        scratch_shapes=[pltpu.VMEM((TP + 2 * HALO, MIX_W), F32),
                        pltpu.VMEM((len(POOL_WINDOWS) - 1, TP + 2 * HALO, MIX_W), F32)],
        compiler_params=_cparams(2),
    )(dx, dx, dx, w_block, scale)


TQ = 128
TQS = 1024
TKEYS = TQ + 2 * ATT_BLOCK


def _attn_kernel(q_ref, kp_ref, km_ref, kn_ref, vp_ref, vm_ref, vn_ref, bias_ref, o_ref, lse_ref):
    i = pl.program_id(2)
    q = q_ref[0, 0] * ATT_HD ** -0.5
    k = jnp.concatenate([kp_ref[0, 0], km_ref[0, 0], kn_ref[0, 0]], axis=0)
    v = jnp.concatenate([vp_ref[0, 0], vm_ref[0, 0], vn_ref[0, 0]], axis=0)
    lane = lax.broadcasted_iota(jnp.int32, (TQ, LANES), 1)
    lane_half = lax.broadcasted_iota(jnp.int32, (1, LANES), 1) // ATT_HD
    keep = [jnp.where(lane_half == hh, 1.0, 0.0).astype(BF16) for hh in range(2)]
    n_sub = q_ref.shape[2] // TQ
    last_step = pl.num_programs(2) - 1
    for j in range(n_sub):
        if j == 0:
            variant = jnp.where(i == 0, 0, 1)
        elif j == n_sub - 1:
            variant = jnp.where(i == last_step, 2, 1)
        else:
            variant = 1
        qrows = slice(j * TQ, (j + 1) * TQ)
        krows = slice(j * TQ, j * TQ + TKEYS)
        lse_tile = jnp.zeros((TQ, LANES), F32)
        for pair in range(ATT_HEADS // 2):
            grp = slice(pair * LANES, (pair + 1) * LANES)
            q_pair, k_pair, v_pair = q[qrows, grp], k[krows, grp], v[krows, grp]
            o_pair = jnp.zeros((TQ, LANES), F32)
            for hh in range(2):
                h = 2 * pair + hh
                logits = lax.dot_general(q_pair * keep[hh], k_pair, (((1,), (1,)), ((), ())),
                                         preferred_element_type=F32) + bias_ref[variant, h]
                m = jnp.max(logits, axis=-1, keepdims=True)
                p = jnp.exp(logits - m)
                ssum = jnp.sum(p, axis=-1, keepdims=True)
                o = jnp.dot(p.astype(BF16), v_pair, preferred_element_type=F32) / ssum
                o_pair = jnp.where(lane_half == hh, o, o_pair)
                lse_tile = jnp.where(lane == h, m + jnp.log(ssum), lse_tile)
            o_ref[0, 0, qrows, grp] = o_pair
        lse_ref[0, 0, qrows, :] = lse_tile


def _attn_call(qkv, bias):
    b, dil, l, _ = qkv.shape
    tqs = min(l, TQS)
    nt = l // tqs
    r64 = tqs // ATT_BLOCK
    main = lambda c: pl.BlockSpec((1, 1, tqs, MIX_W), lambda b, r, i: (b, r, i, c))
    prev = lambda c: pl.BlockSpec((1, 1, ATT_BLOCK, MIX_W),
                                  lambda b, r, i: (b, r, jnp.maximum(i * r64 - 1, 0), c))
    nxt = lambda c: pl.BlockSpec((1, 1, ATT_BLOCK, MIX_W),
                                 lambda b, r, i: (b, r, jnp.minimum((i + 1) * r64, nt * r64 - 1), c))
    return pl.pallas_call(
        _attn_kernel, name="band_attention",
        grid=(b, dil, nt),
        in_specs=[main(0), prev(1), main(1), nxt(1), prev(2), main(2), nxt(2),
                  pl.BlockSpec((3, ATT_HEADS, TQ, TKEYS), lambda b, r, i: (0, 0, 0, 0))],
        out_specs=(pl.BlockSpec((1, 1, tqs, MIX_W), lambda b, r, i: (b, r, i, 0)),
                   pl.BlockSpec((1, 1, tqs, LANES), lambda b, r, i: (b, r, i, 0))),
        out_shape=(jax.ShapeDtypeStruct((b, dil, l, MIX_W), F32),
                   jax.ShapeDtypeStruct((b, dil, l, LANES), F32)),
        compiler_params=_cparams(3),
    )(qkv, qkv, qkv, qkv, qkv, qkv, qkv, bias)


def _t5_bucket_static(rel):
    half = REL_BUCKETS // 2
    max_exact = half // 2
    ret = np.where(rel > 0, half, 0)
    n = np.abs(rel)
    nf = np.maximum(n, 1).astype(np.float32)
    large = max_exact + (np.log(nf / np.float32(max_exact)) / np.float32(math.log(REL_MAX_DIST / max_exact))
                         * np.float32(half - max_exact)).astype(np.int32)
    large = np.minimum(large, half - 1)
    return ret + np.where(n < max_exact, n, large)


def _attn_bias_tile(rel_bias, window, dil):
    side = (window // 2) // dil
    rel = np.arange(TKEYS)[None, :] - ATT_BLOCK - np.arange(TQ)[:, None]
    n_rel = TKEYS + TQ - 1
    rel_values = np.arange(n_rel) - (ATT_BLOCK + TQ - 1)
    onehot = jax.nn.one_hot(jnp.asarray(_t5_bucket_static(dil * rel_values), jnp.int32), REL_BUCKETS, dtype=F32)
    table = jnp.einsum('nr,rh->hn', onehot, rel_bias, precision=lax.Precision.HIGHEST)
    periodic = jnp.tile(jnp.pad(table, ((0, 0), (0, 1))), (1, TQ))[:, :TQ * n_rel]
    bias = periodic.reshape(ATT_HEADS, TQ, n_rel)[:, :, TQ - 1:]
    key = np.arange(TKEYS)[None, :]
    inside = np.abs(rel) <= side
    masks = np.stack([inside & (key >= ATT_BLOCK), inside, inside & (key < ATT_BLOCK + TQ)])
    return jnp.where(jnp.asarray(masks)[:, None], bias[None], NEG_BIG)


TM = 512
VT_ROWS = ML_HD + 16


def _mlprep_kernel(x_ref, p_ref, n_ref, v_ref, w_ref, qt_out, k_out, vt_out, buf):
    _fill_halo_scratch(buf, x_ref, p_ref, n_ref, TM)
    conv = (buf[HALO - 1:HALO - 1 + TM, :] * w_ref[0:1, :] + buf[HALO:HALO + TM, :] * w_ref[1:2, :]
            + buf[HALO + 1:HALO + 1 + TM, :] * w_ref[2:3, :])
    qk = jax.nn.silu(conv)
    qt = jnp.transpose(qk[:, :MIX_W])
    vt = jnp.transpose(v_ref[0].astype(F32))
    ones_rows = jnp.where(lax.broadcasted_iota(jnp.int32, (VT_ROWS - ML_HD, LANES), 0) == 0, 1.0, 0.0)
    for h in range(ML_HEADS):
        sl = slice(h * ML_HD, (h + 1) * ML_HD)
        k_out[0, h] = (qk[:, MIX_W + h * ML_HD:MIX_W + (h + 1) * ML_HD] * ML_HD ** -0.5).astype(BF16)
        for p in range(TM // LANES):
            pl_ = slice(p * LANES, (p + 1) * LANES)
            qt_out[0, h, p] = qt[sl, pl_].astype(BF16)
            vt_out[0, h, p] = jnp.concatenate([vt[sl, pl_], ones_rows], axis=0).astype(BF16)


def _mlprep_call(cqk, cv, conv_w):
    b, s, _ = cqk.shape
    nt = s // TM
    n_pairs = s // LANES
    ppt = TM // LANES
    main, prev, nxt = _halo_specs(TM, 2 * MIX_W, nt)
    return pl.pallas_call(
        _mlprep_kernel, name="mlstm_prep",
        grid=(b, nt),
        in_specs=[main, prev, nxt,
                  pl.BlockSpec((1, TM, MIX_W), lambda b, i: (b, i, 0)),
                  pl.BlockSpec((3, 2 * MIX_W), lambda b, i: (0, 0))],
        out_specs=(pl.BlockSpec((1, ML_HEADS, ppt, ML_HD, LANES), lambda b, i: (b, 0, i, 0, 0)),
                   pl.BlockSpec((1, ML_HEADS, TM, ML_HD), lambda b, i: (b, 0, i, 0)),
                   pl.BlockSpec((1, ML_HEADS, ppt, VT_ROWS, LANES), lambda b, i: (b, 0, i, 0, 0))),
        out_shape=(jax.ShapeDtypeStruct((b, ML_HEADS, n_pairs, ML_HD, LANES), BF16),
                   jax.ShapeDtypeStruct((b, ML_HEADS, s, ML_HD), BF16),
                   jax.ShapeDtypeStruct((b, ML_HEADS, n_pairs, VT_ROWS, LANES), BF16)),
        scratch_shapes=[pltpu.VMEM((TM + 2 * HALO, 2 * MIX_W), F32)],
        compiler_params=_cparams(2),
    )(cqk, cqk, cqk, cv, conv_w)


def _gate_kernel(g_ref, fb_ref, a_ref, m_ref, iw_ref, en_ref, ws_ref, dec_ref):
    x = g_ref[...]
    n_pairs = x.shape[0]
    n_ch = 2 * ML_HEADS
    lane = lax.broadcasted_iota(jnp.int32, (1, 1, LANES), 2)
    t_in = lane % ML_CHUNK
    second = lane >= ML_CHUNK
    fwd_row = lax.broadcasted_iota(jnp.int32, (1, n_ch, 1), 1) < ML_HEADS
    li = x[:, 0:n_ch, :]
    z = x[:, n_ch:2 * n_ch, :] + fb_ref[...]
    lf = jnp.minimum(z, 0.0) - jnp.log1p(jnp.exp(-jnp.abs(z)))

    def within_chunk(v, op, ident, prefix):
        s = 1
        while s < ML_CHUNK:
            nb = pltpu.roll(v, s if prefix else LANES - s, 2)
            ok = (t_in >= s) if prefix else (t_in < ML_CHUNK - s)
            v = op(v, jnp.where(ok, nb, ident))
            s *= 2
        return v

    pre = within_chunk(lf, jnp.add, 0.0, True)
    suf = within_chunk(lf, jnp.add, 0.0, False)
    g = pre + suf - lf
    b = jnp.where(fwd_row, pre, suf)
    a = li - b
    cm_pre = within_chunk(a, jnp.maximum, -jnp.inf, True)
    cm_suf = within_chunk(a, jnp.maximum, -jnp.inf, False)
    cm = jnp.where(fwd_row, cm_pre, cm_suf)
    amax = jnp.maximum(cm_pre, cm_suf)

    def shift_pairs(v, k, fill):
        pad = jnp.full((abs(k),) + v.shape[1:], fill, F32)
        return (jnp.concatenate([pad, v[:n_pairs - k]], axis=0) if k > 0
                else jnp.concatenate([v[-k:], pad], axis=0))

    def from_chunk(v, dist, fill, forward):
        if dist == 1:
            y = pltpu.roll(v, ML_CHUNK, 2)
            if forward:
                return jnp.where(second, y, shift_pairs(y, 1, fill))
            return jnp.where(second, shift_pairs(y, -1, fill), y)
        return shift_pairs(v, dist // 2 if forward else -(dist // 2), fill)

    def running_stabiliser(forward):
        big_g, big_a = g, amax + g
        dist = 1
        while dist < 2 * n_pairs:
            gp = from_chunk(big_g, dist, 0.0, forward)
            ap = from_chunk(big_a, dist, -jnp.inf, forward)
            big_g, big_a = gp + big_g, jnp.maximum(ap + big_g, big_a)
            dist *= 2
        ge = from_chunk(big_g, 1, 0.0, forward)
        ae = from_chunk(big_a, 1, -jnp.inf, forward)
        return jnp.maximum(ge, ae)

    m_chunk = jnp.where(fwd_row, running_stabiliser(True), running_stabiliser(False))
    m_t = jnp.maximum(cm, m_chunk)
    m_last = jnp.maximum(amax, m_chunk)
    a_ref[0] = a
    m_ref[0] = m_t
    iw_ref[0] = jnp.exp(m_chunk - m_t)
    en_ref[0] = jnp.exp(-(b + m_t))
    ws_ref[0] = jnp.exp(a - m_last)
    dec_ref[0] = jnp.exp(m_chunk - m_last)


def _gate_call(gates_t, fbias_col, batch):
    n_pairs = gates_t.shape[0] // batch
    n_ch = 2 * ML_HEADS
    out = pl.BlockSpec((1, n_pairs, n_ch, LANES), lambda i: (i, 0, 0, 0))
    return pl.pallas_call(
        _gate_kernel, name="mlstm_gates",
        grid=(batch,),
        in_specs=[pl.BlockSpec((n_pairs, 2 * n_ch, LANES), lambda i: (i, 0, 0)),
                  pl.BlockSpec((n_ch, 1), lambda i: (0, 0))],
        out_specs=(out,) * 6,
        out_shape=(jax.ShapeDtypeStruct((batch, n_pairs, n_ch, LANES), F32),) * 6,
        compiler_params=_cparams(1),
    )(gates_t, fbias_col)


TE = 1024


def _mlstm_kernel(*refs):
    fwd, bwd, (hf_ref, hb_ref, state) = refs[:9], refs[9:18], refs[18:]
    i = pl.program_id(1)

    @pl.when(i == 0)
    def _():
        state[...] = jnp.zeros(state.shape, F32)

    n_pairs = TE // LANES
    s_idx = lax.broadcasted_iota(jnp.int32, (LANES, LANES), 0)
    t_idx = lax.broadcasted_iota(jnp.int32, (LANES, LANES), 1)
    same_chunk = (s_idx >= ML_CHUNK) == (t_idx >= ML_CHUNK)
    upper_lanes = lax.broadcasted_iota(jnp.int32, (1, LANES), 1) >= ML_CHUNK

    def pair_body(p, carry):
        jobs = []
        for d, ((qt_r, k_r, vt_r, a_r, m_r, iw_r, en_r, ws_r, dec_r), out_r) in enumerate(
                ((fwd, hf_ref), (bwd, hb_ref))):
            pp = p if d == 0 else n_pairs - 1 - p
            srows = pl.ds(pl.multiple_of(pp * LANES, LANES), LANES)
            a_t = jnp.transpose(jnp.concatenate(
                [a_r[0, pp], jnp.zeros((LANES - 2 * ML_HEADS, LANES), F32)], axis=0))
            m_t, iw_t, en_t, ws_t, dec_t = [r[0, pp] for r in (m_r, iw_r, en_r, ws_r, dec_r)]
            dec_lo, dec_hi = dec_t[:, :ML_CHUNK], pltpu.roll(dec_t, ML_CHUNK, 1)[:, :ML_CHUNK]
            in_first = upper_lanes if d else ~upper_lanes
            tri = same_chunk & ((s_idx >= t_idx) if d else (s_idx <= t_idx))
            for h in range(ML_HEADS):
                ch = d * ML_HEADS + h
                row = lambda t: t[ch:ch + 1]
                jobs.append(dict(
                    ch=ch, tri=tri, in_first=in_first, k=k_r[0, h, srows, :], qt=qt_r[0, h, pp],
                    vt=vt_r[0, h, pp], a=a_t[:, ch:ch + 1], m=row(m_t), iw=row(iw_t), en=row(en_t), ws=row(ws_t),
                    dec_first=row(dec_hi if d else dec_lo), dec_second=row(dec_lo if d else dec_hi),
                    out=(out_r, pp, h)))
        for j in jobs:
            vt_f = j["vt"].astype(F32)
            j["c0"] = state[j["ch"]]
            j["st"] = jnp.dot(j["k"], j["qt"], preferred_element_type=F32)
            j["inter1"] = jnp.dot(j["c0"].astype(BF16), j["qt"], preferred_element_type=F32)
            j["upd1"] = jnp.dot((vt_f * jnp.where(j["in_first"], j["ws"], 0.0)).astype(BF16), j["k"],
                                preferred_element_type=F32)
            j["upd2"] = jnp.dot((vt_f * jnp.where(j["in_first"], 0.0, j["ws"])).astype(BF16), j["k"],
                                preferred_element_type=F32)
        for j in jobs:
            j["swt"] = j["st"] * jnp.exp(jnp.where(j["tri"], j["a"] - j["m"], NEG_BIG))
            j["intra"] = jnp.dot(j["vt"], j["swt"].astype(BF16), preferred_element_type=F32)
            j["c1"] = j["dec_first"] * j["c0"] + j["upd1"]
            j["inter2"] = jnp.dot(j["c1"].astype(BF16), j["qt"], preferred_element_type=F32)
        for j in jobs:
            inter = jnp.where(j["in_first"], j["inter1"], j["inter2"])
            den = jnp.sum(j["swt"], axis=0, keepdims=True) + j["iw"] * inter[ML_HD:ML_HD + 1]
            tot = j["intra"][:ML_HD] + j["iw"] * inter[:ML_HD]
            out_r, pp, h = j["out"]
            out_r[0, pp, h * ML_HD:(h + 1) * ML_HD, :] = tot / jnp.maximum(jnp.abs(den), j["en"])
            state[j["ch"]] = j["dec_second"] * j["c1"] + j["upd2"]
        return carry

    lax.fori_loop(0, n_pairs, pair_body, 0)


def _mlstm_call(qt, k, vt, a_t, m_t, iw_t, en_t, ws_t, dec_t):
    b, _, s, _ = k.shape
    nt = s // TE
    ppt = TE // LANES

    def specs(rev):
        ti = (lambda i: nt - 1 - i) if rev else (lambda i: i)
        tile = pl.BlockSpec((1, ppt, 2 * ML_HEADS, LANES), lambda b, i: (b, ti(i), 0, 0))
        return [
            pl.BlockSpec((1, ML_HEADS, ppt, ML_HD, LANES), lambda b, i: (b, 0, ti(i), 0, 0)),
            pl.BlockSpec((1, ML_HEADS, TE, ML_HD), lambda b, i: (b, 0, ti(i), 0)),
            pl.BlockSpec((1, ML_HEADS, ppt, VT_ROWS, LANES), lambda b, i: (b, 0, ti(i), 0, 0)),
            tile, tile, tile, tile, tile, tile]

    args = [qt, k, vt, a_t, m_t, iw_t, en_t, ws_t, dec_t]
    out_f = pl.BlockSpec((1, ppt, MIX_W, LANES), lambda b, i: (b, i, 0, 0))
    out_b = pl.BlockSpec((1, ppt, MIX_W, LANES), lambda b, i: (b, nt - 1 - i, 0, 0))
    return pl.pallas_call(
        _mlstm_kernel, name="mlstm_scan",
        grid=(b, nt),
        in_specs=specs(False) + specs(True),
        out_specs=(out_f, out_b),
        out_shape=(jax.ShapeDtypeStruct((b, s // LANES, MIX_W, LANES), F32),) * 2,
        scratch_shapes=[pltpu.VMEM((2 * ML_HEADS, VT_ROWS, ML_HD), F32)],
        compiler_params=_cparams(2),
    )(*args, *args)


def _mlstm_branch(cqk, cv, gates_t, conv_w, fbias):
    qt, k, vt = _mlprep_call(cqk, cv, conv_w)
    factors = _gate_call(gates_t, fbias.reshape(2 * ML_HEADS, 1), cqk.shape[0])
    return _mlstm_call(qt, k, vt, *factors)


TF = 512


def _merge_kernel(x_ref, ya_ref, o1_ref, o2_ref, o3_ref, l1_ref, l2_ref, l3_ref, hf_ref, hb_ref,
                  co_ref, yd_ref, wg_ref, bg_ref, wbr_ref, wout_ref, mng_ref, lng_ref, lnb_ref,
                  wr_ref, x1_ref, x1p_ref, aff_ref, o_scr, l_scr, *, alpha):
    x = x_ref[0]
    xb = x.astype(BF16)

    def natural_order(src_ref, scr):
        dil, width = src_ref.shape[1], src_ref.shape[3]
        if dil == 1:
            return src_ref[0, 0]
        for r in range(dil):
            for c in range(width // LANES):
                scr[c, pl.ds(r, TF // dil, stride=dil), :] = src_ref[0, r, :, c * LANES:(c + 1) * LANES]
        return jnp.concatenate([scr[c] for c in range(width // LANES)], axis=1)

    lane_head = lax.broadcasted_iota(jnp.int32, (TF, MIX_W), 1) // ML_HD
    l1, l2, l3 = [natural_order(r, l_scr.at[p]) for p, r in enumerate((l1_ref, l2_ref, l3_ref))]
    o1, o2, o3 = [natural_order(r, o_scr.at[p]) for p, r in enumerate((o1_ref, o2_ref, o3_ref))]
    lm = jnp.maximum(jnp.maximum(l1, l2), l3)
    e1, e2, e3 = jnp.exp(l1 - lm), jnp.exp(l2 - lm), jnp.exp(l3 - lm)
    inv = 1.0 / (e1 + e2 + e3)

    def per_head(w):
        out = jnp.zeros((TF, MIX_W), F32)
        for h in range(ATT_HEADS):
            out = jnp.where(lane_head == h, w[:, h:h + 1], out)
        return out

    y_b = per_head(e1 * inv) * o1 + per_head(e2 * inv) * o2 + per_head(e3 * inv) * o3
    hsum_t = jnp.concatenate([hf_ref[0, p] + hb_ref[0, p] for p in range(TF // LANES)], axis=1)
    per_head_rows = hsum_t.reshape(ML_HEADS, ML_HD, TF)
    mu = jnp.mean(per_head_rows, axis=1, keepdims=True)
    cen = per_head_rows - mu
    var = jnp.mean(cen * cen, axis=1, keepdims=True)
    hn_t = (cen * lax.rsqrt(var + LN_EPS)).reshape(MIX_W, TF)
    y_c_t = (jax.nn.sigmoid(jnp.transpose(co_ref[0])) * (hn_t * mng_ref[...])).astype(BF16)
    ys = (ya_ref[0], y_b.astype(BF16), None, yd_ref[0])
    merged = jnp.zeros((TF, D_MODEL), F32)
    for n in range(N_BRANCH):
        cols = slice(n * D_MODEL, (n + 1) * D_MODEL)
        gate = jax.nn.sigmoid(jnp.dot(xb, wg_ref[:, cols], preferred_element_type=F32) + bg_ref[:, cols])
        if ys[n] is None:
            proj = lax.dot_general(y_c_t, wbr_ref[n], (((0,), (0,)), ((), ())), preferred_element_type=F32)
        else:
            proj = jnp.dot(ys[n], wbr_ref[n], preferred_element_type=F32)
        merged = merged + gate * proj
    mix = jnp.dot(merged.astype(BF16), wout_ref[...], preferred_element_type=F32)
    x1 = _standardize(alpha * x + mix) * lng_ref[...] + lnb_ref[...]
    x1_ref[0] = x1
    x1b = x1.astype(BF16)
    x1p_ref[0] = _pack_bf16_pair(x1b[:, :D_MODEL // 2], x1b[:, D_MODEL // 2:])
    logits = lax.dot_general(wr_ref[...], x1b, (((1,), (1,)), ((), ())),
                             preferred_element_type=F32)
    ex = jnp.exp(logits - jnp.max(logits, axis=0, keepdims=True))
    aff_ref[0] = ex / jnp.sum(ex, axis=0, keepdims=True)


def _merge_call(x, ya, o_list, l_list, hf, hb, co, yd, wg, bg, wbr, wout, mng, lng, lnb, wr_t, alpha):
    b, s, _ = x.shape
    tok = lambda w: pl.BlockSpec((1, TF, w), lambda b, i: (b, i, 0))
    grouped = lambda dil, w: pl.BlockSpec((1, dil, TF // dil, w), lambda b, i: (b, 0, i, 0))
    chunked = pl.BlockSpec((1, TF // LANES, MIX_W, LANES), lambda b, i: (b, i, 0, 0))
    const = lambda shp: pl.BlockSpec(shp, lambda b, i: (0,) * len(shp))
    return pl.pallas_call(
        functools.partial(_merge_kernel, alpha=alpha), name="merge_ln_router",
        grid=(b, s // TF),
        in_specs=[tok(D_MODEL), tok(MIX_W)] + [grouped(dil, MIX_W) for _, dil in DIL_PATTERNS]
                 + [grouped(dil, LANES) for _, dil in DIL_PATTERNS]
                 + [chunked, chunked, tok(MIX_W), tok(MIX_W)]
                 + [const((D_MODEL, N_BRANCH * D_MODEL)), const((1, N_BRANCH * D_MODEL)),
                    const((N_BRANCH, MIX_W, D_MODEL)), const((D_MODEL, D_MODEL)), const((MIX_W, 1)),
                    const((1, D_MODEL)), const((1, D_MODEL)), const((N_EXPERTS, D_MODEL))],
        out_specs=(tok(D_MODEL), tok(D_MODEL // 2), pl.BlockSpec((1, N_EXPERTS, TF), lambda b, i: (b, 0, i))),
        out_shape=(jax.ShapeDtypeStruct((b, s, D_MODEL), F32),
                   jax.ShapeDtypeStruct((b, s, D_MODEL // 2), jnp.int32),
                   jax.ShapeDtypeStruct((b, N_EXPERTS, s), F32)),
        scratch_shapes=[pltpu.VMEM((len(DIL_PATTERNS), MIX_W // LANES, TF, LANES), F32),
                        pltpu.VMEM((len(DIL_PATTERNS), 1, TF, LANES), F32)],
        compiler_params=_cparams(2),
    )(x, ya, *o_list, *l_list, hf, hb, co, yd, wg, bg, wbr, wout, mng, lng, lnb, wr_t)


TT = 256


def _select_kernel(aff_ref, slot_ref, *, cap):
    s = aff_ref.shape[2]
    bits = pltpu.bitcast(aff_ref[0], jnp.int32)

    def bit_step(i, thr):
        cand = thr | jnp.left_shift(jnp.int32(1), 30 - i)
        cnt = jnp.sum((bits >= cand).astype(jnp.int32), axis=1, keepdims=True)
        return jnp.where(cnt >= cap, cand, thr)

    thr = lax.fori_loop(0, 31, bit_step, jnp.zeros((N_EXPERTS, 1), jnp.int32))
    gt = bits > thr
    eq = bits == thr
    need = (cap - jnp.sum(gt.astype(jnp.int32), axis=1, keepdims=True)).astype(F32)
    upper = (lax.broadcasted_iota(jnp.int32, (TT, TT), 0)
             <= lax.broadcasted_iota(jnp.int32, (TT, TT), 1)).astype(BF16)
    eq_before = jnp.zeros((N_EXPERTS, 1), F32)
    sel_before = jnp.zeros((N_EXPERTS, 1), F32)
    for j in range(s // TT):
        cols = slice(j * TT, (j + 1) * TT)
        eq_j = eq[:, cols]
        eq_incl = eq_before + jnp.dot(eq_j.astype(BF16), upper, preferred_element_type=F32)
        sel_j = gt[:, cols] | (eq_j & (eq_incl <= need))
        sel_f = sel_j.astype(F32)
        sel_incl = sel_before + jnp.dot(sel_f.astype(BF16), upper, preferred_element_type=F32)
        slot_ref[0, :, cols] = jnp.where(sel_j, sel_incl - 1.0, -1.0).astype(jnp.int32)
        eq_before = eq_incl[:, TT - 1:TT]
        sel_before = sel_incl[:, TT - 1:TT]


def _select_call(aff_t, cap):
    b, e, s = aff_t.shape
    return pl.pallas_call(
        functools.partial(_select_kernel, cap=cap), name="expert_choice_select",
        grid=(b,),
        in_specs=[pl.BlockSpec((1, e, s), lambda i: (i, 0, 0))],
        out_specs=pl.BlockSpec((1, e, s), lambda i: (i, 0, 0)),
        out_shape=jax.ShapeDtypeStruct((b, e, s), jnp.int32),
        compiler_params=_cparams(1),
    )(aff_t)


SC_LANES = 16
SC_ROWS = 64
SC_IDX = 128
SC_SLAB = 128
SC_ZROWS = 64
CF = 1024


def _sc_dispatch_call(x_flat, slot2, aff2, seq, cap):
    n_pairs = slot2.shape[0]
    d = x_flat.shape[1]
    info = plsc.get_sparse_core_info()
    n_workers = info.num_cores * info.num_subcores
    assert n_pairs % n_workers == 0 and seq % SC_LANES == 0 and cap % (2 * SC_ROWS) == 0
    pairs_per_worker = n_pairs // n_workers
    mesh = plsc.VectorSubcoreMesh(core_axis_name="c", subcore_axis_name="s")

    @functools.partial(
        pl.kernel, mesh=mesh, name="expert_dispatch_sc",
        compiler_params=pltpu.CompilerParams(needs_layout_passes=False),
        out_type=(jax.ShapeDtypeStruct((n_pairs * cap, d), x_flat.dtype),
                  jax.ShapeDtypeStruct((n_pairs, cap // SC_IDX, SC_IDX), jnp.int32),
                  jax.ShapeDtypeStruct((n_pairs, cap // SC_IDX, SC_IDX), F32)),
        scratch_types=[pltpu.VMEM((seq,), jnp.int32), pltpu.VMEM((seq,), F32),
                       pltpu.VMEM((cap,), jnp.int32), pltpu.VMEM((cap // SC_IDX, SC_IDX), jnp.int32),
                       pltpu.VMEM((cap // SC_IDX, SC_IDX), F32),
                       pltpu.VMEM((2, SC_ROWS, d), x_flat.dtype), pltpu.SemaphoreType.DMA((2,))])
    def dispatch(x_hbm, slot_hbm, aff_hbm, xs_hbm, tok_hbm, gate_hbm,
                 slot_v, aff_v, idx_v, tok_v, gate_v, rows_v, sem):
        worker = lax.axis_index("s") * info.num_cores + lax.axis_index("c")
        lane = lax.iota(jnp.int32, SC_LANES)

        def gather(c0, buf):
            return pltpu.make_async_copy(x_hbm.at[idx_v.at[pl.ds(c0, SC_ROWS)]], rows_v.at[buf], sem.at[buf])
        for k in range(pairs_per_worker):
            pair = worker * pairs_per_worker + k
            row0 = (pair // N_EXPERTS) * seq
            pltpu.sync_copy(slot_hbm.at[pair], slot_v)
            pltpu.sync_copy(aff_hbm.at[pair], aff_v)

            @plsc.parallel_loop(0, seq, step=SC_LANES, unroll=4)
            def _(t0):
                sv = slot_v[pl.ds(t0, SC_LANES)]
                picked = sv >= 0
                hi, lo = lax.shift_right_logical(sv, 7), sv & (SC_IDX - 1)
                plsc.store_scatter(tok_v, [hi, lo], t0 + lane, mask=picked)
                plsc.store_scatter(idx_v, [sv], row0 + t0 + lane, mask=picked)
                plsc.store_scatter(gate_v, [hi, lo], aff_v[pl.ds(t0, SC_LANES)], mask=picked)

            pltpu.sync_copy(tok_v, tok_hbm.at[pair])
            pltpu.sync_copy(gate_v, gate_hbm.at[pair])

            gather(0, 0).start()

            @pl.loop(0, cap, step=2 * SC_ROWS)
            def _(c0):
                gather(c0 + SC_ROWS, 1).start()
                gather(c0, 0).wait()
                pltpu.sync_copy(rows_v.at[0], xs_hbm.at[pl.ds(pair * cap + c0, SC_ROWS)])

                @pl.when(c0 + 2 * SC_ROWS < cap)
                def _():
                    gather(c0 + 2 * SC_ROWS, 0).start()

                gather(c0 + SC_ROWS, 1).wait()
                pltpu.sync_copy(rows_v.at[1], xs_hbm.at[pl.ds(pair * cap + c0 + SC_ROWS, SC_ROWS)])

    return dispatch(x_flat, slot2, aff2)


def _expert_kernel(xs_ref, g_ref, w1_ref, w3_ref, w2_ref, ye_ref, w1_bf, w3_bf, w2_bf):
    @pl.when((pl.program_id(1) == 0) & (pl.program_id(2) == 0))
    def _():
        w1_bf[...] = w1_ref[0, 0].astype(BF16)
        w3_bf[...] = w3_ref[0, 0].astype(BF16)
        w2_bf[...] = w2_ref[0, 0].astype(BF16)

    xs = jnp.concatenate(_unpack_bf16_pair(xs_ref[0, 0]), axis=1)
    hid = (jax.nn.silu(jnp.dot(xs, w1_bf[...], preferred_element_type=F32))
           * jnp.dot(xs, w3_bf[...], preferred_element_type=F32))
    g_rows = g_ref[0, 0]
    n_rows = g_rows.shape[0]
    g_t = jnp.transpose(jnp.concatenate([g_rows, jnp.zeros((LANES - n_rows, LANES), F32)], axis=0))
    g_col = jnp.concatenate([g_t[:, r:r + 1] for r in range(n_rows)], axis=0)
    ye_ref[0, 0] = jnp.dot(hid.astype(BF16), w2_bf[...], preferred_element_type=F32) * g_col


def _expert_call(xs4, gate4, w1, w3, w2, layer):
    b, e, cap, half = xs4.shape
    d, ff = w1.shape[2], w1.shape[3]
    assert d == 2 * half
    rows = lambda w: pl.BlockSpec((1, 1, CF, w), lambda e, b, j: (b, e, j, 0))
    wspec = lambda r, c: pl.BlockSpec((1, 1, r, c), lambda e, b, j: (layer, e, 0, 0))
    return pl.pallas_call(
        _expert_kernel, name="expert_ffn",
        grid=(e, b, cap // CF),
        in_specs=[rows(half), pl.BlockSpec((1, 1, CF // LANES, LANES), lambda e, b, j: (b, e, j, 0)),
                  wspec(d, ff), wspec(d, ff), wspec(ff, d)],
        out_specs=rows(d),
        out_shape=jax.ShapeDtypeStruct((b, e, cap, d), F32),
        scratch_shapes=[pltpu.VMEM((d, ff), BF16), pltpu.VMEM((d, ff), BF16), pltpu.VMEM((ff, d), BF16)],
        compiler_params=_cparams(3),
    )(xs4, gate4, w1, w3, w2)


def _sc_combine_call(ye_flat, tok3, seq):
    n_pairs, n_chunks, _ = tok3.shape
    cap = n_chunks * SC_IDX
    d = ye_flat.shape[1]
    nb = n_pairs // N_EXPERTS
    info = plsc.get_sparse_core_info()
    assert info.num_subcores == N_EXPERTS and nb % info.num_cores == 0 and n_chunks % 2 == 0
    assert seq % (info.num_subcores * SC_ZROWS) == 0 and d % SC_SLAB == 0
    batches_per_core = nb // info.num_cores
    own_rows = seq // info.num_subcores
    mesh = plsc.VectorSubcoreMesh(core_axis_name="c", subcore_axis_name="s")

    @functools.partial(
        pl.kernel, mesh=mesh, name="expert_combine_sc",
        compiler_params=pltpu.CompilerParams(needs_layout_passes=False),
        out_type=jax.ShapeDtypeStruct((nb * seq, d), F32),
        scratch_types=[pltpu.VMEM_SHARED((seq, SC_SLAB), F32),
                       pltpu.VMEM((n_chunks, SC_IDX), jnp.int32),
                       pltpu.VMEM((2, SC_IDX, SC_SLAB), F32),
                       pltpu.VMEM((SC_ZROWS, SC_SLAB), F32),
                       pltpu.SemaphoreType.DMA((2,))])
    def combine(ye_hbm, tok_hbm, out_hbm, acc_sh, tok_v, rows_v, zero_v, sem):
        core = lax.axis_index("c")
        sub = lax.axis_index("s")

        @pl.loop(0, SC_ZROWS)
        def _(r):
            for l0 in range(0, SC_SLAB, SC_LANES):
                zero_v[r, pl.ds(l0, SC_LANES)] = jnp.zeros((SC_LANES,), F32)

        for bb in range(batches_per_core):
            batch = core * batches_per_core + bb
            pair = batch * N_EXPERTS + sub
            pltpu.sync_copy(tok_hbm.at[pair], tok_v)

            @pl.loop(0, d // SC_SLAB)
            def _(slab):
                cols = pl.ds(pl.multiple_of(slab * SC_SLAB, SC_SLAB), SC_SLAB)

                @pl.loop(0, own_rows, step=SC_ZROWS)
                def _(r0):
                    pltpu.sync_copy(zero_v, acc_sh.at[pl.ds(sub * own_rows + r0, SC_ZROWS)])

                def load(j, buf):
                    return pltpu.make_async_copy(
                        ye_hbm.at[pl.ds(pair * cap + j * SC_IDX, SC_IDX), cols], rows_v.at[buf], sem.at[buf])

                load(0, 0).start()
                plsc.subcore_barrier()

                for j in range(0, n_chunks, 2):
                    load(j + 1, 1).start()
                    load(j, 0).wait()
                    pltpu.sync_copy(rows_v.at[0], acc_sh.at[tok_v.at[j]], add=True)
                    if j + 2 < n_chunks:
                        load(j + 2, 0).start()
                    load(j + 1, 1).wait()
                    pltpu.sync_copy(rows_v.at[1], acc_sh.at[tok_v.at[j + 1]], add=True)

                plsc.subcore_barrier()
                pltpu.sync_copy(acc_sh.at[pl.ds(sub * own_rows, own_rows)],
                                out_hbm.at[pl.ds(batch * seq + sub * own_rows, own_rows), cols])

    return combine(ye_flat, tok3)


TN = 512


def _resln_kernel(x_ref, y_ref, g_ref, b_ref, o_ref, *, alpha):
    o_ref[...] = _standardize(alpha * x_ref[...] + y_ref[...]) * g_ref[...] + b_ref[...]


def _resln_call(x2d, y2d, g, bta, alpha):
    n, d = x2d.shape
    tok = pl.BlockSpec((TN, d), lambda i: (i, 0))
    vec = pl.BlockSpec((1, d), lambda i: (0, 0))
    return pl.pallas_call(
        functools.partial(_resln_kernel, alpha=alpha), name="residual_layernorm",
        grid=(n // TN,), in_specs=[tok, tok, vec, vec], out_specs=tok,
        out_shape=jax.ShapeDtypeStruct((n, d), F32),
        compiler_params=_cparams(1),
    )(x2d, y2d, g, bta)


def _expert_choice_ffn(x1p, aff_t, w1, w3, w2, layer):
    b, s, half = x1p.shape
    d = 2 * half
    cap = EC_FACTOR * s // N_EXPERTS
    slot = _select_call(aff_t, cap)
    xs, tok, gate = _sc_dispatch_call(x1p.reshape(b * s, half), slot.reshape(b * N_EXPERTS, s),
                                      aff_t.reshape(b * N_EXPERTS, s), s, cap)
    ye = _expert_call(xs.reshape(b, N_EXPERTS, cap, half), gate.reshape(b, N_EXPERTS, cap // SC_IDX, SC_IDX), w1, w3, w2,
                      layer)
    out = _sc_combine_call(ye.reshape(b * N_EXPERTS * cap, d),
                           tok, s)
    return out.reshape(b, s, d)


def _pack_pool(pool_w):
    g, gd, _ = pool_w.shape
    out = jnp.zeros((g * gd, g * gd), F32)
    for i in range(g):
        out = out.at[i * gd:(i + 1) * gd, i * gd:(i + 1) * gd].set(pool_w[i])
    return out.astype(BF16)


def _layer(layer, x, pending, alpha, bias_tiles, w_in, b_in, gm_ln_g, gm_ws, gm_bs, ml_conv, ml_fbias,
           ml_norm_g, pool_w, pool_scale, w_branch, w_out, ln1_g, ln1_b, w_router, w_e1, w_e3, w_e2):
    b, s, d = x.shape
    n_small = 2576
    w_cat, b_cat = _pack_inproj_weights(w_in, b_in)
    wscat, bsfull = _pack_gmlp(gm_ws, gm_bs)
    if pending is None:
        outs = _inproj_call(x.reshape(b * s, d), w_cat, b_cat, gm_ln_g[None], wscat, bsfull, b)
    else:
        *outs, x2 = _inproj_call(pending, w_cat, b_cat, gm_ln_g[None], wscat, bsfull, b, alpha)
        x = x2.reshape(b, s, d)
    ya, qkv1, qkv4, qkv16, cqk, cv, co, dx, gates_t = outs
    r3 = lambda t: t.reshape(b, s, t.shape[-1])
    o_list, l_list = [], []
    for qkv, bias in zip((qkv1, qkv4, qkv16), bias_tiles):
        o, lse = _attn_call(qkv, bias)
        o_list.append(o)
        l_list.append(lse)
    hf, hb = _mlstm_branch(r3(cqk), r3(cv), gates_t, ml_conv, ml_fbias)
    yd = _pool_call(r3(dx), _pack_pool(pool_w), pool_scale[None])
    x1, x1p, aff_t = _merge_call(
        x, r3(ya), o_list, l_list, hf, hb, r3(co), yd,
        w_in[:, n_small:].astype(BF16), b_in[None, n_small:], w_branch.astype(BF16), w_out.astype(BF16),
        ml_norm_g[:, None], ln1_g[None], ln1_b[None], jnp.transpose(w_router).astype(BF16), alpha)
    ffn = _expert_choice_ffn(x1p, aff_t, w_e1, w_e3, w_e2, layer)
    return x1.reshape(b * s, d), ffn.reshape(b * s, d)


def kernel(x, w_in, b_in, gm_ln_g, gm_ws, gm_bs, rel_bias, ml_conv, ml_fbias, ml_norm_g, pool_w,
           pool_scale, w_branch, w_out, ln1_g, ln1_b, w_router, w_e1, w_e3, w_e2, ln2_g, ln2_b):
    depth = w_in.shape[0]
    alpha = (2 * depth) ** 0.25
    bias_tiles = [_attn_bias_tile(rel_bias, window, dil) for window, dil in DIL_PATTERNS]
    b, s, d = x.shape
    pending = None
    for l in range(depth):
        x1, ffn = _layer(l, x, pending, alpha, bias_tiles, w_in[l], b_in[l], gm_ln_g[l], gm_ws[l], gm_bs[l],
                         ml_conv[l], ml_fbias[l], ml_norm_g[l], pool_w[l], pool_scale[l], w_branch[l],
                         w_out[l], ln1_g[l], ln1_b[l], w_router[l], w_e1, w_e3, w_e2)
        pending = (x1, ffn, ln2_g[l][None], ln2_b[l][None])
    return _resln_call(*pending, alpha).reshape(b, s, d)
```

```python
import functools
import math

import jax
import jax.numpy as jnp
import numpy as np
from jax import lax
from jax.experimental import pallas as pl
from jax.experimental.pallas import tpu as pltpu
from jax.experimental.pallas import tpu_sc as plsc

F32 = jnp.float32
BF16 = jnp.bfloat16

D_MODEL = 1024
MIX_W = 256
N_BRANCH = 4
GM_CHUNK = 128
GM_GROUPS = 4
ATT_HEADS = 4
ATT_HD = 64
DIL_PATTERNS = ((128, 1), (512, 4), (2048, 16))
ATT_BLOCK = 64
REL_BUCKETS = 32
REL_MAX_DIST = 1024
ML_HEADS = 4
ML_HD = 64
ML_CHUNK = 64
POOL_WINDOWS = (2, 4, 8, 16)
N_EXPERTS = 16
EXPERT_FF = 1024
EC_FACTOR = 2
LN_EPS = 1e-5
NEG_BIG = -1e30

V7X_VMEM_LIMIT = 56 * 1024 * 1024
LANES = 128
HALO = 8


def _cparams(n_grid, vmem=V7X_VMEM_LIMIT):
    return pltpu.CompilerParams(dimension_semantics=("arbitrary",) * n_grid,
                                vmem_limit_bytes=vmem)


def _pack_bf16_pair(lo, hi):
    lo_bits = lax.shift_right_logical(pltpu.bitcast(lo.astype(F32), jnp.int32), 16)
    return pltpu.bitcast(hi.astype(F32), jnp.int32) | lo_bits


def _unpack_bf16_pair(packed):
    lo = pltpu.bitcast(lax.shift_left(packed, 16), F32).astype(BF16)
    hi = pltpu.bitcast(packed & jnp.int32(-65536), F32).astype(BF16)
    return lo, hi


def _standardize(xf):
    mu = jnp.mean(xf, axis=-1, keepdims=True)
    var = jnp.mean(jnp.square(xf - mu), axis=-1, keepdims=True)
    return (xf - mu) * lax.rsqrt(var + LN_EPS)


TA = 512
A_COLS = 2560 + LANES


def _inproj_kernel(*refs, alpha):
    if alpha is None:
        x_ref, *refs = refs
        x = x_ref[...]
    else:
        x1_ref, y_ref, g2_ref, b2_ref, *refs = refs
        x = _standardize(alpha * x1_ref[...] + y_ref[...]) * g2_ref[...] + b2_ref[...]
        refs[-2][...] = x
        refs = refs[:-2] + refs[-1:]
    (w_ref, b_ref, lng_ref, wscat_ref, bsfull_ref,
     ya_ref, qkv1_ref, qkv4_ref, qkv16_ref, cqk_ref, cv_ref, co_ref, dx_ref, gt_ref, qkv_scr) = refs
    xb = x.astype(BF16)
    h = jnp.dot(xb, w_ref[...], preferred_element_type=F32) + b_ref[...]
    qkv1_ref[0, 0] = h[:, 512:1280].astype(BF16)
    for c in range(768 // LANES):
        qkv_scr[c] = h[:, 512 + c * LANES:512 + (c + 1) * LANES]
    for (_, dil), out_ref in zip(DIL_PATTERNS[1:], (qkv4_ref, qkv16_ref)):
        for r in range(dil):
            for c in range(768 // LANES):
                out_ref[0, r, :, c * LANES:(c + 1) * LANES] = (
                    qkv_scr[c, pl.ds(r, TA // dil, stride=dil), :].astype(BF16))
    cqk_ref[...] = h[:, 1280:1792]
    cv_ref[...] = h[:, 1792:2048].astype(BF16)
    co_ref[...] = h[:, 2048:2304]
    dx_ref[...] = h[:, 2304:2560]
    gates_t = jnp.transpose(h[:, 2560:2688])
    for j in range(TA // LANES):
        gt_ref[j] = gates_t[0:4 * ML_HEADS, j * LANES:(j + 1) * LANES]
    u = jax.nn.gelu(h[:, 0:256])
    v = jax.nn.gelu(h[:, 256:512])
    vn = _standardize(v) * lng_ref[...]
    lane_grp = lax.broadcasted_iota(jnp.int32, (GM_CHUNK, MIX_W), 1) // (MIX_W // GM_GROUPS)
    for c in range(TA // GM_CHUNK):
        vc = vn[c * GM_CHUNK:(c + 1) * GM_CHUNK]
        stacked = jnp.concatenate(
            [jnp.where(lane_grp == g, vc, 0.0).astype(BF16) for g in range(GM_GROUPS)], axis=0)
        mixed = jnp.dot(wscat_ref[...], stacked, preferred_element_type=F32) + bsfull_ref[...]
        ya_ref[c * GM_CHUNK:(c + 1) * GM_CHUNK, :] = (
            u[c * GM_CHUNK:(c + 1) * GM_CHUNK] * mixed).astype(BF16)


def _inproj_call(x_in, w_cat, b_cat, lng, wscat, bsfull, batch, alpha=None):
    fused = alpha is not None
    n = (x_in[0] if fused else x_in).shape[0]
    seq = n // batch
    tpb = seq // TA
    tok = lambda w: pl.BlockSpec((TA, w), lambda i: (i, 0))
    const = lambda s: pl.BlockSpec(s, lambda i: (0,) * len(s))
    regrouped = lambda dil: pl.BlockSpec((1, dil, TA // dil, 768), lambda i: (i // tpb, 0, i % tpb, 0))
    out_shape = (
        jax.ShapeDtypeStruct((n, 256), BF16),
    ) + tuple(jax.ShapeDtypeStruct((batch, dil, seq // dil, 768), BF16)
              for _, dil in DIL_PATTERNS) + (
        jax.ShapeDtypeStruct((n, 512), F32),
        jax.ShapeDtypeStruct((n, 256), BF16),
        jax.ShapeDtypeStruct((n, 256), F32),
        jax.ShapeDtypeStruct((n, 256), F32),
        jax.ShapeDtypeStruct((n // LANES, 4 * ML_HEADS, LANES), F32),
    )
    x_specs = [tok(D_MODEL), tok(D_MODEL), const((1, D_MODEL)), const((1, D_MODEL))] if fused else [tok(D_MODEL)]
    out_specs = ((tok(256),) + tuple(regrouped(dil) for _, dil in DIL_PATTERNS)
                 + (tok(512), tok(256), tok(256), tok(256),
                    pl.BlockSpec((TA // LANES, 4 * ML_HEADS, LANES), lambda i: (i, 0, 0))))
    if fused:
        out_specs += (tok(D_MODEL),)
        out_shape += (jax.ShapeDtypeStruct((n, D_MODEL), F32),)
    return pl.pallas_call(
        functools.partial(_inproj_kernel, alpha=alpha), name="inproj_gmlp",
        grid=(n // TA,),
        in_specs=x_specs + [const((D_MODEL, A_COLS)), const((1, A_COLS)), const((1, MIX_W)),
                            const((GM_CHUNK, GM_GROUPS * GM_CHUNK)), const((GM_CHUNK, MIX_W))],
        out_specs=out_specs,
        out_shape=out_shape,
        scratch_shapes=[pltpu.VMEM((768 // LANES, TA, LANES), F32)],
        compiler_params=_cparams(1),
    )(*(x_in if fused else (x_in,)), w_cat, b_cat, lng, wscat, bsfull)


def _pack_inproj_weights(w_in, b_in):
    pad = lambda a: jnp.pad(a, ((0, 0), (0, LANES - 4 * ML_HEADS)))
    w_cat = jnp.concatenate([w_in[:, 0:2304], w_in[:, 2320:2576], pad(w_in[:, 2304:2320])], axis=1)
    b2 = b_in[None, :]
    b_cat = jnp.concatenate([b2[:, 0:2304], b2[:, 2320:2576], pad(b2[:, 2304:2320])], axis=1)
    return w_cat.astype(BF16), b_cat


def _pack_gmlp(gm_ws, gm_bs):
    wscat = jnp.transpose(gm_ws, (1, 0, 2)).reshape(GM_CHUNK, GM_GROUPS * GM_CHUNK).astype(BF16)
    bsfull = jnp.repeat(jnp.transpose(gm_bs), MIX_W // GM_GROUPS, axis=1)
    return wscat, bsfull


def _halo_specs(t, width, n_tiles):
    r = t // HALO
    main = pl.BlockSpec((1, t, width), lambda b, i: (b, i, 0))
    prev = pl.BlockSpec((1, HALO, width), lambda b, i: (b, jnp.maximum(i * r - 1, 0), 0))
    nxt = pl.BlockSpec((1, HALO, width), lambda b, i: (b, jnp.minimum((i + 1) * r, n_tiles * r - 1), 0))
    return main, prev, nxt


def _fill_halo_scratch(buf, x_ref, p_ref, n_ref, t):
    i = pl.program_id(1)
    last = pl.num_programs(1) - 1
    buf[0:HALO, :] = jnp.where(i > 0, p_ref[0], 0.0)
    buf[HALO:HALO + t, :] = x_ref[0]
    buf[HALO + t:2 * HALO + t, :] = jnp.where(i < last, n_ref[0], 0.0)


TP = 1024


def _pool_kernel(x_ref, p_ref, n_ref, w_ref, sc_ref, o_ref, buf, lvl):
    _fill_halo_scratch(buf, x_ref, p_ref, n_ref, TP)
    seq = pl.num_programs(1) * TP
    pos = pl.program_id(1) * TP + lax.broadcasted_iota(jnp.int32, (TP, 1), 0)
    lane_grp = lax.broadcasted_iota(jnp.int32, (TP, MIX_W), 1) // (MIX_W // len(POOL_WINDOWS))
    x0 = buf[HALO:HALO + TP, :]
    sums = []
    src, rows = buf, TP + 2 * HALO
    for k, win in enumerate(POOL_WINDOWS):
        half = win // 2
        rows -= half
        cur = src[0:rows, :] + src[half:rows + half, :] if k else buf[0:rows, :] + buf[1:rows + 1, :]
        if k + 1 < len(POOL_WINDOWS):
            lvl[k, 0:rows, :] = cur
            sums.append(lvl[k, HALO - half:HALO - half + TP, :])
            src = lvl.at[k]
        else:
            sums.append(cur[0:TP])
    pooled = jnp.zeros((TP, MIX_W), F32)
    for gi, win in enumerate(POOL_WINDOWS):
        half = win // 2
        cnt = (jnp.minimum(pos + half, seq) - jnp.maximum(pos - half, 0)).astype(F32)
        pooled = jnp.where(lane_grp == gi, sums[gi] / cnt - x0, pooled)
    mixed = jnp.dot(pooled.astype(BF16), w_ref[...], preferred_element_type=F32)
    o_ref[0] = (mixed * sc_ref[...]).astype(BF16)


def _pool_call(dx, w_block, scale):
    b, s, _ = dx.shape
    nt = s // TP
    main, prev, nxt = _halo_specs(TP, MIX_W, nt)
    return pl.pallas_call(
        _pool_kernel, name="pool_mixer",
        grid=(b, nt),
        in_specs=[main, prev, nxt,
                  pl.BlockSpec((MIX_W, MIX_W), lambda b, i: (0, 0)),
                  pl.BlockSpec((1, MIX_W), lambda b, i: (0, 0))],
        out_specs=pl.BlockSpec((1, TP, MIX_W), lambda b, i: (b, i, 0)),
        out_shape=jax.ShapeDtypeStruct((b, s, MIX_W), BF16),
        scratch_shapes=[pltpu.VMEM((TP + 2 * HALO, MIX_W), F32),
                        pltpu.VMEM((len(POOL_WINDOWS) - 1, TP + 2 * HALO, MIX_W), F32)],
        compiler_params=_cparams(2),
    )(dx, dx, dx, w_block, scale)


TQ = 128
TQS = 1024
TKEYS = TQ + 2 * ATT_BLOCK


def _attn_kernel(q_ref, kp_ref, km_ref, kn_ref, vp_ref, vm_ref, vn_ref, bias_ref, o_ref, lse_ref):
    i = pl.program_id(2)
    q = q_ref[0, 0] * ATT_HD ** -0.5
    k = jnp.concatenate([kp_ref[0, 0], km_ref[0, 0], kn_ref[0, 0]], axis=0)
    v = jnp.concatenate([vp_ref[0, 0], vm_ref[0, 0], vn_ref[0, 0]], axis=0)
    lane = lax.broadcasted_iota(jnp.int32, (TQ, LANES), 1)
    lane_half = lax.broadcasted_iota(jnp.int32, (1, LANES), 1) // ATT_HD
    keep = [jnp.where(lane_half == hh, 1.0, 0.0).astype(BF16) for hh in range(2)]
    n_sub = q_ref.shape[2] // TQ
    last_step = pl.num_programs(2) - 1
    for j in range(n_sub):
        if j == 0:
            variant = jnp.where(i == 0, 0, 1)
        elif j == n_sub - 1:
            variant = jnp.where(i == last_step, 2, 1)
        else:
            variant = 1
        qrows = slice(j * TQ, (j + 1) * TQ)
        krows = slice(j * TQ, j * TQ + TKEYS)
        lse_tile = jnp.zeros((TQ, LANES), F32)
        for pair in range(ATT_HEADS // 2):
            grp = slice(pair * LANES, (pair + 1) * LANES)
            q_pair, k_pair, v_pair = q[qrows, grp], k[krows, grp], v[krows, grp]
            o_pair = jnp.zeros((TQ, LANES), F32)
            for hh in range(2):
                h = 2 * pair + hh
                logits = lax.dot_general(q_pair * keep[hh], k_pair, (((1,), (1,)), ((), ())),
                                         preferred_element_type=F32) + bias_ref[variant, h]
                m = jnp.max(logits, axis=-1, keepdims=True)
                p = jnp.exp(logits - m)
                ssum = jnp.sum(p, axis=-1, keepdims=True)
                o = jnp.dot(p.astype(BF16), v_pair, preferred_element_type=F32) / ssum
                o_pair = jnp.where(lane_half == hh, o, o_pair)
                lse_tile = jnp.where(lane == h, m + jnp.log(ssum), lse_tile)
            o_ref[0, 0, qrows, grp] = o_pair
        lse_ref[0, 0, qrows, :] = lse_tile


def _attn_call(qkv, bias):
    b, dil, l, _ = qkv.shape
    tqs = min(l, TQS)
    nt = l // tqs
    r64 = tqs // ATT_BLOCK
    main = lambda c: pl.BlockSpec((1, 1, tqs, MIX_W), lambda b, r, i: (b, r, i, c))
    prev = lambda c: pl.BlockSpec((1, 1, ATT_BLOCK, MIX_W),
                                  lambda b, r, i: (b, r, jnp.maximum(i * r64 - 1, 0), c))
    nxt = lambda c: pl.BlockSpec((1, 1, ATT_BLOCK, MIX_W),
                                 lambda b, r, i: (b, r, jnp.minimum((i + 1) * r64, nt * r64 - 1), c))
    return pl.pallas_call(
        _attn_kernel, name="band_attention",
        grid=(b, dil, nt),
        in_specs=[main(0), prev(1), main(1), nxt(1), prev(2), main(2), nxt(2),
                  pl.BlockSpec((3, ATT_HEADS, TQ, TKEYS), lambda b, r, i: (0, 0, 0, 0))],
        out_specs=(pl.BlockSpec((1, 1, tqs, MIX_W), lambda b, r, i: (b, r, i, 0)),
                   pl.BlockSpec((1, 1, tqs, LANES), lambda b, r, i: (b, r, i, 0))),
        out_shape=(jax.ShapeDtypeStruct((b, dil, l, MIX_W), F32),
                   jax.ShapeDtypeStruct((b, dil, l, LANES), F32)),
        compiler_params=_cparams(3),
    )(qkv, qkv, qkv, qkv, qkv, qkv, qkv, bias)


def _t5_bucket_static(rel):
    half = REL_BUCKETS // 2
    max_exact = half // 2
    ret = np.where(rel > 0, half, 0)
    n = np.abs(rel)
    nf = np.maximum(n, 1).astype(np.float32)
    large = max_exact + (np.log(nf / np.float32(max_exact)) / np.float32(math.log(REL_MAX_DIST / max_exact))
                         * np.float32(half - max_exact)).astype(np.int32)
    large = np.minimum(large, half - 1)
    return ret + np.where(n < max_exact, n, large)


def _attn_bias_tile(rel_bias, window, dil):
    side = (window // 2) // dil
    rel = np.arange(TKEYS)[None, :] - ATT_BLOCK - np.arange(TQ)[:, None]
    n_rel = TKEYS + TQ - 1
    rel_values = np.arange(n_rel) - (ATT_BLOCK + TQ - 1)
    onehot = jax.nn.one_hot(jnp.asarray(_t5_bucket_static(dil * rel_values), jnp.int32), REL_BUCKETS, dtype=F32)
    table = jnp.einsum('nr,rh->hn', onehot, rel_bias, precision=lax.Precision.HIGHEST)
    periodic = jnp.tile(jnp.pad(table, ((0, 0), (0, 1))), (1, TQ))[:, :TQ * n_rel]
    bias = periodic.reshape(ATT_HEADS, TQ, n_rel)[:, :, TQ - 1:]
    key = np.arange(TKEYS)[None, :]
    inside = np.abs(rel) <= side
    masks = np.stack([inside & (key >= ATT_BLOCK), inside, inside & (key < ATT_BLOCK + TQ)])
    return jnp.where(jnp.asarray(masks)[:, None], bias[None], NEG_BIG)


TM = 1024
VT_ROWS = ML_HD + 16


def _mlprep_kernel(x_ref, p_ref, n_ref, v_ref, w_ref, qt_out, k_out, vt_out, buf):
    _fill_halo_scratch(buf, x_ref, p_ref, n_ref, TM)
    conv = (buf[HALO - 1:HALO - 1 + TM, :] * w_ref[0:1, :] + buf[HALO:HALO + TM, :] * w_ref[1:2, :]
            + buf[HALO + 1:HALO + 1 + TM, :] * w_ref[2:3, :])
    qk = jax.nn.silu(conv)
    qt = jnp.transpose(qk[:, :MIX_W])
    vt = jnp.transpose(v_ref[0].astype(F32))
    ones_rows = jnp.where(lax.broadcasted_iota(jnp.int32, (VT_ROWS - ML_HD, LANES), 0) == 0, 1.0, 0.0)
    for h in range(ML_HEADS):
        sl = slice(h * ML_HD, (h + 1) * ML_HD)
        k_out[0, h] = (qk[:, MIX_W + h * ML_HD:MIX_W + (h + 1) * ML_HD] * ML_HD ** -0.5).astype(BF16)
        for p in range(TM // LANES):
            pl_ = slice(p * LANES, (p + 1) * LANES)
            qt_out[0, h, p] = qt[sl, pl_].astype(BF16)
            vt_out[0, h, p] = jnp.concatenate([vt[sl, pl_], ones_rows], axis=0).astype(BF16)


def _mlprep_call(cqk, cv, conv_w):
    b, s, _ = cqk.shape
    nt = s // TM
    n_pairs = s // LANES
    ppt = TM // LANES
    main, prev, nxt = _halo_specs(TM, 2 * MIX_W, nt)
    return pl.pallas_call(
        _mlprep_kernel, name="mlstm_prep",
        grid=(b, nt),
        in_specs=[main, prev, nxt,
                  pl.BlockSpec((1, TM, MIX_W), lambda b, i: (b, i, 0)),
                  pl.BlockSpec((3, 2 * MIX_W), lambda b, i: (0, 0))],
        out_specs=(pl.BlockSpec((1, ML_HEADS, ppt, ML_HD, LANES), lambda b, i: (b, 0, i, 0, 0)),
                   pl.BlockSpec((1, ML_HEADS, TM, ML_HD), lambda b, i: (b, 0, i, 0)),
                   pl.BlockSpec((1, ML_HEADS, ppt, VT_ROWS, LANES), lambda b, i: (b, 0, i, 0, 0))),
        out_shape=(jax.ShapeDtypeStruct((b, ML_HEADS, n_pairs, ML_HD, LANES), BF16),
                   jax.ShapeDtypeStruct((b, ML_HEADS, s, ML_HD), BF16),
                   jax.ShapeDtypeStruct((b, ML_HEADS, n_pairs, VT_ROWS, LANES), BF16)),
        scratch_shapes=[pltpu.VMEM((TM + 2 * HALO, 2 * MIX_W), F32)],
        compiler_params=_cparams(2),
    )(cqk, cqk, cqk, cv, conv_w)


def _gate_kernel(g_ref, fb_ref, a_ref, m_ref, iw_ref, en_ref, ws_ref, dec_ref):
    x = g_ref[...]
    n_pairs = x.shape[0]
    n_ch = 2 * ML_HEADS
    lane = lax.broadcasted_iota(jnp.int32, (1, 1, LANES), 2)
    t_in = lane % ML_CHUNK
    second = lane >= ML_CHUNK
    fwd_row = lax.broadcasted_iota(jnp.int32, (1, n_ch, 1), 1) < ML_HEADS
    li = x[:, 0:n_ch, :]
    z = x[:, n_ch:2 * n_ch, :] + fb_ref[...]
    lf = jnp.minimum(z, 0.0) - jnp.log1p(jnp.exp(-jnp.abs(z)))

    def within_chunk(v, op, ident, prefix):
        s = 1
        while s < ML_CHUNK:
            nb = pltpu.roll(v, s if prefix else LANES - s, 2)
            ok = (t_in >= s) if prefix else (t_in < ML_CHUNK - s)
            v = op(v, jnp.where(ok, nb, ident))
            s *= 2
        return v

    pre = within_chunk(lf, jnp.add, 0.0, True)
    suf = within_chunk(lf, jnp.add, 0.0, False)
    g = pre + suf - lf
    b = jnp.where(fwd_row, pre, suf)
    a = li - b
    cm_pre = within_chunk(a, jnp.maximum, -jnp.inf, True)
    cm_suf = within_chunk(a, jnp.maximum, -jnp.inf, False)
    cm = jnp.where(fwd_row, cm_pre, cm_suf)
    amax = jnp.maximum(cm_pre, cm_suf)

    def shift_pairs(v, k, fill):
        pad = jnp.full((abs(k),) + v.shape[1:], fill, F32)
        return (jnp.concatenate([pad, v[:n_pairs - k]], axis=0) if k > 0
                else jnp.concatenate([v[-k:], pad], axis=0))

    def from_chunk(v, dist, fill, forward):
        if dist == 1:
            y = pltpu.roll(v, ML_CHUNK, 2)
            if forward:
                return jnp.where(second, y, shift_pairs(y, 1, fill))
            return jnp.where(second, shift_pairs(y, -1, fill), y)
        return shift_pairs(v, dist // 2 if forward else -(dist // 2), fill)

    def running_stabiliser(forward):
        big_g, big_a = g, amax + g
        dist = 1
        while dist < 2 * n_pairs:
            gp = from_chunk(big_g, dist, 0.0, forward)
            ap = from_chunk(big_a, dist, -jnp.inf, forward)
            big_g, big_a = gp + big_g, jnp.maximum(ap + big_g, big_a)
            dist *= 2
        ge = from_chunk(big_g, 1, 0.0, forward)
        ae = from_chunk(big_a, 1, -jnp.inf, forward)
        return jnp.maximum(ge, ae)

    m_chunk = jnp.where(fwd_row, running_stabiliser(True), running_stabiliser(False))
    m_t = jnp.maximum(cm, m_chunk)
    m_last = jnp.maximum(amax, m_chunk)
    a_ref[0] = a
    m_ref[0] = m_t
    iw_ref[0] = jnp.exp(m_chunk - m_t)
    en_ref[0] = jnp.exp(-(b + m_t))
    ws_ref[0] = jnp.exp(a - m_last)
    dec_ref[0] = jnp.exp(m_chunk - m_last)


def _gate_call(gates_t, fbias_col, batch):
    n_pairs = gates_t.shape[0] // batch
    n_ch = 2 * ML_HEADS
    out = pl.BlockSpec((1, n_pairs, n_ch, LANES), lambda i: (i, 0, 0, 0))
    return pl.pallas_call(
        _gate_kernel, name="mlstm_gates",
        grid=(batch,),
        in_specs=[pl.BlockSpec((n_pairs, 2 * n_ch, LANES), lambda i: (i, 0, 0)),
                  pl.BlockSpec((n_ch, 1), lambda i: (0, 0))],
        out_specs=(out,) * 6,
        out_shape=(jax.ShapeDtypeStruct((batch, n_pairs, n_ch, LANES), F32),) * 6,
        compiler_params=_cparams(1),
    )(gates_t, fbias_col)


TE = 1024


def _mlstm_kernel(*refs):
    fwd, bwd, (hf_ref, hb_ref, state) = refs[:9], refs[9:18], refs[18:]
    i = pl.program_id(1)

    @pl.when(i == 0)
    def _():
        state[...] = jnp.zeros(state.shape, F32)

    n_pairs = TE // LANES
    s_idx = lax.broadcasted_iota(jnp.int32, (LANES, LANES), 0)
    t_idx = lax.broadcasted_iota(jnp.int32, (LANES, LANES), 1)
    same_chunk = (s_idx >= ML_CHUNK) == (t_idx >= ML_CHUNK)
    upper_lanes = lax.broadcasted_iota(jnp.int32, (1, LANES), 1) >= ML_CHUNK

    def pair_body(p, carry):
        jobs = []
        for d, ((qt_r, k_r, vt_r, a_r, m_r, iw_r, en_r, ws_r, dec_r), out_r) in enumerate(
                ((fwd, hf_ref), (bwd, hb_ref))):
            pp = p if d == 0 else n_pairs - 1 - p
            srows = pl.ds(pl.multiple_of(pp * LANES, LANES), LANES)
            a_t = jnp.transpose(jnp.concatenate(
                [a_r[0, pp], jnp.zeros((LANES - 2 * ML_HEADS, LANES), F32)], axis=0))
            m_t, iw_t, en_t, ws_t, dec_t = [r[0, pp] for r in (m_r, iw_r, en_r, ws_r, dec_r)]
            dec_lo, dec_hi = dec_t[:, :ML_CHUNK], pltpu.roll(dec_t, ML_CHUNK, 1)[:, :ML_CHUNK]
            in_first = upper_lanes if d else ~upper_lanes
            tri = same_chunk & ((s_idx >= t_idx) if d else (s_idx <= t_idx))
            for h in range(ML_HEADS):
                ch = d * ML_HEADS + h
                row = lambda t: t[ch:ch + 1]
                jobs.append(dict(
                    ch=ch, tri=tri, in_first=in_first, k=k_r[0, h, srows, :], qt=qt_r[0, h, pp],
                    vt=vt_r[0, h, pp], a=a_t[:, ch:ch + 1], m=row(m_t), iw=row(iw_t), en=row(en_t), ws=row(ws_t),
                    dec_first=row(dec_hi if d else dec_lo), dec_second=row(dec_lo if d else dec_hi),
                    out=(out_r, pp, h)))
        for j in jobs:
            vt_f = j["vt"].astype(F32)
            j["c0"] = state[j["ch"]]
            j["st"] = jnp.dot(j["k"], j["qt"], preferred_element_type=F32)
            j["inter1"] = jnp.dot(j["c0"].astype(BF16), j["qt"], preferred_element_type=F32)
            j["upd1"] = jnp.dot((vt_f * jnp.where(j["in_first"], j["ws"], 0.0)).astype(BF16), j["k"],
                                preferred_element_type=F32)
            j["upd2"] = jnp.dot((vt_f * jnp.where(j["in_first"], 0.0, j["ws"])).astype(BF16), j["k"],
                                preferred_element_type=F32)
        for j in jobs:
            j["swt"] = j["st"] * jnp.exp(jnp.where(j["tri"], j["a"] - j["m"], NEG_BIG))
            j["intra"] = jnp.dot(j["vt"], j["swt"].astype(BF16), preferred_element_type=F32)
            j["c1"] = j["dec_first"] * j["c0"] + j["upd1"]
            j["inter2"] = jnp.dot(j["c1"].astype(BF16), j["qt"], preferred_element_type=F32)
        for j in jobs:
            inter = jnp.where(j["in_first"], j["inter1"], j["inter2"])
            den = jnp.sum(j["swt"], axis=0, keepdims=True) + j["iw"] * inter[ML_HD:ML_HD + 1]
            tot = j["intra"][:ML_HD] + j["iw"] * inter[:ML_HD]
            out_r, pp, h = j["out"]
            out_r[0, pp, h * ML_HD:(h + 1) * ML_HD, :] = tot / jnp.maximum(jnp.abs(den), j["en"])
            state[j["ch"]] = j["dec_second"] * j["c1"] + j["upd2"]
        return carry

    lax.fori_loop(0, n_pairs, pair_body, 0)


def _mlstm_call(qt, k, vt, a_t, m_t, iw_t, en_t, ws_t, dec_t):
    b, _, s, _ = k.shape
    nt = s // TE
    ppt = TE // LANES

    def specs(rev):
        ti = (lambda i: nt - 1 - i) if rev else (lambda i: i)
        tile = pl.BlockSpec((1, ppt, 2 * ML_HEADS, LANES), lambda b, i: (b, ti(i), 0, 0))
        return [
            pl.BlockSpec((1, ML_HEADS, ppt, ML_HD, LANES), lambda b, i: (b, 0, ti(i), 0, 0)),
            pl.BlockSpec((1, ML_HEADS, TE, ML_HD), lambda b, i: (b, 0, ti(i), 0)),
            pl.BlockSpec((1, ML_HEADS, ppt, VT_ROWS, LANES), lambda b, i: (b, 0, ti(i), 0, 0)),
            tile, tile, tile, tile, tile, tile]

    args = [qt, k, vt, a_t, m_t, iw_t, en_t, ws_t, dec_t]
    out_f = pl.BlockSpec((1, ppt, MIX_W, LANES), lambda b, i: (b, i, 0, 0))
    out_b = pl.BlockSpec((1, ppt, MIX_W, LANES), lambda b, i: (b, nt - 1 - i, 0, 0))
    return pl.pallas_call(
        _mlstm_kernel, name="mlstm_scan",
        grid=(b, nt),
        in_specs=specs(False) + specs(True),
        out_specs=(out_f, out_b),
        out_shape=(jax.ShapeDtypeStruct((b, s // LANES, MIX_W, LANES), F32),) * 2,
        scratch_shapes=[pltpu.VMEM((2 * ML_HEADS, VT_ROWS, ML_HD), F32)],
        compiler_params=_cparams(2),
    )(*args, *args)


def _mlstm_branch(cqk, cv, gates_t, conv_w, fbias):
    qt, k, vt = _mlprep_call(cqk, cv, conv_w)
    factors = _gate_call(gates_t, fbias.reshape(2 * ML_HEADS, 1), cqk.shape[0])
    return _mlstm_call(qt, k, vt, *factors)


TF = 512


def _merge_kernel(x_ref, ya_ref, o1_ref, o2_ref, o3_ref, l1_ref, l2_ref, l3_ref, hf_ref, hb_ref,
                  co_ref, yd_ref, wg_ref, bg_ref, wbr_ref, wout_ref, mng_ref, lng_ref, lnb_ref,
                  wr_ref, x1_ref, x1p_ref, aff_ref, o_scr, l_scr, *, alpha):
    x = x_ref[0]
    xb = x.astype(BF16)

    def natural_order(src_ref, scr):
        dil, width = src_ref.shape[1], src_ref.shape[3]
        if dil == 1:
            return src_ref[0, 0]
        for r in range(dil):
            for c in range(width // LANES):
                scr[c, pl.ds(r, TF // dil, stride=dil), :] = src_ref[0, r, :, c * LANES:(c + 1) * LANES]
        return jnp.concatenate([scr[c] for c in range(width // LANES)], axis=1)

    lane_head = lax.broadcasted_iota(jnp.int32, (TF, MIX_W), 1) // ML_HD
    l1, l2, l3 = [natural_order(r, l_scr.at[p]) for p, r in enumerate((l1_ref, l2_ref, l3_ref))]
    o1, o2, o3 = [natural_order(r, o_scr.at[p]) for p, r in enumerate((o1_ref, o2_ref, o3_ref))]
    lm = jnp.maximum(jnp.maximum(l1, l2), l3)
    e1, e2, e3 = jnp.exp(l1 - lm), jnp.exp(l2 - lm), jnp.exp(l3 - lm)
    inv = 1.0 / (e1 + e2 + e3)

    def per_head(w):
        out = jnp.zeros((TF, MIX_W), F32)
        for h in range(ATT_HEADS):
            out = jnp.where(lane_head == h, w[:, h:h + 1], out)
        return out

    y_b = per_head(e1 * inv) * o1 + per_head(e2 * inv) * o2 + per_head(e3 * inv) * o3
    hsum_t = jnp.concatenate([hf_ref[0, p] + hb_ref[0, p] for p in range(TF // LANES)], axis=1)
    per_head_rows = hsum_t.reshape(ML_HEADS, ML_HD, TF)
    mu = jnp.mean(per_head_rows, axis=1, keepdims=True)
    cen = per_head_rows - mu
    var = jnp.mean(cen * cen, axis=1, keepdims=True)
    hn_t = (cen * lax.rsqrt(var + LN_EPS)).reshape(MIX_W, TF)
    y_c_t = (jax.nn.sigmoid(jnp.transpose(co_ref[0])) * (hn_t * mng_ref[...])).astype(BF16)
    ys = (ya_ref[0], y_b.astype(BF16), None, yd_ref[0])
    merged = jnp.zeros((TF, D_MODEL), F32)
    for n in range(N_BRANCH):
        cols = slice(n * D_MODEL, (n + 1) * D_MODEL)
        gate = jax.nn.sigmoid(jnp.dot(xb, wg_ref[:, cols], preferred_element_type=F32) + bg_ref[:, cols])
        if ys[n] is None:
            proj = lax.dot_general(y_c_t, wbr_ref[n], (((0,), (0,)), ((), ())), preferred_element_type=F32)
        else:
            proj = jnp.dot(ys[n], wbr_ref[n], preferred_element_type=F32)
        merged = merged + gate * proj
    mix = jnp.dot(merged.astype(BF16), wout_ref[...], preferred_element_type=F32)
    x1 = _standardize(alpha * x + mix) * lng_ref[...] + lnb_ref[...]
    x1_ref[0] = x1
    x1b = x1.astype(BF16)
    x1p_ref[0] = _pack_bf16_pair(x1b[:, :D_MODEL // 2], x1b[:, D_MODEL // 2:])
    logits = lax.dot_general(wr_ref[...], x1b, (((1,), (1,)), ((), ())),
                             preferred_element_type=F32)
    ex = jnp.exp(logits - jnp.max(logits, axis=0, keepdims=True))
    aff_ref[0] = ex / jnp.sum(ex, axis=0, keepdims=True)


def _merge_call(x, ya, o_list, l_list, hf, hb, co, yd, wg, bg, wbr, wout, mng, lng, lnb, wr_t, alpha):
    b, s, _ = x.shape
    tok = lambda w: pl.BlockSpec((1, TF, w), lambda b, i: (b, i, 0))
    grouped = lambda dil, w: pl.BlockSpec((1, dil, TF // dil, w), lambda b, i: (b, 0, i, 0))
    chunked = pl.BlockSpec((1, TF // LANES, MIX_W, LANES), lambda b, i: (b, i, 0, 0))
    const = lambda shp: pl.BlockSpec(shp, lambda b, i: (0,) * len(shp))
    return pl.pallas_call(
        functools.partial(_merge_kernel, alpha=alpha), name="merge_ln_router",
        grid=(b, s // TF),
        in_specs=[tok(D_MODEL), tok(MIX_W)] + [grouped(dil, MIX_W) for _, dil in DIL_PATTERNS]
                 + [grouped(dil, LANES) for _, dil in DIL_PATTERNS]
                 + [chunked, chunked, tok(MIX_W), tok(MIX_W)]
                 + [const((D_MODEL, N_BRANCH * D_MODEL)), const((1, N_BRANCH * D_MODEL)),
                    const((N_BRANCH, MIX_W, D_MODEL)), const((D_MODEL, D_MODEL)), const((MIX_W, 1)),
                    const((1, D_MODEL)), const((1, D_MODEL)), const((N_EXPERTS, D_MODEL))],
        out_specs=(tok(D_MODEL), tok(D_MODEL // 2), pl.BlockSpec((1, N_EXPERTS, TF), lambda b, i: (b, 0, i))),
        out_shape=(jax.ShapeDtypeStruct((b, s, D_MODEL), F32),
                   jax.ShapeDtypeStruct((b, s, D_MODEL // 2), jnp.int32),
                   jax.ShapeDtypeStruct((b, N_EXPERTS, s), F32)),
        scratch_shapes=[pltpu.VMEM((len(DIL_PATTERNS), MIX_W // LANES, TF, LANES), F32),
                        pltpu.VMEM((len(DIL_PATTERNS), 1, TF, LANES), F32)],
        compiler_params=_cparams(2),
    )(x, ya, *o_list, *l_list, hf, hb, co, yd, wg, bg, wbr, wout, mng, lng, lnb, wr_t)


TT = 256


def _select_kernel(aff_ref, slot_ref, *, cap):
    s = aff_ref.shape[2]
    bits = pltpu.bitcast(aff_ref[0], jnp.int32)

    def bit_step(i, thr):
        cand = thr | jnp.left_shift(jnp.int32(1), 30 - i)
        cnt = jnp.sum((bits >= cand).astype(jnp.int32), axis=1, keepdims=True)
        return jnp.where(cnt >= cap, cand, thr)

    thr = lax.fori_loop(0, 31, bit_step, jnp.zeros((N_EXPERTS, 1), jnp.int32))
    gt = bits > thr
    eq = bits == thr
    need = (cap - jnp.sum(gt.astype(jnp.int32), axis=1, keepdims=True)).astype(F32)
    upper = (lax.broadcasted_iota(jnp.int32, (TT, TT), 0)
             <= lax.broadcasted_iota(jnp.int32, (TT, TT), 1)).astype(BF16)
    eq_before = jnp.zeros((N_EXPERTS, 1), F32)
    sel_before = jnp.zeros((N_EXPERTS, 1), F32)
    for j in range(s // TT):
        cols = slice(j * TT, (j + 1) * TT)
        eq_j = eq[:, cols]
        eq_incl = eq_before + jnp.dot(eq_j.astype(BF16), upper, preferred_element_type=F32)
        sel_j = gt[:, cols] | (eq_j & (eq_incl <= need))
        sel_f = sel_j.astype(F32)
        sel_incl = sel_before + jnp.dot(sel_f.astype(BF16), upper, preferred_element_type=F32)
        slot_ref[0, :, cols] = jnp.where(sel_j, sel_incl - 1.0, -1.0).astype(jnp.int32)
        eq_before = eq_incl[:, TT - 1:TT]
        sel_before = sel_incl[:, TT - 1:TT]


def _select_call(aff_t, cap):
    b, e, s = aff_t.shape
    return pl.pallas_call(
        functools.partial(_select_kernel, cap=cap), name="expert_choice_select",
        grid=(b,),
        in_specs=[pl.BlockSpec((1, e, s), lambda i: (i, 0, 0))],
        out_specs=pl.BlockSpec((1, e, s), lambda i: (i, 0, 0)),
        out_shape=jax.ShapeDtypeStruct((b, e, s), jnp.int32),
        compiler_params=_cparams(1),
    )(aff_t)


SC_LANES = 16
SC_ROWS = 64
SC_IDX = 128
SC_SLAB = 128
SC_ZROWS = 64
CF = 1024


def _sc_dispatch_call(x_flat, slot2, aff2, seq, cap):
    n_pairs = slot2.shape[0]
    d = x_flat.shape[1]
    info = plsc.get_sparse_core_info()
    n_workers = info.num_cores * info.num_subcores
    assert n_pairs % n_workers == 0 and seq % SC_LANES == 0 and cap % (2 * SC_ROWS) == 0
    pairs_per_worker = n_pairs // n_workers
    mesh = plsc.VectorSubcoreMesh(core_axis_name="c", subcore_axis_name="s")

    @functools.partial(
        pl.kernel, mesh=mesh, name="expert_dispatch_sc",
        compiler_params=pltpu.CompilerParams(needs_layout_passes=False),
        out_type=(jax.ShapeDtypeStruct((n_pairs * cap, d), x_flat.dtype),
                  jax.ShapeDtypeStruct((n_pairs, cap // SC_IDX, SC_IDX), jnp.int32),
                  jax.ShapeDtypeStruct((n_pairs, cap // SC_IDX, SC_IDX), F32)),
        scratch_types=[pltpu.VMEM((seq,), jnp.int32), pltpu.VMEM((seq,), F32),
                       pltpu.VMEM((cap,), jnp.int32), pltpu.VMEM((cap // SC_IDX, SC_IDX), jnp.int32),
                       pltpu.VMEM((cap // SC_IDX, SC_IDX), F32),
                       pltpu.VMEM((2, SC_ROWS, d), x_flat.dtype), pltpu.SemaphoreType.DMA((2,))])
    def dispatch(x_hbm, slot_hbm, aff_hbm, xs_hbm, tok_hbm, gate_hbm,
                 slot_v, aff_v, idx_v, tok_v, gate_v, rows_v, sem):
        worker = lax.axis_index("s") * info.num_cores + lax.axis_index("c")
        lane = lax.iota(jnp.int32, SC_LANES)

        def gather(c0, buf):
            return pltpu.make_async_copy(x_hbm.at[idx_v.at[pl.ds(c0, SC_ROWS)]], rows_v.at[buf], sem.at[buf])
        for k in range(pairs_per_worker):
            pair = worker * pairs_per_worker + k
            row0 = (pair // N_EXPERTS) * seq
            pltpu.sync_copy(slot_hbm.at[pair], slot_v)
            pltpu.sync_copy(aff_hbm.at[pair], aff_v)

            @plsc.parallel_loop(0, seq, step=SC_LANES, unroll=4)
            def _(t0):
                sv = slot_v[pl.ds(t0, SC_LANES)]
                picked = sv >= 0
                hi, lo = lax.shift_right_logical(sv, 7), sv & (SC_IDX - 1)
                plsc.store_scatter(tok_v, [hi, lo], t0 + lane, mask=picked)
                plsc.store_scatter(idx_v, [sv], row0 + t0 + lane, mask=picked)
                plsc.store_scatter(gate_v, [hi, lo], aff_v[pl.ds(t0, SC_LANES)], mask=picked)

            pltpu.sync_copy(tok_v, tok_hbm.at[pair])
            pltpu.sync_copy(gate_v, gate_hbm.at[pair])

            gather(0, 0).start()

            @pl.loop(0, cap, step=2 * SC_ROWS)
            def _(c0):
                gather(c0 + SC_ROWS, 1).start()
                gather(c0, 0).wait()
                pltpu.sync_copy(rows_v.at[0], xs_hbm.at[pl.ds(pair * cap + c0, SC_ROWS)])

                @pl.when(c0 + 2 * SC_ROWS < cap)
                def _():
                    gather(c0 + 2 * SC_ROWS, 0).start()

                gather(c0 + SC_ROWS, 1).wait()
                pltpu.sync_copy(rows_v.at[1], xs_hbm.at[pl.ds(pair * cap + c0 + SC_ROWS, SC_ROWS)])

    return dispatch(x_flat, slot2, aff2)


def _expert_kernel(xs_ref, g_ref, w1_ref, w3_ref, w2_ref, ye_ref, w1_bf, w3_bf, w2_bf):
    @pl.when((pl.program_id(1) == 0) & (pl.program_id(2) == 0))
    def _():
        w1_bf[...] = w1_ref[0, 0].astype(BF16)
        w3_bf[...] = w3_ref[0, 0].astype(BF16)
        w2_bf[...] = w2_ref[0, 0].astype(BF16)

    xs = jnp.concatenate(_unpack_bf16_pair(xs_ref[0, 0]), axis=1)
    hid = (jax.nn.silu(jnp.dot(xs, w1_bf[...], preferred_element_type=F32))
           * jnp.dot(xs, w3_bf[...], preferred_element_type=F32))
    g_rows = g_ref[0, 0]
    n_rows = g_rows.shape[0]
    g_t = jnp.transpose(jnp.concatenate([g_rows, jnp.zeros((LANES - n_rows, LANES), F32)], axis=0))
    g_col = jnp.concatenate([g_t[:, r:r + 1] for r in range(n_rows)], axis=0)
    ye_ref[0, 0] = jnp.dot(hid.astype(BF16), w2_bf[...], preferred_element_type=F32) * g_col


def _expert_call(xs4, gate4, w1, w3, w2, layer):
    b, e, cap, half = xs4.shape
    d, ff = w1.shape[2], w1.shape[3]
    assert d == 2 * half
    rows = lambda w: pl.BlockSpec((1, 1, CF, w), lambda e, b, j: (b, e, j, 0))
    wspec = lambda r, c: pl.BlockSpec((1, 1, r, c), lambda e, b, j: (layer, e, 0, 0))
    return pl.pallas_call(
        _expert_kernel, name="expert_ffn",
        grid=(e, b, cap // CF),
        in_specs=[rows(half), pl.BlockSpec((1, 1, CF // LANES, LANES), lambda e, b, j: (b, e, j, 0)),
                  wspec(d, ff), wspec(d, ff), wspec(ff, d)],
        out_specs=rows(d),
        out_shape=jax.ShapeDtypeStruct((b, e, cap, d), F32),
        scratch_shapes=[pltpu.VMEM((d, ff), BF16), pltpu.VMEM((d, ff), BF16), pltpu.VMEM((ff, d), BF16)],
        compiler_params=_cparams(3),
    )(xs4, gate4, w1, w3, w2)


def _sc_combine_call(ye_flat, tok3, seq):
    n_pairs, n_chunks, _ = tok3.shape
    cap = n_chunks * SC_IDX
    d = ye_flat.shape[1]
    nb = n_pairs // N_EXPERTS
    info = plsc.get_sparse_core_info()
    assert info.num_subcores == N_EXPERTS and nb % info.num_cores == 0 and n_chunks % 2 == 0
    assert seq % (info.num_subcores * SC_ZROWS) == 0 and d % SC_SLAB == 0
    batches_per_core = nb // info.num_cores
    own_rows = seq // info.num_subcores
    mesh = plsc.VectorSubcoreMesh(core_axis_name="c", subcore_axis_name="s")

    @functools.partial(
        pl.kernel, mesh=mesh, name="expert_combine_sc",
        compiler_params=pltpu.CompilerParams(needs_layout_passes=False),
        out_type=jax.ShapeDtypeStruct((nb * seq, d), F32),
        scratch_types=[pltpu.VMEM_SHARED((seq, SC_SLAB), F32),
                       pltpu.VMEM((n_chunks, SC_IDX), jnp.int32),
                       pltpu.VMEM((2, SC_IDX, SC_SLAB), F32),
                       pltpu.VMEM((SC_ZROWS, SC_SLAB), F32),
                       pltpu.SemaphoreType.DMA((2,))])
    def combine(ye_hbm, tok_hbm, out_hbm, acc_sh, tok_v, rows_v, zero_v, sem):
        core = lax.axis_index("c")
        sub = lax.axis_index("s")

        @pl.loop(0, SC_ZROWS)
        def _(r):
            for l0 in range(0, SC_SLAB, SC_LANES):
                zero_v[r, pl.ds(l0, SC_LANES)] = jnp.zeros((SC_LANES,), F32)

        for bb in range(batches_per_core):
            batch = core * batches_per_core + bb
            pair = batch * N_EXPERTS + sub
            pltpu.sync_copy(tok_hbm.at[pair], tok_v)

            @pl.loop(0, d // SC_SLAB)
            def _(slab):
                cols = pl.ds(pl.multiple_of(slab * SC_SLAB, SC_SLAB), SC_SLAB)

                @pl.loop(0, own_rows, step=SC_ZROWS)
                def _(r0):
                    pltpu.sync_copy(zero_v, acc_sh.at[pl.ds(sub * own_rows + r0, SC_ZROWS)])

                def load(j, buf):
                    return pltpu.make_async_copy(
                        ye_hbm.at[pl.ds(pair * cap + j * SC_IDX, SC_IDX), cols], rows_v.at[buf], sem.at[buf])

                load(0, 0).start()
                plsc.subcore_barrier()

                for j in range(0, n_chunks, 2):
                    load(j + 1, 1).start()
                    load(j, 0).wait()
                    pltpu.sync_copy(rows_v.at[0], acc_sh.at[tok_v.at[j]], add=True)
                    if j + 2 < n_chunks:
                        load(j + 2, 0).start()
                    load(j + 1, 1).wait()
                    pltpu.sync_copy(rows_v.at[1], acc_sh.at[tok_v.at[j + 1]], add=True)

                plsc.subcore_barrier()
                pltpu.sync_copy(acc_sh.at[pl.ds(sub * own_rows, own_rows)],
                                out_hbm.at[pl.ds(batch * seq + sub * own_rows, own_rows), cols])

    return combine(ye_flat, tok3)


TN = 1024


def _resln_kernel(x_ref, y_ref, g_ref, b_ref, o_ref, *, alpha):
    o_ref[...] = _standardize(alpha * x_ref[...] + y_ref[...]) * g_ref[...] + b_ref[...]


def _resln_call(x2d, y2d, g, bta, alpha):
    n, d = x2d.shape
    tok = pl.BlockSpec((TN, d), lambda i: (i, 0))
    vec = pl.BlockSpec((1, d), lambda i: (0, 0))
    return pl.pallas_call(
        functools.partial(_resln_kernel, alpha=alpha), name="residual_layernorm",
        grid=(n // TN,), in_specs=[tok, tok, vec, vec], out_specs=tok,
        out_shape=jax.ShapeDtypeStruct((n, d), F32),
        compiler_params=_cparams(1),
    )(x2d, y2d, g, bta)


def _expert_choice_ffn(x1p, aff_t, w1, w3, w2, layer):
    b, s, half = x1p.shape
    d = 2 * half
    cap = EC_FACTOR * s // N_EXPERTS
    slot = _select_call(aff_t, cap)
    xs, tok, gate = _sc_dispatch_call(x1p.reshape(b * s, half), slot.reshape(b * N_EXPERTS, s),
                                      aff_t.reshape(b * N_EXPERTS, s), s, cap)
    ye = _expert_call(xs.reshape(b, N_EXPERTS, cap, half), gate.reshape(b, N_EXPERTS, cap // SC_IDX, SC_IDX), w1, w3, w2,
                      layer)
    out = _sc_combine_call(ye.reshape(b * N_EXPERTS * cap, d),
                           tok, s)
    return out.reshape(b, s, d)


def _pack_pool(pool_w):
    g, gd, _ = pool_w.shape
    out = jnp.zeros((g * gd, g * gd), F32)
    for i in range(g):
        out = out.at[i * gd:(i + 1) * gd, i * gd:(i + 1) * gd].set(pool_w[i])
    return out.astype(BF16)


def _layer(layer, x, pending, alpha, bias_tiles, w_in, b_in, gm_ln_g, gm_ws, gm_bs, ml_conv, ml_fbias,
           ml_norm_g, pool_w, pool_scale, w_branch, w_out, ln1_g, ln1_b, w_router, w_e1, w_e3, w_e2):
    b, s, d = x.shape
    n_small = 2576
    w_cat, b_cat = _pack_inproj_weights(w_in, b_in)
    wscat, bsfull = _pack_gmlp(gm_ws, gm_bs)
    if pending is None:
        outs = _inproj_call(x.reshape(b * s, d), w_cat, b_cat, gm_ln_g[None], wscat, bsfull, b)
    else:
        *outs, x2 = _inproj_call(pending, w_cat, b_cat, gm_ln_g[None], wscat, bsfull, b, alpha)
        x = x2.reshape(b, s, d)
    ya, qkv1, qkv4, qkv16, cqk, cv, co, dx, gates_t = outs
    r3 = lambda t: t.reshape(b, s, t.shape[-1])
    o_list, l_list = [], []
    for qkv, bias in zip((qkv1, qkv4, qkv16), bias_tiles):
        o, lse = _attn_call(qkv, bias)
        o_list.append(o)
        l_list.append(lse)
    hf, hb = _mlstm_branch(r3(cqk), r3(cv), gates_t, ml_conv, ml_fbias)
    yd = _pool_call(r3(dx), _pack_pool(pool_w), pool_scale[None])
    x1, x1p, aff_t = _merge_call(
        x, r3(ya), o_list, l_list, hf, hb, r3(co), yd,
        w_in[:, n_small:].astype(BF16), b_in[None, n_small:], w_branch.astype(BF16), w_out.astype(BF16),
        ml_norm_g[:, None], ln1_g[None], ln1_b[None], jnp.transpose(w_router).astype(BF16), alpha)
    ffn = _expert_choice_ffn(x1p, aff_t, w_e1, w_e3, w_e2, layer)
    return x1.reshape(b * s, d), ffn.reshape(b * s, d)


def kernel(x, w_in, b_in, gm_ln_g, gm_ws, gm_bs, rel_bias, ml_conv, ml_fbias, ml_norm_g, pool_w,
           pool_scale, w_branch, w_out, ln1_g, ln1_b, w_router, w_e1, w_e3, w_e2, ln2_g, ln2_b):
    depth = w_in.shape[0]
    alpha = (2 * depth) ** 0.25
    bias_tiles = [_attn_bias_tile(rel_bias, window, dil) for window, dil in DIL_PATTERNS]
    b, s, d = x.shape
    pending = None
    for l in range(depth):
        x1, ffn = _layer(l, x, pending, alpha, bias_tiles, w_in[l], b_in[l], gm_ln_g[l], gm_ws[l], gm_bs[l],
                         ml_conv[l], ml_fbias[l], ml_norm_g[l], pool_w[l], pool_scale[l], w_branch[l],
                         w_out[l], ln1_g[l], ln1_b[l], w_router[l], w_e1, w_e3, w_e2)
        pending = (x1, ffn, ln2_g[l][None], ln2_b[l][None])
    return _resln_call(*pending, alpha).reshape(b, s, d)
```

```python
import functools
import math

import jax
import jax.numpy as jnp
import numpy as np
from jax import lax
from jax.experimental import pallas as pl
from jax.experimental.pallas import tpu as pltpu
from jax.experimental.pallas import tpu_sc as plsc

F32 = jnp.float32
BF16 = jnp.bfloat16

D_MODEL = 1024
MIX_W = 256
N_BRANCH = 4
GM_CHUNK = 128
GM_GROUPS = 4
ATT_HEADS = 4
ATT_HD = 64
DIL_PATTERNS = ((128, 1), (512, 4), (2048, 16))
ATT_BLOCK = 64
REL_BUCKETS = 32
REL_MAX_DIST = 1024
ML_HEADS = 4
ML_HD = 64
ML_CHUNK = 64
POOL_WINDOWS = (2, 4, 8, 16)
N_EXPERTS = 16
EXPERT_FF = 1024
EC_FACTOR = 2
LN_EPS = 1e-5
NEG_BIG = -1e30

V7X_VMEM_LIMIT = 56 * 1024 * 1024
LANES = 128
HALO = 8


def _cparams(n_grid, vmem=V7X_VMEM_LIMIT):
    return pltpu.CompilerParams(dimension_semantics=("arbitrary",) * n_grid,
                                vmem_limit_bytes=vmem)


def _pack_bf16_pair(lo, hi):
    lo_bits = lax.shift_right_logical(pltpu.bitcast(lo.astype(F32), jnp.int32), 16)
    return pltpu.bitcast(hi.astype(F32), jnp.int32) | lo_bits


def _unpack_bf16_pair(packed):
    lo = pltpu.bitcast(lax.shift_left(packed, 16), F32).astype(BF16)
    hi = pltpu.bitcast(packed & jnp.int32(-65536), F32).astype(BF16)
    return lo, hi


def _standardize(xf):
    mu = jnp.mean(xf, axis=-1, keepdims=True)
    var = jnp.mean(jnp.square(xf - mu), axis=-1, keepdims=True)
    return (xf - mu) * lax.rsqrt(var + LN_EPS)


TA = 512
A_COLS = 2560 + LANES


def _inproj_kernel(*refs, alpha):
    if alpha is None:
        x_ref, *refs = refs
        x = x_ref[...]
    else:
        x1_ref, y_ref, g2_ref, b2_ref, *refs = refs
        x = _standardize(alpha * x1_ref[...] + y_ref[...]) * g2_ref[...] + b2_ref[...]
        refs[-2][...] = x
        refs = refs[:-2] + refs[-1:]
    (w_ref, b_ref, lng_ref, wscat_ref, bsfull_ref,
     ya_ref, qkv1_ref, qkv4_ref, qkv16_ref, cqk_ref, cv_ref, co_ref, dx_ref, gt_ref, qkv_scr) = refs
    xb = x.astype(BF16)
    h = jnp.dot(xb, w_ref[...], preferred_element_type=F32) + b_ref[...]
    qkv1_ref[0, 0] = h[:, 512:1280].astype(BF16)
    for c in range(768 // LANES):
        qkv_scr[c] = h[:, 512 + c * LANES:512 + (c + 1) * LANES]
    for (_, dil), out_ref in zip(DIL_PATTERNS[1:], (qkv4_ref, qkv16_ref)):
        for r in range(dil):
            for c in range(768 // LANES):
                out_ref[0, r, :, c * LANES:(c + 1) * LANES] = (
                    qkv_scr[c, pl.ds(r, TA // dil, stride=dil), :].astype(BF16))
    cqk_ref[...] = h[:, 1280:1792]
    cv_ref[...] = h[:, 1792:2048].astype(BF16)
    co_ref[...] = h[:, 2048:2304]
    dx_ref[...] = h[:, 2304:2560]
    gates_t = jnp.transpose(h[:, 2560:2688])
    for j in range(TA // LANES):
        gt_ref[j] = gates_t[0:4 * ML_HEADS, j * LANES:(j + 1) * LANES]
    u = jax.nn.gelu(h[:, 0:256])
    v = jax.nn.gelu(h[:, 256:512])
    vn = _standardize(v) * lng_ref[...]
    lane_grp = lax.broadcasted_iota(jnp.int32, (GM_CHUNK, MIX_W), 1) // (MIX_W // GM_GROUPS)
    for c in range(TA // GM_CHUNK):
        vc = vn[c * GM_CHUNK:(c + 1) * GM_CHUNK]
        stacked = jnp.concatenate(
            [jnp.where(lane_grp == g, vc, 0.0).astype(BF16) for g in range(GM_GROUPS)], axis=0)
        mixed = jnp.dot(wscat_ref[...], stacked, preferred_element_type=F32) + bsfull_ref[...]
        ya_ref[c * GM_CHUNK:(c + 1) * GM_CHUNK, :] = (
            u[c * GM_CHUNK:(c + 1) * GM_CHUNK] * mixed).astype(BF16)


def _inproj_call(x_in, w_cat, b_cat, lng, wscat, bsfull, batch, alpha=None):
    fused = alpha is not None
    n = (x_in[0] if fused else x_in).shape[0]
    seq = n // batch
    tpb = seq // TA
    tok = lambda w: pl.BlockSpec((TA, w), lambda i: (i, 0))
    const = lambda s: pl.BlockSpec(s, lambda i: (0,) * len(s))
    regrouped = lambda dil: pl.BlockSpec((1, dil, TA // dil, 768), lambda i: (i // tpb, 0, i % tpb, 0))
    out_shape = (
        jax.ShapeDtypeStruct((n, 256), BF16),
    ) + tuple(jax.ShapeDtypeStruct((batch, dil, seq // dil, 768), BF16)
              for _, dil in DIL_PATTERNS) + (
        jax.ShapeDtypeStruct((n, 512), F32),
        jax.ShapeDtypeStruct((n, 256), BF16),
        jax.ShapeDtypeStruct((n, 256), F32),
        jax.ShapeDtypeStruct((n, 256), F32),
        jax.ShapeDtypeStruct((n // LANES, 4 * ML_HEADS, LANES), F32),
    )
    x_specs = [tok(D_MODEL), tok(D_MODEL), const((1, D_MODEL)), const((1, D_MODEL))] if fused else [tok(D_MODEL)]
    out_specs = ((tok(256),) + tuple(regrouped(dil) for _, dil in DIL_PATTERNS)
                 + (tok(512), tok(256), tok(256), tok(256),
                    pl.BlockSpec((TA // LANES, 4 * ML_HEADS, LANES), lambda i: (i, 0, 0))))
    if fused:
        out_specs += (tok(D_MODEL),)
        out_shape += (jax.ShapeDtypeStruct((n, D_MODEL), F32),)
    return pl.pallas_call(
        functools.partial(_inproj_kernel, alpha=alpha), name="inproj_gmlp",
        grid=(n // TA,),
        in_specs=x_specs + [const((D_MODEL, A_COLS)), const((1, A_COLS)), const((1, MIX_W)),
                            const((GM_CHUNK, GM_GROUPS * GM_CHUNK)), const((GM_CHUNK, MIX_W))],
        out_specs=out_specs,
        out_shape=out_shape,
        scratch_shapes=[pltpu.VMEM((768 // LANES, TA, LANES), F32)],
        compiler_params=_cparams(1),
    )(*(x_in if fused else (x_in,)), w_cat, b_cat, lng, wscat, bsfull)


def _pack_inproj_weights(w_in, b_in):
    pad = lambda a: jnp.pad(a, ((0, 0), (0, LANES - 4 * ML_HEADS)))
    w_cat = jnp.concatenate([w_in[:, 0:2304], w_in[:, 2320:2576], pad(w_in[:, 2304:2320])], axis=1)
    b2 = b_in[None, :]
    b_cat = jnp.concatenate([b2[:, 0:2304], b2[:, 2320:2576], pad(b2[:, 2304:2320])], axis=1)
    return w_cat.astype(BF16), b_cat


def _pack_gmlp(gm_ws, gm_bs):
    wscat = jnp.transpose(gm_ws, (1, 0, 2)).reshape(GM_CHUNK, GM_GROUPS * GM_CHUNK).astype(BF16)
    bsfull = jnp.repeat(jnp.transpose(gm_bs), MIX_W // GM_GROUPS, axis=1)
    return wscat, bsfull


def _halo_specs(t, width, n_tiles):
    r = t // HALO
    main = pl.BlockSpec((1, t, width), lambda b, i: (b, i, 0))
    prev = pl.BlockSpec((1, HALO, width), lambda b, i: (b, jnp.maximum(i * r - 1, 0), 0))
    nxt = pl.BlockSpec((1, HALO, width), lambda b, i: (b, jnp.minimum((i + 1) * r, n_tiles * r - 1), 0))
    return main, prev, nxt


def _fill_halo_scratch(buf, x_ref, p_ref, n_ref, t):
    i = pl.program_id(1)
    last = pl.num_programs(1) - 1
    buf[0:HALO, :] = jnp.where(i > 0, p_ref[0], 0.0)
    buf[HALO:HALO + t, :] = x_ref[0]
    buf[HALO + t:2 * HALO + t, :] = jnp.where(i < last, n_ref[0], 0.0)


TP = 1024


def _pool_kernel(x_ref, p_ref, n_ref, w_ref, sc_ref, o_ref, buf, lvl):
    _fill_halo_scratch(buf, x_ref, p_ref, n_ref, TP)
    seq = pl.num_programs(1) * TP
    pos = pl.program_id(1) * TP + lax.broadcasted_iota(jnp.int32, (TP, 1), 0)
    lane_grp = lax.broadcasted_iota(jnp.int32, (TP, MIX_W), 1) // (MIX_W // len(POOL_WINDOWS))
    x0 = buf[HALO:HALO + TP, :]
    sums = []
    src, rows = buf, TP + 2 * HALO
    for k, win in enumerate(POOL_WINDOWS):
        half = win // 2
        rows -= half
        cur = src[0:rows, :] + src[half:rows + half, :] if k else buf[0:rows, :] + buf[1:rows + 1, :]
        if k + 1 < len(POOL_WINDOWS):
            lvl[k, 0:rows, :] = cur
            sums.append(lvl[k, HALO - half:HALO - half + TP, :])
            src = lvl.at[k]
        else:
            sums.append(cur[0:TP])
    pooled = jnp.zeros((TP, MIX_W), F32)
    for gi, win in enumerate(POOL_WINDOWS):
        half = win // 2
        cnt = (jnp.minimum(pos + half, seq) - jnp.maximum(pos - half, 0)).astype(F32)
        pooled = jnp.where(lane_grp == gi, sums[gi] / cnt - x0, pooled)
    mixed = jnp.dot(pooled.astype(BF16), w_ref[...], preferred_element_type=F32)
    o_ref[0] = (mixed * sc_ref[...]).astype(BF16)


def _pool_call(dx, w_block, scale):
    b, s, _ = dx.shape
    nt = s // TP
    main, prev, nxt = _halo_specs(TP, MIX_W, nt)
    return pl.pallas_call(
        _pool_kernel, name="pool_mixer",
        grid=(b, nt),
        in_specs=[main, prev, nxt,
                  pl.BlockSpec((MIX_W, MIX_W), lambda b, i: (0, 0)),
                  pl.BlockSpec((1, MIX_W), lambda b, i: (0, 0))],
        out_specs=pl.BlockSpec((1, TP, MIX_W), lambda b, i: (b, i, 0)),
        out_shape=jax.ShapeDtypeStruct((b, s, MIX_W), BF16),
        scratch_shapes=[pltpu.VMEM((TP + 2 * HALO, MIX_W), F32),
                        pltpu.VMEM((len(POOL_WINDOWS) - 1, TP + 2 * HALO, MIX_W), F32)],
        compiler_params=_cparams(2),
    )(dx, dx, dx, w_block, scale)


TQ = 128
TQS = 1024
TKEYS = TQ + 2 * ATT_BLOCK


def _attn_kernel(q_ref, kp_ref, km_ref, kn_ref, vp_ref, vm_ref, vn_ref, bias_ref, o_ref, lse_ref):
    for sq in range(q_ref.shape[1]):
        _attn_sequence(sq, q_ref, kp_ref, km_ref, kn_ref, vp_ref, vm_ref, vn_ref, bias_ref, o_ref, lse_ref)


def _attn_sequence(sq, q_ref, kp_ref, km_ref, kn_ref, vp_ref, vm_ref, vn_ref, bias_ref, o_ref, lse_ref):
    i = pl.program_id(2)
    q = q_ref[0, sq] * ATT_HD ** -0.5
    k = jnp.concatenate([kp_ref[0, sq], km_ref[0, sq], kn_ref[0, sq]], axis=0)
    v = jnp.concatenate([vp_ref[0, sq], vm_ref[0, sq], vn_ref[0, sq]], axis=0)
    lane = lax.broadcasted_iota(jnp.int32, (TQ, LANES), 1)
    lane_half = lax.broadcasted_iota(jnp.int32, (1, LANES), 1) // ATT_HD
    keep = [jnp.where(lane_half == hh, 1.0, 0.0).astype(BF16) for hh in range(2)]
    n_sub = q_ref.shape[2] // TQ
    last_step = pl.num_programs(2) - 1
    for j in range(n_sub):
        if j == 0:
            variant = jnp.where(i == 0, 0, 1)
        elif j == n_sub - 1:
            variant = jnp.where(i == last_step, 2, 1)
        else:
            variant = 1
        qrows = slice(j * TQ, (j + 1) * TQ)
        krows = slice(j * TQ, j * TQ + TKEYS)
        lse_tile = jnp.zeros((TQ, LANES), F32)
        for pair in range(ATT_HEADS // 2):
            grp = slice(pair * LANES, (pair + 1) * LANES)
            q_pair, k_pair, v_pair = q[qrows, grp], k[krows, grp], v[krows, grp]
            o_pair = jnp.zeros((TQ, LANES), F32)
            for hh in range(2):
                h = 2 * pair + hh
                logits = lax.dot_general(q_pair * keep[hh], k_pair, (((1,), (1,)), ((), ())),
                                         preferred_element_type=F32) + bias_ref[variant, h]
                m = jnp.max(logits, axis=-1, keepdims=True)
                p = jnp.exp(logits - m)
                ssum = jnp.sum(p, axis=-1, keepdims=True)
                o = jnp.dot(p.astype(BF16), v_pair, preferred_element_type=F32) / ssum
                o_pair = jnp.where(lane_half == hh, o, o_pair)
                lse_tile = jnp.where(lane == h, m + jnp.log(ssum), lse_tile)
            o_ref[0, sq, qrows, grp] = o_pair
        lse_ref[0, sq, qrows, :] = lse_tile


def _attn_call(qkv, bias):
    b, dil, l, _ = qkv.shape
    tqs = min(l, TQS)
    nsq = min(dil, TQS // tqs)
    nt = l // tqs
    r64 = tqs // ATT_BLOCK
    main = lambda c: pl.BlockSpec((1, nsq, tqs, MIX_W), lambda b, r, i: (b, r, i, c))
    prev = lambda c: pl.BlockSpec((1, nsq, ATT_BLOCK, MIX_W),
                                  lambda b, r, i: (b, r, jnp.maximum(i * r64 - 1, 0), c))
    nxt = lambda c: pl.BlockSpec((1, nsq, ATT_BLOCK, MIX_W),
                                 lambda b, r, i: (b, r, jnp.minimum((i + 1) * r64, nt * r64 - 1), c))
    return pl.pallas_call(
        _attn_kernel, name="band_attention",
        grid=(b, dil // nsq, nt),
        in_specs=[main(0), prev(1), main(1), nxt(1), prev(2), main(2), nxt(2),
                  pl.BlockSpec((3, ATT_HEADS, TQ, TKEYS), lambda b, r, i: (0, 0, 0, 0))],
        out_specs=(pl.BlockSpec((1, nsq, tqs, MIX_W), lambda b, r, i: (b, r, i, 0)),
                   pl.BlockSpec((1, nsq, tqs, LANES), lambda b, r, i: (b, r, i, 0))),
        out_shape=(jax.ShapeDtypeStruct((b, dil, l, MIX_W), F32),
                   jax.ShapeDtypeStruct((b, dil, l, LANES), F32)),
        compiler_params=_cparams(3),
    )(qkv, qkv, qkv, qkv, qkv, qkv, qkv, bias)


def _t5_bucket_static(rel):
    half = REL_BUCKETS // 2
    max_exact = half // 2
    ret = np.where(rel > 0, half, 0)
    n = np.abs(rel)
    nf = np.maximum(n, 1).astype(np.float32)
    large = max_exact + (np.log(nf / np.float32(max_exact)) / np.float32(math.log(REL_MAX_DIST / max_exact))
                         * np.float32(half - max_exact)).astype(np.int32)
    large = np.minimum(large, half - 1)
    return ret + np.where(n < max_exact, n, large)


def _attn_bias_tile(rel_bias, window, dil):
    side = (window // 2) // dil
    rel = np.arange(TKEYS)[None, :] - ATT_BLOCK - np.arange(TQ)[:, None]
    n_rel = TKEYS + TQ - 1
    rel_values = np.arange(n_rel) - (ATT_BLOCK + TQ - 1)
    onehot = jax.nn.one_hot(jnp.asarray(_t5_bucket_static(dil * rel_values), jnp.int32), REL_BUCKETS, dtype=F32)
    table = jnp.einsum('nr,rh->hn', onehot, rel_bias, precision=lax.Precision.HIGHEST)
    periodic = jnp.tile(jnp.pad(table, ((0, 0), (0, 1))), (1, TQ))[:, :TQ * n_rel]
    bias = periodic.reshape(ATT_HEADS, TQ, n_rel)[:, :, TQ - 1:]
    key = np.arange(TKEYS)[None, :]
    inside = np.abs(rel) <= side
    masks = np.stack([inside & (key >= ATT_BLOCK), inside, inside & (key < ATT_BLOCK + TQ)])
    return jnp.where(jnp.asarray(masks)[:, None], bias[None], NEG_BIG)


TM = 1024
VT_ROWS = ML_HD + 16


def _mlprep_kernel(x_ref, p_ref, n_ref, v_ref, w_ref, qt_out, k_out, vt_out, buf):
    _fill_halo_scratch(buf, x_ref, p_ref, n_ref, TM)
    conv = (buf[HALO - 1:HALO - 1 + TM, :] * w_ref[0:1, :] + buf[HALO:HALO + TM, :] * w_ref[1:2, :]
            + buf[HALO + 1:HALO + 1 + TM, :] * w_ref[2:3, :])
    qk = jax.nn.silu(conv)
    qt = jnp.transpose(qk[:, :MIX_W])
    vt = jnp.transpose(v_ref[0].astype(F32))
    ones_rows = jnp.where(lax.broadcasted_iota(jnp.int32, (VT_ROWS - ML_HD, LANES), 0) == 0, 1.0, 0.0)
    for h in range(ML_HEADS):
        sl = slice(h * ML_HD, (h + 1) * ML_HD)
        k_out[0, h] = (qk[:, MIX_W + h * ML_HD:MIX_W + (h + 1) * ML_HD] * ML_HD ** -0.5).astype(BF16)
        for p in range(TM // LANES):
            pl_ = slice(p * LANES, (p + 1) * LANES)
            qt_out[0, h, p] = qt[sl, pl_].astype(BF16)
            vt_out[0, h, p] = jnp.concatenate([vt[sl, pl_], ones_rows], axis=0).astype(BF16)


def _mlprep_call(cqk, cv, conv_w):
    b, s, _ = cqk.shape
    nt = s // TM
    n_pairs = s // LANES
    ppt = TM // LANES
    main, prev, nxt = _halo_specs(TM, 2 * MIX_W, nt)
    return pl.pallas_call(
        _mlprep_kernel, name="mlstm_prep",
        grid=(b, nt),
        in_specs=[main, prev, nxt,
                  pl.BlockSpec((1, TM, MIX_W), lambda b, i: (b, i, 0)),
                  pl.BlockSpec((3, 2 * MIX_W), lambda b, i: (0, 0))],
        out_specs=(pl.BlockSpec((1, ML_HEADS, ppt, ML_HD, LANES), lambda b, i: (b, 0, i, 0, 0)),
                   pl.BlockSpec((1, ML_HEADS, TM, ML_HD), lambda b, i: (b, 0, i, 0)),
                   pl.BlockSpec((1, ML_HEADS, ppt, VT_ROWS, LANES), lambda b, i: (b, 0, i, 0, 0))),
        out_shape=(jax.ShapeDtypeStruct((b, ML_HEADS, n_pairs, ML_HD, LANES), BF16),
                   jax.ShapeDtypeStruct((b, ML_HEADS, s, ML_HD), BF16),
                   jax.ShapeDtypeStruct((b, ML_HEADS, n_pairs, VT_ROWS, LANES), BF16)),
        scratch_shapes=[pltpu.VMEM((TM + 2 * HALO, 2 * MIX_W), F32)],
        compiler_params=_cparams(2),
    )(cqk, cqk, cqk, cv, conv_w)


def _gate_kernel(g_ref, fb_ref, a_ref, m_ref, iw_ref, en_ref, ws_ref, dec_ref):
    x = g_ref[...]
    n_pairs = x.shape[0]
    n_ch = 2 * ML_HEADS
    lane = lax.broadcasted_iota(jnp.int32, (1, 1, LANES), 2)
    t_in = lane % ML_CHUNK
    second = lane >= ML_CHUNK
    fwd_row = lax.broadcasted_iota(jnp.int32, (1, n_ch, 1), 1) < ML_HEADS
    li = x[:, 0:n_ch, :]
    z = x[:, n_ch:2 * n_ch, :] + fb_ref[...]
    lf = jnp.minimum(z, 0.0) - jnp.log1p(jnp.exp(-jnp.abs(z)))

    def within_chunk(v, op, ident, prefix):
        s = 1
        while s < ML_CHUNK:
            nb = pltpu.roll(v, s if prefix else LANES - s, 2)
            ok = (t_in >= s) if prefix else (t_in < ML_CHUNK - s)
            v = op(v, jnp.where(ok, nb, ident))
            s *= 2
        return v

    pre = within_chunk(lf, jnp.add, 0.0, True)
    suf = within_chunk(lf, jnp.add, 0.0, False)
    g = pre + suf - lf
    b = jnp.where(fwd_row, pre, suf)
    a = li - b
    cm_pre = within_chunk(a, jnp.maximum, -jnp.inf, True)
    cm_suf = within_chunk(a, jnp.maximum, -jnp.inf, False)
    cm = jnp.where(fwd_row, cm_pre, cm_suf)
    amax = jnp.maximum(cm_pre, cm_suf)

    def shift_pairs(v, k, fill):
        pad = jnp.full((abs(k),) + v.shape[1:], fill, F32)
        return (jnp.concatenate([pad, v[:n_pairs - k]], axis=0) if k > 0
                else jnp.concatenate([v[-k:], pad], axis=0))

    def from_chunk(v, dist, fill, forward):
        if dist == 1:
            y = pltpu.roll(v, ML_CHUNK, 2)
            if forward:
                return jnp.where(second, y, shift_pairs(y, 1, fill))
            return jnp.where(second, shift_pairs(y, -1, fill), y)
        return shift_pairs(v, dist // 2 if forward else -(dist // 2), fill)

    def running_stabiliser(forward):
        big_g, big_a = g, amax + g
        dist = 1
        while dist < 2 * n_pairs:
            gp = from_chunk(big_g, dist, 0.0, forward)
            ap = from_chunk(big_a, dist, -jnp.inf, forward)
            big_g, big_a = gp + big_g, jnp.maximum(ap + big_g, big_a)
            dist *= 2
        ge = from_chunk(big_g, 1, 0.0, forward)
        ae = from_chunk(big_a, 1, -jnp.inf, forward)
        return jnp.maximum(ge, ae)

    m_chunk = jnp.where(fwd_row, running_stabiliser(True), running_stabiliser(False))
    m_t = jnp.maximum(cm, m_chunk)
    m_last = jnp.maximum(amax, m_chunk)
    a_ref[0] = a
    m_ref[0] = m_t
    iw_ref[0] = jnp.exp(m_chunk - m_t)
    en_ref[0] = jnp.exp(-(b + m_t))
    ws_ref[0] = jnp.exp(a - m_last)
    dec_ref[0] = jnp.exp(m_chunk - m_last)


def _gate_call(gates_t, fbias_col, batch):
    n_pairs = gates_t.shape[0] // batch
    n_ch = 2 * ML_HEADS
    out = pl.BlockSpec((1, n_pairs, n_ch, LANES), lambda i: (i, 0, 0, 0))
    return pl.pallas_call(
        _gate_kernel, name="mlstm_gates",
        grid=(batch,),
        in_specs=[pl.BlockSpec((n_pairs, 2 * n_ch, LANES), lambda i: (i, 0, 0)),
                  pl.BlockSpec((n_ch, 1), lambda i: (0, 0))],
        out_specs=(out,) * 6,
        out_shape=(jax.ShapeDtypeStruct((batch, n_pairs, n_ch, LANES), F32),) * 6,
        compiler_params=_cparams(1),
    )(gates_t, fbias_col)


TE = 2048


def _mlstm_kernel(*refs):
    fwd, bwd, (hf_ref, hb_ref, state) = refs[:9], refs[9:18], refs[18:]
    i = pl.program_id(1)

    @pl.when(i == 0)
    def _():
        state[...] = jnp.zeros(state.shape, F32)

    n_pairs = TE // LANES
    s_idx = lax.broadcasted_iota(jnp.int32, (LANES, LANES), 0)
    t_idx = lax.broadcasted_iota(jnp.int32, (LANES, LANES), 1)
    same_chunk = (s_idx >= ML_CHUNK) == (t_idx >= ML_CHUNK)
    upper_lanes = lax.broadcasted_iota(jnp.int32, (1, LANES), 1) >= ML_CHUNK

    def pair_body(p, carry):
        jobs = []
        for d, ((qt_r, k_r, vt_r, a_r, m_r, iw_r, en_r, ws_r, dec_r), out_r) in enumerate(
                ((fwd, hf_ref), (bwd, hb_ref))):
            pp = p if d == 0 else n_pairs - 1 - p
            srows = pl.ds(pl.multiple_of(pp * LANES, LANES), LANES)
            a_t = jnp.transpose(jnp.concatenate(
                [a_r[0, pp], jnp.zeros((LANES - 2 * ML_HEADS, LANES), F32)], axis=0))
            m_t, iw_t, en_t, ws_t, dec_t = [r[0, pp] for r in (m_r, iw_r, en_r, ws_r, dec_r)]
            dec_lo, dec_hi = dec_t[:, :ML_CHUNK], pltpu.roll(dec_t, ML_CHUNK, 1)[:, :ML_CHUNK]
            in_first = upper_lanes if d else ~upper_lanes
            tri = same_chunk & ((s_idx >= t_idx) if d else (s_idx <= t_idx))
            for h in range(ML_HEADS):
                ch = d * ML_HEADS + h
                row = lambda t: t[ch:ch + 1]
                jobs.append(dict(
                    ch=ch, tri=tri, in_first=in_first, k=k_r[0, h, srows, :], qt=qt_r[0, h, pp],
                    vt=vt_r[0, h, pp], a=a_t[:, ch:ch + 1], m=row(m_t), iw=row(iw_t), en=row(en_t), ws=row(ws_t),
                    dec_first=row(dec_hi if d else dec_lo), dec_second=row(dec_lo if d else dec_hi),
                    out=(out_r, pp, h)))
        for j in jobs:
            vt_f = j["vt"].astype(F32)
            j["c0"] = state[j["ch"]]
            j["st"] = jnp.dot(j["k"], j["qt"], preferred_element_type=F32)
            j["inter1"] = jnp.dot(j["c0"].astype(BF16), j["qt"], preferred_element_type=F32)
            j["upd1"] = jnp.dot((vt_f * jnp.where(j["in_first"], j["ws"], 0.0)).astype(BF16), j["k"],
                                preferred_element_type=F32)
            j["upd2"] = jnp.dot((vt_f * jnp.where(j["in_first"], 0.0, j["ws"])).astype(BF16), j["k"],
                                preferred_element_type=F32)
        for j in jobs:
            j["swt"] = j["st"] * jnp.exp(jnp.where(j["tri"], j["a"] - j["m"], NEG_BIG))
            j["intra"] = jnp.dot(j["vt"], j["swt"].astype(BF16), preferred_element_type=F32)
            j["c1"] = j["dec_first"] * j["c0"] + j["upd1"]
            j["inter2"] = jnp.dot(j["c1"].astype(BF16), j["qt"], preferred_element_type=F32)
        for j in jobs:
            inter = jnp.where(j["in_first"], j["inter1"], j["inter2"])
            den = jnp.sum(j["swt"], axis=0, keepdims=True) + j["iw"] * inter[ML_HD:ML_HD + 1]
            tot = j["intra"][:ML_HD] + j["iw"] * inter[:ML_HD]
            out_r, pp, h = j["out"]
            out_r[0, pp, h * ML_HD:(h + 1) * ML_HD, :] = tot / jnp.maximum(jnp.abs(den), j["en"])
            state[j["ch"]] = j["dec_second"] * j["c1"] + j["upd2"]
        return carry

    lax.fori_loop(0, n_pairs, pair_body, 0)


def _mlstm_call(qt, k, vt, a_t, m_t, iw_t, en_t, ws_t, dec_t):
    b, _, s, _ = k.shape
    nt = s // TE
    ppt = TE // LANES

    def specs(rev):
        ti = (lambda i: nt - 1 - i) if rev else (lambda i: i)
        tile = pl.BlockSpec((1, ppt, 2 * ML_HEADS, LANES), lambda b, i: (b, ti(i), 0, 0))
        return [
            pl.BlockSpec((1, ML_HEADS, ppt, ML_HD, LANES), lambda b, i: (b, 0, ti(i), 0, 0)),
            pl.BlockSpec((1, ML_HEADS, TE, ML_HD), lambda b, i: (b, 0, ti(i), 0)),
            pl.BlockSpec((1, ML_HEADS, ppt, VT_ROWS, LANES), lambda b, i: (b, 0, ti(i), 0, 0)),
            tile, tile, tile, tile, tile, tile]

    args = [qt, k, vt, a_t, m_t, iw_t, en_t, ws_t, dec_t]
    out_f = pl.BlockSpec((1, ppt, MIX_W, LANES), lambda b, i: (b, i, 0, 0))
    out_b = pl.BlockSpec((1, ppt, MIX_W, LANES), lambda b, i: (b, nt - 1 - i, 0, 0))
    return pl.pallas_call(
        _mlstm_kernel, name="mlstm_scan",
        grid=(b, nt),
        in_specs=specs(False) + specs(True),
        out_specs=(out_f, out_b),
        out_shape=(jax.ShapeDtypeStruct((b, s // LANES, MIX_W, LANES), F32),) * 2,
        scratch_shapes=[pltpu.VMEM((2 * ML_HEADS, VT_ROWS, ML_HD), F32)],
        compiler_params=_cparams(2),
    )(*args, *args)


def _mlstm_branch(cqk, cv, gates_t, conv_w, fbias):
    qt, k, vt = _mlprep_call(cqk, cv, conv_w)
    factors = _gate_call(gates_t, fbias.reshape(2 * ML_HEADS, 1), cqk.shape[0])
    return _mlstm_call(qt, k, vt, *factors)


TF = 512


def _merge_kernel(x_ref, ya_ref, o1_ref, o2_ref, o3_ref, l1_ref, l2_ref, l3_ref, hf_ref, hb_ref,
                  co_ref, yd_ref, wg_ref, bg_ref, wbr_ref, wout_ref, mng_ref, lng_ref, lnb_ref,
                  wr_ref, x1_ref, x1p_ref, aff_ref, o_scr, l_scr, *, alpha):
    x = x_ref[0]
    xb = x.astype(BF16)

    def natural_order(src_ref, scr):
        dil, width = src_ref.shape[1], src_ref.shape[3]
        if dil == 1:
            return src_ref[0, 0]
        for r in range(dil):
            for c in range(width // LANES):
                scr[c, pl.ds(r, TF // dil, stride=dil), :] = src_ref[0, r, :, c * LANES:(c + 1) * LANES]
        return jnp.concatenate([scr[c] for c in range(width // LANES)], axis=1)

    lane_head = lax.broadcasted_iota(jnp.int32, (TF, MIX_W), 1) // ML_HD
    l1, l2, l3 = [natural_order(r, l_scr.at[p]) for p, r in enumerate((l1_ref, l2_ref, l3_ref))]
    o1, o2, o3 = [natural_order(r, o_scr.at[p]) for p, r in enumerate((o1_ref, o2_ref, o3_ref))]
    lm = jnp.maximum(jnp.maximum(l1, l2), l3)
    e1, e2, e3 = jnp.exp(l1 - lm), jnp.exp(l2 - lm), jnp.exp(l3 - lm)
    inv = 1.0 / (e1 + e2 + e3)

    def per_head(w):
        out = jnp.zeros((TF, MIX_W), F32)
        for h in range(ATT_HEADS):
            out = jnp.where(lane_head == h, w[:, h:h + 1], out)
        return out

    y_b = per_head(e1 * inv) * o1 + per_head(e2 * inv) * o2 + per_head(e3 * inv) * o3
    hsum_t = jnp.concatenate([hf_ref[0, p] + hb_ref[0, p] for p in range(TF // LANES)], axis=1)
    per_head_rows = hsum_t.reshape(ML_HEADS, ML_HD, TF)
    mu = jnp.mean(per_head_rows, axis=1, keepdims=True)
    cen = per_head_rows - mu
    var = jnp.mean(cen * cen, axis=1, keepdims=True)
    hn_t = (cen * lax.rsqrt(var + LN_EPS)).reshape(MIX_W, TF)
    y_c_t = (jax.nn.sigmoid(jnp.transpose(co_ref[0])) * (hn_t * mng_ref[...])).astype(BF16)
    ys = (ya_ref[0], y_b.astype(BF16), None, yd_ref[0])
    merged = jnp.zeros((TF, D_MODEL), F32)
    for n in range(N_BRANCH):
        cols = slice(n * D_MODEL, (n + 1) * D_MODEL)
        gate = jax.nn.sigmoid(jnp.dot(xb, wg_ref[:, cols], preferred_element_type=F32) + bg_ref[:, cols])
        if ys[n] is None:
            proj = lax.dot_general(y_c_t, wbr_ref[n], (((0,), (0,)), ((), ())), preferred_element_type=F32)
        else:
            proj = jnp.dot(ys[n], wbr_ref[n], preferred_element_type=F32)
        merged = merged + gate * proj
    mix = jnp.dot(merged.astype(BF16), wout_ref[...], preferred_element_type=F32)
    x1 = _standardize(alpha * x + mix) * lng_ref[...] + lnb_ref[...]
    x1_ref[0] = x1
    x1b = x1.astype(BF16)
    x1p_ref[0] = _pack_bf16_pair(x1b[:, :D_MODEL // 2], x1b[:, D_MODEL // 2:])
    logits = lax.dot_general(wr_ref[...], x1b, (((1,), (1,)), ((), ())),
                             preferred_element_type=F32)
    ex = jnp.exp(logits - jnp.max(logits, axis=0, keepdims=True))
    aff_ref[0] = ex / jnp.sum(ex, axis=0, keepdims=True)


def _merge_call(x, ya, o_list, l_list, hf, hb, co, yd, wg, bg, wbr, wout, mng, lng, lnb, wr_t, alpha):
    b, s, _ = x.shape
    tok = lambda w: pl.BlockSpec((1, TF, w), lambda b, i: (b, i, 0))
    grouped = lambda dil, w: pl.BlockSpec((1, dil, TF // dil, w), lambda b, i: (b, 0, i, 0))
    chunked = pl.BlockSpec((1, TF // LANES, MIX_W, LANES), lambda b, i: (b, i, 0, 0))
    const = lambda shp: pl.BlockSpec(shp, lambda b, i: (0,) * len(shp))
    return pl.pallas_call(
        functools.partial(_merge_kernel, alpha=alpha), name="merge_ln_router",
        grid=(b, s // TF),
        in_specs=[tok(D_MODEL), tok(MIX_W)] + [grouped(dil, MIX_W) for _, dil in DIL_PATTERNS]
                 + [grouped(dil, LANES) for _, dil in DIL_PATTERNS]
                 + [chunked, chunked, tok(MIX_W), tok(MIX_W)]
                 + [const((D_MODEL, N_BRANCH * D_MODEL)), const((1, N_BRANCH * D_MODEL)),
                    const((N_BRANCH, MIX_W, D_MODEL)), const((D_MODEL, D_MODEL)), const((MIX_W, 1)),
                    const((1, D_MODEL)), const((1, D_MODEL)), const((N_EXPERTS, D_MODEL))],
        out_specs=(tok(D_MODEL), tok(D_MODEL // 2), pl.BlockSpec((1, N_EXPERTS, TF), lambda b, i: (b, 0, i))),
        out_shape=(jax.ShapeDtypeStruct((b, s, D_MODEL), F32),
                   jax.ShapeDtypeStruct((b, s, D_MODEL // 2), jnp.int32),
                   jax.ShapeDtypeStruct((b, N_EXPERTS, s), F32)),
        scratch_shapes=[pltpu.VMEM((len(DIL_PATTERNS), MIX_W // LANES, TF, LANES), F32),
                        pltpu.VMEM((len(DIL_PATTERNS), 1, TF, LANES), F32)],
        compiler_params=_cparams(2),
    )(x, ya, *o_list, *l_list, hf, hb, co, yd, wg, bg, wbr, wout, mng, lng, lnb, wr_t)


TT = 256


def _select_kernel(aff_ref, slot_ref, *, cap):
    s = aff_ref.shape[2]
    bits = pltpu.bitcast(aff_ref[0], jnp.int32)

    def bit_step(i, thr):
        cand = thr | jnp.left_shift(jnp.int32(1), 30 - i)
        cnt = jnp.sum((bits >= cand).astype(jnp.int32), axis=1, keepdims=True)
        return jnp.where(cnt >= cap, cand, thr)

    thr = lax.fori_loop(0, 31, bit_step, jnp.zeros((N_EXPERTS, 1), jnp.int32))
    gt = bits > thr
    eq = bits == thr
    need = (cap - jnp.sum(gt.astype(jnp.int32), axis=1, keepdims=True)).astype(F32)
    upper = (lax.broadcasted_iota(jnp.int32, (TT, TT), 0)
             <= lax.broadcasted_iota(jnp.int32, (TT, TT), 1)).astype(BF16)
    eq_before = jnp.zeros((N_EXPERTS, 1), F32)
    sel_before = jnp.zeros((N_EXPERTS, 1), F32)
    for j in range(s // TT):
        cols = slice(j * TT, (j + 1) * TT)
        eq_j = eq[:, cols]
        eq_incl = eq_before + jnp.dot(eq_j.astype(BF16), upper, preferred_element_type=F32)
        sel_j = gt[:, cols] | (eq_j & (eq_incl <= need))
        sel_f = sel_j.astype(F32)
        sel_incl = sel_before + jnp.dot(sel_f.astype(BF16), upper, preferred_element_type=F32)
        slot_ref[0, :, cols] = jnp.where(sel_j, sel_incl - 1.0, -1.0).astype(jnp.int32)
        eq_before = eq_incl[:, TT - 1:TT]
        sel_before = sel_incl[:, TT - 1:TT]


def _select_call(aff_t, cap):
    b, e, s = aff_t.shape
    return pl.pallas_call(
        functools.partial(_select_kernel, cap=cap), name="expert_choice_select",
        grid=(b,),
        in_specs=[pl.BlockSpec((1, e, s), lambda i: (i, 0, 0))],
        out_specs=pl.BlockSpec((1, e, s), lambda i: (i, 0, 0)),
        out_shape=jax.ShapeDtypeStruct((b, e, s), jnp.int32),
        compiler_params=_cparams(1),
    )(aff_t)


SC_LANES = 16
SC_ROWS = 64
SC_IDX = 128
SC_SLAB = 128
SC_ZROWS = 64
CF = 1024


def _sc_dispatch_call(x_flat, slot2, aff2, seq, cap):
    n_pairs = slot2.shape[0]
    d = x_flat.shape[1]
    info = plsc.get_sparse_core_info()
    n_workers = info.num_cores * info.num_subcores
    assert n_pairs % n_workers == 0 and seq % SC_LANES == 0 and cap % (2 * SC_ROWS) == 0
    pairs_per_worker = n_pairs // n_workers
    mesh = plsc.VectorSubcoreMesh(core_axis_name="c", subcore_axis_name="s")

    @functools.partial(
        pl.kernel, mesh=mesh, name="expert_dispatch_sc",
        compiler_params=pltpu.CompilerParams(needs_layout_passes=False),
        out_type=(jax.ShapeDtypeStruct((n_pairs * cap, d), x_flat.dtype),
                  jax.ShapeDtypeStruct((n_pairs, cap // SC_IDX, SC_IDX), jnp.int32),
                  jax.ShapeDtypeStruct((n_pairs, cap // SC_IDX, SC_IDX), F32)),
        scratch_types=[pltpu.VMEM((seq,), jnp.int32), pltpu.VMEM((seq,), F32),
                       pltpu.VMEM((cap,), jnp.int32), pltpu.VMEM((cap // SC_IDX, SC_IDX), jnp.int32),
                       pltpu.VMEM((cap // SC_IDX, SC_IDX), F32),
                       pltpu.VMEM((2, SC_ROWS, d), x_flat.dtype), pltpu.SemaphoreType.DMA((2,))])
    def dispatch(x_hbm, slot_hbm, aff_hbm, xs_hbm, tok_hbm, gate_hbm,
                 slot_v, aff_v, idx_v, tok_v, gate_v, rows_v, sem):
        worker = lax.axis_index("s") * info.num_cores + lax.axis_index("c")
        lane = lax.iota(jnp.int32, SC_LANES)

        def gather(c0, buf):
            return pltpu.make_async_copy(x_hbm.at[idx_v.at[pl.ds(c0, SC_ROWS)]], rows_v.at[buf], sem.at[buf])
        for k in range(pairs_per_worker):
            pair = worker * pairs_per_worker + k
            row0 = (pair // N_EXPERTS) * seq
            pltpu.sync_copy(slot_hbm.at[pair], slot_v)
            pltpu.sync_copy(aff_hbm.at[pair], aff_v)

            @plsc.parallel_loop(0, seq, step=SC_LANES, unroll=4)
            def _(t0):
                sv = slot_v[pl.ds(t0, SC_LANES)]
                picked = sv >= 0
                hi, lo = lax.shift_right_logical(sv, 7), sv & (SC_IDX - 1)
                plsc.store_scatter(tok_v, [hi, lo], t0 + lane, mask=picked)
                plsc.store_scatter(idx_v, [sv], row0 + t0 + lane, mask=picked)
                plsc.store_scatter(gate_v, [hi, lo], aff_v[pl.ds(t0, SC_LANES)], mask=picked)

            pltpu.sync_copy(tok_v, tok_hbm.at[pair])
            pltpu.sync_copy(gate_v, gate_hbm.at[pair])

            gather(0, 0).start()

            @pl.loop(0, cap, step=2 * SC_ROWS)
            def _(c0):
                gather(c0 + SC_ROWS, 1).start()
                gather(c0, 0).wait()
                pltpu.sync_copy(rows_v.at[0], xs_hbm.at[pl.ds(pair * cap + c0, SC_ROWS)])

                @pl.when(c0 + 2 * SC_ROWS < cap)
                def _():
                    gather(c0 + 2 * SC_ROWS, 0).start()

                gather(c0 + SC_ROWS, 1).wait()
                pltpu.sync_copy(rows_v.at[1], xs_hbm.at[pl.ds(pair * cap + c0 + SC_ROWS, SC_ROWS)])

    return dispatch(x_flat, slot2, aff2)


def _expert_kernel(xs_ref, g_ref, w1_ref, w3_ref, w2_ref, ye_ref, w1_bf, w3_bf, w2_bf):
    @pl.when((pl.program_id(1) == 0) & (pl.program_id(2) == 0))
    def _():
        w1_bf[...] = w1_ref[0, 0].astype(BF16)
        w3_bf[...] = w3_ref[0, 0].astype(BF16)
        w2_bf[...] = w2_ref[0, 0].astype(BF16)

    xs = jnp.concatenate(_unpack_bf16_pair(xs_ref[0, 0]), axis=1)
    hid = (jax.nn.silu(jnp.dot(xs, w1_bf[...], preferred_element_type=F32))
           * jnp.dot(xs, w3_bf[...], preferred_element_type=F32))
    g_rows = g_ref[0, 0]
    n_rows = g_rows.shape[0]
    g_t = jnp.transpose(jnp.concatenate([g_rows, jnp.zeros((LANES - n_rows, LANES), F32)], axis=0))
    g_col = jnp.concatenate([g_t[:, r:r + 1] for r in range(n_rows)], axis=0)
    ye_ref[0, 0] = jnp.dot(hid.astype(BF16), w2_bf[...], preferred_element_type=F32) * g_col


def _expert_call(xs4, gate4, w1, w3, w2, layer):
    b, e, cap, half = xs4.shape
    d, ff = w1.shape[2], w1.shape[3]
    assert d == 2 * half
    rows = lambda w: pl.BlockSpec((1, 1, CF, w), lambda e, b, j: (b, e, j, 0))
    wspec = lambda r, c: pl.BlockSpec((1, 1, r, c), lambda e, b, j: (layer, e, 0, 0))
    return pl.pallas_call(
        _expert_kernel, name="expert_ffn",
        grid=(e, b, cap // CF),
        in_specs=[rows(half), pl.BlockSpec((1, 1, CF // LANES, LANES), lambda e, b, j: (b, e, j, 0)),
                  wspec(d, ff), wspec(d, ff), wspec(ff, d)],
        out_specs=rows(d),
        out_shape=jax.ShapeDtypeStruct((b, e, cap, d), F32),
        scratch_shapes=[pltpu.VMEM((d, ff), BF16), pltpu.VMEM((d, ff), BF16), pltpu.VMEM((ff, d), BF16)],
        compiler_params=_cparams(3),
    )(xs4, gate4, w1, w3, w2)


def _sc_combine_call(ye_flat, tok3, seq):
    n_pairs, n_chunks, _ = tok3.shape
    cap = n_chunks * SC_IDX
    d = ye_flat.shape[1]
    nb = n_pairs // N_EXPERTS
    info = plsc.get_sparse_core_info()
    assert info.num_subcores == N_EXPERTS and nb % info.num_cores == 0 and n_chunks % 2 == 0
    assert seq % (info.num_subcores * SC_ZROWS) == 0 and d % SC_SLAB == 0
    batches_per_core = nb // info.num_cores
    own_rows = seq // info.num_subcores
    mesh = plsc.VectorSubcoreMesh(core_axis_name="c", subcore_axis_name="s")

    @functools.partial(
        pl.kernel, mesh=mesh, name="expert_combine_sc",
        compiler_params=pltpu.CompilerParams(needs_layout_passes=False),
        out_type=jax.ShapeDtypeStruct((nb * seq, d), F32),
        scratch_types=[pltpu.VMEM_SHARED((seq, SC_SLAB), F32),
                       pltpu.VMEM((n_chunks, SC_IDX), jnp.int32),
                       pltpu.VMEM((2, SC_IDX, SC_SLAB), F32),
                       pltpu.VMEM((SC_ZROWS, SC_SLAB), F32),
                       pltpu.SemaphoreType.DMA((2,))])
    def combine(ye_hbm, tok_hbm, out_hbm, acc_sh, tok_v, rows_v, zero_v, sem):
        core = lax.axis_index("c")
        sub = lax.axis_index("s")

        @pl.loop(0, SC_ZROWS)
        def _(r):
            for l0 in range(0, SC_SLAB, SC_LANES):
                zero_v[r, pl.ds(l0, SC_LANES)] = jnp.zeros((SC_LANES,), F32)

        for bb in range(batches_per_core):
            batch = core * batches_per_core + bb
            pair = batch * N_EXPERTS + sub
            pltpu.sync_copy(tok_hbm.at[pair], tok_v)

            @pl.loop(0, d // SC_SLAB)
            def _(slab):
                cols = pl.ds(pl.multiple_of(slab * SC_SLAB, SC_SLAB), SC_SLAB)

                @pl.loop(0, own_rows, step=SC_ZROWS)
                def _(r0):
                    pltpu.sync_copy(zero_v, acc_sh.at[pl.ds(sub * own_rows + r0, SC_ZROWS)])

                def load(j, buf):
                    return pltpu.make_async_copy(
                        ye_hbm.at[pl.ds(pair * cap + j * SC_IDX, SC_IDX), cols], rows_v.at[buf], sem.at[buf])

                load(0, 0).start()
                plsc.subcore_barrier()

                for j in range(0, n_chunks, 2):
                    load(j + 1, 1).start()
                    load(j, 0).wait()
                    pltpu.sync_copy(rows_v.at[0], acc_sh.at[tok_v.at[j]], add=True)
                    if j + 2 < n_chunks:
                        load(j + 2, 0).start()
                    load(j + 1, 1).wait()
                    pltpu.sync_copy(rows_v.at[1], acc_sh.at[tok_v.at[j + 1]], add=True)

                plsc.subcore_barrier()
                pltpu.sync_copy(acc_sh.at[pl.ds(sub * own_rows, own_rows)],
                                out_hbm.at[pl.ds(batch * seq + sub * own_rows, own_rows), cols])

    return combine(ye_flat, tok3)


TN = 1024


def _resln_kernel(x_ref, y_ref, g_ref, b_ref, o_ref, *, alpha):
    o_ref[...] = _standardize(alpha * x_ref[...] + y_ref[...]) * g_ref[...] + b_ref[...]


def _resln_call(x2d, y2d, g, bta, alpha):
    n, d = x2d.shape
    tok = pl.BlockSpec((TN, d), lambda i: (i, 0))
    vec = pl.BlockSpec((1, d), lambda i: (0, 0))
    return pl.pallas_call(
        functools.partial(_resln_kernel, alpha=alpha), name="residual_layernorm",
        grid=(n // TN,), in_specs=[tok, tok, vec, vec], out_specs=tok,
        out_shape=jax.ShapeDtypeStruct((n, d), F32),
        compiler_params=_cparams(1),
    )(x2d, y2d, g, bta)


def _expert_choice_ffn(x1p, aff_t, w1, w3, w2, layer):
    b, s, half = x1p.shape
    d = 2 * half
    cap = EC_FACTOR * s // N_EXPERTS
    slot = _select_call(aff_t, cap)
    xs, tok, gate = _sc_dispatch_call(x1p.reshape(b * s, half), slot.reshape(b * N_EXPERTS, s),
                                      aff_t.reshape(b * N_EXPERTS, s), s, cap)
    ye = _expert_call(xs.reshape(b, N_EXPERTS, cap, half), gate.reshape(b, N_EXPERTS, cap // SC_IDX, SC_IDX), w1, w3, w2,
                      layer)
    out = _sc_combine_call(ye.reshape(b * N_EXPERTS * cap, d),
                           tok, s)
    return out.reshape(b, s, d)


def _pack_pool(pool_w):
    g, gd, _ = pool_w.shape
    out = jnp.zeros((g * gd, g * gd), F32)
    for i in range(g):
        out = out.at[i * gd:(i + 1) * gd, i * gd:(i + 1) * gd].set(pool_w[i])
    return out.astype(BF16)


def _layer(layer, x, pending, alpha, bias_tiles, w_in, b_in, gm_ln_g, gm_ws, gm_bs, ml_conv, ml_fbias,
           ml_norm_g, pool_w, pool_scale, w_branch, w_out, ln1_g, ln1_b, w_router, w_e1, w_e3, w_e2):
    b, s, d = x.shape
    n_small = 2576
    w_cat, b_cat = _pack_inproj_weights(w_in, b_in)
    wscat, bsfull = _pack_gmlp(gm_ws, gm_bs)
    if pending is None:
        outs = _inproj_call(x.reshape(b * s, d), w_cat, b_cat, gm_ln_g[None], wscat, bsfull, b)
    else:
        *outs, x2 = _inproj_call(pending, w_cat, b_cat, gm_ln_g[None], wscat, bsfull, b, alpha)
        x = x2.reshape(b, s, d)
    ya, qkv1, qkv4, qkv16, cqk, cv, co, dx, gates_t = outs
    r3 = lambda t: t.reshape(b, s, t.shape[-1])
    o_list, l_list = [], []
    for qkv, bias in zip((qkv1, qkv4, qkv16), bias_tiles):
        o, lse = _attn_call(qkv, bias)
        o_list.append(o)
        l_list.append(lse)
    hf, hb = _mlstm_branch(r3(cqk), r3(cv), gates_t, ml_conv, ml_fbias)
    yd = _pool_call(r3(dx), _pack_pool(pool_w), pool_scale[None])
    x1, x1p, aff_t = _merge_call(
        x, r3(ya), o_list, l_list, hf, hb, r3(co), yd,
        w_in[:, n_small:].astype(BF16), b_in[None, n_small:], w_branch.astype(BF16), w_out.astype(BF16),
        ml_norm_g[:, None], ln1_g[None], ln1_b[None], jnp.transpose(w_router).astype(BF16), alpha)
    ffn = _expert_choice_ffn(x1p, aff_t, w_e1, w_e3, w_e2, layer)
    return x1.reshape(b * s, d), ffn.reshape(b * s, d)


def kernel(x, w_in, b_in, gm_ln_g, gm_ws, gm_bs, rel_bias, ml_conv, ml_fbias, ml_norm_g, pool_w,
           pool_scale, w_branch, w_out, ln1_g, ln1_b, w_router, w_e1, w_e3, w_e2, ln2_g, ln2_b):
    depth = w_in.shape[0]
    alpha = (2 * depth) ** 0.25
    bias_tiles = [_attn_bias_tile(rel_bias, window, dil) for window, dil in DIL_PATTERNS]
    b, s, d = x.shape
    pending = None
    for l in range(depth):
        x1, ffn = _layer(l, x, pending, alpha, bias_tiles, w_in[l], b_in[l], gm_ln_g[l], gm_ws[l], gm_bs[l],
                         ml_conv[l], ml_fbias[l], ml_norm_g[l], pool_w[l], pool_scale[l], w_branch[l],
                         w_out[l], ln1_g[l], ln1_b[l], w_router[l], w_e1, w_e3, w_e2)
        pending = (x1, ffn, ln2_g[l][None], ln2_b[l][None])
    return _resln_call(*pending, alpha).reshape(b, s, d)
```

```python
import functools
import math

import jax
import jax.numpy as jnp
import numpy as np
from jax import lax
from jax.experimental import pallas as pl
from jax.experimental.pallas import tpu as pltpu
from jax.experimental.pallas import tpu_sc as plsc

F32 = jnp.float32
BF16 = jnp.bfloat16

D_MODEL = 1024
MIX_W = 256
N_BRANCH = 4
GM_CHUNK = 128
GM_GROUPS = 4
ATT_HEADS = 4
ATT_HD = 64
DIL_PATTERNS = ((128, 1), (512, 4), (2048, 16))
ATT_BLOCK = 64
REL_BUCKETS = 32
REL_MAX_DIST = 1024
ML_HEADS = 4
ML_HD = 64
ML_CHUNK = 64
POOL_WINDOWS = (2, 4, 8, 16)
N_EXPERTS = 16
EXPERT_FF = 1024
EC_FACTOR = 2
LN_EPS = 1e-5
NEG_BIG = -1e30

V7X_VMEM_LIMIT = 56 * 1024 * 1024
LANES = 128
HALO = 8


def _cparams(n_grid, vmem=V7X_VMEM_LIMIT):
    return pltpu.CompilerParams(dimension_semantics=("arbitrary",) * n_grid,
                                vmem_limit_bytes=vmem)


def _pack_bf16_pair(lo, hi):
    lo_bits = lax.shift_right_logical(pltpu.bitcast(lo.astype(F32), jnp.int32), 16)
    return pltpu.bitcast(hi.astype(F32), jnp.int32) | lo_bits


def _unpack_bf16_pair(packed):
    lo = pltpu.bitcast(lax.shift_left(packed, 16), F32).astype(BF16)
    hi = pltpu.bitcast(packed & jnp.int32(-65536), F32).astype(BF16)
    return lo, hi


def _standardize(xf):
    mu = jnp.mean(xf, axis=-1, keepdims=True)
    var = jnp.mean(jnp.square(xf - mu), axis=-1, keepdims=True)
    return (xf - mu) * lax.rsqrt(var + LN_EPS)


TA = 512
A_COLS = 2560 + LANES


def _inproj_kernel(*refs, alpha):
    if alpha is None:
        x_ref, *refs = refs
        x = x_ref[...]
    else:
        x1_ref, y_ref, g2_ref, b2_ref, *refs = refs
        x = _standardize(alpha * x1_ref[...] + y_ref[...]) * g2_ref[...] + b2_ref[...]
        refs[-2][...] = x
        refs = refs[:-2] + refs[-1:]
    (w_ref, b_ref, lng_ref, wscat_ref, bsfull_ref,
     ya_ref, qkv1_ref, qkv4_ref, qkv16_ref, cqk_ref, cv_ref, co_ref, dx_ref, gt_ref, qkv_scr) = refs
    xb = x.astype(BF16)
    h = jnp.dot(xb, w_ref[...], preferred_element_type=F32) + b_ref[...]
    qkv1_ref[0, 0] = h[:, 512:1280].astype(BF16)
    for c in range(768 // LANES):
        qkv_scr[c] = h[:, 512 + c * LANES:512 + (c + 1) * LANES]
    for (_, dil), out_ref in zip(DIL_PATTERNS[1:], (qkv4_ref, qkv16_ref)):
        for r in range(dil):
            for c in range(768 // LANES):
                out_ref[0, r, :, c * LANES:(c + 1) * LANES] = (
                    qkv_scr[c, pl.ds(r, TA // dil, stride=dil), :].astype(BF16))
    cqk_ref[...] = h[:, 1280:1792]
    cv_ref[...] = h[:, 1792:2048].astype(BF16)
    co_ref[...] = h[:, 2048:2304]
    dx_ref[...] = h[:, 2304:2560]
    gates_t = jnp.transpose(h[:, 2560:2688])
    for j in range(TA // LANES):
        gt_ref[j] = gates_t[0:4 * ML_HEADS, j * LANES:(j + 1) * LANES]
    u = jax.nn.gelu(h[:, 0:256])
    v = jax.nn.gelu(h[:, 256:512])
    vn = _standardize(v) * lng_ref[...]
    lane_grp = lax.broadcasted_iota(jnp.int32, (GM_CHUNK, MIX_W), 1) // (MIX_W // GM_GROUPS)
    for c in range(TA // GM_CHUNK):
        vc = vn[c * GM_CHUNK:(c + 1) * GM_CHUNK]
        stacked = jnp.concatenate(
            [jnp.where(lane_grp == g, vc, 0.0).astype(BF16) for g in range(GM_GROUPS)], axis=0)
        mixed = jnp.dot(wscat_ref[...], stacked, preferred_element_type=F32) + bsfull_ref[...]
        ya_ref[c * GM_CHUNK:(c + 1) * GM_CHUNK, :] = (
            u[c * GM_CHUNK:(c + 1) * GM_CHUNK] * mixed).astype(BF16)


def _inproj_call(x_in, w_cat, b_cat, lng, wscat, bsfull, batch, alpha=None):
    fused = alpha is not None
    n = (x_in[0] if fused else x_in).shape[0]
    seq = n // batch
    tpb = seq // TA
    tok = lambda w: pl.BlockSpec((TA, w), lambda i: (i, 0))
    const = lambda s: pl.BlockSpec(s, lambda i: (0,) * len(s))
    regrouped = lambda dil: pl.BlockSpec((1, dil, TA // dil, 768), lambda i: (i // tpb, 0, i % tpb, 0))
    out_shape = (
        jax.ShapeDtypeStruct((n, 256), BF16),
    ) + tuple(jax.ShapeDtypeStruct((batch, dil, seq // dil, 768), BF16)
              for _, dil in DIL_PATTERNS) + (
        jax.ShapeDtypeStruct((n, 512), F32),
        jax.ShapeDtypeStruct((n, 256), BF16),
        jax.ShapeDtypeStruct((n, 256), F32),
        jax.ShapeDtypeStruct((n, 256), F32),
        jax.ShapeDtypeStruct((n // LANES, 4 * ML_HEADS, LANES), F32),
    )
    x_specs = [tok(D_MODEL), tok(D_MODEL), const((1, D_MODEL)), const((1, D_MODEL))] if fused else [tok(D_MODEL)]
    out_specs = ((tok(256),) + tuple(regrouped(dil) for _, dil in DIL_PATTERNS)
                 + (tok(512), tok(256), tok(256), tok(256),
                    pl.BlockSpec((TA // LANES, 4 * ML_HEADS, LANES), lambda i: (i, 0, 0))))
    if fused:
        out_specs += (tok(D_MODEL),)
        out_shape += (jax.ShapeDtypeStruct((n, D_MODEL), F32),)
    return pl.pallas_call(
        functools.partial(_inproj_kernel, alpha=alpha), name="inproj_gmlp",
        grid=(n // TA,),
        in_specs=x_specs + [const((D_MODEL, A_COLS)), const((1, A_COLS)), const((1, MIX_W)),
                            const((GM_CHUNK, GM_GROUPS * GM_CHUNK)), const((GM_CHUNK, MIX_W))],
        out_specs=out_specs,
        out_shape=out_shape,
        scratch_shapes=[pltpu.VMEM((768 // LANES, TA, LANES), F32)],
        compiler_params=_cparams(1),
    )(*(x_in if fused else (x_in,)), w_cat, b_cat, lng, wscat, bsfull)


def _pack_inproj_weights(w_in, b_in):
    pad = lambda a: jnp.pad(a, ((0, 0), (0, LANES - 4 * ML_HEADS)))
    w_cat = jnp.concatenate([w_in[:, 0:2304], w_in[:, 2320:2576], pad(w_in[:, 2304:2320])], axis=1)
    b2 = b_in[None, :]
    b_cat = jnp.concatenate([b2[:, 0:2304], b2[:, 2320:2576], pad(b2[:, 2304:2320])], axis=1)
    return w_cat.astype(BF16), b_cat


def _pack_gmlp(gm_ws, gm_bs):
    wscat = jnp.transpose(gm_ws, (1, 0, 2)).reshape(GM_CHUNK, GM_GROUPS * GM_CHUNK).astype(BF16)
    bsfull = jnp.repeat(jnp.transpose(gm_bs), MIX_W // GM_GROUPS, axis=1)
    return wscat, bsfull


def _halo_specs(t, width, n_tiles):
    r = t // HALO
    main = pl.BlockSpec((1, t, width), lambda b, i: (b, i, 0))
    prev = pl.BlockSpec((1, HALO, width), lambda b, i: (b, jnp.maximum(i * r - 1, 0), 0))
    nxt = pl.BlockSpec((1, HALO, width), lambda b, i: (b, jnp.minimum((i + 1) * r, n_tiles * r - 1), 0))
    return main, prev, nxt


def _fill_halo_scratch(buf, x_ref, p_ref, n_ref, t):
    i = pl.program_id(1)
    last = pl.num_programs(1) - 1
    buf[0:HALO, :] = jnp.where(i > 0, p_ref[0], 0.0)
    buf[HALO:HALO + t, :] = x_ref[0]
    buf[HALO + t:2 * HALO + t, :] = jnp.where(i < last, n_ref[0], 0.0)


TP = 1024


def _pool_kernel(x_ref, p_ref, n_ref, w_ref, sc_ref, o_ref, buf, lvl):
    _fill_halo_scratch(buf, x_ref, p_ref, n_ref, TP)
    seq = pl.num_programs(1) * TP
    pos = pl.program_id(1) * TP + lax.broadcasted_iota(jnp.int32, (TP, 1), 0)
    lane_grp = lax.broadcasted_iota(jnp.int32, (TP, MIX_W), 1) // (MIX_W // len(POOL_WINDOWS))
    x0 = buf[HALO:HALO + TP, :]
    sums = []
    src, rows = buf, TP + 2 * HALO
    for k, win in enumerate(POOL_WINDOWS):
        half = win // 2
        rows -= half
        cur = src[0:rows, :] + src[half:rows + half, :] if k else buf[0:rows, :] + buf[1:rows + 1, :]
        if k + 1 < len(POOL_WINDOWS):
            lvl[k, 0:rows, :] = cur
            sums.append(lvl[k, HALO - half:HALO - half + TP, :])
            src = lvl.at[k]
        else:
            sums.append(cur[0:TP])
    pooled = jnp.zeros((TP, MIX_W), F32)
    for gi, win in enumerate(POOL_WINDOWS):
        half = win // 2
        cnt = (jnp.minimum(pos + half, seq) - jnp.maximum(pos - half, 0)).astype(F32)
        pooled = jnp.where(lane_grp == gi, sums[gi] / cnt - x0, pooled)
    mixed = jnp.dot(pooled.astype(BF16), w_ref[...], preferred_element_type=F32)
    o_ref[0] = (mixed * sc_ref[...]).astype(BF16)


def _pool_call(dx, w_block, scale):
    b, s, _ = dx.shape
    nt = s // TP
    main, prev, nxt = _halo_specs(TP, MIX_W, nt)
    return pl.pallas_call(
        _pool_kernel, name="pool_mixer",
        grid=(b, nt),
        in_specs=[main, prev, nxt,
                  pl.BlockSpec((MIX_W, MIX_W), lambda b, i: (0, 0)),
                  pl.BlockSpec((1, MIX_W), lambda b, i: (0, 0))],
        out_specs=pl.BlockSpec((1, TP, MIX_W), lambda b, i: (b, i, 0)),
        out_shape=jax.ShapeDtypeStruct((b, s, MIX_W), BF16),
        scratch_shapes=[pltpu.VMEM((TP + 2 * HALO, MIX_W), F32),
                        pltpu.VMEM((len(POOL_WINDOWS) - 1, TP + 2 * HALO, MIX_W), F32)],
        compiler_params=_cparams(2),
    )(dx, dx, dx, w_block, scale)


TQ = 128
TQS = 2048
TKEYS = TQ + 2 * ATT_BLOCK


def _attn_kernel(q_ref, kp_ref, km_ref, kn_ref, vp_ref, vm_ref, vn_ref, bias_ref, o_ref, lse_ref):
    for sq in range(q_ref.shape[1]):
        _attn_sequence(sq, q_ref, kp_ref, km_ref, kn_ref, vp_ref, vm_ref, vn_ref, bias_ref, o_ref, lse_ref)


def _attn_sequence(sq, q_ref, kp_ref, km_ref, kn_ref, vp_ref, vm_ref, vn_ref, bias_ref, o_ref, lse_ref):
    i = pl.program_id(2)
    q = q_ref[0, sq] * ATT_HD ** -0.5
    k = jnp.concatenate([kp_ref[0, sq], km_ref[0, sq], kn_ref[0, sq]], axis=0)
    v = jnp.concatenate([vp_ref[0, sq], vm_ref[0, sq], vn_ref[0, sq]], axis=0)
    lane = lax.broadcasted_iota(jnp.int32, (TQ, LANES), 1)
    lane_half = lax.broadcasted_iota(jnp.int32, (1, LANES), 1) // ATT_HD
    keep = [jnp.where(lane_half == hh, 1.0, 0.0).astype(BF16) for hh in range(2)]
    n_sub = q_ref.shape[2] // TQ
    last_step = pl.num_programs(2) - 1
    for j in range(n_sub):
        if j == 0:
            variant = jnp.where(i == 0, 0, 1)
        elif j == n_sub - 1:
            variant = jnp.where(i == last_step, 2, 1)
        else:
            variant = 1
        qrows = slice(j * TQ, (j + 1) * TQ)
        krows = slice(j * TQ, j * TQ + TKEYS)
        lse_tile = jnp.zeros((TQ, LANES), F32)
        for pair in range(ATT_HEADS // 2):
            grp = slice(pair * LANES, (pair + 1) * LANES)
            q_pair, k_pair, v_pair = q[qrows, grp], k[krows, grp], v[krows, grp]
            o_pair = jnp.zeros((TQ, LANES), F32)
            for hh in range(2):
                h = 2 * pair + hh
                logits = lax.dot_general(q_pair * keep[hh], k_pair, (((1,), (1,)), ((), ())),
                                         preferred_element_type=F32) + bias_ref[variant, h]
                m = jnp.max(logits, axis=-1, keepdims=True)
                p = jnp.exp(logits - m)
                ssum = jnp.sum(p, axis=-1, keepdims=True)
                o = jnp.dot(p.astype(BF16), v_pair, preferred_element_type=F32) / ssum
                o_pair = jnp.where(lane_half == hh, o, o_pair)
                lse_tile = jnp.where(lane == h, m + jnp.log(ssum), lse_tile)
            o_ref[0, sq, qrows, grp] = o_pair
        lse_ref[0, sq, qrows, :] = lse_tile


def _attn_call(qkv, bias):
    b, dil, l, _ = qkv.shape
    tqs = min(l, TQS)
    nsq = min(dil, TQS // tqs)
    nt = l // tqs
    r64 = tqs // ATT_BLOCK
    main = lambda c: pl.BlockSpec((1, nsq, tqs, MIX_W), lambda b, r, i: (b, r, i, c))
    prev = lambda c: pl.BlockSpec((1, nsq, ATT_BLOCK, MIX_W),
                                  lambda b, r, i: (b, r, jnp.maximum(i * r64 - 1, 0), c))
    nxt = lambda c: pl.BlockSpec((1, nsq, ATT_BLOCK, MIX_W),
                                 lambda b, r, i: (b, r, jnp.minimum((i + 1) * r64, nt * r64 - 1), c))
    return pl.pallas_call(
        _attn_kernel, name="band_attention",
        grid=(b, dil // nsq, nt),
        in_specs=[main(0), prev(1), main(1), nxt(1), prev(2), main(2), nxt(2),
                  pl.BlockSpec((3, ATT_HEADS, TQ, TKEYS), lambda b, r, i: (0, 0, 0, 0))],
        out_specs=(pl.BlockSpec((1, nsq, tqs, MIX_W), lambda b, r, i: (b, r, i, 0)),
                   pl.BlockSpec((1, nsq, tqs, LANES), lambda b, r, i: (b, r, i, 0))),
        out_shape=(jax.ShapeDtypeStruct((b, dil, l, MIX_W), F32),
                   jax.ShapeDtypeStruct((b, dil, l, LANES), F32)),
        compiler_params=_cparams(3),
    )(qkv, qkv, qkv, qkv, qkv, qkv, qkv, bias)


def _t5_bucket_static(rel):
    half = REL_BUCKETS // 2
    max_exact = half // 2
    ret = np.where(rel > 0, half, 0)
    n = np.abs(rel)
    nf = np.maximum(n, 1).astype(np.float32)
    large = max_exact + (np.log(nf / np.float32(max_exact)) / np.float32(math.log(REL_MAX_DIST / max_exact))
                         * np.float32(half - max_exact)).astype(np.int32)
    large = np.minimum(large, half - 1)
    return ret + np.where(n < max_exact, n, large)


def _attn_bias_tile(rel_bias, window, dil):
    side = (window // 2) // dil
    rel = np.arange(TKEYS)[None, :] - ATT_BLOCK - np.arange(TQ)[:, None]
    n_rel = TKEYS + TQ - 1
    rel_values = np.arange(n_rel) - (ATT_BLOCK + TQ - 1)
    onehot = jax.nn.one_hot(jnp.asarray(_t5_bucket_static(dil * rel_values), jnp.int32), REL_BUCKETS, dtype=F32)
    table = jnp.einsum('nr,rh->hn', onehot, rel_bias, precision=lax.Precision.HIGHEST)
    periodic = jnp.tile(jnp.pad(table, ((0, 0), (0, 1))), (1, TQ))[:, :TQ * n_rel]
    bias = periodic.reshape(ATT_HEADS, TQ, n_rel)[:, :, TQ - 1:]
    key = np.arange(TKEYS)[None, :]
    inside = np.abs(rel) <= side
    masks = np.stack([inside & (key >= ATT_BLOCK), inside, inside & (key < ATT_BLOCK + TQ)])
    return jnp.where(jnp.asarray(masks)[:, None], bias[None], NEG_BIG)


TM = 1024
VT_ROWS = ML_HD + 16


def _mlprep_kernel(x_ref, p_ref, n_ref, v_ref, w_ref, qt_out, k_out, vt_out, buf):
    _fill_halo_scratch(buf, x_ref, p_ref, n_ref, TM)
    conv = (buf[HALO - 1:HALO - 1 + TM, :] * w_ref[0:1, :] + buf[HALO:HALO + TM, :] * w_ref[1:2, :]
            + buf[HALO + 1:HALO + 1 + TM, :] * w_ref[2:3, :])
    qk = jax.nn.silu(conv)
    qt = jnp.transpose(qk[:, :MIX_W])
    vt = jnp.transpose(v_ref[0].astype(F32))
    ones_rows = jnp.where(lax.broadcasted_iota(jnp.int32, (VT_ROWS - ML_HD, LANES), 0) == 0, 1.0, 0.0)
    for h in range(ML_HEADS):
        sl = slice(h * ML_HD, (h + 1) * ML_HD)
        k_out[0, h] = (qk[:, MIX_W + h * ML_HD:MIX_W + (h + 1) * ML_HD] * ML_HD ** -0.5).astype(BF16)
        for p in range(TM // LANES):
            pl_ = slice(p * LANES, (p + 1) * LANES)
            qt_out[0, h, p] = qt[sl, pl_].astype(BF16)
            vt_out[0, h, p] = jnp.concatenate([vt[sl, pl_], ones_rows], axis=0).astype(BF16)


def _mlprep_call(cqk, cv, conv_w):
    b, s, _ = cqk.shape
    nt = s // TM
    n_pairs = s // LANES
    ppt = TM // LANES
    main, prev, nxt = _halo_specs(TM, 2 * MIX_W, nt)
    return pl.pallas_call(
        _mlprep_kernel, name="mlstm_prep",
        grid=(b, nt),
        in_specs=[main, prev, nxt,
                  pl.BlockSpec((1, TM, MIX_W), lambda b, i: (b, i, 0)),
                  pl.BlockSpec((3, 2 * MIX_W), lambda b, i: (0, 0))],
        out_specs=(pl.BlockSpec((1, ML_HEADS, ppt, ML_HD, LANES), lambda b, i: (b, 0, i, 0, 0)),
                   pl.BlockSpec((1, ML_HEADS, TM, ML_HD), lambda b, i: (b, 0, i, 0)),
                   pl.BlockSpec((1, ML_HEADS, ppt, VT_ROWS, LANES), lambda b, i: (b, 0, i, 0, 0))),
        out_shape=(jax.ShapeDtypeStruct((b, ML_HEADS, n_pairs, ML_HD, LANES), BF16),
                   jax.ShapeDtypeStruct((b, ML_HEADS, s, ML_HD), BF16),
                   jax.ShapeDtypeStruct((b, ML_HEADS, n_pairs, VT_ROWS, LANES), BF16)),
        scratch_shapes=[pltpu.VMEM((TM + 2 * HALO, 2 * MIX_W), F32)],
        compiler_params=_cparams(2),
    )(cqk, cqk, cqk, cv, conv_w)


def _gate_kernel(g_ref, fb_ref, a_ref, m_ref, iw_ref, en_ref, ws_ref, dec_ref):
    x = g_ref[...]
    n_pairs = x.shape[0]
    n_ch = 2 * ML_HEADS
    lane = lax.broadcasted_iota(jnp.int32, (1, 1, LANES), 2)
    t_in = lane % ML_CHUNK
    second = lane >= ML_CHUNK
    fwd_row = lax.broadcasted_iota(jnp.int32, (1, n_ch, 1), 1) < ML_HEADS
    li = x[:, 0:n_ch, :]
    z = x[:, n_ch:2 * n_ch, :] + fb_ref[...]
    lf = jnp.minimum(z, 0.0) - jnp.log1p(jnp.exp(-jnp.abs(z)))

    def within_chunk(v, op, ident, prefix):
        s = 1
        while s < ML_CHUNK:
            nb = pltpu.roll(v, s if prefix else LANES - s, 2)
            ok = (t_in >= s) if prefix else (t_in < ML_CHUNK - s)
            v = op(v, jnp.where(ok, nb, ident))
            s *= 2
        return v

    pre = within_chunk(lf, jnp.add, 0.0, True)
    suf = within_chunk(lf, jnp.add, 0.0, False)
    g = pre + suf - lf
    b = jnp.where(fwd_row, pre, suf)
    a = li - b
    cm_pre = within_chunk(a, jnp.maximum, -jnp.inf, True)
    cm_suf = within_chunk(a, jnp.maximum, -jnp.inf, False)
    cm = jnp.where(fwd_row, cm_pre, cm_suf)
    amax = jnp.maximum(cm_pre, cm_suf)

    def shift_pairs(v, k, fill):
        pad = jnp.full((abs(k),) + v.shape[1:], fill, F32)
        return (jnp.concatenate([pad, v[:n_pairs - k]], axis=0) if k > 0
                else jnp.concatenate([v[-k:], pad], axis=0))

    def from_chunk(v, dist, fill, forward):
        if dist == 1:
            y = pltpu.roll(v, ML_CHUNK, 2)
            if forward:
                return jnp.where(second, y, shift_pairs(y, 1, fill))
            return jnp.where(second, shift_pairs(y, -1, fill), y)
        return shift_pairs(v, dist // 2 if forward else -(dist // 2), fill)

    def running_stabiliser(forward):
        big_g, big_a = g, amax + g
        dist = 1
        while dist < 2 * n_pairs:
            gp = from_chunk(big_g, dist, 0.0, forward)
            ap = from_chunk(big_a, dist, -jnp.inf, forward)
            big_g, big_a = gp + big_g, jnp.maximum(ap + big_g, big_a)
            dist *= 2
        ge = from_chunk(big_g, 1, 0.0, forward)
        ae = from_chunk(big_a, 1, -jnp.inf, forward)
        return jnp.maximum(ge, ae)

    m_chunk = jnp.where(fwd_row, running_stabiliser(True), running_stabiliser(False))
    m_t = jnp.maximum(cm, m_chunk)
    m_last = jnp.maximum(amax, m_chunk)
    a_ref[0] = a
    m_ref[0] = m_t
    iw_ref[0] = jnp.exp(m_chunk - m_t)
    en_ref[0] = jnp.exp(-(b + m_t))
    ws_ref[0] = jnp.exp(a - m_last)
    dec_ref[0] = jnp.exp(m_chunk - m_last)


def _gate_call(gates_t, fbias_col, batch):
    n_pairs = gates_t.shape[0] // batch
    n_ch = 2 * ML_HEADS
    out = pl.BlockSpec((1, n_pairs, n_ch, LANES), lambda i: (i, 0, 0, 0))
    return pl.pallas_call(
        _gate_kernel, name="mlstm_gates",
        grid=(batch,),
        in_specs=[pl.BlockSpec((n_pairs, 2 * n_ch, LANES), lambda i: (i, 0, 0)),
                  pl.BlockSpec((n_ch, 1), lambda i: (0, 0))],
        out_specs=(out,) * 6,
        out_shape=(jax.ShapeDtypeStruct((batch, n_pairs, n_ch, LANES), F32),) * 6,
        compiler_params=_cparams(1),
    )(gates_t, fbias_col)


TE = 2048


def _mlstm_kernel(*refs):
    fwd, bwd, (hf_ref, hb_ref, state) = refs[:9], refs[9:18], refs[18:]
    i = pl.program_id(1)

    @pl.when(i == 0)
    def _():
        state[...] = jnp.zeros(state.shape, F32)

    n_pairs = TE // LANES
    s_idx = lax.broadcasted_iota(jnp.int32, (LANES, LANES), 0)
    t_idx = lax.broadcasted_iota(jnp.int32, (LANES, LANES), 1)
    same_chunk = (s_idx >= ML_CHUNK) == (t_idx >= ML_CHUNK)
    upper_lanes = lax.broadcasted_iota(jnp.int32, (1, LANES), 1) >= ML_CHUNK

    def pair_body(p, carry):
        jobs = []
        for d, ((qt_r, k_r, vt_r, a_r, m_r, iw_r, en_r, ws_r, dec_r), out_r) in enumerate(
                ((fwd, hf_ref), (bwd, hb_ref))):
            pp = p if d == 0 else n_pairs - 1 - p
            srows = pl.ds(pl.multiple_of(pp * LANES, LANES), LANES)
            a_t = jnp.transpose(jnp.concatenate(
                [a_r[0, pp], jnp.zeros((LANES - 2 * ML_HEADS, LANES), F32)], axis=0))
            m_t, iw_t, en_t, ws_t, dec_t = [r[0, pp] for r in (m_r, iw_r, en_r, ws_r, dec_r)]
            dec_lo, dec_hi = dec_t[:, :ML_CHUNK], pltpu.roll(dec_t, ML_CHUNK, 1)[:, :ML_CHUNK]
            in_first = upper_lanes if d else ~upper_lanes
            tri = same_chunk & ((s_idx >= t_idx) if d else (s_idx <= t_idx))
            for h in range(ML_HEADS):
                ch = d * ML_HEADS + h
                row = lambda t: t[ch:ch + 1]
                jobs.append(dict(
                    ch=ch, tri=tri, in_first=in_first, k=k_r[0, h, srows, :], qt=qt_r[0, h, pp],
                    vt=vt_r[0, h, pp], a=a_t[:, ch:ch + 1], m=row(m_t), iw=row(iw_t), en=row(en_t), ws=row(ws_t),
                    dec_first=row(dec_hi if d else dec_lo), dec_second=row(dec_lo if d else dec_hi),
                    out=(out_r, pp, h)))
        for j in jobs:
            vt_f = j["vt"].astype(F32)
            j["c0"] = state[j["ch"]]
            j["st"] = jnp.dot(j["k"], j["qt"], preferred_element_type=F32)
            j["inter1"] = jnp.dot(j["c0"].astype(BF16), j["qt"], preferred_element_type=F32)
            j["upd1"] = jnp.dot((vt_f * jnp.where(j["in_first"], j["ws"], 0.0)).astype(BF16), j["k"],
                                preferred_element_type=F32)
            j["upd2"] = jnp.dot((vt_f * jnp.where(j["in_first"], 0.0, j["ws"])).astype(BF16), j["k"],
                                preferred_element_type=F32)
        for j in jobs:
            j["swt"] = j["st"] * jnp.exp(jnp.where(j["tri"], j["a"] - j["m"], NEG_BIG))
            j["intra"] = jnp.dot(j["vt"], j["swt"].astype(BF16), preferred_element_type=F32)
            j["c1"] = j["dec_first"] * j["c0"] + j["upd1"]
            j["inter2"] = jnp.dot(j["c1"].astype(BF16), j["qt"], preferred_element_type=F32)
        for j in jobs:
            inter = jnp.where(j["in_first"], j["inter1"], j["inter2"])
            den = jnp.sum(j["swt"], axis=0, keepdims=True) + j["iw"] * inter[ML_HD:ML_HD + 1]
            tot = j["intra"][:ML_HD] + j["iw"] * inter[:ML_HD]
            out_r, pp, h = j["out"]
            out_r[0, pp, h * ML_HD:(h + 1) * ML_HD, :] = tot / jnp.maximum(jnp.abs(den), j["en"])
            state[j["ch"]] = j["dec_second"] * j["c1"] + j["upd2"]
        return carry

    lax.fori_loop(0, n_pairs, pair_body, 0)


def _mlstm_call(qt, k, vt, a_t, m_t, iw_t, en_t, ws_t, dec_t):
    b, _, s, _ = k.shape
    nt = s // TE
    ppt = TE // LANES

    def specs(rev):
        ti = (lambda i: nt - 1 - i) if rev else (lambda i: i)
        tile = pl.BlockSpec((1, ppt, 2 * ML_HEADS, LANES), lambda b, i: (b, ti(i), 0, 0))
        return [
            pl.BlockSpec((1, ML_HEADS, ppt, ML_HD, LANES), lambda b, i: (b, 0, ti(i), 0, 0)),
            pl.BlockSpec((1, ML_HEADS, TE, ML_HD), lambda b, i: (b, 0, ti(i), 0)),
            pl.BlockSpec((1, ML_HEADS, ppt, VT_ROWS, LANES), lambda b, i: (b, 0, ti(i), 0, 0)),
            tile, tile, tile, tile, tile, tile]

    args = [qt, k, vt, a_t, m_t, iw_t, en_t, ws_t, dec_t]
    out_f = pl.BlockSpec((1, ppt, MIX_W, LANES), lambda b, i: (b, i, 0, 0))
    out_b = pl.BlockSpec((1, ppt, MIX_W, LANES), lambda b, i: (b, nt - 1 - i, 0, 0))
    return pl.pallas_call(
        _mlstm_kernel, name="mlstm_scan",
        grid=(b, nt),
        in_specs=specs(False) + specs(True),
        out_specs=(out_f, out_b),
        out_shape=(jax.ShapeDtypeStruct((b, s // LANES, MIX_W, LANES), F32),) * 2,
        scratch_shapes=[pltpu.VMEM((2 * ML_HEADS, VT_ROWS, ML_HD), F32)],
        compiler_params=_cparams(2),
    )(*args, *args)


def _mlstm_branch(cqk, cv, gates_t, conv_w, fbias):
    qt, k, vt = _mlprep_call(cqk, cv, conv_w)
    factors = _gate_call(gates_t, fbias.reshape(2 * ML_HEADS, 1), cqk.shape[0])
    return _mlstm_call(qt, k, vt, *factors)


TF = 512


def _merge_kernel(x_ref, ya_ref, o1_ref, o2_ref, o3_ref, l1_ref, l2_ref, l3_ref, hf_ref, hb_ref,
                  co_ref, yd_ref, wg_ref, bg_ref, wbr_ref, wout_ref, mng_ref, lng_ref, lnb_ref,
                  wr_ref, x1_ref, x1p_ref, aff_ref, o_scr, l_scr, *, alpha):
    x = x_ref[0]
    xb = x.astype(BF16)

    def natural_order(src_ref, scr):
        dil, width = src_ref.shape[1], src_ref.shape[3]
        if dil == 1:
            return src_ref[0, 0]
        for r in range(dil):
            for c in range(width // LANES):
                scr[c, pl.ds(r, TF // dil, stride=dil), :] = src_ref[0, r, :, c * LANES:(c + 1) * LANES]
        return jnp.concatenate([scr[c] for c in range(width // LANES)], axis=1)

    lane_head = lax.broadcasted_iota(jnp.int32, (TF, MIX_W), 1) // ML_HD
    l1, l2, l3 = [natural_order(r, l_scr.at[p]) for p, r in enumerate((l1_ref, l2_ref, l3_ref))]
    o1, o2, o3 = [natural_order(r, o_scr.at[p]) for p, r in enumerate((o1_ref, o2_ref, o3_ref))]
    lm = jnp.maximum(jnp.maximum(l1, l2), l3)
    e1, e2, e3 = jnp.exp(l1 - lm), jnp.exp(l2 - lm), jnp.exp(l3 - lm)
    inv = 1.0 / (e1 + e2 + e3)

    def per_head(w):
        out = jnp.zeros((TF, MIX_W), F32)
        for h in range(ATT_HEADS):
            out = jnp.where(lane_head == h, w[:, h:h + 1], out)
        return out

    y_b = per_head(e1 * inv) * o1 + per_head(e2 * inv) * o2 + per_head(e3 * inv) * o3
    hsum_t = jnp.concatenate([hf_ref[0, p] + hb_ref[0, p] for p in range(TF // LANES)], axis=1)
    per_head_rows = hsum_t.reshape(ML_HEADS, ML_HD, TF)
    mu = jnp.mean(per_head_rows, axis=1, keepdims=True)
    cen = per_head_rows - mu
    var = jnp.mean(cen * cen, axis=1, keepdims=True)
    hn_t = (cen * lax.rsqrt(var + LN_EPS)).reshape(MIX_W, TF)
    y_c_t = (jax.nn.sigmoid(jnp.transpose(co_ref[0])) * (hn_t * mng_ref[...])).astype(BF16)
    ys = (ya_ref[0], y_b.astype(BF16), None, yd_ref[0])
    merged = jnp.zeros((TF, D_MODEL), F32)
    for n in range(N_BRANCH):
        cols = slice(n * D_MODEL, (n + 1) * D_MODEL)
        gate = jax.nn.sigmoid(jnp.dot(xb, wg_ref[:, cols], preferred_element_type=F32) + bg_ref[:, cols])
        if ys[n] is None:
            proj = lax.dot_general(y_c_t, wbr_ref[n], (((0,), (0,)), ((), ())), preferred_element_type=F32)
        else:
            proj = jnp.dot(ys[n], wbr_ref[n], preferred_element_type=F32)
        merged = merged + gate * proj
    mix = jnp.dot(merged.astype(BF16), wout_ref[...], preferred_element_type=F32)
    x1 = _standardize(alpha * x + mix) * lng_ref[...] + lnb_ref[...]
    x1_ref[0] = x1
    x1b = x1.astype(BF16)
    x1p_ref[0] = _pack_bf16_pair(x1b[:, :D_MODEL // 2], x1b[:, D_MODEL // 2:])
    logits = lax.dot_general(wr_ref[...], x1b, (((1,), (1,)), ((), ())),
                             preferred_element_type=F32)
    ex = jnp.exp(logits - jnp.max(logits, axis=0, keepdims=True))
    aff_ref[0] = ex / jnp.sum(ex, axis=0, keepdims=True)


def _merge_call(x, ya, o_list, l_list, hf, hb, co, yd, wg, bg, wbr, wout, mng, lng, lnb, wr_t, alpha):
    b, s, _ = x.shape
    tok = lambda w: pl.BlockSpec((1, TF, w), lambda b, i: (b, i, 0))
    grouped = lambda dil, w: pl.BlockSpec((1, dil, TF // dil, w), lambda b, i: (b, 0, i, 0))
    chunked = pl.BlockSpec((1, TF // LANES, MIX_W, LANES), lambda b, i: (b, i, 0, 0))
    const = lambda shp: pl.BlockSpec(shp, lambda b, i: (0,) * len(shp))
    return pl.pallas_call(
        functools.partial(_merge_kernel, alpha=alpha), name="merge_ln_router",
        grid=(b, s // TF),
        in_specs=[tok(D_MODEL), tok(MIX_W)] + [grouped(dil, MIX_W) for _, dil in DIL_PATTERNS]
                 + [grouped(dil, LANES) for _, dil in DIL_PATTERNS]
                 + [chunked, chunked, tok(MIX_W), tok(MIX_W)]
                 + [const((D_MODEL, N_BRANCH * D_MODEL)), const((1, N_BRANCH * D_MODEL)),
                    const((N_BRANCH, MIX_W, D_MODEL)), const((D_MODEL, D_MODEL)), const((MIX_W, 1)),
                    const((1, D_MODEL)), const((1, D_MODEL)), const((N_EXPERTS, D_MODEL))],
        out_specs=(tok(D_MODEL), tok(D_MODEL // 2), pl.BlockSpec((1, N_EXPERTS, TF), lambda b, i: (b, 0, i))),
        out_shape=(jax.ShapeDtypeStruct((b, s, D_MODEL), F32),
                   jax.ShapeDtypeStruct((b, s, D_MODEL // 2), jnp.int32),
                   jax.ShapeDtypeStruct((b, N_EXPERTS, s), F32)),
        scratch_shapes=[pltpu.VMEM((len(DIL_PATTERNS), MIX_W // LANES, TF, LANES), F32),
                        pltpu.VMEM((len(DIL_PATTERNS), 1, TF, LANES), F32)],
        compiler_params=_cparams(2),
    )(x, ya, *o_list, *l_list, hf, hb, co, yd, wg, bg, wbr, wout, mng, lng, lnb, wr_t)


TT = 256


def _select_kernel(aff_ref, slot_ref, *, cap):
    s = aff_ref.shape[2]
    bits = pltpu.bitcast(aff_ref[0], jnp.int32)

    def bit_step(i, thr):
        cand = thr | jnp.left_shift(jnp.int32(1), 30 - i)
        cnt = jnp.sum((bits >= cand).astype(jnp.int32), axis=1, keepdims=True)
        return jnp.where(cnt >= cap, cand, thr)

    thr = lax.fori_loop(0, 31, bit_step, jnp.zeros((N_EXPERTS, 1), jnp.int32))
    gt = bits > thr
    eq = bits == thr
    need = (cap - jnp.sum(gt.astype(jnp.int32), axis=1, keepdims=True)).astype(F32)
    upper = (lax.broadcasted_iota(jnp.int32, (TT, TT), 0)
             <= lax.broadcasted_iota(jnp.int32, (TT, TT), 1)).astype(BF16)
    eq_before = jnp.zeros((N_EXPERTS, 1), F32)
    sel_before = jnp.zeros((N_EXPERTS, 1), F32)
    for j in range(s // TT):
        cols = slice(j * TT, (j + 1) * TT)
        eq_j = eq[:, cols]
        eq_incl = eq_before + jnp.dot(eq_j.astype(BF16), upper, preferred_element_type=F32)
        sel_j = gt[:, cols] | (eq_j & (eq_incl <= need))
        sel_f = sel_j.astype(F32)
        sel_incl = sel_before + jnp.dot(sel_f.astype(BF16), upper, preferred_element_type=F32)
        slot_ref[0, :, cols] = jnp.where(sel_j, sel_incl - 1.0, -1.0).astype(jnp.int32)
        eq_before = eq_incl[:, TT - 1:TT]
        sel_before = sel_incl[:, TT - 1:TT]


def _select_call(aff_t, cap):
    b, e, s = aff_t.shape
    return pl.pallas_call(
        functools.partial(_select_kernel, cap=cap), name="expert_choice_select",
        grid=(b,),
        in_specs=[pl.BlockSpec((1, e, s), lambda i: (i, 0, 0))],
        out_specs=pl.BlockSpec((1, e, s), lambda i: (i, 0, 0)),
        out_shape=jax.ShapeDtypeStruct((b, e, s), jnp.int32),
        compiler_params=_cparams(1),
    )(aff_t)


SC_LANES = 16
SC_ROWS = 64
SC_IDX = 128
SC_SLAB = 128
SC_ZROWS = 64
CF = 1024


def _sc_dispatch_call(x_flat, slot2, aff2, seq, cap):
    n_pairs = slot2.shape[0]
    d = x_flat.shape[1]
    info = plsc.get_sparse_core_info()
    n_workers = info.num_cores * info.num_subcores
    assert n_pairs % n_workers == 0 and seq % SC_LANES == 0 and cap % (2 * SC_ROWS) == 0
    pairs_per_worker = n_pairs // n_workers
    mesh = plsc.VectorSubcoreMesh(core_axis_name="c", subcore_axis_name="s")

    @functools.partial(
        pl.kernel, mesh=mesh, name="expert_dispatch_sc",
        compiler_params=pltpu.CompilerParams(needs_layout_passes=False),
        out_type=(jax.ShapeDtypeStruct((n_pairs * cap, d), x_flat.dtype),
                  jax.ShapeDtypeStruct((n_pairs, cap // SC_IDX, SC_IDX), jnp.int32),
                  jax.ShapeDtypeStruct((n_pairs, cap // SC_IDX, SC_IDX), F32)),
        scratch_types=[pltpu.VMEM((seq,), jnp.int32), pltpu.VMEM((seq,), F32),
                       pltpu.VMEM((cap,), jnp.int32), pltpu.VMEM((cap // SC_IDX, SC_IDX), jnp.int32),
                       pltpu.VMEM((cap // SC_IDX, SC_IDX), F32),
                       pltpu.VMEM((2, SC_ROWS, d), x_flat.dtype), pltpu.SemaphoreType.DMA((2,))])
    def dispatch(x_hbm, slot_hbm, aff_hbm, xs_hbm, tok_hbm, gate_hbm,
                 slot_v, aff_v, idx_v, tok_v, gate_v, rows_v, sem):
        worker = lax.axis_index("s") * info.num_cores + lax.axis_index("c")
        lane = lax.iota(jnp.int32, SC_LANES)

        def gather(c0, buf):
            return pltpu.make_async_copy(x_hbm.at[idx_v.at[pl.ds(c0, SC_ROWS)]], rows_v.at[buf], sem.at[buf])
        for k in range(pairs_per_worker):
            pair = worker * pairs_per_worker + k
            row0 = (pair // N_EXPERTS) * seq
            pltpu.sync_copy(slot_hbm.at[pair], slot_v)
            pltpu.sync_copy(aff_hbm.at[pair], aff_v)

            @plsc.parallel_loop(0, seq, step=SC_LANES, unroll=4)
            def _(t0):
                sv = slot_v[pl.ds(t0, SC_LANES)]
                picked = sv >= 0
                hi, lo = lax.shift_right_logical(sv, 7), sv & (SC_IDX - 1)
                plsc.store_scatter(tok_v, [hi, lo], t0 + lane, mask=picked)
                plsc.store_scatter(idx_v, [sv], row0 + t0 + lane, mask=picked)
                plsc.store_scatter(gate_v, [hi, lo], aff_v[pl.ds(t0, SC_LANES)], mask=picked)

            pltpu.sync_copy(tok_v, tok_hbm.at[pair])
            pltpu.sync_copy(gate_v, gate_hbm.at[pair])

            gather(0, 0).start()

            @pl.loop(0, cap, step=2 * SC_ROWS)
            def _(c0):
                gather(c0 + SC_ROWS, 1).start()
                gather(c0, 0).wait()
                pltpu.sync_copy(rows_v.at[0], xs_hbm.at[pl.ds(pair * cap + c0, SC_ROWS)])

                @pl.when(c0 + 2 * SC_ROWS < cap)
                def _():
                    gather(c0 + 2 * SC_ROWS, 0).start()

                gather(c0 + SC_ROWS, 1).wait()
                pltpu.sync_copy(rows_v.at[1], xs_hbm.at[pl.ds(pair * cap + c0 + SC_ROWS, SC_ROWS)])

    return dispatch(x_flat, slot2, aff2)


def _expert_kernel(xs_ref, g_ref, w1_ref, w3_ref, w2_ref, ye_ref, w1_bf, w3_bf, w2_bf):
    @pl.when((pl.program_id(1) == 0) & (pl.program_id(2) == 0))
    def _():
        w1_bf[...] = w1_ref[0, 0].astype(BF16)
        w3_bf[...] = w3_ref[0, 0].astype(BF16)
        w2_bf[...] = w2_ref[0, 0].astype(BF16)

    xs = jnp.concatenate(_unpack_bf16_pair(xs_ref[0, 0]), axis=1)
    hid = (jax.nn.silu(jnp.dot(xs, w1_bf[...], preferred_element_type=F32))
           * jnp.dot(xs, w3_bf[...], preferred_element_type=F32))
    g_rows = g_ref[0, 0]
    n_rows = g_rows.shape[0]
    g_t = jnp.transpose(jnp.concatenate([g_rows, jnp.zeros((LANES - n_rows, LANES), F32)], axis=0))
    g_col = jnp.concatenate([g_t[:, r:r + 1] for r in range(n_rows)], axis=0)
    ye_ref[0, 0] = jnp.dot(hid.astype(BF16), w2_bf[...], preferred_element_type=F32) * g_col


def _expert_call(xs4, gate4, w1, w3, w2, layer):
    b, e, cap, half = xs4.shape
    d, ff = w1.shape[2], w1.shape[3]
    assert d == 2 * half
    rows = lambda w: pl.BlockSpec((1, 1, CF, w), lambda e, b, j: (b, e, j, 0))
    wspec = lambda r, c: pl.BlockSpec((1, 1, r, c), lambda e, b, j: (layer, e, 0, 0))
    return pl.pallas_call(
        _expert_kernel, name="expert_ffn",
        grid=(e, b, cap // CF),
        in_specs=[rows(half), pl.BlockSpec((1, 1, CF // LANES, LANES), lambda e, b, j: (b, e, j, 0)),
                  wspec(d, ff), wspec(d, ff), wspec(ff, d)],
        out_specs=rows(d),
        out_shape=jax.ShapeDtypeStruct((b, e, cap, d), F32),
        scratch_shapes=[pltpu.VMEM((d, ff), BF16), pltpu.VMEM((d, ff), BF16), pltpu.VMEM((ff, d), BF16)],
        compiler_params=_cparams(3),
    )(xs4, gate4, w1, w3, w2)


def _sc_combine_call(ye_flat, tok3, seq):
    n_pairs, n_chunks, _ = tok3.shape
    cap = n_chunks * SC_IDX
    d = ye_flat.shape[1]
    nb = n_pairs // N_EXPERTS
    info = plsc.get_sparse_core_info()
    assert info.num_subcores == N_EXPERTS and nb % info.num_cores == 0 and n_chunks % 2 == 0
    assert seq % (info.num_subcores * SC_ZROWS) == 0 and d % SC_SLAB == 0
    batches_per_core = nb // info.num_cores
    own_rows = seq // info.num_subcores
    mesh = plsc.VectorSubcoreMesh(core_axis_name="c", subcore_axis_name="s")

    @functools.partial(
        pl.kernel, mesh=mesh, name="expert_combine_sc",
        compiler_params=pltpu.CompilerParams(needs_layout_passes=False),
        out_type=jax.ShapeDtypeStruct((nb * seq, d), F32),
        scratch_types=[pltpu.VMEM_SHARED((seq, SC_SLAB), F32),
                       pltpu.VMEM((n_chunks, SC_IDX), jnp.int32),
                       pltpu.VMEM((2, SC_IDX, SC_SLAB), F32),
                       pltpu.VMEM((SC_ZROWS, SC_SLAB), F32),
                       pltpu.SemaphoreType.DMA((2,))])
    def combine(ye_hbm, tok_hbm, out_hbm, acc_sh, tok_v, rows_v, zero_v, sem):
        core = lax.axis_index("c")
        sub = lax.axis_index("s")

        @pl.loop(0, SC_ZROWS)
        def _(r):
            for l0 in range(0, SC_SLAB, SC_LANES):
                zero_v[r, pl.ds(l0, SC_LANES)] = jnp.zeros((SC_LANES,), F32)

        for bb in range(batches_per_core):
            batch = core * batches_per_core + bb
            pair = batch * N_EXPERTS + sub
            pltpu.sync_copy(tok_hbm.at[pair], tok_v)

            @pl.loop(0, d // SC_SLAB)
            def _(slab):
                cols = pl.ds(pl.multiple_of(slab * SC_SLAB, SC_SLAB), SC_SLAB)

                @pl.loop(0, own_rows, step=SC_ZROWS)
                def _(r0):
                    pltpu.sync_copy(zero_v, acc_sh.at[pl.ds(sub * own_rows + r0, SC_ZROWS)])

                def load(j, buf):
                    return pltpu.make_async_copy(
                        ye_hbm.at[pl.ds(pair * cap + j * SC_IDX, SC_IDX), cols], rows_v.at[buf], sem.at[buf])

                load(0, 0).start()
                plsc.subcore_barrier()

                for j in range(0, n_chunks, 2):
                    load(j + 1, 1).start()
                    load(j, 0).wait()
                    pltpu.sync_copy(rows_v.at[0], acc_sh.at[tok_v.at[j]], add=True)
                    if j + 2 < n_chunks:
                        load(j + 2, 0).start()
                    load(j + 1, 1).wait()
                    pltpu.sync_copy(rows_v.at[1], acc_sh.at[tok_v.at[j + 1]], add=True)

                plsc.subcore_barrier()
                pltpu.sync_copy(acc_sh.at[pl.ds(sub * own_rows, own_rows)],
                                out_hbm.at[pl.ds(batch * seq + sub * own_rows, own_rows), cols])

    return combine(ye_flat, tok3)


TN = 1024


def _resln_kernel(x_ref, y_ref, g_ref, b_ref, o_ref, *, alpha):
    o_ref[...] = _standardize(alpha * x_ref[...] + y_ref[...]) * g_ref[...] + b_ref[...]


def _resln_call(x2d, y2d, g, bta, alpha):
    n, d = x2d.shape
    tok = pl.BlockSpec((TN, d), lambda i: (i, 0))
    vec = pl.BlockSpec((1, d), lambda i: (0, 0))
    return pl.pallas_call(
        functools.partial(_resln_kernel, alpha=alpha), name="residual_layernorm",
        grid=(n // TN,), in_specs=[tok, tok, vec, vec], out_specs=tok,
        out_shape=jax.ShapeDtypeStruct((n, d), F32),
        compiler_params=_cparams(1),
    )(x2d, y2d, g, bta)


def _expert_choice_ffn(x1p, aff_t, w1, w3, w2, layer):
    b, s, half = x1p.shape
    d = 2 * half
    cap = EC_FACTOR * s // N_EXPERTS
    slot = _select_call(aff_t, cap)
    xs, tok, gate = _sc_dispatch_call(x1p.reshape(b * s, half), slot.reshape(b * N_EXPERTS, s),
                                      aff_t.reshape(b * N_EXPERTS, s), s, cap)
    ye = _expert_call(xs.reshape(b, N_EXPERTS, cap, half), gate.reshape(b, N_EXPERTS, cap // SC_IDX, SC_IDX), w1, w3, w2,
                      layer)
    out = _sc_combine_call(ye.reshape(b * N_EXPERTS * cap, d),
                           tok, s)
    return out.reshape(b, s, d)


def _pack_pool(pool_w):
    g, gd, _ = pool_w.shape
    out = jnp.zeros((g * gd, g * gd), F32)
    for i in range(g):
        out = out.at[i * gd:(i + 1) * gd, i * gd:(i + 1) * gd].set(pool_w[i])
    return out.astype(BF16)


def _layer(layer, x, pending, alpha, bias_tiles, w_in, b_in, gm_ln_g, gm_ws, gm_bs, ml_conv, ml_fbias,
           ml_norm_g, pool_w, pool_scale, w_branch, w_out, ln1_g, ln1_b, w_router, w_e1, w_e3, w_e2):
    b, s, d = x.shape
    n_small = 2576
    w_cat, b_cat = _pack_inproj_weights(w_in, b_in)
    wscat, bsfull = _pack_gmlp(gm_ws, gm_bs)
    if pending is None:
        outs = _inproj_call(x.reshape(b * s, d), w_cat, b_cat, gm_ln_g[None], wscat, bsfull, b)
    else:
        *outs, x2 = _inproj_call(pending, w_cat, b_cat, gm_ln_g[None], wscat, bsfull, b, alpha)
        x = x2.reshape(b, s, d)
    ya, qkv1, qkv4, qkv16, cqk, cv, co, dx, gates_t = outs
    r3 = lambda t: t.reshape(b, s, t.shape[-1])
    o_list, l_list = [], []
    for qkv, bias in zip((qkv1, qkv4, qkv16), bias_tiles):
        o, lse = _attn_call(qkv, bias)
        o_list.append(o)
        l_list.append(lse)
    hf, hb = _mlstm_branch(r3(cqk), r3(cv), gates_t, ml_conv, ml_fbias)
    yd = _pool_call(r3(dx), _pack_pool(pool_w), pool_scale[None])
    x1, x1p, aff_t = _merge_call(
        x, r3(ya), o_list, l_list, hf, hb, r3(co), yd,
        w_in[:, n_small:].astype(BF16), b_in[None, n_small:], w_branch.astype(BF16), w_out.astype(BF16),
        ml_norm_g[:, None], ln1_g[None], ln1_b[None], jnp.transpose(w_router).astype(BF16), alpha)
    ffn = _expert_choice_ffn(x1p, aff_t, w_e1, w_e3, w_e2, layer)
    return x1.reshape(b * s, d), ffn.reshape(b * s, d)


def kernel(x, w_in, b_in, gm_ln_g, gm_ws, gm_bs, rel_bias, ml_conv, ml_fbias, ml_norm_g, pool_w,
           pool_scale, w_branch, w_out, ln1_g, ln1_b, w_router, w_e1, w_e3, w_e2, ln2_g, ln2_b):
    depth = w_in.shape[0]
    alpha = (2 * depth) ** 0.25
    bias_tiles = [_attn_bias_tile(rel_bias, window, dil) for window, dil in DIL_PATTERNS]
    b, s, d = x.shape
    pending = None
    for l in range(depth):
        x1, ffn = _layer(l, x, pending, alpha, bias_tiles, w_in[l], b_in[l], gm_ln_g[l], gm_ws[l], gm_bs[l],
                         ml_conv[l], ml_fbias[l], ml_norm_g[l], pool_w[l], pool_scale[l], w_branch[l],
                         w_out[l], ln1_g[l], ln1_b[l], w_router[l], w_e1, w_e3, w_e2)
        pending = (x1, ffn, ln2_g[l][None], ln2_b[l][None])
    return _resln_call(*pending, alpha).reshape(b, s, d)
```

```python
import functools
import math

import jax
import jax.numpy as jnp
import numpy as np
from jax import lax
from jax.experimental import pallas as pl
from jax.experimental.pallas import tpu as pltpu
from jax.experimental.pallas import tpu_sc as plsc

F32 = jnp.float32
BF16 = jnp.bfloat16

D_MODEL = 1024
MIX_W = 256
N_BRANCH = 4
GM_CHUNK = 128
GM_GROUPS = 4
ATT_HEADS = 4
ATT_HD = 64
DIL_PATTERNS = ((128, 1), (512, 4), (2048, 16))
ATT_BLOCK = 64
REL_BUCKETS = 32
REL_MAX_DIST = 1024
ML_HEADS = 4
ML_HD = 64
ML_CHUNK = 64
POOL_WINDOWS = (2, 4, 8, 16)
N_EXPERTS = 16
EXPERT_FF = 1024
EC_FACTOR = 2
LN_EPS = 1e-5
NEG_BIG = -1e30

V7X_VMEM_LIMIT = 56 * 1024 * 1024
LANES = 128
HALO = 8


def _cparams(n_grid, vmem=V7X_VMEM_LIMIT):
    return pltpu.CompilerParams(dimension_semantics=("arbitrary",) * n_grid,
                                vmem_limit_bytes=vmem)


def _pack_bf16_pair(lo, hi):
    lo_bits = lax.shift_right_logical(pltpu.bitcast(lo.astype(F32), jnp.int32), 16)
    return pltpu.bitcast(hi.astype(F32), jnp.int32) | lo_bits


def _unpack_bf16_pair(packed):
    lo = pltpu.bitcast(lax.shift_left(packed, 16), F32).astype(BF16)
    hi = pltpu.bitcast(packed & jnp.int32(-65536), F32).astype(BF16)
    return lo, hi


def _standardize(xf):
    mu = jnp.mean(xf, axis=-1, keepdims=True)
    var = jnp.mean(jnp.square(xf - mu), axis=-1, keepdims=True)
    return (xf - mu) * lax.rsqrt(var + LN_EPS)


TA = 512
A_COLS = 2560 + LANES


def _inproj_kernel(*refs, alpha):
    if alpha is None:
        x_ref, *refs = refs
        x = x_ref[...]
    else:
        x1_ref, y_ref, g2_ref, b2_ref, *refs = refs
        x = _standardize(alpha * x1_ref[...] + y_ref[...]) * g2_ref[...] + b2_ref[...]
        refs[-2][...] = x
        refs = refs[:-2] + refs[-1:]
    (w_ref, b_ref, lng_ref, wscat_ref, bsfull_ref,
     ya_ref, qkv1_ref, qkv4_ref, qkv16_ref, cqk_ref, cv_ref, co_ref, dx_ref, gt_ref, qkv_scr) = refs
    xb = x.astype(BF16)
    h = jnp.dot(xb, w_ref[...], preferred_element_type=F32) + b_ref[...]
    qkv1_ref[0, 0] = h[:, 512:1280].astype(BF16)
    for c in range(768 // LANES):
        qkv_scr[c] = h[:, 512 + c * LANES:512 + (c + 1) * LANES]
    for (_, dil), out_ref in zip(DIL_PATTERNS[1:], (qkv4_ref, qkv16_ref)):
        for r in range(dil):
            for c in range(768 // LANES):
                out_ref[0, r, :, c * LANES:(c + 1) * LANES] = (
                    qkv_scr[c, pl.ds(r, TA // dil, stride=dil), :].astype(BF16))
    cqk_ref[...] = h[:, 1280:1792]
    cv_ref[...] = h[:, 1792:2048].astype(BF16)
    co_ref[...] = h[:, 2048:2304]
    dx_ref[...] = h[:, 2304:2560]
    gates_t = jnp.transpose(h[:, 2560:2688])
    for j in range(TA // LANES):
        gt_ref[j] = gates_t[0:4 * ML_HEADS, j * LANES:(j + 1) * LANES]
    u = jax.nn.gelu(h[:, 0:256])
    v = jax.nn.gelu(h[:, 256:512])
    vn = _standardize(v) * lng_ref[...]
    lane_grp = lax.broadcasted_iota(jnp.int32, (GM_CHUNK, MIX_W), 1) // (MIX_W // GM_GROUPS)
    for c in range(TA // GM_CHUNK):
        vc = vn[c * GM_CHUNK:(c + 1) * GM_CHUNK]
        stacked = jnp.concatenate(
            [jnp.where(lane_grp == g, vc, 0.0).astype(BF16) for g in range(GM_GROUPS)], axis=0)
        mixed = jnp.dot(wscat_ref[...], stacked, preferred_element_type=F32) + bsfull_ref[...]
        ya_ref[c * GM_CHUNK:(c + 1) * GM_CHUNK, :] = (
            u[c * GM_CHUNK:(c + 1) * GM_CHUNK] * mixed).astype(BF16)


def _inproj_call(x_in, w_cat, b_cat, lng, wscat, bsfull, batch, alpha=None):
    fused = alpha is not None
    n = (x_in[0] if fused else x_in).shape[0]
    seq = n // batch
    tpb = seq // TA
    tok = lambda w: pl.BlockSpec((TA, w), lambda i: (i, 0))
    const = lambda s: pl.BlockSpec(s, lambda i: (0,) * len(s))
    regrouped = lambda dil: pl.BlockSpec((1, dil, TA // dil, 768), lambda i: (i // tpb, 0, i % tpb, 0))
    out_shape = (
        jax.ShapeDtypeStruct((n, 256), BF16),
    ) + tuple(jax.ShapeDtypeStruct((batch, dil, seq // dil, 768), BF16)
              for _, dil in DIL_PATTERNS) + (
        jax.ShapeDtypeStruct((n, 512), F32),
        jax.ShapeDtypeStruct((n, 256), BF16),
        jax.ShapeDtypeStruct((n, 256), F32),
        jax.ShapeDtypeStruct((n, 256), F32),
        jax.ShapeDtypeStruct((n // LANES, 4 * ML_HEADS, LANES), F32),
    )
    x_specs = [tok(D_MODEL), tok(D_MODEL), const((1, D_MODEL)), const((1, D_MODEL))] if fused else [tok(D_MODEL)]
    out_specs = ((tok(256),) + tuple(regrouped(dil) for _, dil in DIL_PATTERNS)
                 + (tok(512), tok(256), tok(256), tok(256),
                    pl.BlockSpec((TA // LANES, 4 * ML_HEADS, LANES), lambda i: (i, 0, 0))))
    if fused:
        out_specs += (tok(D_MODEL),)
        out_shape += (jax.ShapeDtypeStruct((n, D_MODEL), F32),)
    return pl.pallas_call(
        functools.partial(_inproj_kernel, alpha=alpha), name="inproj_gmlp",
        grid=(n // TA,),
        in_specs=x_specs + [const((D_MODEL, A_COLS)), const((1, A_COLS)), const((1, MIX_W)),
                            const((GM_CHUNK, GM_GROUPS * GM_CHUNK)), const((GM_CHUNK, MIX_W))],
        out_specs=out_specs,
        out_shape=out_shape,
        scratch_shapes=[pltpu.VMEM((768 // LANES, TA, LANES), F32)],
        compiler_params=_cparams(1),
    )(*(x_in if fused else (x_in,)), w_cat, b_cat, lng, wscat, bsfull)


def _pack_inproj_weights(w_in, b_in):
    pad = lambda a: jnp.pad(a, ((0, 0), (0, LANES - 4 * ML_HEADS)))
    w_cat = jnp.concatenate([w_in[:, 0:2304], w_in[:, 2320:2576], pad(w_in[:, 2304:2320])], axis=1)
    b2 = b_in[None, :]
    b_cat = jnp.concatenate([b2[:, 0:2304], b2[:, 2320:2576], pad(b2[:, 2304:2320])], axis=1)
    return w_cat.astype(BF16), b_cat


def _pack_gmlp(gm_ws, gm_bs):
    wscat = jnp.transpose(gm_ws, (1, 0, 2)).reshape(GM_CHUNK, GM_GROUPS * GM_CHUNK).astype(BF16)
    bsfull = jnp.repeat(jnp.transpose(gm_bs), MIX_W // GM_GROUPS, axis=1)
    return wscat, bsfull


def _halo_specs(t, width, n_tiles):
    r = t // HALO
    main = pl.BlockSpec((1, t, width), lambda b, i: (b, i, 0))
    prev = pl.BlockSpec((1, HALO, width), lambda b, i: (b, jnp.maximum(i * r - 1, 0), 0))
    nxt = pl.BlockSpec((1, HALO, width), lambda b, i: (b, jnp.minimum((i + 1) * r, n_tiles * r - 1), 0))
    return main, prev, nxt


def _fill_halo_scratch(buf, x_ref, p_ref, n_ref, t):
    i = pl.program_id(1)
    last = pl.num_programs(1) - 1
    buf[0:HALO, :] = jnp.where(i > 0, p_ref[0], 0.0)
    buf[HALO:HALO + t, :] = x_ref[0]
    buf[HALO + t:2 * HALO + t, :] = jnp.where(i < last, n_ref[0], 0.0)


TP = 2048


def _pool_kernel(x_ref, p_ref, n_ref, w_ref, sc_ref, o_ref, buf, lvl):
    _fill_halo_scratch(buf, x_ref, p_ref, n_ref, TP)
    seq = pl.num_programs(1) * TP
    pos = pl.program_id(1) * TP + lax.broadcasted_iota(jnp.int32, (TP, 1), 0)
    lane_grp = lax.broadcasted_iota(jnp.int32, (TP, MIX_W), 1) // (MIX_W // len(POOL_WINDOWS))
    x0 = buf[HALO:HALO + TP, :]
    sums = []
    src, rows = buf, TP + 2 * HALO
    for k, win in enumerate(POOL_WINDOWS):
        half = win // 2
        rows -= half
        cur = src[0:rows, :] + src[half:rows + half, :] if k else buf[0:rows, :] + buf[1:rows + 1, :]
        if k + 1 < len(POOL_WINDOWS):
            lvl[k, 0:rows, :] = cur
            sums.append(lvl[k, HALO - half:HALO - half + TP, :])
            src = lvl.at[k]
        else:
            sums.append(cur[0:TP])
    pooled = jnp.zeros((TP, MIX_W), F32)
    for gi, win in enumerate(POOL_WINDOWS):
        half = win // 2
        cnt = (jnp.minimum(pos + half, seq) - jnp.maximum(pos - half, 0)).astype(F32)
        pooled = jnp.where(lane_grp == gi, sums[gi] / cnt - x0, pooled)
    mixed = jnp.dot(pooled.astype(BF16), w_ref[...], preferred_element_type=F32)
    o_ref[0] = (mixed * sc_ref[...]).astype(BF16)


def _pool_call(dx, w_block, scale):
    b, s, _ = dx.shape
    nt = s // TP
    main, prev, nxt = _halo_specs(TP, MIX_W, nt)
    return pl.pallas_call(
        _pool_kernel, name="pool_mixer",
        grid=(b, nt),
        in_specs=[main, prev, nxt,
                  pl.BlockSpec((MIX_W, MIX_W), lambda b, i: (0, 0)),
                  pl.BlockSpec((1, MIX_W), lambda b, i: (0, 0))],
        out_specs=pl.BlockSpec((1, TP, MIX_W), lambda b, i: (b, i, 0)),
        out_shape=jax.ShapeDtypeStruct((b, s, MIX_W), BF16),
        scratch_shapes=[pltpu.VMEM((TP + 2 * HALO, MIX_W), F32),
                        pltpu.VMEM((len(POOL_WINDOWS) - 1, TP + 2 * HALO, MIX_W), F32)],
        compiler_params=_cparams(2),
    )(dx, dx, dx, w_block, scale)


TQ = 128
TQS = 2048
TKEYS = TQ + 2 * ATT_BLOCK


def _attn_kernel(q_ref, kp_ref, km_ref, kn_ref, vp_ref, vm_ref, vn_ref, bias_ref, o_ref, lse_ref):
    for sq in range(q_ref.shape[1]):
        _attn_sequence(sq, q_ref, kp_ref, km_ref, kn_ref, vp_ref, vm_ref, vn_ref, bias_ref, o_ref, lse_ref)


def _attn_sequence(sq, q_ref, kp_ref, km_ref, kn_ref, vp_ref, vm_ref, vn_ref, bias_ref, o_ref, lse_ref):
    i = pl.program_id(2)
    q = q_ref[0, sq] * ATT_HD ** -0.5
    k = jnp.concatenate([kp_ref[0, sq], km_ref[0, sq], kn_ref[0, sq]], axis=0)
    v = jnp.concatenate([vp_ref[0, sq], vm_ref[0, sq], vn_ref[0, sq]], axis=0)
    lane = lax.broadcasted_iota(jnp.int32, (TQ, LANES), 1)
    lane_half = lax.broadcasted_iota(jnp.int32, (1, LANES), 1) // ATT_HD
    keep = [jnp.where(lane_half == hh, 1.0, 0.0).astype(BF16) for hh in range(2)]
    n_sub = q_ref.shape[2] // TQ
    last_step = pl.num_programs(2) - 1
    for j in range(n_sub):
        if j == 0:
            variant = jnp.where(i == 0, 0, 1)
        elif j == n_sub - 1:
            variant = jnp.where(i == last_step, 2, 1)
        else:
            variant = 1
        qrows = slice(j * TQ, (j + 1) * TQ)
        krows = slice(j * TQ, j * TQ + TKEYS)
        lse_tile = jnp.zeros((TQ, LANES), F32)
        for pair in range(ATT_HEADS // 2):
            grp = slice(pair * LANES, (pair + 1) * LANES)
            q_pair, k_pair, v_pair = q[qrows, grp], k[krows, grp], v[krows, grp]
            o_pair = jnp.zeros((TQ, LANES), F32)
            for hh in range(2):
                h = 2 * pair + hh
                logits = lax.dot_general(q_pair * keep[hh], k_pair, (((1,), (1,)), ((), ())),
                                         preferred_element_type=F32) + bias_ref[variant, h]
                m = jnp.max(logits, axis=-1, keepdims=True)
                p = jnp.exp(logits - m)
                ssum = jnp.sum(p, axis=-1, keepdims=True)
                o = jnp.dot(p.astype(BF16), v_pair, preferred_element_type=F32) / ssum
                o_pair = jnp.where(lane_half == hh, o, o_pair)
                lse_tile = jnp.where(lane == h, m + jnp.log(ssum), lse_tile)
            o_ref[0, sq, qrows, grp] = o_pair
        lse_ref[0, sq, qrows, :] = lse_tile


def _attn_call(qkv, bias):
    b, dil, l, _ = qkv.shape
    tqs = min(l, TQS)
    nsq = min(dil, TQS // tqs)
    nt = l // tqs
    r64 = tqs // ATT_BLOCK
    main = lambda c: pl.BlockSpec((1, nsq, tqs, MIX_W), lambda b, r, i: (b, r, i, c))
    prev = lambda c: pl.BlockSpec((1, nsq, ATT_BLOCK, MIX_W),
                                  lambda b, r, i: (b, r, jnp.maximum(i * r64 - 1, 0), c))
    nxt = lambda c: pl.BlockSpec((1, nsq, ATT_BLOCK, MIX_W),
                                 lambda b, r, i: (b, r, jnp.minimum((i + 1) * r64, nt * r64 - 1), c))
    return pl.pallas_call(
        _attn_kernel, name="band_attention",
        grid=(b, dil // nsq, nt),
        in_specs=[main(0), prev(1), main(1), nxt(1), prev(2), main(2), nxt(2),
                  pl.BlockSpec((3, ATT_HEADS, TQ, TKEYS), lambda b, r, i: (0, 0, 0, 0))],
        out_specs=(pl.BlockSpec((1, nsq, tqs, MIX_W), lambda b, r, i: (b, r, i, 0)),
                   pl.BlockSpec((1, nsq, tqs, LANES), lambda b, r, i: (b, r, i, 0))),
        out_shape=(jax.ShapeDtypeStruct((b, dil, l, MIX_W), F32),
                   jax.ShapeDtypeStruct((b, dil, l, LANES), F32)),
        compiler_params=_cparams(3),
    )(qkv, qkv, qkv, qkv, qkv, qkv, qkv, bias)


def _t5_bucket_static(rel):
    half = REL_BUCKETS // 2
    max_exact = half // 2
    ret = np.where(rel > 0, half, 0)
    n = np.abs(rel)
    nf = np.maximum(n, 1).astype(np.float32)
    large = max_exact + (np.log(nf / np.float32(max_exact)) / np.float32(math.log(REL_MAX_DIST / max_exact))
                         * np.float32(half - max_exact)).astype(np.int32)
    large = np.minimum(large, half - 1)
    return ret + np.where(n < max_exact, n, large)


def _attn_bias_tile(rel_bias, window, dil):
    side = (window // 2) // dil
    rel = np.arange(TKEYS)[None, :] - ATT_BLOCK - np.arange(TQ)[:, None]
    n_rel = TKEYS + TQ - 1
    rel_values = np.arange(n_rel) - (ATT_BLOCK + TQ - 1)
    onehot = jax.nn.one_hot(jnp.asarray(_t5_bucket_static(dil * rel_values), jnp.int32), REL_BUCKETS, dtype=F32)
    table = jnp.einsum('nr,rh->hn', onehot, rel_bias, precision=lax.Precision.HIGHEST)
    periodic = jnp.tile(jnp.pad(table, ((0, 0), (0, 1))), (1, TQ))[:, :TQ * n_rel]
    bias = periodic.reshape(ATT_HEADS, TQ, n_rel)[:, :, TQ - 1:]
    key = np.arange(TKEYS)[None, :]
    inside = np.abs(rel) <= side
    masks = np.stack([inside & (key >= ATT_BLOCK), inside, inside & (key < ATT_BLOCK + TQ)])
    return jnp.where(jnp.asarray(masks)[:, None], bias[None], NEG_BIG)


TM = 2048
VT_ROWS = ML_HD + 16


def _mlprep_kernel(x_ref, p_ref, n_ref, v_ref, w_ref, qt_out, k_out, vt_out, buf):
    _fill_halo_scratch(buf, x_ref, p_ref, n_ref, TM)
    conv = (buf[HALO - 1:HALO - 1 + TM, :] * w_ref[0:1, :] + buf[HALO:HALO + TM, :] * w_ref[1:2, :]
            + buf[HALO + 1:HALO + 1 + TM, :] * w_ref[2:3, :])
    qk = jax.nn.silu(conv)
    qt = jnp.transpose(qk[:, :MIX_W])
    vt = jnp.transpose(v_ref[0].astype(F32))
    ones_rows = jnp.where(lax.broadcasted_iota(jnp.int32, (VT_ROWS - ML_HD, LANES), 0) == 0, 1.0, 0.0)
    for h in range(ML_HEADS):
        sl = slice(h * ML_HD, (h + 1) * ML_HD)
        k_out[0, h] = (qk[:, MIX_W + h * ML_HD:MIX_W + (h + 1) * ML_HD] * ML_HD ** -0.5).astype(BF16)
        for p in range(TM // LANES):
            pl_ = slice(p * LANES, (p + 1) * LANES)
            qt_out[0, h, p] = qt[sl, pl_].astype(BF16)
            vt_out[0, h, p] = jnp.concatenate([vt[sl, pl_], ones_rows], axis=0).astype(BF16)


def _mlprep_call(cqk, cv, conv_w):
    b, s, _ = cqk.shape
    nt = s // TM
    n_pairs = s // LANES
    ppt = TM // LANES
    main, prev, nxt = _halo_specs(TM, 2 * MIX_W, nt)
    return pl.pallas_call(
        _mlprep_kernel, name="mlstm_prep",
        grid=(b, nt),
        in_specs=[main, prev, nxt,
                  pl.BlockSpec((1, TM, MIX_W), lambda b, i: (b, i, 0)),
                  pl.BlockSpec((3, 2 * MIX_W), lambda b, i: (0, 0))],
        out_specs=(pl.BlockSpec((1, ML_HEADS, ppt, ML_HD, LANES), lambda b, i: (b, 0, i, 0, 0)),
                   pl.BlockSpec((1, ML_HEADS, TM, ML_HD), lambda b, i: (b, 0, i, 0)),
                   pl.BlockSpec((1, ML_HEADS, ppt, VT_ROWS, LANES), lambda b, i: (b, 0, i, 0, 0))),
        out_shape=(jax.ShapeDtypeStruct((b, ML_HEADS, n_pairs, ML_HD, LANES), BF16),
                   jax.ShapeDtypeStruct((b, ML_HEADS, s, ML_HD), BF16),
                   jax.ShapeDtypeStruct((b, ML_HEADS, n_pairs, VT_ROWS, LANES), BF16)),
        scratch_shapes=[pltpu.VMEM((TM + 2 * HALO, 2 * MIX_W), F32)],
        compiler_params=_cparams(2),
    )(cqk, cqk, cqk, cv, conv_w)


def _gate_kernel(g_ref, fb_ref, a_ref, m_ref, iw_ref, en_ref, ws_ref, dec_ref):
    x = g_ref[...]
    n_pairs = x.shape[0]
    n_ch = 2 * ML_HEADS
    lane = lax.broadcasted_iota(jnp.int32, (1, 1, LANES), 2)
    t_in = lane % ML_CHUNK
    second = lane >= ML_CHUNK
    fwd_row = lax.broadcasted_iota(jnp.int32, (1, n_ch, 1), 1) < ML_HEADS
    li = x[:, 0:n_ch, :]
    z = x[:, n_ch:2 * n_ch, :] + fb_ref[...]
    lf = jnp.minimum(z, 0.0) - jnp.log1p(jnp.exp(-jnp.abs(z)))

    def within_chunk(v, op, ident, prefix):
        s = 1
        while s < ML_CHUNK:
            nb = pltpu.roll(v, s if prefix else LANES - s, 2)
            ok = (t_in >= s) if prefix else (t_in < ML_CHUNK - s)
            v = op(v, jnp.where(ok, nb, ident))
            s *= 2
        return v

    pre = within_chunk(lf, jnp.add, 0.0, True)
    suf = within_chunk(lf, jnp.add, 0.0, False)
    g = pre + suf - lf
    b = jnp.where(fwd_row, pre, suf)
    a = li - b
    cm_pre = within_chunk(a, jnp.maximum, -jnp.inf, True)
    cm_suf = within_chunk(a, jnp.maximum, -jnp.inf, False)
    cm = jnp.where(fwd_row, cm_pre, cm_suf)
    amax = jnp.maximum(cm_pre, cm_suf)

    def shift_pairs(v, k, fill):
        pad = jnp.full((abs(k),) + v.shape[1:], fill, F32)
        return (jnp.concatenate([pad, v[:n_pairs - k]], axis=0) if k > 0
                else jnp.concatenate([v[-k:], pad], axis=0))

    def from_chunk(v, dist, fill, forward):
        if dist == 1:
            y = pltpu.roll(v, ML_CHUNK, 2)
            if forward:
                return jnp.where(second, y, shift_pairs(y, 1, fill))
            return jnp.where(second, shift_pairs(y, -1, fill), y)
        return shift_pairs(v, dist // 2 if forward else -(dist // 2), fill)

    def running_stabiliser(forward):
        big_g, big_a = g, amax + g
        dist = 1
        while dist < 2 * n_pairs:
            gp = from_chunk(big_g, dist, 0.0, forward)
            ap = from_chunk(big_a, dist, -jnp.inf, forward)
            big_g, big_a = gp + big_g, jnp.maximum(ap + big_g, big_a)
            dist *= 2
        ge = from_chunk(big_g, 1, 0.0, forward)
        ae = from_chunk(big_a, 1, -jnp.inf, forward)
        return jnp.maximum(ge, ae)

    m_chunk = jnp.where(fwd_row, running_stabiliser(True), running_stabiliser(False))
    m_t = jnp.maximum(cm, m_chunk)
    m_last = jnp.maximum(amax, m_chunk)
    a_ref[0] = a
    m_ref[0] = m_t
    iw_ref[0] = jnp.exp(m_chunk - m_t)
    en_ref[0] = jnp.exp(-(b + m_t))
    ws_ref[0] = jnp.exp(a - m_last)
    dec_ref[0] = jnp.exp(m_chunk - m_last)


def _gate_call(gates_t, fbias_col, batch):
    n_pairs = gates_t.shape[0] // batch
    n_ch = 2 * ML_HEADS
    out = pl.BlockSpec((1, n_pairs, n_ch, LANES), lambda i: (i, 0, 0, 0))
    return pl.pallas_call(
        _gate_kernel, name="mlstm_gates",
        grid=(batch,),
        in_specs=[pl.BlockSpec((n_pairs, 2 * n_ch, LANES), lambda i: (i, 0, 0)),
                  pl.BlockSpec((n_ch, 1), lambda i: (0, 0))],
        out_specs=(out,) * 6,
        out_shape=(jax.ShapeDtypeStruct((batch, n_pairs, n_ch, LANES), F32),) * 6,
        compiler_params=_cparams(1),
    )(gates_t, fbias_col)


TE = 2048


def _mlstm_kernel(*refs):
    fwd, bwd, (hf_ref, hb_ref, state) = refs[:9], refs[9:18], refs[18:]
    i = pl.program_id(1)

    @pl.when(i == 0)
    def _():
        state[...] = jnp.zeros(state.shape, F32)

    n_pairs = TE // LANES
    s_idx = lax.broadcasted_iota(jnp.int32, (LANES, LANES), 0)
    t_idx = lax.broadcasted_iota(jnp.int32, (LANES, LANES), 1)
    same_chunk = (s_idx >= ML_CHUNK) == (t_idx >= ML_CHUNK)
    upper_lanes = lax.broadcasted_iota(jnp.int32, (1, LANES), 1) >= ML_CHUNK

    def pair_body(p, carry):
        jobs = []
        for d, ((qt_r, k_r, vt_r, a_r, m_r, iw_r, en_r, ws_r, dec_r), out_r) in enumerate(
                ((fwd, hf_ref), (bwd, hb_ref))):
            pp = p if d == 0 else n_pairs - 1 - p
            srows = pl.ds(pl.multiple_of(pp * LANES, LANES), LANES)
            a_t = jnp.transpose(jnp.concatenate(
                [a_r[0, pp], jnp.zeros((LANES - 2 * ML_HEADS, LANES), F32)], axis=0))
            m_t, iw_t, en_t, ws_t, dec_t = [r[0, pp] for r in (m_r, iw_r, en_r, ws_r, dec_r)]
            dec_lo, dec_hi = dec_t[:, :ML_CHUNK], pltpu.roll(dec_t, ML_CHUNK, 1)[:, :ML_CHUNK]
            in_first = upper_lanes if d else ~upper_lanes
            tri = same_chunk & ((s_idx >= t_idx) if d else (s_idx <= t_idx))
            for h in range(ML_HEADS):
                ch = d * ML_HEADS + h
                row = lambda t: t[ch:ch + 1]
                jobs.append(dict(
                    ch=ch, tri=tri, in_first=in_first, k=k_r[0, h, srows, :], qt=qt_r[0, h, pp],
                    vt=vt_r[0, h, pp], a=a_t[:, ch:ch + 1], m=row(m_t), iw=row(iw_t), en=row(en_t), ws=row(ws_t),
                    dec_first=row(dec_hi if d else dec_lo), dec_second=row(dec_lo if d else dec_hi),
                    out=(out_r, pp, h)))
        for j in jobs:
            vt_f = j["vt"].astype(F32)
            j["c0"] = state[j["ch"]]
            j["st"] = jnp.dot(j["k"], j["qt"], preferred_element_type=F32)
            j["inter1"] = jnp.dot(j["c0"].astype(BF16), j["qt"], preferred_element_type=F32)
            j["upd1"] = jnp.dot((vt_f * jnp.where(j["in_first"], j["ws"], 0.0)).astype(BF16), j["k"],
                                preferred_element_type=F32)
            j["upd2"] = jnp.dot((vt_f * jnp.where(j["in_first"], 0.0, j["ws"])).astype(BF16), j["k"],
                                preferred_element_type=F32)
        for j in jobs:
            j["swt"] = j["st"] * jnp.exp(jnp.where(j["tri"], j["a"] - j["m"], NEG_BIG))
            j["intra"] = jnp.dot(j["vt"], j["swt"].astype(BF16), preferred_element_type=F32)
            j["c1"] = j["dec_first"] * j["c0"] + j["upd1"]
            j["inter2"] = jnp.dot(j["c1"].astype(BF16), j["qt"], preferred_element_type=F32)
        for j in jobs:
            inter = jnp.where(j["in_first"], j["inter1"], j["inter2"])
            den = jnp.sum(j["swt"], axis=0, keepdims=True) + j["iw"] * inter[ML_HD:ML_HD + 1]
            tot = j["intra"][:ML_HD] + j["iw"] * inter[:ML_HD]
            out_r, pp, h = j["out"]
            out_r[0, pp, h * ML_HD:(h + 1) * ML_HD, :] = tot / jnp.maximum(jnp.abs(den), j["en"])
            state[j["ch"]] = j["dec_second"] * j["c1"] + j["upd2"]
        return carry

    lax.fori_loop(0, n_pairs, pair_body, 0)


def _mlstm_call(qt, k, vt, a_t, m_t, iw_t, en_t, ws_t, dec_t):
    b, _, s, _ = k.shape
    nt = s // TE
    ppt = TE // LANES

    def specs(rev):
        ti = (lambda i: nt - 1 - i) if rev else (lambda i: i)
        tile = pl.BlockSpec((1, ppt, 2 * ML_HEADS, LANES), lambda b, i: (b, ti(i), 0, 0))
        return [
            pl.BlockSpec((1, ML_HEADS, ppt, ML_HD, LANES), lambda b, i: (b, 0, ti(i), 0, 0)),
            pl.BlockSpec((1, ML_HEADS, TE, ML_HD), lambda b, i: (b, 0, ti(i), 0)),
            pl.BlockSpec((1, ML_HEADS, ppt, VT_ROWS, LANES), lambda b, i: (b, 0, ti(i), 0, 0)),
            tile, tile, tile, tile, tile, tile]

    args = [qt, k, vt, a_t, m_t, iw_t, en_t, ws_t, dec_t]
    out_f = pl.BlockSpec((1, ppt, MIX_W, LANES), lambda b, i: (b, i, 0, 0))
    out_b = pl.BlockSpec((1, ppt, MIX_W, LANES), lambda b, i: (b, nt - 1 - i, 0, 0))
    return pl.pallas_call(
        _mlstm_kernel, name="mlstm_scan",
        grid=(b, nt),
        in_specs=specs(False) + specs(True),
        out_specs=(out_f, out_b),
        out_shape=(jax.ShapeDtypeStruct((b, s // LANES, MIX_W, LANES), F32),) * 2,
        scratch_shapes=[pltpu.VMEM((2 * ML_HEADS, VT_ROWS, ML_HD), F32)],
        compiler_params=_cparams(2),
    )(*args, *args)


def _mlstm_branch(cqk, cv, gates_t, conv_w, fbias):
    qt, k, vt = _mlprep_call(cqk, cv, conv_w)
    factors = _gate_call(gates_t, fbias.reshape(2 * ML_HEADS, 1), cqk.shape[0])
    return _mlstm_call(qt, k, vt, *factors)


TF = 512


def _merge_kernel(x_ref, ya_ref, o1_ref, o2_ref, o3_ref, l1_ref, l2_ref, l3_ref, hf_ref, hb_ref,
                  co_ref, yd_ref, wg_ref, bg_ref, wbr_ref, wout_ref, mng_ref, lng_ref, lnb_ref,
                  wr_ref, x1_ref, x1p_ref, aff_ref, o_scr, l_scr, *, alpha):
    x = x_ref[0]
    xb = x.astype(BF16)

    def natural_order(src_ref, scr):
        dil, width = src_ref.shape[1], src_ref.shape[3]
        if dil == 1:
            return src_ref[0, 0]
        for r in range(dil):
            for c in range(width // LANES):
                scr[c, pl.ds(r, TF // dil, stride=dil), :] = src_ref[0, r, :, c * LANES:(c + 1) * LANES]
        return jnp.concatenate([scr[c] for c in range(width // LANES)], axis=1)

    lane_head = lax.broadcasted_iota(jnp.int32, (TF, MIX_W), 1) // ML_HD
    l1, l2, l3 = [natural_order(r, l_scr.at[p]) for p, r in enumerate((l1_ref, l2_ref, l3_ref))]
    o1, o2, o3 = [natural_order(r, o_scr.at[p]) for p, r in enumerate((o1_ref, o2_ref, o3_ref))]
    lm = jnp.maximum(jnp.maximum(l1, l2), l3)
    e1, e2, e3 = jnp.exp(l1 - lm), jnp.exp(l2 - lm), jnp.exp(l3 - lm)
    inv = 1.0 / (e1 + e2 + e3)

    def per_head(w):
        out = jnp.zeros((TF, MIX_W), F32)
        for h in range(ATT_HEADS):
            out = jnp.where(lane_head == h, w[:, h:h + 1], out)
        return out

    y_b = per_head(e1 * inv) * o1 + per_head(e2 * inv) * o2 + per_head(e3 * inv) * o3
    hsum_t = jnp.concatenate([hf_ref[0, p] + hb_ref[0, p] for p in range(TF // LANES)], axis=1)
    per_head_rows = hsum_t.reshape(ML_HEADS, ML_HD, TF)
    mu = jnp.mean(per_head_rows, axis=1, keepdims=True)
    cen = per_head_rows - mu
    var = jnp.mean(cen * cen, axis=1, keepdims=True)
    hn_t = (cen * lax.rsqrt(var + LN_EPS)).reshape(MIX_W, TF)
    y_c_t = (jax.nn.sigmoid(jnp.transpose(co_ref[0])) * (hn_t * mng_ref[...])).astype(BF16)
    ys = (ya_ref[0], y_b.astype(BF16), None, yd_ref[0])
    merged = jnp.zeros((TF, D_MODEL), F32)
    for n in range(N_BRANCH):
        cols = slice(n * D_MODEL, (n + 1) * D_MODEL)
        gate = jax.nn.sigmoid(jnp.dot(xb, wg_ref[:, cols], preferred_element_type=F32) + bg_ref[:, cols])
        if ys[n] is None:
            proj = lax.dot_general(y_c_t, wbr_ref[n], (((0,), (0,)), ((), ())), preferred_element_type=F32)
        else:
            proj = jnp.dot(ys[n], wbr_ref[n], preferred_element_type=F32)
        merged = merged + gate * proj
    mix = jnp.dot(merged.astype(BF16), wout_ref[...], preferred_element_type=F32)
    x1 = _standardize(alpha * x + mix) * lng_ref[...] + lnb_ref[...]
    x1_ref[0] = x1
    x1b = x1.astype(BF16)
    x1p_ref[0] = _pack_bf16_pair(x1b[:, :D_MODEL // 2], x1b[:, D_MODEL // 2:])
    logits = lax.dot_general(wr_ref[...], x1b, (((1,), (1,)), ((), ())),
                             preferred_element_type=F32)
    ex = jnp.exp(logits - jnp.max(logits, axis=0, keepdims=True))
    aff_ref[0] = ex / jnp.sum(ex, axis=0, keepdims=True)


def _merge_call(x, ya, o_list, l_list, hf, hb, co, yd, wg, bg, wbr, wout, mng, lng, lnb, wr_t, alpha):
    b, s, _ = x.shape
    tok = lambda w: pl.BlockSpec((1, TF, w), lambda b, i: (b, i, 0))
    grouped = lambda dil, w: pl.BlockSpec((1, dil, TF // dil, w), lambda b, i: (b, 0, i, 0))
    chunked = pl.BlockSpec((1, TF // LANES, MIX_W, LANES), lambda b, i: (b, i, 0, 0))
    const = lambda shp: pl.BlockSpec(shp, lambda b, i: (0,) * len(shp))
    return pl.pallas_call(
        functools.partial(_merge_kernel, alpha=alpha), name="merge_ln_router",
        grid=(b, s // TF),
        in_specs=[tok(D_MODEL), tok(MIX_W)] + [grouped(dil, MIX_W) for _, dil in DIL_PATTERNS]
                 + [grouped(dil, LANES) for _, dil in DIL_PATTERNS]
                 + [chunked, chunked, tok(MIX_W), tok(MIX_W)]
                 + [const((D_MODEL, N_BRANCH * D_MODEL)), const((1, N_BRANCH * D_MODEL)),
                    const((N_BRANCH, MIX_W, D_MODEL)), const((D_MODEL, D_MODEL)), const((MIX_W, 1)),
                    const((1, D_MODEL)), const((1, D_MODEL)), const((N_EXPERTS, D_MODEL))],
        out_specs=(tok(D_MODEL), tok(D_MODEL // 2), pl.BlockSpec((1, N_EXPERTS, TF), lambda b, i: (b, 0, i))),
        out_shape=(jax.ShapeDtypeStruct((b, s, D_MODEL), F32),
                   jax.ShapeDtypeStruct((b, s, D_MODEL // 2), jnp.int32),
                   jax.ShapeDtypeStruct((b, N_EXPERTS, s), F32)),
        scratch_shapes=[pltpu.VMEM((len(DIL_PATTERNS), MIX_W // LANES, TF, LANES), F32),
                        pltpu.VMEM((len(DIL_PATTERNS), 1, TF, LANES), F32)],
        compiler_params=_cparams(2),
    )(x, ya, *o_list, *l_list, hf, hb, co, yd, wg, bg, wbr, wout, mng, lng, lnb, wr_t)


TT = 256


def _select_kernel(aff_ref, slot_ref, *, cap):
    s = aff_ref.shape[2]
    bits = pltpu.bitcast(aff_ref[0], jnp.int32)

    def bit_step(i, thr):
        cand = thr | jnp.left_shift(jnp.int32(1), 30 - i)
        cnt = jnp.sum((bits >= cand).astype(jnp.int32), axis=1, keepdims=True)
        return jnp.where(cnt >= cap, cand, thr)

    thr = lax.fori_loop(0, 31, bit_step, jnp.zeros((N_EXPERTS, 1), jnp.int32))
    gt = bits > thr
    eq = bits == thr
    need = (cap - jnp.sum(gt.astype(jnp.int32), axis=1, keepdims=True)).astype(F32)
    upper = (lax.broadcasted_iota(jnp.int32, (TT, TT), 0)
             <= lax.broadcasted_iota(jnp.int32, (TT, TT), 1)).astype(BF16)
    eq_before = jnp.zeros((N_EXPERTS, 1), F32)
    sel_before = jnp.zeros((N_EXPERTS, 1), F32)
    for j in range(s // TT):
        cols = slice(j * TT, (j + 1) * TT)
        eq_j = eq[:, cols]
        eq_incl = eq_before + jnp.dot(eq_j.astype(BF16), upper, preferred_element_type=F32)
        sel_j = gt[:, cols] | (eq_j & (eq_incl <= need))
        sel_f = sel_j.astype(F32)
        sel_incl = sel_before + jnp.dot(sel_f.astype(BF16), upper, preferred_element_type=F32)
        slot_ref[0, :, cols] = jnp.where(sel_j, sel_incl - 1.0, -1.0).astype(jnp.int32)
        eq_before = eq_incl[:, TT - 1:TT]
        sel_before = sel_incl[:, TT - 1:TT]


def _select_call(aff_t, cap):
    b, e, s = aff_t.shape
    return pl.pallas_call(
        functools.partial(_select_kernel, cap=cap), name="expert_choice_select",
        grid=(b,),
        in_specs=[pl.BlockSpec((1, e, s), lambda i: (i, 0, 0))],
        out_specs=pl.BlockSpec((1, e, s), lambda i: (i, 0, 0)),
        out_shape=jax.ShapeDtypeStruct((b, e, s), jnp.int32),
        compiler_params=_cparams(1),
    )(aff_t)


SC_LANES = 16
SC_ROWS = 64
SC_IDX = 128
SC_SLAB = 128
SC_ZROWS = 64
CF = 1024


def _sc_dispatch_call(x_flat, slot2, aff2, seq, cap):
    n_pairs = slot2.shape[0]
    d = x_flat.shape[1]
    info = plsc.get_sparse_core_info()
    n_workers = info.num_cores * info.num_subcores
    assert n_pairs % n_workers == 0 and seq % SC_LANES == 0 and cap % (2 * SC_ROWS) == 0
    pairs_per_worker = n_pairs // n_workers
    mesh = plsc.VectorSubcoreMesh(core_axis_name="c", subcore_axis_name="s")

    @functools.partial(
        pl.kernel, mesh=mesh, name="expert_dispatch_sc",
        compiler_params=pltpu.CompilerParams(needs_layout_passes=False),
        out_type=(jax.ShapeDtypeStruct((n_pairs * cap, d), x_flat.dtype),
                  jax.ShapeDtypeStruct((n_pairs, cap // SC_IDX, SC_IDX), jnp.int32),
                  jax.ShapeDtypeStruct((n_pairs, cap // SC_IDX, SC_IDX), F32)),
        scratch_types=[pltpu.VMEM((seq,), jnp.int32), pltpu.VMEM((seq,), F32),
                       pltpu.VMEM((cap,), jnp.int32), pltpu.VMEM((cap // SC_IDX, SC_IDX), jnp.int32),
                       pltpu.VMEM((cap // SC_IDX, SC_IDX), F32),
                       pltpu.VMEM((2, SC_ROWS, d), x_flat.dtype), pltpu.SemaphoreType.DMA((2,))])
    def dispatch(x_hbm, slot_hbm, aff_hbm, xs_hbm, tok_hbm, gate_hbm,
                 slot_v, aff_v, idx_v, tok_v, gate_v, rows_v, sem):
        worker = lax.axis_index("s") * info.num_cores + lax.axis_index("c")
        lane = lax.iota(jnp.int32, SC_LANES)

        def gather(c0, buf):
            return pltpu.make_async_copy(x_hbm.at[idx_v.at[pl.ds(c0, SC_ROWS)]], rows_v.at[buf], sem.at[buf])
        for k in range(pairs_per_worker):
            pair = worker * pairs_per_worker + k
            row0 = (pair // N_EXPERTS) * seq
            pltpu.sync_copy(slot_hbm.at[pair], slot_v)
            pltpu.sync_copy(aff_hbm.at[pair], aff_v)

            @plsc.parallel_loop(0, seq, step=SC_LANES, unroll=4)
            def _(t0):
                sv = slot_v[pl.ds(t0, SC_LANES)]
                picked = sv >= 0
                hi, lo = lax.shift_right_logical(sv, 7), sv & (SC_IDX - 1)
                plsc.store_scatter(tok_v, [hi, lo], t0 + lane, mask=picked)
                plsc.store_scatter(idx_v, [sv], row0 + t0 + lane, mask=picked)
                plsc.store_scatter(gate_v, [hi, lo], aff_v[pl.ds(t0, SC_LANES)], mask=picked)

            pltpu.sync_copy(tok_v, tok_hbm.at[pair])
            pltpu.sync_copy(gate_v, gate_hbm.at[pair])

            gather(0, 0).start()

            @pl.loop(0, cap, step=2 * SC_ROWS)
            def _(c0):
                gather(c0 + SC_ROWS, 1).start()
                gather(c0, 0).wait()
                pltpu.sync_copy(rows_v.at[0], xs_hbm.at[pl.ds(pair * cap + c0, SC_ROWS)])

                @pl.when(c0 + 2 * SC_ROWS < cap)
                def _():
                    gather(c0 + 2 * SC_ROWS, 0).start()

                gather(c0 + SC_ROWS, 1).wait()
                pltpu.sync_copy(rows_v.at[1], xs_hbm.at[pl.ds(pair * cap + c0 + SC_ROWS, SC_ROWS)])

    return dispatch(x_flat, slot2, aff2)


def _expert_kernel(xs_ref, g_ref, w1_ref, w3_ref, w2_ref, ye_ref, w1_bf, w3_bf, w2_bf):
    @pl.when((pl.program_id(1) == 0) & (pl.program_id(2) == 0))
    def _():
        w1_bf[...] = w1_ref[0, 0].astype(BF16)
        w3_bf[...] = w3_ref[0, 0].astype(BF16)
        w2_bf[...] = w2_ref[0, 0].astype(BF16)

    xs = jnp.concatenate(_unpack_bf16_pair(xs_ref[0, 0]), axis=1)
    hid = (jax.nn.silu(jnp.dot(xs, w1_bf[...], preferred_element_type=F32))
           * jnp.dot(xs, w3_bf[...], preferred_element_type=F32))
    g_rows = g_ref[0, 0]
    n_rows = g_rows.shape[0]
    g_t = jnp.transpose(jnp.concatenate([g_rows, jnp.zeros((LANES - n_rows, LANES), F32)], axis=0))
    g_col = jnp.concatenate([g_t[:, r:r + 1] for r in range(n_rows)], axis=0)
    ye_ref[0, 0] = jnp.dot(hid.astype(BF16), w2_bf[...], preferred_element_type=F32) * g_col


def _expert_call(xs4, gate4, w1, w3, w2, layer):
    b, e, cap, half = xs4.shape
    d, ff = w1.shape[2], w1.shape[3]
    assert d == 2 * half
    rows = lambda w: pl.BlockSpec((1, 1, CF, w), lambda e, b, j: (b, e, j, 0))
    wspec = lambda r, c: pl.BlockSpec((1, 1, r, c), lambda e, b, j: (layer, e, 0, 0))
    return pl.pallas_call(
        _expert_kernel, name="expert_ffn",
        grid=(e, b, cap // CF),
        in_specs=[rows(half), pl.BlockSpec((1, 1, CF // LANES, LANES), lambda e, b, j: (b, e, j, 0)),
                  wspec(d, ff), wspec(d, ff), wspec(ff, d)],
        out_specs=rows(d),
        out_shape=jax.ShapeDtypeStruct((b, e, cap, d), F32),
        scratch_shapes=[pltpu.VMEM((d, ff), BF16), pltpu.VMEM((d, ff), BF16), pltpu.VMEM((ff, d), BF16)],
        compiler_params=_cparams(3),
    )(xs4, gate4, w1, w3, w2)


def _sc_combine_call(ye_flat, tok3, seq):
    n_pairs, n_chunks, _ = tok3.shape
    cap = n_chunks * SC_IDX
    d = ye_flat.shape[1]
    nb = n_pairs // N_EXPERTS
    info = plsc.get_sparse_core_info()
    assert info.num_subcores == N_EXPERTS and nb % info.num_cores == 0 and n_chunks % 2 == 0
    assert seq % (info.num_subcores * SC_ZROWS) == 0 and d % SC_SLAB == 0
    batches_per_core = nb // info.num_cores
    own_rows = seq // info.num_subcores
    mesh = plsc.VectorSubcoreMesh(core_axis_name="c", subcore_axis_name="s")

    @functools.partial(
        pl.kernel, mesh=mesh, name="expert_combine_sc",
        compiler_params=pltpu.CompilerParams(needs_layout_passes=False),
        out_type=jax.ShapeDtypeStruct((nb * seq, d), F32),
        scratch_types=[pltpu.VMEM_SHARED((seq, SC_SLAB), F32),
                       pltpu.VMEM((n_chunks, SC_IDX), jnp.int32),
                       pltpu.VMEM((2, SC_IDX, SC_SLAB), F32),
                       pltpu.VMEM((SC_ZROWS, SC_SLAB), F32),
                       pltpu.SemaphoreType.DMA((2,))])
    def combine(ye_hbm, tok_hbm, out_hbm, acc_sh, tok_v, rows_v, zero_v, sem):
        core = lax.axis_index("c")
        sub = lax.axis_index("s")

        @pl.loop(0, SC_ZROWS)
        def _(r):
            for l0 in range(0, SC_SLAB, SC_LANES):
                zero_v[r, pl.ds(l0, SC_LANES)] = jnp.zeros((SC_LANES,), F32)

        for bb in range(batches_per_core):
            batch = core * batches_per_core + bb
            pair = batch * N_EXPERTS + sub
            pltpu.sync_copy(tok_hbm.at[pair], tok_v)

            @pl.loop(0, d // SC_SLAB)
            def _(slab):
                cols = pl.ds(pl.multiple_of(slab * SC_SLAB, SC_SLAB), SC_SLAB)

                @pl.loop(0, own_rows, step=SC_ZROWS)
                def _(r0):
                    pltpu.sync_copy(zero_v, acc_sh.at[pl.ds(sub * own_rows + r0, SC_ZROWS)])

                def load(j, buf):
                    return pltpu.make_async_copy(
                        ye_hbm.at[pl.ds(pair * cap + j * SC_IDX, SC_IDX), cols], rows_v.at[buf], sem.at[buf])

                load(0, 0).start()
                plsc.subcore_barrier()

                for j in range(0, n_chunks, 2):
                    load(j + 1, 1).start()
                    load(j, 0).wait()
                    pltpu.sync_copy(rows_v.at[0], acc_sh.at[tok_v.at[j]], add=True)
                    if j + 2 < n_chunks:
                        load(j + 2, 0).start()
                    load(j + 1, 1).wait()
                    pltpu.sync_copy(rows_v.at[1], acc_sh.at[tok_v.at[j + 1]], add=True)

                plsc.subcore_barrier()
                pltpu.sync_copy(acc_sh.at[pl.ds(sub * own_rows, own_rows)],
                                out_hbm.at[pl.ds(batch * seq + sub * own_rows, own_rows), cols])

    return combine(ye_flat, tok3)


TN = 1024


def _resln_kernel(x_ref, y_ref, g_ref, b_ref, o_ref, *, alpha):
    o_ref[...] = _standardize(alpha * x_ref[...] + y_ref[...]) * g_ref[...] + b_ref[...]


def _resln_call(x2d, y2d, g, bta, alpha):
    n, d = x2d.shape
    tok = pl.BlockSpec((TN, d), lambda i: (i, 0))
    vec = pl.BlockSpec((1, d), lambda i: (0, 0))
    return pl.pallas_call(
        functools.partial(_resln_kernel, alpha=alpha), name="residual_layernorm",
        grid=(n // TN,), in_specs=[tok, tok, vec, vec], out_specs=tok,
        out_shape=jax.ShapeDtypeStruct((n, d), F32),
        compiler_params=_cparams(1),
    )(x2d, y2d, g, bta)


def _expert_choice_ffn(x1p, aff_t, w1, w3, w2, layer):
    b, s, half = x1p.shape
    d = 2 * half
    cap = EC_FACTOR * s // N_EXPERTS
    slot = _select_call(aff_t, cap)
    xs, tok, gate = _sc_dispatch_call(x1p.reshape(b * s, half), slot.reshape(b * N_EXPERTS, s),
                                      aff_t.reshape(b * N_EXPERTS, s), s, cap)
    ye = _expert_call(xs.reshape(b, N_EXPERTS, cap, half), gate.reshape(b, N_EXPERTS, cap // SC_IDX, SC_IDX), w1, w3, w2,
                      layer)
    out = _sc_combine_call(ye.reshape(b * N_EXPERTS * cap, d),
                           tok, s)
    return out.reshape(b, s, d)


def _pack_pool(pool_w):
    g, gd, _ = pool_w.shape
    out = jnp.zeros((g * gd, g * gd), F32)
    for i in range(g):
        out = out.at[i * gd:(i + 1) * gd, i * gd:(i + 1) * gd].set(pool_w[i])
    return out.astype(BF16)


def _layer(layer, x, pending, alpha, bias_tiles, w_in, b_in, gm_ln_g, gm_ws, gm_bs, ml_conv, ml_fbias,
           ml_norm_g, pool_w, pool_scale, w_branch, w_out, ln1_g, ln1_b, w_router, w_e1, w_e3, w_e2):
    b, s, d = x.shape
    n_small = 2576
    w_cat, b_cat = _pack_inproj_weights(w_in, b_in)
    wscat, bsfull = _pack_gmlp(gm_ws, gm_bs)
    if pending is None:
        outs = _inproj_call(x.reshape(b * s, d), w_cat, b_cat, gm_ln_g[None], wscat, bsfull, b)
    else:
        *outs, x2 = _inproj_call(pending, w_cat, b_cat, gm_ln_g[None], wscat, bsfull, b, alpha)
        x = x2.reshape(b, s, d)
    ya, qkv1, qkv4, qkv16, cqk, cv, co, dx, gates_t = outs
    r3 = lambda t: t.reshape(b, s, t.shape[-1])
    o_list, l_list = [], []
    for qkv, bias in zip((qkv1, qkv4, qkv16), bias_tiles):
        o, lse = _attn_call(qkv, bias)
        o_list.append(o)
        l_list.append(lse)
    hf, hb = _mlstm_branch(r3(cqk), r3(cv), gates_t, ml_conv, ml_fbias)
    yd = _pool_call(r3(dx), _pack_pool(pool_w), pool_scale[None])
    x1, x1p, aff_t = _merge_call(
        x, r3(ya), o_list, l_list, hf, hb, r3(co), yd,
        w_in[:, n_small:].astype(BF16), b_in[None, n_small:], w_branch.astype(BF16), w_out.astype(BF16),
        ml_norm_g[:, None], ln1_g[None], ln1_b[None], jnp.transpose(w_router).astype(BF16), alpha)
    ffn = _expert_choice_ffn(x1p, aff_t, w_e1, w_e3, w_e2, layer)
    return x1.reshape(b * s, d), ffn.reshape(b * s, d)


def kernel(x, w_in, b_in, gm_ln_g, gm_ws, gm_bs, rel_bias, ml_conv, ml_fbias, ml_norm_g, pool_w,
           pool_scale, w_branch, w_out, ln1_g, ln1_b, w_router, w_e1, w_e3, w_e2, ln2_g, ln2_b):
    depth = w_in.shape[0]
    alpha = (2 * depth) ** 0.25
    bias_tiles = [_attn_bias_tile(rel_bias, window, dil) for window, dil in DIL_PATTERNS]
    b, s, d = x.shape
    pending = None
    for l in range(depth):
        x1, ffn = _layer(l, x, pending, alpha, bias_tiles, w_in[l], b_in[l], gm_ln_g[l], gm_ws[l], gm_bs[l],
                         ml_conv[l], ml_fbias[l], ml_norm_g[l], pool_w[l], pool_scale[l], w_branch[l],
                         w_out[l], ln1_g[l], ln1_b[l], w_router[l], w_e1, w_e3, w_e2)
        pending = (x1, ffn, ln2_g[l][None], ln2_b[l][None])
    return _resln_call(*pending, alpha).reshape(b, s, d)
```

```python
import functools
import math

import jax
import jax.numpy as jnp
import numpy as np
from jax import lax
from jax.experimental import pallas as pl
from jax.experimental.pallas import tpu as pltpu
from jax.experimental.pallas import tpu_sc as plsc

F32 = jnp.float32
BF16 = jnp.bfloat16

D_MODEL = 1024
MIX_W = 256
N_BRANCH = 4
GM_CHUNK = 128
GM_GROUPS = 4
ATT_HEADS = 4
ATT_HD = 64
DIL_PATTERNS = ((128, 1), (512, 4), (2048, 16))
ATT_BLOCK = 64
REL_BUCKETS = 32
REL_MAX_DIST = 1024
ML_HEADS = 4
ML_HD = 64
ML_CHUNK = 64
POOL_WINDOWS = (2, 4, 8, 16)
N_EXPERTS = 16
EXPERT_FF = 1024
EC_FACTOR = 2
LN_EPS = 1e-5
NEG_BIG = -1e30

V7X_VMEM_LIMIT = 56 * 1024 * 1024
LANES = 128
HALO = 8


def _cparams(n_grid, vmem=V7X_VMEM_LIMIT):
    return pltpu.CompilerParams(dimension_semantics=("arbitrary",) * n_grid,
                                vmem_limit_bytes=vmem)


def _pack_bf16_pair(lo, hi):
    lo_bits = lax.shift_right_logical(pltpu.bitcast(lo.astype(F32), jnp.int32), 16)
    return pltpu.bitcast(hi.astype(F32), jnp.int32) | lo_bits


def _unpack_bf16_pair(packed):
    lo = pltpu.bitcast(lax.shift_left(packed, 16), F32).astype(BF16)
    hi = pltpu.bitcast(packed & jnp.int32(-65536), F32).astype(BF16)
    return lo, hi


def _standardize(xf):
    mu = jnp.mean(xf, axis=-1, keepdims=True)
    var = jnp.mean(jnp.square(xf - mu), axis=-1, keepdims=True)
    return (xf - mu) * lax.rsqrt(var + LN_EPS)


TA = 512
A_COLS = 2560 + LANES


def _inproj_kernel(*refs, alpha):
    if alpha is None:
        x_ref, *refs = refs
        x = x_ref[...]
    else:
        x1_ref, y_ref, g2_ref, b2_ref, *refs = refs
        x = _standardize(alpha * x1_ref[...] + y_ref[...]) * g2_ref[...] + b2_ref[...]
        refs[-2][...] = x
        refs = refs[:-2] + refs[-1:]
    (w_ref, b_ref, lng_ref, wscat_ref, bsfull_ref,
     ya_ref, qkv1_ref, qkv4_ref, qkv16_ref, cqk_ref, cv_ref, co_ref, dx_ref, gt_ref, qkv_scr) = refs
    xb = x.astype(BF16)
    h = jnp.dot(xb, w_ref[...], preferred_element_type=F32) + b_ref[...]
    qkv1_ref[0, 0] = h[:, 512:1280].astype(BF16)
    for c in range(768 // LANES):
        qkv_scr[c] = h[:, 512 + c * LANES:512 + (c + 1) * LANES]
    for (_, dil), out_ref in zip(DIL_PATTERNS[1:], (qkv4_ref, qkv16_ref)):
        for r in range(dil):
            for c in range(768 // LANES):
                out_ref[0, r, :, c * LANES:(c + 1) * LANES] = (
                    qkv_scr[c, pl.ds(r, TA // dil, stride=dil), :].astype(BF16))
    cqk_ref[...] = h[:, 1280:1792]
    cv_ref[...] = h[:, 1792:2048].astype(BF16)
    co_ref[...] = h[:, 2048:2304]
    dx_ref[...] = h[:, 2304:2560]
    gates_t = jnp.transpose(h[:, 2560:2688])
    for j in range(TA // LANES):
        gt_ref[j] = gates_t[0:4 * ML_HEADS, j * LANES:(j + 1) * LANES]
    u = jax.nn.gelu(h[:, 0:256])
    v = jax.nn.gelu(h[:, 256:512])
    vn = _standardize(v) * lng_ref[...]
    lane_grp = lax.broadcasted_iota(jnp.int32, (GM_CHUNK, MIX_W), 1) // (MIX_W // GM_GROUPS)
    for c in range(TA // GM_CHUNK):
        vc = vn[c * GM_CHUNK:(c + 1) * GM_CHUNK]
        stacked = jnp.concatenate(
            [jnp.where(lane_grp == g, vc, 0.0).astype(BF16) for g in range(GM_GROUPS)], axis=0)
        mixed = jnp.dot(wscat_ref[...], stacked, preferred_element_type=F32) + bsfull_ref[...]
        ya_ref[c * GM_CHUNK:(c + 1) * GM_CHUNK, :] = (
            u[c * GM_CHUNK:(c + 1) * GM_CHUNK] * mixed).astype(BF16)


def _inproj_call(x_in, w_cat, b_cat, lng, wscat, bsfull, batch, alpha=None):
    fused = alpha is not None
    n = (x_in[0] if fused else x_in).shape[0]
    seq = n // batch
    tpb = seq // TA
    tok = lambda w: pl.BlockSpec((TA, w), lambda i: (i, 0))
    const = lambda s: pl.BlockSpec(s, lambda i: (0,) * len(s))
    regrouped = lambda dil: pl.BlockSpec((1, dil, TA // dil, 768), lambda i: (i // tpb, 0, i % tpb, 0))
    out_shape = (
        jax.ShapeDtypeStruct((n, 256), BF16),
    ) + tuple(jax.ShapeDtypeStruct((batch, dil, seq // dil, 768), BF16)
              for _, dil in DIL_PATTERNS) + (
        jax.ShapeDtypeStruct((n, 512), F32),
        jax.ShapeDtypeStruct((n, 256), BF16),
        jax.ShapeDtypeStruct((n, 256), F32),
        jax.ShapeDtypeStruct((n, 256), F32),
        jax.ShapeDtypeStruct((n // LANES, 4 * ML_HEADS, LANES), F32),
    )
    x_specs = [tok(D_MODEL), tok(D_MODEL), const((1, D_MODEL)), const((1, D_MODEL))] if fused else [tok(D_MODEL)]
    out_specs = ((tok(256),) + tuple(regrouped(dil) for _, dil in DIL_PATTERNS)
                 + (tok(512), tok(256), tok(256), tok(256),
                    pl.BlockSpec((TA // LANES, 4 * ML_HEADS, LANES), lambda i: (i, 0, 0))))
    if fused:
        out_specs += (tok(D_MODEL),)
        out_shape += (jax.ShapeDtypeStruct((n, D_MODEL), F32),)
    return pl.pallas_call(
        functools.partial(_inproj_kernel, alpha=alpha), name="inproj_gmlp",
        grid=(n // TA,),
        in_specs=x_specs + [const((D_MODEL, A_COLS)), const((1, A_COLS)), const((1, MIX_W)),
                            const((GM_CHUNK, GM_GROUPS * GM_CHUNK)), const((GM_CHUNK, MIX_W))],
        out_specs=out_specs,
        out_shape=out_shape,
        scratch_shapes=[pltpu.VMEM((768 // LANES, TA, LANES), F32)],
        compiler_params=_cparams(1),
    )(*(x_in if fused else (x_in,)), w_cat, b_cat, lng, wscat, bsfull)


def _pack_inproj_weights(w_in, b_in):
    pad = lambda a: jnp.pad(a, ((0, 0), (0, LANES - 4 * ML_HEADS)))
    w_cat = jnp.concatenate([w_in[:, 0:2304], w_in[:, 2320:2576], pad(w_in[:, 2304:2320])], axis=1)
    b2 = b_in[None, :]
    b_cat = jnp.concatenate([b2[:, 0:2304], b2[:, 2320:2576], pad(b2[:, 2304:2320])], axis=1)
    return w_cat.astype(BF16), b_cat


def _pack_gmlp(gm_ws, gm_bs):
    wscat = jnp.transpose(gm_ws, (1, 0, 2)).reshape(GM_CHUNK, GM_GROUPS * GM_CHUNK).astype(BF16)
    bsfull = jnp.repeat(jnp.transpose(gm_bs), MIX_W // GM_GROUPS, axis=1)
    return wscat, bsfull


def _halo_specs(t, width, n_tiles):
    r = t // HALO
    main = pl.BlockSpec((1, t, width), lambda b, i: (b, i, 0))
    prev = pl.BlockSpec((1, HALO, width), lambda b, i: (b, jnp.maximum(i * r - 1, 0), 0))
    nxt = pl.BlockSpec((1, HALO, width), lambda b, i: (b, jnp.minimum((i + 1) * r, n_tiles * r - 1), 0))
    return main, prev, nxt


def _fill_halo_scratch(buf, x_ref, p_ref, n_ref, t):
    i = pl.program_id(1)
    last = pl.num_programs(1) - 1
    buf[0:HALO, :] = jnp.where(i > 0, p_ref[0], 0.0)
    buf[HALO:HALO + t, :] = x_ref[0]
    buf[HALO + t:2 * HALO + t, :] = jnp.where(i < last, n_ref[0], 0.0)


TP = 4096


def _pool_kernel(x_ref, p_ref, n_ref, w_ref, sc_ref, o_ref, buf, lvl):
    _fill_halo_scratch(buf, x_ref, p_ref, n_ref, TP)
    seq = pl.num_programs(1) * TP
    pos = pl.program_id(1) * TP + lax.broadcasted_iota(jnp.int32, (TP, 1), 0)
    lane_grp = lax.broadcasted_iota(jnp.int32, (TP, MIX_W), 1) // (MIX_W // len(POOL_WINDOWS))
    x0 = buf[HALO:HALO + TP, :]
    sums = []
    src, rows = buf, TP + 2 * HALO
    for k, win in enumerate(POOL_WINDOWS):
        half = win // 2
        rows -= half
        cur = src[0:rows, :] + src[half:rows + half, :] if k else buf[0:rows, :] + buf[1:rows + 1, :]
        if k + 1 < len(POOL_WINDOWS):
            lvl[k, 0:rows, :] = cur
            sums.append(lvl[k, HALO - half:HALO - half + TP, :])
            src = lvl.at[k]
        else:
            sums.append(cur[0:TP])
    pooled = jnp.zeros((TP, MIX_W), F32)
    for gi, win in enumerate(POOL_WINDOWS):
        half = win // 2
        cnt = (jnp.minimum(pos + half, seq) - jnp.maximum(pos - half, 0)).astype(F32)
        pooled = jnp.where(lane_grp == gi, sums[gi] / cnt - x0, pooled)
    mixed = jnp.dot(pooled.astype(BF16), w_ref[...], preferred_element_type=F32)
    o_ref[0] = (mixed * sc_ref[...]).astype(BF16)


def _pool_call(dx, w_block, scale):
    b, s, _ = dx.shape
    nt = s // TP
    main, prev, nxt = _halo_specs(TP, MIX_W, nt)
    return pl.pallas_call(
        _pool_kernel, name="pool_mixer",
        grid=(b, nt),
        in_specs=[main, prev, nxt,
                  pl.BlockSpec((MIX_W, MIX_W), lambda b, i: (0, 0)),
                  pl.BlockSpec((1, MIX_W), lambda b, i: (0, 0))],
        out_specs=pl.BlockSpec((1, TP, MIX_W), lambda b, i: (b, i, 0)),
        out_shape=jax.ShapeDtypeStruct((b, s, MIX_W), BF16),
        scratch_shapes=[pltpu.VMEM((TP + 2 * HALO, MIX_W), F32),
                        pltpu.VMEM((len(POOL_WINDOWS) - 1, TP + 2 * HALO, MIX_W), F32)],
        compiler_params=_cparams(2),
    )(dx, dx, dx, w_block, scale)


TQ = 128
TQS = 2048
TKEYS = TQ + 2 * ATT_BLOCK


def _attn_kernel(q_ref, kp_ref, km_ref, kn_ref, vp_ref, vm_ref, vn_ref, bias_ref, o_ref, lse_ref):
    for sq in range(q_ref.shape[1]):
        _attn_sequence(sq, q_ref, kp_ref, km_ref, kn_ref, vp_ref, vm_ref, vn_ref, bias_ref, o_ref, lse_ref)


def _attn_sequence(sq, q_ref, kp_ref, km_ref, kn_ref, vp_ref, vm_ref, vn_ref, bias_ref, o_ref, lse_ref):
    i = pl.program_id(2)
    q = q_ref[0, sq] * ATT_HD ** -0.5
    k = jnp.concatenate([kp_ref[0, sq], km_ref[0, sq], kn_ref[0, sq]], axis=0)
    v = jnp.concatenate([vp_ref[0, sq], vm_ref[0, sq], vn_ref[0, sq]], axis=0)
    lane = lax.broadcasted_iota(jnp.int32, (TQ, LANES), 1)
    lane_half = lax.broadcasted_iota(jnp.int32, (1, LANES), 1) // ATT_HD
    keep = [jnp.where(lane_half == hh, 1.0, 0.0).astype(BF16) for hh in range(2)]
    n_sub = q_ref.shape[2] // TQ
    last_step = pl.num_programs(2) - 1
    for j in range(n_sub):
        if j == 0:
            variant = jnp.where(i == 0, 0, 1)
        elif j == n_sub - 1:
            variant = jnp.where(i == last_step, 2, 1)
        else:
            variant = 1
        qrows = slice(j * TQ, (j + 1) * TQ)
        krows = slice(j * TQ, j * TQ + TKEYS)
        lse_tile = jnp.zeros((TQ, LANES), F32)
        for pair in range(ATT_HEADS // 2):
            grp = slice(pair * LANES, (pair + 1) * LANES)
            q_pair, k_pair, v_pair = q[qrows, grp], k[krows, grp], v[krows, grp]
            o_pair = jnp.zeros((TQ, LANES), F32)
            for hh in range(2):
                h = 2 * pair + hh
                logits = lax.dot_general(q_pair * keep[hh], k_pair, (((1,), (1,)), ((), ())),
                                         preferred_element_type=F32) + bias_ref[variant, h]
                m = jnp.max(logits, axis=-1, keepdims=True)
                p = jnp.exp(logits - m)
                ssum = jnp.sum(p, axis=-1, keepdims=True)
                o = jnp.dot(p.astype(BF16), v_pair, preferred_element_type=F32) / ssum
                o_pair = jnp.where(lane_half == hh, o, o_pair)
                lse_tile = jnp.where(lane == h, m + jnp.log(ssum), lse_tile)
            o_ref[0, sq, qrows, grp] = o_pair
        lse_ref[0, sq, qrows, :] = lse_tile


def _attn_call(qkv, bias):
    b, dil, l, _ = qkv.shape
    tqs = min(l, TQS)
    nsq = min(dil, TQS // tqs)
    nt = l // tqs
    r64 = tqs // ATT_BLOCK
    main = lambda c: pl.BlockSpec((1, nsq, tqs, MIX_W), lambda b, r, i: (b, r, i, c))
    prev = lambda c: pl.BlockSpec((1, nsq, ATT_BLOCK, MIX_W),
                                  lambda b, r, i: (b, r, jnp.maximum(i * r64 - 1, 0), c))
    nxt = lambda c: pl.BlockSpec((1, nsq, ATT_BLOCK, MIX_W),
                                 lambda b, r, i: (b, r, jnp.minimum((i + 1) * r64, nt * r64 - 1), c))
    return pl.pallas_call(
        _attn_kernel, name="band_attention",
        grid=(b, dil // nsq, nt),
        in_specs=[main(0), prev(1), main(1), nxt(1), prev(2), main(2), nxt(2),
                  pl.BlockSpec((3, ATT_HEADS, TQ, TKEYS), lambda b, r, i: (0, 0, 0, 0))],
        out_specs=(pl.BlockSpec((1, nsq, tqs, MIX_W), lambda b, r, i: (b, r, i, 0)),
                   pl.BlockSpec((1, nsq, tqs, LANES), lambda b, r, i: (b, r, i, 0))),
        out_shape=(jax.ShapeDtypeStruct((b, dil, l, MIX_W), F32),
                   jax.ShapeDtypeStruct((b, dil, l, LANES), F32)),
        compiler_params=_cparams(3),
    )(qkv, qkv, qkv, qkv, qkv, qkv, qkv, bias)


def _t5_bucket_static(rel):
    half = REL_BUCKETS // 2
    max_exact = half // 2
    ret = np.where(rel > 0, half, 0)
    n = np.abs(rel)
    nf = np.maximum(n, 1).astype(np.float32)
    large = max_exact + (np.log(nf / np.float32(max_exact)) / np.float32(math.log(REL_MAX_DIST / max_exact))
                         * np.float32(half - max_exact)).astype(np.int32)
    large = np.minimum(large, half - 1)
    return ret + np.where(n < max_exact, n, large)


def _attn_bias_tile(rel_bias, window, dil):
    side = (window // 2) // dil
    rel = np.arange(TKEYS)[None, :] - ATT_BLOCK - np.arange(TQ)[:, None]
    n_rel = TKEYS + TQ - 1
    rel_values = np.arange(n_rel) - (ATT_BLOCK + TQ - 1)
    onehot = jax.nn.one_hot(jnp.asarray(_t5_bucket_static(dil * rel_values), jnp.int32), REL_BUCKETS, dtype=F32)
    table = jnp.einsum('nr,rh->hn', onehot, rel_bias, precision=lax.Precision.HIGHEST)
    periodic = jnp.tile(jnp.pad(table, ((0, 0), (0, 1))), (1, TQ))[:, :TQ * n_rel]
    bias = periodic.reshape(ATT_HEADS, TQ, n_rel)[:, :, TQ - 1:]
    key = np.arange(TKEYS)[None, :]
    inside = np.abs(rel) <= side
    masks = np.stack([inside & (key >= ATT_BLOCK), inside, inside & (key < ATT_BLOCK + TQ)])
    return jnp.where(jnp.asarray(masks)[:, None], bias[None], NEG_BIG)


TM = 4096
VT_ROWS = ML_HD + 16


def _mlprep_kernel(x_ref, p_ref, n_ref, v_ref, w_ref, qt_out, k_out, vt_out, buf):
    _fill_halo_scratch(buf, x_ref, p_ref, n_ref, TM)
    conv = (buf[HALO - 1:HALO - 1 + TM, :] * w_ref[0:1, :] + buf[HALO:HALO + TM, :] * w_ref[1:2, :]
            + buf[HALO + 1:HALO + 1 + TM, :] * w_ref[2:3, :])
    qk = jax.nn.silu(conv)
    qt = jnp.transpose(qk[:, :MIX_W])
    vt = jnp.transpose(v_ref[0].astype(F32))
    ones_rows = jnp.where(lax.broadcasted_iota(jnp.int32, (VT_ROWS - ML_HD, LANES), 0) == 0, 1.0, 0.0)
    for h in range(ML_HEADS):
        sl = slice(h * ML_HD, (h + 1) * ML_HD)
        k_out[0, h] = (qk[:, MIX_W + h * ML_HD:MIX_W + (h + 1) * ML_HD] * ML_HD ** -0.5).astype(BF16)
        for p in range(TM // LANES):
            pl_ = slice(p * LANES, (p + 1) * LANES)
            qt_out[0, h, p] = qt[sl, pl_].astype(BF16)
            vt_out[0, h, p] = jnp.concatenate([vt[sl, pl_], ones_rows], axis=0).astype(BF16)


def _mlprep_call(cqk, cv, conv_w):
    b, s, _ = cqk.shape
    nt = s // TM
    n_pairs = s // LANES
    ppt = TM // LANES
    main, prev, nxt = _halo_specs(TM, 2 * MIX_W, nt)
    return pl.pallas_call(
        _mlprep_kernel, name="mlstm_prep",
        grid=(b, nt),
        in_specs=[main, prev, nxt,
                  pl.BlockSpec((1, TM, MIX_W), lambda b, i: (b, i, 0)),
                  pl.BlockSpec((3, 2 * MIX_W), lambda b, i: (0, 0))],
        out_specs=(pl.BlockSpec((1, ML_HEADS, ppt, ML_HD, LANES), lambda b, i: (b, 0, i, 0, 0)),
                   pl.BlockSpec((1, ML_HEADS, TM, ML_HD), lambda b, i: (b, 0, i, 0)),
                   pl.BlockSpec((1, ML_HEADS, ppt, VT_ROWS, LANES), lambda b, i: (b, 0, i, 0, 0))),
        out_shape=(jax.ShapeDtypeStruct((b, ML_HEADS, n_pairs, ML_HD, LANES), BF16),
                   jax.ShapeDtypeStruct((b, ML_HEADS, s, ML_HD), BF16),
                   jax.ShapeDtypeStruct((b, ML_HEADS, n_pairs, VT_ROWS, LANES), BF16)),
        scratch_shapes=[pltpu.VMEM((TM + 2 * HALO, 2 * MIX_W), F32)],
        compiler_params=_cparams(2),
    )(cqk, cqk, cqk, cv, conv_w)


def _gate_kernel(g_ref, fb_ref, a_ref, m_ref, iw_ref, en_ref, ws_ref, dec_ref):
    x = g_ref[...]
    n_pairs = x.shape[0]
    n_ch = 2 * ML_HEADS
    lane = lax.broadcasted_iota(jnp.int32, (1, 1, LANES), 2)
    t_in = lane % ML_CHUNK
    second = lane >= ML_CHUNK
    fwd_row = lax.broadcasted_iota(jnp.int32, (1, n_ch, 1), 1) < ML_HEADS
    li = x[:, 0:n_ch, :]
    z = x[:, n_ch:2 * n_ch, :] + fb_ref[...]
    lf = jnp.minimum(z, 0.0) - jnp.log1p(jnp.exp(-jnp.abs(z)))

    def within_chunk(v, op, ident, prefix):
        s = 1
        while s < ML_CHUNK:
            nb = pltpu.roll(v, s if prefix else LANES - s, 2)
            ok = (t_in >= s) if prefix else (t_in < ML_CHUNK - s)
            v = op(v, jnp.where(ok, nb, ident))
            s *= 2
        return v

    pre = within_chunk(lf, jnp.add, 0.0, True)
    suf = within_chunk(lf, jnp.add, 0.0, False)
    g = pre + suf - lf
    b = jnp.where(fwd_row, pre, suf)
    a = li - b
    cm_pre = within_chunk(a, jnp.maximum, -jnp.inf, True)
    cm_suf = within_chunk(a, jnp.maximum, -jnp.inf, False)
    cm = jnp.where(fwd_row, cm_pre, cm_suf)
    amax = jnp.maximum(cm_pre, cm_suf)

    def shift_pairs(v, k, fill):
        pad = jnp.full((abs(k),) + v.shape[1:], fill, F32)
        return (jnp.concatenate([pad, v[:n_pairs - k]], axis=0) if k > 0
                else jnp.concatenate([v[-k:], pad], axis=0))

    def from_chunk(v, dist, fill, forward):
        if dist == 1:
            y = pltpu.roll(v, ML_CHUNK, 2)
            if forward:
                return jnp.where(second, y, shift_pairs(y, 1, fill))
            return jnp.where(second, shift_pairs(y, -1, fill), y)
        return shift_pairs(v, dist // 2 if forward else -(dist // 2), fill)

    def running_stabiliser(forward):
        big_g, big_a = g, amax + g
        dist = 1
        while dist < 2 * n_pairs:
            gp = from_chunk(big_g, dist, 0.0, forward)
            ap = from_chunk(big_a, dist, -jnp.inf, forward)
            big_g, big_a = gp + big_g, jnp.maximum(ap + big_g, big_a)
            dist *= 2
        ge = from_chunk(big_g, 1, 0.0, forward)
        ae = from_chunk(big_a, 1, -jnp.inf, forward)
        return jnp.maximum(ge, ae)

    m_chunk = jnp.where(fwd_row, running_stabiliser(True), running_stabiliser(False))
    m_t = jnp.maximum(cm, m_chunk)
    m_last = jnp.maximum(amax, m_chunk)
    a_ref[0] = a
    m_ref[0] = m_t
    iw_ref[0] = jnp.exp(m_chunk - m_t)
    en_ref[0] = jnp.exp(-(b + m_t))
    ws_ref[0] = jnp.exp(a - m_last)
    dec_ref[0] = jnp.exp(m_chunk - m_last)


def _gate_call(gates_t, fbias_col, batch):
    n_pairs = gates_t.shape[0] // batch
    n_ch = 2 * ML_HEADS
    out = pl.BlockSpec((1, n_pairs, n_ch, LANES), lambda i: (i, 0, 0, 0))
    return pl.pallas_call(
        _gate_kernel, name="mlstm_gates",
        grid=(batch,),
        in_specs=[pl.BlockSpec((n_pairs, 2 * n_ch, LANES), lambda i: (i, 0, 0)),
                  pl.BlockSpec((n_ch, 1), lambda i: (0, 0))],
        out_specs=(out,) * 6,
        out_shape=(jax.ShapeDtypeStruct((batch, n_pairs, n_ch, LANES), F32),) * 6,
        compiler_params=_cparams(1),
    )(gates_t, fbias_col)


TE = 2048


def _mlstm_kernel(*refs):
    fwd, bwd, (hf_ref, hb_ref, state) = refs[:9], refs[9:18], refs[18:]
    i = pl.program_id(1)

    @pl.when(i == 0)
    def _():
        state[...] = jnp.zeros(state.shape, F32)

    n_pairs = TE // LANES
    s_idx = lax.broadcasted_iota(jnp.int32, (LANES, LANES), 0)
    t_idx = lax.broadcasted_iota(jnp.int32, (LANES, LANES), 1)
    same_chunk = (s_idx >= ML_CHUNK) == (t_idx >= ML_CHUNK)
    upper_lanes = lax.broadcasted_iota(jnp.int32, (1, LANES), 1) >= ML_CHUNK

    def pair_body(p, carry):
        jobs = []
        for d, ((qt_r, k_r, vt_r, a_r, m_r, iw_r, en_r, ws_r, dec_r), out_r) in enumerate(
                ((fwd, hf_ref), (bwd, hb_ref))):
            pp = p if d == 0 else n_pairs - 1 - p
            srows = pl.ds(pl.multiple_of(pp * LANES, LANES), LANES)
            a_t = jnp.transpose(jnp.concatenate(
                [a_r[0, pp], jnp.zeros((LANES - 2 * ML_HEADS, LANES), F32)], axis=0))
            m_t, iw_t, en_t, ws_t, dec_t = [r[0, pp] for r in (m_r, iw_r, en_r, ws_r, dec_r)]
            dec_lo, dec_hi = dec_t[:, :ML_CHUNK], pltpu.roll(dec_t, ML_CHUNK, 1)[:, :ML_CHUNK]
            in_first = upper_lanes if d else ~upper_lanes
            tri = same_chunk & ((s_idx >= t_idx) if d else (s_idx <= t_idx))
            for h in range(ML_HEADS):
                ch = d * ML_HEADS + h
                row = lambda t: t[ch:ch + 1]
                jobs.append(dict(
                    ch=ch, tri=tri, in_first=in_first, k=k_r[0, h, srows, :], qt=qt_r[0, h, pp],
                    vt=vt_r[0, h, pp], a=a_t[:, ch:ch + 1], m=row(m_t), iw=row(iw_t), en=row(en_t), ws=row(ws_t),
                    dec_first=row(dec_hi if d else dec_lo), dec_second=row(dec_lo if d else dec_hi),
                    out=(out_r, pp, h)))
        for j in jobs:
            vt_f = j["vt"].astype(F32)
            j["c0"] = state[j["ch"]]
            j["st"] = jnp.dot(j["k"], j["qt"], preferred_element_type=F32)
            j["inter1"] = jnp.dot(j["c0"].astype(BF16), j["qt"], preferred_element_type=F32)
            j["upd1"] = jnp.dot((vt_f * jnp.where(j["in_first"], j["ws"], 0.0)).astype(BF16), j["k"],
                                preferred_element_type=F32)
            j["upd2"] = jnp.dot((vt_f * jnp.where(j["in_first"], 0.0, j["ws"])).astype(BF16), j["k"],
                                preferred_element_type=F32)
        for j in jobs:
            j["swt"] = j["st"] * jnp.exp(jnp.where(j["tri"], j["a"] - j["m"], NEG_BIG))
            j["intra"] = jnp.dot(j["vt"], j["swt"].astype(BF16), preferred_element_type=F32)
            j["c1"] = j["dec_first"] * j["c0"] + j["upd1"]
            j["inter2"] = jnp.dot(j["c1"].astype(BF16), j["qt"], preferred_element_type=F32)
        for j in jobs:
            inter = jnp.where(j["in_first"], j["inter1"], j["inter2"])
            den = jnp.sum(j["swt"], axis=0, keepdims=True) + j["iw"] * inter[ML_HD:ML_HD + 1]
            tot = j["intra"][:ML_HD] + j["iw"] * inter[:ML_HD]
            out_r, pp, h = j["out"]
            out_r[0, pp, h * ML_HD:(h + 1) * ML_HD, :] = tot / jnp.maximum(jnp.abs(den), j["en"])
            state[j["ch"]] = j["dec_second"] * j["c1"] + j["upd2"]
        return carry

    lax.fori_loop(0, n_pairs, pair_body, 0)


def _mlstm_call(qt, k, vt, a_t, m_t, iw_t, en_t, ws_t, dec_t):
    b, _, s, _ = k.shape
    nt = s // TE
    ppt = TE // LANES

    def specs(rev):
        ti = (lambda i: nt - 1 - i) if rev else (lambda i: i)
        tile = pl.BlockSpec((1, ppt, 2 * ML_HEADS, LANES), lambda b, i: (b, ti(i), 0, 0))
        return [
            pl.BlockSpec((1, ML_HEADS, ppt, ML_HD, LANES), lambda b, i: (b, 0, ti(i), 0, 0)),
            pl.BlockSpec((1, ML_HEADS, TE, ML_HD), lambda b, i: (b, 0, ti(i), 0)),
            pl.BlockSpec((1, ML_HEADS, ppt, VT_ROWS, LANES), lambda b, i: (b, 0, ti(i), 0, 0)),
            tile, tile, tile, tile, tile, tile]

    args = [qt, k, vt, a_t, m_t, iw_t, en_t, ws_t, dec_t]
    out_f = pl.BlockSpec((1, ppt, MIX_W, LANES), lambda b, i: (b, i, 0, 0))
    out_b = pl.BlockSpec((1, ppt, MIX_W, LANES), lambda b, i: (b, nt - 1 - i, 0, 0))
    return pl.pallas_call(
        _mlstm_kernel, name="mlstm_scan",
        grid=(b, nt),
        in_specs=specs(False) + specs(True),
        out_specs=(out_f, out_b),
        out_shape=(jax.ShapeDtypeStruct((b, s // LANES, MIX_W, LANES), F32),) * 2,
        scratch_shapes=[pltpu.VMEM((2 * ML_HEADS, VT_ROWS, ML_HD), F32)],
        compiler_params=_cparams(2),
    )(*args, *args)


def _mlstm_branch(cqk, cv, gates_t, conv_w, fbias):
    qt, k, vt = _mlprep_call(cqk, cv, conv_w)
    factors = _gate_call(gates_t, fbias.reshape(2 * ML_HEADS, 1), cqk.shape[0])
    return _mlstm_call(qt, k, vt, *factors)


TF = 512


def _merge_kernel(x_ref, ya_ref, o1_ref, o2_ref, o3_ref, l1_ref, l2_ref, l3_ref, hf_ref, hb_ref,
                  co_ref, yd_ref, wg_ref, bg_ref, wbr_ref, wout_ref, mng_ref, lng_ref, lnb_ref,
                  wr_ref, x1_ref, x1p_ref, aff_ref, o_scr, l_scr, *, alpha):
    x = x_ref[0]
    xb = x.astype(BF16)

    def natural_order(src_ref, scr):
        dil, width = src_ref.shape[1], src_ref.shape[3]
        if dil == 1:
            return src_ref[0, 0]
        for r in range(dil):
            for c in range(width // LANES):
                scr[c, pl.ds(r, TF // dil, stride=dil), :] = src_ref[0, r, :, c * LANES:(c + 1) * LANES]
        return jnp.concatenate([scr[c] for c in range(width // LANES)], axis=1)

    lane_head = lax.broadcasted_iota(jnp.int32, (TF, MIX_W), 1) // ML_HD
    l1, l2, l3 = [natural_order(r, l_scr.at[p]) for p, r in enumerate((l1_ref, l2_ref, l3_ref))]
    o1, o2, o3 = [natural_order(r, o_scr.at[p]) for p, r in enumerate((o1_ref, o2_ref, o3_ref))]
    lm = jnp.maximum(jnp.maximum(l1, l2), l3)
    e1, e2, e3 = jnp.exp(l1 - lm), jnp.exp(l2 - lm), jnp.exp(l3 - lm)
    inv = 1.0 / (e1 + e2 + e3)

    def per_head(w):
        out = jnp.zeros((TF, MIX_W), F32)
        for h in range(ATT_HEADS):
            out = jnp.where(lane_head == h, w[:, h:h + 1], out)
        return out

    y_b = per_head(e1 * inv) * o1 + per_head(e2 * inv) * o2 + per_head(e3 * inv) * o3
    hsum_t = jnp.concatenate([hf_ref[0, p] + hb_ref[0, p] for p in range(TF // LANES)], axis=1)
    per_head_rows = hsum_t.reshape(ML_HEADS, ML_HD, TF)
    mu = jnp.mean(per_head_rows, axis=1, keepdims=True)
    cen = per_head_rows - mu
    var = jnp.mean(cen * cen, axis=1, keepdims=True)
    hn_t = (cen * lax.rsqrt(var + LN_EPS)).reshape(MIX_W, TF)
    y_c_t = (jax.nn.sigmoid(jnp.transpose(co_ref[0])) * (hn_t * mng_ref[...])).astype(BF16)
    ys = (ya_ref[0], y_b.astype(BF16), None, yd_ref[0])
    merged = jnp.zeros((TF, D_MODEL), F32)
    for n in range(N_BRANCH):
        cols = slice(n * D_MODEL, (n + 1) * D_MODEL)
        gate = jax.nn.sigmoid(jnp.dot(xb, wg_ref[:, cols], preferred_element_type=F32) + bg_ref[:, cols])
        if ys[n] is None:
            proj = lax.dot_general(y_c_t, wbr_ref[n], (((0,), (0,)), ((), ())), preferred_element_type=F32)
        else:
            proj = jnp.dot(ys[n], wbr_ref[n], preferred_element_type=F32)
        merged = merged + gate * proj
    mix = jnp.dot(merged.astype(BF16), wout_ref[...], preferred_element_type=F32)
    x1 = _standardize(alpha * x + mix) * lng_ref[...] + lnb_ref[...]
    x1_ref[0] = x1
    x1b = x1.astype(BF16)
    x1p_ref[0] = _pack_bf16_pair(x1b[:, :D_MODEL // 2], x1b[:, D_MODEL // 2:])
    logits = lax.dot_general(wr_ref[...], x1b, (((1,), (1,)), ((), ())),
                             preferred_element_type=F32)
    ex = jnp.exp(logits - jnp.max(logits, axis=0, keepdims=True))
    aff_ref[0] = ex / jnp.sum(ex, axis=0, keepdims=True)


def _merge_call(x, ya, o_list, l_list, hf, hb, co, yd, wg, bg, wbr, wout, mng, lng, lnb, wr_t, alpha):
    b, s, _ = x.shape
    tok = lambda w: pl.BlockSpec((1, TF, w), lambda b, i: (b, i, 0))
    grouped = lambda dil, w: pl.BlockSpec((1, dil, TF // dil, w), lambda b, i: (b, 0, i, 0))
    chunked = pl.BlockSpec((1, TF // LANES, MIX_W, LANES), lambda b, i: (b, i, 0, 0))
    const = lambda shp: pl.BlockSpec(shp, lambda b, i: (0,) * len(shp))
    return pl.pallas_call(
        functools.partial(_merge_kernel, alpha=alpha), name="merge_ln_router",
        grid=(b, s // TF),
        in_specs=[tok(D_MODEL), tok(MIX_W)] + [grouped(dil, MIX_W) for _, dil in DIL_PATTERNS]
                 + [grouped(dil, LANES) for _, dil in DIL_PATTERNS]
                 + [chunked, chunked, tok(MIX_W), tok(MIX_W)]
                 + [const((D_MODEL, N_BRANCH * D_MODEL)), const((1, N_BRANCH * D_MODEL)),
                    const((N_BRANCH, MIX_W, D_MODEL)), const((D_MODEL, D_MODEL)), const((MIX_W, 1)),
                    const((1, D_MODEL)), const((1, D_MODEL)), const((N_EXPERTS, D_MODEL))],
        out_specs=(tok(D_MODEL), tok(D_MODEL // 2), pl.BlockSpec((1, N_EXPERTS, TF), lambda b, i: (b, 0, i))),
        out_shape=(jax.ShapeDtypeStruct((b, s, D_MODEL), F32),
                   jax.ShapeDtypeStruct((b, s, D_MODEL // 2), jnp.int32),
                   jax.ShapeDtypeStruct((b, N_EXPERTS, s), F32)),
        scratch_shapes=[pltpu.VMEM((len(DIL_PATTERNS), MIX_W // LANES, TF, LANES), F32),
                        pltpu.VMEM((len(DIL_PATTERNS), 1, TF, LANES), F32)],
        compiler_params=_cparams(2),
    )(x, ya, *o_list, *l_list, hf, hb, co, yd, wg, bg, wbr, wout, mng, lng, lnb, wr_t)


TT = 256


def _select_kernel(aff_ref, slot_ref, *, cap):
    s = aff_ref.shape[2]
    bits = pltpu.bitcast(aff_ref[0], jnp.int32)

    def bit_step(i, thr):
        cand = thr | jnp.left_shift(jnp.int32(1), 30 - i)
        cnt = jnp.sum((bits >= cand).astype(jnp.int32), axis=1, keepdims=True)
        return jnp.where(cnt >= cap, cand, thr)

    thr = lax.fori_loop(0, 31, bit_step, jnp.zeros((N_EXPERTS, 1), jnp.int32))
    gt = bits > thr
    eq = bits == thr
    need = (cap - jnp.sum(gt.astype(jnp.int32), axis=1, keepdims=True)).astype(F32)
    upper = (lax.broadcasted_iota(jnp.int32, (TT, TT), 0)
             <= lax.broadcasted_iota(jnp.int32, (TT, TT), 1)).astype(BF16)
    eq_before = jnp.zeros((N_EXPERTS, 1), F32)
    sel_before = jnp.zeros((N_EXPERTS, 1), F32)
    for j in range(s // TT):
        cols = slice(j * TT, (j + 1) * TT)
        eq_j = eq[:, cols]
        eq_incl = eq_before + jnp.dot(eq_j.astype(BF16), upper, preferred_element_type=F32)
        sel_j = gt[:, cols] | (eq_j & (eq_incl <= need))
        sel_f = sel_j.astype(F32)
        sel_incl = sel_before + jnp.dot(sel_f.astype(BF16), upper, preferred_element_type=F32)
        slot_ref[0, :, cols] = jnp.where(sel_j, sel_incl - 1.0, -1.0).astype(jnp.int32)
        eq_before = eq_incl[:, TT - 1:TT]
        sel_before = sel_incl[:, TT - 1:TT]


def _select_call(aff_t, cap):
    b, e, s = aff_t.shape
    return pl.pallas_call(
        functools.partial(_select_kernel, cap=cap), name="expert_choice_select",
        grid=(b,),
        in_specs=[pl.BlockSpec((1, e, s), lambda i: (i, 0, 0))],
        out_specs=pl.BlockSpec((1, e, s), lambda i: (i, 0, 0)),
        out_shape=jax.ShapeDtypeStruct((b, e, s), jnp.int32),
        compiler_params=_cparams(1),
    )(aff_t)


SC_LANES = 16
SC_ROWS = 64
SC_IDX = 128
SC_SLAB = 128
SC_ZROWS = 64
CF = 1024


def _sc_dispatch_call(x_flat, slot2, aff2, seq, cap):
    n_pairs = slot2.shape[0]
    d = x_flat.shape[1]
    info = plsc.get_sparse_core_info()
    n_workers = info.num_cores * info.num_subcores
    assert n_pairs % n_workers == 0 and seq % SC_LANES == 0 and cap % (2 * SC_ROWS) == 0
    pairs_per_worker = n_pairs // n_workers
    mesh = plsc.VectorSubcoreMesh(core_axis_name="c", subcore_axis_name="s")

    @functools.partial(
        pl.kernel, mesh=mesh, name="expert_dispatch_sc",
        compiler_params=pltpu.CompilerParams(needs_layout_passes=False),
        out_type=(jax.ShapeDtypeStruct((n_pairs * cap, d), x_flat.dtype),
                  jax.ShapeDtypeStruct((n_pairs, cap // SC_IDX, SC_IDX), jnp.int32),
                  jax.ShapeDtypeStruct((n_pairs, cap // SC_IDX, SC_IDX), F32)),
        scratch_types=[pltpu.VMEM((seq,), jnp.int32), pltpu.VMEM((seq,), F32),
                       pltpu.VMEM((cap,), jnp.int32), pltpu.VMEM((cap // SC_IDX, SC_IDX), jnp.int32),
                       pltpu.VMEM((cap // SC_IDX, SC_IDX), F32),
                       pltpu.VMEM((2, SC_ROWS, d), x_flat.dtype), pltpu.SemaphoreType.DMA((2,))])
    def dispatch(x_hbm, slot_hbm, aff_hbm, xs_hbm, tok_hbm, gate_hbm,
                 slot_v, aff_v, idx_v, tok_v, gate_v, rows_v, sem):
        worker = lax.axis_index("s") * info.num_cores + lax.axis_index("c")
        lane = lax.iota(jnp.int32, SC_LANES)

        def gather(c0, buf):
            return pltpu.make_async_copy(x_hbm.at[idx_v.at[pl.ds(c0, SC_ROWS)]], rows_v.at[buf], sem.at[buf])
        for k in range(pairs_per_worker):
            pair = worker * pairs_per_worker + k
            row0 = (pair // N_EXPERTS) * seq
            pltpu.sync_copy(slot_hbm.at[pair], slot_v)
            pltpu.sync_copy(aff_hbm.at[pair], aff_v)

            @plsc.parallel_loop(0, seq, step=SC_LANES, unroll=4)
            def _(t0):
                sv = slot_v[pl.ds(t0, SC_LANES)]
                picked = sv >= 0
                hi, lo = lax.shift_right_logical(sv, 7), sv & (SC_IDX - 1)
                plsc.store_scatter(tok_v, [hi, lo], t0 + lane, mask=picked)
                plsc.store_scatter(idx_v, [sv], row0 + t0 + lane, mask=picked)
                plsc.store_scatter(gate_v, [hi, lo], aff_v[pl.ds(t0, SC_LANES)], mask=picked)

            pltpu.sync_copy(tok_v, tok_hbm.at[pair])
            pltpu.sync_copy(gate_v, gate_hbm.at[pair])

            gather(0, 0).start()

            @pl.loop(0, cap, step=2 * SC_ROWS)
            def _(c0):
                gather(c0 + SC_ROWS, 1).start()
                gather(c0, 0).wait()
                pltpu.sync_copy(rows_v.at[0], xs_hbm.at[pl.ds(pair * cap + c0, SC_ROWS)])

                @pl.when(c0 + 2 * SC_ROWS < cap)
                def _():
                    gather(c0 + 2 * SC_ROWS, 0).start()

                gather(c0 + SC_ROWS, 1).wait()
                pltpu.sync_copy(rows_v.at[1], xs_hbm.at[pl.ds(pair * cap + c0 + SC_ROWS, SC_ROWS)])

    return dispatch(x_flat, slot2, aff2)


def _expert_kernel(xs_ref, g_ref, w1_ref, w3_ref, w2_ref, ye_ref, w1_bf, w3_bf, w2_bf):
    @pl.when((pl.program_id(1) == 0) & (pl.program_id(2) == 0))
    def _():
        w1_bf[...] = w1_ref[0, 0].astype(BF16)
        w3_bf[...] = w3_ref[0, 0].astype(BF16)
        w2_bf[...] = w2_ref[0, 0].astype(BF16)

    xs = jnp.concatenate(_unpack_bf16_pair(xs_ref[0, 0]), axis=1)
    hid = (jax.nn.silu(jnp.dot(xs, w1_bf[...], preferred_element_type=F32))
           * jnp.dot(xs, w3_bf[...], preferred_element_type=F32))
    g_rows = g_ref[0, 0]
    n_rows = g_rows.shape[0]
    g_t = jnp.transpose(jnp.concatenate([g_rows, jnp.zeros((LANES - n_rows, LANES), F32)], axis=0))
    g_col = jnp.concatenate([g_t[:, r:r + 1] for r in range(n_rows)], axis=0)
    ye_ref[0, 0] = jnp.dot(hid.astype(BF16), w2_bf[...], preferred_element_type=F32) * g_col


def _expert_call(xs4, gate4, w1, w3, w2, layer):
    b, e, cap, half = xs4.shape
    d, ff = w1.shape[2], w1.shape[3]
    assert d == 2 * half
    rows = lambda w: pl.BlockSpec((1, 1, CF, w), lambda e, b, j: (b, e, j, 0))
    wspec = lambda r, c: pl.BlockSpec((1, 1, r, c), lambda e, b, j: (layer, e, 0, 0))
    return pl.pallas_call(
        _expert_kernel, name="expert_ffn",
        grid=(e, b, cap // CF),
        in_specs=[rows(half), pl.BlockSpec((1, 1, CF // LANES, LANES), lambda e, b, j: (b, e, j, 0)),
                  wspec(d, ff), wspec(d, ff), wspec(ff, d)],
        out_specs=rows(d),
        out_shape=jax.ShapeDtypeStruct((b, e, cap, d), F32),
        scratch_shapes=[pltpu.VMEM((d, ff), BF16), pltpu.VMEM((d, ff), BF16), pltpu.VMEM((ff, d), BF16)],
        compiler_params=_cparams(3),
    )(xs4, gate4, w1, w3, w2)


def _sc_combine_call(ye_flat, tok3, seq):
    n_pairs, n_chunks, _ = tok3.shape
    cap = n_chunks * SC_IDX
    d = ye_flat.shape[1]
    nb = n_pairs // N_EXPERTS
    info = plsc.get_sparse_core_info()
    assert info.num_subcores == N_EXPERTS and nb % info.num_cores == 0 and n_chunks % 2 == 0
    assert seq % (info.num_subcores * SC_ZROWS) == 0 and d % SC_SLAB == 0
    batches_per_core = nb // info.num_cores
    own_rows = seq // info.num_subcores
    mesh = plsc.VectorSubcoreMesh(core_axis_name="c", subcore_axis_name="s")

    @functools.partial(
        pl.kernel, mesh=mesh, name="expert_combine_sc",
        compiler_params=pltpu.CompilerParams(needs_layout_passes=False),
        out_type=jax.ShapeDtypeStruct((nb * seq, d), F32),
        scratch_types=[pltpu.VMEM_SHARED((seq, SC_SLAB), F32),
                       pltpu.VMEM((n_chunks, SC_IDX), jnp.int32),
                       pltpu.VMEM((2, SC_IDX, SC_SLAB), F32),
                       pltpu.VMEM((SC_ZROWS, SC_SLAB), F32),
                       pltpu.SemaphoreType.DMA((2,))])
    def combine(ye_hbm, tok_hbm, out_hbm, acc_sh, tok_v, rows_v, zero_v, sem):
        core = lax.axis_index("c")
        sub = lax.axis_index("s")

        @pl.loop(0, SC_ZROWS)
        def _(r):
            for l0 in range(0, SC_SLAB, SC_LANES):
                zero_v[r, pl.ds(l0, SC_LANES)] = jnp.zeros((SC_LANES,), F32)

        for bb in range(batches_per_core):
            batch = core * batches_per_core + bb
            pair = batch * N_EXPERTS + sub
            pltpu.sync_copy(tok_hbm.at[pair], tok_v)

            @pl.loop(0, d // SC_SLAB)
            def _(slab):
                cols = pl.ds(pl.multiple_of(slab * SC_SLAB, SC_SLAB), SC_SLAB)

                @pl.loop(0, own_rows, step=SC_ZROWS)
                def _(r0):
                    pltpu.sync_copy(zero_v, acc_sh.at[pl.ds(sub * own_rows + r0, SC_ZROWS)])

                def load(j, buf):
                    return pltpu.make_async_copy(
                        ye_hbm.at[pl.ds(pair * cap + j * SC_IDX, SC_IDX), cols], rows_v.at[buf], sem.at[buf])

                load(0, 0).start()
                plsc.subcore_barrier()

                for j in range(0, n_chunks, 2):
                    load(j + 1, 1).start()
                    load(j, 0).wait()
                    pltpu.sync_copy(rows_v.at[0], acc_sh.at[tok_v.at[j]], add=True)
                    if j + 2 < n_chunks:
                        load(j + 2, 0).start()
                    load(j + 1, 1).wait()
                    pltpu.sync_copy(rows_v.at[1], acc_sh.at[tok_v.at[j + 1]], add=True)

                plsc.subcore_barrier()
                pltpu.sync_copy(acc_sh.at[pl.ds(sub * own_rows, own_rows)],
                                out_hbm.at[pl.ds(batch * seq + sub * own_rows, own_rows), cols])

    return combine(ye_flat, tok3)


TN = 1024


def _resln_kernel(x_ref, y_ref, g_ref, b_ref, o_ref, *, alpha):
    o_ref[...] = _standardize(alpha * x_ref[...] + y_ref[...]) * g_ref[...] + b_ref[...]


def _resln_call(x2d, y2d, g, bta, alpha):
    n, d = x2d.shape
    tok = pl.BlockSpec((TN, d), lambda i: (i, 0))
    vec = pl.BlockSpec((1, d), lambda i: (0, 0))
    return pl.pallas_call(
        functools.partial(_resln_kernel, alpha=alpha), name="residual_layernorm",
        grid=(n // TN,), in_specs=[tok, tok, vec, vec], out_specs=tok,
        out_shape=jax.ShapeDtypeStruct((n, d), F32),
        compiler_params=_cparams(1),
    )(x2d, y2d, g, bta)


def _expert_choice_ffn(x1p, aff_t, w1, w3, w2, layer):
    b, s, half = x1p.shape
    d = 2 * half
    cap = EC_FACTOR * s // N_EXPERTS
    slot = _select_call(aff_t, cap)
    xs, tok, gate = _sc_dispatch_call(x1p.reshape(b * s, half), slot.reshape(b * N_EXPERTS, s),
                                      aff_t.reshape(b * N_EXPERTS, s), s, cap)
    ye = _expert_call(xs.reshape(b, N_EXPERTS, cap, half), gate.reshape(b, N_EXPERTS, cap // SC_IDX, SC_IDX), w1, w3, w2,
                      layer)
    out = _sc_combine_call(ye.reshape(b * N_EXPERTS * cap, d),
                           tok, s)
    return out.reshape(b, s, d)


def _pack_pool(pool_w):
    g, gd, _ = pool_w.shape
    out = jnp.zeros((g * gd, g * gd), F32)
    for i in range(g):
        out = out.at[i * gd:(i + 1) * gd, i * gd:(i + 1) * gd].set(pool_w[i])
    return out.astype(BF16)


def _layer(layer, x, pending, alpha, bias_tiles, w_in, b_in, gm_ln_g, gm_ws, gm_bs, ml_conv, ml_fbias,
           ml_norm_g, pool_w, pool_scale, w_branch, w_out, ln1_g, ln1_b, w_router, w_e1, w_e3, w_e2):
    b, s, d = x.shape
    n_small = 2576
    w_cat, b_cat = _pack_inproj_weights(w_in, b_in)
    wscat, bsfull = _pack_gmlp(gm_ws, gm_bs)
    if pending is None:
        outs = _inproj_call(x.reshape(b * s, d), w_cat, b_cat, gm_ln_g[None], wscat, bsfull, b)
    else:
        *outs, x2 = _inproj_call(pending, w_cat, b_cat, gm_ln_g[None], wscat, bsfull, b, alpha)
        x = x2.reshape(b, s, d)
    ya, qkv1, qkv4, qkv16, cqk, cv, co, dx, gates_t = outs
    r3 = lambda t: t.reshape(b, s, t.shape[-1])
    o_list, l_list = [], []
    for qkv, bias in zip((qkv1, qkv4, qkv16), bias_tiles):
        o, lse = _attn_call(qkv, bias)
        o_list.append(o)
        l_list.append(lse)
    hf, hb = _mlstm_branch(r3(cqk), r3(cv), gates_t, ml_conv, ml_fbias)
    yd = _pool_call(r3(dx), _pack_pool(pool_w), pool_scale[None])
    x1, x1p, aff_t = _merge_call(
        x, r3(ya), o_list, l_list, hf, hb, r3(co), yd,
        w_in[:, n_small:].astype(BF16), b_in[None, n_small:], w_branch.astype(BF16), w_out.astype(BF16),
        ml_norm_g[:, None], ln1_g[None], ln1_b[None], jnp.transpose(w_router).astype(BF16), alpha)
    ffn = _expert_choice_ffn(x1p, aff_t, w_e1, w_e3, w_e2, layer)
    return x1.reshape(b * s, d), ffn.reshape(b * s, d)


def kernel(x, w_in, b_in, gm_ln_g, gm_ws, gm_bs, rel_bias, ml_conv, ml_fbias, ml_norm_g, pool_w,
           pool_scale, w_branch, w_out, ln1_g, ln1_b, w_router, w_e1, w_e3, w_e2, ln2_g, ln2_b):
    depth = w_in.shape[0]
    alpha = (2 * depth) ** 0.25
    bias_tiles = [_attn_bias_tile(rel_bias, window, dil) for window, dil in DIL_PATTERNS]
    b, s, d = x.shape
    pending = None
    for l in range(depth):
        x1, ffn = _layer(l, x, pending, alpha, bias_tiles, w_in[l], b_in[l], gm_ln_g[l], gm_ws[l], gm_bs[l],
                         ml_conv[l], ml_fbias[l], ml_norm_g[l], pool_w[l], pool_scale[l], w_branch[l],
                         w_out[l], ln1_g[l], ln1_b[l], w_router[l], w_e1, w_e3, w_e2)
        pending = (x1, ffn, ln2_g[l][None], ln2_b[l][None])
    return _resln_call(*pending, alpha).reshape(b, s, d)
```

```python
import functools
import math

import jax
import jax.numpy as jnp
import numpy as np
from jax import lax
from jax.experimental import pallas as pl
from jax.experimental.pallas import tpu as pltpu
from jax.experimental.pallas import tpu_sc as plsc

F32 = jnp.float32
BF16 = jnp.bfloat16

D_MODEL = 1024
MIX_W = 256
N_BRANCH = 4
GM_CHUNK = 128
GM_GROUPS = 4
ATT_HEADS = 4
ATT_HD = 64
DIL_PATTERNS = ((128, 1), (512, 4), (2048, 16))
ATT_BLOCK = 64
REL_BUCKETS = 32
REL_MAX_DIST = 1024
ML_HEADS = 4
ML_HD = 64
ML_CHUNK = 64
POOL_WINDOWS = (2, 4, 8, 16)
N_EXPERTS = 16
EXPERT_FF = 1024
EC_FACTOR = 2
LN_EPS = 1e-5
NEG_BIG = -1e30

V7X_VMEM_LIMIT = 56 * 1024 * 1024
LANES = 128
HALO = 8


def _cparams(n_grid, vmem=V7X_VMEM_LIMIT):
    return pltpu.CompilerParams(dimension_semantics=("arbitrary",) * n_grid,
                                vmem_limit_bytes=vmem)


def _pack_bf16_pair(lo, hi):
    lo_bits = lax.shift_right_logical(pltpu.bitcast(lo.astype(F32), jnp.int32), 16)
    return pltpu.bitcast(hi.astype(F32), jnp.int32) | lo_bits


def _unpack_bf16_pair(packed):
    lo = pltpu.bitcast(lax.shift_left(packed, 16), F32).astype(BF16)
    hi = pltpu.bitcast(packed & jnp.int32(-65536), F32).astype(BF16)
    return lo, hi


def _standardize(xf):
    mu = jnp.mean(xf, axis=-1, keepdims=True)
    var = jnp.mean(jnp.square(xf - mu), axis=-1, keepdims=True)
    return (xf - mu) * lax.rsqrt(var + LN_EPS)


TA = 512
A_COLS = 2560 + LANES


def _inproj_kernel(*refs, alpha):
    if alpha is None:
        x_ref, *refs = refs
        x = x_ref[...]
    else:
        x1_ref, y_ref, g2_ref, b2_ref, *refs = refs
        x = _standardize(alpha * x1_ref[...] + y_ref[...]) * g2_ref[...] + b2_ref[...]
        refs[-2][...] = x
        refs = refs[:-2] + refs[-1:]
    (w_ref, b_ref, lng_ref, wscat_ref, bsfull_ref,
     ya_ref, qkv1_ref, qkv4_ref, qkv16_ref, cqk_ref, cv_ref, co_ref, dx_ref, gt_ref, qkv_scr) = refs
    xb = x.astype(BF16)
    h = jnp.dot(xb, w_ref[...], preferred_element_type=F32) + b_ref[...]
    qkv1_ref[0, 0] = h[:, 512:1280].astype(BF16)
    for c in range(768 // LANES):
        qkv_scr[c] = h[:, 512 + c * LANES:512 + (c + 1) * LANES]
    for (_, dil), out_ref in zip(DIL_PATTERNS[1:], (qkv4_ref, qkv16_ref)):
        for r in range(dil):
            for c in range(768 // LANES):
                out_ref[0, r, :, c * LANES:(c + 1) * LANES] = (
                    qkv_scr[c, pl.ds(r, TA // dil, stride=dil), :].astype(BF16))
    cqk_ref[...] = h[:, 1280:1792]
    cv_ref[...] = h[:, 1792:2048].astype(BF16)
    co_ref[...] = h[:, 2048:2304]
    dx_ref[...] = h[:, 2304:2560]
    gates_t = jnp.transpose(h[:, 2560:2688])
    for j in range(TA // LANES):
        gt_ref[j] = gates_t[0:4 * ML_HEADS, j * LANES:(j + 1) * LANES]
    u = jax.nn.gelu(h[:, 0:256])
    v = jax.nn.gelu(h[:, 256:512])
    vn = _standardize(v) * lng_ref[...]
    lane_grp = lax.broadcasted_iota(jnp.int32, (GM_CHUNK, MIX_W), 1) // (MIX_W // GM_GROUPS)
    for c in range(TA // GM_CHUNK):
        vc = vn[c * GM_CHUNK:(c + 1) * GM_CHUNK]
        stacked = jnp.concatenate(
            [jnp.where(lane_grp == g, vc, 0.0).astype(BF16) for g in range(GM_GROUPS)], axis=0)
        mixed = jnp.dot(wscat_ref[...], stacked, preferred_element_type=F32) + bsfull_ref[...]
        ya_ref[c * GM_CHUNK:(c + 1) * GM_CHUNK, :] = (
            u[c * GM_CHUNK:(c + 1) * GM_CHUNK] * mixed).astype(BF16)


def _inproj_call(x_in, w_cat, b_cat, lng, wscat, bsfull, batch, alpha=None):
    fused = alpha is not None
    n = (x_in[0] if fused else x_in).shape[0]
    seq = n // batch
    tpb = seq // TA
    tok = lambda w: pl.BlockSpec((TA, w), lambda i: (i, 0))
    const = lambda s: pl.BlockSpec(s, lambda i: (0,) * len(s))
    regrouped = lambda dil: pl.BlockSpec((1, dil, TA // dil, 768), lambda i: (i // tpb, 0, i % tpb, 0))
    out_shape = (
        jax.ShapeDtypeStruct((n, 256), BF16),
    ) + tuple(jax.ShapeDtypeStruct((batch, dil, seq // dil, 768), BF16)
              for _, dil in DIL_PATTERNS) + (
        jax.ShapeDtypeStruct((n, 512), F32),
        jax.ShapeDtypeStruct((n, 256), BF16),
        jax.ShapeDtypeStruct((n, 256), F32),
        jax.ShapeDtypeStruct((n, 256), F32),
        jax.ShapeDtypeStruct((n // LANES, 4 * ML_HEADS, LANES), F32),
    )
    x_specs = [tok(D_MODEL), tok(D_MODEL), const((1, D_MODEL)), const((1, D_MODEL))] if fused else [tok(D_MODEL)]
    out_specs = ((tok(256),) + tuple(regrouped(dil) for _, dil in DIL_PATTERNS)
                 + (tok(512), tok(256), tok(256), tok(256),
                    pl.BlockSpec((TA // LANES, 4 * ML_HEADS, LANES), lambda i: (i, 0, 0))))
    if fused:
        out_specs += (tok(D_MODEL),)
        out_shape += (jax.ShapeDtypeStruct((n, D_MODEL), F32),)
    return pl.pallas_call(
        functools.partial(_inproj_kernel, alpha=alpha), name="inproj_gmlp",
        grid=(n // TA,),
        in_specs=x_specs + [const((D_MODEL, A_COLS)), const((1, A_COLS)), const((1, MIX_W)),
                            const((GM_CHUNK, GM_GROUPS * GM_CHUNK)), const((GM_CHUNK, MIX_W))],
        out_specs=out_specs,
        out_shape=out_shape,
        scratch_shapes=[pltpu.VMEM((768 // LANES, TA, LANES), F32)],
        compiler_params=_cparams(1),
    )(*(x_in if fused else (x_in,)), w_cat, b_cat, lng, wscat, bsfull)


def _pack_inproj_weights(w_in, b_in):
    pad = lambda a: jnp.pad(a, ((0, 0), (0, LANES - 4 * ML_HEADS)))
    w_cat = jnp.concatenate([w_in[:, 0:2304], w_in[:, 2320:2576], pad(w_in[:, 2304:2320])], axis=1)
    b2 = b_in[None, :]
    b_cat = jnp.concatenate([b2[:, 0:2304], b2[:, 2320:2576], pad(b2[:, 2304:2320])], axis=1)
    return w_cat.astype(BF16), b_cat


def _pack_gmlp(gm_ws, gm_bs):
    wscat = jnp.transpose(gm_ws, (1, 0, 2)).reshape(GM_CHUNK, GM_GROUPS * GM_CHUNK).astype(BF16)
    bsfull = jnp.repeat(jnp.transpose(gm_bs), MIX_W // GM_GROUPS, axis=1)
    return wscat, bsfull


def _halo_specs(t, width, n_tiles):
    r = t // HALO
    main = pl.BlockSpec((1, t, width), lambda b, i: (b, i, 0))
    prev = pl.BlockSpec((1, HALO, width), lambda b, i: (b, jnp.maximum(i * r - 1, 0), 0))
    nxt = pl.BlockSpec((1, HALO, width), lambda b, i: (b, jnp.minimum((i + 1) * r, n_tiles * r - 1), 0))
    return main, prev, nxt


def _fill_halo_scratch(buf, x_ref, p_ref, n_ref, t):
    i = pl.program_id(1)
    last = pl.num_programs(1) - 1
    buf[0:HALO, :] = jnp.where(i > 0, p_ref[0], 0.0)
    buf[HALO:HALO + t, :] = x_ref[0]
    buf[HALO + t:2 * HALO + t, :] = jnp.where(i < last, n_ref[0], 0.0)


TP = 2048


def _pool_kernel(x_ref, p_ref, n_ref, w_ref, sc_ref, o_ref, buf, lvl):
    _fill_halo_scratch(buf, x_ref, p_ref, n_ref, TP)
    seq = pl.num_programs(1) * TP
    pos = pl.program_id(1) * TP + lax.broadcasted_iota(jnp.int32, (TP, 1), 0)
    lane_grp = lax.broadcasted_iota(jnp.int32, (TP, MIX_W), 1) // (MIX_W // len(POOL_WINDOWS))
    x0 = buf[HALO:HALO + TP, :]
    sums = []
    src, rows = buf, TP + 2 * HALO
    for k, win in enumerate(POOL_WINDOWS):
        half = win // 2
        rows -= half
        cur = src[0:rows, :] + src[half:rows + half, :] if k else buf[0:rows, :] + buf[1:rows + 1, :]
        if k + 1 < len(POOL_WINDOWS):
            lvl[k, 0:rows, :] = cur
            sums.append(lvl[k, HALO - half:HALO - half + TP, :])
            src = lvl.at[k]
        else:
            sums.append(cur[0:TP])
    pooled = jnp.zeros((TP, MIX_W), F32)
    for gi, win in enumerate(POOL_WINDOWS):
        half = win // 2
        cnt = (jnp.minimum(pos + half, seq) - jnp.maximum(pos - half, 0)).astype(F32)
        pooled = jnp.where(lane_grp == gi, sums[gi] / cnt - x0, pooled)
    mixed = jnp.dot(pooled.astype(BF16), w_ref[...], preferred_element_type=F32)
    o_ref[0] = (mixed * sc_ref[...]).astype(BF16)


def _pool_call(dx, w_block, scale):
    b, s, _ = dx.shape
    nt = s // TP
    main, prev, nxt = _halo_specs(TP, MIX_W, nt)
    return pl.pallas_call(
        _pool_kernel, name="pool_mixer",
        grid=(b, nt),
        in_specs=[main, prev, nxt,
                  pl.BlockSpec((MIX_W, MIX_W), lambda b, i: (0, 0)),
                  pl.BlockSpec((1, MIX_W), lambda b, i: (0, 0))],
        out_specs=pl.BlockSpec((1, TP, MIX_W), lambda b, i: (b, i, 0)),
        out_shape=jax.ShapeDtypeStruct((b, s, MIX_W), BF16),
        scratch_shapes=[pltpu.VMEM((TP + 2 * HALO, MIX_W), F32),
                        pltpu.VMEM((len(POOL_WINDOWS) - 1, TP + 2 * HALO, MIX_W), F32)],
        compiler_params=_cparams(2),
    )(dx, dx, dx, w_block, scale)


TQ = 128
TQS = 2048
TKEYS = TQ + 2 * ATT_BLOCK


def _attn_kernel(q_ref, kp_ref, km_ref, kn_ref, vp_ref, vm_ref, vn_ref, bias_ref, o_ref, lse_ref):
    for sq in range(q_ref.shape[1]):
        _attn_sequence(sq, q_ref, kp_ref, km_ref, kn_ref, vp_ref, vm_ref, vn_ref, bias_ref, o_ref, lse_ref)


def _attn_sequence(sq, q_ref, kp_ref, km_ref, kn_ref, vp_ref, vm_ref, vn_ref, bias_ref, o_ref, lse_ref):
    i = pl.program_id(2)
    q = q_ref[0, sq] * ATT_HD ** -0.5
    k = jnp.concatenate([kp_ref[0, sq], km_ref[0, sq], kn_ref[0, sq]], axis=0)
    v = jnp.concatenate([vp_ref[0, sq], vm_ref[0, sq], vn_ref[0, sq]], axis=0)
    lane = lax.broadcasted_iota(jnp.int32, (TQ, LANES), 1)
    lane_half = lax.broadcasted_iota(jnp.int32, (1, LANES), 1) // ATT_HD
    keep = [jnp.where(lane_half == hh, 1.0, 0.0).astype(BF16) for hh in range(2)]
    n_sub = q_ref.shape[2] // TQ
    last_step = pl.num_programs(2) - 1
    for j in range(n_sub):
        if j == 0:
            variant = jnp.where(i == 0, 0, 1)
        elif j == n_sub - 1:
            variant = jnp.where(i == last_step, 2, 1)
        else:
            variant = 1
        qrows = slice(j * TQ, (j + 1) * TQ)
        krows = slice(j * TQ, j * TQ + TKEYS)
        lse_tile = jnp.zeros((TQ, LANES), F32)
        for pair in range(ATT_HEADS // 2):
            grp = slice(pair * LANES, (pair + 1) * LANES)
            q_pair, k_pair, v_pair = q[qrows, grp], k[krows, grp], v[krows, grp]
            o_pair = jnp.zeros((TQ, LANES), F32)
            for hh in range(2):
                h = 2 * pair + hh
                logits = lax.dot_general(q_pair * keep[hh], k_pair, (((1,), (1,)), ((), ())),
                                         preferred_element_type=F32) + bias_ref[variant, h]
                m = jnp.max(logits, axis=-1, keepdims=True)
                p = jnp.exp(logits - m)
                ssum = jnp.sum(p, axis=-1, keepdims=True)
                o = jnp.dot(p.astype(BF16), v_pair, preferred_element_type=F32) / ssum
                o_pair = jnp.where(lane_half == hh, o, o_pair)
                lse_tile = jnp.where(lane == h, m + jnp.log(ssum), lse_tile)
            o_ref[0, sq, qrows, grp] = o_pair
        lse_ref[0, sq, qrows, :] = lse_tile


def _attn_call(qkv, bias):
    b, dil, l, _ = qkv.shape
    tqs = min(l, TQS)
    nsq = min(dil, TQS // tqs)
    nt = l // tqs
    r64 = tqs // ATT_BLOCK
    main = lambda c: pl.BlockSpec((1, nsq, tqs, MIX_W), lambda b, r, i: (b, r, i, c))
    prev = lambda c: pl.BlockSpec((1, nsq, ATT_BLOCK, MIX_W),
                                  lambda b, r, i: (b, r, jnp.maximum(i * r64 - 1, 0), c))
    nxt = lambda c: pl.BlockSpec((1, nsq, ATT_BLOCK, MIX_W),
                                 lambda b, r, i: (b, r, jnp.minimum((i + 1) * r64, nt * r64 - 1), c))
    return pl.pallas_call(
        _attn_kernel, name="band_attention",
        grid=(b, dil // nsq, nt),
        in_specs=[main(0), prev(1), main(1), nxt(1), prev(2), main(2), nxt(2),
                  pl.BlockSpec((3, ATT_HEADS, TQ, TKEYS), lambda b, r, i: (0, 0, 0, 0))],
        out_specs=(pl.BlockSpec((1, nsq, tqs, MIX_W), lambda b, r, i: (b, r, i, 0)),
                   pl.BlockSpec((1, nsq, tqs, LANES), lambda b, r, i: (b, r, i, 0))),
        out_shape=(jax.ShapeDtypeStruct((b, dil, l, MIX_W), F32),
                   jax.ShapeDtypeStruct((b, dil, l, LANES), F32)),
        compiler_params=_cparams(3),
    )(qkv, qkv, qkv, qkv, qkv, qkv, qkv, bias)


def _t5_bucket_static(rel):
    half = REL_BUCKETS // 2
    max_exact = half // 2
    ret = np.where(rel > 0, half, 0)
    n = np.abs(rel)
    nf = np.maximum(n, 1).astype(np.float32)
    large = max_exact + (np.log(nf / np.float32(max_exact)) / np.float32(math.log(REL_MAX_DIST / max_exact))
                         * np.float32(half - max_exact)).astype(np.int32)
    large = np.minimum(large, half - 1)
    return ret + np.where(n < max_exact, n, large)


def _attn_bias_tile(rel_bias, window, dil):
    side = (window // 2) // dil
    rel = np.arange(TKEYS)[None, :] - ATT_BLOCK - np.arange(TQ)[:, None]
    n_rel = TKEYS + TQ - 1
    rel_values = np.arange(n_rel) - (ATT_BLOCK + TQ - 1)
    onehot = jax.nn.one_hot(jnp.asarray(_t5_bucket_static(dil * rel_values), jnp.int32), REL_BUCKETS, dtype=F32)
    table = jnp.einsum('nr,rh->hn', onehot, rel_bias, precision=lax.Precision.HIGHEST)
    periodic = jnp.tile(jnp.pad(table, ((0, 0), (0, 1))), (1, TQ))[:, :TQ * n_rel]
    bias = periodic.reshape(ATT_HEADS, TQ, n_rel)[:, :, TQ - 1:]
    key = np.arange(TKEYS)[None, :]
    inside = np.abs(rel) <= side
    masks = np.stack([inside & (key >= ATT_BLOCK), inside, inside & (key < ATT_BLOCK + TQ)])
    return jnp.where(jnp.asarray(masks)[:, None], bias[None], NEG_BIG)


TM = 2048
VT_ROWS = ML_HD + 16


def _mlprep_kernel(x_ref, p_ref, n_ref, v_ref, w_ref, qt_out, k_out, vt_out, buf):
    _fill_halo_scratch(buf, x_ref, p_ref, n_ref, TM)
    conv = (buf[HALO - 1:HALO - 1 + TM, :] * w_ref[0:1, :] + buf[HALO:HALO + TM, :] * w_ref[1:2, :]
            + buf[HALO + 1:HALO + 1 + TM, :] * w_ref[2:3, :])
    qk = jax.nn.silu(conv)
    qt = jnp.transpose(qk[:, :MIX_W])
    vt = jnp.transpose(v_ref[0].astype(F32))
    ones_rows = jnp.where(lax.broadcasted_iota(jnp.int32, (VT_ROWS - ML_HD, LANES), 0) == 0, 1.0, 0.0)
    for h in range(ML_HEADS):
        sl = slice(h * ML_HD, (h + 1) * ML_HD)
        k_out[0, h] = (qk[:, MIX_W + h * ML_HD:MIX_W + (h + 1) * ML_HD] * ML_HD ** -0.5).astype(BF16)
        for p in range(TM // LANES):
            pl_ = slice(p * LANES, (p + 1) * LANES)
            qt_out[0, h, p] = qt[sl, pl_].astype(BF16)
            vt_out[0, h, p] = jnp.concatenate([vt[sl, pl_], ones_rows], axis=0).astype(BF16)


def _mlprep_call(cqk, cv, conv_w):
    b, s, _ = cqk.shape
    nt = s // TM
    n_pairs = s // LANES
    ppt = TM // LANES
    main, prev, nxt = _halo_specs(TM, 2 * MIX_W, nt)
    return pl.pallas_call(
        _mlprep_kernel, name="mlstm_prep",
        grid=(b, nt),
        in_specs=[main, prev, nxt,
                  pl.BlockSpec((1, TM, MIX_W), lambda b, i: (b, i, 0)),
                  pl.BlockSpec((3, 2 * MIX_W), lambda b, i: (0, 0))],
        out_specs=(pl.BlockSpec((1, ML_HEADS, ppt, ML_HD, LANES), lambda b, i: (b, 0, i, 0, 0)),
                   pl.BlockSpec((1, ML_HEADS, TM, ML_HD), lambda b, i: (b, 0, i, 0)),
                   pl.BlockSpec((1, ML_HEADS, ppt, VT_ROWS, LANES), lambda b, i: (b, 0, i, 0, 0))),
        out_shape=(jax.ShapeDtypeStruct((b, ML_HEADS, n_pairs, ML_HD, LANES), BF16),
                   jax.ShapeDtypeStruct((b, ML_HEADS, s, ML_HD), BF16),
                   jax.ShapeDtypeStruct((b, ML_HEADS, n_pairs, VT_ROWS, LANES), BF16)),
        scratch_shapes=[pltpu.VMEM((TM + 2 * HALO, 2 * MIX_W), F32)],
        compiler_params=_cparams(2),
    )(cqk, cqk, cqk, cv, conv_w)


def _gate_kernel(g_ref, fb_ref, a_ref, m_ref, iw_ref, en_ref, ws_ref, dec_ref):
    x = g_ref[...]
    n_pairs = x.shape[0]
    n_ch = 2 * ML_HEADS
    lane = lax.broadcasted_iota(jnp.int32, (1, 1, LANES), 2)
    t_in = lane % ML_CHUNK
    second = lane >= ML_CHUNK
    fwd_row = lax.broadcasted_iota(jnp.int32, (1, n_ch, 1), 1) < ML_HEADS
    li = x[:, 0:n_ch, :]
    z = x[:, n_ch:2 * n_ch, :] + fb_ref[...]
    lf = jnp.minimum(z, 0.0) - jnp.log1p(jnp.exp(-jnp.abs(z)))

    def within_chunk(v, op, ident, prefix):
        s = 1
        while s < ML_CHUNK:
            nb = pltpu.roll(v, s if prefix else LANES - s, 2)
            ok = (t_in >= s) if prefix else (t_in < ML_CHUNK - s)
            v = op(v, jnp.where(ok, nb, ident))
            s *= 2
        return v

    pre = within_chunk(lf, jnp.add, 0.0, True)
    suf = within_chunk(lf, jnp.add, 0.0, False)
    g = pre + suf - lf
    b = jnp.where(fwd_row, pre, suf)
    a = li - b
    cm_pre = within_chunk(a, jnp.maximum, -jnp.inf, True)
    cm_suf = within_chunk(a, jnp.maximum, -jnp.inf, False)
    cm = jnp.where(fwd_row, cm_pre, cm_suf)
    amax = jnp.maximum(cm_pre, cm_suf)

    def shift_pairs(v, k, fill):
        pad = jnp.full((abs(k),) + v.shape[1:], fill, F32)
        return (jnp.concatenate([pad, v[:n_pairs - k]], axis=0) if k > 0
                else jnp.concatenate([v[-k:], pad], axis=0))

    def from_chunk(v, dist, fill, forward):
        if dist == 1:
            y = pltpu.roll(v, ML_CHUNK, 2)
            if forward:
                return jnp.where(second, y, shift_pairs(y, 1, fill))
            return jnp.where(second, shift_pairs(y, -1, fill), y)
        return shift_pairs(v, dist // 2 if forward else -(dist // 2), fill)

    def running_stabiliser(forward):
        big_g, big_a = g, amax + g
        dist = 1
        while dist < 2 * n_pairs:
            gp = from_chunk(big_g, dist, 0.0, forward)
            ap = from_chunk(big_a, dist, -jnp.inf, forward)
            big_g, big_a = gp + big_g, jnp.maximum(ap + big_g, big_a)
            dist *= 2
        ge = from_chunk(big_g, 1, 0.0, forward)
        ae = from_chunk(big_a, 1, -jnp.inf, forward)
        return jnp.maximum(ge, ae)

    m_chunk = jnp.where(fwd_row, running_stabiliser(True), running_stabiliser(False))
    m_t = jnp.maximum(cm, m_chunk)
    m_last = jnp.maximum(amax, m_chunk)
    a_ref[0] = a
    m_ref[0] = m_t
    iw_ref[0] = jnp.exp(m_chunk - m_t)
    en_ref[0] = jnp.exp(-(b + m_t))
    ws_ref[0] = jnp.exp(a - m_last)
    dec_ref[0] = jnp.exp(m_chunk - m_last)


def _gate_call(gates_t, fbias_col, batch):
    n_pairs = gates_t.shape[0] // batch
    n_ch = 2 * ML_HEADS
    out = pl.BlockSpec((1, n_pairs, n_ch, LANES), lambda i: (i, 0, 0, 0))
    return pl.pallas_call(
        _gate_kernel, name="mlstm_gates",
        grid=(batch,),
        in_specs=[pl.BlockSpec((n_pairs, 2 * n_ch, LANES), lambda i: (i, 0, 0)),
                  pl.BlockSpec((n_ch, 1), lambda i: (0, 0))],
        out_specs=(out,) * 6,
        out_shape=(jax.ShapeDtypeStruct((batch, n_pairs, n_ch, LANES), F32),) * 6,
        compiler_params=_cparams(1),
    )(gates_t, fbias_col)


TE = 2048


def _mlstm_kernel(*refs):
    fwd, bwd, (hf_ref, hb_ref, state) = refs[:9], refs[9:18], refs[18:]
    i = pl.program_id(1)

    @pl.when(i == 0)
    def _():
        state[...] = jnp.zeros(state.shape, F32)

    n_pairs = TE // LANES
    s_idx = lax.broadcasted_iota(jnp.int32, (LANES, LANES), 0)
    t_idx = lax.broadcasted_iota(jnp.int32, (LANES, LANES), 1)
    same_chunk = (s_idx >= ML_CHUNK) == (t_idx >= ML_CHUNK)
    upper_lanes = lax.broadcasted_iota(jnp.int32, (1, LANES), 1) >= ML_CHUNK

    def pair_body(p, carry):
        jobs = []
        for d, ((qt_r, k_r, vt_r, a_r, m_r, iw_r, en_r, ws_r, dec_r), out_r) in enumerate(
                ((fwd, hf_ref), (bwd, hb_ref))):
            pp = p if d == 0 else n_pairs - 1 - p
            srows = pl.ds(pl.multiple_of(pp * LANES, LANES), LANES)
            a_t = jnp.transpose(jnp.concatenate(
                [a_r[0, pp], jnp.zeros((LANES - 2 * ML_HEADS, LANES), F32)], axis=0))
            m_t, iw_t, en_t, ws_t, dec_t = [r[0, pp] for r in (m_r, iw_r, en_r, ws_r, dec_r)]
            dec_lo, dec_hi = dec_t[:, :ML_CHUNK], pltpu.roll(dec_t, ML_CHUNK, 1)[:, :ML_CHUNK]
            in_first = upper_lanes if d else ~upper_lanes
            tri = same_chunk & ((s_idx >= t_idx) if d else (s_idx <= t_idx))
            for h in range(ML_HEADS):
                ch = d * ML_HEADS + h
                row = lambda t: t[ch:ch + 1]
                jobs.append(dict(
                    ch=ch, tri=tri, in_first=in_first, k=k_r[0, h, srows, :], qt=qt_r[0, h, pp],
                    vt=vt_r[0, h, pp], a=a_t[:, ch:ch + 1], m=row(m_t), iw=row(iw_t), en=row(en_t), ws=row(ws_t),
                    dec_first=row(dec_hi if d else dec_lo), dec_second=row(dec_lo if d else dec_hi),
                    out=(out_r, pp, h)))
        for j in jobs:
            vt_f = j["vt"].astype(F32)
            j["c0"] = state[j["ch"]]
            j["st"] = jnp.dot(j["k"], j["qt"], preferred_element_type=F32)
            j["inter1"] = jnp.dot(j["c0"].astype(BF16), j["qt"], preferred_element_type=F32)
            j["upd1"] = jnp.dot((vt_f * jnp.where(j["in_first"], j["ws"], 0.0)).astype(BF16), j["k"],
                                preferred_element_type=F32)
            j["upd2"] = jnp.dot((vt_f * jnp.where(j["in_first"], 0.0, j["ws"])).astype(BF16), j["k"],
                                preferred_element_type=F32)
        for j in jobs:
            j["swt"] = j["st"] * jnp.exp(jnp.where(j["tri"], j["a"] - j["m"], NEG_BIG))
            j["intra"] = jnp.dot(j["vt"], j["swt"].astype(BF16), preferred_element_type=F32)
            j["c1"] = j["dec_first"] * j["c0"] + j["upd1"]
            j["inter2"] = jnp.dot(j["c1"].astype(BF16), j["qt"], preferred_element_type=F32)
        for j in jobs:
            inter = jnp.where(j["in_first"], j["inter1"], j["inter2"])
            den = jnp.sum(j["swt"], axis=0, keepdims=True) + j["iw"] * inter[ML_HD:ML_HD + 1]
            tot = j["intra"][:ML_HD] + j["iw"] * inter[:ML_HD]
            out_r, pp, h = j["out"]
            out_r[0, pp, h * ML_HD:(h + 1) * ML_HD, :] = tot / jnp.maximum(jnp.abs(den), j["en"])
            state[j["ch"]] = j["dec_second"] * j["c1"] + j["upd2"]
        return carry

    lax.fori_loop(0, n_pairs, pair_body, 0, unroll=2)


def _mlstm_call(qt, k, vt, a_t, m_t, iw_t, en_t, ws_t, dec_t):
    b, _, s, _ = k.shape
    nt = s // TE
    ppt = TE // LANES

    def specs(rev):
        ti = (lambda i: nt - 1 - i) if rev else (lambda i: i)
        tile = pl.BlockSpec((1, ppt, 2 * ML_HEADS, LANES), lambda b, i: (b, ti(i), 0, 0))
        return [
            pl.BlockSpec((1, ML_HEADS, ppt, ML_HD, LANES), lambda b, i: (b, 0, ti(i), 0, 0)),
            pl.BlockSpec((1, ML_HEADS, TE, ML_HD), lambda b, i: (b, 0, ti(i), 0)),
            pl.BlockSpec((1, ML_HEADS, ppt, VT_ROWS, LANES), lambda b, i: (b, 0, ti(i), 0, 0)),
            tile, tile, tile, tile, tile, tile]

    args = [qt, k, vt, a_t, m_t, iw_t, en_t, ws_t, dec_t]
    out_f = pl.BlockSpec((1, ppt, MIX_W, LANES), lambda b, i: (b, i, 0, 0))
    out_b = pl.BlockSpec((1, ppt, MIX_W, LANES), lambda b, i: (b, nt - 1 - i, 0, 0))
    return pl.pallas_call(
        _mlstm_kernel, name="mlstm_scan",
        grid=(b, nt),
        in_specs=specs(False) + specs(True),
        out_specs=(out_f, out_b),
        out_shape=(jax.ShapeDtypeStruct((b, s // LANES, MIX_W, LANES), F32),) * 2,
        scratch_shapes=[pltpu.VMEM((2 * ML_HEADS, VT_ROWS, ML_HD), F32)],
        compiler_params=_cparams(2),
    )(*args, *args)


def _mlstm_branch(cqk, cv, gates_t, conv_w, fbias):
    qt, k, vt = _mlprep_call(cqk, cv, conv_w)
    factors = _gate_call(gates_t, fbias.reshape(2 * ML_HEADS, 1), cqk.shape[0])
    return _mlstm_call(qt, k, vt, *factors)


TF = 512


def _merge_kernel(x_ref, ya_ref, o1_ref, o2_ref, o3_ref, l1_ref, l2_ref, l3_ref, hf_ref, hb_ref,
                  co_ref, yd_ref, wg_ref, bg_ref, wbr_ref, wout_ref, mng_ref, lng_ref, lnb_ref,
                  wr_ref, x1_ref, x1p_ref, aff_ref, o_scr, l_scr, *, alpha):
    x = x_ref[0]
    xb = x.astype(BF16)

    def natural_order(src_ref, scr):
        dil, width = src_ref.shape[1], src_ref.shape[3]
        if dil == 1:
            return src_ref[0, 0]
        for r in range(dil):
            for c in range(width // LANES):
                scr[c, pl.ds(r, TF // dil, stride=dil), :] = src_ref[0, r, :, c * LANES:(c + 1) * LANES]
        return jnp.concatenate([scr[c] for c in range(width // LANES)], axis=1)

    lane_head = lax.broadcasted_iota(jnp.int32, (TF, MIX_W), 1) // ML_HD
    l1, l2, l3 = [natural_order(r, l_scr.at[p]) for p, r in enumerate((l1_ref, l2_ref, l3_ref))]
    o1, o2, o3 = [natural_order(r, o_scr.at[p]) for p, r in enumerate((o1_ref, o2_ref, o3_ref))]
    lm = jnp.maximum(jnp.maximum(l1, l2), l3)
    e1, e2, e3 = jnp.exp(l1 - lm), jnp.exp(l2 - lm), jnp.exp(l3 - lm)
    inv = 1.0 / (e1 + e2 + e3)

    def per_head(w):
        out = jnp.zeros((TF, MIX_W), F32)
        for h in range(ATT_HEADS):
            out = jnp.where(lane_head == h, w[:, h:h + 1], out)
        return out

    y_b = per_head(e1 * inv) * o1 + per_head(e2 * inv) * o2 + per_head(e3 * inv) * o3
    hsum_t = jnp.concatenate([hf_ref[0, p] + hb_ref[0, p] for p in range(TF // LANES)], axis=1)
    per_head_rows = hsum_t.reshape(ML_HEADS, ML_HD, TF)
    mu = jnp.mean(per_head_rows, axis=1, keepdims=True)
    cen = per_head_rows - mu
    var = jnp.mean(cen * cen, axis=1, keepdims=True)
    hn_t = (cen * lax.rsqrt(var + LN_EPS)).reshape(MIX_W, TF)
    y_c_t = (jax.nn.sigmoid(jnp.transpose(co_ref[0])) * (hn_t * mng_ref[...])).astype(BF16)
    ys = (ya_ref[0], y_b.astype(BF16), None, yd_ref[0])
    merged = jnp.zeros((TF, D_MODEL), F32)
    for n in range(N_BRANCH):
        cols = slice(n * D_MODEL, (n + 1) * D_MODEL)
        gate = jax.nn.sigmoid(jnp.dot(xb, wg_ref[:, cols], preferred_element_type=F32) + bg_ref[:, cols])
        if ys[n] is None:
            proj = lax.dot_general(y_c_t, wbr_ref[n], (((0,), (0,)), ((), ())), preferred_element_type=F32)
        else:
            proj = jnp.dot(ys[n], wbr_ref[n], preferred_element_type=F32)
        merged = merged + gate * proj
    mix = jnp.dot(merged.astype(BF16), wout_ref[...], preferred_element_type=F32)
    x1 = _standardize(alpha * x + mix) * lng_ref[...] + lnb_ref[...]
    x1_ref[0] = x1
    x1b = x1.astype(BF16)
    x1p_ref[0] = _pack_bf16_pair(x1b[:, :D_MODEL // 2], x1b[:, D_MODEL // 2:])
    logits = lax.dot_general(wr_ref[...], x1b, (((1,), (1,)), ((), ())),
                             preferred_element_type=F32)
    ex = jnp.exp(logits - jnp.max(logits, axis=0, keepdims=True))
    aff_ref[0] = ex / jnp.sum(ex, axis=0, keepdims=True)


def _merge_call(x, ya, o_list, l_list, hf, hb, co, yd, wg, bg, wbr, wout, mng, lng, lnb, wr_t, alpha):
    b, s, _ = x.shape
    tok = lambda w: pl.BlockSpec((1, TF, w), lambda b, i: (b, i, 0))
    grouped = lambda dil, w: pl.BlockSpec((1, dil, TF // dil, w), lambda b, i: (b, 0, i, 0))
    chunked = pl.BlockSpec((1, TF // LANES, MIX_W, LANES), lambda b, i: (b, i, 0, 0))
    const = lambda shp: pl.BlockSpec(shp, lambda b, i: (0,) * len(shp))
    return pl.pallas_call(
        functools.partial(_merge_kernel, alpha=alpha), name="merge_ln_router",
        grid=(b, s // TF),
        in_specs=[tok(D_MODEL), tok(MIX_W)] + [grouped(dil, MIX_W) for _, dil in DIL_PATTERNS]
                 + [grouped(dil, LANES) for _, dil in DIL_PATTERNS]
                 + [chunked, chunked, tok(MIX_W), tok(MIX_W)]
                 + [const((D_MODEL, N_BRANCH * D_MODEL)), const((1, N_BRANCH * D_MODEL)),
                    const((N_BRANCH, MIX_W, D_MODEL)), const((D_MODEL, D_MODEL)), const((MIX_W, 1)),
                    const((1, D_MODEL)), const((1, D_MODEL)), const((N_EXPERTS, D_MODEL))],
        out_specs=(tok(D_MODEL), tok(D_MODEL // 2), pl.BlockSpec((1, N_EXPERTS, TF), lambda b, i: (b, 0, i))),
        out_shape=(jax.ShapeDtypeStruct((b, s, D_MODEL), F32),
                   jax.ShapeDtypeStruct((b, s, D_MODEL // 2), jnp.int32),
                   jax.ShapeDtypeStruct((b, N_EXPERTS, s), F32)),
        scratch_shapes=[pltpu.VMEM((len(DIL_PATTERNS), MIX_W // LANES, TF, LANES), F32),
                        pltpu.VMEM((len(DIL_PATTERNS), 1, TF, LANES), F32)],
        compiler_params=_cparams(2),
    )(x, ya, *o_list, *l_list, hf, hb, co, yd, wg, bg, wbr, wout, mng, lng, lnb, wr_t)


TT = 256


def _select_kernel(aff_ref, slot_ref, *, cap):
    s = aff_ref.shape[2]
    bits = pltpu.bitcast(aff_ref[0], jnp.int32)

    def bit_step(i, thr):
        cand = thr | jnp.left_shift(jnp.int32(1), 30 - i)
        cnt = jnp.sum((bits >= cand).astype(jnp.int32), axis=1, keepdims=True)
        return jnp.where(cnt >= cap, cand, thr)

    thr = lax.fori_loop(0, 31, bit_step, jnp.zeros((N_EXPERTS, 1), jnp.int32))
    gt = bits > thr
    eq = bits == thr
    need = (cap - jnp.sum(gt.astype(jnp.int32), axis=1, keepdims=True)).astype(F32)
    upper = (lax.broadcasted_iota(jnp.int32, (TT, TT), 0)
             <= lax.broadcasted_iota(jnp.int32, (TT, TT), 1)).astype(BF16)
    eq_before = jnp.zeros((N_EXPERTS, 1), F32)
    sel_before = jnp.zeros((N_EXPERTS, 1), F32)
    for j in range(s // TT):
        cols = slice(j * TT, (j + 1) * TT)
        eq_j = eq[:, cols]
        eq_incl = eq_before + jnp.dot(eq_j.astype(BF16), upper, preferred_element_type=F32)
        sel_j = gt[:, cols] | (eq_j & (eq_incl <= need))
        sel_f = sel_j.astype(F32)
        sel_incl = sel_before + jnp.dot(sel_f.astype(BF16), upper, preferred_element_type=F32)
        slot_ref[0, :, cols] = jnp.where(sel_j, sel_incl - 1.0, -1.0).astype(jnp.int32)
        eq_before = eq_incl[:, TT - 1:TT]
        sel_before = sel_incl[:, TT - 1:TT]


def _select_call(aff_t, cap):
    b, e, s = aff_t.shape
    return pl.pallas_call(
        functools.partial(_select_kernel, cap=cap), name="expert_choice_select",
        grid=(b,),
        in_specs=[pl.BlockSpec((1, e, s), lambda i: (i, 0, 0))],
        out_specs=pl.BlockSpec((1, e, s), lambda i: (i, 0, 0)),
        out_shape=jax.ShapeDtypeStruct((b, e, s), jnp.int32),
        compiler_params=_cparams(1),
    )(aff_t)


SC_LANES = 16
SC_ROWS = 64
SC_IDX = 128
SC_SLAB = 128
SC_ZROWS = 64
CF = 1024


def _sc_dispatch_call(x_flat, slot2, aff2, seq, cap):
    n_pairs = slot2.shape[0]
    d = x_flat.shape[1]
    info = plsc.get_sparse_core_info()
    n_workers = info.num_cores * info.num_subcores
    assert n_pairs % n_workers == 0 and seq % SC_LANES == 0 and cap % (2 * SC_ROWS) == 0
    pairs_per_worker = n_pairs // n_workers
    mesh = plsc.VectorSubcoreMesh(core_axis_name="c", subcore_axis_name="s")

    @functools.partial(
        pl.kernel, mesh=mesh, name="expert_dispatch_sc",
        compiler_params=pltpu.CompilerParams(needs_layout_passes=False),
        out_type=(jax.ShapeDtypeStruct((n_pairs * cap, d), x_flat.dtype),
                  jax.ShapeDtypeStruct((n_pairs, cap // SC_IDX, SC_IDX), jnp.int32),
                  jax.ShapeDtypeStruct((n_pairs, cap // SC_IDX, SC_IDX), F32)),
        scratch_types=[pltpu.VMEM((seq,), jnp.int32), pltpu.VMEM((seq,), F32),
                       pltpu.VMEM((cap,), jnp.int32), pltpu.VMEM((cap // SC_IDX, SC_IDX), jnp.int32),
                       pltpu.VMEM((cap // SC_IDX, SC_IDX), F32),
                       pltpu.VMEM((2, SC_ROWS, d), x_flat.dtype), pltpu.SemaphoreType.DMA((2,))])
    def dispatch(x_hbm, slot_hbm, aff_hbm, xs_hbm, tok_hbm, gate_hbm,
                 slot_v, aff_v, idx_v, tok_v, gate_v, rows_v, sem):
        worker = lax.axis_index("s") * info.num_cores + lax.axis_index("c")
        lane = lax.iota(jnp.int32, SC_LANES)

        def gather(c0, buf):
            return pltpu.make_async_copy(x_hbm.at[idx_v.at[pl.ds(c0, SC_ROWS)]], rows_v.at[buf], sem.at[buf])
        for k in range(pairs_per_worker):
            pair = worker * pairs_per_worker + k
            row0 = (pair // N_EXPERTS) * seq
            pltpu.sync_copy(slot_hbm.at[pair], slot_v)
            pltpu.sync_copy(aff_hbm.at[pair], aff_v)

            @plsc.parallel_loop(0, seq, step=SC_LANES, unroll=4)
            def _(t0):
                sv = slot_v[pl.ds(t0, SC_LANES)]
                picked = sv >= 0
                hi, lo = lax.shift_right_logical(sv, 7), sv & (SC_IDX - 1)
                plsc.store_scatter(tok_v, [hi, lo], t0 + lane, mask=picked)
                plsc.store_scatter(idx_v, [sv], row0 + t0 + lane, mask=picked)
                plsc.store_scatter(gate_v, [hi, lo], aff_v[pl.ds(t0, SC_LANES)], mask=picked)

            pltpu.sync_copy(tok_v, tok_hbm.at[pair])
            pltpu.sync_copy(gate_v, gate_hbm.at[pair])

            gather(0, 0).start()

            @pl.loop(0, cap, step=2 * SC_ROWS)
            def _(c0):
                gather(c0 + SC_ROWS, 1).start()
                gather(c0, 0).wait()
                pltpu.sync_copy(rows_v.at[0], xs_hbm.at[pl.ds(pair * cap + c0, SC_ROWS)])

                @pl.when(c0 + 2 * SC_ROWS < cap)
                def _():
                    gather(c0 + 2 * SC_ROWS, 0).start()

                gather(c0 + SC_ROWS, 1).wait()
                pltpu.sync_copy(rows_v.at[1], xs_hbm.at[pl.ds(pair * cap + c0 + SC_ROWS, SC_ROWS)])

    return dispatch(x_flat, slot2, aff2)


def _expert_kernel(xs_ref, g_ref, w1_ref, w3_ref, w2_ref, ye_ref, w1_bf, w3_bf, w2_bf):
    @pl.when((pl.program_id(1) == 0) & (pl.program_id(2) == 0))
    def _():
        w1_bf[...] = w1_ref[0, 0].astype(BF16)
        w3_bf[...] = w3_ref[0, 0].astype(BF16)
        w2_bf[...] = w2_ref[0, 0].astype(BF16)

    xs = jnp.concatenate(_unpack_bf16_pair(xs_ref[0, 0]), axis=1)
    hid = (jax.nn.silu(jnp.dot(xs, w1_bf[...], preferred_element_type=F32))
           * jnp.dot(xs, w3_bf[...], preferred_element_type=F32))
    g_rows = g_ref[0, 0]
    n_rows = g_rows.shape[0]
    g_t = jnp.transpose(jnp.concatenate([g_rows, jnp.zeros((LANES - n_rows, LANES), F32)], axis=0))
    g_col = jnp.concatenate([g_t[:, r:r + 1] for r in range(n_rows)], axis=0)
    ye_ref[0, 0] = jnp.dot(hid.astype(BF16), w2_bf[...], preferred_element_type=F32) * g_col


def _expert_call(xs4, gate4, w1, w3, w2, layer):
    b, e, cap, half = xs4.shape
    d, ff = w1.shape[2], w1.shape[3]
    assert d == 2 * half
    rows = lambda w: pl.BlockSpec((1, 1, CF, w), lambda e, b, j: (b, e, j, 0))
    wspec = lambda r, c: pl.BlockSpec((1, 1, r, c), lambda e, b, j: (layer, e, 0, 0))
    return pl.pallas_call(
        _expert_kernel, name="expert_ffn",
        grid=(e, b, cap // CF),
        in_specs=[rows(half), pl.BlockSpec((1, 1, CF // LANES, LANES), lambda e, b, j: (b, e, j, 0)),
                  wspec(d, ff), wspec(d, ff), wspec(ff, d)],
        out_specs=rows(d),
        out_shape=jax.ShapeDtypeStruct((b, e, cap, d), F32),
        scratch_shapes=[pltpu.VMEM((d, ff), BF16), pltpu.VMEM((d, ff), BF16), pltpu.VMEM((ff, d), BF16)],
        compiler_params=_cparams(3),
    )(xs4, gate4, w1, w3, w2)


def _sc_combine_call(ye_flat, tok3, seq):
    n_pairs, n_chunks, _ = tok3.shape
    cap = n_chunks * SC_IDX
    d = ye_flat.shape[1]
    nb = n_pairs // N_EXPERTS
    info = plsc.get_sparse_core_info()
    assert info.num_subcores == N_EXPERTS and nb % info.num_cores == 0 and n_chunks % 2 == 0
    assert seq % (info.num_subcores * SC_ZROWS) == 0 and d % SC_SLAB == 0
    batches_per_core = nb // info.num_cores
    own_rows = seq // info.num_subcores
    mesh = plsc.VectorSubcoreMesh(core_axis_name="c", subcore_axis_name="s")

    @functools.partial(
        pl.kernel, mesh=mesh, name="expert_combine_sc",
        compiler_params=pltpu.CompilerParams(needs_layout_passes=False),
        out_type=jax.ShapeDtypeStruct((nb * seq, d), F32),
        scratch_types=[pltpu.VMEM_SHARED((seq, SC_SLAB), F32),
                       pltpu.VMEM((n_chunks, SC_IDX), jnp.int32),
                       pltpu.VMEM((2, SC_IDX, SC_SLAB), F32),
                       pltpu.VMEM((SC_ZROWS, SC_SLAB), F32),
                       pltpu.SemaphoreType.DMA((2,))])
    def combine(ye_hbm, tok_hbm, out_hbm, acc_sh, tok_v, rows_v, zero_v, sem):
        core = lax.axis_index("c")
        sub = lax.axis_index("s")

        @pl.loop(0, SC_ZROWS)
        def _(r):
            for l0 in range(0, SC_SLAB, SC_LANES):
                zero_v[r, pl.ds(l0, SC_LANES)] = jnp.zeros((SC_LANES,), F32)

        for bb in range(batches_per_core):
            batch = core * batches_per_core + bb
            pair = batch * N_EXPERTS + sub
            pltpu.sync_copy(tok_hbm.at[pair], tok_v)

            @pl.loop(0, d // SC_SLAB)
            def _(slab):
                cols = pl.ds(pl.multiple_of(slab * SC_SLAB, SC_SLAB), SC_SLAB)

                @pl.loop(0, own_rows, step=SC_ZROWS)
                def _(r0):
                    pltpu.sync_copy(zero_v, acc_sh.at[pl.ds(sub * own_rows + r0, SC_ZROWS)])

                def load(j, buf):
                    return pltpu.make_async_copy(
                        ye_hbm.at[pl.ds(pair * cap + j * SC_IDX, SC_IDX), cols], rows_v.at[buf], sem.at[buf])

                load(0, 0).start()
                plsc.subcore_barrier()

                for j in range(0, n_chunks, 2):
                    load(j + 1, 1).start()
                    load(j, 0).wait()
                    pltpu.sync_copy(rows_v.at[0], acc_sh.at[tok_v.at[j]], add=True)
                    if j + 2 < n_chunks:
                        load(j + 2, 0).start()
                    load(j + 1, 1).wait()
                    pltpu.sync_copy(rows_v.at[1], acc_sh.at[tok_v.at[j + 1]], add=True)

                plsc.subcore_barrier()
                pltpu.sync_copy(acc_sh.at[pl.ds(sub * own_rows, own_rows)],
                                out_hbm.at[pl.ds(batch * seq + sub * own_rows, own_rows), cols])

    return combine(ye_flat, tok3)


TN = 1024


def _resln_kernel(x_ref, y_ref, g_ref, b_ref, o_ref, *, alpha):
    o_ref[...] = _standardize(alpha * x_ref[...] + y_ref[...]) * g_ref[...] + b_ref[...]


def _resln_call(x2d, y2d, g, bta, alpha):
    n, d = x2d.shape
    tok = pl.BlockSpec((TN, d), lambda i: (i, 0))
    vec = pl.BlockSpec((1, d), lambda i: (0, 0))
    return pl.pallas_call(
        functools.partial(_resln_kernel, alpha=alpha), name="residual_layernorm",
        grid=(n // TN,), in_specs=[tok, tok, vec, vec], out_specs=tok,
        out_shape=jax.ShapeDtypeStruct((n, d), F32),
        compiler_params=_cparams(1),
    )(x2d, y2d, g, bta)


def _expert_choice_ffn(x1p, aff_t, w1, w3, w2, layer):
    b, s, half = x1p.shape
    d = 2 * half
    cap = EC_FACTOR * s // N_EXPERTS
    slot = _select_call(aff_t, cap)
    xs, tok, gate = _sc_dispatch_call(x1p.reshape(b * s, half), slot.reshape(b * N_EXPERTS, s),
                                      aff_t.reshape(b * N_EXPERTS, s), s, cap)
    ye = _expert_call(xs.reshape(b, N_EXPERTS, cap, half), gate.reshape(b, N_EXPERTS, cap // SC_IDX, SC_IDX), w1, w3, w2,
                      layer)
    out = _sc_combine_call(ye.reshape(b * N_EXPERTS * cap, d),
                           tok, s)
    return out.reshape(b, s, d)


def _pack_pool(pool_w):
    g, gd, _ = pool_w.shape
    out = jnp.zeros((g * gd, g * gd), F32)
    for i in range(g):
        out = out.at[i * gd:(i + 1) * gd, i * gd:(i + 1) * gd].set(pool_w[i])
    return out.astype(BF16)


def _layer(layer, x, pending, alpha, bias_tiles, w_in, b_in, gm_ln_g, gm_ws, gm_bs, ml_conv, ml_fbias,
           ml_norm_g, pool_w, pool_scale, w_branch, w_out, ln1_g, ln1_b, w_router, w_e1, w_e3, w_e2):
    b, s, d = x.shape
    n_small = 2576
    w_cat, b_cat = _pack_inproj_weights(w_in, b_in)
    wscat, bsfull = _pack_gmlp(gm_ws, gm_bs)
    if pending is None:
        outs = _inproj_call(x.reshape(b * s, d), w_cat, b_cat, gm_ln_g[None], wscat, bsfull, b)
    else:
        *outs, x2 = _inproj_call(pending, w_cat, b_cat, gm_ln_g[None], wscat, bsfull, b, alpha)
        x = x2.reshape(b, s, d)
    ya, qkv1, qkv4, qkv16, cqk, cv, co, dx, gates_t = outs
    r3 = lambda t: t.reshape(b, s, t.shape[-1])
    o_list, l_list = [], []
    for qkv, bias in zip((qkv1, qkv4, qkv16), bias_tiles):
        o, lse = _attn_call(qkv, bias)
        o_list.append(o)
        l_list.append(lse)
    hf, hb = _mlstm_branch(r3(cqk), r3(cv), gates_t, ml_conv, ml_fbias)
    yd = _pool_call(r3(dx), _pack_pool(pool_w), pool_scale[None])
    x1, x1p, aff_t = _merge_call(
        x, r3(ya), o_list, l_list, hf, hb, r3(co), yd,
        w_in[:, n_small:].astype(BF16), b_in[None, n_small:], w_branch.astype(BF16), w_out.astype(BF16),
        ml_norm_g[:, None], ln1_g[None], ln1_b[None], jnp.transpose(w_router).astype(BF16), alpha)
    ffn = _expert_choice_ffn(x1p, aff_t, w_e1, w_e3, w_e2, layer)
    return x1.reshape(b * s, d), ffn.reshape(b * s, d)


def kernel(x, w_in, b_in, gm_ln_g, gm_ws, gm_bs, rel_bias, ml_conv, ml_fbias, ml_norm_g, pool_w,
           pool_scale, w_branch, w_out, ln1_g, ln1_b, w_router, w_e1, w_e3, w_e2, ln2_g, ln2_b):
    depth = w_in.shape[0]
    alpha = (2 * depth) ** 0.25
    bias_tiles = [_attn_bias_tile(rel_bias, window, dil) for window, dil in DIL_PATTERNS]
    b, s, d = x.shape
    pending = None
    for l in range(depth):
        x1, ffn = _layer(l, x, pending, alpha, bias_tiles, w_in[l], b_in[l], gm_ln_g[l], gm_ws[l], gm_bs[l],
                         ml_conv[l], ml_fbias[l], ml_norm_g[l], pool_w[l], pool_scale[l], w_branch[l],
                         w_out[l], ln1_g[l], ln1_b[l], w_router[l], w_e1, w_e3, w_e2)
        pending = (x1, ffn, ln2_g[l][None], ln2_b[l][None])
    return _resln_call(*pending, alpha).reshape(b, s, d)
```

```python
import functools
import math

import jax
import jax.numpy as jnp
import numpy as np
from jax import lax
from jax.experimental import pallas as pl
from jax.experimental.pallas import tpu as pltpu
from jax.experimental.pallas import tpu_sc as plsc

F32 = jnp.float32
BF16 = jnp.bfloat16

D_MODEL = 1024
MIX_W = 256
N_BRANCH = 4
GM_CHUNK = 128
GM_GROUPS = 4
ATT_HEADS = 4
ATT_HD = 64
DIL_PATTERNS = ((128, 1), (512, 4), (2048, 16))
ATT_BLOCK = 64
REL_BUCKETS = 32
REL_MAX_DIST = 1024
ML_HEADS = 4
ML_HD = 64
ML_CHUNK = 64
POOL_WINDOWS = (2, 4, 8, 16)
N_EXPERTS = 16
EXPERT_FF = 1024
EC_FACTOR = 2
LN_EPS = 1e-5
NEG_BIG = -1e30

V7X_VMEM_LIMIT = 56 * 1024 * 1024
LANES = 128
HALO = 8


def _cparams(n_grid, vmem=V7X_VMEM_LIMIT):
    return pltpu.CompilerParams(dimension_semantics=("arbitrary",) * n_grid,
                                vmem_limit_bytes=vmem)


def _pack_bf16_pair(lo, hi):
    lo_bits = lax.shift_right_logical(pltpu.bitcast(lo.astype(F32), jnp.int32), 16)
    return pltpu.bitcast(hi.astype(F32), jnp.int32) | lo_bits


def _unpack_bf16_pair(packed):
    lo = pltpu.bitcast(lax.shift_left(packed, 16), F32).astype(BF16)
    hi = pltpu.bitcast(packed & jnp.int32(-65536), F32).astype(BF16)
    return lo, hi


def _standardize(xf):
    mu = jnp.mean(xf, axis=-1, keepdims=True)
    var = jnp.mean(jnp.square(xf - mu), axis=-1, keepdims=True)
    return (xf - mu) * lax.rsqrt(var + LN_EPS)


TA = 512
A_COLS = 2560 + LANES


def _inproj_kernel(*refs, alpha):
    if alpha is None:
        x_ref, *refs = refs
        x = x_ref[...]
    else:
        x1_ref, y_ref, g2_ref, b2_ref, *refs = refs
        x = _standardize(alpha * x1_ref[...] + y_ref[...]) * g2_ref[...] + b2_ref[...]
        refs[-2][...] = x
        refs = refs[:-2] + refs[-1:]
    (w_ref, b_ref, lng_ref, wscat_ref, bsfull_ref,
     ya_ref, qkv1_ref, qkv4_ref, qkv16_ref, cqk_ref, cv_ref, co_ref, dx_ref, gt_ref, qkv_scr) = refs
    xb = x.astype(BF16)
    h = jnp.dot(xb, w_ref[...], preferred_element_type=F32) + b_ref[...]
    qkv1_ref[0, 0] = h[:, 512:1280].astype(BF16)
    for c in range(768 // LANES):
        qkv_scr[c] = h[:, 512 + c * LANES:512 + (c + 1) * LANES]
    for (_, dil), out_ref in zip(DIL_PATTERNS[1:], (qkv4_ref, qkv16_ref)):
        for r in range(dil):
            for c in range(768 // LANES):
                out_ref[0, r, :, c * LANES:(c + 1) * LANES] = (
                    qkv_scr[c, pl.ds(r, TA // dil, stride=dil), :].astype(BF16))
    cqk_ref[...] = h[:, 1280:1792]
    cv_ref[...] = h[:, 1792:2048].astype(BF16)
    co_ref[...] = h[:, 2048:2304]
    dx_ref[...] = h[:, 2304:2560]
    gates_t = jnp.transpose(h[:, 2560:2688])
    for j in range(TA // LANES):
        gt_ref[j] = gates_t[0:4 * ML_HEADS, j * LANES:(j + 1) * LANES]
    u = jax.nn.gelu(h[:, 0:256])
    v = jax.nn.gelu(h[:, 256:512])
    vn = _standardize(v) * lng_ref[...]
    lane_grp = lax.broadcasted_iota(jnp.int32, (GM_CHUNK, MIX_W), 1) // (MIX_W // GM_GROUPS)
    for c in range(TA // GM_CHUNK):
        vc = vn[c * GM_CHUNK:(c + 1) * GM_CHUNK]
        stacked = jnp.concatenate(
            [jnp.where(lane_grp == g, vc, 0.0).astype(BF16) for g in range(GM_GROUPS)], axis=0)
        mixed = jnp.dot(wscat_ref[...], stacked, preferred_element_type=F32) + bsfull_ref[...]
        ya_ref[c * GM_CHUNK:(c + 1) * GM_CHUNK, :] = (
            u[c * GM_CHUNK:(c + 1) * GM_CHUNK] * mixed).astype(BF16)


def _inproj_call(x_in, w_cat, b_cat, lng, wscat, bsfull, batch, alpha=None):
    fused = alpha is not None
    n = (x_in[0] if fused else x_in).shape[0]
    seq = n // batch
    tpb = seq // TA
    tok = lambda w: pl.BlockSpec((TA, w), lambda i: (i, 0))
    const = lambda s: pl.BlockSpec(s, lambda i: (0,) * len(s))
    regrouped = lambda dil: pl.BlockSpec((1, dil, TA // dil, 768), lambda i: (i // tpb, 0, i % tpb, 0))
    out_shape = (
        jax.ShapeDtypeStruct((n, 256), BF16),
    ) + tuple(jax.ShapeDtypeStruct((batch, dil, seq // dil, 768), BF16)
              for _, dil in DIL_PATTERNS) + (
        jax.ShapeDtypeStruct((n, 512), F32),
        jax.ShapeDtypeStruct((n, 256), BF16),
        jax.ShapeDtypeStruct((n, 256), F32),
        jax.ShapeDtypeStruct((n, 256), F32),
        jax.ShapeDtypeStruct((n // LANES, 4 * ML_HEADS, LANES), F32),
    )
    x_specs = [tok(D_MODEL), tok(D_MODEL), const((1, D_MODEL)), const((1, D_MODEL))] if fused else [tok(D_MODEL)]
    out_specs = ((tok(256),) + tuple(regrouped(dil) for _, dil in DIL_PATTERNS)
                 + (tok(512), tok(256), tok(256), tok(256),
                    pl.BlockSpec((TA // LANES, 4 * ML_HEADS, LANES), lambda i: (i, 0, 0))))
    if fused:
        out_specs += (tok(D_MODEL),)
        out_shape += (jax.ShapeDtypeStruct((n, D_MODEL), F32),)
    return pl.pallas_call(
        functools.partial(_inproj_kernel, alpha=alpha), name="inproj_gmlp",
        grid=(n // TA,),
        in_specs=x_specs + [const((D_MODEL, A_COLS)), const((1, A_COLS)), const((1, MIX_W)),
                            const((GM_CHUNK, GM_GROUPS * GM_CHUNK)), const((GM_CHUNK, MIX_W))],
        out_specs=out_specs,
        out_shape=out_shape,
        scratch_shapes=[pltpu.VMEM((768 // LANES, TA, LANES), F32)],
        compiler_params=_cparams(1),
    )(*(x_in if fused else (x_in,)), w_cat, b_cat, lng, wscat, bsfull)


def _pack_inproj_weights(w_in, b_in):
    pad = lambda a: jnp.pad(a, ((0, 0), (0, LANES - 4 * ML_HEADS)))
    w_cat = jnp.concatenate([w_in[:, 0:2304], w_in[:, 2320:2576], pad(w_in[:, 2304:2320])], axis=1)
    b2 = b_in[None, :]
    b_cat = jnp.concatenate([b2[:, 0:2304], b2[:, 2320:2576], pad(b2[:, 2304:2320])], axis=1)
    return w_cat.astype(BF16), b_cat


def _pack_gmlp(gm_ws, gm_bs):
    wscat = jnp.transpose(gm_ws, (1, 0, 2)).reshape(GM_CHUNK, GM_GROUPS * GM_CHUNK).astype(BF16)
    bsfull = jnp.repeat(jnp.transpose(gm_bs), MIX_W // GM_GROUPS, axis=1)
    return wscat, bsfull


def _halo_specs(t, width, n_tiles):
    r = t // HALO
    main = pl.BlockSpec((1, t, width), lambda b, i: (b, i, 0))
    prev = pl.BlockSpec((1, HALO, width), lambda b, i: (b, jnp.maximum(i * r - 1, 0), 0))
    nxt = pl.BlockSpec((1, HALO, width), lambda b, i: (b, jnp.minimum((i + 1) * r, n_tiles * r - 1), 0))
    return main, prev, nxt


def _fill_halo_scratch(buf, x_ref, p_ref, n_ref, t):
    i = pl.program_id(1)
    last = pl.num_programs(1) - 1
    buf[0:HALO, :] = jnp.where(i > 0, p_ref[0], 0.0)
    buf[HALO:HALO + t, :] = x_ref[0]
    buf[HALO + t:2 * HALO + t, :] = jnp.where(i < last, n_ref[0], 0.0)


TP = 2048


def _pool_kernel(x_ref, p_ref, n_ref, w_ref, sc_ref, o_ref, buf, lvl):
    _fill_halo_scratch(buf, x_ref, p_ref, n_ref, TP)
    seq = pl.num_programs(1) * TP
    pos = pl.program_id(1) * TP + lax.broadcasted_iota(jnp.int32, (TP, 1), 0)
    lane_grp = lax.broadcasted_iota(jnp.int32, (TP, MIX_W), 1) // (MIX_W // len(POOL_WINDOWS))
    x0 = buf[HALO:HALO + TP, :]
    sums = []
    src, rows = buf, TP + 2 * HALO
    for k, win in enumerate(POOL_WINDOWS):
        half = win // 2
        rows -= half
        cur = src[0:rows, :] + src[half:rows + half, :] if k else buf[0:rows, :] + buf[1:rows + 1, :]
        if k + 1 < len(POOL_WINDOWS):
            lvl[k, 0:rows, :] = cur
            sums.append(lvl[k, HALO - half:HALO - half + TP, :])
            src = lvl.at[k]
        else:
            sums.append(cur[0:TP])
    pooled = jnp.zeros((TP, MIX_W), F32)
    for gi, win in enumerate(POOL_WINDOWS):
        half = win // 2
        cnt = (jnp.minimum(pos + half, seq) - jnp.maximum(pos - half, 0)).astype(F32)
        pooled = jnp.where(lane_grp == gi, sums[gi] / cnt - x0, pooled)
    mixed = jnp.dot(pooled.astype(BF16), w_ref[...], preferred_element_type=F32)
    o_ref[0] = (mixed * sc_ref[...]).astype(BF16)


def _pool_call(dx, w_block, scale):
    b, s, _ = dx.shape
    nt = s // TP
    main, prev, nxt = _halo_specs(TP, MIX_W, nt)
    return pl.pallas_call(
        _pool_kernel, name="pool_mixer",
        grid=(b, nt),
        in_specs=[main, prev, nxt,
                  pl.BlockSpec((MIX_W, MIX_W), lambda b, i: (0, 0)),
                  pl.BlockSpec((1, MIX_W), lambda b, i: (0, 0))],
        out_specs=pl.BlockSpec((1, TP, MIX_W), lambda b, i: (b, i, 0)),
        out_shape=jax.ShapeDtypeStruct((b, s, MIX_W), BF16),
        scratch_shapes=[pltpu.VMEM((TP + 2 * HALO, MIX_W), F32),
                        pltpu.VMEM((len(POOL_WINDOWS) - 1, TP + 2 * HALO, MIX_W), F32)],
        compiler_params=_cparams(2),
    )(dx, dx, dx, w_block, scale)


TQ = 128
TQS = 2048
TKEYS = TQ + 2 * ATT_BLOCK


def _attn_kernel(q_ref, kp_ref, km_ref, kn_ref, vp_ref, vm_ref, vn_ref, bias_ref, o_ref, lse_ref):
    for sq in range(q_ref.shape[1]):
        _attn_sequence(sq, q_ref, kp_ref, km_ref, kn_ref, vp_ref, vm_ref, vn_ref, bias_ref, o_ref, lse_ref)


def _attn_sequence(sq, q_ref, kp_ref, km_ref, kn_ref, vp_ref, vm_ref, vn_ref, bias_ref, o_ref, lse_ref):
    i = pl.program_id(2)
    q = q_ref[0, sq] * ATT_HD ** -0.5
    k = jnp.concatenate([kp_ref[0, sq], km_ref[0, sq], kn_ref[0, sq]], axis=0)
    v = jnp.concatenate([vp_ref[0, sq], vm_ref[0, sq], vn_ref[0, sq]], axis=0)
    lane = lax.broadcasted_iota(jnp.int32, (TQ, LANES), 1)
    lane_half = lax.broadcasted_iota(jnp.int32, (1, LANES), 1) // ATT_HD
    keep = [jnp.where(lane_half == hh, 1.0, 0.0).astype(BF16) for hh in range(2)]
    n_sub = q_ref.shape[2] // TQ
    last_step = pl.num_programs(2) - 1
    for j in range(n_sub):
        if j == 0:
            variant = jnp.where(i == 0, 0, 1)
        elif j == n_sub - 1:
            variant = jnp.where(i == last_step, 2, 1)
        else:
            variant = 1
        qrows = slice(j * TQ, (j + 1) * TQ)
        krows = slice(j * TQ, j * TQ + TKEYS)
        lse_tile = jnp.zeros((TQ, LANES), F32)
        for pair in range(ATT_HEADS // 2):
            grp = slice(pair * LANES, (pair + 1) * LANES)
            q_pair, k_pair, v_pair = q[qrows, grp], k[krows, grp], v[krows, grp]
            o_pair = jnp.zeros((TQ, LANES), F32)
            for hh in range(2):
                h = 2 * pair + hh
                logits = lax.dot_general(q_pair * keep[hh], k_pair, (((1,), (1,)), ((), ())),
                                         preferred_element_type=F32) + bias_ref[variant, h]
                m = jnp.max(logits, axis=-1, keepdims=True)
                p = jnp.exp(logits - m)
                ssum = jnp.sum(p, axis=-1, keepdims=True)
                o = jnp.dot(p.astype(BF16), v_pair, preferred_element_type=F32) / ssum
                o_pair = jnp.where(lane_half == hh, o, o_pair)
                lse_tile = jnp.where(lane == h, m + jnp.log(ssum), lse_tile)
            o_ref[0, sq, qrows, grp] = o_pair
        lse_ref[0, sq, qrows, :] = lse_tile


def _attn_call(qkv, bias):
    b, dil, l, _ = qkv.shape
    tqs = min(l, TQS)
    nsq = min(dil, TQS // tqs)
    nt = l // tqs
    r64 = tqs // ATT_BLOCK
    main = lambda c: pl.BlockSpec((1, nsq, tqs, MIX_W), lambda b, r, i: (b, r, i, c))
    prev = lambda c: pl.BlockSpec((1, nsq, ATT_BLOCK, MIX_W),
                                  lambda b, r, i: (b, r, jnp.maximum(i * r64 - 1, 0), c))
    nxt = lambda c: pl.BlockSpec((1, nsq, ATT_BLOCK, MIX_W),
                                 lambda b, r, i: (b, r, jnp.minimum((i + 1) * r64, nt * r64 - 1), c))
    return pl.pallas_call(
        _attn_kernel, name="band_attention",
        grid=(b, dil // nsq, nt),
        in_specs=[main(0), prev(1), main(1), nxt(1), prev(2), main(2), nxt(2),
                  pl.BlockSpec((3, ATT_HEADS, TQ, TKEYS), lambda b, r, i: (0, 0, 0, 0))],
        out_specs=(pl.BlockSpec((1, nsq, tqs, MIX_W), lambda b, r, i: (b, r, i, 0)),
                   pl.BlockSpec((1, nsq, tqs, LANES), lambda b, r, i: (b, r, i, 0))),
        out_shape=(jax.ShapeDtypeStruct((b, dil, l, MIX_W), F32),
                   jax.ShapeDtypeStruct((b, dil, l, LANES), F32)),
        compiler_params=_cparams(3),
    )(qkv, qkv, qkv, qkv, qkv, qkv, qkv, bias)


def _t5_bucket_static(rel):
    half = REL_BUCKETS // 2
    max_exact = half // 2
    ret = np.where(rel > 0, half, 0)
    n = np.abs(rel)
    nf = np.maximum(n, 1).astype(np.float32)
    large = max_exact + (np.log(nf / np.float32(max_exact)) / np.float32(math.log(REL_MAX_DIST / max_exact))
                         * np.float32(half - max_exact)).astype(np.int32)
    large = np.minimum(large, half - 1)
    return ret + np.where(n < max_exact, n, large)


def _attn_bias_tile(rel_bias, window, dil):
    side = (window // 2) // dil
    rel = np.arange(TKEYS)[None, :] - ATT_BLOCK - np.arange(TQ)[:, None]
    n_rel = TKEYS + TQ - 1
    rel_values = np.arange(n_rel) - (ATT_BLOCK + TQ - 1)
    onehot = jax.nn.one_hot(jnp.asarray(_t5_bucket_static(dil * rel_values), jnp.int32), REL_BUCKETS, dtype=F32)
    table = jnp.einsum('nr,rh->hn', onehot, rel_bias, precision=lax.Precision.HIGHEST)
    periodic = jnp.tile(jnp.pad(table, ((0, 0), (0, 1))), (1, TQ))[:, :TQ * n_rel]
    bias = periodic.reshape(ATT_HEADS, TQ, n_rel)[:, :, TQ - 1:]
    key = np.arange(TKEYS)[None, :]
    inside = np.abs(rel) <= side
    masks = np.stack([inside & (key >= ATT_BLOCK), inside, inside & (key < ATT_BLOCK + TQ)])
    return jnp.where(jnp.asarray(masks)[:, None], bias[None], NEG_BIG)


TM = 2048
VT_ROWS = ML_HD + 16


def _mlprep_kernel(x_ref, p_ref, n_ref, v_ref, w_ref, qt_out, k_out, vt_out, buf):
    _fill_halo_scratch(buf, x_ref, p_ref, n_ref, TM)
    conv = (buf[HALO - 1:HALO - 1 + TM, :] * w_ref[0:1, :] + buf[HALO:HALO + TM, :] * w_ref[1:2, :]
            + buf[HALO + 1:HALO + 1 + TM, :] * w_ref[2:3, :])
    qk = jax.nn.silu(conv)
    qt = jnp.transpose(qk[:, :MIX_W])
    vt = jnp.transpose(v_ref[0].astype(F32))
    ones_rows = jnp.where(lax.broadcasted_iota(jnp.int32, (VT_ROWS - ML_HD, LANES), 0) == 0, 1.0, 0.0)
    for h in range(ML_HEADS):
        sl = slice(h * ML_HD, (h + 1) * ML_HD)
        k_out[0, h] = (qk[:, MIX_W + h * ML_HD:MIX_W + (h + 1) * ML_HD] * ML_HD ** -0.5).astype(BF16)
        for p in range(TM // LANES):
            pl_ = slice(p * LANES, (p + 1) * LANES)
            qt_out[0, h, p] = qt[sl, pl_].astype(BF16)
            vt_out[0, h, p] = jnp.concatenate([vt[sl, pl_], ones_rows], axis=0).astype(BF16)


def _mlprep_call(cqk, cv, conv_w):
    b, s, _ = cqk.shape
    nt = s // TM
    n_pairs = s // LANES
    ppt = TM // LANES
    main, prev, nxt = _halo_specs(TM, 2 * MIX_W, nt)
    return pl.pallas_call(
        _mlprep_kernel, name="mlstm_prep",
        grid=(b, nt),
        in_specs=[main, prev, nxt,
                  pl.BlockSpec((1, TM, MIX_W), lambda b, i: (b, i, 0)),
                  pl.BlockSpec((3, 2 * MIX_W), lambda b, i: (0, 0))],
        out_specs=(pl.BlockSpec((1, ML_HEADS, ppt, ML_HD, LANES), lambda b, i: (b, 0, i, 0, 0)),
                   pl.BlockSpec((1, ML_HEADS, TM, ML_HD), lambda b, i: (b, 0, i, 0)),
                   pl.BlockSpec((1, ML_HEADS, ppt, VT_ROWS, LANES), lambda b, i: (b, 0, i, 0, 0))),
        out_shape=(jax.ShapeDtypeStruct((b, ML_HEADS, n_pairs, ML_HD, LANES), BF16),
                   jax.ShapeDtypeStruct((b, ML_HEADS, s, ML_HD), BF16),
                   jax.ShapeDtypeStruct((b, ML_HEADS, n_pairs, VT_ROWS, LANES), BF16)),
        scratch_shapes=[pltpu.VMEM((TM + 2 * HALO, 2 * MIX_W), F32)],
        compiler_params=_cparams(2),
    )(cqk, cqk, cqk, cv, conv_w)


def _gate_kernel(g_ref, fb_ref, a_ref, m_ref, iw_ref, en_ref, ws_ref, dec_ref):
    x = g_ref[...]
    n_pairs = x.shape[0]
    n_ch = 2 * ML_HEADS
    lane = lax.broadcasted_iota(jnp.int32, (1, 1, LANES), 2)
    t_in = lane % ML_CHUNK
    second = lane >= ML_CHUNK
    fwd_row = lax.broadcasted_iota(jnp.int32, (1, n_ch, 1), 1) < ML_HEADS
    li = x[:, 0:n_ch, :]
    z = x[:, n_ch:2 * n_ch, :] + fb_ref[...]
    lf = jnp.minimum(z, 0.0) - jnp.log1p(jnp.exp(-jnp.abs(z)))

    def within_chunk(v, op, ident, prefix):
        s = 1
        while s < ML_CHUNK:
            nb = pltpu.roll(v, s if prefix else LANES - s, 2)
            ok = (t_in >= s) if prefix else (t_in < ML_CHUNK - s)
            v = op(v, jnp.where(ok, nb, ident))
            s *= 2
        return v

    pre = within_chunk(lf, jnp.add, 0.0, True)
    suf = within_chunk(lf, jnp.add, 0.0, False)
    g = pre + suf - lf
    b = jnp.where(fwd_row, pre, suf)
    a = li - b
    cm_pre = within_chunk(a, jnp.maximum, -jnp.inf, True)
    cm_suf = within_chunk(a, jnp.maximum, -jnp.inf, False)
    cm = jnp.where(fwd_row, cm_pre, cm_suf)
    amax = jnp.maximum(cm_pre, cm_suf)

    def shift_pairs(v, k, fill):
        pad = jnp.full((abs(k),) + v.shape[1:], fill, F32)
        return (jnp.concatenate([pad, v[:n_pairs - k]], axis=0) if k > 0
                else jnp.concatenate([v[-k:], pad], axis=0))

    def from_chunk(v, dist, fill, forward):
        if dist == 1:
            y = pltpu.roll(v, ML_CHUNK, 2)
            if forward:
                return jnp.where(second, y, shift_pairs(y, 1, fill))
            return jnp.where(second, shift_pairs(y, -1, fill), y)
        return shift_pairs(v, dist // 2 if forward else -(dist // 2), fill)

    def running_stabiliser(forward):
        big_g, big_a = g, amax + g
        dist = 1
        while dist < 2 * n_pairs:
            gp = from_chunk(big_g, dist, 0.0, forward)
            ap = from_chunk(big_a, dist, -jnp.inf, forward)
            big_g, big_a = gp + big_g, jnp.maximum(ap + big_g, big_a)
            dist *= 2
        ge = from_chunk(big_g, 1, 0.0, forward)
        ae = from_chunk(big_a, 1, -jnp.inf, forward)
        return jnp.maximum(ge, ae)

    m_chunk = jnp.where(fwd_row, running_stabiliser(True), running_stabiliser(False))
    m_t = jnp.maximum(cm, m_chunk)
    m_last = jnp.maximum(amax, m_chunk)
    a_ref[0] = a
    m_ref[0] = m_t
    iw_ref[0] = jnp.exp(m_chunk - m_t)
    en_ref[0] = jnp.exp(-(b + m_t))
    ws_ref[0] = jnp.exp(a - m_last)
    dec_ref[0] = jnp.exp(m_chunk - m_last)


def _gate_call(gates_t, fbias_col, batch):
    n_pairs = gates_t.shape[0] // batch
    n_ch = 2 * ML_HEADS
    out = pl.BlockSpec((1, n_pairs, n_ch, LANES), lambda i: (i, 0, 0, 0))
    return pl.pallas_call(
        _gate_kernel, name="mlstm_gates",
        grid=(batch,),
        in_specs=[pl.BlockSpec((n_pairs, 2 * n_ch, LANES), lambda i: (i, 0, 0)),
                  pl.BlockSpec((n_ch, 1), lambda i: (0, 0))],
        out_specs=(out,) * 6,
        out_shape=(jax.ShapeDtypeStruct((batch, n_pairs, n_ch, LANES), F32),) * 6,
        compiler_params=_cparams(1),
    )(gates_t, fbias_col)


TE = 2048


def _mlstm_kernel(*refs):
    fwd, bwd, (hf_ref, hb_ref, state) = refs[:9], refs[9:18], refs[18:]
    i = pl.program_id(1)

    @pl.when(i == 0)
    def _():
        state[...] = jnp.zeros(state.shape, F32)

    n_pairs = TE // LANES
    s_idx = lax.broadcasted_iota(jnp.int32, (LANES, LANES), 0)
    t_idx = lax.broadcasted_iota(jnp.int32, (LANES, LANES), 1)
    same_chunk = (s_idx >= ML_CHUNK) == (t_idx >= ML_CHUNK)
    upper_lanes = lax.broadcasted_iota(jnp.int32, (1, LANES), 1) >= ML_CHUNK

    def pair_body(p, carry):
        jobs = []
        for d, ((qt_r, k_r, vt_r, a_r, m_r, iw_r, en_r, ws_r, dec_r), out_r) in enumerate(
                ((fwd, hf_ref), (bwd, hb_ref))):
            pp = p if d == 0 else n_pairs - 1 - p
            srows = pl.ds(pl.multiple_of(pp * LANES, LANES), LANES)
            a_t = jnp.transpose(jnp.concatenate(
                [a_r[0, pp], jnp.zeros((LANES - 2 * ML_HEADS, LANES), F32)], axis=0))
            m_t, iw_t, en_t, ws_t, dec_t = [r[0, pp] for r in (m_r, iw_r, en_r, ws_r, dec_r)]
            dec_lo, dec_hi = dec_t[:, :ML_CHUNK], pltpu.roll(dec_t, ML_CHUNK, 1)[:, :ML_CHUNK]
            in_first = upper_lanes if d else ~upper_lanes
            tri = same_chunk & ((s_idx >= t_idx) if d else (s_idx <= t_idx))
            for h in range(ML_HEADS):
                ch = d * ML_HEADS + h
                row = lambda t: t[ch:ch + 1]
                jobs.append(dict(
                    ch=ch, tri=tri, in_first=in_first, k=k_r[0, h, srows, :], qt=qt_r[0, h, pp],
                    vt=vt_r[0, h, pp], a=a_t[:, ch:ch + 1], m=row(m_t), iw=row(iw_t), en=row(en_t), ws=row(ws_t),
                    dec_first=row(dec_hi if d else dec_lo), dec_second=row(dec_lo if d else dec_hi),
                    out=(out_r, pp, h)))
        for j in jobs:
            vt_f = j["vt"].astype(F32)
            j["c0"] = state[j["ch"]]
            j["st"] = jnp.dot(j["k"], j["qt"], preferred_element_type=F32)
            j["inter1"] = jnp.dot(j["c0"].astype(BF16), j["qt"], preferred_element_type=F32)
            j["upd1"] = jnp.dot((vt_f * jnp.where(j["in_first"], j["ws"], 0.0)).astype(BF16), j["k"],
                                preferred_element_type=F32)
            j["upd2"] = jnp.dot((vt_f * jnp.where(j["in_first"], 0.0, j["ws"])).astype(BF16), j["k"],
                                preferred_element_type=F32)
        for j in jobs:
            j["swt"] = j["st"] * jnp.exp(jnp.where(j["tri"], j["a"] - j["m"], NEG_BIG))
            j["intra"] = jnp.dot(j["vt"], j["swt"].astype(BF16), preferred_element_type=F32)
            j["c1"] = j["dec_first"] * j["c0"] + j["upd1"]
            j["inter2"] = jnp.dot(j["c1"].astype(BF16), j["qt"], preferred_element_type=F32)
        for j in jobs:
            inter = jnp.where(j["in_first"], j["inter1"], j["inter2"])
            den = jnp.sum(j["swt"], axis=0, keepdims=True) + j["iw"] * inter[ML_HD:ML_HD + 1]
            tot = j["intra"][:ML_HD] + j["iw"] * inter[:ML_HD]
            out_r, pp, h = j["out"]
            out_r[0, pp, h * ML_HD:(h + 1) * ML_HD, :] = tot / jnp.maximum(jnp.abs(den), j["en"])
            state[j["ch"]] = j["dec_second"] * j["c1"] + j["upd2"]
        return carry

    lax.fori_loop(0, n_pairs, pair_body, 0, unroll=4)


def _mlstm_call(qt, k, vt, a_t, m_t, iw_t, en_t, ws_t, dec_t):
    b, _, s, _ = k.shape
    nt = s // TE
    ppt = TE // LANES

    def specs(rev):
        ti = (lambda i: nt - 1 - i) if rev else (lambda i: i)
        tile = pl.BlockSpec((1, ppt, 2 * ML_HEADS, LANES), lambda b, i: (b, ti(i), 0, 0))
        return [
            pl.BlockSpec((1, ML_HEADS, ppt, ML_HD, LANES), lambda b, i: (b, 0, ti(i), 0, 0)),
            pl.BlockSpec((1, ML_HEADS, TE, ML_HD), lambda b, i: (b, 0, ti(i), 0)),
            pl.BlockSpec((1, ML_HEADS, ppt, VT_ROWS, LANES), lambda b, i: (b, 0, ti(i), 0, 0)),
            tile, tile, tile, tile, tile, tile]

    args = [qt, k, vt, a_t, m_t, iw_t, en_t, ws_t, dec_t]
    out_f = pl.BlockSpec((1, ppt, MIX_W, LANES), lambda b, i: (b, i, 0, 0))
    out_b = pl.BlockSpec((1, ppt, MIX_W, LANES), lambda b, i: (b, nt - 1 - i, 0, 0))
    return pl.pallas_call(
        _mlstm_kernel, name="mlstm_scan",
        grid=(b, nt),
        in_specs=specs(False) + specs(True),
        out_specs=(out_f, out_b),
        out_shape=(jax.ShapeDtypeStruct((b, s // LANES, MIX_W, LANES), F32),) * 2,
        scratch_shapes=[pltpu.VMEM((2 * ML_HEADS, VT_ROWS, ML_HD), F32)],
        compiler_params=_cparams(2),
    )(*args, *args)


def _mlstm_branch(cqk, cv, gates_t, conv_w, fbias):
    qt, k, vt = _mlprep_call(cqk, cv, conv_w)
    factors = _gate_call(gates_t, fbias.reshape(2 * ML_HEADS, 1), cqk.shape[0])
    return _mlstm_call(qt, k, vt, *factors)


TF = 512


def _merge_kernel(x_ref, ya_ref, o1_ref, o2_ref, o3_ref, l1_ref, l2_ref, l3_ref, hf_ref, hb_ref,
                  co_ref, yd_ref, wg_ref, bg_ref, wbr_ref, wout_ref, mng_ref, lng_ref, lnb_ref,
                  wr_ref, x1_ref, x1p_ref, aff_ref, o_scr, l_scr, *, alpha):
    x = x_ref[0]
    xb = x.astype(BF16)

    def natural_order(src_ref, scr):
        dil, width = src_ref.shape[1], src_ref.shape[3]
        if dil == 1:
            return src_ref[0, 0]
        for r in range(dil):
            for c in range(width // LANES):
                scr[c, pl.ds(r, TF // dil, stride=dil), :] = src_ref[0, r, :, c * LANES:(c + 1) * LANES]
        return jnp.concatenate([scr[c] for c in range(width // LANES)], axis=1)

    lane_head = lax.broadcasted_iota(jnp.int32, (TF, MIX_W), 1) // ML_HD
    l1, l2, l3 = [natural_order(r, l_scr.at[p]) for p, r in enumerate((l1_ref, l2_ref, l3_ref))]
    o1, o2, o3 = [natural_order(r, o_scr.at[p]) for p, r in enumerate((o1_ref, o2_ref, o3_ref))]
    lm = jnp.maximum(jnp.maximum(l1, l2), l3)
    e1, e2, e3 = jnp.exp(l1 - lm), jnp.exp(l2 - lm), jnp.exp(l3 - lm)
    inv = 1.0 / (e1 + e2 + e3)

    def per_head(w):
        out = jnp.zeros((TF, MIX_W), F32)
        for h in range(ATT_HEADS):
            out = jnp.where(lane_head == h, w[:, h:h + 1], out)
        return out

    y_b = per_head(e1 * inv) * o1 + per_head(e2 * inv) * o2 + per_head(e3 * inv) * o3
    hsum_t = jnp.concatenate([hf_ref[0, p] + hb_ref[0, p] for p in range(TF // LANES)], axis=1)
    per_head_rows = hsum_t.reshape(ML_HEADS, ML_HD, TF)
    mu = jnp.mean(per_head_rows, axis=1, keepdims=True)
    cen = per_head_rows - mu
    var = jnp.mean(cen * cen, axis=1, keepdims=True)
    hn_t = (cen * lax.rsqrt(var + LN_EPS)).reshape(MIX_W, TF)
    y_c_t = (jax.nn.sigmoid(jnp.transpose(co_ref[0])) * (hn_t * mng_ref[...])).astype(BF16)
    ys = (ya_ref[0], y_b.astype(BF16), None, yd_ref[0])
    merged = jnp.zeros((TF, D_MODEL), F32)
    for n in range(N_BRANCH):
        cols = slice(n * D_MODEL, (n + 1) * D_MODEL)
        gate = jax.nn.sigmoid(jnp.dot(xb, wg_ref[:, cols], preferred_element_type=F32) + bg_ref[:, cols])
        if ys[n] is None:
            proj = lax.dot_general(y_c_t, wbr_ref[n], (((0,), (0,)), ((), ())), preferred_element_type=F32)
        else:
            proj = jnp.dot(ys[n], wbr_ref[n], preferred_element_type=F32)
        merged = merged + gate * proj
    mix = jnp.dot(merged.astype(BF16), wout_ref[...], preferred_element_type=F32)
    x1 = _standardize(alpha * x + mix) * lng_ref[...] + lnb_ref[...]
    x1_ref[0] = x1
    x1b = x1.astype(BF16)
    x1p_ref[0] = _pack_bf16_pair(x1b[:, :D_MODEL // 2], x1b[:, D_MODEL // 2:])
    logits = lax.dot_general(wr_ref[...], x1b, (((1,), (1,)), ((), ())),
                             preferred_element_type=F32)
    ex = jnp.exp(logits - jnp.max(logits, axis=0, keepdims=True))
    aff_ref[0] = ex / jnp.sum(ex, axis=0, keepdims=True)


def _merge_call(x, ya, o_list, l_list, hf, hb, co, yd, wg, bg, wbr, wout, mng, lng, lnb, wr_t, alpha):
    b, s, _ = x.shape
    tok = lambda w: pl.BlockSpec((1, TF, w), lambda b, i: (b, i, 0))
    grouped = lambda dil, w: pl.BlockSpec((1, dil, TF // dil, w), lambda b, i: (b, 0, i, 0))
    chunked = pl.BlockSpec((1, TF // LANES, MIX_W, LANES), lambda b, i: (b, i, 0, 0))
    const = lambda shp: pl.BlockSpec(shp, lambda b, i: (0,) * len(shp))
    return pl.pallas_call(
        functools.partial(_merge_kernel, alpha=alpha), name="merge_ln_router",
        grid=(b, s // TF),
        in_specs=[tok(D_MODEL), tok(MIX_W)] + [grouped(dil, MIX_W) for _, dil in DIL_PATTERNS]
                 + [grouped(dil, LANES) for _, dil in DIL_PATTERNS]
                 + [chunked, chunked, tok(MIX_W), tok(MIX_W)]
                 + [const((D_MODEL, N_BRANCH * D_MODEL)), const((1, N_BRANCH * D_MODEL)),
                    const((N_BRANCH, MIX_W, D_MODEL)), const((D_MODEL, D_MODEL)), const((MIX_W, 1)),
                    const((1, D_MODEL)), const((1, D_MODEL)), const((N_EXPERTS, D_MODEL))],
        out_specs=(tok(D_MODEL), tok(D_MODEL // 2), pl.BlockSpec((1, N_EXPERTS, TF), lambda b, i: (b, 0, i))),
        out_shape=(jax.ShapeDtypeStruct((b, s, D_MODEL), F32),
                   jax.ShapeDtypeStruct((b, s, D_MODEL // 2), jnp.int32),
                   jax.ShapeDtypeStruct((b, N_EXPERTS, s), F32)),
        scratch_shapes=[pltpu.VMEM((len(DIL_PATTERNS), MIX_W // LANES, TF, LANES), F32),
                        pltpu.VMEM((len(DIL_PATTERNS), 1, TF, LANES), F32)],
        compiler_params=_cparams(2),
    )(x, ya, *o_list, *l_list, hf, hb, co, yd, wg, bg, wbr, wout, mng, lng, lnb, wr_t)


TT = 256


def _select_kernel(aff_ref, slot_ref, *, cap):
    s = aff_ref.shape[2]
    bits = pltpu.bitcast(aff_ref[0], jnp.int32)

    def bit_step(i, thr):
        cand = thr | jnp.left_shift(jnp.int32(1), 30 - i)
        cnt = jnp.sum((bits >= cand).astype(jnp.int32), axis=1, keepdims=True)
        return jnp.where(cnt >= cap, cand, thr)

    thr = lax.fori_loop(0, 31, bit_step, jnp.zeros((N_EXPERTS, 1), jnp.int32))
    gt = bits > thr
    eq = bits == thr
    need = (cap - jnp.sum(gt.astype(jnp.int32), axis=1, keepdims=True)).astype(F32)
    upper = (lax.broadcasted_iota(jnp.int32, (TT, TT), 0)
             <= lax.broadcasted_iota(jnp.int32, (TT, TT), 1)).astype(BF16)
    eq_before = jnp.zeros((N_EXPERTS, 1), F32)
    sel_before = jnp.zeros((N_EXPERTS, 1), F32)
    for j in range(s // TT):
        cols = slice(j * TT, (j + 1) * TT)
        eq_j = eq[:, cols]
        eq_incl = eq_before + jnp.dot(eq_j.astype(BF16), upper, preferred_element_type=F32)
        sel_j = gt[:, cols] | (eq_j & (eq_incl <= need))
        sel_f = sel_j.astype(F32)
        sel_incl = sel_before + jnp.dot(sel_f.astype(BF16), upper, preferred_element_type=F32)
        slot_ref[0, :, cols] = jnp.where(sel_j, sel_incl - 1.0, -1.0).astype(jnp.int32)
        eq_before = eq_incl[:, TT - 1:TT]
        sel_before = sel_incl[:, TT - 1:TT]


def _select_call(aff_t, cap):
    b, e, s = aff_t.shape
    return pl.pallas_call(
        functools.partial(_select_kernel, cap=cap), name="expert_choice_select",
        grid=(b,),
        in_specs=[pl.BlockSpec((1, e, s), lambda i: (i, 0, 0))],
        out_specs=pl.BlockSpec((1, e, s), lambda i: (i, 0, 0)),
        out_shape=jax.ShapeDtypeStruct((b, e, s), jnp.int32),
        compiler_params=_cparams(1),
    )(aff_t)


SC_LANES = 16
SC_ROWS = 64
SC_IDX = 128
SC_SLAB = 128
SC_ZROWS = 64
CF = 1024


def _sc_dispatch_call(x_flat, slot2, aff2, seq, cap):
    n_pairs = slot2.shape[0]
    d = x_flat.shape[1]
    info = plsc.get_sparse_core_info()
    n_workers = info.num_cores * info.num_subcores
    assert n_pairs % n_workers == 0 and seq % SC_LANES == 0 and cap % (2 * SC_ROWS) == 0
    pairs_per_worker = n_pairs // n_workers
    mesh = plsc.VectorSubcoreMesh(core_axis_name="c", subcore_axis_name="s")

    @functools.partial(
        pl.kernel, mesh=mesh, name="expert_dispatch_sc",
        compiler_params=pltpu.CompilerParams(needs_layout_passes=False),
        out_type=(jax.ShapeDtypeStruct((n_pairs * cap, d), x_flat.dtype),
                  jax.ShapeDtypeStruct((n_pairs, cap // SC_IDX, SC_IDX), jnp.int32),
                  jax.ShapeDtypeStruct((n_pairs, cap // SC_IDX, SC_IDX), F32)),
        scratch_types=[pltpu.VMEM((seq,), jnp.int32), pltpu.VMEM((seq,), F32),
                       pltpu.VMEM((cap,), jnp.int32), pltpu.VMEM((cap // SC_IDX, SC_IDX), jnp.int32),
                       pltpu.VMEM((cap // SC_IDX, SC_IDX), F32),
                       pltpu.VMEM((2, SC_ROWS, d), x_flat.dtype), pltpu.SemaphoreType.DMA((2,))])
    def dispatch(x_hbm, slot_hbm, aff_hbm, xs_hbm, tok_hbm, gate_hbm,
                 slot_v, aff_v, idx_v, tok_v, gate_v, rows_v, sem):
        worker = lax.axis_index("s") * info.num_cores + lax.axis_index("c")
        lane = lax.iota(jnp.int32, SC_LANES)

        def gather(c0, buf):
            return pltpu.make_async_copy(x_hbm.at[idx_v.at[pl.ds(c0, SC_ROWS)]], rows_v.at[buf], sem.at[buf])
        for k in range(pairs_per_worker):
            pair = worker * pairs_per_worker + k
            row0 = (pair // N_EXPERTS) * seq
            pltpu.sync_copy(slot_hbm.at[pair], slot_v)
            pltpu.sync_copy(aff_hbm.at[pair], aff_v)

            @plsc.parallel_loop(0, seq, step=SC_LANES, unroll=4)
            def _(t0):
                sv = slot_v[pl.ds(t0, SC_LANES)]
                picked = sv >= 0
                hi, lo = lax.shift_right_logical(sv, 7), sv & (SC_IDX - 1)
                plsc.store_scatter(tok_v, [hi, lo], t0 + lane, mask=picked)
                plsc.store_scatter(idx_v, [sv], row0 + t0 + lane, mask=picked)
                plsc.store_scatter(gate_v, [hi, lo], aff_v[pl.ds(t0, SC_LANES)], mask=picked)

            pltpu.sync_copy(tok_v, tok_hbm.at[pair])
            pltpu.sync_copy(gate_v, gate_hbm.at[pair])

            gather(0, 0).start()

            @pl.loop(0, cap, step=2 * SC_ROWS)
            def _(c0):
                gather(c0 + SC_ROWS, 1).start()
                gather(c0, 0).wait()
                pltpu.sync_copy(rows_v.at[0], xs_hbm.at[pl.ds(pair * cap + c0, SC_ROWS)])

                @pl.when(c0 + 2 * SC_ROWS < cap)
                def _():
                    gather(c0 + 2 * SC_ROWS, 0).start()

                gather(c0 + SC_ROWS, 1).wait()
                pltpu.sync_copy(rows_v.at[1], xs_hbm.at[pl.ds(pair * cap + c0 + SC_ROWS, SC_ROWS)])

    return dispatch(x_flat, slot2, aff2)


def _expert_kernel(xs_ref, g_ref, w1_ref, w3_ref, w2_ref, ye_ref, w1_bf, w3_bf, w2_bf):
    @pl.when((pl.program_id(1) == 0) & (pl.program_id(2) == 0))
    def _():
        w1_bf[...] = w1_ref[0, 0].astype(BF16)
        w3_bf[...] = w3_ref[0, 0].astype(BF16)
        w2_bf[...] = w2_ref[0, 0].astype(BF16)

    xs = jnp.concatenate(_unpack_bf16_pair(xs_ref[0, 0]), axis=1)
    hid = (jax.nn.silu(jnp.dot(xs, w1_bf[...], preferred_element_type=F32))
           * jnp.dot(xs, w3_bf[...], preferred_element_type=F32))
    g_rows = g_ref[0, 0]
    n_rows = g_rows.shape[0]
    g_t = jnp.transpose(jnp.concatenate([g_rows, jnp.zeros((LANES - n_rows, LANES), F32)], axis=0))
    g_col = jnp.concatenate([g_t[:, r:r + 1] for r in range(n_rows)], axis=0)
    ye_ref[0, 0] = jnp.dot(hid.astype(BF16), w2_bf[...], preferred_element_type=F32) * g_col


def _expert_call(xs4, gate4, w1, w3, w2, layer):
    b, e, cap, half = xs4.shape
    d, ff = w1.shape[2], w1.shape[3]
    assert d == 2 * half
    rows = lambda w: pl.BlockSpec((1, 1, CF, w), lambda e, b, j: (b, e, j, 0))
    wspec = lambda r, c: pl.BlockSpec((1, 1, r, c), lambda e, b, j: (layer, e, 0, 0))
    return pl.pallas_call(
        _expert_kernel, name="expert_ffn",
        grid=(e, b, cap // CF),
        in_specs=[rows(half), pl.BlockSpec((1, 1, CF // LANES, LANES), lambda e, b, j: (b, e, j, 0)),
                  wspec(d, ff), wspec(d, ff), wspec(ff, d)],
        out_specs=rows(d),
        out_shape=jax.ShapeDtypeStruct((b, e, cap, d), F32),
        scratch_shapes=[pltpu.VMEM((d, ff), BF16), pltpu.VMEM((d, ff), BF16), pltpu.VMEM((ff, d), BF16)],
        compiler_params=_cparams(3),
    )(xs4, gate4, w1, w3, w2)


def _sc_combine_call(ye_flat, tok3, seq):
    n_pairs, n_chunks, _ = tok3.shape
    cap = n_chunks * SC_IDX
    d = ye_flat.shape[1]
    nb = n_pairs // N_EXPERTS
    info = plsc.get_sparse_core_info()
    assert info.num_subcores == N_EXPERTS and nb % info.num_cores == 0 and n_chunks % 2 == 0
    assert seq % (info.num_subcores * SC_ZROWS) == 0 and d % SC_SLAB == 0
    batches_per_core = nb // info.num_cores
    own_rows = seq // info.num_subcores
    mesh = plsc.VectorSubcoreMesh(core_axis_name="c", subcore_axis_name="s")

    @functools.partial(
        pl.kernel, mesh=mesh, name="expert_combine_sc",
        compiler_params=pltpu.CompilerParams(needs_layout_passes=False),
        out_type=jax.ShapeDtypeStruct((nb * seq, d), F32),
        scratch_types=[pltpu.VMEM_SHARED((seq, SC_SLAB), F32),
                       pltpu.VMEM((n_chunks, SC_IDX), jnp.int32),
                       pltpu.VMEM((2, SC_IDX, SC_SLAB), F32),
                       pltpu.VMEM((SC_ZROWS, SC_SLAB), F32),
                       pltpu.SemaphoreType.DMA((2,))])
    def combine(ye_hbm, tok_hbm, out_hbm, acc_sh, tok_v, rows_v, zero_v, sem):
        core = lax.axis_index("c")
        sub = lax.axis_index("s")

        @pl.loop(0, SC_ZROWS)
        def _(r):
            for l0 in range(0, SC_SLAB, SC_LANES):
                zero_v[r, pl.ds(l0, SC_LANES)] = jnp.zeros((SC_LANES,), F32)

        for bb in range(batches_per_core):
            batch = core * batches_per_core + bb
            pair = batch * N_EXPERTS + sub
            pltpu.sync_copy(tok_hbm.at[pair], tok_v)

            @pl.loop(0, d // SC_SLAB)
            def _(slab):
                cols = pl.ds(pl.multiple_of(slab * SC_SLAB, SC_SLAB), SC_SLAB)

                @pl.loop(0, own_rows, step=SC_ZROWS)
                def _(r0):
                    pltpu.sync_copy(zero_v, acc_sh.at[pl.ds(sub * own_rows + r0, SC_ZROWS)])

                def load(j, buf):
                    return pltpu.make_async_copy(
                        ye_hbm.at[pl.ds(pair * cap + j * SC_IDX, SC_IDX), cols], rows_v.at[buf], sem.at[buf])

                load(0, 0).start()
                plsc.subcore_barrier()

                for j in range(0, n_chunks, 2):
                    load(j + 1, 1).start()
                    load(j, 0).wait()
                    pltpu.sync_copy(rows_v.at[0], acc_sh.at[tok_v.at[j]], add=True)
                    if j + 2 < n_chunks:
                        load(j + 2, 0).start()
                    load(j + 1, 1).wait()
                    pltpu.sync_copy(rows_v.at[1], acc_sh.at[tok_v.at[j + 1]], add=True)

                plsc.subcore_barrier()
                pltpu.sync_copy(acc_sh.at[pl.ds(sub * own_rows, own_rows)],
                                out_hbm.at[pl.ds(batch * seq + sub * own_rows, own_rows), cols])

    return combine(ye_flat, tok3)


TN = 1024


def _resln_kernel(x_ref, y_ref, g_ref, b_ref, o_ref, *, alpha):
    o_ref[...] = _standardize(alpha * x_ref[...] + y_ref[...]) * g_ref[...] + b_ref[...]


def _resln_call(x2d, y2d, g, bta, alpha):
    n, d = x2d.shape
    tok = pl.BlockSpec((TN, d), lambda i: (i, 0))
    vec = pl.BlockSpec((1, d), lambda i: (0, 0))
    return pl.pallas_call(
        functools.partial(_resln_kernel, alpha=alpha), name="residual_layernorm",
        grid=(n // TN,), in_specs=[tok, tok, vec, vec], out_specs=tok,
        out_shape=jax.ShapeDtypeStruct((n, d), F32),
        compiler_params=_cparams(1),
    )(x2d, y2d, g, bta)


def _expert_choice_ffn(x1p, aff_t, w1, w3, w2, layer):
    b, s, half = x1p.shape
    d = 2 * half
    cap = EC_FACTOR * s // N_EXPERTS
    slot = _select_call(aff_t, cap)
    xs, tok, gate = _sc_dispatch_call(x1p.reshape(b * s, half), slot.reshape(b * N_EXPERTS, s),
                                      aff_t.reshape(b * N_EXPERTS, s), s, cap)
    ye = _expert_call(xs.reshape(b, N_EXPERTS, cap, half), gate.reshape(b, N_EXPERTS, cap // SC_IDX, SC_IDX), w1, w3, w2,
                      layer)
    out = _sc_combine_call(ye.reshape(b * N_EXPERTS * cap, d),
                           tok, s)
    return out.reshape(b, s, d)


def _pack_pool(pool_w):
    g, gd, _ = pool_w.shape
    out = jnp.zeros((g * gd, g * gd), F32)
    for i in range(g):
        out = out.at[i * gd:(i + 1) * gd, i * gd:(i + 1) * gd].set(pool_w[i])
    return out.astype(BF16)


def _layer(layer, x, pending, alpha, bias_tiles, w_in, b_in, gm_ln_g, gm_ws, gm_bs, ml_conv, ml_fbias,
           ml_norm_g, pool_w, pool_scale, w_branch, w_out, ln1_g, ln1_b, w_router, w_e1, w_e3, w_e2):
    b, s, d = x.shape
    n_small = 2576
    w_cat, b_cat = _pack_inproj_weights(w_in, b_in)
    wscat, bsfull = _pack_gmlp(gm_ws, gm_bs)
    if pending is None:
        outs = _inproj_call(x.reshape(b * s, d), w_cat, b_cat, gm_ln_g[None], wscat, bsfull, b)
    else:
        *outs, x2 = _inproj_call(pending, w_cat, b_cat, gm_ln_g[None], wscat, bsfull, b, alpha)
        x = x2.reshape(b, s, d)
    ya, qkv1, qkv4, qkv16, cqk, cv, co, dx, gates_t = outs
    r3 = lambda t: t.reshape(b, s, t.shape[-1])
    o_list, l_list = [], []
    for qkv, bias in zip((qkv1, qkv4, qkv16), bias_tiles):
        o, lse = _attn_call(qkv, bias)
        o_list.append(o)
        l_list.append(lse)
    hf, hb = _mlstm_branch(r3(cqk), r3(cv), gates_t, ml_conv, ml_fbias)
    yd = _pool_call(r3(dx), _pack_pool(pool_w), pool_scale[None])
    x1, x1p, aff_t = _merge_call(
        x, r3(ya), o_list, l_list, hf, hb, r3(co), yd,
        w_in[:, n_small:].astype(BF16), b_in[None, n_small:], w_branch.astype(BF16), w_out.astype(BF16),
        ml_norm_g[:, None], ln1_g[None], ln1_b[None], jnp.transpose(w_router).astype(BF16), alpha)
    ffn = _expert_choice_ffn(x1p, aff_t, w_e1, w_e3, w_e2, layer)
    return x1.reshape(b * s, d), ffn.reshape(b * s, d)


def kernel(x, w_in, b_in, gm_ln_g, gm_ws, gm_bs, rel_bias, ml_conv, ml_fbias, ml_norm_g, pool_w,
           pool_scale, w_branch, w_out, ln1_g, ln1_b, w_router, w_e1, w_e3, w_e2, ln2_g, ln2_b):
    depth = w_in.shape[0]
    alpha = (2 * depth) ** 0.25
    bias_tiles = [_attn_bias_tile(rel_bias, window, dil) for window, dil in DIL_PATTERNS]
    b, s, d = x.shape
    pending = None
    for l in range(depth):
        x1, ffn = _layer(l, x, pending, alpha, bias_tiles, w_in[l], b_in[l], gm_ln_g[l], gm_ws[l], gm_bs[l],
                         ml_conv[l], ml_fbias[l], ml_norm_g[l], pool_w[l], pool_scale[l], w_branch[l],
                         w_out[l], ln1_g[l], ln1_b[l], w_router[l], w_e1, w_e3, w_e2)
        pending = (x1, ffn, ln2_g[l][None], ln2_b[l][None])
    return _resln_call(*pending, alpha).reshape(b, s, d)
```
